```python
import math
import jax, jax.numpy as jnp
from jax import lax
import numpy as np

D_MODEL = 2048
BATCH = 8
SEQ = 4096
DEPTH = 1

HEAD_DIM = 128
N_HEAD_SLOTS = 8
DILATED_GROUPS = ((128, 1), (512, 4), (2048, 16))
N_GROUPS = len(DILATED_GROUPS)
ATTN_WIDTH = N_HEAD_SLOTS * HEAD_DIM
POOL_WINDOWS = (2, 4, 8, 16)
N_POOL_GROUPS = len(POOL_WINDOWS)
POOL_WIDTH = D_MODEL // 2
POOL_GROUP = POOL_WIDTH // N_POOL_GROUPS
N_BRANCHES = 2
SPLIT_SIZES = (N_GROUPS * ATTN_WIDTH,
               N_GROUPS * ATTN_WIDTH,
               N_GROUPS * ATTN_WIDTH,
               ATTN_WIDTH,
               POOL_WIDTH,
               POOL_WIDTH,
               N_BRANCHES * D_MODEL)
IN_WIDTH = sum(SPLIT_SIZES)
DEEPNORM_ALPHA = (2.0 * DEPTH) ** 0.25
DEEPNORM_BETA = (8.0 * DEPTH) ** -0.25
LN_EPS = 1e-5
NEG_INF = -1e30

kernel_name = "hybrid_dilated_attn_pool_gated_deepnorm"


def layer_norm(x, gamma, beta):
    xf = x.astype(jnp.float32)
    mu = jnp.mean(xf, axis=-1, keepdims=True)
    var = jnp.mean(jnp.square(xf - mu), axis=-1, keepdims=True)
    y = (xf - mu) * lax.rsqrt(var + LN_EPS) * gamma.astype(jnp.float32) + beta.astype(jnp.float32)
    return y.astype(x.dtype)


def dilated_window_attention(q, k, v, window, dilation):
    B, S, H, hd = q.shape
    steps = window // dilation
    span = steps * dilation
    Sp = -(-S // span) * span
    L = Sp // dilation
    nb = L // steps

    def to_blocks(t):
        t = jnp.pad(t, ((0, 0), (0, Sp - S), (0, 0), (0, 0)))
        t = t.reshape(B, L, dilation, H, hd).transpose(0, 2, 1, 3, 4)
        return t.reshape(B, dilation, nb, steps, H, hd)

    def with_prev(t):
        prev = jnp.pad(t[:, :, :-1], ((0, 0), (0, 0), (1, 0), (0, 0), (0, 0), (0, 0)))
        return jnp.concatenate([prev, t], axis=3)

    qb = to_blocks(q)
    kb = with_prev(to_blocks(k))
    vb = with_prev(to_blocks(v))
    scores = jnp.einsum('brnqhd,brnkhd->brnhqk', qb, kb,
                        preferred_element_type=jnp.float32) * (hd ** -0.5)
    qi = np.arange(steps)[:, None]
    kj = np.arange(2 * steps)[None, :]
    dist = steps + qi - kj
    blk = np.arange(nb)[:, None, None]
    mask = (dist >= 0) & (dist <= steps) & ((blk > 0) | (kj >= steps))[...]
    scores = jnp.where(jnp.asarray(mask)[None, None, :, None], scores, NEG_INF)
    m = jnp.max(scores, axis=-1, keepdims=True)
    e = jnp.exp(scores - m)
    den = jnp.sum(e, axis=-1)
    o = jnp.einsum('brnhqk,brnkhd->brnqhd', e.astype(v.dtype), vb,
                   preferred_element_type=jnp.float32)
    o = o / jnp.transpose(den, (0, 1, 2, 4, 3))[..., None]
    lse = m[..., 0] + jnp.log(den)
    o = o.reshape(B, dilation, L, H, hd).transpose(0, 2, 1, 3, 4).reshape(B, Sp, H, hd)[:, :S]
    lse = jnp.transpose(lse, (0, 1, 2, 4, 3)).reshape(B, dilation, L, H)
    lse = lse.transpose(0, 2, 1, 3).reshape(B, Sp, H)[:, :S]
    return o, lse


def causal_pool_mixer(u, w_pool, pool_scale):
    B, S, _ = u.shape
    uf = u.astype(jnp.float32)
    c = jnp.concatenate([jnp.zeros((B, 1, POOL_WIDTH), jnp.float32), jnp.cumsum(uf, axis=1)], axis=1)
    hi = np.arange(1, S + 1)
    outs = []
    for g, w in enumerate(POOL_WINDOWS):
        lo = np.maximum(hi - w, 0)
        cnt = np.minimum(hi, w).astype(np.float32)
        sl = slice(g * POOL_GROUP, (g + 1) * POOL_GROUP)
        window_sum = jnp.take(c[..., sl], hi, axis=1) - jnp.take(c[..., sl], lo, axis=1)
        outs.append(window_sum / jnp.asarray(cnt)[None, :, None] - uf[..., sl])
    p = jnp.stack(outs, axis=2).astype(u.dtype)
    y = jnp.einsum('bsgc,gcd->bsgd', p, w_pool).reshape(B, S, POOL_WIDTH)
    return y * pool_scale


def hybrid_layer(x, w_in, b_gate, w_pool, pool_scale, w_proj_attn, w_proj_pool, w_out, ln_gamma, ln_beta):
    B, S, _ = x.shape
    h = jnp.einsum('bsd,de->bse', x, w_in)
    idx = list(np.cumsum(SPLIT_SIZES)[:-1])
    q, k, v, z_attn, u_pool, z_pool, g_pre = jnp.split(h, idx, axis=-1)
    q = q.reshape(B, S, N_GROUPS, N_HEAD_SLOTS, HEAD_DIM)
    k = k.reshape(B, S, N_GROUPS, N_HEAD_SLOTS, HEAD_DIM)
    v = v.reshape(B, S, N_GROUPS, N_HEAD_SLOTS, HEAD_DIM)
    outs, lses = [], []
    for g, (window, dilation) in enumerate(DILATED_GROUPS):
        o_g, lse_g = dilated_window_attention(q[:, :, g], k[:, :, g], v[:, :, g], window, dilation)
        outs.append(o_g)
        lses.append(lse_g)
    wts = jax.nn.softmax(jnp.stack(lses, axis=0), axis=0)
    o = jnp.sum(wts[..., None] * jnp.stack(outs, axis=0), axis=0).reshape(B, S, ATTN_WIDTH)
    y_attn = o.astype(x.dtype) * jax.nn.silu(z_attn)
    y_pool = causal_pool_mixer(u_pool, w_pool, pool_scale) * jax.nn.silu(z_pool)
    gates = jax.nn.sigmoid((g_pre + b_gate).astype(jnp.float32)).astype(x.dtype)
    g_attn, g_pool = jnp.split(gates, 2, axis=-1)
    merged = g_attn * jnp.einsum('bsc,cd->bsd', y_attn, w_proj_attn) \
        + g_pool * jnp.einsum('bsc,cd->bsd', y_pool, w_proj_pool)
    out = jnp.einsum('bsd,de->bse', merged, w_out)
    return layer_norm(DEEPNORM_ALPHA * x + out, ln_gamma, ln_beta)


def _fwd_setup_inputs(seed: int = 0) -> dict:
    key = jax.random.key(seed)
    ks = jax.random.split(key, 11)
    f32 = jnp.float32
    x = jax.random.normal(ks[0], (BATCH, SEQ, D_MODEL), f32)
    w_in = jax.random.normal(ks[1], (DEPTH, D_MODEL, IN_WIDTH), f32) * D_MODEL ** -0.5
    b_gate = jax.random.normal(ks[2], (DEPTH, N_BRANCHES * D_MODEL), f32) * 0.02
    w_pool = jax.random.normal(ks[3], (DEPTH, N_POOL_GROUPS, POOL_GROUP, POOL_GROUP), f32) * POOL_GROUP ** -0.5
    pool_scale = 1.0 + 0.02 * jax.random.normal(ks[4], (DEPTH, POOL_WIDTH), f32)
    w_proj_attn = jax.random.normal(ks[5], (DEPTH, ATTN_WIDTH, D_MODEL), f32) * ATTN_WIDTH ** -0.5 * DEEPNORM_BETA
    w_proj_pool = jax.random.normal(ks[6], (DEPTH, POOL_WIDTH, D_MODEL), f32) * POOL_WIDTH ** -0.5 * DEEPNORM_BETA
    w_out = jax.random.normal(ks[7], (DEPTH, D_MODEL, D_MODEL), f32) * D_MODEL ** -0.5 * DEEPNORM_BETA
    ln_gamma = 1.0 + 0.02 * jax.random.normal(ks[8], (DEPTH, D_MODEL), f32)
    ln_beta = 0.02 * jax.random.normal(ks[9], (DEPTH, D_MODEL), f32)
    return {"x": x, "w_in": w_in, "b_gate": b_gate, "w_pool": w_pool, "pool_scale": pool_scale,
            "w_proj_attn": w_proj_attn, "w_proj_pool": w_proj_pool, "w_out": w_out,
            "ln_gamma": ln_gamma, "ln_beta": ln_beta}


def _fwd_reference(x, w_in, b_gate, w_pool, pool_scale, w_proj_attn, w_proj_pool, w_out, ln_gamma, ln_beta):
    for layer in range(DEPTH):
        x = hybrid_layer(x, w_in[layer], b_gate[layer], w_pool[layer], pool_scale[layer],
                         w_proj_attn[layer], w_proj_pool[layer], w_out[layer],
                         ln_gamma[layer], ln_beta[layer])
    return x


import jax as _jax
import jax.numpy as _jnp

TWIN_FORMAT = 'train_step'
FWD_PARAMS = ['x', 'w_in', 'b_gate', 'w_pool', 'pool_scale', 'w_proj_attn', 'w_proj_pool', 'w_out', 'ln_gamma', 'ln_beta']
TWIN_WEIGHTS = ['w_in', 'b_gate', 'w_pool', 'pool_scale', 'w_proj_attn', 'w_proj_pool', 'w_out', 'ln_gamma', 'ln_beta']
TWIN_DIFF_INPUT = 'x'
TWIN_INPUTS = ['x', 'w_in', 'b_gate', 'w_pool', 'pool_scale', 'w_proj_attn', 'w_proj_pool', 'w_out', 'ln_gamma', 'ln_beta', 'loss_target', 'm_w_in', 'm_b_gate', 'm_w_pool', 'm_pool_scale', 'm_w_proj_attn', 'm_w_proj_pool', 'm_w_out', 'm_ln_gamma', 'm_ln_beta', 'v_w_in', 'v_b_gate', 'v_w_pool', 'v_pool_scale', 'v_w_proj_attn', 'v_w_proj_pool', 'v_w_out', 'v_ln_gamma', 'v_ln_beta']
TWIN_OUTPUTS = ['loss', 'grad_x', 'grad_w_in', 'grad_b_gate', 'grad_w_pool', 'grad_pool_scale', 'grad_w_proj_attn', 'grad_w_proj_pool', 'grad_w_out', 'grad_ln_gamma', 'grad_ln_beta', 'delta_w_in', 'delta_b_gate', 'delta_w_pool', 'delta_pool_scale', 'delta_w_proj_attn', 'delta_w_proj_pool', 'delta_w_out', 'delta_ln_gamma', 'delta_ln_beta', 'new_m_w_in', 'new_m_b_gate', 'new_m_w_pool', 'new_m_pool_scale', 'new_m_w_proj_attn', 'new_m_w_proj_pool', 'new_m_w_out', 'new_m_ln_gamma', 'new_m_ln_beta', 'new_v_w_in', 'new_v_b_gate', 'new_v_w_pool', 'new_v_pool_scale', 'new_v_w_proj_attn', 'new_v_w_proj_pool', 'new_v_w_out', 'new_v_ln_gamma', 'new_v_ln_beta']
TWIN_LEAF_KINDS = {'loss': 'loss', 'grad_x': 'grad_x', 'grad_w_in': 'grad_w', 'grad_b_gate': 'grad_w', 'grad_w_pool': 'grad_w', 'grad_pool_scale': 'grad_w', 'grad_w_proj_attn': 'grad_w', 'grad_w_proj_pool': 'grad_w', 'grad_w_out': 'grad_w', 'grad_ln_gamma': 'grad_w', 'grad_ln_beta': 'grad_w', 'delta_w_in': 'delta_w', 'delta_b_gate': 'delta_w', 'delta_w_pool': 'delta_w', 'delta_pool_scale': 'delta_w', 'delta_w_proj_attn': 'delta_w', 'delta_w_proj_pool': 'delta_w', 'delta_w_out': 'delta_w', 'delta_ln_gamma': 'delta_w', 'delta_ln_beta': 'delta_w', 'new_m_w_in': 'new_m', 'new_m_b_gate': 'new_m', 'new_m_w_pool': 'new_m', 'new_m_pool_scale': 'new_m', 'new_m_w_proj_attn': 'new_m', 'new_m_w_proj_pool': 'new_m', 'new_m_w_out': 'new_m', 'new_m_ln_gamma': 'new_m', 'new_m_ln_beta': 'new_m', 'new_v_w_in': 'new_v', 'new_v_b_gate': 'new_v', 'new_v_w_pool': 'new_v', 'new_v_pool_scale': 'new_v', 'new_v_w_proj_attn': 'new_v', 'new_v_w_proj_pool': 'new_v', 'new_v_w_out': 'new_v', 'new_v_ln_gamma': 'new_v', 'new_v_ln_beta': 'new_v'}


def _forward(args):
    return _fwd_reference(*[args[k] for k in FWD_PARAMS])


def _output_shape():
    def fwd():
        inp = _fwd_setup_inputs(0)
        return _fwd_reference(*[inp[k] for k in FWD_PARAMS])
    out = _jax.eval_shape(fwd)
    return out.shape, out.dtype

N_MICROBATCH = 1
ADAM_LR = 0.001
ADAM_B1 = 0.9
ADAM_B2 = 0.999
ADAM_EPS = 1e-08
ADAM_WD = 0.01
ADAM_STEP = 10
PER_EXAMPLE_BATCH_AXIS = {'x': 0, 'loss_target': 0}
SHARED_INPUTS = []
_WEIGHT_DTYPES = {'w_in': _jnp.float32, 'b_gate': _jnp.float32, 'w_pool': _jnp.float32, 'pool_scale': _jnp.float32, 'w_proj_attn': _jnp.float32, 'w_proj_pool': _jnp.float32, 'w_out': _jnp.float32, 'ln_gamma': _jnp.float32, 'ln_beta': _jnp.float32}
MOMENT_SCALE = {'w_in': 4.038524e-03, 'b_gate': 2.195879e-03, 'w_pool': 1.052088e-02, 'pool_scale': 1.094595e-02, 'w_proj_attn': 2.709898e-03, 'w_proj_pool': 1.246006e-02, 'w_out': 1.264155e-02, 'ln_gamma': 1.599565e+01, 'ln_beta': 2.759684e-01}


def _to_microbatches(a, axis):
    t = _jnp.moveaxis(a, axis, 0)
    t = t.reshape((N_MICROBATCH, t.shape[0] // N_MICROBATCH) + t.shape[1:])
    return _jnp.moveaxis(t, 1, axis + 1)


def setup_inputs(seed: int = 0) -> dict:
    inp = _fwd_setup_inputs(seed)
    key = _jax.random.fold_in(_jax.random.key(seed), 7919)
    shape, _ = _output_shape()
    out = dict(inp)
    out["loss_target"] = _jax.random.normal(_jax.random.fold_in(key, 0), shape, _jnp.float32)
    for i, name in enumerate(TWIN_WEIGHTS):
        w = inp[name].astype(_jnp.float32)
        if MOMENT_SCALE is None:
            s = _jnp.sqrt(_jnp.mean(_jnp.square(w)) + 1e-30)
        else:
            s = MOMENT_SCALE[name]
        km, kv = _jax.random.split(_jax.random.fold_in(key, i + 1))
        out[name] = w
        out["m_" + name] = s * _jax.random.normal(km, w.shape, _jnp.float32)
        out["v_" + name] = (s * s) * _jax.random.uniform(kv, w.shape, _jnp.float32, 0.5, 1.5)
    if N_MICROBATCH > 1:
        for name, axis in PER_EXAMPLE_BATCH_AXIS.items():
            out[name] = _to_microbatches(out[name], axis)
    return {'x': out['x'], 'w_in': out['w_in'], 'b_gate': out['b_gate'], 'w_pool': out['w_pool'], 'pool_scale': out['pool_scale'], 'w_proj_attn': out['w_proj_attn'], 'w_proj_pool': out['w_proj_pool'], 'w_out': out['w_out'], 'ln_gamma': out['ln_gamma'], 'ln_beta': out['ln_beta'], 'loss_target': out['loss_target'], 'm_w_in': out['m_w_in'], 'm_b_gate': out['m_b_gate'], 'm_w_pool': out['m_w_pool'], 'm_pool_scale': out['m_pool_scale'], 'm_w_proj_attn': out['m_w_proj_attn'], 'm_w_proj_pool': out['m_w_proj_pool'], 'm_w_out': out['m_w_out'], 'm_ln_gamma': out['m_ln_gamma'], 'm_ln_beta': out['m_ln_beta'], 'v_w_in': out['v_w_in'], 'v_b_gate': out['v_b_gate'], 'v_w_pool': out['v_w_pool'], 'v_pool_scale': out['v_pool_scale'], 'v_w_proj_attn': out['v_w_proj_attn'], 'v_w_proj_pool': out['v_w_proj_pool'], 'v_w_out': out['v_w_out'], 'v_ln_gamma': out['v_ln_gamma'], 'v_ln_beta': out['v_ln_beta']}


def _loss(weights, diff, rest, loss_target):
    with _jax.named_scope("forward"):
        args = {**rest, TWIN_DIFF_INPUT: diff, **{k: w.astype(_WEIGHT_DTYPES[k]) for k, w in weights.items()}}
        y = _forward(args)
    with _jax.named_scope("loss_head"):
        err = _jnp.square(y.astype(_jnp.float32) - loss_target)
        return 0.5 * _jnp.sum(_jnp.mean(err, axis=-1)) if err.ndim else 0.5 * err


def _adamw(w, g, m, v):
    m = ADAM_B1 * m + (1.0 - ADAM_B1) * g
    v = ADAM_B2 * v + (1.0 - ADAM_B2) * _jnp.square(g)
    m_hat = m / (1.0 - ADAM_B1 ** ADAM_STEP)
    v_hat = v / (1.0 - ADAM_B2 ** ADAM_STEP)
    delta = -ADAM_LR * (m_hat / (_jnp.sqrt(v_hat) + ADAM_EPS) + ADAM_WD * w)
    return delta, m, v


def reference(x, w_in, b_gate, w_pool, pool_scale, w_proj_attn, w_proj_pool, w_out, ln_gamma, ln_beta, loss_target, m_w_in, m_b_gate, m_w_pool, m_pool_scale, m_w_proj_attn, m_w_proj_pool, m_w_out, m_ln_gamma, m_ln_beta, v_w_in, v_b_gate, v_w_pool, v_pool_scale, v_w_proj_attn, v_w_proj_pool, v_w_out, v_ln_gamma, v_ln_beta):
    given = dict(x=x, w_in=w_in, b_gate=b_gate, w_pool=w_pool, pool_scale=pool_scale, w_proj_attn=w_proj_attn, w_proj_pool=w_proj_pool, w_out=w_out, ln_gamma=ln_gamma, ln_beta=ln_beta, loss_target=loss_target, m_w_in=m_w_in, m_b_gate=m_b_gate, m_w_pool=m_w_pool, m_pool_scale=m_pool_scale, m_w_proj_attn=m_w_proj_attn, m_w_proj_pool=m_w_proj_pool, m_w_out=m_w_out, m_ln_gamma=m_ln_gamma, m_ln_beta=m_ln_beta, v_w_in=v_w_in, v_b_gate=v_b_gate, v_w_pool=v_w_pool, v_pool_scale=v_pool_scale, v_w_proj_attn=v_w_proj_attn, v_w_proj_pool=v_w_proj_pool, v_w_out=v_w_out, v_ln_gamma=v_ln_gamma, v_ln_beta=v_ln_beta)
    weights = {n: given[n] for n in TWIN_WEIGHTS}
    shared = {n: given[n] for n in SHARED_INPUTS}
    per_example = {n: given[n] for n in ['x']}
    grad_fn = _jax.value_and_grad(_loss, argnums=(0, 1))

    def one_microbatch(ex, loss_target):
        ex = dict(ex)
        diff = ex.pop(TWIN_DIFF_INPUT)
        return grad_fn(weights, diff, {**shared, **ex}, loss_target)

    if N_MICROBATCH == 1:
        loss, (grad_w, grad_x) = one_microbatch(per_example, given["loss_target"])
    else:
        def body(carry, xs):
            loss_sum, grad_sum = carry
            l_k, (gw_k, gx_k) = one_microbatch(xs[0], xs[1])
            with _jax.named_scope("update"):
                return (loss_sum + l_k, _jax.tree.map(_jnp.add, grad_sum, gw_k)), gx_k

        init = (_jnp.zeros((), _jnp.float32), _jax.tree.map(_jnp.zeros_like, weights))
        (loss, grad_w), grad_x = _jax.lax.scan(body, init, (per_example, given["loss_target"]))
    with _jax.named_scope("update"):
        delta_w, new_m, new_v = {}, {}, {}
        for n in TWIN_WEIGHTS:
            delta_w[n], new_m[n], new_v[n] = _adamw(weights[n], grad_w[n], given["m_" + n], given["v_" + n])
    return (loss, grad_x, *[grad_w[n] for n in TWIN_WEIGHTS], *[delta_w[n] for n in TWIN_WEIGHTS],
            *[new_m[n] for n in TWIN_WEIGHTS], *[new_v[n] for n in TWIN_WEIGHTS])
```

```python
import functools

import jax
import jax.numpy as jnp
from jax import lax
from jax.experimental import pallas as pl
from jax.experimental.pallas import tpu as pltpu

F32 = jnp.float32
BF16 = jnp.bfloat16

S = 4096
D = 2048
NW = 16384
AW = 1024
HD = 128
NH = 8
QB = 128
NBLK = S // QB
DILATIONS = (1, 4, 16)
POOL_WINDOWS = (2, 4, 8, 16)
PG = 256
N_DEV = 8
COL_Q, COL_K, COL_V = 0, 3 * AW, 6 * AW
COL_ZA, COL_U, COL_ZP, COL_G = 9 * AW, 10 * AW, 11 * AW, 12 * AW
ALPHA = 2.0 ** 0.25
LN_EPS = 1e-5
NEG_INF = -1e30
LR, B1, B2, EPS, WD, STEP = 0.001, 0.9, 0.999, 1e-08, 0.01, 10
R_IN, R_OUT, R_PA, R_PP, R_PL = 0, 2048, 2304, 2432, 2560
R_ALL = 2576
R_SMALL = R_ALL - R_OUT
VMEM_LIMIT = 56 * 1024 * 1024
MESH = pl.DeviceIdType.MESH
ANY = pl.BlockSpec(memory_space=pl.ANY)


def _cparams(n_axes):
    return pltpu.CompilerParams(dimension_semantics=("arbitrary",) * n_axes, vmem_limit_bytes=VMEM_LIMIT)


def _sigmoid(z):
    return 1.0 / (1.0 + jnp.exp(-z))


def _nt(a, b):
    return lax.dot_general(a, b, (((1,), (1,)), ((), ())), preferred_element_type=F32)


def _tn(a, b):
    return lax.dot_general(a, b, (((0,), (0,)), ((), ())), preferred_element_type=F32)


def _nn(a, b):
    return jnp.dot(a, b, preferred_element_type=F32)


def _lin(x, y, c):
    return 4 * x + 2 * y + c


def _flip(v, f):
    return 1 - v if f else v


def _exchange(name, src, plan, *, dst_shape=None, local_dst=None):
    n = len(plan)
    in_place = dst_shape is None
    out_sds = jax.ShapeDtypeStruct(src.shape, src.dtype) if in_place else dst_shape

    def body(src_ref, dst_ref, send_sems, recv_sems, local_sem):
        x, y, c = lax.axis_index("x"), lax.axis_index("y"), lax.axis_index("c")

        def copy(k, sender):
            flip, src_index, dst_index = plan[k]
            sx, sy, sc = sender
            to = (_flip(sx, flip[0]), _flip(sy, flip[1]), _flip(sc, flip[2]))
            s = src_ref if src_index is None else src_ref.at[src_index(sx, sy, sc)]
            return pltpu.make_async_remote_copy(
                src_ref=s, dst_ref=dst_ref.at[dst_index(sx, sy, sc)],
                send_sem=send_sems.at[k], recv_sem=recv_sems.at[k],
                device_id=to, device_id_type=MESH)

        me = (x, y, c)
        if local_dst is not None:
            mine = pltpu.make_async_copy(src_ref, dst_ref.at[local_dst(x, y, c)], local_sem)
            mine.start()
        sends = [copy(k, me) for k in range(n)]
        for cp in sends:
            cp.start()
        for k in range(n):
            flip = plan[k][0]
            copy(k, (_flip(x, flip[0]), _flip(y, flip[1]), _flip(c, flip[2]))).wait_recv()
        for cp in sends:
            cp.wait_send()
        if local_dst is not None:
            mine.wait()

    return pl.pallas_call(
        body, name=name, out_shape=out_sds, in_specs=[ANY], out_specs=ANY,
        input_output_aliases={0: 0} if in_place else {},
        scratch_shapes=[pltpu.SemaphoreType.DMA((n,)), pltpu.SemaphoreType.DMA((n,)),
                        pltpu.SemaphoreType.DMA(())],
    )(src)


FLIP_C, FLIP_X, FLIP_Y, FLIP_XY = (0, 0, 1), (1, 0, 0), (0, 1, 0), (1, 1, 0)
CHIP_FLIPS = ((0, 0), (1, 0), (0, 1), (1, 1))


def _all_gather_slabs(pack):
    own = lambda x, y, c: _lin(x, y, c)
    g = _exchange("ag_chips", pack,
                  [(FLIP_C, None, own), (FLIP_X, None, own), (FLIP_Y, None, own), (FLIP_XY, None, own)],
                  dst_shape=jax.ShapeDtypeStruct((N_DEV,) + pack.shape, pack.dtype), local_dst=own)
    fwd = [(FLIP_C, (lambda x, y, c, f=f: _lin(_flip(x, f[0]), _flip(y, f[1]), c)),
            (lambda x, y, c, f=f: _lin(_flip(x, f[0]), _flip(y, f[1]), c))) for f in CHIP_FLIPS[1:]]
    return _exchange("ag_sibling", g, fwd)


def _all_gather_direct(name, vec):
    own = lambda x, y, c: _lin(x, y, c)
    flips = [(fx, fy, fc) for fx in (0, 1) for fy in (0, 1) for fc in (0, 1) if (fx, fy, fc) != (0, 0, 0)]
    return _exchange(name, vec, [(f, None, own) for f in flips],
                     dst_shape=jax.ShapeDtypeStruct((N_DEV,) + vec.shape, vec.dtype), local_dst=own)


def _rs_sibling(p):
    plan = [(FLIP_C, (lambda x, y, c, f=f: _lin(_flip(x, f[0]), _flip(y, f[1]), 1 - c)),
             (lambda x, y, c, k=k: k)) for k, f in enumerate(CHIP_FLIPS)]
    return _exchange("rs_sibling", p, plan, dst_shape=jax.ShapeDtypeStruct((4,) + p.shape[1:], p.dtype))


def _rs_chips(q):
    plan = [((f[0], f[1], 0), (lambda x, y, c, k=k: k + 1), (lambda x, y, c, k=k: k))
            for k, f in enumerate(CHIP_FLIPS[1:])]
    return _exchange("rs_chips", q, plan, dst_shape=jax.ShapeDtypeStruct((3,) + q.shape[1:], q.dtype))


def _pair_sum(coords, p, l1):
    tr = 368
    nr = R_ALL // tr

    def body(crd, p_ref, l_ref, q_ref):
        q_ref[...] = (p_ref[...].astype(F32) + l_ref[...].astype(F32)).astype(q_ref.dtype)

    def p_map(k, i, crd):
        fx, fy = k % 2, k // 2
        px = crd[0] + fx - 2 * fx * crd[0]
        py = crd[1] + fy - 2 * fy * crd[1]
        return (_lin(px, py, crd[2]), i, 0)

    grid_spec = pltpu.PrefetchScalarGridSpec(
        num_scalar_prefetch=1, grid=(4, nr),
        in_specs=[pl.BlockSpec((None, tr, D), p_map),
                  pl.BlockSpec((None, tr, D), lambda k, i, crd: (k, i, 0))],
        out_specs=pl.BlockSpec((None, tr, D), lambda k, i, crd: (k, i, 0)))
    return pl.pallas_call(body, name="pair_sum", grid_spec=grid_spec,
                          out_shape=jax.ShapeDtypeStruct((4, R_ALL, D), BF16),
                          compiler_params=_cparams(2))(coords, p, l1)


def _proj_in(xb, g):
    tm, tn = 1024, 1024

    def body(x_ref, w_ref, o_ref):
        o_ref[...] = _nn(x_ref[...], w_ref[...]).astype(o_ref.dtype)

    return pl.pallas_call(
        body, name="proj_in", grid=(NW // tn, S // tm),
        in_specs=[pl.BlockSpec((tm, D), lambda n, i: (i, 0)),
                  pl.BlockSpec((None, D, tn), lambda n, i: (n // 2, 0, n % 2))],
        out_specs=pl.BlockSpec((tm, tn), lambda n, i: (i, n)),
        out_shape=jax.ShapeDtypeStruct((S, NW), BF16), compiler_params=_cparams(2))(xb, g)


def _grad_x(dh, g, dr):
    tm, tk = 512, 1024
    nk = NW // tk

    def body(dh_ref, w_ref, dr_ref, o_ref):
        k = pl.program_id(1)

        @pl.when(k == 0)
        def _():
            o_ref[...] = ALPHA * dr_ref[...]

        o_ref[...] += _nt(dh_ref[...], w_ref[...])

    return pl.pallas_call(
        body, name="grad_x", grid=(S // tm, nk),
        in_specs=[pl.BlockSpec((tm, tk), lambda i, k: (i, k)),
                  pl.BlockSpec((None, D, tk), lambda i, k: (k // 2, 0, k % 2)),
                  pl.BlockSpec((tm, D), lambda i, k: (i, 0))],
        out_specs=pl.BlockSpec((tm, D), lambda i, k: (i, 0)),
        out_shape=jax.ShapeDtypeStruct((S, D), F32), compiler_params=_cparams(2))(dh, g, dr)


def _grad_w(name, at, b, out_shape, out_spec, tn, tk):
    m, k_all = at.shape
    n_all = b.shape[1]
    nk = k_all // tk

    def body(a_ref, b_ref, o_ref, acc_ref):
        k = pl.program_id(1)

        @pl.when(k == 0)
        def _():
            acc_ref[...] = jnp.zeros_like(acc_ref)

        acc_ref[...] += _nn(a_ref[...], b_ref[...])

        @pl.when(k == nk - 1)
        def _():
            o_ref[...] = acc_ref[...].astype(o_ref.dtype)

    return pl.pallas_call(
        body, name=name, grid=(n_all // tn, nk),
        in_specs=[pl.BlockSpec((m, tk), lambda n, k: (0, k)),
                  pl.BlockSpec((tk, tn), lambda n, k: (k, n))],
        out_specs=out_spec, out_shape=out_shape,
        scratch_shapes=[pltpu.VMEM((m, tn), F32)], compiler_params=_cparams(2))(at, b)


def _attn_mask(n):
    qi = lax.broadcasted_iota(jnp.int32, (QB, 2 * QB), 0)
    kj = lax.broadcasted_iota(jnp.int32, (QB, 2 * QB), 1)
    dist = QB + qi - kj
    return (dist >= 0) & (dist <= QB) & ((kj >= QB) | (n > 0))


def _qkv_specs(qc, kc, vc, clamp):
    cur = lambda col: pl.BlockSpec((QB, AW), lambda b: (clamp(b), col))
    prev = lambda col: pl.BlockSpec((QB, AW), lambda b: (jnp.maximum(clamp(b) - 1, 0), col))
    return [cur(qc), cur(kc), prev(kc), cur(vc), prev(vc)]


def _attn_fwd(g, q, k, v, qc, kc, vc):
    nb = NBLK // DILATIONS[g]
    scale = HD ** -0.5

    def body(q_ref, kc_ref, kp_ref, vc_ref, vp_ref, o_ref, l_ref):
        valid = _attn_mask(pl.program_id(0) % nb)
        for h in range(NH):
            sl = slice(h * HD, (h + 1) * HD)
            kh = jnp.concatenate([kp_ref[:, sl], kc_ref[:, sl]], axis=0)
            vh = jnp.concatenate([vp_ref[:, sl], vc_ref[:, sl]], axis=0)
            s = jnp.where(valid, _nt(q_ref[:, sl], kh) * scale, NEG_INF)
            m = jnp.max(s, axis=-1, keepdims=True)
            e = jnp.exp(s - m)
            den = jnp.sum(e, axis=-1, keepdims=True)
            o_ref[:, sl] = (_nn(e.astype(BF16), vh) / den).astype(o_ref.dtype)
            l_ref[:, h:h + 1] = m + jnp.log(den)

    return pl.pallas_call(
        body, name=f"attn_fwd_{g}", grid=(NBLK,),
        in_specs=_qkv_specs(qc, kc, vc, lambda b: b),
        out_specs=[pl.BlockSpec((QB, AW), lambda b: (b, 0)), pl.BlockSpec((QB, NH), lambda b: (b, 0))],
        out_shape=[jax.ShapeDtypeStruct((S, AW), BF16), jax.ShapeDtypeStruct((S, NH), F32)],
        compiler_params=_cparams(1))(q, k, k, v, v)


def _attn_bwd(g, q, k, v, qc, kc, vc, do, lse, delta):
    nb = NBLK // DILATIONS[g]
    scale = HD ** -0.5
    last = NBLK - 1
    clamp = lambda b: jnp.minimum(b, last)
    row = lambda width: pl.BlockSpec((QB, width), lambda b: (clamp(b), 0))
    behind = pl.BlockSpec((QB, AW), lambda b: (jnp.maximum(b - 1, 0), 0))

    def body(q_ref, kc_ref, kp_ref, vc_ref, vp_ref, do_ref, l_ref, dl_ref, dq_ref, dk_ref, dv_ref, ck_ref, cv_ref):
        b = pl.program_id(0)

        @pl.when(b == 0)
        def _():
            ck_ref[...] = jnp.zeros_like(ck_ref)
            cv_ref[...] = jnp.zeros_like(cv_ref)

        @pl.when(b <= last)
        def _():
            valid = _attn_mask(b % nb)
            for h in range(NH):
                sl = slice(h * HD, (h + 1) * HD)
                qh, doh = q_ref[:, sl], do_ref[:, sl]
                kh = jnp.concatenate([kp_ref[:, sl], kc_ref[:, sl]], axis=0)
                vh = jnp.concatenate([vp_ref[:, sl], vc_ref[:, sl]], axis=0)
                s = _nt(qh, kh) * scale
                p = jnp.where(valid, jnp.exp(s - l_ref[:, h:h + 1]), 0.0)
                ds = p * (_nt(doh, vh) - dl_ref[:, h:h + 1])
                dsb = (ds * scale).astype(BF16)
                dq_ref[:, sl] = _nn(dsb, kh).astype(dq_ref.dtype)
                dk2 = _tn(dsb, qh)
                dv2 = _tn(p.astype(BF16), doh)
                dk_ref[:, sl] = (ck_ref[:, sl] + dk2[:QB]).astype(dk_ref.dtype)
                dv_ref[:, sl] = (cv_ref[:, sl] + dv2[:QB]).astype(dv_ref.dtype)
                ck_ref[:, sl] = dk2[QB:]
                cv_ref[:, sl] = dv2[QB:]

        @pl.when(b > last)
        def _():
            dk_ref[...] = ck_ref[...].astype(dk_ref.dtype)
            dv_ref[...] = cv_ref[...].astype(dv_ref.dtype)

    return pl.pallas_call(
        body, name=f"attn_bwd_{g}", grid=(NBLK + 1,),
        in_specs=_qkv_specs(qc, kc, vc, clamp) + [row(AW), row(NH), row(NH)],
        out_specs=[row(AW), behind, behind],
        out_shape=[jax.ShapeDtypeStruct((S, AW), BF16)] * 3,
        scratch_shapes=[pltpu.VMEM((QB, AW), F32), pltpu.VMEM((QB, AW), F32)],
        compiler_params=_cparams(1))(q, k, k, v, v, do, lse, delta)


TM = 256
NT = S // TM


def _group_weights(l0, l1, l2):
    m = jnp.maximum(jnp.maximum(l0, l1), l2)
    e0, e1, e2 = jnp.exp(l0 - m), jnp.exp(l1 - m), jnp.exp(l2 - m)
    inv = 1.0 / (e0 + e1 + e2)
    return e0 * inv, e1 * inv, e2 * inv


def _window_count(tile, w):
    t = tile * TM + lax.broadcasted_iota(jnp.int32, (TM, 1), 0)
    return jnp.minimum(t + 1, w).astype(F32)


def _pool_tokens(u_prev, u_cur, tile):
    ext = jnp.concatenate([jnp.where(tile > 0, u_prev, 0.0), u_cur], axis=0)
    out = []
    for g, w in enumerate(POOL_WINDOWS):
        s = ext[:, g * PG:(g + 1) * PG]
        span = 1
        while span < w:
            s = s + pltpu.roll(s, span, 0)
            span *= 2
        out.append(s[TM:] / _window_count(tile, w) - u_cur[:, g * PG:(g + 1) * PG])
    return jnp.concatenate(out, axis=1)


def _pool_tokens_bwd(dp, dpc_next, tile):
    du, dpc_all = [], []
    for g, w in enumerate(POOL_WINDOWS):
        sl = slice(g * PG, (g + 1) * PG)
        dpc = dp[:, sl] / _window_count(tile, w)
        s = jnp.concatenate([dpc, dpc_next[:, sl]], axis=0)
        span = 1
        while span < w:
            s = s + pltpu.roll(s, 2 * TM - span, 0)
            span *= 2
        du.append(s[:TM] - dp[:, sl])
        dpc_all.append(dpc)
    return jnp.concatenate(du, axis=1), jnp.concatenate(dpc_all, axis=1)


def _pool_linear(pb, wpool_ref):
    return jnp.concatenate([_nn(pb[:, g * PG:(g + 1) * PG], wpool_ref[g]) for g in range(len(POOL_WINDOWS))], axis=1)


def _tok(width, col=0, rev=False):
    if rev:
        return pl.BlockSpec((TM, width), lambda i: (NT - 1 - i, col))
    return pl.BlockSpec((TM, width), lambda i: (i, col))


def _whole(shape):
    return pl.BlockSpec(shape, lambda i: (0,) * len(shape))


def _mix_fwd(h, o, lse, wpa, wpp, wpool, pscale, bgate):
    def body(o0_ref, o1_ref, o2_ref, l0_ref, l1_ref, l2_ref, za_ref, uc_ref, up_ref, zp_ref, gp_ref,
             wpa_ref, wpp_ref, wpool_ref, ps_ref, bg_ref, ya_ref, yp_ref, mg_ref, a_ref, b_ref, p_ref):
        i = pl.program_id(0)
        w0, w1, w2 = _group_weights(l0_ref[...], l1_ref[...], l2_ref[...])
        za = za_ref[...].astype(F32)
        silu_a = za * _sigmoid(za)
        for hh in range(NH):
            sl = slice(hh * HD, (hh + 1) * HD)
            c = slice(hh, hh + 1)
            oh = (w0[:, c] * o0_ref[:, sl].astype(F32) + w1[:, c] * o1_ref[:, sl].astype(F32)
                  + w2[:, c] * o2_ref[:, sl].astype(F32))
            ya_ref[:, sl] = (oh * silu_a[:, sl]).astype(BF16)
        p_ref[...] = _pool_tokens(up_ref[...].astype(F32), uc_ref[...].astype(F32), i).astype(BF16)
        zp = zp_ref[...].astype(F32)
        yp_ref[...] = (_pool_linear(p_ref[...], wpool_ref) * ps_ref[...] * (zp * _sigmoid(zp))).astype(BF16)
        a = _nn(ya_ref[...], wpa_ref[...])
        b = _nn(yp_ref[...], wpp_ref[...])
        a_ref[...] = a.astype(BF16)
        b_ref[...] = b.astype(BF16)
        gates = _sigmoid(gp_ref[...].astype(F32) + bg_ref[...])
        mg_ref[...] = (gates[:, :D] * a + gates[:, D:] * b).astype(BF16)

    u_prev = pl.BlockSpec((TM, AW), lambda i: (jnp.maximum(i - 1, 0), COL_U // AW))
    return pl.pallas_call(
        body, name="mix_fwd", grid=(NT,),
        in_specs=[_tok(AW)] * 3 + [_tok(NH)] * 3
        + [_tok(AW, COL_ZA // AW), _tok(AW, COL_U // AW), u_prev, _tok(AW, COL_ZP // AW), _tok(2 * D, COL_G // (2 * D))]
        + [_whole((AW, D)), _whole((AW, D)), _whole((4, PG, PG)), _whole((1, AW)), _whole((1, 2 * D))],
        out_specs=[_tok(AW), _tok(AW), _tok(D), _tok(D), _tok(D), _tok(AW)],
        out_shape=[jax.ShapeDtypeStruct((S, AW), BF16)] * 2 + [jax.ShapeDtypeStruct((S, D), BF16)] * 3
        + [jax.ShapeDtypeStruct((S, AW), BF16)],
        compiler_params=_cparams(1))(*o, *lse, h, h, h, h, h, wpa, wpp, wpool, pscale, bgate)


def _out_ln(merged, x, target, wout, gamma, beta):
    def body(mg_ref, x_ref, t_ref, w_ref, g_ref, b_ref, dr_ref, drb_ref, dm_ref, loss_ref, dg_ref, db_ref):
        i = pl.program_id(0)

        @pl.when(i == 0)
        def _():
            loss_ref[...] = jnp.zeros_like(loss_ref)
            dg_ref[...] = jnp.zeros_like(dg_ref)
            db_ref[...] = jnp.zeros_like(db_ref)

        r = ALPHA * x_ref[...] + _nn(mg_ref[...], w_ref[...])
        mu = jnp.mean(r, axis=-1, keepdims=True)
        rc = r - mu
        rstd = lax.rsqrt(jnp.mean(rc * rc, axis=-1, keepdims=True) + LN_EPS)
        xhat = rc * rstd
        err = xhat * g_ref[...] + b_ref[...] - t_ref[...]
        loss_ref[...] += 0.5 * jnp.sum(jnp.mean(err * err, axis=-1, keepdims=True), axis=0, keepdims=True)
        dy = err * (1.0 / D)
        dg_ref[...] += jnp.sum(dy * xhat, axis=0, keepdims=True)
        db_ref[...] += jnp.sum(dy, axis=0, keepdims=True)
        dxh = dy * g_ref[...]
        dr = rstd * (dxh - jnp.mean(dxh, axis=-1, keepdims=True)
                     - xhat * jnp.mean(dxh * xhat, axis=-1, keepdims=True))
        dr_ref[...] = dr
        drb_ref[...] = dr.astype(BF16)
        dm_ref[...] = _nt(drb_ref[...], w_ref[...]).astype(BF16)

    return pl.pallas_call(
        body, name="out_ln", grid=(NT,),
        in_specs=[_tok(D), _tok(D), _tok(D), _whole((D, D)), _whole((1, D)), _whole((1, D))],
        out_specs=[_tok(D), _tok(D), _tok(D), _whole((8, 128)), _whole((1, D)), _whole((1, D))],
        out_shape=[jax.ShapeDtypeStruct((S, D), F32), jax.ShapeDtypeStruct((S, D), BF16),
                   jax.ShapeDtypeStruct((S, D), BF16), jax.ShapeDtypeStruct((8, 128), F32),
                   jax.ShapeDtypeStruct((1, D), F32), jax.ShapeDtypeStruct((1, D), F32)],
        compiler_params=_cparams(1))(merged, x, target, wout, gamma, beta)


def _gate_bwd(dm, a, b, h, bgate):
    def body(dm_ref, a_ref, b_ref, gp_ref, bg_ref, dgp_ref, da_ref, db_ref, dbg_ref):
        @pl.when(pl.program_id(0) == 0)
        def _():
            dbg_ref[...] = jnp.zeros_like(dbg_ref)

        dm_ = dm_ref[...].astype(F32)
        gates = _sigmoid(gp_ref[...].astype(F32) + bg_ref[...])
        ga, gb = gates[:, :D], gates[:, D:]
        da_ref[...] = (dm_ * ga).astype(BF16)
        db_ref[...] = (dm_ * gb).astype(BF16)
        dgp = jnp.concatenate([dm_ * a_ref[...].astype(F32) * ga * (1.0 - ga),
                               dm_ * b_ref[...].astype(F32) * gb * (1.0 - gb)], axis=1)
        dgp_ref[...] = dgp.astype(BF16)
        dbg_ref[...] += jnp.sum(dgp, axis=0, keepdims=True)

    return pl.pallas_call(
        body, name="gate_bwd", grid=(NT,),
        in_specs=[_tok(D), _tok(D), _tok(D), _tok(2 * D, COL_G // (2 * D)), _whole((1, 2 * D))],
        out_specs=[_tok(2 * D), _tok(D), _tok(D), _whole((1, 2 * D))],
        out_shape=[jax.ShapeDtypeStruct((S, 2 * D), BF16), jax.ShapeDtypeStruct((S, D), BF16),
                   jax.ShapeDtypeStruct((S, D), BF16), jax.ShapeDtypeStruct((1, 2 * D), F32)],
        compiler_params=_cparams(1))(dm, a, b, h, bgate)


def _mix_bwd(da, db, h, o, lse, p, wpa, wpp, wpool, pscale):
    def body(da_ref, db_ref, o0_ref, o1_ref, o2_ref, l0_ref, l1_ref, l2_ref, za_ref, zp_ref, p_ref,
             wpa_ref, wpp_ref, wpool_ref, ps_ref,
             dza_ref, du_ref, dzp_ref, do0_ref, do1_ref, do2_ref, dl0_ref, dl1_ref, dl2_ref, dwp_ref, dps_ref,
             nxt_ref):
        i = pl.program_id(0)
        tile = NT - 1 - i

        @pl.when(i == 0)
        def _():
            nxt_ref[...] = jnp.zeros_like(nxt_ref)
            dwp_ref[...] = jnp.zeros_like(dwp_ref)
            dps_ref[...] = jnp.zeros_like(dps_ref)

        dya = _nt(da_ref[...], wpa_ref[...])
        w0, w1, w2 = _group_weights(l0_ref[...], l1_ref[...], l2_ref[...])
        za = za_ref[...].astype(F32)
        sig = _sigmoid(za)
        silu_a = za * sig
        dsilu_a = sig * (1.0 + za * (1.0 - sig))
        for hh in range(NH):
            sl = slice(hh * HD, (hh + 1) * HD)
            c = slice(hh, hh + 1)
            oh = (w0[:, c] * o0_ref[:, sl].astype(F32) + w1[:, c] * o1_ref[:, sl].astype(F32)
                  + w2[:, c] * o2_ref[:, sl].astype(F32))
            doh = dya[:, sl] * silu_a[:, sl]
            dza_ref[:, sl] = (dya[:, sl] * oh * dsilu_a[:, sl]).astype(BF16)
            dot_ = jnp.sum(doh * oh, axis=-1, keepdims=True)
            do0_ref[:, sl] = (w0[:, c] * doh).astype(BF16)
            do1_ref[:, sl] = (w1[:, c] * doh).astype(BF16)
            do2_ref[:, sl] = (w2[:, c] * doh).astype(BF16)
            dl0_ref[:, c] = w0[:, c] * dot_
            dl1_ref[:, c] = w1[:, c] * dot_
            dl2_ref[:, c] = w2[:, c] * dot_
        dyp = _nt(db_ref[...], wpp_ref[...])
        pb = p_ref[...]
        pw = _pool_linear(pb, wpool_ref)
        zp = zp_ref[...].astype(F32)
        sigp = _sigmoid(zp)
        dypre = dyp * (zp * sigp)
        dzp_ref[...] = (dyp * (pw * ps_ref[...]) * (sigp * (1.0 + zp * (1.0 - sigp)))).astype(BF16)
        dps_ref[...] += jnp.sum(dypre * pw, axis=0, keepdims=True)
        dpw = (dypre * ps_ref[...]).astype(BF16)
        dp = []
        for g in range(len(POOL_WINDOWS)):
            sl = slice(g * PG, (g + 1) * PG)
            dwp_ref[g] += _tn(pb[:, sl], dpw[:, sl])
            dp.append(_nt(dpw[:, sl], wpool_ref[g]))
        du, dpc = _pool_tokens_bwd(jnp.concatenate(dp, axis=1), nxt_ref[...], tile)
        du_ref[...] = du.astype(BF16)
        nxt_ref[...] = dpc

    r = functools.partial(_tok, rev=True)
    return pl.pallas_call(
        body, name="mix_bwd", grid=(NT,),
        in_specs=[r(D), r(D)] + [r(AW)] * 3 + [r(NH)] * 3 + [r(AW, COL_ZA // AW), r(AW, COL_ZP // AW), r(AW)]
        + [_whole((AW, D)), _whole((AW, D)), _whole((4, PG, PG)), _whole((1, AW))],
        out_specs=[r(AW)] * 6 + [r(NH)] * 3 + [_whole((4, PG, PG)), _whole((1, AW))],
        out_shape=[jax.ShapeDtypeStruct((S, AW), BF16)] * 6 + [jax.ShapeDtypeStruct((S, NH), F32)] * 3
        + [jax.ShapeDtypeStruct((4, PG, PG), F32), jax.ShapeDtypeStruct((1, AW), F32)],
        scratch_shapes=[pltpu.VMEM((TM, AW), F32)],
        compiler_params=_cparams(1))(da, db, *o, *lse, h, h, p, wpa, wpp, wpool, pscale)


def _adamw(w, g, m, v):
    m = B1 * m + (1.0 - B1) * g
    v = B2 * v + (1.0 - B2) * jnp.square(g)
    m_hat = m / (1.0 - B1 ** STEP)
    v_hat = v / (1.0 - B2 ** STEP)
    return -LR * (m_hat / (jnp.sqrt(v_hat) + EPS) + WD * w), m, v


def _adam_shard(name, q, l2, w, m, v, tr):
    rows = w.shape[0]

    def body(q_ref, l_ref, w_ref, m_ref, v_ref, g_out, d_out, m_out, v_out):
        g = q_ref[...].astype(F32)
        for k in range(3):
            g = g + l_ref[k].astype(F32)
        g_out[...] = g
        d_out[...], m_out[...], v_out[...] = _adamw(w_ref[...], g, m_ref[...], v_ref[...])

    blk = pl.BlockSpec((tr, D), lambda i: (i, 0))
    return pl.pallas_call(
        body, name=name, grid=(rows // tr,),
        in_specs=[pl.BlockSpec((None, tr, D), lambda i: (0, i, 0)), pl.BlockSpec((3, tr, D), lambda i: (0, i, 0)),
                  blk, blk, blk],
        out_specs=[blk] * 4, out_shape=[jax.ShapeDtypeStruct((rows, D), F32)] * 4,
        compiler_params=_cparams(1))(q, l2, w, m, v)


def _adam_replicated(gathered, w, m, v):
    def body(g_ref, w_ref, m_ref, v_ref, g_out, d_out, m_out, v_out):
        g = g_ref[0]
        for k in range(1, N_DEV):
            g = g + g_ref[k]
        g_out[...] = g
        d_out[...], m_out[...], v_out[...] = _adamw(w_ref[...], g, m_ref[...], v_ref[...])

    return pl.pallas_call(body, name="adam_replicated", out_shape=[jax.ShapeDtypeStruct((8, D), F32)] * 4,
                          compiler_params=pltpu.CompilerParams(vmem_limit_bytes=VMEM_LIMIT))(gathered, w, m, v)


def _dilate(a, d):
    if d == 1:
        return a
    return a.reshape(S // d, d, a.shape[1]).transpose(1, 0, 2).reshape(S, a.shape[1])


def _undilate(a, d):
    if d == 1:
        return a
    return a.reshape(d, S // d, a.shape[1]).transpose(1, 0, 2).reshape(S, a.shape[1])


def _pack_small(w_out, w_pa, w_pp, w_pool):
    return jnp.concatenate([w_out, w_pa.reshape(-1, D), w_pp.reshape(-1, D), w_pool.reshape(-1, D)], axis=0)


def _unpack_small(a):
    o = R_OUT
    return (a[:R_PA - o], a[R_PA - o:R_PP - o].reshape(AW, 256), a[R_PP - o:R_PL - o].reshape(AW, 256),
            a[R_PL - o:].reshape(4, 32, PG))


def _pack_vec(b_gate, gamma, beta, pscale, extra):
    z = jnp.zeros((D,), F32)
    return jnp.stack([b_gate[:D], b_gate[D:], gamma, beta, jnp.concatenate([pscale, z[:D - AW]]),
                      jnp.broadcast_to(extra, (D,)), z, z])


def _unpack_vec(a):
    return jnp.concatenate([a[0], a[1]])[None], a[4, :AW][None], a[2][None], a[3][None]


def kernel(x, w_in, b_gate, w_pool, pool_scale, w_proj_attn, w_proj_pool, w_out, ln_gamma, ln_beta, loss_target, m_w_in, m_b_gate, m_w_pool, m_pool_scale, m_w_proj_attn, m_w_proj_pool, m_w_out, m_ln_gamma, m_ln_beta, v_w_in, v_b_gate, v_w_pool, v_pool_scale, v_w_proj_attn, v_w_proj_pool, v_w_out, v_ln_gamma, v_ln_beta):
    coords = jnp.stack([lax.axis_index("x"), lax.axis_index("y"), lax.axis_index("c")]).astype(jnp.int32)
    x2, tgt = x[0], loss_target[0]
    xb = x2.astype(BF16)
    xt = x2.T.astype(BF16)

    pack = jnp.concatenate([w_in[0].astype(BF16),
                            _pack_small(w_out[0], w_proj_attn[0], w_proj_pool[0], w_pool[0]).astype(BF16)], axis=0)
    gw = _all_gather_slabs(pack)
    wout = gw[:, R_OUT:R_PA].reshape(D, D)
    wpa = gw[:, R_PA:R_PP].reshape(N_DEV, AW, 256).transpose(1, 0, 2).reshape(AW, D)
    wpp = gw[:, R_PP:R_PL].reshape(N_DEV, AW, 256).transpose(1, 0, 2).reshape(AW, D)
    wpool = gw[:, R_PL:].reshape(N_DEV, 4, 32, PG).transpose(1, 0, 2, 3).reshape(4, PG, PG)

    h = _proj_in(xb, gw)
    qkv, o_nat, lse_nat, o_dil, lse_dil = [], [], [], [], []
    for g, d in enumerate(DILATIONS):
        if d == 1:
            arrs, cols = (h, h, h), (COL_Q // AW, COL_K // AW, COL_V // AW)
        else:
            arrs = tuple(_dilate(h[:, c + g * AW:c + (g + 1) * AW], d) for c in (COL_Q, COL_K, COL_V))
            cols = (0, 0, 0)
        qkv.append((arrs, cols))
        og, lg = _attn_fwd(g, *arrs, *cols)
        o_dil.append(og)
        lse_dil.append(lg)
        o_nat.append(_undilate(og, d))
        lse_nat.append(_undilate(lg, d))
    ya, yp, merged, a, b, p = _mix_fwd(h, o_nat, lse_nat, wpa, wpp, wpool, pool_scale, b_gate)
    dr, drb, dm, loss_part, dgamma, dbeta = _out_ln(merged, x2, tgt, wout, ln_gamma, ln_beta)

    dgp, da, db, dbgate = _gate_bwd(dm, a, b, h, b_gate)
    dza, du, dzp, do0, do1, do2, dl0, dl1, dl2, dwpool, dpscale = _mix_bwd(
        da, db, h, o_nat, lse_nat, p, wpa, wpp, wpool, pool_scale)
    dq, dk, dv = [], [], []
    for g, (d, do_g, dl_g) in enumerate(zip(DILATIONS, (do0, do1, do2), (dl0, dl1, dl2))):
        arrs, cols = qkv[g]
        dqg, dkg, dvg = _attn_bwd(g, *arrs, *cols, _dilate(do_g, d), lse_dil[g], _dilate(dl_g, d))
        dq.append(_undilate(dqg, d))
        dk.append(_undilate(dkg, d))
        dv.append(_undilate(dvg, d))
    dh = jnp.concatenate(dq + dk + dv + [dza, du, dzp, dgp], axis=1)
    grad_x = _grad_x(dh, gw, dr)

    slab = lambda n, k: (n // 2, 0, n % 2)
    part = _grad_w("grad_w_in", xt, dh, jax.ShapeDtypeStruct((N_DEV, R_ALL, D), BF16),
                   pl.BlockSpec((None, D, 1024), slab), 1024, 1024)
    flat = lambda n, k: (0, n)
    d_wout = _grad_w("grad_w_out", merged.T, drb, jax.ShapeDtypeStruct((D, D), BF16),
                     pl.BlockSpec((D, 1024), flat), 1024, 1024)
    d_wpa = _grad_w("grad_w_pa", ya.T, da, jax.ShapeDtypeStruct((AW, D), BF16),
                    pl.BlockSpec((AW, 1024), flat), 1024, 1024)
    d_wpp = _grad_w("grad_w_pp", yp.T, db, jax.ShapeDtypeStruct((AW, D), BF16),
                    pl.BlockSpec((AW, 1024), flat), 1024, 1024)
    small = jnp.concatenate([
        d_wout.reshape(N_DEV, 256, D),
        d_wpa.reshape(AW, N_DEV, 256).transpose(1, 0, 2).reshape(N_DEV, -1, D),
        d_wpp.reshape(AW, N_DEV, 256).transpose(1, 0, 2).reshape(N_DEV, -1, D),
        dwpool.astype(BF16).reshape(4, N_DEV, 32, PG).transpose(1, 0, 2, 3).reshape(N_DEV, -1, D)], axis=1)
    part = lax.dynamic_update_slice(part, small, (0, R_OUT, 0))

    q = _pair_sum(coords, part, _rs_sibling(part))
    l2 = _rs_chips(q)
    g_in, d_in, m_in, v_in = _adam_shard("adam_w_in", q, l2, w_in[0], m_w_in[0], v_w_in[0], 256)
    outs_small = _adam_shard(
        "adam_small", q[:, R_OUT:], l2[:, R_OUT:],
        _pack_small(w_out[0], w_proj_attn[0], w_proj_pool[0], w_pool[0]),
        _pack_small(m_w_out[0], m_w_proj_attn[0], m_w_proj_pool[0], m_w_pool[0]),
        _pack_small(v_w_out[0], v_w_proj_attn[0], v_w_proj_pool[0], v_w_pool[0]), 176)
    small_parts = [_unpack_small(t) for t in outs_small]

    vec = _pack_vec(dbgate[0], dgamma[0], dbeta[0], dpscale[0], loss_part[0, 0])
    outs_vec = _adam_replicated(
        _all_gather_direct("ag_vec", vec),
        _pack_vec(b_gate[0], ln_gamma[0], ln_beta[0], pool_scale[0], 0.0),
        _pack_vec(m_b_gate[0], m_ln_gamma[0], m_ln_beta[0], m_pool_scale[0], 0.0),
        _pack_vec(v_b_gate[0], v_ln_gamma[0], v_ln_beta[0], v_pool_scale[0], 0.0))
    vec_parts = [_unpack_vec(t) for t in outs_vec]
    loss = outs_vec[0][5, 0]

    def leaves(kind, big):
        out, pa, pp, pool = small_parts[kind]
        bg, ps, gm, bt = vec_parts[kind]
        return [big[None], bg, pool[None], ps, pa[None], pp[None], out[None], gm, bt]

    return (loss, grad_x[None], *leaves(0, g_in), *leaves(1, d_in), *leaves(2, m_in), *leaves(3, v_in))
```

```python
import functools

import jax
import jax.numpy as jnp
from jax import lax
from jax.experimental import pallas as pl
from jax.experimental.pallas import tpu as pltpu

F32 = jnp.float32
BF16 = jnp.bfloat16

S = 4096
D = 2048
NW = 16384
AW = 1024
HD = 128
NH = 8
QB = 128
NBLK = S // QB
DILATIONS = (1, 4, 16)
POOL_WINDOWS = (2, 4, 8, 16)
PG = 256
N_DEV = 8
COL_Q, COL_K, COL_V = 0, 3 * AW, 6 * AW
COL_ZA, COL_U, COL_ZP, COL_G = 9 * AW, 10 * AW, 11 * AW, 12 * AW
ALPHA = 2.0 ** 0.25
LN_EPS = 1e-5
NEG_INF = -1e30
LR, B1, B2, EPS, WD, STEP = 0.001, 0.9, 0.999, 1e-08, 0.01, 10
R_IN, R_OUT, R_PA, R_PP, R_PL = 0, 2048, 2304, 2432, 2560
R_ALL = 2576
R_SMALL = R_ALL - R_OUT
VMEM_LIMIT = 56 * 1024 * 1024
MESH = pl.DeviceIdType.MESH
ANY = pl.BlockSpec(memory_space=pl.ANY)


def _cparams(n_axes):
    return pltpu.CompilerParams(dimension_semantics=("arbitrary",) * n_axes, vmem_limit_bytes=VMEM_LIMIT)


def _sigmoid(z):
    return 1.0 / (1.0 + jnp.exp(-z))


def _nt(a, b):
    return lax.dot_general(a, b, (((1,), (1,)), ((), ())), preferred_element_type=F32)


def _tn(a, b):
    return lax.dot_general(a, b, (((0,), (0,)), ((), ())), preferred_element_type=F32)


def _nn(a, b):
    return jnp.dot(a, b, preferred_element_type=F32)


def _lin(x, y, c):
    return 4 * x + 2 * y + c


def _flip(v, f):
    return 1 - v if f else v


def _exchange(name, src, plan, *, dst_shape=None, local_dst=None):
    n = len(plan)
    in_place = dst_shape is None
    out_sds = jax.ShapeDtypeStruct(src.shape, src.dtype) if in_place else dst_shape

    def body(src_ref, dst_ref, send_sems, recv_sems, local_sem):
        x, y, c = lax.axis_index("x"), lax.axis_index("y"), lax.axis_index("c")

        def copy(k, sender):
            flip, src_index, dst_index = plan[k]
            sx, sy, sc = sender
            to = (_flip(sx, flip[0]), _flip(sy, flip[1]), _flip(sc, flip[2]))
            s = src_ref if src_index is None else src_ref.at[src_index(sx, sy, sc)]
            return pltpu.make_async_remote_copy(
                src_ref=s, dst_ref=dst_ref.at[dst_index(sx, sy, sc)],
                send_sem=send_sems.at[k], recv_sem=recv_sems.at[k],
                device_id=to, device_id_type=MESH)

        me = (x, y, c)
        if local_dst is not None:
            mine = pltpu.make_async_copy(src_ref, dst_ref.at[local_dst(x, y, c)], local_sem)
            mine.start()
        sends = [copy(k, me) for k in range(n)]
        for cp in sends:
            cp.start()
        for k in range(n):
            flip = plan[k][0]
            copy(k, (_flip(x, flip[0]), _flip(y, flip[1]), _flip(c, flip[2]))).wait_recv()
        for cp in sends:
            cp.wait_send()
        if local_dst is not None:
            mine.wait()

    return pl.pallas_call(
        body, name=name, out_shape=out_sds, in_specs=[ANY], out_specs=ANY,
        input_output_aliases={0: 0} if in_place else {},
        scratch_shapes=[pltpu.SemaphoreType.DMA((n,)), pltpu.SemaphoreType.DMA((n,)),
                        pltpu.SemaphoreType.DMA(())],
    )(src)


FLIP_C, FLIP_X, FLIP_Y, FLIP_XY = (0, 0, 1), (1, 0, 0), (0, 1, 0), (1, 1, 0)
CHIP_FLIPS = ((0, 0), (1, 0), (0, 1), (1, 1))


def _arrival_order(x, y, c):
    chips = [(x, y), (1 - x, y), (x, 1 - y), (1 - x, 1 - y)]
    return jnp.stack([_lin(px, py, pc) for px, py in chips for pc in (c, 1 - c)]).astype(jnp.int32)


def _ag_proj(order, xb, pack):
    tm = 1024
    nrow = S // tm
    w_rows = pl.ds(R_IN, D)

    def body(order_ref, x_ref, pack_ref, h_ref, gw_ref, wbuf, wsem, send_sems, recv_sems, local_sem):
        s, i = pl.program_id(0), pl.program_id(1)
        x, y, c = lax.axis_index("x"), lax.axis_index("y"), lax.axis_index("c")
        me, sib = _lin(x, y, c), (x, y, 1 - c)
        chips = [(1 - x, y), (x, 1 - y), (1 - x, 1 - y)]

        def remote(src, slab, k, to):
            return pltpu.make_async_remote_copy(
                src_ref=src, dst_ref=gw_ref.at[slab], send_sem=send_sems.at[k], recv_sem=recv_sems.at[k],
                device_id=to, device_id_type=MESH)

        def peer(k):
            return sib if k == 0 else (*chips[k - 1], c)

        def mine_to(k):
            return remote(pack_ref, me, k, peer(k))

        def landed(k):
            return remote(pack_ref, _lin(*peer(k)), k, peer(k))

        def pass_on(j, core):
            slab = _lin(*chips[j], core)
            return remote(gw_ref.at[slab], slab, 4 + j, sib)

        local = pltpu.make_async_copy(pack_ref, gw_ref.at[me], local_sem)

        def fetch(src, slot):
            return pltpu.make_async_copy(src, wbuf.at[slot], wsem.at[slot])

        @pl.when((s == 0) & (i == 0))
        def _():
            local.start()
            for k in (1, 2, 3, 0):
                mine_to(k).start()
            first = fetch(pack_ref.at[w_rows], 0)
            first.start()
            first.wait()

        for nxt in range(1, N_DEV):
            @pl.when((s == nxt - 1) & (i == nrow - 1))
            def _(nxt=nxt):
                j = nxt // 2 - 1
                if nxt == 1:
                    landed(0).wait_recv()
                elif nxt % 2 == 0:
                    landed(1 + j).wait_recv()
                    pass_on(j, c).start()
                else:
                    pass_on(j, 1 - c).wait_recv()
                fetch(gw_ref.at[order_ref[nxt], w_rows], nxt % 2).start()

        for slot in (0, 1):
            @pl.when(s % 2 == slot)
            def _(slot=slot):
                @pl.when((i == 0) & (s > 0))
                def _():
                    fetch(pack_ref.at[w_rows], slot).wait()
                h_ref[...] = _nn(x_ref[...], wbuf[slot]).astype(h_ref.dtype)

        @pl.when((s == N_DEV - 1) & (i == nrow - 1))
        def _():
            for k in range(4):
                mine_to(k).wait_send()
            for j in range(3):
                pass_on(j, c).wait_send()
            local.wait()

    grid_spec = pltpu.PrefetchScalarGridSpec(
        num_scalar_prefetch=1, grid=(N_DEV, nrow),
        in_specs=[pl.BlockSpec((tm, D), lambda s, i, order: (i, 0)), ANY],
        out_specs=[pl.BlockSpec((tm, D), lambda s, i, order: (i, order[s])), ANY],
        scratch_shapes=[pltpu.VMEM((2, D, D), BF16), pltpu.SemaphoreType.DMA((2,)),
                        pltpu.SemaphoreType.DMA((7,)), pltpu.SemaphoreType.DMA((7,)), pltpu.SemaphoreType.DMA(())])
    return pl.pallas_call(
        body, name="ag_proj", grid_spec=grid_spec,
        out_shape=[jax.ShapeDtypeStruct((S, NW), BF16), jax.ShapeDtypeStruct((N_DEV, R_ALL, D), BF16)],
        compiler_params=_cparams(2))(order, xb, pack)


def _all_gather_direct(name, vec):
    own = lambda x, y, c: _lin(x, y, c)
    flips = [(fx, fy, fc) for fx in (0, 1) for fy in (0, 1) for fc in (0, 1) if (fx, fy, fc) != (0, 0, 0)]
    return _exchange(name, vec, [(f, None, own) for f in flips],
                     dst_shape=jax.ShapeDtypeStruct((N_DEV,) + vec.shape, vec.dtype), local_dst=own)


def _rs_sibling(p):
    plan = [(FLIP_C, (lambda x, y, c, f=f: _lin(_flip(x, f[0]), _flip(y, f[1]), 1 - c)),
             (lambda x, y, c, k=k: k)) for k, f in enumerate(CHIP_FLIPS)]
    return _exchange("rs_sibling", p, plan, dst_shape=jax.ShapeDtypeStruct((4,) + p.shape[1:], p.dtype))


def _pair_sum(coords, p, l1):
    tr = 368
    nr = R_ALL // tr

    def body(crd, p_ref, l_ref, q_ref):
        q_ref[...] = (p_ref[...].astype(F32) + l_ref[...].astype(F32)).astype(q_ref.dtype)

    def p_map(k, i, crd):
        fx, fy = k % 2, k // 2
        px = crd[0] + fx - 2 * fx * crd[0]
        py = crd[1] + fy - 2 * fy * crd[1]
        return (_lin(px, py, crd[2]), i, 0)

    grid_spec = pltpu.PrefetchScalarGridSpec(
        num_scalar_prefetch=1, grid=(4, nr),
        in_specs=[pl.BlockSpec((None, tr, D), p_map),
                  pl.BlockSpec((None, tr, D), lambda k, i, crd: (k, i, 0))],
        out_specs=pl.BlockSpec((None, tr, D), lambda k, i, crd: (k, i, 0)))
    return pl.pallas_call(body, name="pair_sum", grid_spec=grid_spec,
                          out_shape=jax.ShapeDtypeStruct((4, R_ALL, D), BF16),
                          compiler_params=_cparams(2))(coords, p, l1)


def _grad_x_rs(dh, g, dr, q):
    tm, tk = 512, 1024
    ni, nk = S // tm, NW // tk

    def body(dh_ref, w_ref, dr_ref, q_ref, o_ref, l2_ref, send_sems, recv_sems):
        i, k = pl.program_id(0), pl.program_id(1)
        x, y, c = lax.axis_index("x"), lax.axis_index("y"), lax.axis_index("c")

        def copy(j):
            f = CHIP_FLIPS[j + 1]
            return pltpu.make_async_remote_copy(
                src_ref=q_ref.at[j + 1], dst_ref=l2_ref.at[j], send_sem=send_sems.at[j], recv_sem=recv_sems.at[j],
                device_id=(_flip(x, f[0]), _flip(y, f[1]), c), device_id_type=MESH)

        @pl.when((i == 0) & (k == 0))
        def _():
            for j in range(3):
                copy(j).start()

        @pl.when(k == 0)
        def _():
            o_ref[...] = ALPHA * dr_ref[...]

        o_ref[...] += _nt(dh_ref[...], w_ref[...])

        @pl.when((i == ni - 1) & (k == nk - 1))
        def _():
            for j in range(3):
                copy(j).wait_recv()
            for j in range(3):
                copy(j).wait_send()

    return pl.pallas_call(
        body, name="grad_x_rs", grid=(ni, nk),
        in_specs=[pl.BlockSpec((tm, tk), lambda i, k: (i, k)),
                  pl.BlockSpec((None, D, tk), lambda i, k: (k // 2, 0, k % 2)),
                  pl.BlockSpec((tm, D), lambda i, k: (i, 0)), ANY],
        out_specs=[pl.BlockSpec((tm, D), lambda i, k: (i, 0)), ANY],
        out_shape=[jax.ShapeDtypeStruct((S, D), F32), jax.ShapeDtypeStruct((3,) + q.shape[1:], q.dtype)],
        scratch_shapes=[pltpu.SemaphoreType.DMA((3,)), pltpu.SemaphoreType.DMA((3,))],
        compiler_params=_cparams(2))(dh, g, dr, q)


def _grad_w(name, at, b, out_shape, out_spec, tn, tk):
    m, k_all = at.shape
    n_all = b.shape[1]
    nk = k_all // tk

    def body(a_ref, b_ref, o_ref, acc_ref):
        k = pl.program_id(1)

        @pl.when(k == 0)
        def _():
            acc_ref[...] = jnp.zeros_like(acc_ref)

        acc_ref[...] += _nn(a_ref[...], b_ref[...])

        @pl.when(k == nk - 1)
        def _():
            o_ref[...] = acc_ref[...].astype(o_ref.dtype)

    return pl.pallas_call(
        body, name=name, grid=(n_all // tn, nk),
        in_specs=[pl.BlockSpec((m, tk), lambda n, k: (0, k)),
                  pl.BlockSpec((tk, tn), lambda n, k: (k, n))],
        out_specs=out_spec, out_shape=out_shape,
        scratch_shapes=[pltpu.VMEM((m, tn), F32)], compiler_params=_cparams(2))(at, b)


def _attn_mask(n):
    qi = lax.broadcasted_iota(jnp.int32, (QB, 2 * QB), 0)
    kj = lax.broadcasted_iota(jnp.int32, (QB, 2 * QB), 1)
    dist = QB + qi - kj
    return (dist >= 0) & (dist <= QB) & ((kj >= QB) | (n > 0))


def _qkv_specs(qc, kc, vc, clamp):
    cur = lambda col: pl.BlockSpec((QB, AW), lambda b: (clamp(b), col))
    prev = lambda col: pl.BlockSpec((QB, AW), lambda b: (jnp.maximum(clamp(b) - 1, 0), col))
    return [cur(qc), cur(kc), prev(kc), cur(vc), prev(vc)]


def _attn_fwd(g, q, k, v, qc, kc, vc):
    nb = NBLK // DILATIONS[g]
    scale = HD ** -0.5

    def body(q_ref, kc_ref, kp_ref, vc_ref, vp_ref, o_ref, l_ref):
        valid = _attn_mask(pl.program_id(0) % nb)
        for h in range(NH):
            sl = slice(h * HD, (h + 1) * HD)
            kh = jnp.concatenate([kp_ref[:, sl], kc_ref[:, sl]], axis=0)
            vh = jnp.concatenate([vp_ref[:, sl], vc_ref[:, sl]], axis=0)
            s = jnp.where(valid, _nt(q_ref[:, sl], kh) * scale, NEG_INF)
            m = jnp.max(s, axis=-1, keepdims=True)
            e = jnp.exp(s - m)
            den = jnp.sum(e, axis=-1, keepdims=True)
            o_ref[:, sl] = (_nn(e.astype(BF16), vh) / den).astype(o_ref.dtype)
            l_ref[:, h:h + 1] = m + jnp.log(den)

    return pl.pallas_call(
        body, name=f"attn_fwd_{g}", grid=(NBLK,),
        in_specs=_qkv_specs(qc, kc, vc, lambda b: b),
        out_specs=[pl.BlockSpec((QB, AW), lambda b: (b, 0)), pl.BlockSpec((QB, NH), lambda b: (b, 0))],
        out_shape=[jax.ShapeDtypeStruct((S, AW), BF16), jax.ShapeDtypeStruct((S, NH), F32)],
        compiler_params=_cparams(1))(q, k, k, v, v)


def _attn_bwd(g, q, k, v, qc, kc, vc, do, lse, delta):
    nb = NBLK // DILATIONS[g]
    scale = HD ** -0.5
    last = NBLK - 1
    clamp = lambda b: jnp.minimum(b, last)
    row = lambda width: pl.BlockSpec((QB, width), lambda b: (clamp(b), 0))
    behind = pl.BlockSpec((QB, AW), lambda b: (jnp.maximum(b - 1, 0), 0))

    def body(q_ref, kc_ref, kp_ref, vc_ref, vp_ref, do_ref, l_ref, dl_ref, dq_ref, dk_ref, dv_ref, ck_ref, cv_ref):
        b = pl.program_id(0)

        @pl.when(b == 0)
        def _():
            ck_ref[...] = jnp.zeros_like(ck_ref)
            cv_ref[...] = jnp.zeros_like(cv_ref)

        @pl.when(b <= last)
        def _():
            valid = _attn_mask(b % nb)
            for h in range(NH):
                sl = slice(h * HD, (h + 1) * HD)
                qh, doh = q_ref[:, sl], do_ref[:, sl]
                kh = jnp.concatenate([kp_ref[:, sl], kc_ref[:, sl]], axis=0)
                vh = jnp.concatenate([vp_ref[:, sl], vc_ref[:, sl]], axis=0)
                s = _nt(qh, kh) * scale
                p = jnp.where(valid, jnp.exp(s - l_ref[:, h:h + 1]), 0.0)
                ds = p * (_nt(doh, vh) - dl_ref[:, h:h + 1])
                dsb = (ds * scale).astype(BF16)
                dq_ref[:, sl] = _nn(dsb, kh).astype(dq_ref.dtype)
                dk2 = _tn(dsb, qh)
                dv2 = _tn(p.astype(BF16), doh)
                dk_ref[:, sl] = (ck_ref[:, sl] + dk2[:QB]).astype(dk_ref.dtype)
                dv_ref[:, sl] = (cv_ref[:, sl] + dv2[:QB]).astype(dv_ref.dtype)
                ck_ref[:, sl] = dk2[QB:]
                cv_ref[:, sl] = dv2[QB:]

        @pl.when(b > last)
        def _():
            dk_ref[...] = ck_ref[...].astype(dk_ref.dtype)
            dv_ref[...] = cv_ref[...].astype(dv_ref.dtype)

    return pl.pallas_call(
        body, name=f"attn_bwd_{g}", grid=(NBLK + 1,),
        in_specs=_qkv_specs(qc, kc, vc, clamp) + [row(AW), row(NH), row(NH)],
        out_specs=[row(AW), behind, behind],
        out_shape=[jax.ShapeDtypeStruct((S, AW), BF16)] * 3,
        scratch_shapes=[pltpu.VMEM((QB, AW), F32), pltpu.VMEM((QB, AW), F32)],
        compiler_params=_cparams(1))(q, k, k, v, v, do, lse, delta)


TM = 256
NT = S // TM


def _group_weights(l0, l1, l2):
    m = jnp.maximum(jnp.maximum(l0, l1), l2)
    e0, e1, e2 = jnp.exp(l0 - m), jnp.exp(l1 - m), jnp.exp(l2 - m)
    inv = 1.0 / (e0 + e1 + e2)
    return e0 * inv, e1 * inv, e2 * inv


def _window_count(tile, w):
    t = tile * TM + lax.broadcasted_iota(jnp.int32, (TM, 1), 0)
    return jnp.minimum(t + 1, w).astype(F32)


def _pool_tokens(u_prev, u_cur, tile):
    ext = jnp.concatenate([jnp.where(tile > 0, u_prev, 0.0), u_cur], axis=0)
    out = []
    for g, w in enumerate(POOL_WINDOWS):
        s = ext[:, g * PG:(g + 1) * PG]
        span = 1
        while span < w:
            s = s + pltpu.roll(s, span, 0)
            span *= 2
        out.append(s[TM:] / _window_count(tile, w) - u_cur[:, g * PG:(g + 1) * PG])
    return jnp.concatenate(out, axis=1)


def _pool_tokens_bwd(dp, dpc_next, tile):
    du, dpc_all = [], []
    for g, w in enumerate(POOL_WINDOWS):
        sl = slice(g * PG, (g + 1) * PG)
        dpc = dp[:, sl] / _window_count(tile, w)
        s = jnp.concatenate([dpc, dpc_next[:, sl]], axis=0)
        span = 1
        while span < w:
            s = s + pltpu.roll(s, 2 * TM - span, 0)
            span *= 2
        du.append(s[:TM] - dp[:, sl])
        dpc_all.append(dpc)
    return jnp.concatenate(du, axis=1), jnp.concatenate(dpc_all, axis=1)


def _pool_linear(pb, wpool_ref):
    return jnp.concatenate([_nn(pb[:, g * PG:(g + 1) * PG], wpool_ref[g]) for g in range(len(POOL_WINDOWS))], axis=1)


def _tok(width, col=0, rev=False):
    if rev:
        return pl.BlockSpec((TM, width), lambda i: (NT - 1 - i, col))
    return pl.BlockSpec((TM, width), lambda i: (i, col))


def _whole(shape):
    return pl.BlockSpec(shape, lambda i: (0,) * len(shape))


def _mix_fwd(h, o, lse, wpa, wpp, wpool, pscale, bgate):
    def body(o0_ref, o1_ref, o2_ref, l0_ref, l1_ref, l2_ref, za_ref, uc_ref, up_ref, zp_ref, gp_ref,
             wpa_ref, wpp_ref, wpool_ref, ps_ref, bg_ref, ya_ref, yp_ref, mg_ref, a_ref, b_ref, p_ref):
        i = pl.program_id(0)
        w0, w1, w2 = _group_weights(l0_ref[...], l1_ref[...], l2_ref[...])
        za = za_ref[...].astype(F32)
        silu_a = za * _sigmoid(za)
        for hh in range(NH):
            sl = slice(hh * HD, (hh + 1) * HD)
            c = slice(hh, hh + 1)
            oh = (w0[:, c] * o0_ref[:, sl].astype(F32) + w1[:, c] * o1_ref[:, sl].astype(F32)
                  + w2[:, c] * o2_ref[:, sl].astype(F32))
            ya_ref[:, sl] = (oh * silu_a[:, sl]).astype(BF16)
        p_ref[...] = _pool_tokens(up_ref[...].astype(F32), uc_ref[...].astype(F32), i).astype(BF16)
        zp = zp_ref[...].astype(F32)
        yp_ref[...] = (_pool_linear(p_ref[...], wpool_ref) * ps_ref[...] * (zp * _sigmoid(zp))).astype(BF16)
        a = _nn(ya_ref[...], wpa_ref[...])
        b = _nn(yp_ref[...], wpp_ref[...])
        a_ref[...] = a.astype(BF16)
        b_ref[...] = b.astype(BF16)
        gates = _sigmoid(gp_ref[...].astype(F32) + bg_ref[...])
        mg_ref[...] = (gates[:, :D] * a + gates[:, D:] * b).astype(BF16)

    u_prev = pl.BlockSpec((TM, AW), lambda i: (jnp.maximum(i - 1, 0), COL_U // AW))
    return pl.pallas_call(
        body, name="mix_fwd", grid=(NT,),
        in_specs=[_tok(AW)] * 3 + [_tok(NH)] * 3
        + [_tok(AW, COL_ZA // AW), _tok(AW, COL_U // AW), u_prev, _tok(AW, COL_ZP // AW), _tok(2 * D, COL_G // (2 * D))]
        + [_whole((AW, D)), _whole((AW, D)), _whole((4, PG, PG)), _whole((1, AW)), _whole((1, 2 * D))],
        out_specs=[_tok(AW), _tok(AW), _tok(D), _tok(D), _tok(D), _tok(AW)],
        out_shape=[jax.ShapeDtypeStruct((S, AW), BF16)] * 2 + [jax.ShapeDtypeStruct((S, D), BF16)] * 3
        + [jax.ShapeDtypeStruct((S, AW), BF16)],
        compiler_params=_cparams(1))(*o, *lse, h, h, h, h, h, wpa, wpp, wpool, pscale, bgate)


def _out_ln(merged, x, target, wout, gamma, beta):
    def body(mg_ref, x_ref, t_ref, w_ref, g_ref, b_ref, dr_ref, drb_ref, dm_ref, loss_ref, dg_ref, db_ref):
        i = pl.program_id(0)

        @pl.when(i == 0)
        def _():
            loss_ref[...] = jnp.zeros_like(loss_ref)
            dg_ref[...] = jnp.zeros_like(dg_ref)
            db_ref[...] = jnp.zeros_like(db_ref)

        r = ALPHA * x_ref[...] + _nn(mg_ref[...], w_ref[...])
        mu = jnp.mean(r, axis=-1, keepdims=True)
        rc = r - mu
        rstd = lax.rsqrt(jnp.mean(rc * rc, axis=-1, keepdims=True) + LN_EPS)
        xhat = rc * rstd
        err = xhat * g_ref[...] + b_ref[...] - t_ref[...]
        loss_ref[...] += 0.5 * jnp.sum(jnp.mean(err * err, axis=-1, keepdims=True), axis=0, keepdims=True)
        dy = err * (1.0 / D)
        dg_ref[...] += jnp.sum(dy * xhat, axis=0, keepdims=True)
        db_ref[...] += jnp.sum(dy, axis=0, keepdims=True)
        dxh = dy * g_ref[...]
        dr = rstd * (dxh - jnp.mean(dxh, axis=-1, keepdims=True)
                     - xhat * jnp.mean(dxh * xhat, axis=-1, keepdims=True))
        dr_ref[...] = dr
        drb_ref[...] = dr.astype(BF16)
        dm_ref[...] = _nt(drb_ref[...], w_ref[...]).astype(BF16)

    return pl.pallas_call(
        body, name="out_ln", grid=(NT,),
        in_specs=[_tok(D), _tok(D), _tok(D), _whole((D, D)), _whole((1, D)), _whole((1, D))],
        out_specs=[_tok(D), _tok(D), _tok(D), _whole((8, 128)), _whole((1, D)), _whole((1, D))],
        out_shape=[jax.ShapeDtypeStruct((S, D), F32), jax.ShapeDtypeStruct((S, D), BF16),
                   jax.ShapeDtypeStruct((S, D), BF16), jax.ShapeDtypeStruct((8, 128), F32),
                   jax.ShapeDtypeStruct((1, D), F32), jax.ShapeDtypeStruct((1, D), F32)],
        compiler_params=_cparams(1))(merged, x, target, wout, gamma, beta)


def _gate_bwd(dm, a, b, h, bgate):
    def body(dm_ref, a_ref, b_ref, gp_ref, bg_ref, dgp_ref, da_ref, db_ref, dbg_ref):
        @pl.when(pl.program_id(0) == 0)
        def _():
            dbg_ref[...] = jnp.zeros_like(dbg_ref)

        dm_ = dm_ref[...].astype(F32)
        gates = _sigmoid(gp_ref[...].astype(F32) + bg_ref[...])
        ga, gb = gates[:, :D], gates[:, D:]
        da_ref[...] = (dm_ * ga).astype(BF16)
        db_ref[...] = (dm_ * gb).astype(BF16)
        dgp = jnp.concatenate([dm_ * a_ref[...].astype(F32) * ga * (1.0 - ga),
                               dm_ * b_ref[...].astype(F32) * gb * (1.0 - gb)], axis=1)
        dgp_ref[...] = dgp.astype(BF16)
        dbg_ref[...] += jnp.sum(dgp, axis=0, keepdims=True)

    return pl.pallas_call(
        body, name="gate_bwd", grid=(NT,),
        in_specs=[_tok(D), _tok(D), _tok(D), _tok(2 * D, COL_G // (2 * D)), _whole((1, 2 * D))],
        out_specs=[_tok(2 * D), _tok(D), _tok(D), _whole((1, 2 * D))],
        out_shape=[jax.ShapeDtypeStruct((S, 2 * D), BF16), jax.ShapeDtypeStruct((S, D), BF16),
                   jax.ShapeDtypeStruct((S, D), BF16), jax.ShapeDtypeStruct((1, 2 * D), F32)],
        compiler_params=_cparams(1))(dm, a, b, h, bgate)


def _mix_bwd(da, db, h, o, lse, p, wpa, wpp, wpool, pscale):
    def body(da_ref, db_ref, o0_ref, o1_ref, o2_ref, l0_ref, l1_ref, l2_ref, za_ref, zp_ref, p_ref,
             wpa_ref, wpp_ref, wpool_ref, ps_ref,
             dza_ref, du_ref, dzp_ref, do0_ref, do1_ref, do2_ref, dl0_ref, dl1_ref, dl2_ref, dwp_ref, dps_ref,
             nxt_ref):
        i = pl.program_id(0)
        tile = NT - 1 - i

        @pl.when(i == 0)
        def _():
            nxt_ref[...] = jnp.zeros_like(nxt_ref)
            dwp_ref[...] = jnp.zeros_like(dwp_ref)
            dps_ref[...] = jnp.zeros_like(dps_ref)

        dya = _nt(da_ref[...], wpa_ref[...])
        w0, w1, w2 = _group_weights(l0_ref[...], l1_ref[...], l2_ref[...])
        za = za_ref[...].astype(F32)
        sig = _sigmoid(za)
        silu_a = za * sig
        dsilu_a = sig * (1.0 + za * (1.0 - sig))
        for hh in range(NH):
            sl = slice(hh * HD, (hh + 1) * HD)
            c = slice(hh, hh + 1)
            oh = (w0[:, c] * o0_ref[:, sl].astype(F32) + w1[:, c] * o1_ref[:, sl].astype(F32)
                  + w2[:, c] * o2_ref[:, sl].astype(F32))
            doh = dya[:, sl] * silu_a[:, sl]
            dza_ref[:, sl] = (dya[:, sl] * oh * dsilu_a[:, sl]).astype(BF16)
            dot_ = jnp.sum(doh * oh, axis=-1, keepdims=True)
            do0_ref[:, sl] = (w0[:, c] * doh).astype(BF16)
            do1_ref[:, sl] = (w1[:, c] * doh).astype(BF16)
            do2_ref[:, sl] = (w2[:, c] * doh).astype(BF16)
            dl0_ref[:, c] = w0[:, c] * dot_
            dl1_ref[:, c] = w1[:, c] * dot_
            dl2_ref[:, c] = w2[:, c] * dot_
        dyp = _nt(db_ref[...], wpp_ref[...])
        pb = p_ref[...]
        pw = _pool_linear(pb, wpool_ref)
        zp = zp_ref[...].astype(F32)
        sigp = _sigmoid(zp)
        dypre = dyp * (zp * sigp)
        dzp_ref[...] = (dyp * (pw * ps_ref[...]) * (sigp * (1.0 + zp * (1.0 - sigp)))).astype(BF16)
        dps_ref[...] += jnp.sum(dypre * pw, axis=0, keepdims=True)
        dpw = (dypre * ps_ref[...]).astype(BF16)
        dp = []
        for g in range(len(POOL_WINDOWS)):
            sl = slice(g * PG, (g + 1) * PG)
            dwp_ref[g] += _tn(pb[:, sl], dpw[:, sl])
            dp.append(_nt(dpw[:, sl], wpool_ref[g]))
        du, dpc = _pool_tokens_bwd(jnp.concatenate(dp, axis=1), nxt_ref[...], tile)
        du_ref[...] = du.astype(BF16)
        nxt_ref[...] = dpc

    r = functools.partial(_tok, rev=True)
    return pl.pallas_call(
        body, name="mix_bwd", grid=(NT,),
        in_specs=[r(D), r(D)] + [r(AW)] * 3 + [r(NH)] * 3 + [r(AW, COL_ZA // AW), r(AW, COL_ZP // AW), r(AW)]
        + [_whole((AW, D)), _whole((AW, D)), _whole((4, PG, PG)), _whole((1, AW))],
        out_specs=[r(AW)] * 6 + [r(NH)] * 3 + [_whole((4, PG, PG)), _whole((1, AW))],
        out_shape=[jax.ShapeDtypeStruct((S, AW), BF16)] * 6 + [jax.ShapeDtypeStruct((S, NH), F32)] * 3
        + [jax.ShapeDtypeStruct((4, PG, PG), F32), jax.ShapeDtypeStruct((1, AW), F32)],
        scratch_shapes=[pltpu.VMEM((TM, AW), F32)],
        compiler_params=_cparams(1))(da, db, *o, *lse, h, h, p, wpa, wpp, wpool, pscale)


def _adamw(w, g, m, v):
    m = B1 * m + (1.0 - B1) * g
    v = B2 * v + (1.0 - B2) * jnp.square(g)
    m_hat = m / (1.0 - B1 ** STEP)
    v_hat = v / (1.0 - B2 ** STEP)
    return -LR * (m_hat / (jnp.sqrt(v_hat) + EPS) + WD * w), m, v


def _adam_shard(name, q, l2, w, m, v, tr):
    rows = w.shape[0]

    def body(q_ref, l_ref, w_ref, m_ref, v_ref, g_out, d_out, m_out, v_out):
        g = q_ref[...].astype(F32)
        for k in range(3):
            g = g + l_ref[k].astype(F32)
        g_out[...] = g
        d_out[...], m_out[...], v_out[...] = _adamw(w_ref[...], g, m_ref[...], v_ref[...])

    blk = pl.BlockSpec((tr, D), lambda i: (i, 0))
    return pl.pallas_call(
        body, name=name, grid=(rows // tr,),
        in_specs=[pl.BlockSpec((None, tr, D), lambda i: (0, i, 0)), pl.BlockSpec((3, tr, D), lambda i: (0, i, 0)),
                  blk, blk, blk],
        out_specs=[blk] * 4, out_shape=[jax.ShapeDtypeStruct((rows, D), F32)] * 4,
        compiler_params=_cparams(1))(q, l2, w, m, v)


def _adam_replicated(gathered, w, m, v):
    def body(g_ref, w_ref, m_ref, v_ref, g_out, d_out, m_out, v_out):
        g = g_ref[0]
        for k in range(1, N_DEV):
            g = g + g_ref[k]
        g_out[...] = g
        d_out[...], m_out[...], v_out[...] = _adamw(w_ref[...], g, m_ref[...], v_ref[...])

    return pl.pallas_call(body, name="adam_replicated", out_shape=[jax.ShapeDtypeStruct((8, D), F32)] * 4,
                          compiler_params=pltpu.CompilerParams(vmem_limit_bytes=VMEM_LIMIT))(gathered, w, m, v)


def _dilate(a, d):
    if d == 1:
        return a
    return a.reshape(S // d, d, a.shape[1]).transpose(1, 0, 2).reshape(S, a.shape[1])


def _undilate(a, d):
    if d == 1:
        return a
    return a.reshape(d, S // d, a.shape[1]).transpose(1, 0, 2).reshape(S, a.shape[1])


def _pack_small(w_out, w_pa, w_pp, w_pool):
    return jnp.concatenate([w_out, w_pa.reshape(-1, D), w_pp.reshape(-1, D), w_pool.reshape(-1, D)], axis=0)


def _unpack_small(a):
    o = R_OUT
    return (a[:R_PA - o], a[R_PA - o:R_PP - o].reshape(AW, 256), a[R_PP - o:R_PL - o].reshape(AW, 256),
            a[R_PL - o:].reshape(4, 32, PG))


def _pack_vec(b_gate, gamma, beta, pscale, extra):
    z = jnp.zeros((D,), F32)
    return jnp.stack([b_gate[:D], b_gate[D:], gamma, beta, jnp.concatenate([pscale, z[:D - AW]]),
                      jnp.broadcast_to(extra, (D,)), z, z])


def _unpack_vec(a):
    return jnp.concatenate([a[0], a[1]])[None], a[4, :AW][None], a[2][None], a[3][None]


def kernel(x, w_in, b_gate, w_pool, pool_scale, w_proj_attn, w_proj_pool, w_out, ln_gamma, ln_beta, loss_target, m_w_in, m_b_gate, m_w_pool, m_pool_scale, m_w_proj_attn, m_w_proj_pool, m_w_out, m_ln_gamma, m_ln_beta, v_w_in, v_b_gate, v_w_pool, v_pool_scale, v_w_proj_attn, v_w_proj_pool, v_w_out, v_ln_gamma, v_ln_beta):
    coords = jnp.stack([lax.axis_index("x"), lax.axis_index("y"), lax.axis_index("c")]).astype(jnp.int32)
    x2, tgt = x[0], loss_target[0]
    xb = x2.astype(BF16)
    xt = x2.T.astype(BF16)

    pack = jnp.concatenate([w_in[0].astype(BF16),
                            _pack_small(w_out[0], w_proj_attn[0], w_proj_pool[0], w_pool[0]).astype(BF16)], axis=0)
    h, gw = _ag_proj(_arrival_order(*coords), xb, pack)
    wout = gw[:, R_OUT:R_PA].reshape(D, D)
    wpa = gw[:, R_PA:R_PP].reshape(N_DEV, AW, 256).transpose(1, 0, 2).reshape(AW, D)
    wpp = gw[:, R_PP:R_PL].reshape(N_DEV, AW, 256).transpose(1, 0, 2).reshape(AW, D)
    wpool = gw[:, R_PL:].reshape(N_DEV, 4, 32, PG).transpose(1, 0, 2, 3).reshape(4, PG, PG)

    qkv, o_nat, lse_nat, o_dil, lse_dil = [], [], [], [], []
    for g, d in enumerate(DILATIONS):
        if d == 1:
            arrs, cols = (h, h, h), (COL_Q // AW, COL_K // AW, COL_V // AW)
        else:
            arrs = tuple(_dilate(h[:, c + g * AW:c + (g + 1) * AW], d) for c in (COL_Q, COL_K, COL_V))
            cols = (0, 0, 0)
        qkv.append((arrs, cols))
        og, lg = _attn_fwd(g, *arrs, *cols)
        o_dil.append(og)
        lse_dil.append(lg)
        o_nat.append(_undilate(og, d))
        lse_nat.append(_undilate(lg, d))
    ya, yp, merged, a, b, p = _mix_fwd(h, o_nat, lse_nat, wpa, wpp, wpool, pool_scale, b_gate)
    dr, drb, dm, loss_part, dgamma, dbeta = _out_ln(merged, x2, tgt, wout, ln_gamma, ln_beta)

    dgp, da, db, dbgate = _gate_bwd(dm, a, b, h, b_gate)
    dza, du, dzp, do0, do1, do2, dl0, dl1, dl2, dwpool, dpscale = _mix_bwd(
        da, db, h, o_nat, lse_nat, p, wpa, wpp, wpool, pool_scale)
    dq, dk, dv = [], [], []
    for g, (d, do_g, dl_g) in enumerate(zip(DILATIONS, (do0, do1, do2), (dl0, dl1, dl2))):
        arrs, cols = qkv[g]
        dqg, dkg, dvg = _attn_bwd(g, *arrs, *cols, _dilate(do_g, d), lse_dil[g], _dilate(dl_g, d))
        dq.append(_undilate(dqg, d))
        dk.append(_undilate(dkg, d))
        dv.append(_undilate(dvg, d))
    dh = jnp.concatenate(dq + dk + dv + [dza, du, dzp, dgp], axis=1)

    slab = lambda n, k: (n // 2, 0, n % 2)
    part = _grad_w("grad_w_in", xt, dh, jax.ShapeDtypeStruct((N_DEV, R_ALL, D), BF16),
                   pl.BlockSpec((None, D, 1024), slab), 1024, 1024)
    flat = lambda n, k: (0, n)
    d_wout = _grad_w("grad_w_out", merged.T, drb, jax.ShapeDtypeStruct((D, D), BF16),
                     pl.BlockSpec((D, 1024), flat), 1024, 1024)
    d_wpa = _grad_w("grad_w_pa", ya.T, da, jax.ShapeDtypeStruct((AW, D), BF16),
                    pl.BlockSpec((AW, 1024), flat), 1024, 1024)
    d_wpp = _grad_w("grad_w_pp", yp.T, db, jax.ShapeDtypeStruct((AW, D), BF16),
                    pl.BlockSpec((AW, 1024), flat), 1024, 1024)
    small = jnp.concatenate([
        d_wout.reshape(N_DEV, 256, D),
        d_wpa.reshape(AW, N_DEV, 256).transpose(1, 0, 2).reshape(N_DEV, -1, D),
        d_wpp.reshape(AW, N_DEV, 256).transpose(1, 0, 2).reshape(N_DEV, -1, D),
        dwpool.astype(BF16).reshape(4, N_DEV, 32, PG).transpose(1, 0, 2, 3).reshape(N_DEV, -1, D)], axis=1)
    part = lax.dynamic_update_slice(part, small, (0, R_OUT, 0))

    q = _pair_sum(coords, part, _rs_sibling(part))
    grad_x, l2 = _grad_x_rs(dh, gw, dr, q)
    g_in, d_in, m_in, v_in = _adam_shard("adam_w_in", q, l2, w_in[0], m_w_in[0], v_w_in[0], 256)
    outs_small = _adam_shard(
        "adam_small", q[:, R_OUT:], l2[:, R_OUT:],
        _pack_small(w_out[0], w_proj_attn[0], w_proj_pool[0], w_pool[0]),
        _pack_small(m_w_out[0], m_w_proj_attn[0], m_w_proj_pool[0], m_w_pool[0]),
        _pack_small(v_w_out[0], v_w_proj_attn[0], v_w_proj_pool[0], v_w_pool[0]), 176)
    small_parts = [_unpack_small(t) for t in outs_small]

    vec = _pack_vec(dbgate[0], dgamma[0], dbeta[0], dpscale[0], loss_part[0, 0])
    outs_vec = _adam_replicated(
        _all_gather_direct("ag_vec", vec),
        _pack_vec(b_gate[0], ln_gamma[0], ln_beta[0], pool_scale[0], 0.0),
        _pack_vec(m_b_gate[0], m_ln_gamma[0], m_ln_beta[0], m_pool_scale[0], 0.0),
        _pack_vec(v_b_gate[0], v_ln_gamma[0], v_ln_beta[0], v_pool_scale[0], 0.0))
    vec_parts = [_unpack_vec(t) for t in outs_vec]
    loss = outs_vec[0][5, 0]

    def leaves(kind, big):
        out, pa, pp, pool = small_parts[kind]
        bg, ps, gm, bt = vec_parts[kind]
        return [big[None], bg, pool[None], ps, pa[None], pp[None], out[None], gm, bt]

    return (loss, grad_x[None], *leaves(0, g_in), *leaves(1, d_in), *leaves(2, m_in), *leaves(3, v_in))
```

```python
import functools

import jax
import jax.numpy as jnp
from jax import lax
from jax.experimental import pallas as pl
from jax.experimental.pallas import tpu as pltpu

F32 = jnp.float32
BF16 = jnp.bfloat16

S = 4096
D = 2048
NW = 16384
AW = 1024
HD = 128
NH = 8
QB = 128
NBLK = S // QB
DILATIONS = (1, 4, 16)
POOL_WINDOWS = (2, 4, 8, 16)
PG = 256
N_DEV = 8
COL_Q, COL_K, COL_V = 0, 3 * AW, 6 * AW
COL_ZA, COL_U, COL_ZP, COL_G = 9 * AW, 10 * AW, 11 * AW, 12 * AW
ALPHA = 2.0 ** 0.25
LN_EPS = 1e-5
NEG_INF = -1e30
LR, B1, B2, EPS, WD, STEP = 0.001, 0.9, 0.999, 1e-08, 0.01, 10
R_IN, R_OUT, R_PA, R_PP, R_PL = 0, 2048, 2304, 2432, 2560
R_ALL = 2576
R_SMALL = R_ALL - R_OUT
VMEM_LIMIT = 56 * 1024 * 1024
MESH = pl.DeviceIdType.MESH
ANY = pl.BlockSpec(memory_space=pl.ANY)


def _cparams(n_axes):
    return pltpu.CompilerParams(dimension_semantics=("arbitrary",) * n_axes, vmem_limit_bytes=VMEM_LIMIT)


def _sigmoid(z):
    return 1.0 / (1.0 + jnp.exp(-z))


def _nt(a, b):
    return lax.dot_general(a, b, (((1,), (1,)), ((), ())), preferred_element_type=F32)


def _tn(a, b):
    return lax.dot_general(a, b, (((0,), (0,)), ((), ())), preferred_element_type=F32)


def _nn(a, b):
    return jnp.dot(a, b, preferred_element_type=F32)


def _lin(x, y, c):
    return 4 * x + 2 * y + c


def _flip(v, f):
    return 1 - v if f else v


def _exchange(name, src, plan, *, dst_shape=None, local_dst=None):
    n = len(plan)
    in_place = dst_shape is None
    out_sds = jax.ShapeDtypeStruct(src.shape, src.dtype) if in_place else dst_shape

    def body(src_ref, dst_ref, send_sems, recv_sems, local_sem):
        x, y, c = lax.axis_index("x"), lax.axis_index("y"), lax.axis_index("c")

        def copy(k, sender):
            flip, src_index, dst_index = plan[k]
            sx, sy, sc = sender
            to = (_flip(sx, flip[0]), _flip(sy, flip[1]), _flip(sc, flip[2]))
            s = src_ref if src_index is None else src_ref.at[src_index(sx, sy, sc)]
            return pltpu.make_async_remote_copy(
                src_ref=s, dst_ref=dst_ref.at[dst_index(sx, sy, sc)],
                send_sem=send_sems.at[k], recv_sem=recv_sems.at[k],
                device_id=to, device_id_type=MESH)

        me = (x, y, c)
        if local_dst is not None:
            mine = pltpu.make_async_copy(src_ref, dst_ref.at[local_dst(x, y, c)], local_sem)
            mine.start()
        sends = [copy(k, me) for k in range(n)]
        for cp in sends:
            cp.start()
        for k in range(n):
            flip = plan[k][0]
            copy(k, (_flip(x, flip[0]), _flip(y, flip[1]), _flip(c, flip[2]))).wait_recv()
        for cp in sends:
            cp.wait_send()
        if local_dst is not None:
            mine.wait()

    return pl.pallas_call(
        body, name=name, out_shape=out_sds, in_specs=[ANY], out_specs=ANY,
        input_output_aliases={0: 0} if in_place else {},
        scratch_shapes=[pltpu.SemaphoreType.DMA((n,)), pltpu.SemaphoreType.DMA((n,)),
                        pltpu.SemaphoreType.DMA(())],
    )(src)


FLIP_C, FLIP_X, FLIP_Y, FLIP_XY = (0, 0, 1), (1, 0, 0), (0, 1, 0), (1, 1, 0)
CHIP_FLIPS = ((0, 0), (1, 0), (0, 1), (1, 1))


def _arrival_order(x, y, c):
    chips = [(x, y), (1 - x, y), (x, 1 - y), (1 - x, 1 - y)]
    return jnp.stack([_lin(px, py, pc) for px, py in chips for pc in (c, 1 - c)]).astype(jnp.int32)


def _ag_proj(order, xb, pack):
    tm = 1024
    nrow = S // tm
    w_rows = pl.ds(R_IN, D)

    def body(order_ref, x_ref, pack_ref, h_ref, gw_ref, wbuf, wsem, send_sems, recv_sems, local_sem):
        s, i = pl.program_id(0), pl.program_id(1)
        x, y, c = lax.axis_index("x"), lax.axis_index("y"), lax.axis_index("c")
        me, sib = _lin(x, y, c), (x, y, 1 - c)
        chips = [(1 - x, y), (x, 1 - y), (1 - x, 1 - y)]

        def remote(src, slab, k, to):
            return pltpu.make_async_remote_copy(
                src_ref=src, dst_ref=gw_ref.at[slab], send_sem=send_sems.at[k], recv_sem=recv_sems.at[k],
                device_id=to, device_id_type=MESH)

        def peer(k):
            return sib if k == 0 else (*chips[k - 1], c)

        def mine_to(k):
            return remote(pack_ref, me, k, peer(k))

        def landed(k):
            return remote(pack_ref, _lin(*peer(k)), k, peer(k))

        def pass_on(j, core):
            slab = _lin(*chips[j], core)
            return remote(gw_ref.at[slab], slab, 4 + j, sib)

        local = pltpu.make_async_copy(pack_ref, gw_ref.at[me], local_sem)

        def fetch(src, slot):
            return pltpu.make_async_copy(src, wbuf.at[slot], wsem.at[slot])

        @pl.when((s == 0) & (i == 0))
        def _():
            local.start()
            for k in (1, 2, 3, 0):
                mine_to(k).start()
            first = fetch(pack_ref.at[w_rows], 0)
            first.start()
            first.wait()

        for nxt in range(1, N_DEV):
            @pl.when((s == nxt - 1) & (i == nrow - 1))
            def _(nxt=nxt):
                j = nxt // 2 - 1
                if nxt == 1:
                    landed(0).wait_recv()
                elif nxt % 2 == 0:
                    landed(1 + j).wait_recv()
                    pass_on(j, c).start()
                else:
                    pass_on(j, 1 - c).wait_recv()
                fetch(gw_ref.at[order_ref[nxt], w_rows], nxt % 2).start()

        for slot in (0, 1):
            @pl.when(s % 2 == slot)
            def _(slot=slot):
                @pl.when((i == 0) & (s > 0))
                def _():
                    fetch(pack_ref.at[w_rows], slot).wait()
                h_ref[...] = _nn(x_ref[...], wbuf[slot]).astype(h_ref.dtype)

        @pl.when((s == N_DEV - 1) & (i == nrow - 1))
        def _():
            for k in range(4):
                mine_to(k).wait_send()
            for j in range(3):
                pass_on(j, c).wait_send()
            local.wait()

    grid_spec = pltpu.PrefetchScalarGridSpec(
        num_scalar_prefetch=1, grid=(N_DEV, nrow),
        in_specs=[pl.BlockSpec((tm, D), lambda s, i, order: (i, 0)), ANY],
        out_specs=[pl.BlockSpec((tm, D), lambda s, i, order: (i, order[s])), ANY],
        scratch_shapes=[pltpu.VMEM((2, D, D), BF16), pltpu.SemaphoreType.DMA((2,)),
                        pltpu.SemaphoreType.DMA((7,)), pltpu.SemaphoreType.DMA((7,)), pltpu.SemaphoreType.DMA(())])
    return pl.pallas_call(
        body, name="ag_proj", grid_spec=grid_spec,
        out_shape=[jax.ShapeDtypeStruct((S, NW), BF16), jax.ShapeDtypeStruct((N_DEV, R_ALL, D), BF16)],
        compiler_params=_cparams(2))(order, xb, pack)


def _all_gather_direct(name, vec):
    own = lambda x, y, c: _lin(x, y, c)
    flips = [(fx, fy, fc) for fx in (0, 1) for fy in (0, 1) for fc in (0, 1) if (fx, fy, fc) != (0, 0, 0)]
    return _exchange(name, vec, [(f, None, own) for f in flips],
                     dst_shape=jax.ShapeDtypeStruct((N_DEV,) + vec.shape, vec.dtype), local_dst=own)


def _rs_sibling(p):
    plan = [(FLIP_C, (lambda x, y, c, f=f: _lin(_flip(x, f[0]), _flip(y, f[1]), 1 - c)),
             (lambda x, y, c, k=k: k)) for k, f in enumerate(CHIP_FLIPS)]
    return _exchange("rs_sibling", p, plan, dst_shape=jax.ShapeDtypeStruct((4,) + p.shape[1:], p.dtype))


def _pair_sum(coords, p, l1):
    tr = 368
    nr = R_ALL // tr

    def body(crd, p_ref, l_ref, q_ref):
        q_ref[...] = (p_ref[...].astype(F32) + l_ref[...].astype(F32)).astype(q_ref.dtype)

    def p_map(k, i, crd):
        fx, fy = k % 2, k // 2
        px = crd[0] + fx - 2 * fx * crd[0]
        py = crd[1] + fy - 2 * fy * crd[1]
        return (_lin(px, py, crd[2]), i, 0)

    grid_spec = pltpu.PrefetchScalarGridSpec(
        num_scalar_prefetch=1, grid=(4, nr),
        in_specs=[pl.BlockSpec((None, tr, D), p_map),
                  pl.BlockSpec((None, tr, D), lambda k, i, crd: (k, i, 0))],
        out_specs=pl.BlockSpec((None, tr, D), lambda k, i, crd: (k, i, 0)))
    return pl.pallas_call(body, name="pair_sum", grid_spec=grid_spec,
                          out_shape=jax.ShapeDtypeStruct((4, R_ALL, D), BF16),
                          compiler_params=_cparams(2))(coords, p, l1)


def _grad_x_rs(dh, g, dr, q):
    tm, tk = 512, 1024
    ni, nk = S // tm, NW // tk

    def body(dh_ref, w_ref, dr_ref, q_ref, o_ref, l2_ref, send_sems, recv_sems):
        i, k = pl.program_id(0), pl.program_id(1)
        x, y, c = lax.axis_index("x"), lax.axis_index("y"), lax.axis_index("c")

        def copy(j):
            f = CHIP_FLIPS[j + 1]
            return pltpu.make_async_remote_copy(
                src_ref=q_ref.at[j + 1], dst_ref=l2_ref.at[j], send_sem=send_sems.at[j], recv_sem=recv_sems.at[j],
                device_id=(_flip(x, f[0]), _flip(y, f[1]), c), device_id_type=MESH)

        @pl.when((i == 0) & (k == 0))
        def _():
            for j in range(3):
                copy(j).start()

        @pl.when(k == 0)
        def _():
            o_ref[...] = ALPHA * dr_ref[...]

        o_ref[...] += _nt(dh_ref[...], w_ref[...])

        @pl.when((i == ni - 1) & (k == nk - 1))
        def _():
            for j in range(3):
                copy(j).wait_recv()
            for j in range(3):
                copy(j).wait_send()

    return pl.pallas_call(
        body, name="grad_x_rs", grid=(ni, nk),
        in_specs=[pl.BlockSpec((tm, tk), lambda i, k: (i, k)),
                  pl.BlockSpec((None, D, tk), lambda i, k: (k // 2, 0, k % 2)),
                  pl.BlockSpec((tm, D), lambda i, k: (i, 0)), ANY],
        out_specs=[pl.BlockSpec((tm, D), lambda i, k: (i, 0)), ANY],
        out_shape=[jax.ShapeDtypeStruct((S, D), F32), jax.ShapeDtypeStruct((3,) + q.shape[1:], q.dtype)],
        scratch_shapes=[pltpu.SemaphoreType.DMA((3,)), pltpu.SemaphoreType.DMA((3,))],
        compiler_params=_cparams(2))(dh, g, dr, q)


def _grad_w(name, at, b, out_shape, out_spec, tn, tk):
    m, k_all = at.shape
    n_all = b.shape[1]
    nk = k_all // tk

    def body(a_ref, b_ref, o_ref, acc_ref):
        k = pl.program_id(1)

        @pl.when(k == 0)
        def _():
            acc_ref[...] = jnp.zeros_like(acc_ref)

        acc_ref[...] += _nn(a_ref[...], b_ref[...])

        @pl.when(k == nk - 1)
        def _():
            o_ref[...] = acc_ref[...].astype(o_ref.dtype)

    return pl.pallas_call(
        body, name=name, grid=(n_all // tn, nk),
        in_specs=[pl.BlockSpec((m, tk), lambda n, k: (0, k)),
                  pl.BlockSpec((tk, tn), lambda n, k: (k, n))],
        out_specs=out_spec, out_shape=out_shape,
        scratch_shapes=[pltpu.VMEM((m, tn), F32)], compiler_params=_cparams(2))(at, b)


NR = 16
TI = 16
TM = NR * TI
NT = S // TM
ATT_QB = (256, 128, 128)
ATT_NB = (16, 8, 2)
ATT_BLOCKS = (16, 32, 32)


def _permute_tokens(a):
    return a.reshape(NT, TI, NR, a.shape[-1]).transpose(0, 2, 1, 3).reshape(a.shape)


def _attn_shape(g, c):
    if g == 0:
        return (S, c)
    if g == 1:
        return (NT, 4, 4, TI, c)
    return (NT, NR, TI, c)


def _attn_view(g, a):
    return a.reshape(_attn_shape(g, a.shape[-1]))


def _attn_spec(g, width, col, blk):
    if g == 0:
        return pl.BlockSpec((TM, width), lambda b: (blk(b), col))
    if g == 1:
        return pl.BlockSpec((2, 4, None, TI, width), lambda b: (blk(b) % 8, 0, blk(b) // 8, 0, col))
    return pl.BlockSpec((8, None, TI, width), lambda b: (blk(b) % 2, blk(b) // 2, 0, col))


def _pieces(g):
    if g == 1:
        return [(t, m) for t in range(2) for m in range(4)]
    return [(t,) for t in range(8)]


def _get(g, ref, sl):
    if g == 0:
        return ref[:, sl]
    return jnp.concatenate([ref[(*p, slice(None), sl)] for p in _pieces(g)], axis=0)


def _put(g, ref, sl, val):
    if g == 0:
        ref[:, sl] = val
    else:
        for n, p in enumerate(_pieces(g)):
            ref[(*p, slice(None), sl)] = val[TI * n:TI * (n + 1)]


def _block_pos(g, a):
    if g == 0:
        return 16 * (a % 16) + a // 16
    if g == 1:
        return 64 * (a // 64) + 4 * (a % 16) + (a // 16) % 4
    return a


def _attn_mask(g, n):
    qb = ATT_QB[g]
    qa = lax.broadcasted_iota(jnp.int32, (qb, 2 * qb), 0)
    kc = lax.broadcasted_iota(jnp.int32, (qb, 2 * qb), 1)
    cur = kc >= qb
    dist = _block_pos(g, qa) - _block_pos(g, kc % qb) + jnp.where(cur, 0, qb)
    return (dist >= 0) & (dist <= QB) & (cur | (n > 0))


def _qkv_specs(g, clamp):
    cur = lambda col: _attn_spec(g, AW, col, clamp)
    prev = lambda col: _attn_spec(g, AW, col, lambda b: jnp.maximum(clamp(b) - 1, 0))
    qc, kc, vc = (c // AW + g for c in (COL_Q, COL_K, COL_V))
    return [cur(qc), cur(kc), prev(kc), cur(vc), prev(vc)]


def _attn_fwd(g, h):
    scale = HD ** -0.5
    hv = _attn_view(g, h)

    def body(q_ref, kc_ref, kp_ref, vc_ref, vp_ref, o_ref, l_ref):
        valid = _attn_mask(g, pl.program_id(0) % ATT_NB[g])
        for hh in range(NH):
            sl = slice(hh * HD, (hh + 1) * HD)
            kh = jnp.concatenate([_get(g, kp_ref, sl), _get(g, kc_ref, sl)], axis=0)
            vh = jnp.concatenate([_get(g, vp_ref, sl), _get(g, vc_ref, sl)], axis=0)
            s = jnp.where(valid, _nt(_get(g, q_ref, sl), kh) * scale, NEG_INF)
            m = jnp.max(s, axis=-1, keepdims=True)
            e = jnp.exp(s - m)
            den = jnp.sum(e, axis=-1, keepdims=True)
            _put(g, o_ref, sl, (_nn(e.astype(BF16), vh) / den).astype(o_ref.dtype))
            _put(g, l_ref, slice(hh, hh + 1), m + jnp.log(den))

    same = lambda b: b
    o, lse = pl.pallas_call(
        body, name=f"attn_fwd_{g}", grid=(ATT_BLOCKS[g],),
        in_specs=_qkv_specs(g, same),
        out_specs=[_attn_spec(g, AW, 0, same), _attn_spec(g, NH, 0, same)],
        out_shape=[jax.ShapeDtypeStruct(_attn_shape(g, AW), BF16), jax.ShapeDtypeStruct(_attn_shape(g, NH), F32)],
        compiler_params=_cparams(1))(hv, hv, hv, hv, hv)
    return o.reshape(S, AW), lse.reshape(S, NH)


def _attn_bwd(g, h, do, lse, delta):
    scale = HD ** -0.5
    qb = ATT_QB[g]
    last = ATT_BLOCKS[g] - 1
    clamp = lambda b: jnp.minimum(b, last)
    behind = lambda b: jnp.maximum(b - 1, 0)
    hv = _attn_view(g, h)

    def body(q_ref, kc_ref, kp_ref, vc_ref, vp_ref, do_ref, l_ref, dl_ref, dq_ref, dk_ref, dv_ref, ck_ref, cv_ref):
        b = pl.program_id(0)

        @pl.when(b == 0)
        def _():
            ck_ref[...] = jnp.zeros_like(ck_ref)
            cv_ref[...] = jnp.zeros_like(cv_ref)

        @pl.when(b <= last)
        def _():
            valid = _attn_mask(g, b % ATT_NB[g])
            for hh in range(NH):
                sl = slice(hh * HD, (hh + 1) * HD)
                one = slice(hh, hh + 1)
                qh, doh = _get(g, q_ref, sl), _get(g, do_ref, sl)
                kh = jnp.concatenate([_get(g, kp_ref, sl), _get(g, kc_ref, sl)], axis=0)
                vh = jnp.concatenate([_get(g, vp_ref, sl), _get(g, vc_ref, sl)], axis=0)
                s = _nt(qh, kh) * scale
                p = jnp.where(valid, jnp.exp(s - _get(g, l_ref, one)), 0.0)
                ds = p * (_nt(doh, vh) - _get(g, dl_ref, one))
                dsb = (ds * scale).astype(BF16)
                _put(g, dq_ref, sl, _nn(dsb, kh).astype(dq_ref.dtype))
                dk2 = _tn(dsb, qh)
                dv2 = _tn(p.astype(BF16), doh)
                _put(g, dk_ref, sl, (ck_ref[:, sl] + dk2[:qb]).astype(dk_ref.dtype))
                _put(g, dv_ref, sl, (cv_ref[:, sl] + dv2[:qb]).astype(dv_ref.dtype))
                ck_ref[:, sl] = dk2[qb:]
                cv_ref[:, sl] = dv2[qb:]

        @pl.when(b > last)
        def _():
            for hh in range(NH):
                sl = slice(hh * HD, (hh + 1) * HD)
                _put(g, dk_ref, sl, ck_ref[:, sl].astype(dk_ref.dtype))
                _put(g, dv_ref, sl, cv_ref[:, sl].astype(dv_ref.dtype))

    out = jax.ShapeDtypeStruct(_attn_shape(g, AW), BF16)
    dq, dk, dv = pl.pallas_call(
        body, name=f"attn_bwd_{g}", grid=(ATT_BLOCKS[g] + 1,),
        in_specs=_qkv_specs(g, clamp) + [_attn_spec(g, AW, 0, clamp), _attn_spec(g, NH, 0, clamp),
                                         _attn_spec(g, NH, 0, clamp)],
        out_specs=[_attn_spec(g, AW, 0, clamp), _attn_spec(g, AW, 0, behind), _attn_spec(g, AW, 0, behind)],
        out_shape=[out] * 3,
        scratch_shapes=[pltpu.VMEM((qb, AW), F32), pltpu.VMEM((qb, AW), F32)],
        compiler_params=_cparams(1))(hv, hv, hv, hv, hv, _attn_view(g, do), _attn_view(g, lse), _attn_view(g, delta))
    return dq.reshape(S, AW), dk.reshape(S, AW), dv.reshape(S, AW)


def _group_weights(l0, l1, l2):
    m = jnp.maximum(jnp.maximum(l0, l1), l2)
    e0, e1, e2 = jnp.exp(l0 - m), jnp.exp(l1 - m), jnp.exp(l2 - m)
    inv = 1.0 / (e0 + e1 + e2)
    return e0 * inv, e1 * inv, e2 * inv


def _residue(ref, r, sl):
    return ref[r * TI:(r + 1) * TI, sl].astype(F32)


def _total(parts):
    return functools.reduce(lambda x, y: x + y, parts)


def _pool_tokens(up_ref, uc_ref, p_ref, tile):
    j0 = lax.broadcasted_iota(jnp.int32, (TI, 1), 0) == 0
    first = (tile == 0) & j0
    for r in range(NR):
        out = []
        for g, w in enumerate(POOL_WINDOWS):
            sl = slice(g * PG, (g + 1) * PG)
            own = _residue(uc_ref, r, sl)
            acc = _total([own] + [_residue(uc_ref, r - k, sl) for k in range(1, min(r, w - 1) + 1)])
            wrapped = [NR + r - k for k in range(r + 1, w)]
            if wrapped:
                wc = _total([_residue(uc_ref, q, sl) for q in wrapped])
                wp = jnp.where(tile > 0, _total([_residue(up_ref, q, sl) for q in wrapped]), 0.0)
                acc = acc + jnp.where(j0, pltpu.roll(wp, 1, 0), pltpu.roll(wc, 1, 0))
            out.append(acc / jnp.where(first, float(min(r + 1, w)), float(w)) - own)
        p_ref[r * TI:(r + 1) * TI, :] = jnp.concatenate(out, axis=1).astype(p_ref.dtype)


def _pool_tokens_bwd(dp, nxt_ref, du_ref, tile):
    ji = lax.broadcasted_iota(jnp.int32, (TI, 1), 0)
    first = (tile == 0) & (ji == 0)
    piece = lambda g, r: dp[g][r * TI:(r + 1) * TI]
    dpc = [[piece(g, r) / jnp.where(first, float(min(r + 1, w)), float(w)) for r in range(NR)]
           for g, w in enumerate(POOL_WINDOWS)]
    for r in range(NR):
        out = []
        for g, w in enumerate(POOL_WINDOWS):
            sl = slice(g * PG, (g + 1) * PG)
            acc = _total([dpc[g][r + k] for k in range(w) if r + k < NR])
            wrapped = [r + k - NR for k in range(1, w) if r + k >= NR]
            if wrapped:
                wc = _total([dpc[g][q] for q in wrapped])
                wn = _total([nxt_ref[q * TI:(q + 1) * TI, sl] for q in wrapped])
                acc = acc + jnp.where(ji == TI - 1, pltpu.roll(wn, TI - 1, 0), pltpu.roll(wc, TI - 1, 0))
            out.append(acc - piece(g, r))
        du_ref[r * TI:(r + 1) * TI, :] = jnp.concatenate(out, axis=1).astype(du_ref.dtype)
    for r in range(NR):
        nxt_ref[r * TI:(r + 1) * TI, :] = jnp.concatenate([dpc[g][r] for g in range(len(POOL_WINDOWS))], axis=1)


def _pool_linear(pb, wpool_ref):
    return jnp.concatenate([_nn(pb[:, g * PG:(g + 1) * PG], wpool_ref[g]) for g in range(len(POOL_WINDOWS))], axis=1)


def _tok(width, col=0, rev=False):
    if rev:
        return pl.BlockSpec((TM, width), lambda i: (NT - 1 - i, col))
    return pl.BlockSpec((TM, width), lambda i: (i, col))


def _whole(shape):
    return pl.BlockSpec(shape, lambda i: (0,) * len(shape))


def _mix_fwd(h, o, lse, wpa, wpp, wpool, pscale, bgate):
    def body(o0_ref, o1_ref, o2_ref, l0_ref, l1_ref, l2_ref, za_ref, uc_ref, up_ref, zp_ref, gp_ref,
             wpa_ref, wpp_ref, wpool_ref, ps_ref, bg_ref, ya_ref, yp_ref, mg_ref, a_ref, b_ref, p_ref):
        i = pl.program_id(0)
        w0, w1, w2 = _group_weights(l0_ref[...], l1_ref[...], l2_ref[...])
        za = za_ref[...].astype(F32)
        silu_a = za * _sigmoid(za)
        for hh in range(NH):
            sl = slice(hh * HD, (hh + 1) * HD)
            c = slice(hh, hh + 1)
            oh = (w0[:, c] * o0_ref[:, sl].astype(F32) + w1[:, c] * o1_ref[:, sl].astype(F32)
                  + w2[:, c] * o2_ref[:, sl].astype(F32))
            ya_ref[:, sl] = (oh * silu_a[:, sl]).astype(BF16)
        _pool_tokens(up_ref, uc_ref, p_ref, i)
        zp = zp_ref[...].astype(F32)
        yp_ref[...] = (_pool_linear(p_ref[...], wpool_ref) * ps_ref[...] * (zp * _sigmoid(zp))).astype(BF16)
        a = _nn(ya_ref[...], wpa_ref[...])
        b = _nn(yp_ref[...], wpp_ref[...])
        a_ref[...] = a.astype(BF16)
        b_ref[...] = b.astype(BF16)
        gates = _sigmoid(gp_ref[...].astype(F32) + bg_ref[...])
        mg_ref[...] = (gates[:, :D] * a + gates[:, D:] * b).astype(BF16)

    u_prev = pl.BlockSpec((TM, AW), lambda i: (jnp.maximum(i - 1, 0), COL_U // AW))
    return pl.pallas_call(
        body, name="mix_fwd", grid=(NT,),
        in_specs=[_tok(AW)] * 3 + [_tok(NH)] * 3
        + [_tok(AW, COL_ZA // AW), _tok(AW, COL_U // AW), u_prev, _tok(AW, COL_ZP // AW), _tok(2 * D, COL_G // (2 * D))]
        + [_whole((AW, D)), _whole((AW, D)), _whole((4, PG, PG)), _whole((1, AW)), _whole((1, 2 * D))],
        out_specs=[_tok(AW), _tok(AW), _tok(D), _tok(D), _tok(D), _tok(AW)],
        out_shape=[jax.ShapeDtypeStruct((S, AW), BF16)] * 2 + [jax.ShapeDtypeStruct((S, D), BF16)] * 3
        + [jax.ShapeDtypeStruct((S, AW), BF16)],
        compiler_params=_cparams(1))(*o, *lse, h, h, h, h, h, wpa, wpp, wpool, pscale, bgate)


def _out_ln(merged, x, target, wout, gamma, beta):
    def body(mg_ref, x_ref, t_ref, w_ref, g_ref, b_ref, dr_ref, drb_ref, dm_ref, loss_ref, dg_ref, db_ref):
        i = pl.program_id(0)

        @pl.when(i == 0)
        def _():
            loss_ref[...] = jnp.zeros_like(loss_ref)
            dg_ref[...] = jnp.zeros_like(dg_ref)
            db_ref[...] = jnp.zeros_like(db_ref)

        r = ALPHA * x_ref[...] + _nn(mg_ref[...], w_ref[...])
        mu = jnp.mean(r, axis=-1, keepdims=True)
        rc = r - mu
        rstd = lax.rsqrt(jnp.mean(rc * rc, axis=-1, keepdims=True) + LN_EPS)
        xhat = rc * rstd
        err = xhat * g_ref[...] + b_ref[...] - t_ref[...]
        loss_ref[...] += 0.5 * jnp.sum(jnp.mean(err * err, axis=-1, keepdims=True), axis=0, keepdims=True)
        dy = err * (1.0 / D)
        dg_ref[...] += jnp.sum(dy * xhat, axis=0, keepdims=True)
        db_ref[...] += jnp.sum(dy, axis=0, keepdims=True)
        dxh = dy * g_ref[...]
        dr = rstd * (dxh - jnp.mean(dxh, axis=-1, keepdims=True)
                     - xhat * jnp.mean(dxh * xhat, axis=-1, keepdims=True))
        dr_ref[...] = dr
        drb_ref[...] = dr.astype(BF16)
        dm_ref[...] = _nt(drb_ref[...], w_ref[...]).astype(BF16)

    return pl.pallas_call(
        body, name="out_ln", grid=(NT,),
        in_specs=[_tok(D), _tok(D), _tok(D), _whole((D, D)), _whole((1, D)), _whole((1, D))],
        out_specs=[_tok(D), _tok(D), _tok(D), _whole((8, 128)), _whole((1, D)), _whole((1, D))],
        out_shape=[jax.ShapeDtypeStruct((S, D), F32), jax.ShapeDtypeStruct((S, D), BF16),
                   jax.ShapeDtypeStruct((S, D), BF16), jax.ShapeDtypeStruct((8, 128), F32),
                   jax.ShapeDtypeStruct((1, D), F32), jax.ShapeDtypeStruct((1, D), F32)],
        compiler_params=_cparams(1))(merged, x, target, wout, gamma, beta)


def _gate_bwd(dm, a, b, h, bgate):
    def body(dm_ref, a_ref, b_ref, gp_ref, bg_ref, dgp_ref, da_ref, db_ref, dbg_ref):
        @pl.when(pl.program_id(0) == 0)
        def _():
            dbg_ref[...] = jnp.zeros_like(dbg_ref)

        dm_ = dm_ref[...].astype(F32)
        gates = _sigmoid(gp_ref[...].astype(F32) + bg_ref[...])
        ga, gb = gates[:, :D], gates[:, D:]
        da_ref[...] = (dm_ * ga).astype(BF16)
        db_ref[...] = (dm_ * gb).astype(BF16)
        dgp = jnp.concatenate([dm_ * a_ref[...].astype(F32) * ga * (1.0 - ga),
                               dm_ * b_ref[...].astype(F32) * gb * (1.0 - gb)], axis=1)
        dgp_ref[...] = dgp.astype(BF16)
        dbg_ref[...] += jnp.sum(dgp, axis=0, keepdims=True)

    return pl.pallas_call(
        body, name="gate_bwd", grid=(NT,),
        in_specs=[_tok(D), _tok(D), _tok(D), _tok(2 * D, COL_G // (2 * D)), _whole((1, 2 * D))],
        out_specs=[_tok(2 * D), _tok(D), _tok(D), _whole((1, 2 * D))],
        out_shape=[jax.ShapeDtypeStruct((S, 2 * D), BF16), jax.ShapeDtypeStruct((S, D), BF16),
                   jax.ShapeDtypeStruct((S, D), BF16), jax.ShapeDtypeStruct((1, 2 * D), F32)],
        compiler_params=_cparams(1))(dm, a, b, h, bgate)


def _mix_bwd(da, db, h, o, lse, p, wpa, wpp, wpool, pscale):
    def body(da_ref, db_ref, o0_ref, o1_ref, o2_ref, l0_ref, l1_ref, l2_ref, za_ref, zp_ref, p_ref,
             wpa_ref, wpp_ref, wpool_ref, ps_ref,
             dza_ref, du_ref, dzp_ref, do0_ref, do1_ref, do2_ref, dl0_ref, dl1_ref, dl2_ref, dwp_ref, dps_ref,
             nxt_ref):
        i = pl.program_id(0)
        tile = NT - 1 - i

        @pl.when(i == 0)
        def _():
            nxt_ref[...] = jnp.zeros_like(nxt_ref)
            dwp_ref[...] = jnp.zeros_like(dwp_ref)
            dps_ref[...] = jnp.zeros_like(dps_ref)

        dya = _nt(da_ref[...], wpa_ref[...])
        w0, w1, w2 = _group_weights(l0_ref[...], l1_ref[...], l2_ref[...])
        za = za_ref[...].astype(F32)
        sig = _sigmoid(za)
        silu_a = za * sig
        dsilu_a = sig * (1.0 + za * (1.0 - sig))
        for hh in range(NH):
            sl = slice(hh * HD, (hh + 1) * HD)
            c = slice(hh, hh + 1)
            oh = (w0[:, c] * o0_ref[:, sl].astype(F32) + w1[:, c] * o1_ref[:, sl].astype(F32)
                  + w2[:, c] * o2_ref[:, sl].astype(F32))
            doh = dya[:, sl] * silu_a[:, sl]
            dza_ref[:, sl] = (dya[:, sl] * oh * dsilu_a[:, sl]).astype(BF16)
            dot_ = jnp.sum(doh * oh, axis=-1, keepdims=True)
            do0_ref[:, sl] = (w0[:, c] * doh).astype(BF16)
            do1_ref[:, sl] = (w1[:, c] * doh).astype(BF16)
            do2_ref[:, sl] = (w2[:, c] * doh).astype(BF16)
            dl0_ref[:, c] = w0[:, c] * dot_
            dl1_ref[:, c] = w1[:, c] * dot_
            dl2_ref[:, c] = w2[:, c] * dot_
        dyp = _nt(db_ref[...], wpp_ref[...])
        pb = p_ref[...]
        pw = _pool_linear(pb, wpool_ref)
        zp = zp_ref[...].astype(F32)
        sigp = _sigmoid(zp)
        dypre = dyp * (zp * sigp)
        dzp_ref[...] = (dyp * (pw * ps_ref[...]) * (sigp * (1.0 + zp * (1.0 - sigp)))).astype(BF16)
        dps_ref[...] += jnp.sum(dypre * pw, axis=0, keepdims=True)
        dpw = (dypre * ps_ref[...]).astype(BF16)
        dp = []
        for g in range(len(POOL_WINDOWS)):
            sl = slice(g * PG, (g + 1) * PG)
            dwp_ref[g] += _tn(pb[:, sl], dpw[:, sl])
            dp.append(_nt(dpw[:, sl], wpool_ref[g]))
        _pool_tokens_bwd(dp, nxt_ref, du_ref, tile)

    r = functools.partial(_tok, rev=True)
    return pl.pallas_call(
        body, name="mix_bwd", grid=(NT,),
        in_specs=[r(D), r(D)] + [r(AW)] * 3 + [r(NH)] * 3 + [r(AW, COL_ZA // AW), r(AW, COL_ZP // AW), r(AW)]
        + [_whole((AW, D)), _whole((AW, D)), _whole((4, PG, PG)), _whole((1, AW))],
        out_specs=[r(AW)] * 6 + [r(NH)] * 3 + [_whole((4, PG, PG)), _whole((1, AW))],
        out_shape=[jax.ShapeDtypeStruct((S, AW), BF16)] * 6 + [jax.ShapeDtypeStruct((S, NH), F32)] * 3
        + [jax.ShapeDtypeStruct((4, PG, PG), F32), jax.ShapeDtypeStruct((1, AW), F32)],
        scratch_shapes=[pltpu.VMEM((TM, AW), F32)],
        compiler_params=_cparams(1))(da, db, *o, *lse, h, h, p, wpa, wpp, wpool, pscale)


def _adamw(w, g, m, v):
    m = B1 * m + (1.0 - B1) * g
    v = B2 * v + (1.0 - B2) * jnp.square(g)
    m_hat = m / (1.0 - B1 ** STEP)
    v_hat = v / (1.0 - B2 ** STEP)
    return -LR * (m_hat / (jnp.sqrt(v_hat) + EPS) + WD * w), m, v


def _adam_shard(name, q, l2, w, m, v, tr):
    rows = w.shape[0]

    def body(q_ref, l_ref, w_ref, m_ref, v_ref, g_out, d_out, m_out, v_out):
        g = q_ref[...].astype(F32)
        for k in range(3):
            g = g + l_ref[k].astype(F32)
        g_out[...] = g
        d_out[...], m_out[...], v_out[...] = _adamw(w_ref[...], g, m_ref[...], v_ref[...])

    blk = pl.BlockSpec((tr, D), lambda i: (i, 0))
    return pl.pallas_call(
        body, name=name, grid=(rows // tr,),
        in_specs=[pl.BlockSpec((None, tr, D), lambda i: (0, i, 0)), pl.BlockSpec((3, tr, D), lambda i: (0, i, 0)),
                  blk, blk, blk],
        out_specs=[blk] * 4, out_shape=[jax.ShapeDtypeStruct((rows, D), F32)] * 4,
        compiler_params=_cparams(1))(q, l2, w, m, v)


def _adam_replicated(gathered, w, m, v):
    def body(g_ref, w_ref, m_ref, v_ref, g_out, d_out, m_out, v_out):
        g = g_ref[0]
        for k in range(1, N_DEV):
            g = g + g_ref[k]
        g_out[...] = g
        d_out[...], m_out[...], v_out[...] = _adamw(w_ref[...], g, m_ref[...], v_ref[...])

    return pl.pallas_call(body, name="adam_replicated", out_shape=[jax.ShapeDtypeStruct((8, D), F32)] * 4,
                          compiler_params=pltpu.CompilerParams(vmem_limit_bytes=VMEM_LIMIT))(gathered, w, m, v)


def _pack_small(w_out, w_pa, w_pp, w_pool):
    return jnp.concatenate([w_out, w_pa.reshape(-1, D), w_pp.reshape(-1, D), w_pool.reshape(-1, D)], axis=0)


def _unpack_small(a):
    o = R_OUT
    return (a[:R_PA - o], a[R_PA - o:R_PP - o].reshape(AW, 256), a[R_PP - o:R_PL - o].reshape(AW, 256),
            a[R_PL - o:].reshape(4, 32, PG))


def _pack_vec(b_gate, gamma, beta, pscale, extra):
    z = jnp.zeros((D,), F32)
    return jnp.stack([b_gate[:D], b_gate[D:], gamma, beta, jnp.concatenate([pscale, z[:D - AW]]),
                      jnp.broadcast_to(extra, (D,)), z, z])


def _unpack_vec(a):
    return jnp.concatenate([a[0], a[1]])[None], a[4, :AW][None], a[2][None], a[3][None]


def kernel(x, w_in, b_gate, w_pool, pool_scale, w_proj_attn, w_proj_pool, w_out, ln_gamma, ln_beta, loss_target, m_w_in, m_b_gate, m_w_pool, m_pool_scale, m_w_proj_attn, m_w_proj_pool, m_w_out, m_ln_gamma, m_ln_beta, v_w_in, v_b_gate, v_w_pool, v_pool_scale, v_w_proj_attn, v_w_proj_pool, v_w_out, v_ln_gamma, v_ln_beta):
    coords = jnp.stack([lax.axis_index("x"), lax.axis_index("y"), lax.axis_index("c")]).astype(jnp.int32)
    x2, tgt = _permute_tokens(x[0]), _permute_tokens(loss_target[0])
    xb = x2.astype(BF16)
    xt = x2.T.astype(BF16)

    pack = jnp.concatenate([w_in[0].astype(BF16),
                            _pack_small(w_out[0], w_proj_attn[0], w_proj_pool[0], w_pool[0]).astype(BF16)], axis=0)
    h, gw = _ag_proj(_arrival_order(*coords), xb, pack)
    wout = gw[:, R_OUT:R_PA].reshape(D, D)
    wpa = gw[:, R_PA:R_PP].reshape(N_DEV, AW, 256).transpose(1, 0, 2).reshape(AW, D)
    wpp = gw[:, R_PP:R_PL].reshape(N_DEV, AW, 256).transpose(1, 0, 2).reshape(AW, D)
    wpool = gw[:, R_PL:].reshape(N_DEV, 4, 32, PG).transpose(1, 0, 2, 3).reshape(4, PG, PG)

    o, lse = zip(*[_attn_fwd(g, h) for g in range(len(DILATIONS))])
    ya, yp, merged, a, b, p = _mix_fwd(h, o, lse, wpa, wpp, wpool, pool_scale, b_gate)
    dr, drb, dm, loss_part, dgamma, dbeta = _out_ln(merged, x2, tgt, wout, ln_gamma, ln_beta)

    dgp, da, db, dbgate = _gate_bwd(dm, a, b, h, b_gate)
    dza, du, dzp, do0, do1, do2, dl0, dl1, dl2, dwpool, dpscale = _mix_bwd(
        da, db, h, o, lse, p, wpa, wpp, wpool, pool_scale)
    dq, dk, dv = zip(*[_attn_bwd(g, h, do_g, lse[g], dl_g)
                       for g, (do_g, dl_g) in enumerate(zip((do0, do1, do2), (dl0, dl1, dl2)))])
    dh = jnp.concatenate([*dq, *dk, *dv, dza, du, dzp, dgp], axis=1)

    slab = lambda n, k: (n // 2, 0, n % 2)
    part = _grad_w("grad_w_in", xt, dh, jax.ShapeDtypeStruct((N_DEV, R_ALL, D), BF16),
                   pl.BlockSpec((None, D, 1024), slab), 1024, 1024)
    flat = lambda n, k: (0, n)
    d_wout = _grad_w("grad_w_out", merged.T, drb, jax.ShapeDtypeStruct((D, D), BF16),
                     pl.BlockSpec((D, 1024), flat), 1024, 1024)
    d_wpa = _grad_w("grad_w_pa", ya.T, da, jax.ShapeDtypeStruct((AW, D), BF16),
                    pl.BlockSpec((AW, 1024), flat), 1024, 1024)
    d_wpp = _grad_w("grad_w_pp", yp.T, db, jax.ShapeDtypeStruct((AW, D), BF16),
                    pl.BlockSpec((AW, 1024), flat), 1024, 1024)
    small = jnp.concatenate([
        d_wout.reshape(N_DEV, 256, D),
        d_wpa.reshape(AW, N_DEV, 256).transpose(1, 0, 2).reshape(N_DEV, -1, D),
        d_wpp.reshape(AW, N_DEV, 256).transpose(1, 0, 2).reshape(N_DEV, -1, D),
        dwpool.astype(BF16).reshape(4, N_DEV, 32, PG).transpose(1, 0, 2, 3).reshape(N_DEV, -1, D)], axis=1)
    part = lax.dynamic_update_slice(part, small, (0, R_OUT, 0))

    q = _pair_sum(coords, part, _rs_sibling(part))
    grad_x, l2 = _grad_x_rs(dh, gw, dr, q)
    g_in, d_in, m_in, v_in = _adam_shard("adam_w_in", q, l2, w_in[0], m_w_in[0], v_w_in[0], 256)
    outs_small = _adam_shard(
        "adam_small", q[:, R_OUT:], l2[:, R_OUT:],
        _pack_small(w_out[0], w_proj_attn[0], w_proj_pool[0], w_pool[0]),
        _pack_small(m_w_out[0], m_w_proj_attn[0], m_w_proj_pool[0], m_w_pool[0]),
        _pack_small(v_w_out[0], v_w_proj_attn[0], v_w_proj_pool[0], v_w_pool[0]), 176)
    small_parts = [_unpack_small(t) for t in outs_small]

    vec = _pack_vec(dbgate[0], dgamma[0], dbeta[0], dpscale[0], loss_part[0, 0])
    outs_vec = _adam_replicated(
        _all_gather_direct("ag_vec", vec),
        _pack_vec(b_gate[0], ln_gamma[0], ln_beta[0], pool_scale[0], 0.0),
        _pack_vec(m_b_gate[0], m_ln_gamma[0], m_ln_beta[0], m_pool_scale[0], 0.0),
        _pack_vec(v_b_gate[0], v_ln_gamma[0], v_ln_beta[0], v_pool_scale[0], 0.0))
    vec_parts = [_unpack_vec(t) for t in outs_vec]
    loss = outs_vec[0][5, 0]

    def leaves(kind, big):
        out, pa, pp, pool = small_parts[kind]
        bg, ps, gm, bt = vec_parts[kind]
        return [big[None], bg, pool[None], ps, pa[None], pp[None], out[None], gm, bt]

    return (loss, _permute_tokens(grad_x)[None], *leaves(0, g_in), *leaves(1, d_in), *leaves(2, m_in), *leaves(3, v_in))
```

```python
import functools

import jax
import jax.numpy as jnp
from jax import lax
from jax.experimental import pallas as pl
from jax.experimental.pallas import tpu as pltpu

F32 = jnp.float32
BF16 = jnp.bfloat16

S = 4096
D = 2048
NW = 16384
AW = 1024
HD = 128
NH = 8
QB = 128
NBLK = S // QB
DILATIONS = (1, 4, 16)
POOL_WINDOWS = (2, 4, 8, 16)
PG = 256
N_DEV = 8
COL_Q, COL_K, COL_V = 0, 3 * AW, 6 * AW
COL_ZA, COL_U, COL_ZP, COL_G = 9 * AW, 10 * AW, 11 * AW, 12 * AW
DH_Z, DH_G = COL_ZA, COL_G
ALPHA = 2.0 ** 0.25
LN_EPS = 1e-5
NEG_INF = -1e30
LR, B1, B2, EPS, WD, STEP = 0.001, 0.9, 0.999, 1e-08, 0.01, 10
R_IN, R_OUT, R_PA, R_PP, R_PL = 0, 2048, 2304, 2432, 2560
R_ALL = 2576
R_SMALL = R_ALL - R_OUT
VMEM_LIMIT = 56 * 1024 * 1024
MESH = pl.DeviceIdType.MESH
ANY = pl.BlockSpec(memory_space=pl.ANY)


def _cparams(n_axes):
    return pltpu.CompilerParams(dimension_semantics=("arbitrary",) * n_axes, vmem_limit_bytes=VMEM_LIMIT)


def _sigmoid(z):
    return 1.0 / (1.0 + jnp.exp(-z))


def _nt(a, b):
    return lax.dot_general(a, b, (((1,), (1,)), ((), ())), preferred_element_type=F32)


def _tn(a, b):
    return lax.dot_general(a, b, (((0,), (0,)), ((), ())), preferred_element_type=F32)


def _nn(a, b):
    return jnp.dot(a, b, preferred_element_type=F32)


def _lin(x, y, c):
    return 4 * x + 2 * y + c


def _flip(v, f):
    return 1 - v if f else v


def _exchange(name, src, plan, *, dst_shape=None, local_dst=None):
    n = len(plan)
    in_place = dst_shape is None
    out_sds = jax.ShapeDtypeStruct(src.shape, src.dtype) if in_place else dst_shape

    def body(src_ref, dst_ref, send_sems, recv_sems, local_sem):
        x, y, c = lax.axis_index("x"), lax.axis_index("y"), lax.axis_index("c")

        def copy(k, sender):
            flip, src_index, dst_index = plan[k]
            sx, sy, sc = sender
            to = (_flip(sx, flip[0]), _flip(sy, flip[1]), _flip(sc, flip[2]))
            s = src_ref if src_index is None else src_ref.at[src_index(sx, sy, sc)]
            return pltpu.make_async_remote_copy(
                src_ref=s, dst_ref=dst_ref.at[dst_index(sx, sy, sc)],
                send_sem=send_sems.at[k], recv_sem=recv_sems.at[k],
                device_id=to, device_id_type=MESH)

        me = (x, y, c)
        if local_dst is not None:
            mine = pltpu.make_async_copy(src_ref, dst_ref.at[local_dst(x, y, c)], local_sem)
            mine.start()
        sends = [copy(k, me) for k in range(n)]
        for cp in sends:
            cp.start()
        for k in range(n):
            flip = plan[k][0]
            copy(k, (_flip(x, flip[0]), _flip(y, flip[1]), _flip(c, flip[2]))).wait_recv()
        for cp in sends:
            cp.wait_send()
        if local_dst is not None:
            mine.wait()

    return pl.pallas_call(
        body, name=name, out_shape=out_sds, in_specs=[ANY], out_specs=ANY,
        input_output_aliases={0: 0} if in_place else {},
        scratch_shapes=[pltpu.SemaphoreType.DMA((n,)), pltpu.SemaphoreType.DMA((n,)),
                        pltpu.SemaphoreType.DMA(())],
    )(src)


FLIP_C, FLIP_X, FLIP_Y, FLIP_XY = (0, 0, 1), (1, 0, 0), (0, 1, 0), (1, 1, 0)
CHIP_FLIPS = ((0, 0), (1, 0), (0, 1), (1, 1))


def _arrival_order(x, y, c):
    chips = [(x, y), (1 - x, y), (x, 1 - y), (1 - x, 1 - y)]
    return jnp.stack([_lin(px, py, pc) for px, py in chips for pc in (c, 1 - c)]).astype(jnp.int32)


def _ag_proj(order, xb, pack):
    tm = 1024
    nrow = S // tm
    w_rows = pl.ds(R_IN, D)

    def body(order_ref, x_ref, pack_ref, h_ref, gw_ref, wbuf, wsem, send_sems, recv_sems, local_sem):
        s, i = pl.program_id(0), pl.program_id(1)
        x, y, c = lax.axis_index("x"), lax.axis_index("y"), lax.axis_index("c")
        me, sib = _lin(x, y, c), (x, y, 1 - c)
        chips = [(1 - x, y), (x, 1 - y), (1 - x, 1 - y)]

        def remote(src, slab, k, to):
            return pltpu.make_async_remote_copy(
                src_ref=src, dst_ref=gw_ref.at[slab], send_sem=send_sems.at[k], recv_sem=recv_sems.at[k],
                device_id=to, device_id_type=MESH)

        def peer(k):
            return sib if k == 0 else (*chips[k - 1], c)

        def mine_to(k):
            return remote(pack_ref, me, k, peer(k))

        def landed(k):
            return remote(pack_ref, _lin(*peer(k)), k, peer(k))

        def pass_on(j, core):
            slab = _lin(*chips[j], core)
            return remote(gw_ref.at[slab], slab, 4 + j, sib)

        local = pltpu.make_async_copy(pack_ref, gw_ref.at[me], local_sem)

        def fetch(src, slot):
            return pltpu.make_async_copy(src, wbuf.at[slot], wsem.at[slot])

        @pl.when((s == 0) & (i == 0))
        def _():
            local.start()
            for k in (1, 2, 3, 0):
                mine_to(k).start()
            first = fetch(pack_ref.at[w_rows], 0)
            first.start()
            first.wait()

        for nxt in range(1, N_DEV):
            @pl.when((s == nxt - 1) & (i == nrow - 1))
            def _(nxt=nxt):
                j = nxt // 2 - 1
                if nxt == 1:
                    landed(0).wait_recv()
                elif nxt % 2 == 0:
                    landed(1 + j).wait_recv()
                    pass_on(j, c).start()
                else:
                    pass_on(j, 1 - c).wait_recv()
                fetch(gw_ref.at[order_ref[nxt], w_rows], nxt % 2).start()

        for slot in (0, 1):
            @pl.when(s % 2 == slot)
            def _(slot=slot):
                @pl.when((i == 0) & (s > 0))
                def _():
                    fetch(pack_ref.at[w_rows], slot).wait()
                h_ref[...] = _nn(x_ref[...], wbuf[slot]).astype(h_ref.dtype)

        @pl.when((s == N_DEV - 1) & (i == nrow - 1))
        def _():
            for k in range(4):
                mine_to(k).wait_send()
            for j in range(3):
                pass_on(j, c).wait_send()
            local.wait()

    grid_spec = pltpu.PrefetchScalarGridSpec(
        num_scalar_prefetch=1, grid=(N_DEV, nrow),
        in_specs=[pl.BlockSpec((tm, D), lambda s, i, order: (i, 0)), ANY],
        out_specs=[pl.BlockSpec((tm, D), lambda s, i, order: (i, order[s])), ANY],
        scratch_shapes=[pltpu.VMEM((2, D, D), BF16), pltpu.SemaphoreType.DMA((2,)),
                        pltpu.SemaphoreType.DMA((7,)), pltpu.SemaphoreType.DMA((7,)), pltpu.SemaphoreType.DMA(())])
    return pl.pallas_call(
        body, name="ag_proj", grid_spec=grid_spec,
        out_shape=[jax.ShapeDtypeStruct((S, NW), BF16), jax.ShapeDtypeStruct((N_DEV, R_ALL, D), BF16)],
        compiler_params=_cparams(2))(order, xb, pack)


def _all_gather_direct(name, vec):
    own = lambda x, y, c: _lin(x, y, c)
    flips = [(fx, fy, fc) for fx in (0, 1) for fy in (0, 1) for fc in (0, 1) if (fx, fy, fc) != (0, 0, 0)]
    return _exchange(name, vec, [(f, None, own) for f in flips],
                     dst_shape=jax.ShapeDtypeStruct((N_DEV,) + vec.shape, vec.dtype), local_dst=own)


def _rs_sibling(p):
    plan = [(FLIP_C, (lambda x, y, c, f=f: _lin(_flip(x, f[0]), _flip(y, f[1]), 1 - c)),
             (lambda x, y, c, k=k: k)) for k, f in enumerate(CHIP_FLIPS)]
    return _exchange("rs_sibling", p, plan, dst_shape=jax.ShapeDtypeStruct((4,) + p.shape[1:], p.dtype))


def _pair_sum(coords, p, l1):
    tr = 368
    nr = R_ALL // tr

    def body(crd, p_ref, l_ref, q_ref):
        q_ref[...] = (p_ref[...].astype(F32) + l_ref[...].astype(F32)).astype(q_ref.dtype)

    def p_map(k, i, crd):
        fx, fy = k % 2, k // 2
        px = crd[0] + fx - 2 * fx * crd[0]
        py = crd[1] + fy - 2 * fy * crd[1]
        return (_lin(px, py, crd[2]), i, 0)

    grid_spec = pltpu.PrefetchScalarGridSpec(
        num_scalar_prefetch=1, grid=(4, nr),
        in_specs=[pl.BlockSpec((None, tr, D), p_map),
                  pl.BlockSpec((None, tr, D), lambda k, i, crd: (k, i, 0))],
        out_specs=pl.BlockSpec((None, tr, D), lambda k, i, crd: (k, i, 0)))
    return pl.pallas_call(body, name="pair_sum", grid_spec=grid_spec,
                          out_shape=jax.ShapeDtypeStruct((4, R_ALL, D), BF16),
                          compiler_params=_cparams(2))(coords, p, l1)


def _h_block(k):
    return jnp.where(k < 9, (k % 3) * 3 + k // 3, k)


def _grad_x_rs(dh, g, dr, q):
    tm, tk = 512, 1024
    ni, nk = S // tm, NW // tk

    def body(dh_ref, w_ref, dr_ref, q_ref, o_ref, l2_ref, send_sems, recv_sems):
        i, k = pl.program_id(0), pl.program_id(1)
        x, y, c = lax.axis_index("x"), lax.axis_index("y"), lax.axis_index("c")

        def copy(j):
            f = CHIP_FLIPS[j + 1]
            return pltpu.make_async_remote_copy(
                src_ref=q_ref.at[j + 1], dst_ref=l2_ref.at[j], send_sem=send_sems.at[j], recv_sem=recv_sems.at[j],
                device_id=(_flip(x, f[0]), _flip(y, f[1]), c), device_id_type=MESH)

        @pl.when((i == 0) & (k == 0))
        def _():
            for j in range(3):
                copy(j).start()

        @pl.when(k == 0)
        def _():
            o_ref[...] = ALPHA * dr_ref[...]

        o_ref[...] += _nt(dh_ref[...], w_ref[...])

        @pl.when((i == ni - 1) & (k == nk - 1))
        def _():
            for j in range(3):
                copy(j).wait_recv()
            for j in range(3):
                copy(j).wait_send()

    return pl.pallas_call(
        body, name="grad_x_rs", grid=(ni, nk),
        in_specs=[pl.BlockSpec((tm, tk), lambda i, k: (i, k)),
                  pl.BlockSpec((None, D, tk), lambda i, k: (_h_block(k) // 2, 0, _h_block(k) % 2)),
                  pl.BlockSpec((tm, D), lambda i, k: (i, 0)), ANY],
        out_specs=[pl.BlockSpec((tm, D), lambda i, k: (i, 0)), ANY],
        out_shape=[jax.ShapeDtypeStruct((S, D), F32), jax.ShapeDtypeStruct((3,) + q.shape[1:], q.dtype)],
        scratch_shapes=[pltpu.SemaphoreType.DMA((3,)), pltpu.SemaphoreType.DMA((3,))],
        compiler_params=_cparams(2))(dh, g, dr, q)


def _grad_w(name, at, b, out_shape, out_spec, tn, tk, b_block=lambda n: n):
    m, k_all = at.shape
    n_all = b.shape[1]
    nk = k_all // tk

    def body(a_ref, b_ref, o_ref, acc_ref):
        k = pl.program_id(1)

        @pl.when(k == 0)
        def _():
            acc_ref[...] = jnp.zeros_like(acc_ref)

        acc_ref[...] += _nn(a_ref[...], b_ref[...])

        @pl.when(k == nk - 1)
        def _():
            o_ref[...] = acc_ref[...].astype(o_ref.dtype)

    return pl.pallas_call(
        body, name=name, grid=(n_all // tn, nk),
        in_specs=[pl.BlockSpec((m, tk), lambda n, k: (0, k)),
                  pl.BlockSpec((tk, tn), lambda n, k: (k, b_block(n)))],
        out_specs=out_spec, out_shape=out_shape,
        scratch_shapes=[pltpu.VMEM((m, tn), F32)], compiler_params=_cparams(2))(at, b)


NR = 16
TI = 16
TM = NR * TI
NT = S // TM
ATT_QB = (256, 128, 128)
ATT_NB = (16, 8, 2)
ATT_BLOCKS = (16, 32, 32)


def _permute_tokens(a):
    return a.reshape(NT, TI, NR, a.shape[-1]).transpose(0, 2, 1, 3).reshape(a.shape)


def _attn_shape(g, c):
    if g == 0:
        return (S, c)
    if g == 1:
        return (NT, 4, 4, TI, c)
    return (NT, NR, TI, c)


def _attn_view(g, a):
    return a.reshape(_attn_shape(g, a.shape[-1]))


def _attn_spec(g, width, col, blk):
    if g == 0:
        return pl.BlockSpec((TM, width), lambda b: (blk(b), col))
    if g == 1:
        return pl.BlockSpec((2, 4, None, TI, width), lambda b: (blk(b) % 8, 0, blk(b) // 8, 0, col))
    return pl.BlockSpec((8, None, TI, width), lambda b: (blk(b) % 2, blk(b) // 2, 0, col))


def _pieces(g):
    if g == 1:
        return [(t, m) for t in range(2) for m in range(4)]
    return [(t,) for t in range(8)]


def _get(g, ref, sl):
    if g == 0:
        return ref[:, sl]
    return jnp.concatenate([ref[(*p, slice(None), sl)] for p in _pieces(g)], axis=0)


def _put(g, ref, sl, val):
    if g == 0:
        ref[:, sl] = val
    else:
        for n, p in enumerate(_pieces(g)):
            ref[(*p, slice(None), sl)] = val[TI * n:TI * (n + 1)]


def _block_pos(g, a):
    if g == 0:
        return 16 * (a % 16) + a // 16
    if g == 1:
        return 64 * (a // 64) + 4 * (a % 16) + (a // 16) % 4
    return a


def _attn_mask(g, n):
    qb = ATT_QB[g]
    qa = lax.broadcasted_iota(jnp.int32, (qb, 2 * qb), 0)
    kc = lax.broadcasted_iota(jnp.int32, (qb, 2 * qb), 1)
    cur = kc >= qb
    dist = _block_pos(g, qa) - _block_pos(g, kc % qb) + jnp.where(cur, 0, qb)
    return (dist >= 0) & (dist <= QB) & (cur | (n > 0))


def _qkv_specs(g, clamp):
    cur = lambda col: _attn_spec(g, AW, col, clamp)
    prev = lambda col: _attn_spec(g, AW, col, lambda b: jnp.maximum(clamp(b) - 1, 0))
    qc, kc, vc = (c // AW + g for c in (COL_Q, COL_K, COL_V))
    return [cur(qc), cur(kc), prev(kc), cur(vc), prev(vc)]


def _attn_fwd(g, h):
    scale = HD ** -0.5
    hv = _attn_view(g, h)

    def body(q_ref, kc_ref, kp_ref, vc_ref, vp_ref, o_ref, l_ref):
        valid = _attn_mask(g, pl.program_id(0) % ATT_NB[g])
        for hh in range(NH):
            sl = slice(hh * HD, (hh + 1) * HD)
            kh = jnp.concatenate([_get(g, kp_ref, sl), _get(g, kc_ref, sl)], axis=0)
            vh = jnp.concatenate([_get(g, vp_ref, sl), _get(g, vc_ref, sl)], axis=0)
            s = jnp.where(valid, _nt(_get(g, q_ref, sl), kh) * scale, NEG_INF)
            m = jnp.max(s, axis=-1, keepdims=True)
            e = jnp.exp(s - m)
            den = jnp.sum(e, axis=-1, keepdims=True)
            _put(g, o_ref, sl, (_nn(e.astype(BF16), vh) / den).astype(o_ref.dtype))
            _put(g, l_ref, slice(hh, hh + 1), m + jnp.log(den))

    same = lambda b: b
    o, lse = pl.pallas_call(
        body, name=f"attn_fwd_{g}", grid=(ATT_BLOCKS[g],),
        in_specs=_qkv_specs(g, same),
        out_specs=[_attn_spec(g, AW, 0, same), _attn_spec(g, NH, 0, same)],
        out_shape=[jax.ShapeDtypeStruct(_attn_shape(g, AW), BF16), jax.ShapeDtypeStruct(_attn_shape(g, NH), F32)],
        compiler_params=_cparams(1))(hv, hv, hv, hv, hv)
    return o.reshape(S, AW), lse.reshape(S, NH)


def _attn_bwd(g, dh, h, do, lse, delta):
    scale = HD ** -0.5
    qb = ATT_QB[g]
    last = ATT_BLOCKS[g] - 1
    clamp = lambda b: jnp.minimum(b, last)
    behind = lambda b: jnp.maximum(b - 1, 0)
    hv = _attn_view(g, h)

    def body(q_ref, kc_ref, kp_ref, vc_ref, vp_ref, do_ref, l_ref, dl_ref, _, dh_ref, cq_ref, ck_ref, cv_ref):
        b = pl.program_id(0)

        def write(col, val):
            _put(g, dh_ref, slice(col, col + HD), val.astype(dh_ref.dtype))

        @pl.when(b == 0)
        def _():
            cq_ref[...] = jnp.zeros_like(cq_ref)
            ck_ref[...] = jnp.zeros_like(ck_ref)
            cv_ref[...] = jnp.zeros_like(cv_ref)

        @pl.when(b <= last)
        def _():
            valid = _attn_mask(g, b % ATT_NB[g])
            for hh in range(NH):
                sl = slice(hh * HD, (hh + 1) * HD)
                one = slice(hh, hh + 1)
                qh, doh = _get(g, q_ref, sl), _get(g, do_ref, sl)
                kh = jnp.concatenate([_get(g, kp_ref, sl), _get(g, kc_ref, sl)], axis=0)
                vh = jnp.concatenate([_get(g, vp_ref, sl), _get(g, vc_ref, sl)], axis=0)
                s = _nt(qh, kh) * scale
                p = jnp.where(valid, jnp.exp(s - _get(g, l_ref, one)), 0.0)
                ds = p * (_nt(doh, vh) - _get(g, dl_ref, one))
                dsb = (ds * scale).astype(BF16)
                dk2 = _tn(dsb, qh)
                dv2 = _tn(p.astype(BF16), doh)
                write(hh * HD, cq_ref[:, sl])
                write(AW + hh * HD, ck_ref[:, sl] + dk2[:qb])
                write(2 * AW + hh * HD, cv_ref[:, sl] + dv2[:qb])
                cq_ref[:, sl] = _nn(dsb, kh)
                ck_ref[:, sl] = dk2[qb:]
                cv_ref[:, sl] = dv2[qb:]

        @pl.when(b > last)
        def _():
            for hh in range(NH):
                sl = slice(hh * HD, (hh + 1) * HD)
                write(hh * HD, cq_ref[:, sl])
                write(AW + hh * HD, ck_ref[:, sl])
                write(2 * AW + hh * HD, cv_ref[:, sl])

    out = pl.pallas_call(
        body, name=f"attn_bwd_{g}", grid=(ATT_BLOCKS[g] + 1,),
        in_specs=_qkv_specs(g, clamp) + [_attn_spec(g, AW, 0, clamp), _attn_spec(g, NH, 0, clamp),
                                         _attn_spec(g, NH, 0, clamp), ANY],
        out_specs=_attn_spec(g, 3 * AW, g, behind),
        out_shape=jax.ShapeDtypeStruct(_attn_shape(g, NW), BF16),
        input_output_aliases={8: 0},
        scratch_shapes=[pltpu.VMEM((qb, AW), F32)] * 3,
        compiler_params=_cparams(1))(hv, hv, hv, hv, hv, _attn_view(g, do), _attn_view(g, lse), _attn_view(g, delta),
                                     _attn_view(g, dh))
    return out.reshape(S, NW)


def _group_weights(l0, l1, l2):
    m = jnp.maximum(jnp.maximum(l0, l1), l2)
    e0, e1, e2 = jnp.exp(l0 - m), jnp.exp(l1 - m), jnp.exp(l2 - m)
    inv = 1.0 / (e0 + e1 + e2)
    return e0 * inv, e1 * inv, e2 * inv


def _residue(ref, r, sl):
    return ref[r * TI:(r + 1) * TI, sl].astype(F32)


def _total(parts):
    return functools.reduce(lambda x, y: x + y, parts)


def _pool_tokens(up_ref, uc_ref, p_ref, tile):
    j0 = lax.broadcasted_iota(jnp.int32, (TI, 1), 0) == 0
    first = (tile == 0) & j0
    for r in range(NR):
        out = []
        for g, w in enumerate(POOL_WINDOWS):
            sl = slice(g * PG, (g + 1) * PG)
            own = _residue(uc_ref, r, sl)
            acc = _total([own] + [_residue(uc_ref, r - k, sl) for k in range(1, min(r, w - 1) + 1)])
            wrapped = [NR + r - k for k in range(r + 1, w)]
            if wrapped:
                wc = _total([_residue(uc_ref, q, sl) for q in wrapped])
                wp = jnp.where(tile > 0, _total([_residue(up_ref, q, sl) for q in wrapped]), 0.0)
                acc = acc + jnp.where(j0, pltpu.roll(wp, 1, 0), pltpu.roll(wc, 1, 0))
            out.append(acc / jnp.where(first, float(min(r + 1, w)), float(w)) - own)
        p_ref[r * TI:(r + 1) * TI, :] = jnp.concatenate(out, axis=1).astype(p_ref.dtype)


def _pool_tokens_bwd(dp, nxt_ref, du_ref, tile):
    ji = lax.broadcasted_iota(jnp.int32, (TI, 1), 0)
    first = (tile == 0) & (ji == 0)
    piece = lambda g, r: dp[g][r * TI:(r + 1) * TI]
    dpc = [[piece(g, r) / jnp.where(first, float(min(r + 1, w)), float(w)) for r in range(NR)]
           for g, w in enumerate(POOL_WINDOWS)]
    for r in range(NR):
        out = []
        for g, w in enumerate(POOL_WINDOWS):
            sl = slice(g * PG, (g + 1) * PG)
            acc = _total([dpc[g][r + k] for k in range(w) if r + k < NR])
            wrapped = [r + k - NR for k in range(1, w) if r + k >= NR]
            if wrapped:
                wc = _total([dpc[g][q] for q in wrapped])
                wn = _total([nxt_ref[q * TI:(q + 1) * TI, sl] for q in wrapped])
                acc = acc + jnp.where(ji == TI - 1, pltpu.roll(wn, TI - 1, 0), pltpu.roll(wc, TI - 1, 0))
            out.append(acc - piece(g, r))
        du_ref[r * TI:(r + 1) * TI, :] = jnp.concatenate(out, axis=1).astype(du_ref.dtype)
    for r in range(NR):
        nxt_ref[r * TI:(r + 1) * TI, :] = jnp.concatenate([dpc[g][r] for g in range(len(POOL_WINDOWS))], axis=1)


def _pool_linear(pb, wpool_ref):
    return jnp.concatenate([_nn(pb[:, g * PG:(g + 1) * PG], wpool_ref[g]) for g in range(len(POOL_WINDOWS))], axis=1)


def _tok(width, col=0, rev=False):
    if rev:
        return pl.BlockSpec((TM, width), lambda i: (NT - 1 - i, col))
    return pl.BlockSpec((TM, width), lambda i: (i, col))


def _whole(shape):
    return pl.BlockSpec(shape, lambda i: (0,) * len(shape))


def _mix_fwd(h, o, lse, wpa, wpp, wpool, pscale, bgate):
    def body(o0_ref, o1_ref, o2_ref, l0_ref, l1_ref, l2_ref, za_ref, uc_ref, up_ref, zp_ref, gp_ref,
             wpa_ref, wpp_ref, wpool_ref, ps_ref, bg_ref, ya_ref, yp_ref, mg_ref, a_ref, b_ref, p_ref):
        i = pl.program_id(0)
        w0, w1, w2 = _group_weights(l0_ref[...], l1_ref[...], l2_ref[...])
        za = za_ref[...].astype(F32)
        silu_a = za * _sigmoid(za)
        for hh in range(NH):
            sl = slice(hh * HD, (hh + 1) * HD)
            c = slice(hh, hh + 1)
            oh = (w0[:, c] * o0_ref[:, sl].astype(F32) + w1[:, c] * o1_ref[:, sl].astype(F32)
                  + w2[:, c] * o2_ref[:, sl].astype(F32))
            ya_ref[:, sl] = (oh * silu_a[:, sl]).astype(BF16)
        _pool_tokens(up_ref, uc_ref, p_ref, i)
        zp = zp_ref[...].astype(F32)
        yp_ref[...] = (_pool_linear(p_ref[...], wpool_ref) * ps_ref[...] * (zp * _sigmoid(zp))).astype(BF16)
        a = _nn(ya_ref[...], wpa_ref[...])
        b = _nn(yp_ref[...], wpp_ref[...])
        a_ref[...] = a.astype(BF16)
        b_ref[...] = b.astype(BF16)
        gates = _sigmoid(gp_ref[...].astype(F32) + bg_ref[...])
        mg_ref[...] = (gates[:, :D] * a + gates[:, D:] * b).astype(BF16)

    u_prev = pl.BlockSpec((TM, AW), lambda i: (jnp.maximum(i - 1, 0), COL_U // AW))
    return pl.pallas_call(
        body, name="mix_fwd", grid=(NT,),
        in_specs=[_tok(AW)] * 3 + [_tok(NH)] * 3
        + [_tok(AW, COL_ZA // AW), _tok(AW, COL_U // AW), u_prev, _tok(AW, COL_ZP // AW), _tok(2 * D, COL_G // (2 * D))]
        + [_whole((AW, D)), _whole((AW, D)), _whole((4, PG, PG)), _whole((1, AW)), _whole((1, 2 * D))],
        out_specs=[_tok(AW), _tok(AW), _tok(D), _tok(D), _tok(D), _tok(AW)],
        out_shape=[jax.ShapeDtypeStruct((S, AW), BF16)] * 2 + [jax.ShapeDtypeStruct((S, D), BF16)] * 3
        + [jax.ShapeDtypeStruct((S, AW), BF16)],
        compiler_params=_cparams(1))(*o, *lse, h, h, h, h, h, wpa, wpp, wpool, pscale, bgate)


def _out_ln(merged, x, target, wout, gamma, beta):
    def body(mg_ref, x_ref, t_ref, w_ref, g_ref, b_ref, dr_ref, drb_ref, dm_ref, loss_ref, dg_ref, db_ref):
        i = pl.program_id(0)

        @pl.when(i == 0)
        def _():
            loss_ref[...] = jnp.zeros_like(loss_ref)
            dg_ref[...] = jnp.zeros_like(dg_ref)
            db_ref[...] = jnp.zeros_like(db_ref)

        r = ALPHA * x_ref[...] + _nn(mg_ref[...], w_ref[...])
        mu = jnp.mean(r, axis=-1, keepdims=True)
        rc = r - mu
        rstd = lax.rsqrt(jnp.mean(rc * rc, axis=-1, keepdims=True) + LN_EPS)
        xhat = rc * rstd
        err = xhat * g_ref[...] + b_ref[...] - t_ref[...]
        loss_ref[...] += 0.5 * jnp.sum(jnp.mean(err * err, axis=-1, keepdims=True), axis=0, keepdims=True)
        dy = err * (1.0 / D)
        dg_ref[...] += jnp.sum(dy * xhat, axis=0, keepdims=True)
        db_ref[...] += jnp.sum(dy, axis=0, keepdims=True)
        dxh = dy * g_ref[...]
        dr = rstd * (dxh - jnp.mean(dxh, axis=-1, keepdims=True)
                     - xhat * jnp.mean(dxh * xhat, axis=-1, keepdims=True))
        dr_ref[...] = dr
        drb_ref[...] = dr.astype(BF16)
        dm_ref[...] = _nt(drb_ref[...], w_ref[...]).astype(BF16)

    return pl.pallas_call(
        body, name="out_ln", grid=(NT,),
        in_specs=[_tok(D), _tok(D), _tok(D), _whole((D, D)), _whole((1, D)), _whole((1, D))],
        out_specs=[_tok(D), _tok(D), _tok(D), _whole((8, 128)), _whole((1, D)), _whole((1, D))],
        out_shape=[jax.ShapeDtypeStruct((S, D), F32), jax.ShapeDtypeStruct((S, D), BF16),
                   jax.ShapeDtypeStruct((S, D), BF16), jax.ShapeDtypeStruct((8, 128), F32),
                   jax.ShapeDtypeStruct((1, D), F32), jax.ShapeDtypeStruct((1, D), F32)],
        compiler_params=_cparams(1))(merged, x, target, wout, gamma, beta)


def _gate_bwd(dm, a, b, h, bgate):
    def body(dm_ref, a_ref, b_ref, gp_ref, bg_ref, dgp_ref, da_ref, db_ref, dbg_ref):
        @pl.when(pl.program_id(0) == 0)
        def _():
            dbg_ref[...] = jnp.zeros_like(dbg_ref)

        dm_ = dm_ref[...].astype(F32)
        gates = _sigmoid(gp_ref[...].astype(F32) + bg_ref[...])
        ga, gb = gates[:, :D], gates[:, D:]
        da_ref[...] = (dm_ * ga).astype(BF16)
        db_ref[...] = (dm_ * gb).astype(BF16)
        dgp = jnp.concatenate([dm_ * a_ref[...].astype(F32) * ga * (1.0 - ga),
                               dm_ * b_ref[...].astype(F32) * gb * (1.0 - gb)], axis=1)
        dgp_ref[...] = dgp.astype(BF16)
        dbg_ref[...] += jnp.sum(dgp, axis=0, keepdims=True)

    return pl.pallas_call(
        body, name="gate_bwd", grid=(NT,),
        in_specs=[_tok(D), _tok(D), _tok(D), _tok(2 * D, COL_G // (2 * D)), _whole((1, 2 * D))],
        out_specs=[_tok(2 * D, DH_G // (2 * D)), _tok(D), _tok(D), _whole((1, 2 * D))],
        out_shape=[jax.ShapeDtypeStruct((S, NW), BF16), jax.ShapeDtypeStruct((S, D), BF16),
                   jax.ShapeDtypeStruct((S, D), BF16), jax.ShapeDtypeStruct((1, 2 * D), F32)],
        compiler_params=_cparams(1))(dm, a, b, h, bgate)


def _mix_bwd(dh, da, db, h, o, lse, p, wpa, wpp, wpool, pscale):
    def body(_, da_ref, db_ref, o0_ref, o1_ref, o2_ref, l0_ref, l1_ref, l2_ref, za_ref, zp_ref, p_ref,
             wpa_ref, wpp_ref, wpool_ref, ps_ref,
             dh_ref, do0_ref, do1_ref, do2_ref, dl0_ref, dl1_ref, dl2_ref, dwp_ref, dps_ref,
             nxt_ref):
        i = pl.program_id(0)
        tile = NT - 1 - i
        dza_ref, du_ref, dzp_ref = (dh_ref.at[:, pl.ds(n * AW, AW)] for n in range(3))

        @pl.when(i == 0)
        def _():
            nxt_ref[...] = jnp.zeros_like(nxt_ref)
            dwp_ref[...] = jnp.zeros_like(dwp_ref)
            dps_ref[...] = jnp.zeros_like(dps_ref)

        dya = _nt(da_ref[...], wpa_ref[...])
        w0, w1, w2 = _group_weights(l0_ref[...], l1_ref[...], l2_ref[...])
        za = za_ref[...].astype(F32)
        sig = _sigmoid(za)
        silu_a = za * sig
        dsilu_a = sig * (1.0 + za * (1.0 - sig))
        for hh in range(NH):
            sl = slice(hh * HD, (hh + 1) * HD)
            c = slice(hh, hh + 1)
            oh = (w0[:, c] * o0_ref[:, sl].astype(F32) + w1[:, c] * o1_ref[:, sl].astype(F32)
                  + w2[:, c] * o2_ref[:, sl].astype(F32))
            doh = dya[:, sl] * silu_a[:, sl]
            dza_ref[:, sl] = (dya[:, sl] * oh * dsilu_a[:, sl]).astype(BF16)
            dot_ = jnp.sum(doh * oh, axis=-1, keepdims=True)
            do0_ref[:, sl] = (w0[:, c] * doh).astype(BF16)
            do1_ref[:, sl] = (w1[:, c] * doh).astype(BF16)
            do2_ref[:, sl] = (w2[:, c] * doh).astype(BF16)
            dl0_ref[:, c] = w0[:, c] * dot_
            dl1_ref[:, c] = w1[:, c] * dot_
            dl2_ref[:, c] = w2[:, c] * dot_
        dyp = _nt(db_ref[...], wpp_ref[...])
        pb = p_ref[...]
        pw = _pool_linear(pb, wpool_ref)
        zp = zp_ref[...].astype(F32)
        sigp = _sigmoid(zp)
        dypre = dyp * (zp * sigp)
        dzp_ref[...] = (dyp * (pw * ps_ref[...]) * (sigp * (1.0 + zp * (1.0 - sigp)))).astype(BF16)
        dps_ref[...] += jnp.sum(dypre * pw, axis=0, keepdims=True)
        dpw = (dypre * ps_ref[...]).astype(BF16)
        dp = []
        for g in range(len(POOL_WINDOWS)):
            sl = slice(g * PG, (g + 1) * PG)
            dwp_ref[g] += _tn(pb[:, sl], dpw[:, sl])
            dp.append(_nt(dpw[:, sl], wpool_ref[g]))
        _pool_tokens_bwd(dp, nxt_ref, du_ref, tile)

    r = functools.partial(_tok, rev=True)
    return pl.pallas_call(
        body, name="mix_bwd", grid=(NT,),
        in_specs=[ANY, r(D), r(D)] + [r(AW)] * 3 + [r(NH)] * 3 + [r(AW, COL_ZA // AW), r(AW, COL_ZP // AW), r(AW)]
        + [_whole((AW, D)), _whole((AW, D)), _whole((4, PG, PG)), _whole((1, AW))],
        out_specs=[r(3 * AW, DH_Z // (3 * AW))] + [r(AW)] * 3 + [r(NH)] * 3 + [_whole((4, PG, PG)), _whole((1, AW))],
        out_shape=[jax.ShapeDtypeStruct((S, NW), BF16)] + [jax.ShapeDtypeStruct((S, AW), BF16)] * 3
        + [jax.ShapeDtypeStruct((S, NH), F32)] * 3
        + [jax.ShapeDtypeStruct((4, PG, PG), F32), jax.ShapeDtypeStruct((1, AW), F32)],
        input_output_aliases={0: 0},
        scratch_shapes=[pltpu.VMEM((TM, AW), F32)],
        compiler_params=_cparams(1))(dh, da, db, *o, *lse, h, h, p, wpa, wpp, wpool, pscale)


def _adamw(w, g, m, v):
    m = B1 * m + (1.0 - B1) * g
    v = B2 * v + (1.0 - B2) * jnp.square(g)
    m_hat = m / (1.0 - B1 ** STEP)
    v_hat = v / (1.0 - B2 ** STEP)
    return -LR * (m_hat / (jnp.sqrt(v_hat) + EPS) + WD * w), m, v


def _adam_shard(name, q, l2, w, m, v, tr):
    rows = w.shape[0]

    def body(q_ref, l_ref, w_ref, m_ref, v_ref, g_out, d_out, m_out, v_out):
        g = q_ref[...].astype(F32)
        for k in range(3):
            g = g + l_ref[k].astype(F32)
        g_out[...] = g
        d_out[...], m_out[...], v_out[...] = _adamw(w_ref[...], g, m_ref[...], v_ref[...])

    blk = pl.BlockSpec((tr, D), lambda i: (i, 0))
    return pl.pallas_call(
        body, name=name, grid=(rows // tr,),
        in_specs=[pl.BlockSpec((None, tr, D), lambda i: (0, i, 0)), pl.BlockSpec((3, tr, D), lambda i: (0, i, 0)),
                  blk, blk, blk],
        out_specs=[blk] * 4, out_shape=[jax.ShapeDtypeStruct((rows, D), F32)] * 4,
        compiler_params=_cparams(1))(q, l2, w, m, v)


def _adam_replicated(gathered, w, m, v):
    def body(g_ref, w_ref, m_ref, v_ref, g_out, d_out, m_out, v_out):
        g = g_ref[0]
        for k in range(1, N_DEV):
            g = g + g_ref[k]
        g_out[...] = g
        d_out[...], m_out[...], v_out[...] = _adamw(w_ref[...], g, m_ref[...], v_ref[...])

    return pl.pallas_call(body, name="adam_replicated", out_shape=[jax.ShapeDtypeStruct((8, D), F32)] * 4,
                          compiler_params=pltpu.CompilerParams(vmem_limit_bytes=VMEM_LIMIT))(gathered, w, m, v)


def _pack_small(w_out, w_pa, w_pp, w_pool):
    return jnp.concatenate([w_out, w_pa.reshape(-1, D), w_pp.reshape(-1, D), w_pool.reshape(-1, D)], axis=0)


def _unpack_small(a):
    o = R_OUT
    return (a[:R_PA - o], a[R_PA - o:R_PP - o].reshape(AW, 256), a[R_PP - o:R_PL - o].reshape(AW, 256),
            a[R_PL - o:].reshape(4, 32, PG))


def _pack_vec(b_gate, gamma, beta, pscale, extra):
    z = jnp.zeros((D,), F32)
    return jnp.stack([b_gate[:D], b_gate[D:], gamma, beta, jnp.concatenate([pscale, z[:D - AW]]),
                      jnp.broadcast_to(extra, (D,)), z, z])


def _unpack_vec(a):
    return jnp.concatenate([a[0], a[1]])[None], a[4, :AW][None], a[2][None], a[3][None]


def kernel(x, w_in, b_gate, w_pool, pool_scale, w_proj_attn, w_proj_pool, w_out, ln_gamma, ln_beta, loss_target, m_w_in, m_b_gate, m_w_pool, m_pool_scale, m_w_proj_attn, m_w_proj_pool, m_w_out, m_ln_gamma, m_ln_beta, v_w_in, v_b_gate, v_w_pool, v_pool_scale, v_w_proj_attn, v_w_proj_pool, v_w_out, v_ln_gamma, v_ln_beta):
    coords = jnp.stack([lax.axis_index("x"), lax.axis_index("y"), lax.axis_index("c")]).astype(jnp.int32)
    x2, tgt = _permute_tokens(x[0]), _permute_tokens(loss_target[0])
    xb = x2.astype(BF16)
    xt = x2.T.astype(BF16)

    pack = jnp.concatenate([w_in[0].astype(BF16),
                            _pack_small(w_out[0], w_proj_attn[0], w_proj_pool[0], w_pool[0]).astype(BF16)], axis=0)
    h, gw = _ag_proj(_arrival_order(*coords), xb, pack)
    wout = gw[:, R_OUT:R_PA].reshape(D, D)
    wpa = gw[:, R_PA:R_PP].reshape(N_DEV, AW, 256).transpose(1, 0, 2).reshape(AW, D)
    wpp = gw[:, R_PP:R_PL].reshape(N_DEV, AW, 256).transpose(1, 0, 2).reshape(AW, D)
    wpool = gw[:, R_PL:].reshape(N_DEV, 4, 32, PG).transpose(1, 0, 2, 3).reshape(4, PG, PG)

    o, lse = zip(*[_attn_fwd(g, h) for g in range(len(DILATIONS))])
    ya, yp, merged, a, b, p = _mix_fwd(h, o, lse, wpa, wpp, wpool, pool_scale, b_gate)
    dr, drb, dm, loss_part, dgamma, dbeta = _out_ln(merged, x2, tgt, wout, ln_gamma, ln_beta)

    dh, da, db, dbgate = _gate_bwd(dm, a, b, h, b_gate)
    dh, do0, do1, do2, dl0, dl1, dl2, dwpool, dpscale = _mix_bwd(
        dh, da, db, h, o, lse, p, wpa, wpp, wpool, pool_scale)
    for g, (do_g, dl_g) in enumerate(zip((do0, do1, do2), (dl0, dl1, dl2))):
        dh = _attn_bwd(g, dh, h, do_g, lse[g], dl_g)

    slab = lambda n, k: (n // 2, 0, n % 2)
    part = _grad_w("grad_w_in", xt, dh, jax.ShapeDtypeStruct((N_DEV, R_ALL, D), BF16),
                   pl.BlockSpec((None, D, 1024), slab), 1024, 1024, b_block=_h_block)
    flat = lambda n, k: (0, n)
    d_wout = _grad_w("grad_w_out", merged.T, drb, jax.ShapeDtypeStruct((D, D), BF16),
                     pl.BlockSpec((D, 1024), flat), 1024, 1024)
    d_wpa = _grad_w("grad_w_pa", ya.T, da, jax.ShapeDtypeStruct((AW, D), BF16),
                    pl.BlockSpec((AW, 1024), flat), 1024, 1024)
    d_wpp = _grad_w("grad_w_pp", yp.T, db, jax.ShapeDtypeStruct((AW, D), BF16),
                    pl.BlockSpec((AW, 1024), flat), 1024, 1024)
    small = jnp.concatenate([
        d_wout.reshape(N_DEV, 256, D),
        d_wpa.reshape(AW, N_DEV, 256).transpose(1, 0, 2).reshape(N_DEV, -1, D),
        d_wpp.reshape(AW, N_DEV, 256).transpose(1, 0, 2).reshape(N_DEV, -1, D),
        dwpool.astype(BF16).reshape(4, N_DEV, 32, PG).transpose(1, 0, 2, 3).reshape(N_DEV, -1, D)], axis=1)
    part = lax.dynamic_update_slice(part, small, (0, R_OUT, 0))

    q = _pair_sum(coords, part, _rs_sibling(part))
    grad_x, l2 = _grad_x_rs(dh, gw, dr, q)
    g_in, d_in, m_in, v_in = _adam_shard("adam_w_in", q, l2, w_in[0], m_w_in[0], v_w_in[0], 256)
    outs_small = _adam_shard(
        "adam_small", q[:, R_OUT:], l2[:, R_OUT:],
        _pack_small(w_out[0], w_proj_attn[0], w_proj_pool[0], w_pool[0]),
        _pack_small(m_w_out[0], m_w_proj_attn[0], m_w_proj_pool[0], m_w_pool[0]),
        _pack_small(v_w_out[0], v_w_proj_attn[0], v_w_proj_pool[0], v_w_pool[0]), 176)
    small_parts = [_unpack_small(t) for t in outs_small]

    vec = _pack_vec(dbgate[0], dgamma[0], dbeta[0], dpscale[0], loss_part[0, 0])
    outs_vec = _adam_replicated(
        _all_gather_direct("ag_vec", vec),
        _pack_vec(b_gate[0], ln_gamma[0], ln_beta[0], pool_scale[0], 0.0),
        _pack_vec(m_b_gate[0], m_ln_gamma[0], m_ln_beta[0], m_pool_scale[0], 0.0),
        _pack_vec(v_b_gate[0], v_ln_gamma[0], v_ln_beta[0], v_pool_scale[0], 0.0))
    vec_parts = [_unpack_vec(t) for t in outs_vec]
    loss = outs_vec[0][5, 0]

    def leaves(kind, big):
        out, pa, pp, pool = small_parts[kind]
        bg, ps, gm, bt = vec_parts[kind]
        return [big[None], bg, pool[None], ps, pa[None], pp[None], out[None], gm, bt]

    return (loss, _permute_tokens(grad_x)[None], *leaves(0, g_in), *leaves(1, d_in), *leaves(2, m_in), *leaves(3, v_in))
```

```python
import functools

import jax
import jax.numpy as jnp
from jax import lax
from jax.experimental import pallas as pl
from jax.experimental.pallas import tpu as pltpu

F32 = jnp.float32
BF16 = jnp.bfloat16

S = 4096
D = 2048
NW = 16384
AW = 1024
HD = 128
NH = 8
QB = 128
NBLK = S // QB
DILATIONS = (1, 4, 16)
POOL_WINDOWS = (2, 4, 8, 16)
PG = 256
N_DEV = 8
COL_Q, COL_K, COL_V = 0, 3 * AW, 6 * AW
COL_ZA, COL_U, COL_ZP, COL_G = 9 * AW, 10 * AW, 11 * AW, 12 * AW
DH_Z, DH_G = COL_ZA, COL_G
ALPHA = 2.0 ** 0.25
LN_EPS = 1e-5
NEG_INF = -1e30
LR, B1, B2, EPS, WD, STEP = 0.001, 0.9, 0.999, 1e-08, 0.01, 10
R_IN, R_OUT, R_PA, R_PP, R_PL = 0, 2048, 2304, 2432, 2560
R_ALL = 2576
R_SMALL = R_ALL - R_OUT
VMEM_LIMIT = 56 * 1024 * 1024
MESH = pl.DeviceIdType.MESH
ANY = pl.BlockSpec(memory_space=pl.ANY)


def _cparams(n_axes):
    return pltpu.CompilerParams(dimension_semantics=("arbitrary",) * n_axes, vmem_limit_bytes=VMEM_LIMIT)


def _sigmoid(z):
    return 1.0 / (1.0 + jnp.exp(-z))


def _nt(a, b):
    return lax.dot_general(a, b, (((1,), (1,)), ((), ())), preferred_element_type=F32)


def _tn(a, b):
    return lax.dot_general(a, b, (((0,), (0,)), ((), ())), preferred_element_type=F32)


def _nn(a, b):
    return jnp.dot(a, b, preferred_element_type=F32)


def _lin(x, y, c):
    return 4 * x + 2 * y + c


def _flip(v, f):
    return 1 - v if f else v


def _exchange(name, src, plan, *, dst_shape=None, local_dst=None):
    n = len(plan)
    in_place = dst_shape is None
    out_sds = jax.ShapeDtypeStruct(src.shape, src.dtype) if in_place else dst_shape

    def body(src_ref, dst_ref, send_sems, recv_sems, local_sem):
        x, y, c = lax.axis_index("x"), lax.axis_index("y"), lax.axis_index("c")

        def copy(k, sender):
            flip, src_index, dst_index = plan[k]
            sx, sy, sc = sender
            to = (_flip(sx, flip[0]), _flip(sy, flip[1]), _flip(sc, flip[2]))
            s = src_ref if src_index is None else src_ref.at[src_index(sx, sy, sc)]
            return pltpu.make_async_remote_copy(
                src_ref=s, dst_ref=dst_ref.at[dst_index(sx, sy, sc)],
                send_sem=send_sems.at[k], recv_sem=recv_sems.at[k],
                device_id=to, device_id_type=MESH)

        me = (x, y, c)
        if local_dst is not None:
            mine = pltpu.make_async_copy(src_ref, dst_ref.at[local_dst(x, y, c)], local_sem)
            mine.start()
        sends = [copy(k, me) for k in range(n)]
        for cp in sends:
            cp.start()
        for k in range(n):
            flip = plan[k][0]
            copy(k, (_flip(x, flip[0]), _flip(y, flip[1]), _flip(c, flip[2]))).wait_recv()
        for cp in sends:
            cp.wait_send()
        if local_dst is not None:
            mine.wait()

    return pl.pallas_call(
        body, name=name, out_shape=out_sds, in_specs=[ANY], out_specs=ANY,
        input_output_aliases={0: 0} if in_place else {},
        scratch_shapes=[pltpu.SemaphoreType.DMA((n,)), pltpu.SemaphoreType.DMA((n,)),
                        pltpu.SemaphoreType.DMA(())],
    )(src)


FLIP_C, FLIP_X, FLIP_Y, FLIP_XY = (0, 0, 1), (1, 0, 0), (0, 1, 0), (1, 1, 0)
CHIP_FLIPS = ((0, 0), (1, 0), (0, 1), (1, 1))


AG_PIECES = ((pl.ds(R_IN, D), pl.ds(0, 1024)), (pl.ds(R_IN, D), pl.ds(1024, 1024)),
             (pl.ds(R_OUT, R_PA - R_OUT), pl.ds(0, D)), (pl.ds(R_PA, R_ALL - R_PA), pl.ds(0, D)))
N_PIECES = len(AG_PIECES)
SIB, TO_X, TO_Y, ON, PASS_X, PASS_Y, PASS_D = range(7)
AG_TILES = ((0, 0), (0, 1), (1, 0), (1, 1), (2, 0), (4, 0), (3, 0), (5, 0),
            (2, 1), (4, 1), (3, 1), (5, 1), (6, 0), (6, 1), (7, 0), (7, 1))
W, G = "wait", "go"
AG_STEPS = {
    2: [(W, SIB, 0)], 3: [(W, SIB, 1)],
    4: [(W, TO_X, 0), (G, ON, 0), (G, PASS_X, 0)], 5: [(W, TO_Y, 0), (G, PASS_Y, 0)],
    6: [(W, PASS_X, 0)], 7: [(W, PASS_Y, 0)],
    8: [(W, TO_X, 1), (G, PASS_X, 1)],
    9: [(W, TO_Y, 1), (G, ON, 1), (G, PASS_Y, 1), (G, TO_X, 2), (G, TO_X, 3), (G, TO_Y, 2), (G, TO_Y, 3)],
    10: [(W, PASS_X, 1)], 11: [(W, PASS_Y, 1)],
    12: [(W, ON, 0), (G, PASS_D, 0)], 13: [(W, ON, 1), (G, PASS_D, 1)],
    14: [(W, PASS_D, 0), (W, TO_X, 2), (G, ON, 2), (G, PASS_X, 2), (W, TO_X, 3), (G, PASS_X, 3),
         (W, TO_Y, 2), (G, PASS_Y, 2), (W, TO_Y, 3), (G, ON, 3), (G, PASS_Y, 3)],
    15: [(W, PASS_D, 1)],
}
AG_LAST = [(W, SIB, 2), (W, SIB, 3), (W, ON, 2), (G, PASS_D, 2), (W, ON, 3), (G, PASS_D, 3),
           (W, PASS_X, 2), (W, PASS_X, 3), (W, PASS_Y, 2), (W, PASS_Y, 3), (W, PASS_D, 2), (W, PASS_D, 3)]


def _arrival_order(x, y, c):
    chips = [(x, y), (1 - x, y), (x, 1 - y), (1 - x, 1 - y)]
    return jnp.stack([_lin(px, py, pc) for px, py in chips for pc in (c, 1 - c)]).astype(jnp.int32)


def _ag_proj(order, xb, pack):
    tm, tn = 1024, 1024
    nrow, ntile = S // tm, len(AG_TILES)
    slabs = jnp.stack([order[pos] for pos, _ in AG_TILES])
    cols = jnp.stack([2 * order[pos] + half for pos, half in AG_TILES])

    def body(cols_ref, slabs_ref, x_ref, pack_ref, h_ref, gw_ref, wbuf, wsem, send_sems, recv_sems, local_sem):
        t, i = pl.program_id(0), pl.program_id(1)
        x, y, c = lax.axis_index("x"), lax.axis_index("y"), lax.axis_index("c")
        me = _lin(x, y, c)
        dev = {"sib": (x, y, 1 - c), "x": (1 - x, y, c), "y": (x, 1 - y, c), "d": (1 - x, 1 - y, c)}

        def slab_of(name, other_core=False):
            px, py, pc = dev[name]
            return _lin(px, py, 1 - pc if other_core else pc)

        def rdma(slab, kind, piece, to, from_pack=False):
            k = kind * N_PIECES + piece
            there = gw_ref.at[(slab, *AG_PIECES[piece])]
            return pltpu.make_async_remote_copy(
                src_ref=pack_ref.at[AG_PIECES[piece]] if from_pack else there, dst_ref=there,
                send_sem=send_sems.at[k], recv_sem=recv_sems.at[k], device_id=dev[to], device_id_type=MESH)

        def mine(kind, piece):
            if kind in (SIB, TO_X, TO_Y):
                return rdma(me, kind, piece, ("sib", "x", "y")[kind], from_pack=True)
            if kind == ON:
                frm, to = ("x", "y") if piece % 2 == 0 else ("y", "x")
                return rdma(slab_of(frm), kind, piece, to)
            return rdma(slab_of({PASS_X: "x", PASS_Y: "y", PASS_D: "d"}[kind]), kind, piece, "sib")

        def landing(kind, piece):
            slab = {SIB: slab_of("sib"), TO_X: slab_of("x"), TO_Y: slab_of("y"), ON: slab_of("d"),
                    PASS_X: slab_of("x", True), PASS_Y: slab_of("y", True), PASS_D: slab_of("d", True)}[kind]
            return rdma(slab, kind, piece, "sib")

        def run(steps):
            for what, kind, piece in steps:
                if what == W:
                    landing(kind, piece).wait_recv()
                else:
                    mine(kind, piece).start()

        local = pltpu.make_async_copy(pack_ref, gw_ref.at[me], local_sem)

        def fetch(slab, half, slot):
            src = pack_ref.at[AG_PIECES[half]] if slab is None else gw_ref.at[(slab, *AG_PIECES[half])]
            return pltpu.make_async_copy(src, wbuf.at[slot], wsem.at[slot])

        @pl.when((t == 0) & (i == 0))
        def _():
            local.start()
            run([(G, kind, piece) for piece in (0, 1) for kind in (TO_X, TO_Y, SIB)] + [(G, SIB, 2), (G, SIB, 3)])
            first = fetch(None, 0, 0)
            first.start()
            first.wait()

        for nxt in range(1, ntile):
            @pl.when((t == nxt - 1) & (i == nrow - 1))
            def _(nxt=nxt):
                run(AG_STEPS.get(nxt, []))
                fetch(None if AG_TILES[nxt][0] == 0 else slabs_ref[nxt], AG_TILES[nxt][1], nxt % 2).start()

        for slot in (0, 1):
            @pl.when(t % 2 == slot)
            def _(slot=slot):
                @pl.when((i == 0) & (t > 0))
                def _():
                    fetch(None, 0, slot).wait()
                h_ref[...] = _nn(x_ref[...], wbuf[slot]).astype(h_ref.dtype)

        @pl.when((t == ntile - 1) & (i == nrow - 1))
        def _():
            run(AG_LAST)
            for kind in range(7):
                for piece in range(N_PIECES):
                    mine(kind, piece).wait_send()
            local.wait()

    n_sem = 7 * N_PIECES
    grid_spec = pltpu.PrefetchScalarGridSpec(
        num_scalar_prefetch=2, grid=(ntile, nrow),
        in_specs=[pl.BlockSpec((tm, D), lambda t, i, cols, slabs: (i, 0)), ANY],
        out_specs=[pl.BlockSpec((tm, tn), lambda t, i, cols, slabs: (i, cols[t])), ANY],
        scratch_shapes=[pltpu.VMEM((2, D, tn), BF16), pltpu.SemaphoreType.DMA((2,)),
                        pltpu.SemaphoreType.DMA((n_sem,)), pltpu.SemaphoreType.DMA((n_sem,)),
                        pltpu.SemaphoreType.DMA(())])
    return pl.pallas_call(
        body, name="ag_proj", grid_spec=grid_spec,
        out_shape=[jax.ShapeDtypeStruct((S, NW), BF16), jax.ShapeDtypeStruct((N_DEV, R_ALL, D), BF16)],
        compiler_params=_cparams(2))(cols, slabs, xb, pack)


def _all_gather_direct(name, vec):
    own = lambda x, y, c: _lin(x, y, c)
    flips = [(fx, fy, fc) for fx in (0, 1) for fy in (0, 1) for fc in (0, 1) if (fx, fy, fc) != (0, 0, 0)]
    return _exchange(name, vec, [(f, None, own) for f in flips],
                     dst_shape=jax.ShapeDtypeStruct((N_DEV,) + vec.shape, vec.dtype), local_dst=own)


def _rs_sibling(p):
    plan = [(FLIP_C, (lambda x, y, c, f=f: _lin(_flip(x, f[0]), _flip(y, f[1]), 1 - c)),
             (lambda x, y, c, k=k: k)) for k, f in enumerate(CHIP_FLIPS)]
    return _exchange("rs_sibling", p, plan, dst_shape=jax.ShapeDtypeStruct((4,) + p.shape[1:], p.dtype))


def _pair_sum(coords, p, l1):
    tr = 368
    nr = R_ALL // tr

    def body(crd, p_ref, l_ref, q_ref):
        q_ref[...] = (p_ref[...].astype(F32) + l_ref[...].astype(F32)).astype(q_ref.dtype)

    def p_map(k, i, crd):
        fx, fy = k % 2, k // 2
        px = crd[0] + fx - 2 * fx * crd[0]
        py = crd[1] + fy - 2 * fy * crd[1]
        return (_lin(px, py, crd[2]), i, 0)

    grid_spec = pltpu.PrefetchScalarGridSpec(
        num_scalar_prefetch=1, grid=(4, nr),
        in_specs=[pl.BlockSpec((None, tr, D), p_map),
                  pl.BlockSpec((None, tr, D), lambda k, i, crd: (k, i, 0))],
        out_specs=pl.BlockSpec((None, tr, D), lambda k, i, crd: (k, i, 0)))
    return pl.pallas_call(body, name="pair_sum", grid_spec=grid_spec,
                          out_shape=jax.ShapeDtypeStruct((4, R_ALL, D), BF16),
                          compiler_params=_cparams(2))(coords, p, l1)


def _h_block(k):
    return jnp.where(k < 9, (k % 3) * 3 + k // 3, k)


def _grad_x_rs(dh, g, dr, q):
    tm, tk = 512, 1024
    ni, nk = S // tm, NW // tk

    def body(dh_ref, w_ref, dr_ref, q_ref, o_ref, l2_ref, send_sems, recv_sems):
        i, k = pl.program_id(0), pl.program_id(1)
        x, y, c = lax.axis_index("x"), lax.axis_index("y"), lax.axis_index("c")

        def copy(j):
            f = CHIP_FLIPS[j + 1]
            return pltpu.make_async_remote_copy(
                src_ref=q_ref.at[j + 1], dst_ref=l2_ref.at[j], send_sem=send_sems.at[j], recv_sem=recv_sems.at[j],
                device_id=(_flip(x, f[0]), _flip(y, f[1]), c), device_id_type=MESH)

        @pl.when((i == 0) & (k == 0))
        def _():
            for j in range(3):
                copy(j).start()

        @pl.when(k == 0)
        def _():
            o_ref[...] = ALPHA * dr_ref[...]

        o_ref[...] += _nt(dh_ref[...], w_ref[...])

        @pl.when((i == ni - 1) & (k == nk - 1))
        def _():
            for j in range(3):
                copy(j).wait_recv()
            for j in range(3):
                copy(j).wait_send()

    return pl.pallas_call(
        body, name="grad_x_rs", grid=(ni, nk),
        in_specs=[pl.BlockSpec((tm, tk), lambda i, k: (i, k)),
                  pl.BlockSpec((None, D, tk), lambda i, k: (_h_block(k) // 2, 0, _h_block(k) % 2)),
                  pl.BlockSpec((tm, D), lambda i, k: (i, 0)), ANY],
        out_specs=[pl.BlockSpec((tm, D), lambda i, k: (i, 0)), ANY],
        out_shape=[jax.ShapeDtypeStruct((S, D), F32), jax.ShapeDtypeStruct((3,) + q.shape[1:], q.dtype)],
        scratch_shapes=[pltpu.SemaphoreType.DMA((3,)), pltpu.SemaphoreType.DMA((3,))],
        compiler_params=_cparams(2))(dh, g, dr, q)


def _grad_w(name, at, b, out_shape, out_spec, tn, tk, b_block=lambda n: n):
    m, k_all = at.shape
    n_all = b.shape[1]
    nk = k_all // tk

    def body(a_ref, b_ref, o_ref, acc_ref):
        k = pl.program_id(1)

        @pl.when(k == 0)
        def _():
            acc_ref[...] = jnp.zeros_like(acc_ref)

        acc_ref[...] += _nn(a_ref[...], b_ref[...])

        @pl.when(k == nk - 1)
        def _():
            o_ref[...] = acc_ref[...].astype(o_ref.dtype)

    return pl.pallas_call(
        body, name=name, grid=(n_all // tn, nk),
        in_specs=[pl.BlockSpec((m, tk), lambda n, k: (0, k)),
                  pl.BlockSpec((tk, tn), lambda n, k: (k, b_block(n)))],
        out_specs=out_spec, out_shape=out_shape,
        scratch_shapes=[pltpu.VMEM((m, tn), F32)], compiler_params=_cparams(2))(at, b)


NR = 16
TI = 16
TM = NR * TI
NT = S // TM
ATT_QB = (256, 128, 128)
ATT_NB = (16, 8, 2)
ATT_BLOCKS = (16, 32, 32)


def _permute_tokens(a):
    return a.reshape(NT, TI, NR, a.shape[-1]).transpose(0, 2, 1, 3).reshape(a.shape)


def _attn_shape(g, c):
    if g == 0:
        return (S, c)
    if g == 1:
        return (NT, 4, 4, TI, c)
    return (NT, NR, TI, c)


def _attn_view(g, a):
    return a.reshape(_attn_shape(g, a.shape[-1]))


def _attn_spec(g, width, col, blk):
    if g == 0:
        return pl.BlockSpec((TM, width), lambda b: (blk(b), col))
    if g == 1:
        return pl.BlockSpec((2, 4, None, TI, width), lambda b: (blk(b) % 8, 0, blk(b) // 8, 0, col))
    return pl.BlockSpec((8, None, TI, width), lambda b: (blk(b) % 2, blk(b) // 2, 0, col))


def _pieces(g):
    if g == 1:
        return [(t, m) for t in range(2) for m in range(4)]
    return [(t,) for t in range(8)]


def _get(g, ref, sl):
    if g == 0:
        return ref[:, sl]
    return jnp.concatenate([ref[(*p, slice(None), sl)] for p in _pieces(g)], axis=0)


def _put(g, ref, sl, val):
    if g == 0:
        ref[:, sl] = val
    else:
        for n, p in enumerate(_pieces(g)):
            ref[(*p, slice(None), sl)] = val[TI * n:TI * (n + 1)]


def _block_pos(g, a):
    if g == 0:
        return 16 * (a % 16) + a // 16
    if g == 1:
        return 64 * (a // 64) + 4 * (a % 16) + (a // 16) % 4
    return a


def _attn_mask(g, n):
    qb = ATT_QB[g]
    qa = lax.broadcasted_iota(jnp.int32, (qb, 2 * qb), 0)
    kc = lax.broadcasted_iota(jnp.int32, (qb, 2 * qb), 1)
    cur = kc >= qb
    dist = _block_pos(g, qa) - _block_pos(g, kc % qb) + jnp.where(cur, 0, qb)
    return (dist >= 0) & (dist <= QB) & (cur | (n > 0))


def _qkv_specs(g, clamp):
    cur = lambda col: _attn_spec(g, AW, col, clamp)
    prev = lambda col: _attn_spec(g, AW, col, lambda b: jnp.maximum(clamp(b) - 1, 0))
    qc, kc, vc = (c // AW + g for c in (COL_Q, COL_K, COL_V))
    return [cur(qc), cur(kc), prev(kc), cur(vc), prev(vc)]


def _attn_fwd(g, h):
    scale = HD ** -0.5
    hv = _attn_view(g, h)

    def body(q_ref, kc_ref, kp_ref, vc_ref, vp_ref, o_ref, l_ref):
        valid = _attn_mask(g, pl.program_id(0) % ATT_NB[g])
        for hh in range(NH):
            sl = slice(hh * HD, (hh + 1) * HD)
            kh = jnp.concatenate([_get(g, kp_ref, sl), _get(g, kc_ref, sl)], axis=0)
            vh = jnp.concatenate([_get(g, vp_ref, sl), _get(g, vc_ref, sl)], axis=0)
            s = jnp.where(valid, _nt(_get(g, q_ref, sl), kh) * scale, NEG_INF)
            m = jnp.max(s, axis=-1, keepdims=True)
            e = jnp.exp(s - m)
            den = jnp.sum(e, axis=-1, keepdims=True)
            _put(g, o_ref, sl, (_nn(e.astype(BF16), vh) / den).astype(o_ref.dtype))
            _put(g, l_ref, slice(hh, hh + 1), m + jnp.log(den))

    same = lambda b: b
    o, lse = pl.pallas_call(
        body, name=f"attn_fwd_{g}", grid=(ATT_BLOCKS[g],),
        in_specs=_qkv_specs(g, same),
        out_specs=[_attn_spec(g, AW, 0, same), _attn_spec(g, NH, 0, same)],
        out_shape=[jax.ShapeDtypeStruct(_attn_shape(g, AW), BF16), jax.ShapeDtypeStruct(_attn_shape(g, NH), F32)],
        compiler_params=_cparams(1))(hv, hv, hv, hv, hv)
    return o.reshape(S, AW), lse.reshape(S, NH)


def _attn_bwd(g, dh, h, do, lse, delta):
    scale = HD ** -0.5
    qb = ATT_QB[g]
    last = ATT_BLOCKS[g] - 1
    clamp = lambda b: jnp.minimum(b, last)
    behind = lambda b: jnp.maximum(b - 1, 0)
    hv = _attn_view(g, h)

    def body(q_ref, kc_ref, kp_ref, vc_ref, vp_ref, do_ref, l_ref, dl_ref, _, dh_ref, cq_ref, ck_ref, cv_ref):
        b = pl.program_id(0)

        def write(col, val):
            _put(g, dh_ref, slice(col, col + HD), val.astype(dh_ref.dtype))

        @pl.when(b == 0)
        def _():
            cq_ref[...] = jnp.zeros_like(cq_ref)
            ck_ref[...] = jnp.zeros_like(ck_ref)
            cv_ref[...] = jnp.zeros_like(cv_ref)

        @pl.when(b <= last)
        def _():
            valid = _attn_mask(g, b % ATT_NB[g])
            for hh in range(NH):
                sl = slice(hh * HD, (hh + 1) * HD)
                one = slice(hh, hh + 1)
                qh, doh = _get(g, q_ref, sl), _get(g, do_ref, sl)
                kh = jnp.concatenate([_get(g, kp_ref, sl), _get(g, kc_ref, sl)], axis=0)
                vh = jnp.concatenate([_get(g, vp_ref, sl), _get(g, vc_ref, sl)], axis=0)
                s = _nt(qh, kh) * scale
                p = jnp.where(valid, jnp.exp(s - _get(g, l_ref, one)), 0.0)
                ds = p * (_nt(doh, vh) - _get(g, dl_ref, one))
                dsb = (ds * scale).astype(BF16)
                dk2 = _tn(dsb, qh)
                dv2 = _tn(p.astype(BF16), doh)
                write(hh * HD, cq_ref[:, sl])
                write(AW + hh * HD, ck_ref[:, sl] + dk2[:qb])
                write(2 * AW + hh * HD, cv_ref[:, sl] + dv2[:qb])
                cq_ref[:, sl] = _nn(dsb, kh)
                ck_ref[:, sl] = dk2[qb:]
                cv_ref[:, sl] = dv2[qb:]

        @pl.when(b > last)
        def _():
            for hh in range(NH):
                sl = slice(hh * HD, (hh + 1) * HD)
                write(hh * HD, cq_ref[:, sl])
                write(AW + hh * HD, ck_ref[:, sl])
                write(2 * AW + hh * HD, cv_ref[:, sl])

    out = pl.pallas_call(
        body, name=f"attn_bwd_{g}", grid=(ATT_BLOCKS[g] + 1,),
        in_specs=_qkv_specs(g, clamp) + [_attn_spec(g, AW, 0, clamp), _attn_spec(g, NH, 0, clamp),
                                         _attn_spec(g, NH, 0, clamp), ANY],
        out_specs=_attn_spec(g, 3 * AW, g, behind),
        out_shape=jax.ShapeDtypeStruct(_attn_shape(g, NW), BF16),
        input_output_aliases={8: 0},
        scratch_shapes=[pltpu.VMEM((qb, AW), F32)] * 3,
        compiler_params=_cparams(1))(hv, hv, hv, hv, hv, _attn_view(g, do), _attn_view(g, lse), _attn_view(g, delta),
                                     _attn_view(g, dh))
    return out.reshape(S, NW)


def _group_weights(l0, l1, l2):
    m = jnp.maximum(jnp.maximum(l0, l1), l2)
    e0, e1, e2 = jnp.exp(l0 - m), jnp.exp(l1 - m), jnp.exp(l2 - m)
    inv = 1.0 / (e0 + e1 + e2)
    return e0 * inv, e1 * inv, e2 * inv


def _residue(ref, r, sl):
    return ref[r * TI:(r + 1) * TI, sl].astype(F32)


def _total(parts):
    return functools.reduce(lambda x, y: x + y, parts)


def _pool_tokens(up_ref, uc_ref, p_ref, tile):
    j0 = lax.broadcasted_iota(jnp.int32, (TI, 1), 0) == 0
    first = (tile == 0) & j0
    for r in range(NR):
        out = []
        for g, w in enumerate(POOL_WINDOWS):
            sl = slice(g * PG, (g + 1) * PG)
            own = _residue(uc_ref, r, sl)
            acc = _total([own] + [_residue(uc_ref, r - k, sl) for k in range(1, min(r, w - 1) + 1)])
            wrapped = [NR + r - k for k in range(r + 1, w)]
            if wrapped:
                wc = _total([_residue(uc_ref, q, sl) for q in wrapped])
                wp = jnp.where(tile > 0, _total([_residue(up_ref, q, sl) for q in wrapped]), 0.0)
                acc = acc + jnp.where(j0, pltpu.roll(wp, 1, 0), pltpu.roll(wc, 1, 0))
            out.append(acc / jnp.where(first, float(min(r + 1, w)), float(w)) - own)
        p_ref[r * TI:(r + 1) * TI, :] = jnp.concatenate(out, axis=1).astype(p_ref.dtype)


def _pool_tokens_bwd(dp, nxt_ref, du_ref, tile):
    ji = lax.broadcasted_iota(jnp.int32, (TI, 1), 0)
    first = (tile == 0) & (ji == 0)
    piece = lambda g, r: dp[g][r * TI:(r + 1) * TI]
    dpc = [[piece(g, r) / jnp.where(first, float(min(r + 1, w)), float(w)) for r in range(NR)]
           for g, w in enumerate(POOL_WINDOWS)]
    for r in range(NR):
        out = []
        for g, w in enumerate(POOL_WINDOWS):
            sl = slice(g * PG, (g + 1) * PG)
            acc = _total([dpc[g][r + k] for k in range(w) if r + k < NR])
            wrapped = [r + k - NR for k in range(1, w) if r + k >= NR]
            if wrapped:
                wc = _total([dpc[g][q] for q in wrapped])
                wn = _total([nxt_ref[q * TI:(q + 1) * TI, sl] for q in wrapped])
                acc = acc + jnp.where(ji == TI - 1, pltpu.roll(wn, TI - 1, 0), pltpu.roll(wc, TI - 1, 0))
            out.append(acc - piece(g, r))
        du_ref[r * TI:(r + 1) * TI, :] = jnp.concatenate(out, axis=1).astype(du_ref.dtype)
    for r in range(NR):
        nxt_ref[r * TI:(r + 1) * TI, :] = jnp.concatenate([dpc[g][r] for g in range(len(POOL_WINDOWS))], axis=1)


def _pool_linear(pb, wpool_ref):
    return jnp.concatenate([_nn(pb[:, g * PG:(g + 1) * PG], wpool_ref[g]) for g in range(len(POOL_WINDOWS))], axis=1)


def _tok(width, col=0, rev=False):
    if rev:
        return pl.BlockSpec((TM, width), lambda i: (NT - 1 - i, col))
    return pl.BlockSpec((TM, width), lambda i: (i, col))


def _whole(shape):
    return pl.BlockSpec(shape, lambda i: (0,) * len(shape))


def _mix_fwd(h, o, lse, wpa, wpp, wpool, pscale, bgate):
    def body(o0_ref, o1_ref, o2_ref, l0_ref, l1_ref, l2_ref, za_ref, uc_ref, up_ref, zp_ref, gp_ref,
             wpa_ref, wpp_ref, wpool_ref, ps_ref, bg_ref, ya_ref, yp_ref, mg_ref, a_ref, b_ref, p_ref):
        i = pl.program_id(0)
        w0, w1, w2 = _group_weights(l0_ref[...], l1_ref[...], l2_ref[...])
        za = za_ref[...].astype(F32)
        silu_a = za * _sigmoid(za)
        for hh in range(NH):
            sl = slice(hh * HD, (hh + 1) * HD)
            c = slice(hh, hh + 1)
            oh = (w0[:, c] * o0_ref[:, sl].astype(F32) + w1[:, c] * o1_ref[:, sl].astype(F32)
                  + w2[:, c] * o2_ref[:, sl].astype(F32))
            ya_ref[:, sl] = (oh * silu_a[:, sl]).astype(BF16)
        _pool_tokens(up_ref, uc_ref, p_ref, i)
        zp = zp_ref[...].astype(F32)
        yp_ref[...] = (_pool_linear(p_ref[...], wpool_ref) * ps_ref[...] * (zp * _sigmoid(zp))).astype(BF16)
        a = _nn(ya_ref[...], wpa_ref[...])
        b = _nn(yp_ref[...], wpp_ref[...])
        a_ref[...] = a.astype(BF16)
        b_ref[...] = b.astype(BF16)
        gates = _sigmoid(gp_ref[...].astype(F32) + bg_ref[...])
        mg_ref[...] = (gates[:, :D] * a + gates[:, D:] * b).astype(BF16)

    u_prev = pl.BlockSpec((TM, AW), lambda i: (jnp.maximum(i - 1, 0), COL_U // AW))
    return pl.pallas_call(
        body, name="mix_fwd", grid=(NT,),
        in_specs=[_tok(AW)] * 3 + [_tok(NH)] * 3
        + [_tok(AW, COL_ZA // AW), _tok(AW, COL_U // AW), u_prev, _tok(AW, COL_ZP // AW), _tok(2 * D, COL_G // (2 * D))]
        + [_whole((AW, D)), _whole((AW, D)), _whole((4, PG, PG)), _whole((1, AW)), _whole((1, 2 * D))],
        out_specs=[_tok(AW), _tok(AW), _tok(D), _tok(D), _tok(D), _tok(AW)],
        out_shape=[jax.ShapeDtypeStruct((S, AW), BF16)] * 2 + [jax.ShapeDtypeStruct((S, D), BF16)] * 3
        + [jax.ShapeDtypeStruct((S, AW), BF16)],
        compiler_params=_cparams(1))(*o, *lse, h, h, h, h, h, wpa, wpp, wpool, pscale, bgate)


def _out_ln(merged, x, target, wout, gamma, beta):
    def body(mg_ref, x_ref, t_ref, w_ref, g_ref, b_ref, dr_ref, drb_ref, dm_ref, loss_ref, dg_ref, db_ref):
        i = pl.program_id(0)

        @pl.when(i == 0)
        def _():
            loss_ref[...] = jnp.zeros_like(loss_ref)
            dg_ref[...] = jnp.zeros_like(dg_ref)
            db_ref[...] = jnp.zeros_like(db_ref)

        r = ALPHA * x_ref[...] + _nn(mg_ref[...], w_ref[...])
        mu = jnp.mean(r, axis=-1, keepdims=True)
        rc = r - mu
        rstd = lax.rsqrt(jnp.mean(rc * rc, axis=-1, keepdims=True) + LN_EPS)
        xhat = rc * rstd
        err = xhat * g_ref[...] + b_ref[...] - t_ref[...]
        loss_ref[...] += 0.5 * jnp.sum(jnp.mean(err * err, axis=-1, keepdims=True), axis=0, keepdims=True)
        dy = err * (1.0 / D)
        dg_ref[...] += jnp.sum(dy * xhat, axis=0, keepdims=True)
        db_ref[...] += jnp.sum(dy, axis=0, keepdims=True)
        dxh = dy * g_ref[...]
        dr = rstd * (dxh - jnp.mean(dxh, axis=-1, keepdims=True)
                     - xhat * jnp.mean(dxh * xhat, axis=-1, keepdims=True))
        dr_ref[...] = dr
        drb_ref[...] = dr.astype(BF16)
        dm_ref[...] = _nt(drb_ref[...], w_ref[...]).astype(BF16)

    return pl.pallas_call(
        body, name="out_ln", grid=(NT,),
        in_specs=[_tok(D), _tok(D), _tok(D), _whole((D, D)), _whole((1, D)), _whole((1, D))],
        out_specs=[_tok(D), _tok(D), _tok(D), _whole((8, 128)), _whole((1, D)), _whole((1, D))],
        out_shape=[jax.ShapeDtypeStruct((S, D), F32), jax.ShapeDtypeStruct((S, D), BF16),
                   jax.ShapeDtypeStruct((S, D), BF16), jax.ShapeDtypeStruct((8, 128), F32),
                   jax.ShapeDtypeStruct((1, D), F32), jax.ShapeDtypeStruct((1, D), F32)],
        compiler_params=_cparams(1))(merged, x, target, wout, gamma, beta)


def _gate_bwd(dm, a, b, h, bgate):
    def body(dm_ref, a_ref, b_ref, gp_ref, bg_ref, dgp_ref, da_ref, db_ref, dbg_ref):
        @pl.when(pl.program_id(0) == 0)
        def _():
            dbg_ref[...] = jnp.zeros_like(dbg_ref)

        dm_ = dm_ref[...].astype(F32)
        gates = _sigmoid(gp_ref[...].astype(F32) + bg_ref[...])
        ga, gb = gates[:, :D], gates[:, D:]
        da_ref[...] = (dm_ * ga).astype(BF16)
        db_ref[...] = (dm_ * gb).astype(BF16)
        dgp = jnp.concatenate([dm_ * a_ref[...].astype(F32) * ga * (1.0 - ga),
                               dm_ * b_ref[...].astype(F32) * gb * (1.0 - gb)], axis=1)
        dgp_ref[...] = dgp.astype(BF16)
        dbg_ref[...] += jnp.sum(dgp, axis=0, keepdims=True)

    return pl.pallas_call(
        body, name="gate_bwd", grid=(NT,),
        in_specs=[_tok(D), _tok(D), _tok(D), _tok(2 * D, COL_G // (2 * D)), _whole((1, 2 * D))],
        out_specs=[_tok(2 * D, DH_G // (2 * D)), _tok(D), _tok(D), _whole((1, 2 * D))],
        out_shape=[jax.ShapeDtypeStruct((S, NW), BF16), jax.ShapeDtypeStruct((S, D), BF16),
                   jax.ShapeDtypeStruct((S, D), BF16), jax.ShapeDtypeStruct((1, 2 * D), F32)],
        compiler_params=_cparams(1))(dm, a, b, h, bgate)


def _mix_bwd(dh, da, db, h, o, lse, p, wpa, wpp, wpool, pscale):
    def body(_, da_ref, db_ref, o0_ref, o1_ref, o2_ref, l0_ref, l1_ref, l2_ref, za_ref, zp_ref, p_ref,
             wpa_ref, wpp_ref, wpool_ref, ps_ref,
             dh_ref, do0_ref, do1_ref, do2_ref, dl0_ref, dl1_ref, dl2_ref, dwp_ref, dps_ref,
             nxt_ref):
        i = pl.program_id(0)
        tile = NT - 1 - i
        dza_ref, du_ref, dzp_ref = (dh_ref.at[:, pl.ds(n * AW, AW)] for n in range(3))

        @pl.when(i == 0)
        def _():
            nxt_ref[...] = jnp.zeros_like(nxt_ref)
            dwp_ref[...] = jnp.zeros_like(dwp_ref)
            dps_ref[...] = jnp.zeros_like(dps_ref)

        dya = _nt(da_ref[...], wpa_ref[...])
        w0, w1, w2 = _group_weights(l0_ref[...], l1_ref[...], l2_ref[...])
        za = za_ref[...].astype(F32)
        sig = _sigmoid(za)
        silu_a = za * sig
        dsilu_a = sig * (1.0 + za * (1.0 - sig))
        for hh in range(NH):
            sl = slice(hh * HD, (hh + 1) * HD)
            c = slice(hh, hh + 1)
            oh = (w0[:, c] * o0_ref[:, sl].astype(F32) + w1[:, c] * o1_ref[:, sl].astype(F32)
                  + w2[:, c] * o2_ref[:, sl].astype(F32))
            doh = dya[:, sl] * silu_a[:, sl]
            dza_ref[:, sl] = (dya[:, sl] * oh * dsilu_a[:, sl]).astype(BF16)
            dot_ = jnp.sum(doh * oh, axis=-1, keepdims=True)
            do0_ref[:, sl] = (w0[:, c] * doh).astype(BF16)
            do1_ref[:, sl] = (w1[:, c] * doh).astype(BF16)
            do2_ref[:, sl] = (w2[:, c] * doh).astype(BF16)
            dl0_ref[:, c] = w0[:, c] * dot_
            dl1_ref[:, c] = w1[:, c] * dot_
            dl2_ref[:, c] = w2[:, c] * dot_
        dyp = _nt(db_ref[...], wpp_ref[...])
        pb = p_ref[...]
        pw = _pool_linear(pb, wpool_ref)
        zp = zp_ref[...].astype(F32)
        sigp = _sigmoid(zp)
        dypre = dyp * (zp * sigp)
        dzp_ref[...] = (dyp * (pw * ps_ref[...]) * (sigp * (1.0 + zp * (1.0 - sigp)))).astype(BF16)
        dps_ref[...] += jnp.sum(dypre * pw, axis=0, keepdims=True)
        dpw = (dypre * ps_ref[...]).astype(BF16)
        dp = []
        for g in range(len(POOL_WINDOWS)):
            sl = slice(g * PG, (g + 1) * PG)
            dwp_ref[g] += _tn(pb[:, sl], dpw[:, sl])
            dp.append(_nt(dpw[:, sl], wpool_ref[g]))
        _pool_tokens_bwd(dp, nxt_ref, du_ref, tile)

    r = functools.partial(_tok, rev=True)
    return pl.pallas_call(
        body, name="mix_bwd", grid=(NT,),
        in_specs=[ANY, r(D), r(D)] + [r(AW)] * 3 + [r(NH)] * 3 + [r(AW, COL_ZA // AW), r(AW, COL_ZP // AW), r(AW)]
        + [_whole((AW, D)), _whole((AW, D)), _whole((4, PG, PG)), _whole((1, AW))],
        out_specs=[r(3 * AW, DH_Z // (3 * AW))] + [r(AW)] * 3 + [r(NH)] * 3 + [_whole((4, PG, PG)), _whole((1, AW))],
        out_shape=[jax.ShapeDtypeStruct((S, NW), BF16)] + [jax.ShapeDtypeStruct((S, AW), BF16)] * 3
        + [jax.ShapeDtypeStruct((S, NH), F32)] * 3
        + [jax.ShapeDtypeStruct((4, PG, PG), F32), jax.ShapeDtypeStruct((1, AW), F32)],
        input_output_aliases={0: 0},
        scratch_shapes=[pltpu.VMEM((TM, AW), F32)],
        compiler_params=_cparams(1))(dh, da, db, *o, *lse, h, h, p, wpa, wpp, wpool, pscale)


def _adamw(w, g, m, v):
    m = B1 * m + (1.0 - B1) * g
    v = B2 * v + (1.0 - B2) * jnp.square(g)
    m_hat = m / (1.0 - B1 ** STEP)
    v_hat = v / (1.0 - B2 ** STEP)
    return -LR * (m_hat / (jnp.sqrt(v_hat) + EPS) + WD * w), m, v


def _adam_shard(name, q, l2, w, m, v, tr):
    rows = w.shape[0]

    def body(q_ref, l_ref, w_ref, m_ref, v_ref, g_out, d_out, m_out, v_out):
        g = q_ref[...].astype(F32)
        for k in range(3):
            g = g + l_ref[k].astype(F32)
        g_out[...] = g
        d_out[...], m_out[...], v_out[...] = _adamw(w_ref[...], g, m_ref[...], v_ref[...])

    blk = pl.BlockSpec((tr, D), lambda i: (i, 0))
    return pl.pallas_call(
        body, name=name, grid=(rows // tr,),
        in_specs=[pl.BlockSpec((None, tr, D), lambda i: (0, i, 0)), pl.BlockSpec((3, tr, D), lambda i: (0, i, 0)),
                  blk, blk, blk],
        out_specs=[blk] * 4, out_shape=[jax.ShapeDtypeStruct((rows, D), F32)] * 4,
        compiler_params=_cparams(1))(q, l2, w, m, v)


def _adam_replicated(gathered, w, m, v):
    def body(g_ref, w_ref, m_ref, v_ref, g_out, d_out, m_out, v_out):
        g = g_ref[0]
        for k in range(1, N_DEV):
            g = g + g_ref[k]
        g_out[...] = g
        d_out[...], m_out[...], v_out[...] = _adamw(w_ref[...], g, m_ref[...], v_ref[...])

    return pl.pallas_call(body, name="adam_replicated", out_shape=[jax.ShapeDtypeStruct((8, D), F32)] * 4,
                          compiler_params=pltpu.CompilerParams(vmem_limit_bytes=VMEM_LIMIT))(gathered, w, m, v)


def _pack_small(w_out, w_pa, w_pp, w_pool):
    return jnp.concatenate([w_out, w_pa.reshape(-1, D), w_pp.reshape(-1, D), w_pool.reshape(-1, D)], axis=0)


def _unpack_small(a):
    o = R_OUT
    return (a[:R_PA - o], a[R_PA - o:R_PP - o].reshape(AW, 256), a[R_PP - o:R_PL - o].reshape(AW, 256),
            a[R_PL - o:].reshape(4, 32, PG))


def _pack_vec(b_gate, gamma, beta, pscale, extra):
    z = jnp.zeros((D,), F32)
    return jnp.stack([b_gate[:D], b_gate[D:], gamma, beta, jnp.concatenate([pscale, z[:D - AW]]),
                      jnp.broadcast_to(extra, (D,)), z, z])


def _unpack_vec(a):
    return jnp.concatenate([a[0], a[1]])[None], a[4, :AW][None], a[2][None], a[3][None]


def kernel(x, w_in, b_gate, w_pool, pool_scale, w_proj_attn, w_proj_pool, w_out, ln_gamma, ln_beta, loss_target, m_w_in, m_b_gate, m_w_pool, m_pool_scale, m_w_proj_attn, m_w_proj_pool, m_w_out, m_ln_gamma, m_ln_beta, v_w_in, v_b_gate, v_w_pool, v_pool_scale, v_w_proj_attn, v_w_proj_pool, v_w_out, v_ln_gamma, v_ln_beta):
    coords = jnp.stack([lax.axis_index("x"), lax.axis_index("y"), lax.axis_index("c")]).astype(jnp.int32)
    x2, tgt = _permute_tokens(x[0]), _permute_tokens(loss_target[0])
    xb = x2.astype(BF16)
    xt = x2.T.astype(BF16)

    pack = jnp.concatenate([w_in[0].astype(BF16),
                            _pack_small(w_out[0], w_proj_attn[0], w_proj_pool[0], w_pool[0]).astype(BF16)], axis=0)
    h, gw = _ag_proj(_arrival_order(*coords), xb, pack)
    wout = gw[:, R_OUT:R_PA].reshape(D, D)
    wpa = gw[:, R_PA:R_PP].reshape(N_DEV, AW, 256).transpose(1, 0, 2).reshape(AW, D)
    wpp = gw[:, R_PP:R_PL].reshape(N_DEV, AW, 256).transpose(1, 0, 2).reshape(AW, D)
    wpool = gw[:, R_PL:].reshape(N_DEV, 4, 32, PG).transpose(1, 0, 2, 3).reshape(4, PG, PG)

    o, lse = zip(*[_attn_fwd(g, h) for g in range(len(DILATIONS))])
    ya, yp, merged, a, b, p = _mix_fwd(h, o, lse, wpa, wpp, wpool, pool_scale, b_gate)
    dr, drb, dm, loss_part, dgamma, dbeta = _out_ln(merged, x2, tgt, wout, ln_gamma, ln_beta)

    dh, da, db, dbgate = _gate_bwd(dm, a, b, h, b_gate)
    dh, do0, do1, do2, dl0, dl1, dl2, dwpool, dpscale = _mix_bwd(
        dh, da, db, h, o, lse, p, wpa, wpp, wpool, pool_scale)
    for g, (do_g, dl_g) in enumerate(zip((do0, do1, do2), (dl0, dl1, dl2))):
        dh = _attn_bwd(g, dh, h, do_g, lse[g], dl_g)

    slab = lambda n, k: (n // 2, 0, n % 2)
    part = _grad_w("grad_w_in", xt, dh, jax.ShapeDtypeStruct((N_DEV, R_ALL, D), BF16),
                   pl.BlockSpec((None, D, 1024), slab), 1024, 1024, b_block=_h_block)
    flat = lambda n, k: (0, n)
    d_wout = _grad_w("grad_w_out", merged.T, drb, jax.ShapeDtypeStruct((D, D), BF16),
                     pl.BlockSpec((D, 1024), flat), 1024, 1024)
    d_wpa = _grad_w("grad_w_pa", ya.T, da, jax.ShapeDtypeStruct((AW, D), BF16),
                    pl.BlockSpec((AW, 1024), flat), 1024, 1024)
    d_wpp = _grad_w("grad_w_pp", yp.T, db, jax.ShapeDtypeStruct((AW, D), BF16),
                    pl.BlockSpec((AW, 1024), flat), 1024, 1024)
    small = jnp.concatenate([
        d_wout.reshape(N_DEV, 256, D),
        d_wpa.reshape(AW, N_DEV, 256).transpose(1, 0, 2).reshape(N_DEV, -1, D),
        d_wpp.reshape(AW, N_DEV, 256).transpose(1, 0, 2).reshape(N_DEV, -1, D),
        dwpool.astype(BF16).reshape(4, N_DEV, 32, PG).transpose(1, 0, 2, 3).reshape(N_DEV, -1, D)], axis=1)
    part = lax.dynamic_update_slice(part, small, (0, R_OUT, 0))

    q = _pair_sum(coords, part, _rs_sibling(part))
    grad_x, l2 = _grad_x_rs(dh, gw, dr, q)
    g_in, d_in, m_in, v_in = _adam_shard("adam_w_in", q, l2, w_in[0], m_w_in[0], v_w_in[0], 256)
    outs_small = _adam_shard(
        "adam_small", q[:, R_OUT:], l2[:, R_OUT:],
        _pack_small(w_out[0], w_proj_attn[0], w_proj_pool[0], w_pool[0]),
        _pack_small(m_w_out[0], m_w_proj_attn[0], m_w_proj_pool[0], m_w_pool[0]),
        _pack_small(v_w_out[0], v_w_proj_attn[0], v_w_proj_pool[0], v_w_pool[0]), 176)
    small_parts = [_unpack_small(t) for t in outs_small]

    vec = _pack_vec(dbgate[0], dgamma[0], dbeta[0], dpscale[0], loss_part[0, 0])
    outs_vec = _adam_replicated(
        _all_gather_direct("ag_vec", vec),
        _pack_vec(b_gate[0], ln_gamma[0], ln_beta[0], pool_scale[0], 0.0),
        _pack_vec(m_b_gate[0], m_ln_gamma[0], m_ln_beta[0], m_pool_scale[0], 0.0),
        _pack_vec(v_b_gate[0], v_ln_gamma[0], v_ln_beta[0], v_pool_scale[0], 0.0))
    vec_parts = [_unpack_vec(t) for t in outs_vec]
    loss = outs_vec[0][5, 0]

    def leaves(kind, big):
        out, pa, pp, pool = small_parts[kind]
        bg, ps, gm, bt = vec_parts[kind]
        return [big[None], bg, pool[None], ps, pa[None], pp[None], out[None], gm, bt]

    return (loss, _permute_tokens(grad_x)[None], *leaves(0, g_in), *leaves(1, d_in), *leaves(2, m_in), *leaves(3, v_in))
```

```python
import functools

import jax
import jax.numpy as jnp
from jax import lax
from jax.experimental import pallas as pl
from jax.experimental.pallas import tpu as pltpu

F32 = jnp.float32
BF16 = jnp.bfloat16

S = 4096
D = 2048
NW = 16384
AW = 1024
HD = 128
NH = 8
QB = 128
NBLK = S // QB
DILATIONS = (1, 4, 16)
POOL_WINDOWS = (2, 4, 8, 16)
PG = 256
N_DEV = 8
COL_Q, COL_K, COL_V = 0, 3 * AW, 6 * AW
COL_ZA, COL_U, COL_ZP, COL_G = 9 * AW, 10 * AW, 11 * AW, 12 * AW
DH_Z, DH_G = COL_ZA, COL_G
ALPHA = 2.0 ** 0.25
LN_EPS = 1e-5
NEG_INF = -1e30
LR, B1, B2, EPS, WD, STEP = 0.001, 0.9, 0.999, 1e-08, 0.01, 10
R_IN, R_OUT, R_PA, R_PP, R_PL = 0, 2048, 2304, 2432, 2560
R_ALL = 2576
R_SMALL = R_ALL - R_OUT
VMEM_LIMIT = 56 * 1024 * 1024
MESH = pl.DeviceIdType.MESH
ANY = pl.BlockSpec(memory_space=pl.ANY)


def _cparams(n_axes):
    return pltpu.CompilerParams(dimension_semantics=("arbitrary",) * n_axes, vmem_limit_bytes=VMEM_LIMIT)


def _sigmoid(z):
    return 1.0 / (1.0 + jnp.exp(-z))


def _nt(a, b):
    return lax.dot_general(a, b, (((1,), (1,)), ((), ())), preferred_element_type=F32)


def _tn(a, b):
    return lax.dot_general(a, b, (((0,), (0,)), ((), ())), preferred_element_type=F32)


def _nn(a, b):
    return jnp.dot(a, b, preferred_element_type=F32)


def _lin(x, y, c):
    return 4 * x + 2 * y + c


def _flip(v, f):
    return 1 - v if f else v


def _exchange(name, src, plan, *, dst_shape=None, local_dst=None):
    n = len(plan)
    in_place = dst_shape is None
    out_sds = jax.ShapeDtypeStruct(src.shape, src.dtype) if in_place else dst_shape

    def body(src_ref, dst_ref, send_sems, recv_sems, local_sem):
        x, y, c = lax.axis_index("x"), lax.axis_index("y"), lax.axis_index("c")

        def copy(k, sender):
            flip, src_index, dst_index = plan[k]
            sx, sy, sc = sender
            to = (_flip(sx, flip[0]), _flip(sy, flip[1]), _flip(sc, flip[2]))
            s = src_ref if src_index is None else src_ref.at[src_index(sx, sy, sc)]
            return pltpu.make_async_remote_copy(
                src_ref=s, dst_ref=dst_ref.at[dst_index(sx, sy, sc)],
                send_sem=send_sems.at[k], recv_sem=recv_sems.at[k],
                device_id=to, device_id_type=MESH)

        me = (x, y, c)
        if local_dst is not None:
            mine = pltpu.make_async_copy(src_ref, dst_ref.at[local_dst(x, y, c)], local_sem)
            mine.start()
        sends = [copy(k, me) for k in range(n)]
        for cp in sends:
            cp.start()
        for k in range(n):
            flip = plan[k][0]
            copy(k, (_flip(x, flip[0]), _flip(y, flip[1]), _flip(c, flip[2]))).wait_recv()
        for cp in sends:
            cp.wait_send()
        if local_dst is not None:
            mine.wait()

    return pl.pallas_call(
        body, name=name, out_shape=out_sds, in_specs=[ANY], out_specs=ANY,
        input_output_aliases={0: 0} if in_place else {},
        scratch_shapes=[pltpu.SemaphoreType.DMA((n,)), pltpu.SemaphoreType.DMA((n,)),
                        pltpu.SemaphoreType.DMA(())],
    )(src)


FLIP_C, FLIP_X, FLIP_Y, FLIP_XY = (0, 0, 1), (1, 0, 0), (0, 1, 0), (1, 1, 0)
CHIP_FLIPS = ((0, 0), (1, 0), (0, 1), (1, 1))


AG_PIECES = ((pl.ds(R_IN, D), pl.ds(0, 1024)), (pl.ds(R_IN, D), pl.ds(1024, 1024)),
             (pl.ds(R_OUT, R_PA - R_OUT), pl.ds(0, D)), (pl.ds(R_PA, R_ALL - R_PA), pl.ds(0, D)))
N_PIECES = len(AG_PIECES)
SIB, TO_X, TO_Y, ON, PASS_X, PASS_Y, PASS_D = range(7)
AG_TILES = ((0, 0), (0, 1), (1, 0), (1, 1), (2, 0), (4, 0), (3, 0), (5, 0),
            (2, 1), (4, 1), (3, 1), (5, 1), (6, 0), (6, 1), (7, 0), (7, 1))
W, G = "wait", "go"
AG_STEPS = {
    2: [(W, SIB, 0)], 3: [(W, SIB, 1)],
    4: [(W, TO_X, 0), (G, ON, 0), (G, PASS_X, 0)], 5: [(W, TO_Y, 0), (G, PASS_Y, 0)],
    6: [(W, PASS_X, 0)], 7: [(W, PASS_Y, 0)],
    8: [(W, TO_X, 1), (G, PASS_X, 1)],
    9: [(W, TO_Y, 1), (G, ON, 1), (G, PASS_Y, 1), (G, TO_X, 2), (G, TO_X, 3), (G, TO_Y, 2), (G, TO_Y, 3)],
    10: [(W, PASS_X, 1)], 11: [(W, PASS_Y, 1)],
    12: [(W, ON, 0), (G, PASS_D, 0)], 13: [(W, ON, 1), (G, PASS_D, 1)],
    14: [(W, PASS_D, 0), (W, TO_X, 2), (G, ON, 2), (G, PASS_X, 2), (W, TO_X, 3), (G, PASS_X, 3),
         (W, TO_Y, 2), (G, PASS_Y, 2), (W, TO_Y, 3), (G, ON, 3), (G, PASS_Y, 3)],
    15: [(W, PASS_D, 1)],
}
AG_LAST = [(W, SIB, 2), (W, SIB, 3), (W, ON, 2), (G, PASS_D, 2), (W, ON, 3), (G, PASS_D, 3),
           (W, PASS_X, 2), (W, PASS_X, 3), (W, PASS_Y, 2), (W, PASS_Y, 3), (W, PASS_D, 2), (W, PASS_D, 3)]


def _arrival_order(x, y, c):
    chips = [(x, y), (1 - x, y), (x, 1 - y), (1 - x, 1 - y)]
    return jnp.stack([_lin(px, py, pc) for px, py in chips for pc in (c, 1 - c)]).astype(jnp.int32)


def _ag_proj(order, xb, pack):
    tm, tn = 1024, 1024
    nrow, ntile = S // tm, len(AG_TILES)
    slabs = jnp.stack([order[pos] for pos, _ in AG_TILES])
    cols = jnp.stack([2 * order[pos] + half for pos, half in AG_TILES])

    def body(cols_ref, slabs_ref, x_ref, pack_ref, h_ref, gw_ref, wbuf, wsem, send_sems, recv_sems, local_sem):
        t, i = pl.program_id(0), pl.program_id(1)
        x, y, c = lax.axis_index("x"), lax.axis_index("y"), lax.axis_index("c")
        me = _lin(x, y, c)
        dev = {"sib": (x, y, 1 - c), "x": (1 - x, y, c), "y": (x, 1 - y, c), "d": (1 - x, 1 - y, c)}

        def slab_of(name, other_core=False):
            px, py, pc = dev[name]
            return _lin(px, py, 1 - pc if other_core else pc)

        def rdma(slab, kind, piece, to, from_pack=False):
            k = kind * N_PIECES + piece
            there = gw_ref.at[(slab, *AG_PIECES[piece])]
            return pltpu.make_async_remote_copy(
                src_ref=pack_ref.at[AG_PIECES[piece]] if from_pack else there, dst_ref=there,
                send_sem=send_sems.at[k], recv_sem=recv_sems.at[k], device_id=dev[to], device_id_type=MESH)

        def mine(kind, piece):
            if kind in (SIB, TO_X, TO_Y):
                return rdma(me, kind, piece, ("sib", "x", "y")[kind], from_pack=True)
            if kind == ON:
                frm, to = ("x", "y") if piece % 2 == 0 else ("y", "x")
                return rdma(slab_of(frm), kind, piece, to)
            return rdma(slab_of({PASS_X: "x", PASS_Y: "y", PASS_D: "d"}[kind]), kind, piece, "sib")

        def landing(kind, piece):
            slab = {SIB: slab_of("sib"), TO_X: slab_of("x"), TO_Y: slab_of("y"), ON: slab_of("d"),
                    PASS_X: slab_of("x", True), PASS_Y: slab_of("y", True), PASS_D: slab_of("d", True)}[kind]
            return rdma(slab, kind, piece, "sib")

        def run(steps):
            for what, kind, piece in steps:
                if what == W:
                    landing(kind, piece).wait_recv()
                else:
                    mine(kind, piece).start()

        local = pltpu.make_async_copy(pack_ref, gw_ref.at[me], local_sem)

        def fetch(slab, half, slot):
            src = pack_ref.at[AG_PIECES[half]] if slab is None else gw_ref.at[(slab, *AG_PIECES[half])]
            return pltpu.make_async_copy(src, wbuf.at[slot], wsem.at[slot])

        @pl.when((t == 0) & (i == 0))
        def _():
            local.start()
            run([(G, kind, piece) for piece in (0, 1) for kind in (TO_X, TO_Y, SIB)] + [(G, SIB, 2), (G, SIB, 3)])
            first = fetch(None, 0, 0)
            first.start()
            first.wait()

        for nxt in range(1, ntile):
            @pl.when((t == nxt - 1) & (i == nrow - 1))
            def _(nxt=nxt):
                run(AG_STEPS.get(nxt, []))
                fetch(None if AG_TILES[nxt][0] == 0 else slabs_ref[nxt], AG_TILES[nxt][1], nxt % 2).start()

        for slot in (0, 1):
            @pl.when(t % 2 == slot)
            def _(slot=slot):
                @pl.when((i == 0) & (t > 0))
                def _():
                    fetch(None, 0, slot).wait()
                h_ref[...] = _nn(x_ref[...], wbuf[slot]).astype(h_ref.dtype)

        @pl.when((t == ntile - 1) & (i == nrow - 1))
        def _():
            run(AG_LAST)
            for kind in range(7):
                for piece in range(N_PIECES):
                    mine(kind, piece).wait_send()
            local.wait()

    n_sem = 7 * N_PIECES
    grid_spec = pltpu.PrefetchScalarGridSpec(
        num_scalar_prefetch=2, grid=(ntile, nrow),
        in_specs=[pl.BlockSpec((tm, D), lambda t, i, cols, slabs: (i, 0)), ANY],
        out_specs=[pl.BlockSpec((tm, tn), lambda t, i, cols, slabs: (i, cols[t])), ANY],
        scratch_shapes=[pltpu.VMEM((2, D, tn), BF16), pltpu.SemaphoreType.DMA((2,)),
                        pltpu.SemaphoreType.DMA((n_sem,)), pltpu.SemaphoreType.DMA((n_sem,)),
                        pltpu.SemaphoreType.DMA(())])
    return pl.pallas_call(
        body, name="ag_proj", grid_spec=grid_spec,
        out_shape=[jax.ShapeDtypeStruct((S, NW), BF16), jax.ShapeDtypeStruct((N_DEV, R_ALL, D), BF16)],
        compiler_params=_cparams(2))(cols, slabs, xb, pack)


def _all_gather_direct(name, vec):
    own = lambda x, y, c: _lin(x, y, c)
    flips = [(fx, fy, fc) for fx in (0, 1) for fy in (0, 1) for fc in (0, 1) if (fx, fy, fc) != (0, 0, 0)]
    return _exchange(name, vec, [(f, None, own) for f in flips],
                     dst_shape=jax.ShapeDtypeStruct((N_DEV,) + vec.shape, vec.dtype), local_dst=own)


def _rs_sibling(p):
    plan = [(FLIP_C, (lambda x, y, c, f=f: _lin(_flip(x, f[0]), _flip(y, f[1]), 1 - c)),
             (lambda x, y, c, k=k: k)) for k, f in enumerate(CHIP_FLIPS)]
    return _exchange("rs_sibling", p, plan, dst_shape=jax.ShapeDtypeStruct((4,) + p.shape[1:], p.dtype))


def _pair_sum(coords, p, l1):
    tr = 368
    nr = R_ALL // tr

    def body(crd, p_ref, l_ref, q_ref):
        q_ref[...] = (p_ref[...].astype(F32) + l_ref[...].astype(F32)).astype(q_ref.dtype)

    def p_map(k, i, crd):
        fx, fy = k % 2, k // 2
        px = crd[0] + fx - 2 * fx * crd[0]
        py = crd[1] + fy - 2 * fy * crd[1]
        return (_lin(px, py, crd[2]), i, 0)

    grid_spec = pltpu.PrefetchScalarGridSpec(
        num_scalar_prefetch=1, grid=(4, nr),
        in_specs=[pl.BlockSpec((None, tr, D), p_map),
                  pl.BlockSpec((None, tr, D), lambda k, i, crd: (k, i, 0))],
        out_specs=pl.BlockSpec((None, tr, D), lambda k, i, crd: (k, i, 0)))
    return pl.pallas_call(body, name="pair_sum", grid_spec=grid_spec,
                          out_shape=jax.ShapeDtypeStruct((4, R_ALL, D), BF16),
                          compiler_params=_cparams(2))(coords, p, l1)


def _h_block(k):
    return jnp.where(k < 9, (k % 3) * 3 + k // 3, k)


RS_PIECES = (pl.ds(0, 1280), pl.ds(1280, R_ALL - 1280))
RS_ROWS = (1280, R_ALL - 1280)
RS_CHUNKS = ((320,) * 4, (432,) * 3)
RS_MERGE_STEP = 3


def _grad_x_rs(dh, g, dr, q):
    tm, tk = 512, 1024
    ni, nk = S // tm, NW // tk
    rmax = max(RS_ROWS)
    cmax = max(max(c) for c in RS_CHUNKS)

    def body(dh_ref, w_ref, dr_ref, q_ref, o_ref, l2_ref, ld_ref, mg_ref, va, vb, send_sems, recv_sems, sems):
        i, k = pl.program_id(0), pl.program_id(1)
        x, y, c = lax.axis_index("x"), lax.axis_index("y"), lax.axis_index("c")
        nbr = ((1 - x, y, c), (x, 1 - y, c))

        def rows(ref, piece):
            return ref.at[piece, pl.ds(0, RS_ROWS[piece])]

        copies = (
            (q_ref.at[3, RS_PIECES[0]], rows(ld_ref, 0), 0),
            (q_ref.at[3, RS_PIECES[1]], rows(ld_ref, 1), 1),
            (q_ref.at[1, RS_PIECES[0]], l2_ref.at[0, RS_PIECES[0]], 0),
            (q_ref.at[2, RS_PIECES[1]], l2_ref.at[1, RS_PIECES[1]], 1),
            (rows(mg_ref, 0), l2_ref.at[1, RS_PIECES[0]], 1),
            (rows(mg_ref, 1), l2_ref.at[0, RS_PIECES[1]], 0),
        )

        def copy(n):
            src, dst, axis = copies[n]
            return pltpu.make_async_remote_copy(src_ref=src, dst_ref=dst, send_sem=send_sems.at[n],
                                                recv_sem=recv_sems.at[n], device_id=nbr[axis], device_id_type=MESH)

        def merge(piece, mine):
            start = 0
            for n_rows in RS_CHUNKS[piece]:
                own = pltpu.make_async_copy(q_ref.at[mine, pl.ds(RS_PIECES[piece].start + start, n_rows)],
                                            va.at[pl.ds(0, n_rows)], sems.at[0])
                got = pltpu.make_async_copy(ld_ref.at[piece, pl.ds(start, n_rows)], vb.at[pl.ds(0, n_rows)], sems.at[1])
                own.start()
                got.start()
                own.wait()
                got.wait()
                va[pl.ds(0, n_rows)] = (va[pl.ds(0, n_rows)].astype(F32)
                                        + vb[pl.ds(0, n_rows)].astype(F32)).astype(va.dtype)
                out = pltpu.make_async_copy(va.at[pl.ds(0, n_rows)], mg_ref.at[piece, pl.ds(start, n_rows)], sems.at[2])
                out.start()
                out.wait()
                start += n_rows

        @pl.when((i == 0) & (k == 0))
        def _():
            for n in range(4):
                copy(n).start()

        @pl.when((i == RS_MERGE_STEP) & (k == 0))
        def _():
            copy(0).wait_recv()
            merge(0, 2)
            copy(4).start()
            copy(1).wait_recv()
            merge(1, 1)
            copy(5).start()

        @pl.when(k == 0)
        def _():
            o_ref[...] = ALPHA * dr_ref[...]

        o_ref[...] += _nt(dh_ref[...], w_ref[...])

        @pl.when((i == ni - 1) & (k == nk - 1))
        def _():
            for n in range(2, 6):
                copy(n).wait_recv()
            for n in range(6):
                copy(n).wait_send()

    slab = q.shape[1:]
    out = pl.pallas_call(
        body, name="grad_x_rs", grid=(ni, nk),
        in_specs=[pl.BlockSpec((tm, tk), lambda i, k: (i, k)),
                  pl.BlockSpec((None, D, tk), lambda i, k: (_h_block(k) // 2, 0, _h_block(k) % 2)),
                  pl.BlockSpec((tm, D), lambda i, k: (i, 0)), ANY],
        out_specs=[pl.BlockSpec((tm, D), lambda i, k: (i, 0)), ANY, ANY, ANY],
        out_shape=[jax.ShapeDtypeStruct((S, D), F32), jax.ShapeDtypeStruct((2,) + slab, q.dtype),
                   jax.ShapeDtypeStruct((2, rmax, slab[1]), q.dtype), jax.ShapeDtypeStruct((2, rmax, slab[1]), q.dtype)],
        scratch_shapes=[pltpu.VMEM((cmax, slab[1]), q.dtype), pltpu.VMEM((cmax, slab[1]), q.dtype),
                        pltpu.SemaphoreType.DMA((6,)), pltpu.SemaphoreType.DMA((6,)), pltpu.SemaphoreType.DMA((3,))],
        compiler_params=_cparams(2))(dh, g, dr, q)
    return out[0], out[1]


def _grad_w(name, at, b, out_shape, out_spec, tn, tk, b_block=lambda n: n):
    m, k_all = at.shape
    n_all = b.shape[1]
    nk = k_all // tk

    def body(a_ref, b_ref, o_ref, acc_ref):
        k = pl.program_id(1)

        @pl.when(k == 0)
        def _():
            acc_ref[...] = jnp.zeros_like(acc_ref)

        acc_ref[...] += _nn(a_ref[...], b_ref[...])

        @pl.when(k == nk - 1)
        def _():
            o_ref[...] = acc_ref[...].astype(o_ref.dtype)

    return pl.pallas_call(
        body, name=name, grid=(n_all // tn, nk),
        in_specs=[pl.BlockSpec((m, tk), lambda n, k: (0, k)),
                  pl.BlockSpec((tk, tn), lambda n, k: (k, b_block(n)))],
        out_specs=out_spec, out_shape=out_shape,
        scratch_shapes=[pltpu.VMEM((m, tn), F32)], compiler_params=_cparams(2))(at, b)


NR = 16
TI = 16
TM = NR * TI
NT = S // TM
ATT_QB = (256, 128, 128)
ATT_NB = (16, 8, 2)
ATT_BLOCKS = (16, 32, 32)


def _permute_tokens(a):
    return a.reshape(NT, TI, NR, a.shape[-1]).transpose(0, 2, 1, 3).reshape(a.shape)


def _attn_shape(g, c):
    if g == 0:
        return (S, c)
    if g == 1:
        return (NT, 4, 4, TI, c)
    return (NT, NR, TI, c)


def _attn_view(g, a):
    return a.reshape(_attn_shape(g, a.shape[-1]))


def _attn_spec(g, width, col, blk):
    if g == 0:
        return pl.BlockSpec((TM, width), lambda b: (blk(b), col))
    if g == 1:
        return pl.BlockSpec((2, 4, None, TI, width), lambda b: (blk(b) % 8, 0, blk(b) // 8, 0, col))
    return pl.BlockSpec((8, None, TI, width), lambda b: (blk(b) % 2, blk(b) // 2, 0, col))


def _pieces(g):
    if g == 1:
        return [(t, m) for t in range(2) for m in range(4)]
    return [(t,) for t in range(8)]


def _get(g, ref, sl):
    if g == 0:
        return ref[:, sl]
    return jnp.concatenate([ref[(*p, slice(None), sl)] for p in _pieces(g)], axis=0)


def _put(g, ref, sl, val):
    if g == 0:
        ref[:, sl] = val
    else:
        for n, p in enumerate(_pieces(g)):
            ref[(*p, slice(None), sl)] = val[TI * n:TI * (n + 1)]


def _block_pos(g, a):
    if g == 0:
        return 16 * (a % 16) + a // 16
    if g == 1:
        return 64 * (a // 64) + 4 * (a % 16) + (a // 16) % 4
    return a


def _attn_mask(g, n):
    qb = ATT_QB[g]
    qa = lax.broadcasted_iota(jnp.int32, (qb, 2 * qb), 0)
    kc = lax.broadcasted_iota(jnp.int32, (qb, 2 * qb), 1)
    cur = kc >= qb
    dist = _block_pos(g, qa) - _block_pos(g, kc % qb) + jnp.where(cur, 0, qb)
    return (dist >= 0) & (dist <= QB) & (cur | (n > 0))


def _qkv_specs(g, clamp):
    cur = lambda col: _attn_spec(g, AW, col, clamp)
    prev = lambda col: _attn_spec(g, AW, col, lambda b: jnp.maximum(clamp(b) - 1, 0))
    qc, kc, vc = (c // AW + g for c in (COL_Q, COL_K, COL_V))
    return [cur(qc), cur(kc), prev(kc), cur(vc), prev(vc)]


def _attn_fwd(g, h):
    scale = HD ** -0.5
    hv = _attn_view(g, h)

    def body(q_ref, kc_ref, kp_ref, vc_ref, vp_ref, o_ref, l_ref):
        valid = _attn_mask(g, pl.program_id(0) % ATT_NB[g])
        for hh in range(NH):
            sl = slice(hh * HD, (hh + 1) * HD)
            kh = jnp.concatenate([_get(g, kp_ref, sl), _get(g, kc_ref, sl)], axis=0)
            vh = jnp.concatenate([_get(g, vp_ref, sl), _get(g, vc_ref, sl)], axis=0)
            s = jnp.where(valid, _nt(_get(g, q_ref, sl), kh) * scale, NEG_INF)
            m = jnp.max(s, axis=-1, keepdims=True)
            e = jnp.exp(s - m)
            den = jnp.sum(e, axis=-1, keepdims=True)
            _put(g, o_ref, sl, (_nn(e.astype(BF16), vh) / den).astype(o_ref.dtype))
            _put(g, l_ref, slice(hh, hh + 1), m + jnp.log(den))

    same = lambda b: b
    o, lse = pl.pallas_call(
        body, name=f"attn_fwd_{g}", grid=(ATT_BLOCKS[g],),
        in_specs=_qkv_specs(g, same),
        out_specs=[_attn_spec(g, AW, 0, same), _attn_spec(g, NH, 0, same)],
        out_shape=[jax.ShapeDtypeStruct(_attn_shape(g, AW), BF16), jax.ShapeDtypeStruct(_attn_shape(g, NH), F32)],
        compiler_params=_cparams(1))(hv, hv, hv, hv, hv)
    return o.reshape(S, AW), lse.reshape(S, NH)


def _attn_bwd(g, dh, h, do, lse, delta):
    scale = HD ** -0.5
    qb = ATT_QB[g]
    last = ATT_BLOCKS[g] - 1
    clamp = lambda b: jnp.minimum(b, last)
    behind = lambda b: jnp.maximum(b - 1, 0)
    hv = _attn_view(g, h)

    def body(q_ref, kc_ref, kp_ref, vc_ref, vp_ref, do_ref, l_ref, dl_ref, _, dh_ref, cq_ref, ck_ref, cv_ref):
        b = pl.program_id(0)

        def write(col, val):
            _put(g, dh_ref, slice(col, col + HD), val.astype(dh_ref.dtype))

        @pl.when(b == 0)
        def _():
            cq_ref[...] = jnp.zeros_like(cq_ref)
            ck_ref[...] = jnp.zeros_like(ck_ref)
            cv_ref[...] = jnp.zeros_like(cv_ref)

        @pl.when(b <= last)
        def _():
            valid = _attn_mask(g, b % ATT_NB[g])
            for hh in range(NH):
                sl = slice(hh * HD, (hh + 1) * HD)
                one = slice(hh, hh + 1)
                qh, doh = _get(g, q_ref, sl), _get(g, do_ref, sl)
                kh = jnp.concatenate([_get(g, kp_ref, sl), _get(g, kc_ref, sl)], axis=0)
                vh = jnp.concatenate([_get(g, vp_ref, sl), _get(g, vc_ref, sl)], axis=0)
                s = _nt(qh, kh) * scale
                p = jnp.where(valid, jnp.exp(s - _get(g, l_ref, one)), 0.0)
                ds = p * (_nt(doh, vh) - _get(g, dl_ref, one))
                dsb = (ds * scale).astype(BF16)
                dk2 = _tn(dsb, qh)
                dv2 = _tn(p.astype(BF16), doh)
                write(hh * HD, cq_ref[:, sl])
                write(AW + hh * HD, ck_ref[:, sl] + dk2[:qb])
                write(2 * AW + hh * HD, cv_ref[:, sl] + dv2[:qb])
                cq_ref[:, sl] = _nn(dsb, kh)
                ck_ref[:, sl] = dk2[qb:]
                cv_ref[:, sl] = dv2[qb:]

        @pl.when(b > last)
        def _():
            for hh in range(NH):
                sl = slice(hh * HD, (hh + 1) * HD)
                write(hh * HD, cq_ref[:, sl])
                write(AW + hh * HD, ck_ref[:, sl])
                write(2 * AW + hh * HD, cv_ref[:, sl])

    out = pl.pallas_call(
        body, name=f"attn_bwd_{g}", grid=(ATT_BLOCKS[g] + 1,),
        in_specs=_qkv_specs(g, clamp) + [_attn_spec(g, AW, 0, clamp), _attn_spec(g, NH, 0, clamp),
                                         _attn_spec(g, NH, 0, clamp), ANY],
        out_specs=_attn_spec(g, 3 * AW, g, behind),
        out_shape=jax.ShapeDtypeStruct(_attn_shape(g, NW), BF16),
        input_output_aliases={8: 0},
        scratch_shapes=[pltpu.VMEM((qb, AW), F32)] * 3,
        compiler_params=_cparams(1))(hv, hv, hv, hv, hv, _attn_view(g, do), _attn_view(g, lse), _attn_view(g, delta),
                                     _attn_view(g, dh))
    return out.reshape(S, NW)


def _group_weights(l0, l1, l2):
    m = jnp.maximum(jnp.maximum(l0, l1), l2)
    e0, e1, e2 = jnp.exp(l0 - m), jnp.exp(l1 - m), jnp.exp(l2 - m)
    inv = 1.0 / (e0 + e1 + e2)
    return e0 * inv, e1 * inv, e2 * inv


def _residue(ref, r, sl):
    return ref[r * TI:(r + 1) * TI, sl].astype(F32)


def _total(parts):
    return functools.reduce(lambda x, y: x + y, parts)


def _pool_tokens(up_ref, uc_ref, p_ref, tile):
    j0 = lax.broadcasted_iota(jnp.int32, (TI, 1), 0) == 0
    first = (tile == 0) & j0
    for r in range(NR):
        out = []
        for g, w in enumerate(POOL_WINDOWS):
            sl = slice(g * PG, (g + 1) * PG)
            own = _residue(uc_ref, r, sl)
            acc = _total([own] + [_residue(uc_ref, r - k, sl) for k in range(1, min(r, w - 1) + 1)])
            wrapped = [NR + r - k for k in range(r + 1, w)]
            if wrapped:
                wc = _total([_residue(uc_ref, q, sl) for q in wrapped])
                wp = jnp.where(tile > 0, _total([_residue(up_ref, q, sl) for q in wrapped]), 0.0)
                acc = acc + jnp.where(j0, pltpu.roll(wp, 1, 0), pltpu.roll(wc, 1, 0))
            out.append(acc / jnp.where(first, float(min(r + 1, w)), float(w)) - own)
        p_ref[r * TI:(r + 1) * TI, :] = jnp.concatenate(out, axis=1).astype(p_ref.dtype)


def _pool_tokens_bwd(dp, nxt_ref, du_ref, tile):
    ji = lax.broadcasted_iota(jnp.int32, (TI, 1), 0)
    first = (tile == 0) & (ji == 0)
    piece = lambda g, r: dp[g][r * TI:(r + 1) * TI]
    dpc = [[piece(g, r) / jnp.where(first, float(min(r + 1, w)), float(w)) for r in range(NR)]
           for g, w in enumerate(POOL_WINDOWS)]
    for r in range(NR):
        out = []
        for g, w in enumerate(POOL_WINDOWS):
            sl = slice(g * PG, (g + 1) * PG)
            acc = _total([dpc[g][r + k] for k in range(w) if r + k < NR])
            wrapped = [r + k - NR for k in range(1, w) if r + k >= NR]
            if wrapped:
                wc = _total([dpc[g][q] for q in wrapped])
                wn = _total([nxt_ref[q * TI:(q + 1) * TI, sl] for q in wrapped])
                acc = acc + jnp.where(ji == TI - 1, pltpu.roll(wn, TI - 1, 0), pltpu.roll(wc, TI - 1, 0))
            out.append(acc - piece(g, r))
        du_ref[r * TI:(r + 1) * TI, :] = jnp.concatenate(out, axis=1).astype(du_ref.dtype)
    for r in range(NR):
        nxt_ref[r * TI:(r + 1) * TI, :] = jnp.concatenate([dpc[g][r] for g in range(len(POOL_WINDOWS))], axis=1)


def _pool_linear(pb, wpool_ref):
    return jnp.concatenate([_nn(pb[:, g * PG:(g + 1) * PG], wpool_ref[g]) for g in range(len(POOL_WINDOWS))], axis=1)


def _tok(width, col=0, rev=False):
    if rev:
        return pl.BlockSpec((TM, width), lambda i: (NT - 1 - i, col))
    return pl.BlockSpec((TM, width), lambda i: (i, col))


def _whole(shape):
    return pl.BlockSpec(shape, lambda i: (0,) * len(shape))


def _mix_fwd(h, o, lse, wpa, wpp, wpool, pscale, bgate):
    def body(o0_ref, o1_ref, o2_ref, l0_ref, l1_ref, l2_ref, za_ref, uc_ref, up_ref, zp_ref, gp_ref,
             wpa_ref, wpp_ref, wpool_ref, ps_ref, bg_ref, ya_ref, yp_ref, mg_ref, a_ref, b_ref, p_ref):
        i = pl.program_id(0)
        w0, w1, w2 = _group_weights(l0_ref[...], l1_ref[...], l2_ref[...])
        za = za_ref[...].astype(F32)
        silu_a = za * _sigmoid(za)
        for hh in range(NH):
            sl = slice(hh * HD, (hh + 1) * HD)
            c = slice(hh, hh + 1)
            oh = (w0[:, c] * o0_ref[:, sl].astype(F32) + w1[:, c] * o1_ref[:, sl].astype(F32)
                  + w2[:, c] * o2_ref[:, sl].astype(F32))
            ya_ref[:, sl] = (oh * silu_a[:, sl]).astype(BF16)
        _pool_tokens(up_ref, uc_ref, p_ref, i)
        zp = zp_ref[...].astype(F32)
        yp_ref[...] = (_pool_linear(p_ref[...], wpool_ref) * ps_ref[...] * (zp * _sigmoid(zp))).astype(BF16)
        a = _nn(ya_ref[...], wpa_ref[...])
        b = _nn(yp_ref[...], wpp_ref[...])
        a_ref[...] = a.astype(BF16)
        b_ref[...] = b.astype(BF16)
        gates = _sigmoid(gp_ref[...].astype(F32) + bg_ref[...])
        mg_ref[...] = (gates[:, :D] * a + gates[:, D:] * b).astype(BF16)

    u_prev = pl.BlockSpec((TM, AW), lambda i: (jnp.maximum(i - 1, 0), COL_U // AW))
    return pl.pallas_call(
        body, name="mix_fwd", grid=(NT,),
        in_specs=[_tok(AW)] * 3 + [_tok(NH)] * 3
        + [_tok(AW, COL_ZA // AW), _tok(AW, COL_U // AW), u_prev, _tok(AW, COL_ZP // AW), _tok(2 * D, COL_G // (2 * D))]
        + [_whole((AW, D)), _whole((AW, D)), _whole((4, PG, PG)), _whole((1, AW)), _whole((1, 2 * D))],
        out_specs=[_tok(AW), _tok(AW), _tok(D), _tok(D), _tok(D), _tok(AW)],
        out_shape=[jax.ShapeDtypeStruct((S, AW), BF16)] * 2 + [jax.ShapeDtypeStruct((S, D), BF16)] * 3
        + [jax.ShapeDtypeStruct((S, AW), BF16)],
        compiler_params=_cparams(1))(*o, *lse, h, h, h, h, h, wpa, wpp, wpool, pscale, bgate)


def _out_ln(merged, x, target, wout, gamma, beta):
    def body(mg_ref, x_ref, t_ref, w_ref, g_ref, b_ref, dr_ref, drb_ref, dm_ref, loss_ref, dg_ref, db_ref):
        i = pl.program_id(0)

        @pl.when(i == 0)
        def _():
            loss_ref[...] = jnp.zeros_like(loss_ref)
            dg_ref[...] = jnp.zeros_like(dg_ref)
            db_ref[...] = jnp.zeros_like(db_ref)

        r = ALPHA * x_ref[...] + _nn(mg_ref[...], w_ref[...])
        mu = jnp.mean(r, axis=-1, keepdims=True)
        rc = r - mu
        rstd = lax.rsqrt(jnp.mean(rc * rc, axis=-1, keepdims=True) + LN_EPS)
        xhat = rc * rstd
        err = xhat * g_ref[...] + b_ref[...] - t_ref[...]
        loss_ref[...] += 0.5 * jnp.sum(jnp.mean(err * err, axis=-1, keepdims=True), axis=0, keepdims=True)
        dy = err * (1.0 / D)
        dg_ref[...] += jnp.sum(dy * xhat, axis=0, keepdims=True)
        db_ref[...] += jnp.sum(dy, axis=0, keepdims=True)
        dxh = dy * g_ref[...]
        dr = rstd * (dxh - jnp.mean(dxh, axis=-1, keepdims=True)
                     - xhat * jnp.mean(dxh * xhat, axis=-1, keepdims=True))
        dr_ref[...] = dr
        drb_ref[...] = dr.astype(BF16)
        dm_ref[...] = _nt(drb_ref[...], w_ref[...]).astype(BF16)

    return pl.pallas_call(
        body, name="out_ln", grid=(NT,),
        in_specs=[_tok(D), _tok(D), _tok(D), _whole((D, D)), _whole((1, D)), _whole((1, D))],
        out_specs=[_tok(D), _tok(D), _tok(D), _whole((8, 128)), _whole((1, D)), _whole((1, D))],
        out_shape=[jax.ShapeDtypeStruct((S, D), F32), jax.ShapeDtypeStruct((S, D), BF16),
                   jax.ShapeDtypeStruct((S, D), BF16), jax.ShapeDtypeStruct((8, 128), F32),
                   jax.ShapeDtypeStruct((1, D), F32), jax.ShapeDtypeStruct((1, D), F32)],
        compiler_params=_cparams(1))(merged, x, target, wout, gamma, beta)


def _gate_bwd(dm, a, b, h, bgate):
    def body(dm_ref, a_ref, b_ref, gp_ref, bg_ref, dgp_ref, da_ref, db_ref, dbg_ref):
        @pl.when(pl.program_id(0) == 0)
        def _():
            dbg_ref[...] = jnp.zeros_like(dbg_ref)

        dm_ = dm_ref[...].astype(F32)
        gates = _sigmoid(gp_ref[...].astype(F32) + bg_ref[...])
        ga, gb = gates[:, :D], gates[:, D:]
        da_ref[...] = (dm_ * ga).astype(BF16)
        db_ref[...] = (dm_ * gb).astype(BF16)
        dgp = jnp.concatenate([dm_ * a_ref[...].astype(F32) * ga * (1.0 - ga),
                               dm_ * b_ref[...].astype(F32) * gb * (1.0 - gb)], axis=1)
        dgp_ref[...] = dgp.astype(BF16)
        dbg_ref[...] += jnp.sum(dgp, axis=0, keepdims=True)

    return pl.pallas_call(
        body, name="gate_bwd", grid=(NT,),
        in_specs=[_tok(D), _tok(D), _tok(D), _tok(2 * D, COL_G // (2 * D)), _whole((1, 2 * D))],
        out_specs=[_tok(2 * D, DH_G // (2 * D)), _tok(D), _tok(D), _whole((1, 2 * D))],
        out_shape=[jax.ShapeDtypeStruct((S, NW), BF16), jax.ShapeDtypeStruct((S, D), BF16),
                   jax.ShapeDtypeStruct((S, D), BF16), jax.ShapeDtypeStruct((1, 2 * D), F32)],
        compiler_params=_cparams(1))(dm, a, b, h, bgate)


def _mix_bwd(dh, da, db, h, o, lse, p, wpa, wpp, wpool, pscale):
    def body(_, da_ref, db_ref, o0_ref, o1_ref, o2_ref, l0_ref, l1_ref, l2_ref, za_ref, zp_ref, p_ref,
             wpa_ref, wpp_ref, wpool_ref, ps_ref,
             dh_ref, do0_ref, do1_ref, do2_ref, dl0_ref, dl1_ref, dl2_ref, dwp_ref, dps_ref,
             nxt_ref):
        i = pl.program_id(0)
        tile = NT - 1 - i
        dza_ref, du_ref, dzp_ref = (dh_ref.at[:, pl.ds(n * AW, AW)] for n in range(3))

        @pl.when(i == 0)
        def _():
            nxt_ref[...] = jnp.zeros_like(nxt_ref)
            dwp_ref[...] = jnp.zeros_like(dwp_ref)
            dps_ref[...] = jnp.zeros_like(dps_ref)

        dya = _nt(da_ref[...], wpa_ref[...])
        w0, w1, w2 = _group_weights(l0_ref[...], l1_ref[...], l2_ref[...])
        za = za_ref[...].astype(F32)
        sig = _sigmoid(za)
        silu_a = za * sig
        dsilu_a = sig * (1.0 + za * (1.0 - sig))
        for hh in range(NH):
            sl = slice(hh * HD, (hh + 1) * HD)
            c = slice(hh, hh + 1)
            oh = (w0[:, c] * o0_ref[:, sl].astype(F32) + w1[:, c] * o1_ref[:, sl].astype(F32)
                  + w2[:, c] * o2_ref[:, sl].astype(F32))
            doh = dya[:, sl] * silu_a[:, sl]
            dza_ref[:, sl] = (dya[:, sl] * oh * dsilu_a[:, sl]).astype(BF16)
            dot_ = jnp.sum(doh * oh, axis=-1, keepdims=True)
            do0_ref[:, sl] = (w0[:, c] * doh).astype(BF16)
            do1_ref[:, sl] = (w1[:, c] * doh).astype(BF16)
            do2_ref[:, sl] = (w2[:, c] * doh).astype(BF16)
            dl0_ref[:, c] = w0[:, c] * dot_
            dl1_ref[:, c] = w1[:, c] * dot_
            dl2_ref[:, c] = w2[:, c] * dot_
        dyp = _nt(db_ref[...], wpp_ref[...])
        pb = p_ref[...]
        pw = _pool_linear(pb, wpool_ref)
        zp = zp_ref[...].astype(F32)
        sigp = _sigmoid(zp)
        dypre = dyp * (zp * sigp)
        dzp_ref[...] = (dyp * (pw * ps_ref[...]) * (sigp * (1.0 + zp * (1.0 - sigp)))).astype(BF16)
        dps_ref[...] += jnp.sum(dypre * pw, axis=0, keepdims=True)
        dpw = (dypre * ps_ref[...]).astype(BF16)
        dp = []
        for g in range(len(POOL_WINDOWS)):
            sl = slice(g * PG, (g + 1) * PG)
            dwp_ref[g] += _tn(pb[:, sl], dpw[:, sl])
            dp.append(_nt(dpw[:, sl], wpool_ref[g]))
        _pool_tokens_bwd(dp, nxt_ref, du_ref, tile)

    r = functools.partial(_tok, rev=True)
    return pl.pallas_call(
        body, name="mix_bwd", grid=(NT,),
        in_specs=[ANY, r(D), r(D)] + [r(AW)] * 3 + [r(NH)] * 3 + [r(AW, COL_ZA // AW), r(AW, COL_ZP // AW), r(AW)]
        + [_whole((AW, D)), _whole((AW, D)), _whole((4, PG, PG)), _whole((1, AW))],
        out_specs=[r(3 * AW, DH_Z // (3 * AW))] + [r(AW)] * 3 + [r(NH)] * 3 + [_whole((4, PG, PG)), _whole((1, AW))],
        out_shape=[jax.ShapeDtypeStruct((S, NW), BF16)] + [jax.ShapeDtypeStruct((S, AW), BF16)] * 3
        + [jax.ShapeDtypeStruct((S, NH), F32)] * 3
        + [jax.ShapeDtypeStruct((4, PG, PG), F32), jax.ShapeDtypeStruct((1, AW), F32)],
        input_output_aliases={0: 0},
        scratch_shapes=[pltpu.VMEM((TM, AW), F32)],
        compiler_params=_cparams(1))(dh, da, db, *o, *lse, h, h, p, wpa, wpp, wpool, pscale)


def _adamw(w, g, m, v):
    m = B1 * m + (1.0 - B1) * g
    v = B2 * v + (1.0 - B2) * jnp.square(g)
    m_hat = m / (1.0 - B1 ** STEP)
    v_hat = v / (1.0 - B2 ** STEP)
    return -LR * (m_hat / (jnp.sqrt(v_hat) + EPS) + WD * w), m, v


def _adam_shard(name, q, l2, w, m, v, tr):
    rows = w.shape[0]

    def body(q_ref, l_ref, w_ref, m_ref, v_ref, g_out, d_out, m_out, v_out):
        g = q_ref[...].astype(F32)
        for k in range(2):
            g = g + l_ref[k].astype(F32)
        g_out[...] = g
        d_out[...], m_out[...], v_out[...] = _adamw(w_ref[...], g, m_ref[...], v_ref[...])

    blk = pl.BlockSpec((tr, D), lambda i: (i, 0))
    return pl.pallas_call(
        body, name=name, grid=(rows // tr,),
        in_specs=[pl.BlockSpec((None, tr, D), lambda i: (0, i, 0)), pl.BlockSpec((2, tr, D), lambda i: (0, i, 0)),
                  blk, blk, blk],
        out_specs=[blk] * 4, out_shape=[jax.ShapeDtypeStruct((rows, D), F32)] * 4,
        compiler_params=_cparams(1))(q, l2, w, m, v)


def _adam_replicated(gathered, w, m, v):
    def body(g_ref, w_ref, m_ref, v_ref, g_out, d_out, m_out, v_out):
        g = g_ref[0]
        for k in range(1, N_DEV):
            g = g + g_ref[k]
        g_out[...] = g
        d_out[...], m_out[...], v_out[...] = _adamw(w_ref[...], g, m_ref[...], v_ref[...])

    return pl.pallas_call(body, name="adam_replicated", out_shape=[jax.ShapeDtypeStruct((8, D), F32)] * 4,
                          compiler_params=pltpu.CompilerParams(vmem_limit_bytes=VMEM_LIMIT))(gathered, w, m, v)


def _pack_small(w_out, w_pa, w_pp, w_pool):
    return jnp.concatenate([w_out, w_pa.reshape(-1, D), w_pp.reshape(-1, D), w_pool.reshape(-1, D)], axis=0)


def _unpack_small(a):
    o = R_OUT
    return (a[:R_PA - o], a[R_PA - o:R_PP - o].reshape(AW, 256), a[R_PP - o:R_PL - o].reshape(AW, 256),
            a[R_PL - o:].reshape(4, 32, PG))


def _pack_vec(b_gate, gamma, beta, pscale, extra):
    z = jnp.zeros((D,), F32)
    return jnp.stack([b_gate[:D], b_gate[D:], gamma, beta, jnp.concatenate([pscale, z[:D - AW]]),
                      jnp.broadcast_to(extra, (D,)), z, z])


def _unpack_vec(a):
    return jnp.concatenate([a[0], a[1]])[None], a[4, :AW][None], a[2][None], a[3][None]


def kernel(x, w_in, b_gate, w_pool, pool_scale, w_proj_attn, w_proj_pool, w_out, ln_gamma, ln_beta, loss_target, m_w_in, m_b_gate, m_w_pool, m_pool_scale, m_w_proj_attn, m_w_proj_pool, m_w_out, m_ln_gamma, m_ln_beta, v_w_in, v_b_gate, v_w_pool, v_pool_scale, v_w_proj_attn, v_w_proj_pool, v_w_out, v_ln_gamma, v_ln_beta):
    coords = jnp.stack([lax.axis_index("x"), lax.axis_index("y"), lax.axis_index("c")]).astype(jnp.int32)
    x2, tgt = _permute_tokens(x[0]), _permute_tokens(loss_target[0])
    xb = x2.astype(BF16)
    xt = x2.T.astype(BF16)

    pack = jnp.concatenate([w_in[0].astype(BF16),
                            _pack_small(w_out[0], w_proj_attn[0], w_proj_pool[0], w_pool[0]).astype(BF16)], axis=0)
    h, gw = _ag_proj(_arrival_order(*coords), xb, pack)
    wout = gw[:, R_OUT:R_PA].reshape(D, D)
    wpa = gw[:, R_PA:R_PP].reshape(N_DEV, AW, 256).transpose(1, 0, 2).reshape(AW, D)
    wpp = gw[:, R_PP:R_PL].reshape(N_DEV, AW, 256).transpose(1, 0, 2).reshape(AW, D)
    wpool = gw[:, R_PL:].reshape(N_DEV, 4, 32, PG).transpose(1, 0, 2, 3).reshape(4, PG, PG)

    o, lse = zip(*[_attn_fwd(g, h) for g in range(len(DILATIONS))])
    ya, yp, merged, a, b, p = _mix_fwd(h, o, lse, wpa, wpp, wpool, pool_scale, b_gate)
    dr, drb, dm, loss_part, dgamma, dbeta = _out_ln(merged, x2, tgt, wout, ln_gamma, ln_beta)

    dh, da, db, dbgate = _gate_bwd(dm, a, b, h, b_gate)
    dh, do0, do1, do2, dl0, dl1, dl2, dwpool, dpscale = _mix_bwd(
        dh, da, db, h, o, lse, p, wpa, wpp, wpool, pool_scale)
    for g, (do_g, dl_g) in enumerate(zip((do0, do1, do2), (dl0, dl1, dl2))):
        dh = _attn_bwd(g, dh, h, do_g, lse[g], dl_g)

    slab = lambda n, k: (n // 2, 0, n % 2)
    part = _grad_w("grad_w_in", xt, dh, jax.ShapeDtypeStruct((N_DEV, R_ALL, D), BF16),
                   pl.BlockSpec((None, D, 1024), slab), 1024, 1024, b_block=_h_block)
    flat = lambda n, k: (0, n)
    d_wout = _grad_w("grad_w_out", merged.T, drb, jax.ShapeDtypeStruct((D, D), BF16),
                     pl.BlockSpec((D, 1024), flat), 1024, 1024)
    d_wpa = _grad_w("grad_w_pa", ya.T, da, jax.ShapeDtypeStruct((AW, D), BF16),
                    pl.BlockSpec((AW, 1024), flat), 1024, 1024)
    d_wpp = _grad_w("grad_w_pp", yp.T, db, jax.ShapeDtypeStruct((AW, D), BF16),
                    pl.BlockSpec((AW, 1024), flat), 1024, 1024)
    small = jnp.concatenate([
        d_wout.reshape(N_DEV, 256, D),
        d_wpa.reshape(AW, N_DEV, 256).transpose(1, 0, 2).reshape(N_DEV, -1, D),
        d_wpp.reshape(AW, N_DEV, 256).transpose(1, 0, 2).reshape(N_DEV, -1, D),
        dwpool.astype(BF16).reshape(4, N_DEV, 32, PG).transpose(1, 0, 2, 3).reshape(N_DEV, -1, D)], axis=1)
    part = lax.dynamic_update_slice(part, small, (0, R_OUT, 0))

    q = _pair_sum(coords, part, _rs_sibling(part))
    grad_x, l2 = _grad_x_rs(dh, gw, dr, q)
    g_in, d_in, m_in, v_in = _adam_shard("adam_w_in", q, l2, w_in[0], m_w_in[0], v_w_in[0], 256)
    outs_small = _adam_shard(
        "adam_small", q[:, R_OUT:], l2[:, R_OUT:],
        _pack_small(w_out[0], w_proj_attn[0], w_proj_pool[0], w_pool[0]),
        _pack_small(m_w_out[0], m_w_proj_attn[0], m_w_proj_pool[0], m_w_pool[0]),
        _pack_small(v_w_out[0], v_w_proj_attn[0], v_w_proj_pool[0], v_w_pool[0]), 176)
    small_parts = [_unpack_small(t) for t in outs_small]

    vec = _pack_vec(dbgate[0], dgamma[0], dbeta[0], dpscale[0], loss_part[0, 0])
    outs_vec = _adam_replicated(
        _all_gather_direct("ag_vec", vec),
        _pack_vec(b_gate[0], ln_gamma[0], ln_beta[0], pool_scale[0], 0.0),
        _pack_vec(m_b_gate[0], m_ln_gamma[0], m_ln_beta[0], m_pool_scale[0], 0.0),
        _pack_vec(v_b_gate[0], v_ln_gamma[0], v_ln_beta[0], v_pool_scale[0], 0.0))
    vec_parts = [_unpack_vec(t) for t in outs_vec]
    loss = outs_vec[0][5, 0]

    def leaves(kind, big):
        out, pa, pp, pool = small_parts[kind]
        bg, ps, gm, bt = vec_parts[kind]
        return [big[None], bg, pool[None], ps, pa[None], pp[None], out[None], gm, bt]

    return (loss, _permute_tokens(grad_x)[None], *leaves(0, g_in), *leaves(1, d_in), *leaves(2, m_in), *leaves(3, v_in))
```

```python
import functools

import jax
import jax.numpy as jnp
from jax import lax
from jax.experimental import pallas as pl
from jax.experimental.pallas import tpu as pltpu

F32 = jnp.float32
BF16 = jnp.bfloat16

S = 4096
D = 2048
NW = 16384
AW = 1024
HD = 128
NH = 8
QB = 128
NBLK = S // QB
DILATIONS = (1, 4, 16)
POOL_WINDOWS = (2, 4, 8, 16)
PG = 256
N_DEV = 8
COL_Q, COL_K, COL_V = 0, 3 * AW, 6 * AW
COL_ZA, COL_U, COL_ZP, COL_G = 9 * AW, 10 * AW, 11 * AW, 12 * AW
DH_Z, DH_G = COL_ZA, COL_G
ALPHA = 2.0 ** 0.25
LN_EPS = 1e-5
NEG_INF = -1e30
LR, B1, B2, EPS, WD, STEP = 0.001, 0.9, 0.999, 1e-08, 0.01, 10
R_IN, R_OUT, R_PA, R_PP, R_PL = 0, 2048, 2304, 2432, 2560
R_ALL = 2576
R_SMALL = R_ALL - R_OUT
VMEM_LIMIT = 56 * 1024 * 1024
MESH = pl.DeviceIdType.MESH
ANY = pl.BlockSpec(memory_space=pl.ANY)


def _cparams(n_axes):
    return pltpu.CompilerParams(dimension_semantics=("arbitrary",) * n_axes, vmem_limit_bytes=VMEM_LIMIT)


def _sigmoid(z):
    return 1.0 / (1.0 + jnp.exp(-z))


def _nt(a, b):
    return lax.dot_general(a, b, (((1,), (1,)), ((), ())), preferred_element_type=F32)


def _tn(a, b):
    return lax.dot_general(a, b, (((0,), (0,)), ((), ())), preferred_element_type=F32)


def _nn(a, b):
    return jnp.dot(a, b, preferred_element_type=F32)


def _lin(x, y, c):
    return 4 * x + 2 * y + c


def _flip(v, f):
    return 1 - v if f else v


def _exchange(name, src, plan, *, dst_shape=None, local_dst=None):
    n = len(plan)
    in_place = dst_shape is None
    out_sds = jax.ShapeDtypeStruct(src.shape, src.dtype) if in_place else dst_shape

    def body(src_ref, dst_ref, send_sems, recv_sems, local_sem):
        x, y, c = lax.axis_index("x"), lax.axis_index("y"), lax.axis_index("c")

        def copy(k, sender):
            flip, src_index, dst_index = plan[k]
            sx, sy, sc = sender
            to = (_flip(sx, flip[0]), _flip(sy, flip[1]), _flip(sc, flip[2]))
            s = src_ref if src_index is None else src_ref.at[src_index(sx, sy, sc)]
            return pltpu.make_async_remote_copy(
                src_ref=s, dst_ref=dst_ref.at[dst_index(sx, sy, sc)],
                send_sem=send_sems.at[k], recv_sem=recv_sems.at[k],
                device_id=to, device_id_type=MESH)

        me = (x, y, c)
        if local_dst is not None:
            mine = pltpu.make_async_copy(src_ref, dst_ref.at[local_dst(x, y, c)], local_sem)
            mine.start()
        sends = [copy(k, me) for k in range(n)]
        for cp in sends:
            cp.start()
        for k in range(n):
            flip = plan[k][0]
            copy(k, (_flip(x, flip[0]), _flip(y, flip[1]), _flip(c, flip[2]))).wait_recv()
        for cp in sends:
            cp.wait_send()
        if local_dst is not None:
            mine.wait()

    return pl.pallas_call(
        body, name=name, out_shape=out_sds, in_specs=[ANY], out_specs=ANY,
        input_output_aliases={0: 0} if in_place else {},
        scratch_shapes=[pltpu.SemaphoreType.DMA((n,)), pltpu.SemaphoreType.DMA((n,)),
                        pltpu.SemaphoreType.DMA(())],
    )(src)


FLIP_C, FLIP_X, FLIP_Y, FLIP_XY = (0, 0, 1), (1, 0, 0), (0, 1, 0), (1, 1, 0)
CHIP_FLIPS = ((0, 0), (1, 0), (0, 1), (1, 1))


AG_PIECES = ((pl.ds(R_IN, D), pl.ds(0, 1024)), (pl.ds(R_IN, D), pl.ds(1024, 1024)),
             (pl.ds(R_OUT, R_PA - R_OUT), pl.ds(0, D)), (pl.ds(R_PA, R_ALL - R_PA), pl.ds(0, D)))
N_PIECES = len(AG_PIECES)
SIB, TO_X, TO_Y, ON, PASS_X, PASS_Y, PASS_D = range(7)
AG_TILES = ((0, 0), (0, 1), (1, 0), (1, 1), (2, 0), (4, 0), (3, 0), (5, 0),
            (2, 1), (4, 1), (3, 1), (5, 1), (6, 0), (6, 1), (7, 0), (7, 1))
W, G = "wait", "go"
AG_STEPS = {
    2: [(W, SIB, 0)], 3: [(W, SIB, 1)],
    4: [(W, TO_X, 0), (G, ON, 0), (G, PASS_X, 0)], 5: [(W, TO_Y, 0), (G, PASS_Y, 0)],
    6: [(W, PASS_X, 0)], 7: [(W, PASS_Y, 0)],
    8: [(W, TO_X, 1), (G, PASS_X, 1)],
    9: [(W, TO_Y, 1), (G, ON, 1), (G, PASS_Y, 1), (G, TO_X, 2), (G, TO_X, 3), (G, TO_Y, 2), (G, TO_Y, 3)],
    10: [(W, PASS_X, 1)], 11: [(W, PASS_Y, 1)],
    12: [(W, ON, 0), (G, PASS_D, 0)], 13: [(W, ON, 1), (G, PASS_D, 1)],
    14: [(W, PASS_D, 0), (W, TO_X, 2), (G, ON, 2), (G, PASS_X, 2), (W, TO_X, 3), (G, PASS_X, 3),
         (W, TO_Y, 2), (G, PASS_Y, 2), (W, TO_Y, 3), (G, ON, 3), (G, PASS_Y, 3)],
    15: [(W, PASS_D, 1)],
}
AG_LAST = [(W, SIB, 2), (W, SIB, 3), (W, ON, 2), (G, PASS_D, 2), (W, ON, 3), (G, PASS_D, 3),
           (W, PASS_X, 2), (W, PASS_X, 3), (W, PASS_Y, 2), (W, PASS_Y, 3), (W, PASS_D, 2), (W, PASS_D, 3)]


def _arrival_order(x, y, c):
    chips = [(x, y), (1 - x, y), (x, 1 - y), (1 - x, 1 - y)]
    return jnp.stack([_lin(px, py, pc) for px, py in chips for pc in (c, 1 - c)]).astype(jnp.int32)


def _ag_proj(order, xb, pack):
    tm, tn = 1024, 1024
    nrow, ntile = S // tm, len(AG_TILES)
    slabs = jnp.stack([order[pos] for pos, _ in AG_TILES])
    cols = jnp.stack([2 * order[pos] + half for pos, half in AG_TILES])

    def body(cols_ref, slabs_ref, x_ref, pack_ref, h_ref, gw_ref, wbuf, wsem, send_sems, recv_sems, local_sem):
        t, i = pl.program_id(0), pl.program_id(1)
        x, y, c = lax.axis_index("x"), lax.axis_index("y"), lax.axis_index("c")
        me = _lin(x, y, c)
        dev = {"sib": (x, y, 1 - c), "x": (1 - x, y, c), "y": (x, 1 - y, c), "d": (1 - x, 1 - y, c)}

        def slab_of(name, other_core=False):
            px, py, pc = dev[name]
            return _lin(px, py, 1 - pc if other_core else pc)

        def rdma(slab, kind, piece, to, from_pack=False):
            k = kind * N_PIECES + piece
            there = gw_ref.at[(slab, *AG_PIECES[piece])]
            return pltpu.make_async_remote_copy(
                src_ref=pack_ref.at[AG_PIECES[piece]] if from_pack else there, dst_ref=there,
                send_sem=send_sems.at[k], recv_sem=recv_sems.at[k], device_id=dev[to], device_id_type=MESH)

        def mine(kind, piece):
            if kind in (SIB, TO_X, TO_Y):
                return rdma(me, kind, piece, ("sib", "x", "y")[kind], from_pack=True)
            if kind == ON:
                frm, to = ("x", "y") if piece % 2 == 0 else ("y", "x")
                return rdma(slab_of(frm), kind, piece, to)
            return rdma(slab_of({PASS_X: "x", PASS_Y: "y", PASS_D: "d"}[kind]), kind, piece, "sib")

        def landing(kind, piece):
            slab = {SIB: slab_of("sib"), TO_X: slab_of("x"), TO_Y: slab_of("y"), ON: slab_of("d"),
                    PASS_X: slab_of("x", True), PASS_Y: slab_of("y", True), PASS_D: slab_of("d", True)}[kind]
            return rdma(slab, kind, piece, "sib")

        def run(steps):
            for what, kind, piece in steps:
                if what == W:
                    landing(kind, piece).wait_recv()
                else:
                    mine(kind, piece).start()

        local = pltpu.make_async_copy(pack_ref, gw_ref.at[me], local_sem)

        def fetch(slab, half, slot):
            src = pack_ref.at[AG_PIECES[half]] if slab is None else gw_ref.at[(slab, *AG_PIECES[half])]
            return pltpu.make_async_copy(src, wbuf.at[slot], wsem.at[slot])

        @pl.when((t == 0) & (i == 0))
        def _():
            local.start()
            run([(G, kind, piece) for piece in (0, 1) for kind in (TO_X, TO_Y, SIB)] + [(G, SIB, 2), (G, SIB, 3)])
            first = fetch(None, 0, 0)
            first.start()
            first.wait()

        for nxt in range(1, ntile):
            @pl.when((t == nxt - 1) & (i == nrow - 1))
            def _(nxt=nxt):
                run(AG_STEPS.get(nxt, []))
                fetch(None if AG_TILES[nxt][0] == 0 else slabs_ref[nxt], AG_TILES[nxt][1], nxt % 2).start()

        for slot in (0, 1):
            @pl.when(t % 2 == slot)
            def _(slot=slot):
                @pl.when((i == 0) & (t > 0))
                def _():
                    fetch(None, 0, slot).wait()
                h_ref[...] = _nn(x_ref[...], wbuf[slot]).astype(h_ref.dtype)

        @pl.when((t == ntile - 1) & (i == nrow - 1))
        def _():
            run(AG_LAST)
            for kind in range(7):
                for piece in range(N_PIECES):
                    mine(kind, piece).wait_send()
            local.wait()

    n_sem = 7 * N_PIECES
    grid_spec = pltpu.PrefetchScalarGridSpec(
        num_scalar_prefetch=2, grid=(ntile, nrow),
        in_specs=[pl.BlockSpec((tm, D), lambda t, i, cols, slabs: (i, 0)), ANY],
        out_specs=[pl.BlockSpec((tm, tn), lambda t, i, cols, slabs: (i, cols[t])), ANY],
        scratch_shapes=[pltpu.VMEM((2, D, tn), BF16), pltpu.SemaphoreType.DMA((2,)),
                        pltpu.SemaphoreType.DMA((n_sem,)), pltpu.SemaphoreType.DMA((n_sem,)),
                        pltpu.SemaphoreType.DMA(())])
    return pl.pallas_call(
        body, name="ag_proj", grid_spec=grid_spec,
        out_shape=[jax.ShapeDtypeStruct((S, NW), BF16), jax.ShapeDtypeStruct((N_DEV, R_ALL, D), BF16)],
        compiler_params=_cparams(2))(cols, slabs, xb, pack)


def _all_gather_direct(name, vec):
    own = lambda x, y, c: _lin(x, y, c)
    flips = [(fx, fy, fc) for fx in (0, 1) for fy in (0, 1) for fc in (0, 1) if (fx, fy, fc) != (0, 0, 0)]
    return _exchange(name, vec, [(f, None, own) for f in flips],
                     dst_shape=jax.ShapeDtypeStruct((N_DEV,) + vec.shape, vec.dtype), local_dst=own)


def _rs_sibling(p):
    plan = [(FLIP_C, (lambda x, y, c, f=f: _lin(_flip(x, f[0]), _flip(y, f[1]), 1 - c)),
             (lambda x, y, c, k=k: k)) for k, f in enumerate(CHIP_FLIPS)]
    return _exchange("rs_sibling", p, plan, dst_shape=jax.ShapeDtypeStruct((4,) + p.shape[1:], p.dtype))


def _pair_sum_small(coords, p, l1, q):
    def body(crd, p_ref, l_ref, _, q_ref, buf, sem):
        k = pl.program_id(0)
        buf[...] = (p_ref[...].astype(F32) + l_ref[...].astype(F32)).astype(buf.dtype)
        out = pltpu.make_async_copy(buf, q_ref.at[k, pl.ds(R_OUT, R_SMALL)], sem)
        out.start()
        out.wait()

    def p_map(k, crd):
        fx, fy = k % 2, k // 2
        px = crd[0] + fx - 2 * fx * crd[0]
        py = crd[1] + fy - 2 * fy * crd[1]
        return (_lin(px, py, crd[2]), 0, 0)

    grid_spec = pltpu.PrefetchScalarGridSpec(
        num_scalar_prefetch=1, grid=(4,),
        in_specs=[pl.BlockSpec((None, R_SMALL, D), p_map),
                  pl.BlockSpec((None, R_SMALL, D), lambda k, crd: (k, 0, 0)), ANY],
        out_specs=ANY,
        scratch_shapes=[pltpu.VMEM((R_SMALL, D), BF16), pltpu.SemaphoreType.DMA(())])
    return pl.pallas_call(body, name="pair_sum_small", grid_spec=grid_spec,
                          out_shape=jax.ShapeDtypeStruct(q.shape, q.dtype), input_output_aliases={3: 0},
                          compiler_params=_cparams(1))(coords, p, l1, q)


def _h_block(k):
    return jnp.where(k < 9, (k % 3) * 3 + k // 3, k)


RS_PIECES = (pl.ds(0, 1280), pl.ds(1280, R_ALL - 1280))
RS_ROWS = (1280, R_ALL - 1280)
RS_CHUNKS = ((320,) * 4, (432,) * 3)
RS_MERGE_STEP = 3


def _grad_x_rs(dh, g, dr, q):
    tm, tk = 512, 1024
    ni, nk = S // tm, NW // tk
    rmax = max(RS_ROWS)
    cmax = max(max(c) for c in RS_CHUNKS)

    def body(dh_ref, w_ref, dr_ref, q_ref, o_ref, l2_ref, ld_ref, mg_ref, va, vb, send_sems, recv_sems, sems):
        i, k = pl.program_id(0), pl.program_id(1)
        x, y, c = lax.axis_index("x"), lax.axis_index("y"), lax.axis_index("c")
        nbr = ((1 - x, y, c), (x, 1 - y, c))

        def rows(ref, piece):
            return ref.at[piece, pl.ds(0, RS_ROWS[piece])]

        copies = (
            (q_ref.at[3, RS_PIECES[0]], rows(ld_ref, 0), 0),
            (q_ref.at[3, RS_PIECES[1]], rows(ld_ref, 1), 1),
            (q_ref.at[1, RS_PIECES[0]], l2_ref.at[0, RS_PIECES[0]], 0),
            (q_ref.at[2, RS_PIECES[1]], l2_ref.at[1, RS_PIECES[1]], 1),
            (rows(mg_ref, 0), l2_ref.at[1, RS_PIECES[0]], 1),
            (rows(mg_ref, 1), l2_ref.at[0, RS_PIECES[1]], 0),
        )

        def copy(n):
            src, dst, axis = copies[n]
            return pltpu.make_async_remote_copy(src_ref=src, dst_ref=dst, send_sem=send_sems.at[n],
                                                recv_sem=recv_sems.at[n], device_id=nbr[axis], device_id_type=MESH)

        def merge(piece, mine):
            start = 0
            for n_rows in RS_CHUNKS[piece]:
                own = pltpu.make_async_copy(q_ref.at[mine, pl.ds(RS_PIECES[piece].start + start, n_rows)],
                                            va.at[pl.ds(0, n_rows)], sems.at[0])
                got = pltpu.make_async_copy(ld_ref.at[piece, pl.ds(start, n_rows)], vb.at[pl.ds(0, n_rows)], sems.at[1])
                own.start()
                got.start()
                own.wait()
                got.wait()
                va[pl.ds(0, n_rows)] = (va[pl.ds(0, n_rows)].astype(F32)
                                        + vb[pl.ds(0, n_rows)].astype(F32)).astype(va.dtype)
                out = pltpu.make_async_copy(va.at[pl.ds(0, n_rows)], mg_ref.at[piece, pl.ds(start, n_rows)], sems.at[2])
                out.start()
                out.wait()
                start += n_rows

        @pl.when((i == 0) & (k == 0))
        def _():
            for n in range(4):
                copy(n).start()

        @pl.when((i == RS_MERGE_STEP) & (k == 0))
        def _():
            copy(0).wait_recv()
            merge(0, 2)
            copy(4).start()
            copy(1).wait_recv()
            merge(1, 1)
            copy(5).start()

        @pl.when(k == 0)
        def _():
            o_ref[...] = ALPHA * dr_ref[...]

        o_ref[...] += _nt(dh_ref[...], w_ref[...])

        @pl.when((i == ni - 1) & (k == nk - 1))
        def _():
            for n in range(2, 6):
                copy(n).wait_recv()
            for n in range(6):
                copy(n).wait_send()

    slab = q.shape[1:]
    out = pl.pallas_call(
        body, name="grad_x_rs", grid=(ni, nk),
        in_specs=[pl.BlockSpec((tm, tk), lambda i, k: (i, k)),
                  pl.BlockSpec((None, D, tk), lambda i, k: (_h_block(k) // 2, 0, _h_block(k) % 2)),
                  pl.BlockSpec((tm, D), lambda i, k: (i, 0)), ANY],
        out_specs=[pl.BlockSpec((tm, D), lambda i, k: (i, 0)), ANY, ANY, ANY],
        out_shape=[jax.ShapeDtypeStruct((S, D), F32), jax.ShapeDtypeStruct((2,) + slab, q.dtype),
                   jax.ShapeDtypeStruct((2, rmax, slab[1]), q.dtype), jax.ShapeDtypeStruct((2, rmax, slab[1]), q.dtype)],
        scratch_shapes=[pltpu.VMEM((cmax, slab[1]), q.dtype), pltpu.VMEM((cmax, slab[1]), q.dtype),
                        pltpu.SemaphoreType.DMA((6,)), pltpu.SemaphoreType.DMA((6,)), pltpu.SemaphoreType.DMA((3,))],
        compiler_params=_cparams(2))(dh, g, dr, q)
    return out[0], out[1]


def _rs_columns(x, y, c):
    out = []
    for core in (1 - c, c):
        for fx, fy in CHIP_FLIPS:
            for half in (0, 1):
                out.append(_h_block(2 * _lin(_flip(x, fx), _flip(y, fy), core) + half))
    return jnp.stack(out).astype(jnp.int32)


def _grad_w_in_rs(cols, xt, dh):
    tn, tk = 1024, 1024
    nk = S // tk
    n_half = 8

    def body(cols_ref, a_ref, b_ref, q_ref, l1_ref, acc_ref, stage, landed, send_sems, recv_sems, sem):
        t, k = pl.program_id(0), pl.program_id(1)
        sib = (lax.axis_index("x"), lax.axis_index("y"), 1 - lax.axis_index("c"))

        @pl.when(k == 0)
        def _():
            acc_ref[...] = jnp.zeros_like(acc_ref)

        acc_ref[...] += _nn(a_ref[...], b_ref[...])

        def there(n):
            return l1_ref.at[n // 2, :, pl.ds((n % 2) * tn, tn)]

        def send(n):
            return pltpu.make_async_remote_copy(src_ref=stage.at[n % 2], dst_ref=there(n), send_sem=send_sems.at[n],
                                                recv_sem=recv_sems.at[n], device_id=sib, device_id_type=MESH)

        for n in range(n_half):
            @pl.when((t == n) & (k == nk - 1))
            def _(n=n):
                if n >= 2:
                    send(n - 2).wait_send()
                stage[n % 2] = acc_ref[...].astype(stage.dtype)
                send(n).start()

        for n in range(n_half):
            @pl.when((t == n_half + n) & (k == nk - 1))
            def _(n=n):
                if n == 0:
                    send(n_half - 2).wait_send()
                    send(n_half - 1).wait_send()
                send(n).wait_recv()
                fetch = pltpu.make_async_copy(there(n), landed, sem)
                fetch.start()
                fetch.wait()
                q_ref[...] = (acc_ref[...] + landed[...].astype(F32)).astype(q_ref.dtype)

    mine = lambda t: jnp.maximum(t - n_half, 0)
    grid_spec = pltpu.PrefetchScalarGridSpec(
        num_scalar_prefetch=1, grid=(2 * n_half, nk),
        in_specs=[pl.BlockSpec((D, tk), lambda t, k, cols: (0, k)),
                  pl.BlockSpec((tk, tn), lambda t, k, cols: (k, cols[t]))],
        out_specs=[pl.BlockSpec((None, D, tn), lambda t, k, cols: (mine(t) // 2, 0, mine(t) % 2)), ANY],
        scratch_shapes=[pltpu.VMEM((D, tn), F32), pltpu.VMEM((2, D, tn), BF16), pltpu.VMEM((D, tn), BF16),
                        pltpu.SemaphoreType.DMA((n_half,)), pltpu.SemaphoreType.DMA((n_half,)),
                        pltpu.SemaphoreType.DMA(())])
    q, _ = pl.pallas_call(
        body, name="grad_w_in_rs", grid_spec=grid_spec,
        out_shape=[jax.ShapeDtypeStruct((4, R_ALL, D), BF16), jax.ShapeDtypeStruct((4, D, D), BF16)],
        compiler_params=_cparams(2))(cols, xt, dh)
    return q


def _grad_w(name, at, b, out_shape, out_spec, tn, tk):
    m, k_all = at.shape
    n_all = b.shape[1]
    nk = k_all // tk

    def body(a_ref, b_ref, o_ref, acc_ref):
        k = pl.program_id(1)

        @pl.when(k == 0)
        def _():
            acc_ref[...] = jnp.zeros_like(acc_ref)

        acc_ref[...] += _nn(a_ref[...], b_ref[...])

        @pl.when(k == nk - 1)
        def _():
            o_ref[...] = acc_ref[...].astype(o_ref.dtype)

    return pl.pallas_call(
        body, name=name, grid=(n_all // tn, nk),
        in_specs=[pl.BlockSpec((m, tk), lambda n, k: (0, k)),
                  pl.BlockSpec((tk, tn), lambda n, k: (k, n))],
        out_specs=out_spec, out_shape=out_shape,
        scratch_shapes=[pltpu.VMEM((m, tn), F32)], compiler_params=_cparams(2))(at, b)


NR = 16
TI = 16
TM = NR * TI
NT = S // TM
ATT_QB = (256, 128, 128)
ATT_NB = (16, 8, 2)
ATT_BLOCKS = (16, 32, 32)


def _permute_tokens(a):
    return a.reshape(NT, TI, NR, a.shape[-1]).transpose(0, 2, 1, 3).reshape(a.shape)


def _attn_shape(g, c):
    if g == 0:
        return (S, c)
    if g == 1:
        return (NT, 4, 4, TI, c)
    return (NT, NR, TI, c)


def _attn_view(g, a):
    return a.reshape(_attn_shape(g, a.shape[-1]))


def _attn_spec(g, width, col, blk):
    if g == 0:
        return pl.BlockSpec((TM, width), lambda b: (blk(b), col))
    if g == 1:
        return pl.BlockSpec((2, 4, None, TI, width), lambda b: (blk(b) % 8, 0, blk(b) // 8, 0, col))
    return pl.BlockSpec((8, None, TI, width), lambda b: (blk(b) % 2, blk(b) // 2, 0, col))


def _pieces(g):
    if g == 1:
        return [(t, m) for t in range(2) for m in range(4)]
    return [(t,) for t in range(8)]


def _get(g, ref, sl):
    if g == 0:
        return ref[:, sl]
    return jnp.concatenate([ref[(*p, slice(None), sl)] for p in _pieces(g)], axis=0)


def _put(g, ref, sl, val):
    if g == 0:
        ref[:, sl] = val
    else:
        for n, p in enumerate(_pieces(g)):
            ref[(*p, slice(None), sl)] = val[TI * n:TI * (n + 1)]


def _block_pos(g, a):
    if g == 0:
        return 16 * (a % 16) + a // 16
    if g == 1:
        return 64 * (a // 64) + 4 * (a % 16) + (a // 16) % 4
    return a


def _attn_mask(g, n):
    qb = ATT_QB[g]
    qa = lax.broadcasted_iota(jnp.int32, (qb, 2 * qb), 0)
    kc = lax.broadcasted_iota(jnp.int32, (qb, 2 * qb), 1)
    cur = kc >= qb
    dist = _block_pos(g, qa) - _block_pos(g, kc % qb) + jnp.where(cur, 0, qb)
    return (dist >= 0) & (dist <= QB) & (cur | (n > 0))


def _qkv_specs(g, clamp):
    cur = lambda col: _attn_spec(g, AW, col, clamp)
    prev = lambda col: _attn_spec(g, AW, col, lambda b: jnp.maximum(clamp(b) - 1, 0))
    qc, kc, vc = (c // AW + g for c in (COL_Q, COL_K, COL_V))
    return [cur(qc), cur(kc), prev(kc), cur(vc), prev(vc)]


def _attn_fwd(g, h):
    scale = HD ** -0.5
    hv = _attn_view(g, h)

    def body(q_ref, kc_ref, kp_ref, vc_ref, vp_ref, o_ref, l_ref):
        valid = _attn_mask(g, pl.program_id(0) % ATT_NB[g])
        for hh in range(NH):
            sl = slice(hh * HD, (hh + 1) * HD)
            kh = jnp.concatenate([_get(g, kp_ref, sl), _get(g, kc_ref, sl)], axis=0)
            vh = jnp.concatenate([_get(g, vp_ref, sl), _get(g, vc_ref, sl)], axis=0)
            s = jnp.where(valid, _nt(_get(g, q_ref, sl), kh) * scale, NEG_INF)
            m = jnp.max(s, axis=-1, keepdims=True)
            e = jnp.exp(s - m)
            den = jnp.sum(e, axis=-1, keepdims=True)
            _put(g, o_ref, sl, (_nn(e.astype(BF16), vh) / den).astype(o_ref.dtype))
            _put(g, l_ref, slice(hh, hh + 1), m + jnp.log(den))

    same = lambda b: b
    o, lse = pl.pallas_call(
        body, name=f"attn_fwd_{g}", grid=(ATT_BLOCKS[g],),
        in_specs=_qkv_specs(g, same),
        out_specs=[_attn_spec(g, AW, 0, same), _attn_spec(g, NH, 0, same)],
        out_shape=[jax.ShapeDtypeStruct(_attn_shape(g, AW), BF16), jax.ShapeDtypeStruct(_attn_shape(g, NH), F32)],
        compiler_params=_cparams(1))(hv, hv, hv, hv, hv)
    return o.reshape(S, AW), lse.reshape(S, NH)


def _attn_bwd(g, dh, h, do, lse, delta):
    scale = HD ** -0.5
    qb = ATT_QB[g]
    last = ATT_BLOCKS[g] - 1
    clamp = lambda b: jnp.minimum(b, last)
    behind = lambda b: jnp.maximum(b - 1, 0)
    hv = _attn_view(g, h)

    def body(q_ref, kc_ref, kp_ref, vc_ref, vp_ref, do_ref, l_ref, dl_ref, _, dh_ref, cq_ref, ck_ref, cv_ref):
        b = pl.program_id(0)

        def write(col, val):
            _put(g, dh_ref, slice(col, col + HD), val.astype(dh_ref.dtype))

        @pl.when(b == 0)
        def _():
            cq_ref[...] = jnp.zeros_like(cq_ref)
            ck_ref[...] = jnp.zeros_like(ck_ref)
            cv_ref[...] = jnp.zeros_like(cv_ref)

        @pl.when(b <= last)
        def _():
            valid = _attn_mask(g, b % ATT_NB[g])
            for hh in range(NH):
                sl = slice(hh * HD, (hh + 1) * HD)
                one = slice(hh, hh + 1)
                qh, doh = _get(g, q_ref, sl), _get(g, do_ref, sl)
                kh = jnp.concatenate([_get(g, kp_ref, sl), _get(g, kc_ref, sl)], axis=0)
                vh = jnp.concatenate([_get(g, vp_ref, sl), _get(g, vc_ref, sl)], axis=0)
                s = _nt(qh, kh) * scale
                p = jnp.where(valid, jnp.exp(s - _get(g, l_ref, one)), 0.0)
                ds = p * (_nt(doh, vh) - _get(g, dl_ref, one))
                dsb = (ds * scale).astype(BF16)
                dk2 = _tn(dsb, qh)
                dv2 = _tn(p.astype(BF16), doh)
                write(hh * HD, cq_ref[:, sl])
                write(AW + hh * HD, ck_ref[:, sl] + dk2[:qb])
                write(2 * AW + hh * HD, cv_ref[:, sl] + dv2[:qb])
                cq_ref[:, sl] = _nn(dsb, kh)
                ck_ref[:, sl] = dk2[qb:]
                cv_ref[:, sl] = dv2[qb:]

        @pl.when(b > last)
        def _():
            for hh in range(NH):
                sl = slice(hh * HD, (hh + 1) * HD)
                write(hh * HD, cq_ref[:, sl])
                write(AW + hh * HD, ck_ref[:, sl])
                write(2 * AW + hh * HD, cv_ref[:, sl])

    out = pl.pallas_call(
        body, name=f"attn_bwd_{g}", grid=(ATT_BLOCKS[g] + 1,),
        in_specs=_qkv_specs(g, clamp) + [_attn_spec(g, AW, 0, clamp), _attn_spec(g, NH, 0, clamp),
                                         _attn_spec(g, NH, 0, clamp), ANY],
        out_specs=_attn_spec(g, 3 * AW, g, behind),
        out_shape=jax.ShapeDtypeStruct(_attn_shape(g, NW), BF16),
        input_output_aliases={8: 0},
        scratch_shapes=[pltpu.VMEM((qb, AW), F32)] * 3,
        compiler_params=_cparams(1))(hv, hv, hv, hv, hv, _attn_view(g, do), _attn_view(g, lse), _attn_view(g, delta),
                                     _attn_view(g, dh))
    return out.reshape(S, NW)


def _group_weights(l0, l1, l2):
    m = jnp.maximum(jnp.maximum(l0, l1), l2)
    e0, e1, e2 = jnp.exp(l0 - m), jnp.exp(l1 - m), jnp.exp(l2 - m)
    inv = 1.0 / (e0 + e1 + e2)
    return e0 * inv, e1 * inv, e2 * inv


def _residue(ref, r, sl):
    return ref[r * TI:(r + 1) * TI, sl].astype(F32)


def _total(parts):
    return functools.reduce(lambda x, y: x + y, parts)


def _pool_tokens(up_ref, uc_ref, p_ref, tile):
    j0 = lax.broadcasted_iota(jnp.int32, (TI, 1), 0) == 0
    first = (tile == 0) & j0
    for r in range(NR):
        out = []
        for g, w in enumerate(POOL_WINDOWS):
            sl = slice(g * PG, (g + 1) * PG)
            own = _residue(uc_ref, r, sl)
            acc = _total([own] + [_residue(uc_ref, r - k, sl) for k in range(1, min(r, w - 1) + 1)])
            wrapped = [NR + r - k for k in range(r + 1, w)]
            if wrapped:
                wc = _total([_residue(uc_ref, q, sl) for q in wrapped])
                wp = jnp.where(tile > 0, _total([_residue(up_ref, q, sl) for q in wrapped]), 0.0)
                acc = acc + jnp.where(j0, pltpu.roll(wp, 1, 0), pltpu.roll(wc, 1, 0))
            out.append(acc / jnp.where(first, float(min(r + 1, w)), float(w)) - own)
        p_ref[r * TI:(r + 1) * TI, :] = jnp.concatenate(out, axis=1).astype(p_ref.dtype)


def _pool_tokens_bwd(dp, nxt_ref, du_ref, tile):
    ji = lax.broadcasted_iota(jnp.int32, (TI, 1), 0)
    first = (tile == 0) & (ji == 0)
    piece = lambda g, r: dp[g][r * TI:(r + 1) * TI]
    dpc = [[piece(g, r) / jnp.where(first, float(min(r + 1, w)), float(w)) for r in range(NR)]
           for g, w in enumerate(POOL_WINDOWS)]
    for r in range(NR):
        out = []
        for g, w in enumerate(POOL_WINDOWS):
            sl = slice(g * PG, (g + 1) * PG)
            acc = _total([dpc[g][r + k] for k in range(w) if r + k < NR])
            wrapped = [r + k - NR for k in range(1, w) if r + k >= NR]
            if wrapped:
                wc = _total([dpc[g][q] for q in wrapped])
                wn = _total([nxt_ref[q * TI:(q + 1) * TI, sl] for q in wrapped])
                acc = acc + jnp.where(ji == TI - 1, pltpu.roll(wn, TI - 1, 0), pltpu.roll(wc, TI - 1, 0))
            out.append(acc - piece(g, r))
        du_ref[r * TI:(r + 1) * TI, :] = jnp.concatenate(out, axis=1).astype(du_ref.dtype)
    for r in range(NR):
        nxt_ref[r * TI:(r + 1) * TI, :] = jnp.concatenate([dpc[g][r] for g in range(len(POOL_WINDOWS))], axis=1)


def _pool_linear(pb, wpool_ref):
    return jnp.concatenate([_nn(pb[:, g * PG:(g + 1) * PG], wpool_ref[g]) for g in range(len(POOL_WINDOWS))], axis=1)


def _tok(width, col=0, rev=False):
    if rev:
        return pl.BlockSpec((TM, width), lambda i: (NT - 1 - i, col))
    return pl.BlockSpec((TM, width), lambda i: (i, col))


def _whole(shape):
    return pl.BlockSpec(shape, lambda i: (0,) * len(shape))


def _mix_fwd(h, o, lse, wpa, wpp, wpool, pscale, bgate):
    def body(o0_ref, o1_ref, o2_ref, l0_ref, l1_ref, l2_ref, za_ref, uc_ref, up_ref, zp_ref, gp_ref,
             wpa_ref, wpp_ref, wpool_ref, ps_ref, bg_ref, ya_ref, yp_ref, mg_ref, a_ref, b_ref, p_ref):
        i = pl.program_id(0)
        w0, w1, w2 = _group_weights(l0_ref[...], l1_ref[...], l2_ref[...])
        za = za_ref[...].astype(F32)
        silu_a = za * _sigmoid(za)
        for hh in range(NH):
            sl = slice(hh * HD, (hh + 1) * HD)
            c = slice(hh, hh + 1)
            oh = (w0[:, c] * o0_ref[:, sl].astype(F32) + w1[:, c] * o1_ref[:, sl].astype(F32)
                  + w2[:, c] * o2_ref[:, sl].astype(F32))
            ya_ref[:, sl] = (oh * silu_a[:, sl]).astype(BF16)
        _pool_tokens(up_ref, uc_ref, p_ref, i)
        zp = zp_ref[...].astype(F32)
        yp_ref[...] = (_pool_linear(p_ref[...], wpool_ref) * ps_ref[...] * (zp * _sigmoid(zp))).astype(BF16)
        a = _nn(ya_ref[...], wpa_ref[...])
        b = _nn(yp_ref[...], wpp_ref[...])
        a_ref[...] = a.astype(BF16)
        b_ref[...] = b.astype(BF16)
        gates = _sigmoid(gp_ref[...].astype(F32) + bg_ref[...])
        mg_ref[...] = (gates[:, :D] * a + gates[:, D:] * b).astype(BF16)

    u_prev = pl.BlockSpec((TM, AW), lambda i: (jnp.maximum(i - 1, 0), COL_U // AW))
    return pl.pallas_call(
        body, name="mix_fwd", grid=(NT,),
        in_specs=[_tok(AW)] * 3 + [_tok(NH)] * 3
        + [_tok(AW, COL_ZA // AW), _tok(AW, COL_U // AW), u_prev, _tok(AW, COL_ZP // AW), _tok(2 * D, COL_G // (2 * D))]
        + [_whole((AW, D)), _whole((AW, D)), _whole((4, PG, PG)), _whole((1, AW)), _whole((1, 2 * D))],
        out_specs=[_tok(AW), _tok(AW), _tok(D), _tok(D), _tok(D), _tok(AW)],
        out_shape=[jax.ShapeDtypeStruct((S, AW), BF16)] * 2 + [jax.ShapeDtypeStruct((S, D), BF16)] * 3
        + [jax.ShapeDtypeStruct((S, AW), BF16)],
        compiler_params=_cparams(1))(*o, *lse, h, h, h, h, h, wpa, wpp, wpool, pscale, bgate)


def _out_ln(merged, x, target, wout, gamma, beta):
    def body(mg_ref, x_ref, t_ref, w_ref, g_ref, b_ref, dr_ref, drb_ref, dm_ref, loss_ref, dg_ref, db_ref):
        i = pl.program_id(0)

        @pl.when(i == 0)
        def _():
            loss_ref[...] = jnp.zeros_like(loss_ref)
            dg_ref[...] = jnp.zeros_like(dg_ref)
            db_ref[...] = jnp.zeros_like(db_ref)

        r = ALPHA * x_ref[...] + _nn(mg_ref[...], w_ref[...])
        mu = jnp.mean(r, axis=-1, keepdims=True)
        rc = r - mu
        rstd = lax.rsqrt(jnp.mean(rc * rc, axis=-1, keepdims=True) + LN_EPS)
        xhat = rc * rstd
        err = xhat * g_ref[...] + b_ref[...] - t_ref[...]
        loss_ref[...] += 0.5 * jnp.sum(jnp.mean(err * err, axis=-1, keepdims=True), axis=0, keepdims=True)
        dy = err * (1.0 / D)
        dg_ref[...] += jnp.sum(dy * xhat, axis=0, keepdims=True)
        db_ref[...] += jnp.sum(dy, axis=0, keepdims=True)
        dxh = dy * g_ref[...]
        dr = rstd * (dxh - jnp.mean(dxh, axis=-1, keepdims=True)
                     - xhat * jnp.mean(dxh * xhat, axis=-1, keepdims=True))
        dr_ref[...] = dr
        drb_ref[...] = dr.astype(BF16)
        dm_ref[...] = _nt(drb_ref[...], w_ref[...]).astype(BF16)

    return pl.pallas_call(
        body, name="out_ln", grid=(NT,),
        in_specs=[_tok(D), _tok(D), _tok(D), _whole((D, D)), _whole((1, D)), _whole((1, D))],
        out_specs=[_tok(D), _tok(D), _tok(D), _whole((8, 128)), _whole((1, D)), _whole((1, D))],
        out_shape=[jax.ShapeDtypeStruct((S, D), F32), jax.ShapeDtypeStruct((S, D), BF16),
                   jax.ShapeDtypeStruct((S, D), BF16), jax.ShapeDtypeStruct((8, 128), F32),
                   jax.ShapeDtypeStruct((1, D), F32), jax.ShapeDtypeStruct((1, D), F32)],
        compiler_params=_cparams(1))(merged, x, target, wout, gamma, beta)


def _gate_bwd(dm, a, b, h, bgate):
    def body(dm_ref, a_ref, b_ref, gp_ref, bg_ref, dgp_ref, da_ref, db_ref, dbg_ref):
        @pl.when(pl.program_id(0) == 0)
        def _():
            dbg_ref[...] = jnp.zeros_like(dbg_ref)

        dm_ = dm_ref[...].astype(F32)
        gates = _sigmoid(gp_ref[...].astype(F32) + bg_ref[...])
        ga, gb = gates[:, :D], gates[:, D:]
        da_ref[...] = (dm_ * ga).astype(BF16)
        db_ref[...] = (dm_ * gb).astype(BF16)
        dgp = jnp.concatenate([dm_ * a_ref[...].astype(F32) * ga * (1.0 - ga),
                               dm_ * b_ref[...].astype(F32) * gb * (1.0 - gb)], axis=1)
        dgp_ref[...] = dgp.astype(BF16)
        dbg_ref[...] += jnp.sum(dgp, axis=0, keepdims=True)

    return pl.pallas_call(
        body, name="gate_bwd", grid=(NT,),
        in_specs=[_tok(D), _tok(D), _tok(D), _tok(2 * D, COL_G // (2 * D)), _whole((1, 2 * D))],
        out_specs=[_tok(2 * D, DH_G // (2 * D)), _tok(D), _tok(D), _whole((1, 2 * D))],
        out_shape=[jax.ShapeDtypeStruct((S, NW), BF16), jax.ShapeDtypeStruct((S, D), BF16),
                   jax.ShapeDtypeStruct((S, D), BF16), jax.ShapeDtypeStruct((1, 2 * D), F32)],
        compiler_params=_cparams(1))(dm, a, b, h, bgate)


def _mix_bwd(dh, da, db, h, o, lse, p, wpa, wpp, wpool, pscale):
    def body(_, da_ref, db_ref, o0_ref, o1_ref, o2_ref, l0_ref, l1_ref, l2_ref, za_ref, zp_ref, p_ref,
             wpa_ref, wpp_ref, wpool_ref, ps_ref,
             dh_ref, do0_ref, do1_ref, do2_ref, dl0_ref, dl1_ref, dl2_ref, dwp_ref, dps_ref,
             nxt_ref):
        i = pl.program_id(0)
        tile = NT - 1 - i
        dza_ref, du_ref, dzp_ref = (dh_ref.at[:, pl.ds(n * AW, AW)] for n in range(3))

        @pl.when(i == 0)
        def _():
            nxt_ref[...] = jnp.zeros_like(nxt_ref)
            dwp_ref[...] = jnp.zeros_like(dwp_ref)
            dps_ref[...] = jnp.zeros_like(dps_ref)

        dya = _nt(da_ref[...], wpa_ref[...])
        w0, w1, w2 = _group_weights(l0_ref[...], l1_ref[...], l2_ref[...])
        za = za_ref[...].astype(F32)
        sig = _sigmoid(za)
        silu_a = za * sig
        dsilu_a = sig * (1.0 + za * (1.0 - sig))
        for hh in range(NH):
            sl = slice(hh * HD, (hh + 1) * HD)
            c = slice(hh, hh + 1)
            oh = (w0[:, c] * o0_ref[:, sl].astype(F32) + w1[:, c] * o1_ref[:, sl].astype(F32)
                  + w2[:, c] * o2_ref[:, sl].astype(F32))
            doh = dya[:, sl] * silu_a[:, sl]
            dza_ref[:, sl] = (dya[:, sl] * oh * dsilu_a[:, sl]).astype(BF16)
            dot_ = jnp.sum(doh * oh, axis=-1, keepdims=True)
            do0_ref[:, sl] = (w0[:, c] * doh).astype(BF16)
            do1_ref[:, sl] = (w1[:, c] * doh).astype(BF16)
            do2_ref[:, sl] = (w2[:, c] * doh).astype(BF16)
            dl0_ref[:, c] = w0[:, c] * dot_
            dl1_ref[:, c] = w1[:, c] * dot_
            dl2_ref[:, c] = w2[:, c] * dot_
        dyp = _nt(db_ref[...], wpp_ref[...])
        pb = p_ref[...]
        pw = _pool_linear(pb, wpool_ref)
        zp = zp_ref[...].astype(F32)
        sigp = _sigmoid(zp)
        dypre = dyp * (zp * sigp)
        dzp_ref[...] = (dyp * (pw * ps_ref[...]) * (sigp * (1.0 + zp * (1.0 - sigp)))).astype(BF16)
        dps_ref[...] += jnp.sum(dypre * pw, axis=0, keepdims=True)
        dpw = (dypre * ps_ref[...]).astype(BF16)
        dp = []
        for g in range(len(POOL_WINDOWS)):
            sl = slice(g * PG, (g + 1) * PG)
            dwp_ref[g] += _tn(pb[:, sl], dpw[:, sl])
            dp.append(_nt(dpw[:, sl], wpool_ref[g]))
        _pool_tokens_bwd(dp, nxt_ref, du_ref, tile)

    r = functools.partial(_tok, rev=True)
    return pl.pallas_call(
        body, name="mix_bwd", grid=(NT,),
        in_specs=[ANY, r(D), r(D)] + [r(AW)] * 3 + [r(NH)] * 3 + [r(AW, COL_ZA // AW), r(AW, COL_ZP // AW), r(AW)]
        + [_whole((AW, D)), _whole((AW, D)), _whole((4, PG, PG)), _whole((1, AW))],
        out_specs=[r(3 * AW, DH_Z // (3 * AW))] + [r(AW)] * 3 + [r(NH)] * 3 + [_whole((4, PG, PG)), _whole((1, AW))],
        out_shape=[jax.ShapeDtypeStruct((S, NW), BF16)] + [jax.ShapeDtypeStruct((S, AW), BF16)] * 3
        + [jax.ShapeDtypeStruct((S, NH), F32)] * 3
        + [jax.ShapeDtypeStruct((4, PG, PG), F32), jax.ShapeDtypeStruct((1, AW), F32)],
        input_output_aliases={0: 0},
        scratch_shapes=[pltpu.VMEM((TM, AW), F32)],
        compiler_params=_cparams(1))(dh, da, db, *o, *lse, h, h, p, wpa, wpp, wpool, pscale)


def _adamw(w, g, m, v):
    m = B1 * m + (1.0 - B1) * g
    v = B2 * v + (1.0 - B2) * jnp.square(g)
    m_hat = m / (1.0 - B1 ** STEP)
    v_hat = v / (1.0 - B2 ** STEP)
    return -LR * (m_hat / (jnp.sqrt(v_hat) + EPS) + WD * w), m, v


def _adam_shard(name, q, l2, w, m, v, tr):
    rows = w.shape[0]

    def body(q_ref, l_ref, w_ref, m_ref, v_ref, g_out, d_out, m_out, v_out):
        g = q_ref[...].astype(F32)
        for k in range(2):
            g = g + l_ref[k].astype(F32)
        g_out[...] = g
        d_out[...], m_out[...], v_out[...] = _adamw(w_ref[...], g, m_ref[...], v_ref[...])

    blk = pl.BlockSpec((tr, D), lambda i: (i, 0))
    return pl.pallas_call(
        body, name=name, grid=(rows // tr,),
        in_specs=[pl.BlockSpec((None, tr, D), lambda i: (0, i, 0)), pl.BlockSpec((2, tr, D), lambda i: (0, i, 0)),
                  blk, blk, blk],
        out_specs=[blk] * 4, out_shape=[jax.ShapeDtypeStruct((rows, D), F32)] * 4,
        compiler_params=_cparams(1))(q, l2, w, m, v)


def _adam_replicated(gathered, w, m, v):
    def body(g_ref, w_ref, m_ref, v_ref, g_out, d_out, m_out, v_out):
        g = g_ref[0]
        for k in range(1, N_DEV):
            g = g + g_ref[k]
        g_out[...] = g
        d_out[...], m_out[...], v_out[...] = _adamw(w_ref[...], g, m_ref[...], v_ref[...])

    return pl.pallas_call(body, name="adam_replicated", out_shape=[jax.ShapeDtypeStruct((8, D), F32)] * 4,
                          compiler_params=pltpu.CompilerParams(vmem_limit_bytes=VMEM_LIMIT))(gathered, w, m, v)


def _pack_small(w_out, w_pa, w_pp, w_pool):
    return jnp.concatenate([w_out, w_pa.reshape(-1, D), w_pp.reshape(-1, D), w_pool.reshape(-1, D)], axis=0)


def _unpack_small(a):
    o = R_OUT
    return (a[:R_PA - o], a[R_PA - o:R_PP - o].reshape(AW, 256), a[R_PP - o:R_PL - o].reshape(AW, 256),
            a[R_PL - o:].reshape(4, 32, PG))


def _pack_vec(b_gate, gamma, beta, pscale, extra):
    z = jnp.zeros((D,), F32)
    return jnp.stack([b_gate[:D], b_gate[D:], gamma, beta, jnp.concatenate([pscale, z[:D - AW]]),
                      jnp.broadcast_to(extra, (D,)), z, z])


def _unpack_vec(a):
    return jnp.concatenate([a[0], a[1]])[None], a[4, :AW][None], a[2][None], a[3][None]


def kernel(x, w_in, b_gate, w_pool, pool_scale, w_proj_attn, w_proj_pool, w_out, ln_gamma, ln_beta, loss_target, m_w_in, m_b_gate, m_w_pool, m_pool_scale, m_w_proj_attn, m_w_proj_pool, m_w_out, m_ln_gamma, m_ln_beta, v_w_in, v_b_gate, v_w_pool, v_pool_scale, v_w_proj_attn, v_w_proj_pool, v_w_out, v_ln_gamma, v_ln_beta):
    coords = jnp.stack([lax.axis_index("x"), lax.axis_index("y"), lax.axis_index("c")]).astype(jnp.int32)
    x2, tgt = _permute_tokens(x[0]), _permute_tokens(loss_target[0])
    xb = x2.astype(BF16)
    xt = x2.T.astype(BF16)

    pack = jnp.concatenate([w_in[0].astype(BF16),
                            _pack_small(w_out[0], w_proj_attn[0], w_proj_pool[0], w_pool[0]).astype(BF16)], axis=0)
    h, gw = _ag_proj(_arrival_order(*coords), xb, pack)
    wout = gw[:, R_OUT:R_PA].reshape(D, D)
    wpa = gw[:, R_PA:R_PP].reshape(N_DEV, AW, 256).transpose(1, 0, 2).reshape(AW, D)
    wpp = gw[:, R_PP:R_PL].reshape(N_DEV, AW, 256).transpose(1, 0, 2).reshape(AW, D)
    wpool = gw[:, R_PL:].reshape(N_DEV, 4, 32, PG).transpose(1, 0, 2, 3).reshape(4, PG, PG)

    o, lse = zip(*[_attn_fwd(g, h) for g in range(len(DILATIONS))])
    ya, yp, merged, a, b, p = _mix_fwd(h, o, lse, wpa, wpp, wpool, pool_scale, b_gate)
    dr, drb, dm, loss_part, dgamma, dbeta = _out_ln(merged, x2, tgt, wout, ln_gamma, ln_beta)

    dh, da, db, dbgate = _gate_bwd(dm, a, b, h, b_gate)
    dh, do0, do1, do2, dl0, dl1, dl2, dwpool, dpscale = _mix_bwd(
        dh, da, db, h, o, lse, p, wpa, wpp, wpool, pool_scale)
    for g, (do_g, dl_g) in enumerate(zip((do0, do1, do2), (dl0, dl1, dl2))):
        dh = _attn_bwd(g, dh, h, do_g, lse[g], dl_g)

    q = _grad_w_in_rs(_rs_columns(*coords), xt, dh)
    flat = lambda n, k: (0, n)
    d_wout = _grad_w("grad_w_out", merged.T, drb, jax.ShapeDtypeStruct((D, D), BF16),
                     pl.BlockSpec((D, 1024), flat), 1024, 1024)
    d_wpa = _grad_w("grad_w_pa", ya.T, da, jax.ShapeDtypeStruct((AW, D), BF16),
                    pl.BlockSpec((AW, 1024), flat), 1024, 1024)
    d_wpp = _grad_w("grad_w_pp", yp.T, db, jax.ShapeDtypeStruct((AW, D), BF16),
                    pl.BlockSpec((AW, 1024), flat), 1024, 1024)
    small = jnp.concatenate([
        d_wout.reshape(N_DEV, 256, D),
        d_wpa.reshape(AW, N_DEV, 256).transpose(1, 0, 2).reshape(N_DEV, -1, D),
        d_wpp.reshape(AW, N_DEV, 256).transpose(1, 0, 2).reshape(N_DEV, -1, D),
        dwpool.astype(BF16).reshape(4, N_DEV, 32, PG).transpose(1, 0, 2, 3).reshape(N_DEV, -1, D)], axis=1)
    q = _pair_sum_small(coords, small, _rs_sibling(small), q)

    grad_x, l2 = _grad_x_rs(dh, gw, dr, q)
    g_in, d_in, m_in, v_in = _adam_shard("adam_w_in", q, l2, w_in[0], m_w_in[0], v_w_in[0], 256)
    outs_small = _adam_shard(
        "adam_small", q[:, R_OUT:], l2[:, R_OUT:],
        _pack_small(w_out[0], w_proj_attn[0], w_proj_pool[0], w_pool[0]),
        _pack_small(m_w_out[0], m_w_proj_attn[0], m_w_proj_pool[0], m_w_pool[0]),
        _pack_small(v_w_out[0], v_w_proj_attn[0], v_w_proj_pool[0], v_w_pool[0]), 176)
    small_parts = [_unpack_small(t) for t in outs_small]

    vec = _pack_vec(dbgate[0], dgamma[0], dbeta[0], dpscale[0], loss_part[0, 0])
    outs_vec = _adam_replicated(
        _all_gather_direct("ag_vec", vec),
        _pack_vec(b_gate[0], ln_gamma[0], ln_beta[0], pool_scale[0], 0.0),
        _pack_vec(m_b_gate[0], m_ln_gamma[0], m_ln_beta[0], m_pool_scale[0], 0.0),
        _pack_vec(v_b_gate[0], v_ln_gamma[0], v_ln_beta[0], v_pool_scale[0], 0.0))
    vec_parts = [_unpack_vec(t) for t in outs_vec]
    loss = outs_vec[0][5, 0]

    def leaves(kind, big):
        out, pa, pp, pool = small_parts[kind]
        bg, ps, gm, bt = vec_parts[kind]
        return [big[None], bg, pool[None], ps, pa[None], pp[None], out[None], gm, bt]

    return (loss, _permute_tokens(grad_x)[None], *leaves(0, g_in), *leaves(1, d_in), *leaves(2, m_in), *leaves(3, v_in))
```

```python
import functools

import jax
import jax.numpy as jnp
from jax import lax
from jax.experimental import pallas as pl
from jax.experimental.pallas import tpu as pltpu

F32 = jnp.float32
BF16 = jnp.bfloat16

S = 4096
D = 2048
NW = 16384
AW = 1024
HD = 128
NH = 8
QB = 128
NBLK = S // QB
DILATIONS = (1, 4, 16)
POOL_WINDOWS = (2, 4, 8, 16)
PG = 256
N_DEV = 8
COL_Q, COL_K, COL_V = 0, 3 * AW, 6 * AW
COL_ZA, COL_U, COL_ZP, COL_G = 9 * AW, 10 * AW, 11 * AW, 12 * AW
DH_Z, DH_G = COL_ZA, COL_G
ALPHA = 2.0 ** 0.25
LN_EPS = 1e-5
NEG_INF = -1e30
LR, B1, B2, EPS, WD, STEP = 0.001, 0.9, 0.999, 1e-08, 0.01, 10
R_IN, R_OUT, R_PA, R_PP, R_PL = 0, 2048, 2304, 2432, 2560
R_ALL = 2576
R_SMALL = R_ALL - R_OUT
VMEM_LIMIT = 56 * 1024 * 1024
MESH = pl.DeviceIdType.MESH
ANY = pl.BlockSpec(memory_space=pl.ANY)


def _cparams(n_axes):
    return pltpu.CompilerParams(dimension_semantics=("arbitrary",) * n_axes, vmem_limit_bytes=VMEM_LIMIT)


def _sigmoid(z):
    return 1.0 / (1.0 + jnp.exp(-z))


def _nt(a, b):
    return lax.dot_general(a, b, (((1,), (1,)), ((), ())), preferred_element_type=F32)


def _tn(a, b):
    return lax.dot_general(a, b, (((0,), (0,)), ((), ())), preferred_element_type=F32)


def _nn(a, b):
    return jnp.dot(a, b, preferred_element_type=F32)


def _lin(x, y, c):
    return 4 * x + 2 * y + c


def _flip(v, f):
    return 1 - v if f else v


def _exchange(name, src, plan, *, dst_shape=None, local_dst=None):
    n = len(plan)
    in_place = dst_shape is None
    out_sds = jax.ShapeDtypeStruct(src.shape, src.dtype) if in_place else dst_shape

    def body(src_ref, dst_ref, send_sems, recv_sems, local_sem):
        x, y, c = lax.axis_index("x"), lax.axis_index("y"), lax.axis_index("c")

        def copy(k, sender):
            flip, src_index, dst_index = plan[k]
            sx, sy, sc = sender
            to = (_flip(sx, flip[0]), _flip(sy, flip[1]), _flip(sc, flip[2]))
            s = src_ref if src_index is None else src_ref.at[src_index(sx, sy, sc)]
            return pltpu.make_async_remote_copy(
                src_ref=s, dst_ref=dst_ref.at[dst_index(sx, sy, sc)],
                send_sem=send_sems.at[k], recv_sem=recv_sems.at[k],
                device_id=to, device_id_type=MESH)

        me = (x, y, c)
        if local_dst is not None:
            mine = pltpu.make_async_copy(src_ref, dst_ref.at[local_dst(x, y, c)], local_sem)
            mine.start()
        sends = [copy(k, me) for k in range(n)]
        for cp in sends:
            cp.start()
        for k in range(n):
            flip = plan[k][0]
            copy(k, (_flip(x, flip[0]), _flip(y, flip[1]), _flip(c, flip[2]))).wait_recv()
        for cp in sends:
            cp.wait_send()
        if local_dst is not None:
            mine.wait()

    return pl.pallas_call(
        body, name=name, out_shape=out_sds, in_specs=[ANY], out_specs=ANY,
        input_output_aliases={0: 0} if in_place else {},
        scratch_shapes=[pltpu.SemaphoreType.DMA((n,)), pltpu.SemaphoreType.DMA((n,)),
                        pltpu.SemaphoreType.DMA(())],
    )(src)


FLIP_C, FLIP_X, FLIP_Y, FLIP_XY = (0, 0, 1), (1, 0, 0), (0, 1, 0), (1, 1, 0)
CHIP_FLIPS = ((0, 0), (1, 0), (0, 1), (1, 1))


AG_PIECES = ((pl.ds(R_IN, D), pl.ds(0, 1024)), (pl.ds(R_IN, D), pl.ds(1024, 1024)),
             (pl.ds(R_OUT, R_PA - R_OUT), pl.ds(0, D)), (pl.ds(R_PA, R_ALL - R_PA), pl.ds(0, D)))
N_PIECES = len(AG_PIECES)
SIB, TO_X, TO_Y, ON, PASS_X, PASS_Y, PASS_D = range(7)
AG_TILES = ((0, 0), (0, 1), (1, 0), (1, 1), (2, 0), (4, 0), (3, 0), (5, 0),
            (2, 1), (4, 1), (3, 1), (5, 1), (6, 0), (6, 1), (7, 0), (7, 1))
W, G = "wait", "go"
AG_STEPS = {
    2: [(W, SIB, 0)], 3: [(W, SIB, 1)],
    4: [(W, TO_X, 0), (G, ON, 0), (G, PASS_X, 0)], 5: [(W, TO_Y, 0), (G, PASS_Y, 0)],
    6: [(W, PASS_X, 0)], 7: [(W, PASS_Y, 0)],
    8: [(W, TO_X, 1), (G, PASS_X, 1)],
    9: [(W, TO_Y, 1), (G, ON, 1), (G, PASS_Y, 1), (G, TO_X, 2), (G, TO_X, 3), (G, TO_Y, 2), (G, TO_Y, 3)],
    10: [(W, PASS_X, 1)], 11: [(W, PASS_Y, 1)],
    12: [(W, ON, 0), (G, PASS_D, 0)], 13: [(W, ON, 1), (G, PASS_D, 1)],
    14: [(W, PASS_D, 0), (W, TO_X, 2), (G, ON, 2), (G, PASS_X, 2), (W, TO_X, 3), (G, PASS_X, 3),
         (W, TO_Y, 2), (G, PASS_Y, 2), (W, TO_Y, 3), (G, ON, 3), (G, PASS_Y, 3)],
    15: [(W, PASS_D, 1)],
}
AG_LAST = [(W, SIB, 2), (W, SIB, 3), (W, ON, 2), (G, PASS_D, 2), (W, ON, 3), (G, PASS_D, 3),
           (W, PASS_X, 2), (W, PASS_X, 3), (W, PASS_Y, 2), (W, PASS_Y, 3), (W, PASS_D, 2), (W, PASS_D, 3)]


def _arrival_order(x, y, c):
    chips = [(x, y), (1 - x, y), (x, 1 - y), (1 - x, 1 - y)]
    return jnp.stack([_lin(px, py, pc) for px, py in chips for pc in (c, 1 - c)]).astype(jnp.int32)


def _ag_proj(order, xb, pack):
    tm, tn = 1024, 1024
    nrow, ntile = S // tm, len(AG_TILES)
    slabs = jnp.stack([order[pos] for pos, _ in AG_TILES])
    cols = jnp.stack([2 * order[pos] + half for pos, half in AG_TILES])

    def body(cols_ref, slabs_ref, x_ref, pack_ref, h_ref, gw_ref, wbuf, wsem, send_sems, recv_sems, local_sem):
        t, i = pl.program_id(0), pl.program_id(1)
        x, y, c = lax.axis_index("x"), lax.axis_index("y"), lax.axis_index("c")
        me = _lin(x, y, c)
        dev = {"sib": (x, y, 1 - c), "x": (1 - x, y, c), "y": (x, 1 - y, c), "d": (1 - x, 1 - y, c)}

        def slab_of(name, other_core=False):
            px, py, pc = dev[name]
            return _lin(px, py, 1 - pc if other_core else pc)

        def rdma(slab, kind, piece, to, from_pack=False):
            k = kind * N_PIECES + piece
            there = gw_ref.at[(slab, *AG_PIECES[piece])]
            return pltpu.make_async_remote_copy(
                src_ref=pack_ref.at[AG_PIECES[piece]] if from_pack else there, dst_ref=there,
                send_sem=send_sems.at[k], recv_sem=recv_sems.at[k], device_id=dev[to], device_id_type=MESH)

        def mine(kind, piece):
            if kind in (SIB, TO_X, TO_Y):
                return rdma(me, kind, piece, ("sib", "x", "y")[kind], from_pack=True)
            if kind == ON:
                frm, to = ("x", "y") if piece % 2 == 0 else ("y", "x")
                return rdma(slab_of(frm), kind, piece, to)
            return rdma(slab_of({PASS_X: "x", PASS_Y: "y", PASS_D: "d"}[kind]), kind, piece, "sib")

        def landing(kind, piece):
            slab = {SIB: slab_of("sib"), TO_X: slab_of("x"), TO_Y: slab_of("y"), ON: slab_of("d"),
                    PASS_X: slab_of("x", True), PASS_Y: slab_of("y", True), PASS_D: slab_of("d", True)}[kind]
            return rdma(slab, kind, piece, "sib")

        def run(steps):
            for what, kind, piece in steps:
                if what == W:
                    landing(kind, piece).wait_recv()
                else:
                    mine(kind, piece).start()

        local = pltpu.make_async_copy(pack_ref, gw_ref.at[me], local_sem)

        def fetch(slab, half, slot):
            src = pack_ref.at[AG_PIECES[half]] if slab is None else gw_ref.at[(slab, *AG_PIECES[half])]
            return pltpu.make_async_copy(src, wbuf.at[slot], wsem.at[slot])

        @pl.when((t == 0) & (i == 0))
        def _():
            local.start()
            run([(G, kind, piece) for piece in (0, 1) for kind in (TO_X, TO_Y, SIB)] + [(G, SIB, 2), (G, SIB, 3)])
            first = fetch(None, 0, 0)
            first.start()
            first.wait()

        for nxt in range(1, ntile):
            @pl.when((t == nxt - 1) & (i == nrow - 1))
            def _(nxt=nxt):
                run(AG_STEPS.get(nxt, []))
                fetch(None if AG_TILES[nxt][0] == 0 else slabs_ref[nxt], AG_TILES[nxt][1], nxt % 2).start()

        for slot in (0, 1):
            @pl.when(t % 2 == slot)
            def _(slot=slot):
                @pl.when((i == 0) & (t > 0))
                def _():
                    fetch(None, 0, slot).wait()
                h_ref[...] = _nn(x_ref[...], wbuf[slot]).astype(h_ref.dtype)

        @pl.when((t == ntile - 1) & (i == nrow - 1))
        def _():
            run(AG_LAST)
            for kind in range(7):
                for piece in range(N_PIECES):
                    mine(kind, piece).wait_send()
            local.wait()

    n_sem = 7 * N_PIECES
    grid_spec = pltpu.PrefetchScalarGridSpec(
        num_scalar_prefetch=2, grid=(ntile, nrow),
        in_specs=[pl.BlockSpec((tm, D), lambda t, i, cols, slabs: (i, 0)), ANY],
        out_specs=[pl.BlockSpec((tm, tn), lambda t, i, cols, slabs: (i, cols[t])), ANY],
        scratch_shapes=[pltpu.VMEM((2, D, tn), BF16), pltpu.SemaphoreType.DMA((2,)),
                        pltpu.SemaphoreType.DMA((n_sem,)), pltpu.SemaphoreType.DMA((n_sem,)),
                        pltpu.SemaphoreType.DMA(())])
    return pl.pallas_call(
        body, name="ag_proj", grid_spec=grid_spec,
        out_shape=[jax.ShapeDtypeStruct((S, NW), BF16), jax.ShapeDtypeStruct((N_DEV, R_ALL, D), BF16)],
        compiler_params=_cparams(2))(cols, slabs, xb, pack)


def _all_gather_direct(name, vec):
    own = lambda x, y, c: _lin(x, y, c)
    flips = [(fx, fy, fc) for fx in (0, 1) for fy in (0, 1) for fc in (0, 1) if (fx, fy, fc) != (0, 0, 0)]
    return _exchange(name, vec, [(f, None, own) for f in flips],
                     dst_shape=jax.ShapeDtypeStruct((N_DEV,) + vec.shape, vec.dtype), local_dst=own)


def _rs_sibling(p):
    plan = [(FLIP_C, (lambda x, y, c, f=f: _lin(_flip(x, f[0]), _flip(y, f[1]), 1 - c)),
             (lambda x, y, c, k=k: k)) for k, f in enumerate(CHIP_FLIPS)]
    return _exchange("rs_sibling", p, plan, dst_shape=jax.ShapeDtypeStruct((4,) + p.shape[1:], p.dtype))


def _pair_sum_small(coords, p, l1, q):
    def body(crd, p_ref, l_ref, _, q_ref, buf, sem):
        k = pl.program_id(0)
        buf[...] = (p_ref[...].astype(F32) + l_ref[...].astype(F32)).astype(buf.dtype)
        out = pltpu.make_async_copy(buf, q_ref.at[k, pl.ds(R_OUT, R_SMALL)], sem)
        out.start()
        out.wait()

    def p_map(k, crd):
        fx, fy = k % 2, k // 2
        px = crd[0] + fx - 2 * fx * crd[0]
        py = crd[1] + fy - 2 * fy * crd[1]
        return (_lin(px, py, crd[2]), 0, 0)

    grid_spec = pltpu.PrefetchScalarGridSpec(
        num_scalar_prefetch=1, grid=(4,),
        in_specs=[pl.BlockSpec((None, R_SMALL, D), p_map),
                  pl.BlockSpec((None, R_SMALL, D), lambda k, crd: (k, 0, 0)), ANY],
        out_specs=ANY,
        scratch_shapes=[pltpu.VMEM((R_SMALL, D), BF16), pltpu.SemaphoreType.DMA(())])
    return pl.pallas_call(body, name="pair_sum_small", grid_spec=grid_spec,
                          out_shape=jax.ShapeDtypeStruct(q.shape, q.dtype), input_output_aliases={3: 0},
                          compiler_params=_cparams(1))(coords, p, l1, q)


def _h_block(k):
    return jnp.where(k < 9, (k % 3) * 3 + k // 3, k)


RS_PIECES = (pl.ds(0, 1280), pl.ds(1280, R_ALL - 1280))
RS_ROWS = (1280, R_ALL - 1280)
RS_CHUNKS = ((320,) * 4, (432,) * 3)
RS_MERGE_STEP = 3


def _grad_x_rs(dh, g, dr, q):
    tm, tk = 512, 1024
    ni, nk = S // tm, NW // tk
    rmax = max(RS_ROWS)
    cmax = max(max(c) for c in RS_CHUNKS)

    def body(dh_ref, w_ref, dr_ref, q_ref, o_ref, l2_ref, ld_ref, mg_ref, va, vb, send_sems, recv_sems, sems):
        i, k = pl.program_id(0), pl.program_id(1)
        x, y, c = lax.axis_index("x"), lax.axis_index("y"), lax.axis_index("c")
        nbr = ((1 - x, y, c), (x, 1 - y, c))

        def rows(ref, piece):
            return ref.at[piece, pl.ds(0, RS_ROWS[piece])]

        copies = (
            (q_ref.at[3, RS_PIECES[0]], rows(ld_ref, 0), 0),
            (q_ref.at[3, RS_PIECES[1]], rows(ld_ref, 1), 1),
            (q_ref.at[1, RS_PIECES[0]], l2_ref.at[0, RS_PIECES[0]], 0),
            (q_ref.at[2, RS_PIECES[1]], l2_ref.at[1, RS_PIECES[1]], 1),
            (rows(mg_ref, 0), l2_ref.at[1, RS_PIECES[0]], 1),
            (rows(mg_ref, 1), l2_ref.at[0, RS_PIECES[1]], 0),
        )

        def copy(n):
            src, dst, axis = copies[n]
            return pltpu.make_async_remote_copy(src_ref=src, dst_ref=dst, send_sem=send_sems.at[n],
                                                recv_sem=recv_sems.at[n], device_id=nbr[axis], device_id_type=MESH)

        def merge(piece, mine):
            start = 0
            for n_rows in RS_CHUNKS[piece]:
                own = pltpu.make_async_copy(q_ref.at[mine, pl.ds(RS_PIECES[piece].start + start, n_rows)],
                                            va.at[pl.ds(0, n_rows)], sems.at[0])
                got = pltpu.make_async_copy(ld_ref.at[piece, pl.ds(start, n_rows)], vb.at[pl.ds(0, n_rows)], sems.at[1])
                own.start()
                got.start()
                own.wait()
                got.wait()
                va[pl.ds(0, n_rows)] = (va[pl.ds(0, n_rows)].astype(F32)
                                        + vb[pl.ds(0, n_rows)].astype(F32)).astype(va.dtype)
                out = pltpu.make_async_copy(va.at[pl.ds(0, n_rows)], mg_ref.at[piece, pl.ds(start, n_rows)], sems.at[2])
                out.start()
                out.wait()
                start += n_rows

        @pl.when((i == 0) & (k == 0))
        def _():
            for n in range(4):
                copy(n).start()

        @pl.when((i == RS_MERGE_STEP) & (k == 0))
        def _():
            copy(0).wait_recv()
            merge(0, 2)
            copy(4).start()
            copy(1).wait_recv()
            merge(1, 1)
            copy(5).start()

        @pl.when(k == 0)
        def _():
            o_ref[...] = ALPHA * dr_ref[...]

        o_ref[...] += _nt(dh_ref[...], w_ref[...])

        @pl.when((i == ni - 1) & (k == nk - 1))
        def _():
            for n in range(2, 6):
                copy(n).wait_recv()
            for n in range(6):
                copy(n).wait_send()

    slab = q.shape[1:]
    out = pl.pallas_call(
        body, name="grad_x_rs", grid=(ni, nk),
        in_specs=[pl.BlockSpec((tm, tk), lambda i, k: (i, k)),
                  pl.BlockSpec((None, D, tk), lambda i, k: (_h_block(k) // 2, 0, _h_block(k) % 2)),
                  pl.BlockSpec((tm, D), lambda i, k: (i, 0)), ANY],
        out_specs=[pl.BlockSpec((tm, D), lambda i, k: (i, 0)), ANY, ANY, ANY],
        out_shape=[jax.ShapeDtypeStruct((S, D), F32), jax.ShapeDtypeStruct((2,) + slab, q.dtype),
                   jax.ShapeDtypeStruct((2, rmax, slab[1]), q.dtype), jax.ShapeDtypeStruct((2, rmax, slab[1]), q.dtype)],
        scratch_shapes=[pltpu.VMEM((cmax, slab[1]), q.dtype), pltpu.VMEM((cmax, slab[1]), q.dtype),
                        pltpu.SemaphoreType.DMA((6,)), pltpu.SemaphoreType.DMA((6,)), pltpu.SemaphoreType.DMA((3,))],
        compiler_params=_cparams(2))(dh, g, dr, q)
    return out[0], out[1]


def _rs_columns(x, y, c):
    out = []
    for core in (1 - c, c):
        for fx, fy in CHIP_FLIPS:
            for half in (0, 1):
                out.append(_h_block(2 * _lin(_flip(x, fx), _flip(y, fy), core) + half))
    return jnp.stack(out).astype(jnp.int32)


def _grad_w_in_rs(cols, xt, dh):
    tn, tk = 1024, 1024
    nk = S // tk
    n_half = 8

    def body(cols_ref, a_ref, b_ref, q_ref, l1_ref, acc_ref, stage, landed, send_sems, recv_sems, sem):
        t, k = pl.program_id(0), pl.program_id(1)
        sib = (lax.axis_index("x"), lax.axis_index("y"), 1 - lax.axis_index("c"))

        @pl.when(k == 0)
        def _():
            acc_ref[...] = jnp.zeros_like(acc_ref)

        acc_ref[...] += _nn(a_ref[...], b_ref[...])

        def there(n):
            return l1_ref.at[n // 2, :, pl.ds((n % 2) * tn, tn)]

        def send(n):
            return pltpu.make_async_remote_copy(src_ref=stage.at[n % 2], dst_ref=there(n), send_sem=send_sems.at[n],
                                                recv_sem=recv_sems.at[n], device_id=sib, device_id_type=MESH)

        for n in range(n_half):
            @pl.when((t == n) & (k == nk - 1))
            def _(n=n):
                if n >= 2:
                    send(n - 2).wait_send()
                stage[n % 2] = acc_ref[...].astype(stage.dtype)
                send(n).start()

        for n in range(n_half):
            @pl.when((t == n_half + n) & (k == nk - 1))
            def _(n=n):
                if n == 0:
                    send(n_half - 2).wait_send()
                    send(n_half - 1).wait_send()
                send(n).wait_recv()
                fetch = pltpu.make_async_copy(there(n), landed, sem)
                fetch.start()
                fetch.wait()
                q_ref[...] = (acc_ref[...] + landed[...].astype(F32)).astype(q_ref.dtype)

    mine = lambda t: jnp.maximum(t - n_half, 0)
    grid_spec = pltpu.PrefetchScalarGridSpec(
        num_scalar_prefetch=1, grid=(2 * n_half, nk),
        in_specs=[pl.BlockSpec((D, tk), lambda t, k, cols: (0, k)),
                  pl.BlockSpec((tk, tn), lambda t, k, cols: (k, cols[t]))],
        out_specs=[pl.BlockSpec((None, D, tn), lambda t, k, cols: (mine(t) // 2, 0, mine(t) % 2)), ANY],
        scratch_shapes=[pltpu.VMEM((D, tn), F32), pltpu.VMEM((2, D, tn), BF16), pltpu.VMEM((D, tn), BF16),
                        pltpu.SemaphoreType.DMA((n_half,)), pltpu.SemaphoreType.DMA((n_half,)),
                        pltpu.SemaphoreType.DMA(())])
    q, _ = pl.pallas_call(
        body, name="grad_w_in_rs", grid_spec=grid_spec,
        out_shape=[jax.ShapeDtypeStruct((4, R_ALL, D), BF16), jax.ShapeDtypeStruct((4, D, D), BF16)],
        compiler_params=_cparams(2))(cols, xt, dh)
    return q


def _grad_w(name, at, b, out_shape, out_spec, tn, tk):
    m, k_all = at.shape
    n_all = b.shape[1]
    nk = k_all // tk

    def body(a_ref, b_ref, o_ref, acc_ref):
        k = pl.program_id(1)

        @pl.when(k == 0)
        def _():
            acc_ref[...] = jnp.zeros_like(acc_ref)

        acc_ref[...] += _nn(a_ref[...], b_ref[...])

        @pl.when(k == nk - 1)
        def _():
            o_ref[...] = acc_ref[...].astype(o_ref.dtype)

    return pl.pallas_call(
        body, name=name, grid=(n_all // tn, nk),
        in_specs=[pl.BlockSpec((m, tk), lambda n, k: (0, k)),
                  pl.BlockSpec((tk, tn), lambda n, k: (k, n))],
        out_specs=out_spec, out_shape=out_shape,
        scratch_shapes=[pltpu.VMEM((m, tn), F32)], compiler_params=_cparams(2))(at, b)


NR = 16
TI = 16
TM = NR * TI
NT = S // TM
ATT_QB = (256, 128, 128)
ATT_NB = (16, 8, 2)
ATT_BLOCKS = (16, 32, 32)


def _permute_tokens(a):
    return a.reshape(NT, TI, NR, a.shape[-1]).transpose(0, 2, 1, 3).reshape(a.shape)


def _attn_shape(g, c):
    if g == 0:
        return (S, c)
    if g == 1:
        return (NT, 4, 4, TI, c)
    return (NT, NR, TI, c)


def _attn_view(g, a):
    return a.reshape(_attn_shape(g, a.shape[-1]))


def _attn_spec(g, width, col, blk):
    if g == 0:
        return pl.BlockSpec((TM, width), lambda b: (blk(b), col))
    if g == 1:
        return pl.BlockSpec((2, 4, None, TI, width), lambda b: (blk(b) % 8, 0, blk(b) // 8, 0, col))
    return pl.BlockSpec((8, None, TI, width), lambda b: (blk(b) % 2, blk(b) // 2, 0, col))


def _pieces(g):
    if g == 1:
        return [(t, m) for t in range(2) for m in range(4)]
    return [(t,) for t in range(8)]


def _get(g, ref, sl):
    if g == 0:
        return ref[:, sl]
    return jnp.concatenate([ref[(*p, slice(None), sl)] for p in _pieces(g)], axis=0)


def _put(g, ref, sl, val):
    if g == 0:
        ref[:, sl] = val
    else:
        for n, p in enumerate(_pieces(g)):
            ref[(*p, slice(None), sl)] = val[TI * n:TI * (n + 1)]


def _block_pos(g, a):
    if g == 0:
        return 16 * (a % 16) + a // 16
    if g == 1:
        return 64 * (a // 64) + 4 * (a % 16) + (a // 16) % 4
    return a


def _attn_mask(g, n):
    qb = ATT_QB[g]
    qa = lax.broadcasted_iota(jnp.int32, (qb, 2 * qb), 0)
    kc = lax.broadcasted_iota(jnp.int32, (qb, 2 * qb), 1)
    cur = kc >= qb
    dist = _block_pos(g, qa) - _block_pos(g, kc % qb) + jnp.where(cur, 0, qb)
    return (dist >= 0) & (dist <= QB) & (cur | (n > 0))


def _qkv_specs(g, clamp):
    cur = lambda col: _attn_spec(g, AW, col, clamp)
    prev = lambda col: _attn_spec(g, AW, col, lambda b: jnp.maximum(clamp(b) - 1, 0))
    qc, kc, vc = (c // AW + g for c in (COL_Q, COL_K, COL_V))
    return [cur(qc), cur(kc), prev(kc), cur(vc), prev(vc)]


def _attn_fwd(g, h):
    scale = HD ** -0.5
    hv = _attn_view(g, h)

    def body(q_ref, kc_ref, kp_ref, vc_ref, vp_ref, o_ref, l_ref):
        valid = _attn_mask(g, pl.program_id(0) % ATT_NB[g])
        for hh in range(NH):
            sl = slice(hh * HD, (hh + 1) * HD)
            kh = jnp.concatenate([_get(g, kp_ref, sl), _get(g, kc_ref, sl)], axis=0)
            vh = jnp.concatenate([_get(g, vp_ref, sl), _get(g, vc_ref, sl)], axis=0)
            s = jnp.where(valid, _nt(_get(g, q_ref, sl), kh) * scale, NEG_INF)
            m = jnp.max(s, axis=-1, keepdims=True)
            e = jnp.exp(s - m)
            den = jnp.sum(e, axis=-1, keepdims=True)
            _put(g, o_ref, sl, (_nn(e.astype(BF16), vh) / den).astype(o_ref.dtype))
            _put(g, l_ref, slice(hh, hh + 1), m + jnp.log(den))

    same = lambda b: b
    o, lse = pl.pallas_call(
        body, name=f"attn_fwd_{g}", grid=(ATT_BLOCKS[g],),
        in_specs=_qkv_specs(g, same),
        out_specs=[_attn_spec(g, AW, 0, same), _attn_spec(g, NH, 0, same)],
        out_shape=[jax.ShapeDtypeStruct(_attn_shape(g, AW), BF16), jax.ShapeDtypeStruct(_attn_shape(g, NH), F32)],
        compiler_params=_cparams(1))(hv, hv, hv, hv, hv)
    return o.reshape(S, AW), lse.reshape(S, NH)


def _attn_bwd(g, dh, h, do, lse, delta):
    scale = HD ** -0.5
    qb = ATT_QB[g]
    last = ATT_BLOCKS[g] - 1
    clamp = lambda b: jnp.minimum(b, last)
    behind = lambda b: jnp.maximum(b - 1, 0)
    hv = _attn_view(g, h)

    def body(q_ref, kc_ref, kp_ref, vc_ref, vp_ref, do_ref, l_ref, dl_ref, _, dh_ref, cq_ref, ck_ref, cv_ref):
        b = pl.program_id(0)

        def write(col, val):
            _put(g, dh_ref, slice(col, col + HD), val.astype(dh_ref.dtype))

        @pl.when(b == 0)
        def _():
            cq_ref[...] = jnp.zeros_like(cq_ref)
            ck_ref[...] = jnp.zeros_like(ck_ref)
            cv_ref[...] = jnp.zeros_like(cv_ref)

        @pl.when(b <= last)
        def _():
            valid = _attn_mask(g, b % ATT_NB[g])
            for hh in range(NH):
                sl = slice(hh * HD, (hh + 1) * HD)
                one = slice(hh, hh + 1)
                qh, doh = _get(g, q_ref, sl), _get(g, do_ref, sl)
                kh = jnp.concatenate([_get(g, kp_ref, sl), _get(g, kc_ref, sl)], axis=0)
                vh = jnp.concatenate([_get(g, vp_ref, sl), _get(g, vc_ref, sl)], axis=0)
                s = _nt(qh, kh) * scale
                p = jnp.where(valid, jnp.exp(s - _get(g, l_ref, one)), 0.0)
                ds = p * (_nt(doh, vh) - _get(g, dl_ref, one))
                dsb = (ds * scale).astype(BF16)
                dk2 = _tn(dsb, qh)
                dv2 = _tn(p.astype(BF16), doh)
                write(hh * HD, cq_ref[:, sl])
                write(AW + hh * HD, ck_ref[:, sl] + dk2[:qb])
                write(2 * AW + hh * HD, cv_ref[:, sl] + dv2[:qb])
                cq_ref[:, sl] = _nn(dsb, kh)
                ck_ref[:, sl] = dk2[qb:]
                cv_ref[:, sl] = dv2[qb:]

        @pl.when(b > last)
        def _():
            for hh in range(NH):
                sl = slice(hh * HD, (hh + 1) * HD)
                write(hh * HD, cq_ref[:, sl])
                write(AW + hh * HD, ck_ref[:, sl])
                write(2 * AW + hh * HD, cv_ref[:, sl])

    out = pl.pallas_call(
        body, name=f"attn_bwd_{g}", grid=(ATT_BLOCKS[g] + 1,),
        in_specs=_qkv_specs(g, clamp) + [_attn_spec(g, AW, 0, clamp), _attn_spec(g, NH, 0, clamp),
                                         _attn_spec(g, NH, 0, clamp), ANY],
        out_specs=_attn_spec(g, 3 * AW, g, behind),
        out_shape=jax.ShapeDtypeStruct(_attn_shape(g, NW), BF16),
        input_output_aliases={8: 0},
        scratch_shapes=[pltpu.VMEM((qb, AW), F32)] * 3,
        compiler_params=_cparams(1))(hv, hv, hv, hv, hv, _attn_view(g, do), _attn_view(g, lse), _attn_view(g, delta),
                                     _attn_view(g, dh))
    return out.reshape(S, NW)


def _group_weights(l0, l1, l2):
    m = jnp.maximum(jnp.maximum(l0, l1), l2)
    e0, e1, e2 = jnp.exp(l0 - m), jnp.exp(l1 - m), jnp.exp(l2 - m)
    inv = 1.0 / (e0 + e1 + e2)
    return e0 * inv, e1 * inv, e2 * inv


def _residue(ref, r, sl):
    return ref[r * TI:(r + 1) * TI, sl].astype(F32)


def _total(parts):
    return functools.reduce(lambda x, y: x + y, parts)


def _pool_tokens(up_ref, uc_ref, p_ref, tile):
    j0 = lax.broadcasted_iota(jnp.int32, (TI, 1), 0) == 0
    first = (tile == 0) & j0
    for r in range(NR):
        out = []
        for g, w in enumerate(POOL_WINDOWS):
            sl = slice(g * PG, (g + 1) * PG)
            own = _residue(uc_ref, r, sl)
            acc = _total([own] + [_residue(uc_ref, r - k, sl) for k in range(1, min(r, w - 1) + 1)])
            wrapped = [NR + r - k for k in range(r + 1, w)]
            if wrapped:
                wc = _total([_residue(uc_ref, q, sl) for q in wrapped])
                wp = jnp.where(tile > 0, _total([_residue(up_ref, q, sl) for q in wrapped]), 0.0)
                acc = acc + jnp.where(j0, pltpu.roll(wp, 1, 0), pltpu.roll(wc, 1, 0))
            out.append(acc / jnp.where(first, float(min(r + 1, w)), float(w)) - own)
        p_ref[r * TI:(r + 1) * TI, :] = jnp.concatenate(out, axis=1).astype(p_ref.dtype)


def _pool_tokens_bwd(dp, nxt_ref, du_ref, tile):
    ji = lax.broadcasted_iota(jnp.int32, (TI, 1), 0)
    first = (tile == 0) & (ji == 0)
    piece = lambda g, r: dp[g][r * TI:(r + 1) * TI]
    dpc = [[piece(g, r) / jnp.where(first, float(min(r + 1, w)), float(w)) for r in range(NR)]
           for g, w in enumerate(POOL_WINDOWS)]
    for r in range(NR):
        out = []
        for g, w in enumerate(POOL_WINDOWS):
            sl = slice(g * PG, (g + 1) * PG)
            acc = _total([dpc[g][r + k] for k in range(w) if r + k < NR])
            wrapped = [r + k - NR for k in range(1, w) if r + k >= NR]
            if wrapped:
                wc = _total([dpc[g][q] for q in wrapped])
                wn = _total([nxt_ref[q * TI:(q + 1) * TI, sl] for q in wrapped])
                acc = acc + jnp.where(ji == TI - 1, pltpu.roll(wn, TI - 1, 0), pltpu.roll(wc, TI - 1, 0))
            out.append(acc - piece(g, r))
        du_ref[r * TI:(r + 1) * TI, :] = jnp.concatenate(out, axis=1).astype(du_ref.dtype)
    for r in range(NR):
        nxt_ref[r * TI:(r + 1) * TI, :] = jnp.concatenate([dpc[g][r] for g in range(len(POOL_WINDOWS))], axis=1)


def _pool_linear(pb, wpool_ref):
    return jnp.concatenate([_nn(pb[:, g * PG:(g + 1) * PG], wpool_ref[g]) for g in range(len(POOL_WINDOWS))], axis=1)


def _tok(width, col=0, rev=False):
    if rev:
        return pl.BlockSpec((TM, width), lambda i: (NT - 1 - i, col))
    return pl.BlockSpec((TM, width), lambda i: (i, col))


def _whole(shape):
    return pl.BlockSpec(shape, lambda i: (0,) * len(shape))


def _mix_fwd(h, o, lse, wpa, wpp, wpool, pscale, bgate):
    def body(o0_ref, o1_ref, o2_ref, l0_ref, l1_ref, l2_ref, za_ref, uc_ref, up_ref, zp_ref, gp_ref,
             wpa_ref, wpp_ref, wpool_ref, ps_ref, bg_ref, ya_ref, yp_ref, mg_ref, a_ref, b_ref, p_ref):
        i = pl.program_id(0)
        w0, w1, w2 = _group_weights(l0_ref[...], l1_ref[...], l2_ref[...])
        za = za_ref[...].astype(F32)
        silu_a = za * _sigmoid(za)
        for hh in range(NH):
            sl = slice(hh * HD, (hh + 1) * HD)
            c = slice(hh, hh + 1)
            oh = (w0[:, c] * o0_ref[:, sl].astype(F32) + w1[:, c] * o1_ref[:, sl].astype(F32)
                  + w2[:, c] * o2_ref[:, sl].astype(F32))
            ya_ref[:, sl] = (oh * silu_a[:, sl]).astype(BF16)
        _pool_tokens(up_ref, uc_ref, p_ref, i)
        zp = zp_ref[...].astype(F32)
        yp_ref[...] = (_pool_linear(p_ref[...], wpool_ref) * ps_ref[...] * (zp * _sigmoid(zp))).astype(BF16)
        a = _nn(ya_ref[...], wpa_ref[...])
        b = _nn(yp_ref[...], wpp_ref[...])
        a_ref[...] = a.astype(BF16)
        b_ref[...] = b.astype(BF16)
        gates = _sigmoid(gp_ref[...].astype(F32) + bg_ref[...])
        mg_ref[...] = (gates[:, :D] * a + gates[:, D:] * b).astype(BF16)

    u_prev = pl.BlockSpec((TM, AW), lambda i: (jnp.maximum(i - 1, 0), COL_U // AW))
    return pl.pallas_call(
        body, name="mix_fwd", grid=(NT,),
        in_specs=[_tok(AW)] * 3 + [_tok(NH)] * 3
        + [_tok(AW, COL_ZA // AW), _tok(AW, COL_U // AW), u_prev, _tok(AW, COL_ZP // AW), _tok(2 * D, COL_G // (2 * D))]
        + [_whole((AW, D)), _whole((AW, D)), _whole((4, PG, PG)), _whole((1, AW)), _whole((1, 2 * D))],
        out_specs=[_tok(AW), _tok(AW), _tok(D), _tok(D), _tok(D), _tok(AW)],
        out_shape=[jax.ShapeDtypeStruct((S, AW), BF16)] * 2 + [jax.ShapeDtypeStruct((S, D), BF16)] * 3
        + [jax.ShapeDtypeStruct((S, AW), BF16)],
        compiler_params=_cparams(1))(*o, *lse, h, h, h, h, h, wpa, wpp, wpool, pscale, bgate)


def _out_ln(merged, x, target, wout, gamma, beta):
    def body(mg_ref, x_ref, t_ref, w_ref, g_ref, b_ref, dr_ref, drb_ref, dm_ref, loss_ref, dg_ref, db_ref):
        i = pl.program_id(0)

        @pl.when(i == 0)
        def _():
            loss_ref[...] = jnp.zeros_like(loss_ref)
            dg_ref[...] = jnp.zeros_like(dg_ref)
            db_ref[...] = jnp.zeros_like(db_ref)

        r = ALPHA * x_ref[...] + _nn(mg_ref[...], w_ref[...])
        mu = jnp.mean(r, axis=-1, keepdims=True)
        rc = r - mu
        rstd = lax.rsqrt(jnp.mean(rc * rc, axis=-1, keepdims=True) + LN_EPS)
        xhat = rc * rstd
        err = xhat * g_ref[...] + b_ref[...] - t_ref[...]
        loss_ref[...] += 0.5 * jnp.sum(jnp.mean(err * err, axis=-1, keepdims=True), axis=0, keepdims=True)
        dy = err * (1.0 / D)
        dg_ref[...] += jnp.sum(dy * xhat, axis=0, keepdims=True)
        db_ref[...] += jnp.sum(dy, axis=0, keepdims=True)
        dxh = dy * g_ref[...]
        dr = rstd * (dxh - jnp.mean(dxh, axis=-1, keepdims=True)
                     - xhat * jnp.mean(dxh * xhat, axis=-1, keepdims=True))
        dr_ref[...] = dr
        drb_ref[...] = dr.astype(BF16)
        dm_ref[...] = _nt(drb_ref[...], w_ref[...]).astype(BF16)

    return pl.pallas_call(
        body, name="out_ln", grid=(NT,),
        in_specs=[_tok(D), _tok(D), _tok(D), _whole((D, D)), _whole((1, D)), _whole((1, D))],
        out_specs=[_tok(D), _tok(D), _tok(D), _whole((8, 128)), _whole((1, D)), _whole((1, D))],
        out_shape=[jax.ShapeDtypeStruct((S, D), F32), jax.ShapeDtypeStruct((S, D), BF16),
                   jax.ShapeDtypeStruct((S, D), BF16), jax.ShapeDtypeStruct((8, 128), F32),
                   jax.ShapeDtypeStruct((1, D), F32), jax.ShapeDtypeStruct((1, D), F32)],
        compiler_params=_cparams(1))(merged, x, target, wout, gamma, beta)


def _gate_bwd(dm, a, b, h, bgate):
    def body(dm_ref, a_ref, b_ref, gp_ref, bg_ref, dgp_ref, da_ref, db_ref, dbg_ref):
        @pl.when(pl.program_id(0) == 0)
        def _():
            dbg_ref[...] = jnp.zeros_like(dbg_ref)

        dm_ = dm_ref[...].astype(F32)
        gates = _sigmoid(gp_ref[...].astype(F32) + bg_ref[...])
        ga, gb = gates[:, :D], gates[:, D:]
        da_ref[...] = (dm_ * ga).astype(BF16)
        db_ref[...] = (dm_ * gb).astype(BF16)
        dgp = jnp.concatenate([dm_ * a_ref[...].astype(F32) * ga * (1.0 - ga),
                               dm_ * b_ref[...].astype(F32) * gb * (1.0 - gb)], axis=1)
        dgp_ref[...] = dgp.astype(BF16)
        dbg_ref[...] += jnp.sum(dgp, axis=0, keepdims=True)

    return pl.pallas_call(
        body, name="gate_bwd", grid=(NT,),
        in_specs=[_tok(D), _tok(D), _tok(D), _tok(2 * D, COL_G // (2 * D)), _whole((1, 2 * D))],
        out_specs=[_tok(2 * D, DH_G // (2 * D)), _tok(D), _tok(D), _whole((1, 2 * D))],
        out_shape=[jax.ShapeDtypeStruct((S, NW), BF16), jax.ShapeDtypeStruct((S, D), BF16),
                   jax.ShapeDtypeStruct((S, D), BF16), jax.ShapeDtypeStruct((1, 2 * D), F32)],
        compiler_params=_cparams(1))(dm, a, b, h, bgate)


def _mix_bwd(dh, da, db, h, o, lse, p, wpa, wpp, wpool, pscale):
    def body(_, da_ref, db_ref, o0_ref, o1_ref, o2_ref, l0_ref, l1_ref, l2_ref, za_ref, zp_ref, p_ref,
             wpa_ref, wpp_ref, wpool_ref, ps_ref,
             dh_ref, do0_ref, do1_ref, do2_ref, dl0_ref, dl1_ref, dl2_ref, dwp_ref, dps_ref,
             nxt_ref):
        i = pl.program_id(0)
        tile = NT - 1 - i
        dza_ref, du_ref, dzp_ref = (dh_ref.at[:, pl.ds(n * AW, AW)] for n in range(3))

        @pl.when(i == 0)
        def _():
            nxt_ref[...] = jnp.zeros_like(nxt_ref)
            dwp_ref[...] = jnp.zeros_like(dwp_ref)
            dps_ref[...] = jnp.zeros_like(dps_ref)

        dya = _nt(da_ref[...], wpa_ref[...])
        w0, w1, w2 = _group_weights(l0_ref[...], l1_ref[...], l2_ref[...])
        za = za_ref[...].astype(F32)
        sig = _sigmoid(za)
        silu_a = za * sig
        dsilu_a = sig * (1.0 + za * (1.0 - sig))
        for hh in range(NH):
            sl = slice(hh * HD, (hh + 1) * HD)
            c = slice(hh, hh + 1)
            oh = (w0[:, c] * o0_ref[:, sl].astype(F32) + w1[:, c] * o1_ref[:, sl].astype(F32)
                  + w2[:, c] * o2_ref[:, sl].astype(F32))
            doh = dya[:, sl] * silu_a[:, sl]
            dza_ref[:, sl] = (dya[:, sl] * oh * dsilu_a[:, sl]).astype(BF16)
            dot_ = jnp.sum(doh * oh, axis=-1, keepdims=True)
            do0_ref[:, sl] = (w0[:, c] * doh).astype(BF16)
            do1_ref[:, sl] = (w1[:, c] * doh).astype(BF16)
            do2_ref[:, sl] = (w2[:, c] * doh).astype(BF16)
            dl0_ref[:, c] = w0[:, c] * dot_
            dl1_ref[:, c] = w1[:, c] * dot_
            dl2_ref[:, c] = w2[:, c] * dot_
        dyp = _nt(db_ref[...], wpp_ref[...])
        pb = p_ref[...]
        pw = _pool_linear(pb, wpool_ref)
        zp = zp_ref[...].astype(F32)
        sigp = _sigmoid(zp)
        dypre = dyp * (zp * sigp)
        dzp_ref[...] = (dyp * (pw * ps_ref[...]) * (sigp * (1.0 + zp * (1.0 - sigp)))).astype(BF16)
        dps_ref[...] += jnp.sum(dypre * pw, axis=0, keepdims=True)
        dpw = (dypre * ps_ref[...]).astype(BF16)
        dp = []
        for g in range(len(POOL_WINDOWS)):
            sl = slice(g * PG, (g + 1) * PG)
            dwp_ref[g] += _tn(pb[:, sl], dpw[:, sl])
            dp.append(_nt(dpw[:, sl], wpool_ref[g]))
        _pool_tokens_bwd(dp, nxt_ref, du_ref, tile)

    r = functools.partial(_tok, rev=True)
    return pl.pallas_call(
        body, name="mix_bwd", grid=(NT,),
        in_specs=[ANY, r(D), r(D)] + [r(AW)] * 3 + [r(NH)] * 3 + [r(AW, COL_ZA // AW), r(AW, COL_ZP // AW), r(AW)]
        + [_whole((AW, D)), _whole((AW, D)), _whole((4, PG, PG)), _whole((1, AW))],
        out_specs=[r(3 * AW, DH_Z // (3 * AW))] + [r(AW)] * 3 + [r(NH)] * 3 + [_whole((4, PG, PG)), _whole((1, AW))],
        out_shape=[jax.ShapeDtypeStruct((S, NW), BF16)] + [jax.ShapeDtypeStruct((S, AW), BF16)] * 3
        + [jax.ShapeDtypeStruct((S, NH), F32)] * 3
        + [jax.ShapeDtypeStruct((4, PG, PG), F32), jax.ShapeDtypeStruct((1, AW), F32)],
        input_output_aliases={0: 0},
        scratch_shapes=[pltpu.VMEM((TM, AW), F32)],
        compiler_params=_cparams(1))(dh, da, db, *o, *lse, h, h, p, wpa, wpp, wpool, pscale)


def _adamw(w, g, m, v):
    m = B1 * m + (1.0 - B1) * g
    v = B2 * v + (1.0 - B2) * jnp.square(g)
    m_hat = m / (1.0 - B1 ** STEP)
    v_hat = v / (1.0 - B2 ** STEP)
    return -LR * (m_hat / (jnp.sqrt(v_hat) + EPS) + WD * w), m, v


def _adam_shard(name, q, l2, w, m, v, tr):
    rows = w.shape[0]

    def body(q_ref, l_ref, w_ref, m_ref, v_ref, g_out, d_out, m_out, v_out):
        g = q_ref[...].astype(F32)
        for k in range(2):
            g = g + l_ref[k].astype(F32)
        g_out[...] = g
        d_out[...], m_out[...], v_out[...] = _adamw(w_ref[...], g, m_ref[...], v_ref[...])

    blk = pl.BlockSpec((tr, D), lambda i: (i, 0))
    return pl.pallas_call(
        body, name=name, grid=(rows // tr,),
        in_specs=[pl.BlockSpec((None, tr, D), lambda i: (0, i, 0)), pl.BlockSpec((2, tr, D), lambda i: (0, i, 0)),
                  blk, blk, blk],
        out_specs=[blk] * 4, out_shape=[jax.ShapeDtypeStruct((rows, D), F32)] * 4,
        compiler_params=_cparams(1))(q, l2, w, m, v)


def _sum_small(q, l2):
    def body(q_ref, l_ref, g_out, buf, sems):
        rows = pl.ds(R_OUT, R_SMALL)
        copies = [pltpu.make_async_copy(src, buf.at[n], sems.at[n])
                  for n, src in enumerate((q_ref.at[0, rows], l_ref.at[0, rows], l_ref.at[1, rows]))]
        for cp in copies:
            cp.start()
        for cp in copies:
            cp.wait()
        g_out[...] = buf[0].astype(F32) + buf[1].astype(F32) + buf[2].astype(F32)

    return pl.pallas_call(
        body, name="sum_small", in_specs=[ANY, ANY], out_shape=jax.ShapeDtypeStruct((R_SMALL, D), F32),
        scratch_shapes=[pltpu.VMEM((3, R_SMALL, D), q.dtype), pltpu.SemaphoreType.DMA((3,))],
        compiler_params=pltpu.CompilerParams(vmem_limit_bytes=VMEM_LIMIT))(q, l2)


def _adam_whole(name, grads, weights, ms, vs):
    n = len(grads)

    def body(*refs):
        ins, outs = refs[:4 * n], refs[4 * n:]
        for t in range(n):
            g, w, m, v = (ins[k * n + t][...] for k in range(4))
            outs[t][...], outs[n + t][...], outs[2 * n + t][...] = _adamw(w, g, m, v)

    out = pl.pallas_call(
        body, name=name, out_shape=[jax.ShapeDtypeStruct(w.shape, F32) for w in weights] * 3,
        compiler_params=pltpu.CompilerParams(vmem_limit_bytes=VMEM_LIMIT))(*grads, *weights, *ms, *vs)
    return out[:n], out[n:2 * n], out[2 * n:]


def _sum_replicated(gathered):
    def body(g_ref, bg_out, ps_out, gm_out, bt_out, loss_out):
        g = g_ref[0]
        for k in range(1, N_DEV):
            g = g + g_ref[k]
        bg_out[...] = jnp.concatenate([g[0:1], g[1:2]], axis=1)
        gm_out[...] = g[2:3]
        bt_out[...] = g[3:4]
        ps_out[...] = g[4:5, :AW]
        loss_out[...] = jnp.broadcast_to(g[5:6, :128], loss_out.shape)

    return pl.pallas_call(
        body, name="sum_replicated",
        out_shape=[jax.ShapeDtypeStruct(shape, F32) for shape in ((1, 2 * D), (1, AW), (1, D), (1, D), (8, 128))],
        compiler_params=pltpu.CompilerParams(vmem_limit_bytes=VMEM_LIMIT))(gathered)


def _pack_small(w_out, w_pa, w_pp, w_pool):
    return jnp.concatenate([w_out, w_pa.reshape(-1, D), w_pp.reshape(-1, D), w_pool.reshape(-1, D)], axis=0)


def _unpack_small(a):
    o = R_OUT
    return (a[:R_PA - o], a[R_PA - o:R_PP - o].reshape(AW, 256), a[R_PP - o:R_PL - o].reshape(AW, 256),
            a[R_PL - o:].reshape(4, 32, PG))


def _pack_vec(b_gate, gamma, beta, pscale, extra):
    z = jnp.zeros((D,), F32)
    return jnp.stack([b_gate[:D], b_gate[D:], gamma, beta, jnp.concatenate([pscale, z[:D - AW]]),
                      jnp.broadcast_to(extra, (D,)), z, z])


def kernel(x, w_in, b_gate, w_pool, pool_scale, w_proj_attn, w_proj_pool, w_out, ln_gamma, ln_beta, loss_target, m_w_in, m_b_gate, m_w_pool, m_pool_scale, m_w_proj_attn, m_w_proj_pool, m_w_out, m_ln_gamma, m_ln_beta, v_w_in, v_b_gate, v_w_pool, v_pool_scale, v_w_proj_attn, v_w_proj_pool, v_w_out, v_ln_gamma, v_ln_beta):
    coords = jnp.stack([lax.axis_index("x"), lax.axis_index("y"), lax.axis_index("c")]).astype(jnp.int32)
    x2, tgt = _permute_tokens(x[0]), _permute_tokens(loss_target[0])
    xb = x2.astype(BF16)
    xt = x2.T.astype(BF16)

    pack = jnp.concatenate([w_in[0].astype(BF16),
                            _pack_small(w_out[0], w_proj_attn[0], w_proj_pool[0], w_pool[0]).astype(BF16)], axis=0)
    h, gw = _ag_proj(_arrival_order(*coords), xb, pack)
    wout = gw[:, R_OUT:R_PA].reshape(D, D)
    wpa = gw[:, R_PA:R_PP].reshape(N_DEV, AW, 256).transpose(1, 0, 2).reshape(AW, D)
    wpp = gw[:, R_PP:R_PL].reshape(N_DEV, AW, 256).transpose(1, 0, 2).reshape(AW, D)
    wpool = gw[:, R_PL:].reshape(N_DEV, 4, 32, PG).transpose(1, 0, 2, 3).reshape(4, PG, PG)

    o, lse = zip(*[_attn_fwd(g, h) for g in range(len(DILATIONS))])
    ya, yp, merged, a, b, p = _mix_fwd(h, o, lse, wpa, wpp, wpool, pool_scale, b_gate)
    dr, drb, dm, loss_part, dgamma, dbeta = _out_ln(merged, x2, tgt, wout, ln_gamma, ln_beta)

    dh, da, db, dbgate = _gate_bwd(dm, a, b, h, b_gate)
    dh, do0, do1, do2, dl0, dl1, dl2, dwpool, dpscale = _mix_bwd(
        dh, da, db, h, o, lse, p, wpa, wpp, wpool, pool_scale)
    for g, (do_g, dl_g) in enumerate(zip((do0, do1, do2), (dl0, dl1, dl2))):
        dh = _attn_bwd(g, dh, h, do_g, lse[g], dl_g)

    q = _grad_w_in_rs(_rs_columns(*coords), xt, dh)
    flat = lambda n, k: (0, n)
    d_wout = _grad_w("grad_w_out", merged.T, drb, jax.ShapeDtypeStruct((D, D), BF16),
                     pl.BlockSpec((D, 1024), flat), 1024, 1024)
    d_wpa = _grad_w("grad_w_pa", ya.T, da, jax.ShapeDtypeStruct((AW, D), BF16),
                    pl.BlockSpec((AW, 1024), flat), 1024, 1024)
    d_wpp = _grad_w("grad_w_pp", yp.T, db, jax.ShapeDtypeStruct((AW, D), BF16),
                    pl.BlockSpec((AW, 1024), flat), 1024, 1024)
    small = jnp.concatenate([
        d_wout.reshape(N_DEV, 256, D),
        d_wpa.reshape(AW, N_DEV, 256).transpose(1, 0, 2).reshape(N_DEV, -1, D),
        d_wpp.reshape(AW, N_DEV, 256).transpose(1, 0, 2).reshape(N_DEV, -1, D),
        dwpool.astype(BF16).reshape(4, N_DEV, 32, PG).transpose(1, 0, 2, 3).reshape(N_DEV, -1, D)], axis=1)
    q = _pair_sum_small(coords, small, _rs_sibling(small), q)

    grad_x, l2 = _grad_x_rs(dh, gw, dr, q)
    g_in, d_in, m_in, v_in = _adam_shard("adam_w_in", q, l2, w_in[0], m_w_in[0], v_w_in[0], 256)
    g_small = [t.reshape(w.shape) for t, w in zip(_unpack_small(_sum_small(q, l2)),
                                                  (w_out, w_proj_attn, w_proj_pool, w_pool))]
    small = (g_small,) + _adam_whole("adam_small", g_small, (w_out, w_proj_attn, w_proj_pool, w_pool),
                                     (m_w_out, m_w_proj_attn, m_w_proj_pool, m_w_pool),
                                     (v_w_out, v_w_proj_attn, v_w_proj_pool, v_w_pool))

    vec = _pack_vec(dbgate[0], dgamma[0], dbeta[0], dpscale[0], loss_part[0, 0])
    *g_vec, loss = _sum_replicated(_all_gather_direct("ag_vec", vec))
    vecs = (g_vec,) + _adam_whole("adam_replicated", g_vec, (b_gate, pool_scale, ln_gamma, ln_beta),
                                  (m_b_gate, m_pool_scale, m_ln_gamma, m_ln_beta),
                                  (v_b_gate, v_pool_scale, v_ln_gamma, v_ln_beta))
    loss = loss[0, 0]

    def leaves(kind, big):
        out, pa, pp, pool = small[kind]
        bg, ps, gm, bt = vecs[kind]
        return [big[None], bg, pool, ps, pa, pp, out, gm, bt]

    return (loss, _permute_tokens(grad_x)[None], *leaves(0, g_in), *leaves(1, d_in), *leaves(2, m_in), *leaves(3, v_in))
```

```python
import functools

import jax
import jax.numpy as jnp
from jax import lax
from jax.experimental import pallas as pl
from jax.experimental.pallas import tpu as pltpu

F32 = jnp.float32
BF16 = jnp.bfloat16

S = 4096
D = 2048
NW = 16384
AW = 1024
HD = 128
NH = 8
QB = 128
NBLK = S // QB
DILATIONS = (1, 4, 16)
POOL_WINDOWS = (2, 4, 8, 16)
PG = 256
N_DEV = 8
COL_Q, COL_K, COL_V = 0, 3 * AW, 6 * AW
COL_ZA, COL_U, COL_ZP, COL_G = 9 * AW, 10 * AW, 11 * AW, 12 * AW
DH_Z, DH_G = COL_ZA, COL_G
ALPHA = 2.0 ** 0.25
LN_EPS = 1e-5
NEG_INF = -1e30
LR, B1, B2, EPS, WD, STEP = 0.001, 0.9, 0.999, 1e-08, 0.01, 10
R_IN, R_OUT, R_PA, R_PP, R_PL = 0, 2048, 2304, 2432, 2560
R_ALL = 2576
R_SMALL = R_ALL - R_OUT
VMEM_LIMIT = 56 * 1024 * 1024
MESH = pl.DeviceIdType.MESH
ANY = pl.BlockSpec(memory_space=pl.ANY)


def _cparams(n_axes):
    return pltpu.CompilerParams(dimension_semantics=("arbitrary",) * n_axes, vmem_limit_bytes=VMEM_LIMIT)


def _sigmoid(z):
    return 0.5 * jnp.tanh(0.5 * z) + 0.5


def _nt(a, b):
    return lax.dot_general(a, b, (((1,), (1,)), ((), ())), preferred_element_type=F32)


def _tn(a, b):
    return lax.dot_general(a, b, (((0,), (0,)), ((), ())), preferred_element_type=F32)


def _nn(a, b):
    return jnp.dot(a, b, preferred_element_type=F32)


def _lin(x, y, c):
    return 4 * x + 2 * y + c


def _flip(v, f):
    return 1 - v if f else v


def _exchange(name, src, plan, *, dst_shape=None, local_dst=None):
    n = len(plan)
    in_place = dst_shape is None
    out_sds = jax.ShapeDtypeStruct(src.shape, src.dtype) if in_place else dst_shape

    def body(src_ref, dst_ref, send_sems, recv_sems, local_sem):
        x, y, c = lax.axis_index("x"), lax.axis_index("y"), lax.axis_index("c")

        def copy(k, sender):
            flip, src_index, dst_index = plan[k]
            sx, sy, sc = sender
            to = (_flip(sx, flip[0]), _flip(sy, flip[1]), _flip(sc, flip[2]))
            s = src_ref if src_index is None else src_ref.at[src_index(sx, sy, sc)]
            return pltpu.make_async_remote_copy(
                src_ref=s, dst_ref=dst_ref.at[dst_index(sx, sy, sc)],
                send_sem=send_sems.at[k], recv_sem=recv_sems.at[k],
                device_id=to, device_id_type=MESH)

        me = (x, y, c)
        if local_dst is not None:
            mine = pltpu.make_async_copy(src_ref, dst_ref.at[local_dst(x, y, c)], local_sem)
            mine.start()
        sends = [copy(k, me) for k in range(n)]
        for cp in sends:
            cp.start()
        for k in range(n):
            flip = plan[k][0]
            copy(k, (_flip(x, flip[0]), _flip(y, flip[1]), _flip(c, flip[2]))).wait_recv()
        for cp in sends:
            cp.wait_send()
        if local_dst is not None:
            mine.wait()

    return pl.pallas_call(
        body, name=name, out_shape=out_sds, in_specs=[ANY], out_specs=ANY,
        input_output_aliases={0: 0} if in_place else {},
        scratch_shapes=[pltpu.SemaphoreType.DMA((n,)), pltpu.SemaphoreType.DMA((n,)),
                        pltpu.SemaphoreType.DMA(())],
    )(src)


FLIP_C, FLIP_X, FLIP_Y, FLIP_XY = (0, 0, 1), (1, 0, 0), (0, 1, 0), (1, 1, 0)
CHIP_FLIPS = ((0, 0), (1, 0), (0, 1), (1, 1))


AG_PIECES = ((pl.ds(R_IN, D), pl.ds(0, 1024)), (pl.ds(R_IN, D), pl.ds(1024, 1024)),
             (pl.ds(R_OUT, R_PA - R_OUT), pl.ds(0, D)), (pl.ds(R_PA, R_ALL - R_PA), pl.ds(0, D)))
N_PIECES = len(AG_PIECES)
SIB, TO_X, TO_Y, ON, PASS_X, PASS_Y, PASS_D = range(7)
AG_TILES = ((0, 0), (0, 1), (1, 0), (1, 1), (2, 0), (4, 0), (3, 0), (5, 0),
            (2, 1), (4, 1), (3, 1), (5, 1), (6, 0), (6, 1), (7, 0), (7, 1))
W, G = "wait", "go"
AG_STEPS = {
    2: [(W, SIB, 0)], 3: [(W, SIB, 1)],
    4: [(W, TO_X, 0), (G, ON, 0), (G, PASS_X, 0)], 5: [(W, TO_Y, 0), (G, PASS_Y, 0)],
    6: [(W, PASS_X, 0)], 7: [(W, PASS_Y, 0)],
    8: [(W, TO_X, 1), (G, PASS_X, 1)],
    9: [(W, TO_Y, 1), (G, ON, 1), (G, PASS_Y, 1), (G, TO_X, 2), (G, TO_X, 3), (G, TO_Y, 2), (G, TO_Y, 3)],
    10: [(W, PASS_X, 1)], 11: [(W, PASS_Y, 1)],
    12: [(W, ON, 0), (G, PASS_D, 0)], 13: [(W, ON, 1), (G, PASS_D, 1)],
    14: [(W, PASS_D, 0), (W, TO_X, 2), (G, ON, 2), (G, PASS_X, 2), (W, TO_X, 3), (G, PASS_X, 3),
         (W, TO_Y, 2), (G, PASS_Y, 2), (W, TO_Y, 3), (G, ON, 3), (G, PASS_Y, 3)],
    15: [(W, PASS_D, 1)],
}
AG_LAST = [(W, SIB, 2), (W, SIB, 3), (W, ON, 2), (G, PASS_D, 2), (W, ON, 3), (G, PASS_D, 3),
           (W, PASS_X, 2), (W, PASS_X, 3), (W, PASS_Y, 2), (W, PASS_Y, 3), (W, PASS_D, 2), (W, PASS_D, 3)]


def _arrival_order(x, y, c):
    chips = [(x, y), (1 - x, y), (x, 1 - y), (1 - x, 1 - y)]
    return jnp.stack([_lin(px, py, pc) for px, py in chips for pc in (c, 1 - c)]).astype(jnp.int32)


def _ag_proj(order, xb, pack):
    tm, tn = 1024, 1024
    nrow, ntile = S // tm, len(AG_TILES)
    slabs = jnp.stack([order[pos] for pos, _ in AG_TILES])
    cols = jnp.stack([2 * order[pos] + half for pos, half in AG_TILES])

    def body(cols_ref, slabs_ref, x_ref, pack_ref, h_ref, gw_ref, wbuf, wsem, send_sems, recv_sems, local_sem):
        t, i = pl.program_id(0), pl.program_id(1)
        x, y, c = lax.axis_index("x"), lax.axis_index("y"), lax.axis_index("c")
        me = _lin(x, y, c)
        dev = {"sib": (x, y, 1 - c), "x": (1 - x, y, c), "y": (x, 1 - y, c), "d": (1 - x, 1 - y, c)}

        def slab_of(name, other_core=False):
            px, py, pc = dev[name]
            return _lin(px, py, 1 - pc if other_core else pc)

        def rdma(slab, kind, piece, to, from_pack=False):
            k = kind * N_PIECES + piece
            there = gw_ref.at[(slab, *AG_PIECES[piece])]
            return pltpu.make_async_remote_copy(
                src_ref=pack_ref.at[AG_PIECES[piece]] if from_pack else there, dst_ref=there,
                send_sem=send_sems.at[k], recv_sem=recv_sems.at[k], device_id=dev[to], device_id_type=MESH)

        def mine(kind, piece):
            if kind in (SIB, TO_X, TO_Y):
                return rdma(me, kind, piece, ("sib", "x", "y")[kind], from_pack=True)
            if kind == ON:
                frm, to = ("x", "y") if piece % 2 == 0 else ("y", "x")
                return rdma(slab_of(frm), kind, piece, to)
            return rdma(slab_of({PASS_X: "x", PASS_Y: "y", PASS_D: "d"}[kind]), kind, piece, "sib")

        def landing(kind, piece):
            slab = {SIB: slab_of("sib"), TO_X: slab_of("x"), TO_Y: slab_of("y"), ON: slab_of("d"),
                    PASS_X: slab_of("x", True), PASS_Y: slab_of("y", True), PASS_D: slab_of("d", True)}[kind]
            return rdma(slab, kind, piece, "sib")

        def run(steps):
            for what, kind, piece in steps:
                if what == W:
                    landing(kind, piece).wait_recv()
                else:
                    mine(kind, piece).start()

        local = pltpu.make_async_copy(pack_ref, gw_ref.at[me], local_sem)

        def fetch(slab, half, slot):
            src = pack_ref.at[AG_PIECES[half]] if slab is None else gw_ref.at[(slab, *AG_PIECES[half])]
            return pltpu.make_async_copy(src, wbuf.at[slot], wsem.at[slot])

        @pl.when((t == 0) & (i == 0))
        def _():
            local.start()
            run([(G, kind, piece) for piece in (0, 1) for kind in (TO_X, TO_Y, SIB)] + [(G, SIB, 2), (G, SIB, 3)])
            first = fetch(None, 0, 0)
            first.start()
            first.wait()

        for nxt in range(1, ntile):
            @pl.when((t == nxt - 1) & (i == nrow - 1))
            def _(nxt=nxt):
                run(AG_STEPS.get(nxt, []))
                fetch(None if AG_TILES[nxt][0] == 0 else slabs_ref[nxt], AG_TILES[nxt][1], nxt % 2).start()

        for slot in (0, 1):
            @pl.when(t % 2 == slot)
            def _(slot=slot):
                @pl.when((i == 0) & (t > 0))
                def _():
                    fetch(None, 0, slot).wait()
                h_ref[...] = _nn(x_ref[...], wbuf[slot]).astype(h_ref.dtype)

        @pl.when((t == ntile - 1) & (i == nrow - 1))
        def _():
            run(AG_LAST)
            for kind in range(7):
                for piece in range(N_PIECES):
                    mine(kind, piece).wait_send()
            local.wait()

    n_sem = 7 * N_PIECES
    grid_spec = pltpu.PrefetchScalarGridSpec(
        num_scalar_prefetch=2, grid=(ntile, nrow),
        in_specs=[pl.BlockSpec((tm, D), lambda t, i, cols, slabs: (i, 0)), ANY],
        out_specs=[pl.BlockSpec((tm, tn), lambda t, i, cols, slabs: (i, cols[t])), ANY],
        scratch_shapes=[pltpu.VMEM((2, D, tn), BF16), pltpu.SemaphoreType.DMA((2,)),
                        pltpu.SemaphoreType.DMA((n_sem,)), pltpu.SemaphoreType.DMA((n_sem,)),
                        pltpu.SemaphoreType.DMA(())])
    return pl.pallas_call(
        body, name="ag_proj", grid_spec=grid_spec,
        out_shape=[jax.ShapeDtypeStruct((S, NW), BF16), jax.ShapeDtypeStruct((N_DEV, R_ALL, D), BF16)],
        compiler_params=_cparams(2))(cols, slabs, xb, pack)


def _all_gather_direct(name, vec):
    own = lambda x, y, c: _lin(x, y, c)
    flips = [(fx, fy, fc) for fx in (0, 1) for fy in (0, 1) for fc in (0, 1) if (fx, fy, fc) != (0, 0, 0)]
    return _exchange(name, vec, [(f, None, own) for f in flips],
                     dst_shape=jax.ShapeDtypeStruct((N_DEV,) + vec.shape, vec.dtype), local_dst=own)


def _rs_sibling(p):
    plan = [(FLIP_C, (lambda x, y, c, f=f: _lin(_flip(x, f[0]), _flip(y, f[1]), 1 - c)),
             (lambda x, y, c, k=k: k)) for k, f in enumerate(CHIP_FLIPS)]
    return _exchange("rs_sibling", p, plan, dst_shape=jax.ShapeDtypeStruct((4,) + p.shape[1:], p.dtype))


def _pair_sum_small(coords, p, l1, q):
    def body(crd, p_ref, l_ref, _, q_ref, buf, sem):
        k = pl.program_id(0)
        buf[...] = (p_ref[...].astype(F32) + l_ref[...].astype(F32)).astype(buf.dtype)
        out = pltpu.make_async_copy(buf, q_ref.at[k, pl.ds(R_OUT, R_SMALL)], sem)
        out.start()
        out.wait()

    def p_map(k, crd):
        fx, fy = k % 2, k // 2
        px = crd[0] + fx - 2 * fx * crd[0]
        py = crd[1] + fy - 2 * fy * crd[1]
        return (_lin(px, py, crd[2]), 0, 0)

    grid_spec = pltpu.PrefetchScalarGridSpec(
        num_scalar_prefetch=1, grid=(4,),
        in_specs=[pl.BlockSpec((None, R_SMALL, D), p_map),
                  pl.BlockSpec((None, R_SMALL, D), lambda k, crd: (k, 0, 0)), ANY],
        out_specs=ANY,
        scratch_shapes=[pltpu.VMEM((R_SMALL, D), BF16), pltpu.SemaphoreType.DMA(())])
    return pl.pallas_call(body, name="pair_sum_small", grid_spec=grid_spec,
                          out_shape=jax.ShapeDtypeStruct(q.shape, q.dtype), input_output_aliases={3: 0},
                          compiler_params=_cparams(1))(coords, p, l1, q)


def _h_block(k):
    return jnp.where(k < 9, (k % 3) * 3 + k // 3, k)


RS_PIECES = (pl.ds(0, 1280), pl.ds(1280, R_ALL - 1280))
RS_ROWS = (1280, R_ALL - 1280)
RS_CHUNKS = ((320,) * 4, (432,) * 3)
RS_MERGE_STEP = 3


def _grad_x_rs(dh, g, dr, q):
    tm, tk = 512, 1024
    ni, nk = S // tm, NW // tk
    rmax = max(RS_ROWS)
    cmax = max(max(c) for c in RS_CHUNKS)

    def body(dh_ref, w_ref, dr_ref, q_ref, o_ref, l2_ref, ld_ref, mg_ref, va, vb, send_sems, recv_sems, sems):
        i, k = pl.program_id(0), pl.program_id(1)
        x, y, c = lax.axis_index("x"), lax.axis_index("y"), lax.axis_index("c")
        nbr = ((1 - x, y, c), (x, 1 - y, c))

        def rows(ref, piece):
            return ref.at[piece, pl.ds(0, RS_ROWS[piece])]

        copies = (
            (q_ref.at[3, RS_PIECES[0]], rows(ld_ref, 0), 0),
            (q_ref.at[3, RS_PIECES[1]], rows(ld_ref, 1), 1),
            (q_ref.at[1, RS_PIECES[0]], l2_ref.at[0, RS_PIECES[0]], 0),
            (q_ref.at[2, RS_PIECES[1]], l2_ref.at[1, RS_PIECES[1]], 1),
            (rows(mg_ref, 0), l2_ref.at[1, RS_PIECES[0]], 1),
            (rows(mg_ref, 1), l2_ref.at[0, RS_PIECES[1]], 0),
        )

        def copy(n):
            src, dst, axis = copies[n]
            return pltpu.make_async_remote_copy(src_ref=src, dst_ref=dst, send_sem=send_sems.at[n],
                                                recv_sem=recv_sems.at[n], device_id=nbr[axis], device_id_type=MESH)

        def merge(piece, mine):
            start = 0
            for n_rows in RS_CHUNKS[piece]:
                own = pltpu.make_async_copy(q_ref.at[mine, pl.ds(RS_PIECES[piece].start + start, n_rows)],
                                            va.at[pl.ds(0, n_rows)], sems.at[0])
                got = pltpu.make_async_copy(ld_ref.at[piece, pl.ds(start, n_rows)], vb.at[pl.ds(0, n_rows)], sems.at[1])
                own.start()
                got.start()
                own.wait()
                got.wait()
                va[pl.ds(0, n_rows)] = (va[pl.ds(0, n_rows)].astype(F32)
                                        + vb[pl.ds(0, n_rows)].astype(F32)).astype(va.dtype)
                out = pltpu.make_async_copy(va.at[pl.ds(0, n_rows)], mg_ref.at[piece, pl.ds(start, n_rows)], sems.at[2])
                out.start()
                out.wait()
                start += n_rows

        @pl.when((i == 0) & (k == 0))
        def _():
            for n in range(4):
                copy(n).start()

        @pl.when((i == RS_MERGE_STEP) & (k == 0))
        def _():
            copy(0).wait_recv()
            merge(0, 2)
            copy(4).start()
            copy(1).wait_recv()
            merge(1, 1)
            copy(5).start()

        @pl.when(k == 0)
        def _():
            o_ref[...] = ALPHA * dr_ref[...]

        o_ref[...] += _nt(dh_ref[...], w_ref[...])

        @pl.when((i == ni - 1) & (k == nk - 1))
        def _():
            for n in range(2, 6):
                copy(n).wait_recv()
            for n in range(6):
                copy(n).wait_send()

    slab = q.shape[1:]
    out = pl.pallas_call(
        body, name="grad_x_rs", grid=(ni, nk),
        in_specs=[pl.BlockSpec((tm, tk), lambda i, k: (i, k)),
                  pl.BlockSpec((None, D, tk), lambda i, k: (_h_block(k) // 2, 0, _h_block(k) % 2)),
                  pl.BlockSpec((tm, D), lambda i, k: (i, 0)), ANY],
        out_specs=[pl.BlockSpec((tm, D), lambda i, k: (i, 0)), ANY, ANY, ANY],
        out_shape=[jax.ShapeDtypeStruct((S, D), F32), jax.ShapeDtypeStruct((2,) + slab, q.dtype),
                   jax.ShapeDtypeStruct((2, rmax, slab[1]), q.dtype), jax.ShapeDtypeStruct((2, rmax, slab[1]), q.dtype)],
        scratch_shapes=[pltpu.VMEM((cmax, slab[1]), q.dtype), pltpu.VMEM((cmax, slab[1]), q.dtype),
                        pltpu.SemaphoreType.DMA((6,)), pltpu.SemaphoreType.DMA((6,)), pltpu.SemaphoreType.DMA((3,))],
        compiler_params=_cparams(2))(dh, g, dr, q)
    return out[0], out[1]


def _rs_columns(x, y, c):
    out = []
    for core in (1 - c, c):
        for fx, fy in CHIP_FLIPS:
            for half in (0, 1):
                out.append(_h_block(2 * _lin(_flip(x, fx), _flip(y, fy), core) + half))
    return jnp.stack(out).astype(jnp.int32)


def _grad_w_in_rs(cols, xt, dh):
    tn, tk = 1024, 1024
    nk = S // tk
    n_half = 8

    def body(cols_ref, a_ref, b_ref, q_ref, l1_ref, acc_ref, stage, landed, send_sems, recv_sems, sem):
        t, k = pl.program_id(0), pl.program_id(1)
        sib = (lax.axis_index("x"), lax.axis_index("y"), 1 - lax.axis_index("c"))

        @pl.when(k == 0)
        def _():
            acc_ref[...] = jnp.zeros_like(acc_ref)

        acc_ref[...] += _nn(a_ref[...], b_ref[...])

        def there(n):
            return l1_ref.at[n // 2, :, pl.ds((n % 2) * tn, tn)]

        def send(n):
            return pltpu.make_async_remote_copy(src_ref=stage.at[n % 2], dst_ref=there(n), send_sem=send_sems.at[n],
                                                recv_sem=recv_sems.at[n], device_id=sib, device_id_type=MESH)

        for n in range(n_half):
            @pl.when((t == n) & (k == nk - 1))
            def _(n=n):
                if n >= 2:
                    send(n - 2).wait_send()
                stage[n % 2] = acc_ref[...].astype(stage.dtype)
                send(n).start()

        def fetch(n):
            return pltpu.make_async_copy(there(n), landed, sem)

        for n in range(n_half):
            @pl.when((t == n_half + n) & (k == nk - 2))
            def _(n=n):
                if n == 0:
                    send(n_half - 2).wait_send()
                    send(n_half - 1).wait_send()
                send(n).wait_recv()
                fetch(n).start()

            @pl.when((t == n_half + n) & (k == nk - 1))
            def _(n=n):
                fetch(n).wait()
                q_ref[...] = (acc_ref[...] + landed[...].astype(F32)).astype(q_ref.dtype)

    mine = lambda t: jnp.maximum(t - n_half, 0)
    grid_spec = pltpu.PrefetchScalarGridSpec(
        num_scalar_prefetch=1, grid=(2 * n_half, nk),
        in_specs=[pl.BlockSpec((D, tk), lambda t, k, cols: (0, k)),
                  pl.BlockSpec((tk, tn), lambda t, k, cols: (k, cols[t]))],
        out_specs=[pl.BlockSpec((None, D, tn), lambda t, k, cols: (mine(t) // 2, 0, mine(t) % 2)), ANY],
        scratch_shapes=[pltpu.VMEM((D, tn), F32), pltpu.VMEM((2, D, tn), BF16), pltpu.VMEM((D, tn), BF16),
                        pltpu.SemaphoreType.DMA((n_half,)), pltpu.SemaphoreType.DMA((n_half,)),
                        pltpu.SemaphoreType.DMA(())])
    q, _ = pl.pallas_call(
        body, name="grad_w_in_rs", grid_spec=grid_spec,
        out_shape=[jax.ShapeDtypeStruct((4, R_ALL, D), BF16), jax.ShapeDtypeStruct((4, D, D), BF16)],
        compiler_params=_cparams(2))(cols, xt, dh)
    return q


def _grad_w(name, at, b, out_shape, out_spec, tn, tk):
    m, k_all = at.shape
    n_all = b.shape[1]
    nk = k_all // tk

    def body(a_ref, b_ref, o_ref, acc_ref):
        k = pl.program_id(1)

        @pl.when(k == 0)
        def _():
            acc_ref[...] = jnp.zeros_like(acc_ref)

        acc_ref[...] += _nn(a_ref[...], b_ref[...])

        @pl.when(k == nk - 1)
        def _():
            o_ref[...] = acc_ref[...].astype(o_ref.dtype)

    return pl.pallas_call(
        body, name=name, grid=(n_all // tn, nk),
        in_specs=[pl.BlockSpec((m, tk), lambda n, k: (0, k)),
                  pl.BlockSpec((tk, tn), lambda n, k: (k, n))],
        out_specs=out_spec, out_shape=out_shape,
        scratch_shapes=[pltpu.VMEM((m, tn), F32)], compiler_params=_cparams(2))(at, b)


NR = 16
TI = 16
TM = NR * TI
NT = S // TM
ATT_QB = (256, 128, 128)
ATT_NB = (16, 8, 2)
ATT_BLOCKS = (16, 32, 32)


def _permute_tokens(a):
    return a.reshape(NT, TI, NR, a.shape[-1]).transpose(0, 2, 1, 3).reshape(a.shape)


def _attn_shape(g, c):
    if g == 0:
        return (S, c)
    if g == 1:
        return (NT, 4, 4, TI, c)
    return (NT, NR, TI, c)


def _attn_view(g, a):
    return a.reshape(_attn_shape(g, a.shape[-1]))


def _attn_spec(g, width, col, blk):
    if g == 0:
        return pl.BlockSpec((TM, width), lambda b: (blk(b), col))
    if g == 1:
        return pl.BlockSpec((2, 4, None, TI, width), lambda b: (blk(b) % 8, 0, blk(b) // 8, 0, col))
    return pl.BlockSpec((8, None, TI, width), lambda b: (blk(b) % 2, blk(b) // 2, 0, col))


def _pieces(g):
    if g == 1:
        return [(t, m) for t in range(2) for m in range(4)]
    return [(t,) for t in range(8)]


def _get(g, ref, sl):
    if g == 0:
        return ref[:, sl]
    return jnp.concatenate([ref[(*p, slice(None), sl)] for p in _pieces(g)], axis=0)


def _put(g, ref, sl, val):
    if g == 0:
        ref[:, sl] = val
    else:
        for n, p in enumerate(_pieces(g)):
            ref[(*p, slice(None), sl)] = val[TI * n:TI * (n + 1)]


def _block_pos(g, a):
    if g == 0:
        return 16 * (a % 16) + a // 16
    if g == 1:
        return 64 * (a // 64) + 4 * (a % 16) + (a // 16) % 4
    return a


def _attn_mask(g, n):
    qb = ATT_QB[g]
    qa = lax.broadcasted_iota(jnp.int32, (qb, 2 * qb), 0)
    kc = lax.broadcasted_iota(jnp.int32, (qb, 2 * qb), 1)
    cur = kc >= qb
    dist = _block_pos(g, qa) - _block_pos(g, kc % qb) + jnp.where(cur, 0, qb)
    return (dist >= 0) & (dist <= QB) & (cur | (n > 0))


def _qkv_specs(g, clamp):
    cur = lambda col: _attn_spec(g, AW, col, clamp)
    prev = lambda col: _attn_spec(g, AW, col, lambda b: jnp.maximum(clamp(b) - 1, 0))
    qc, kc, vc = (c // AW + g for c in (COL_Q, COL_K, COL_V))
    return [cur(qc), cur(kc), prev(kc), cur(vc), prev(vc)]


def _attn_fwd(g, h):
    scale = HD ** -0.5
    hv = _attn_view(g, h)

    def body(q_ref, kc_ref, kp_ref, vc_ref, vp_ref, o_ref, l_ref):
        valid = _attn_mask(g, pl.program_id(0) % ATT_NB[g])
        for hh in range(NH):
            sl = slice(hh * HD, (hh + 1) * HD)
            kh = jnp.concatenate([_get(g, kp_ref, sl), _get(g, kc_ref, sl)], axis=0)
            vh = jnp.concatenate([_get(g, vp_ref, sl), _get(g, vc_ref, sl)], axis=0)
            s = jnp.where(valid, _nt(_get(g, q_ref, sl), kh) * scale, NEG_INF)
            m = jnp.max(s, axis=-1, keepdims=True)
            e = jnp.exp(s - m)
            den = jnp.sum(e, axis=-1, keepdims=True)
            _put(g, o_ref, sl, (_nn(e.astype(BF16), vh) * (1.0 / den)).astype(o_ref.dtype))
            _put(g, l_ref, slice(hh, hh + 1), m + jnp.log(den))

    same = lambda b: b
    o, lse = pl.pallas_call(
        body, name=f"attn_fwd_{g}", grid=(ATT_BLOCKS[g],),
        in_specs=_qkv_specs(g, same),
        out_specs=[_attn_spec(g, AW, 0, same), _attn_spec(g, NH, 0, same)],
        out_shape=[jax.ShapeDtypeStruct(_attn_shape(g, AW), BF16), jax.ShapeDtypeStruct(_attn_shape(g, NH), F32)],
        compiler_params=_cparams(1))(hv, hv, hv, hv, hv)
    return o.reshape(S, AW), lse.reshape(S, NH)


def _attn_bwd(g, dh, h, do, lse, delta):
    scale = HD ** -0.5
    qb = ATT_QB[g]
    last = ATT_BLOCKS[g] - 1
    clamp = lambda b: jnp.minimum(b, last)
    behind = lambda b: jnp.maximum(b - 1, 0)
    hv = _attn_view(g, h)

    def body(q_ref, kc_ref, kp_ref, vc_ref, vp_ref, do_ref, l_ref, dl_ref, _, dh_ref, cq_ref, ck_ref, cv_ref):
        b = pl.program_id(0)

        def write(col, val):
            _put(g, dh_ref, slice(col, col + HD), val.astype(dh_ref.dtype))

        @pl.when(b == 0)
        def _():
            cq_ref[...] = jnp.zeros_like(cq_ref)
            ck_ref[...] = jnp.zeros_like(ck_ref)
            cv_ref[...] = jnp.zeros_like(cv_ref)

        @pl.when(b <= last)
        def _():
            valid = _attn_mask(g, b % ATT_NB[g])
            for hh in range(NH):
                sl = slice(hh * HD, (hh + 1) * HD)
                one = slice(hh, hh + 1)
                qh, doh = _get(g, q_ref, sl), _get(g, do_ref, sl)
                kh = jnp.concatenate([_get(g, kp_ref, sl), _get(g, kc_ref, sl)], axis=0)
                vh = jnp.concatenate([_get(g, vp_ref, sl), _get(g, vc_ref, sl)], axis=0)
                s = _nt(qh, kh) * scale
                p = jnp.where(valid, jnp.exp(s - _get(g, l_ref, one)), 0.0)
                ds = p * (_nt(doh, vh) - _get(g, dl_ref, one))
                dsb = (ds * scale).astype(BF16)
                dk2 = _tn(dsb, qh)
                dv2 = _tn(p.astype(BF16), doh)
                write(hh * HD, cq_ref[:, sl])
                write(AW + hh * HD, ck_ref[:, sl] + dk2[:qb])
                write(2 * AW + hh * HD, cv_ref[:, sl] + dv2[:qb])
                cq_ref[:, sl] = _nn(dsb, kh)
                ck_ref[:, sl] = dk2[qb:]
                cv_ref[:, sl] = dv2[qb:]

        @pl.when(b > last)
        def _():
            for hh in range(NH):
                sl = slice(hh * HD, (hh + 1) * HD)
                write(hh * HD, cq_ref[:, sl])
                write(AW + hh * HD, ck_ref[:, sl])
                write(2 * AW + hh * HD, cv_ref[:, sl])

    out = pl.pallas_call(
        body, name=f"attn_bwd_{g}", grid=(ATT_BLOCKS[g] + 1,),
        in_specs=_qkv_specs(g, clamp) + [_attn_spec(g, AW, 0, clamp), _attn_spec(g, NH, 0, clamp),
                                         _attn_spec(g, NH, 0, clamp), ANY],
        out_specs=_attn_spec(g, 3 * AW, g, behind),
        out_shape=jax.ShapeDtypeStruct(_attn_shape(g, NW), BF16),
        input_output_aliases={8: 0},
        scratch_shapes=[pltpu.VMEM((qb, AW), F32)] * 3,
        compiler_params=_cparams(1))(hv, hv, hv, hv, hv, _attn_view(g, do), _attn_view(g, lse), _attn_view(g, delta),
                                     _attn_view(g, dh))
    return out.reshape(S, NW)


def _group_weights(l0, l1, l2):
    m = jnp.maximum(jnp.maximum(l0, l1), l2)
    e0, e1, e2 = jnp.exp(l0 - m), jnp.exp(l1 - m), jnp.exp(l2 - m)
    inv = 1.0 / (e0 + e1 + e2)
    return e0 * inv, e1 * inv, e2 * inv


def _residue(ref, r, sl):
    return ref[r * TI:(r + 1) * TI, sl].astype(F32)


def _total(parts):
    return functools.reduce(lambda x, y: x + y, parts)


def _pool_tokens(up_ref, uc_ref, p_ref, tile):
    j0 = lax.broadcasted_iota(jnp.int32, (TI, 1), 0) == 0
    first = (tile == 0) & j0
    for r in range(NR):
        out = []
        for g, w in enumerate(POOL_WINDOWS):
            sl = slice(g * PG, (g + 1) * PG)
            own = _residue(uc_ref, r, sl)
            acc = _total([own] + [_residue(uc_ref, r - k, sl) for k in range(1, min(r, w - 1) + 1)])
            wrapped = [NR + r - k for k in range(r + 1, w)]
            if wrapped:
                wc = _total([_residue(uc_ref, q, sl) for q in wrapped])
                wp = jnp.where(tile > 0, _total([_residue(up_ref, q, sl) for q in wrapped]), 0.0)
                acc = acc + jnp.where(j0, pltpu.roll(wp, 1, 0), pltpu.roll(wc, 1, 0))
            out.append(acc * jnp.where(first, 1.0 / min(r + 1, w), 1.0 / w) - own)
        p_ref[r * TI:(r + 1) * TI, :] = jnp.concatenate(out, axis=1).astype(p_ref.dtype)


def _pool_tokens_bwd(dp, nxt_ref, du_ref, tile):
    ji = lax.broadcasted_iota(jnp.int32, (TI, 1), 0)
    first = (tile == 0) & (ji == 0)
    piece = lambda g, r: dp[g][r * TI:(r + 1) * TI]
    dpc = [[piece(g, r) * jnp.where(first, 1.0 / min(r + 1, w), 1.0 / w) for r in range(NR)]
           for g, w in enumerate(POOL_WINDOWS)]
    for r in range(NR):
        out = []
        for g, w in enumerate(POOL_WINDOWS):
            sl = slice(g * PG, (g + 1) * PG)
            acc = _total([dpc[g][r + k] for k in range(w) if r + k < NR])
            wrapped = [r + k - NR for k in range(1, w) if r + k >= NR]
            if wrapped:
                wc = _total([dpc[g][q] for q in wrapped])
                wn = _total([nxt_ref[q * TI:(q + 1) * TI, sl] for q in wrapped])
                acc = acc + jnp.where(ji == TI - 1, pltpu.roll(wn, TI - 1, 0), pltpu.roll(wc, TI - 1, 0))
            out.append(acc - piece(g, r))
        du_ref[r * TI:(r + 1) * TI, :] = jnp.concatenate(out, axis=1).astype(du_ref.dtype)
    for r in range(NR):
        nxt_ref[r * TI:(r + 1) * TI, :] = jnp.concatenate([dpc[g][r] for g in range(len(POOL_WINDOWS))], axis=1)


def _pool_linear(pb, wpool_ref):
    return jnp.concatenate([_nn(pb[:, g * PG:(g + 1) * PG], wpool_ref[g]) for g in range(len(POOL_WINDOWS))], axis=1)


def _tok(width, col=0, rev=False):
    if rev:
        return pl.BlockSpec((TM, width), lambda i: (NT - 1 - i, col))
    return pl.BlockSpec((TM, width), lambda i: (i, col))


def _whole(shape):
    return pl.BlockSpec(shape, lambda i: (0,) * len(shape))


def _prep_x(x2):
    def body(x_ref, xb_ref, xt_ref):
        xb_ref[...] = x_ref[...].astype(BF16)
        xt_ref[...] = x_ref[...].T.astype(BF16)

    return pl.pallas_call(
        body, name="prep_x", grid=(NT,), in_specs=[_tok(D)],
        out_specs=[_tok(D), pl.BlockSpec((D, TM), lambda i: (0, i))],
        out_shape=[jax.ShapeDtypeStruct((S, D), BF16), jax.ShapeDtypeStruct((D, S), BF16)],
        compiler_params=_cparams(1))(x2)


def _mix_fwd(h, o, lse, wpa, wpp, wpool, pscale, bgate):
    def body(o0_ref, o1_ref, o2_ref, l0_ref, l1_ref, l2_ref, za_ref, uc_ref, up_ref, zp_ref, gp_ref,
             wpa_ref, wpp_ref, wpool_ref, ps_ref, bg_ref,
             mg_ref, a_ref, b_ref, p_ref, yat_ref, ypt_ref, mgt_ref, ya_ref, yp_ref):
        i = pl.program_id(0)
        w0, w1, w2 = _group_weights(l0_ref[...], l1_ref[...], l2_ref[...])
        za = za_ref[...].astype(F32)
        silu_a = za * _sigmoid(za)
        for hh in range(NH):
            sl = slice(hh * HD, (hh + 1) * HD)
            c = slice(hh, hh + 1)
            oh = (w0[:, c] * o0_ref[:, sl].astype(F32) + w1[:, c] * o1_ref[:, sl].astype(F32)
                  + w2[:, c] * o2_ref[:, sl].astype(F32))
            ya = oh * silu_a[:, sl]
            ya_ref[:, sl] = ya.astype(BF16)
            yat_ref[sl, :] = ya.T.astype(BF16)
        _pool_tokens(up_ref, uc_ref, p_ref, i)
        zp = zp_ref[...].astype(F32)
        yp = _pool_linear(p_ref[...], wpool_ref) * ps_ref[...] * (zp * _sigmoid(zp))
        yp_ref[...] = yp.astype(BF16)
        ypt_ref[...] = yp.T.astype(BF16)
        a = _nn(ya_ref[...], wpa_ref[...])
        b = _nn(yp_ref[...], wpp_ref[...])
        a_ref[...] = a.astype(BF16)
        b_ref[...] = b.astype(BF16)
        gates = _sigmoid(gp_ref[...].astype(F32) + bg_ref[...])
        mg = gates[:, :D] * a + gates[:, D:] * b
        mg_ref[...] = mg.astype(BF16)
        mgt_ref[...] = mg.T.astype(BF16)

    u_prev = pl.BlockSpec((TM, AW), lambda i: (jnp.maximum(i - 1, 0), COL_U // AW))
    across = lambda width: pl.BlockSpec((width, TM), lambda i: (0, i))
    return pl.pallas_call(
        body, name="mix_fwd", grid=(NT,),
        in_specs=[_tok(AW)] * 3 + [_tok(NH)] * 3
        + [_tok(AW, COL_ZA // AW), _tok(AW, COL_U // AW), u_prev, _tok(AW, COL_ZP // AW), _tok(2 * D, COL_G // (2 * D))]
        + [_whole((AW, D)), _whole((AW, D)), _whole((4, PG, PG)), _whole((1, AW)), _whole((1, 2 * D))],
        out_specs=[_tok(D), _tok(D), _tok(D), _tok(AW), across(AW), across(AW), across(D)],
        out_shape=[jax.ShapeDtypeStruct((S, D), BF16)] * 3 + [jax.ShapeDtypeStruct((S, AW), BF16)]
        + [jax.ShapeDtypeStruct((AW, S), BF16)] * 2 + [jax.ShapeDtypeStruct((D, S), BF16)],
        scratch_shapes=[pltpu.VMEM((TM, AW), BF16), pltpu.VMEM((TM, AW), BF16)],
        compiler_params=_cparams(1))(*o, *lse, h, h, h, h, h, wpa, wpp, wpool, pscale, bgate)


def _out_ln(merged, x, target, wout, gamma, beta):
    def body(mg_ref, x_ref, t_ref, w_ref, g_ref, b_ref, dr_ref, drb_ref, dm_ref, loss_ref, dg_ref, db_ref):
        i = pl.program_id(0)

        @pl.when(i == 0)
        def _():
            loss_ref[...] = jnp.zeros_like(loss_ref)
            dg_ref[...] = jnp.zeros_like(dg_ref)
            db_ref[...] = jnp.zeros_like(db_ref)

        r = ALPHA * x_ref[...] + _nn(mg_ref[...], w_ref[...])
        mu = jnp.mean(r, axis=-1, keepdims=True)
        rc = r - mu
        rstd = lax.rsqrt(jnp.mean(rc * rc, axis=-1, keepdims=True) + LN_EPS)
        xhat = rc * rstd
        err = xhat * g_ref[...] + b_ref[...] - t_ref[...]
        loss_ref[...] += 0.5 * jnp.sum(jnp.mean(err * err, axis=-1, keepdims=True), axis=0, keepdims=True)
        dy = err * (1.0 / D)
        dg_ref[...] += jnp.sum(dy * xhat, axis=0, keepdims=True)
        db_ref[...] += jnp.sum(dy, axis=0, keepdims=True)
        dxh = dy * g_ref[...]
        dr = rstd * (dxh - jnp.mean(dxh, axis=-1, keepdims=True)
                     - xhat * jnp.mean(dxh * xhat, axis=-1, keepdims=True))
        dr_ref[...] = dr
        drb_ref[...] = dr.astype(BF16)
        dm_ref[...] = _nt(drb_ref[...], w_ref[...]).astype(BF16)

    return pl.pallas_call(
        body, name="out_ln", grid=(NT,),
        in_specs=[_tok(D), _tok(D), _tok(D), _whole((D, D)), _whole((1, D)), _whole((1, D))],
        out_specs=[_tok(D), _tok(D), _tok(D), _whole((8, 128)), _whole((1, D)), _whole((1, D))],
        out_shape=[jax.ShapeDtypeStruct((S, D), F32), jax.ShapeDtypeStruct((S, D), BF16),
                   jax.ShapeDtypeStruct((S, D), BF16), jax.ShapeDtypeStruct((8, 128), F32),
                   jax.ShapeDtypeStruct((1, D), F32), jax.ShapeDtypeStruct((1, D), F32)],
        compiler_params=_cparams(1))(merged, x, target, wout, gamma, beta)


def _gate_bwd(dm, a, b, h, bgate):
    def body(dm_ref, a_ref, b_ref, gp_ref, bg_ref, dgp_ref, da_ref, db_ref, dbg_ref):
        @pl.when(pl.program_id(0) == 0)
        def _():
            dbg_ref[...] = jnp.zeros_like(dbg_ref)

        dm_ = dm_ref[...].astype(F32)
        gates = _sigmoid(gp_ref[...].astype(F32) + bg_ref[...])
        ga, gb = gates[:, :D], gates[:, D:]
        da_ref[...] = (dm_ * ga).astype(BF16)
        db_ref[...] = (dm_ * gb).astype(BF16)
        dgp = jnp.concatenate([dm_ * a_ref[...].astype(F32) * ga * (1.0 - ga),
                               dm_ * b_ref[...].astype(F32) * gb * (1.0 - gb)], axis=1)
        dgp_ref[...] = dgp.astype(BF16)
        dbg_ref[...] += jnp.sum(dgp, axis=0, keepdims=True)

    return pl.pallas_call(
        body, name="gate_bwd", grid=(NT,),
        in_specs=[_tok(D), _tok(D), _tok(D), _tok(2 * D, COL_G // (2 * D)), _whole((1, 2 * D))],
        out_specs=[_tok(2 * D, DH_G // (2 * D)), _tok(D), _tok(D), _whole((1, 2 * D))],
        out_shape=[jax.ShapeDtypeStruct((S, NW), BF16), jax.ShapeDtypeStruct((S, D), BF16),
                   jax.ShapeDtypeStruct((S, D), BF16), jax.ShapeDtypeStruct((1, 2 * D), F32)],
        compiler_params=_cparams(1))(dm, a, b, h, bgate)


def _mix_bwd(dh, da, db, h, o, lse, p, wpa, wpp, wpool, pscale):
    def body(_, da_ref, db_ref, o0_ref, o1_ref, o2_ref, l0_ref, l1_ref, l2_ref, za_ref, zp_ref, p_ref,
             wpa_ref, wpp_ref, wpool_ref, ps_ref,
             dh_ref, do0_ref, do1_ref, do2_ref, dl0_ref, dl1_ref, dl2_ref, dwp_ref, dps_ref,
             nxt_ref):
        i = pl.program_id(0)
        tile = NT - 1 - i
        dza_ref, du_ref, dzp_ref = (dh_ref.at[:, pl.ds(n * AW, AW)] for n in range(3))

        @pl.when(i == 0)
        def _():
            nxt_ref[...] = jnp.zeros_like(nxt_ref)
            dwp_ref[...] = jnp.zeros_like(dwp_ref)
            dps_ref[...] = jnp.zeros_like(dps_ref)

        dya = _nt(da_ref[...], wpa_ref[...])
        w0, w1, w2 = _group_weights(l0_ref[...], l1_ref[...], l2_ref[...])
        za = za_ref[...].astype(F32)
        sig = _sigmoid(za)
        silu_a = za * sig
        dsilu_a = sig * (1.0 + za * (1.0 - sig))
        for hh in range(NH):
            sl = slice(hh * HD, (hh + 1) * HD)
            c = slice(hh, hh + 1)
            oh = (w0[:, c] * o0_ref[:, sl].astype(F32) + w1[:, c] * o1_ref[:, sl].astype(F32)
                  + w2[:, c] * o2_ref[:, sl].astype(F32))
            doh = dya[:, sl] * silu_a[:, sl]
            dza_ref[:, sl] = (dya[:, sl] * oh * dsilu_a[:, sl]).astype(BF16)
            dot_ = jnp.sum(doh * oh, axis=-1, keepdims=True)
            do0_ref[:, sl] = (w0[:, c] * doh).astype(BF16)
            do1_ref[:, sl] = (w1[:, c] * doh).astype(BF16)
            do2_ref[:, sl] = (w2[:, c] * doh).astype(BF16)
            dl0_ref[:, c] = w0[:, c] * dot_
            dl1_ref[:, c] = w1[:, c] * dot_
            dl2_ref[:, c] = w2[:, c] * dot_
        dyp = _nt(db_ref[...], wpp_ref[...])
        pb = p_ref[...]
        pw = _pool_linear(pb, wpool_ref)
        zp = zp_ref[...].astype(F32)
        sigp = _sigmoid(zp)
        dypre = dyp * (zp * sigp)
        dzp_ref[...] = (dyp * (pw * ps_ref[...]) * (sigp * (1.0 + zp * (1.0 - sigp)))).astype(BF16)
        dps_ref[...] += jnp.sum(dypre * pw, axis=0, keepdims=True)
        dpw = (dypre * ps_ref[...]).astype(BF16)
        dp = []
        for g in range(len(POOL_WINDOWS)):
            sl = slice(g * PG, (g + 1) * PG)
            dwp_ref[g] += _tn(pb[:, sl], dpw[:, sl])
            dp.append(_nt(dpw[:, sl], wpool_ref[g]))
        _pool_tokens_bwd(dp, nxt_ref, du_ref, tile)

    r = functools.partial(_tok, rev=True)
    return pl.pallas_call(
        body, name="mix_bwd", grid=(NT,),
        in_specs=[ANY, r(D), r(D)] + [r(AW)] * 3 + [r(NH)] * 3 + [r(AW, COL_ZA // AW), r(AW, COL_ZP // AW), r(AW)]
        + [_whole((AW, D)), _whole((AW, D)), _whole((4, PG, PG)), _whole((1, AW))],
        out_specs=[r(3 * AW, DH_Z // (3 * AW))] + [r(AW)] * 3 + [r(NH)] * 3 + [_whole((4, PG, PG)), _whole((1, AW))],
        out_shape=[jax.ShapeDtypeStruct((S, NW), BF16)] + [jax.ShapeDtypeStruct((S, AW), BF16)] * 3
        + [jax.ShapeDtypeStruct((S, NH), F32)] * 3
        + [jax.ShapeDtypeStruct((4, PG, PG), F32), jax.ShapeDtypeStruct((1, AW), F32)],
        input_output_aliases={0: 0},
        scratch_shapes=[pltpu.VMEM((TM, AW), F32)],
        compiler_params=_cparams(1))(dh, da, db, *o, *lse, h, h, p, wpa, wpp, wpool, pscale)


def _adamw(w, g, m, v):
    m = B1 * m + (1.0 - B1) * g
    v = B2 * v + (1.0 - B2) * jnp.square(g)
    m_hat = m / (1.0 - B1 ** STEP)
    v_hat = v / (1.0 - B2 ** STEP)
    return -LR * (m_hat / (jnp.sqrt(v_hat) + EPS) + WD * w), m, v


def _adam_shard(name, q, l2, w, m, v, tr):
    rows = w.shape[0]

    def body(q_ref, l_ref, w_ref, m_ref, v_ref, g_out, d_out, m_out, v_out):
        g = q_ref[...].astype(F32)
        for k in range(2):
            g = g + l_ref[k].astype(F32)
        g_out[...] = g
        d_out[...], m_out[...], v_out[...] = _adamw(w_ref[...], g, m_ref[...], v_ref[...])

    blk = pl.BlockSpec((tr, D), lambda i: (i, 0))
    return pl.pallas_call(
        body, name=name, grid=(rows // tr,),
        in_specs=[pl.BlockSpec((None, tr, D), lambda i: (0, i, 0)), pl.BlockSpec((2, tr, D), lambda i: (0, i, 0)),
                  blk, blk, blk],
        out_specs=[blk] * 4, out_shape=[jax.ShapeDtypeStruct((rows, D), F32)] * 4,
        compiler_params=_cparams(1))(q, l2, w, m, v)


def _sum_small(q, l2):
    def body(q_ref, l_ref, g_out, buf, sems):
        rows = pl.ds(R_OUT, R_SMALL)
        copies = [pltpu.make_async_copy(src, buf.at[n], sems.at[n])
                  for n, src in enumerate((q_ref.at[0, rows], l_ref.at[0, rows], l_ref.at[1, rows]))]
        for cp in copies:
            cp.start()
        for cp in copies:
            cp.wait()
        g_out[...] = buf[0].astype(F32) + buf[1].astype(F32) + buf[2].astype(F32)

    return pl.pallas_call(
        body, name="sum_small", in_specs=[ANY, ANY], out_shape=jax.ShapeDtypeStruct((R_SMALL, D), F32),
        scratch_shapes=[pltpu.VMEM((3, R_SMALL, D), q.dtype), pltpu.SemaphoreType.DMA((3,))],
        compiler_params=pltpu.CompilerParams(vmem_limit_bytes=VMEM_LIMIT))(q, l2)


def _adam_whole(name, grads, weights, ms, vs):
    n = len(grads)

    def body(*refs):
        ins, outs = refs[:4 * n], refs[4 * n:]
        for t in range(n):
            g, w, m, v = (ins[k * n + t][...] for k in range(4))
            outs[t][...], outs[n + t][...], outs[2 * n + t][...] = _adamw(w, g, m, v)

    out = pl.pallas_call(
        body, name=name, out_shape=[jax.ShapeDtypeStruct(w.shape, F32) for w in weights] * 3,
        compiler_params=pltpu.CompilerParams(vmem_limit_bytes=VMEM_LIMIT))(*grads, *weights, *ms, *vs)
    return out[:n], out[n:2 * n], out[2 * n:]


def _sum_replicated(gathered):
    def body(g_ref, bg_out, ps_out, gm_out, bt_out, loss_out):
        g = g_ref[0]
        for k in range(1, N_DEV):
            g = g + g_ref[k]
        bg_out[...] = jnp.concatenate([g[0:1], g[1:2]], axis=1)
        gm_out[...] = g[2:3]
        bt_out[...] = g[3:4]
        ps_out[...] = g[4:5, :AW]
        loss_out[...] = jnp.broadcast_to(g[5:6, :128], loss_out.shape)

    return pl.pallas_call(
        body, name="sum_replicated",
        out_shape=[jax.ShapeDtypeStruct(shape, F32) for shape in ((1, 2 * D), (1, AW), (1, D), (1, D), (8, 128))],
        compiler_params=pltpu.CompilerParams(vmem_limit_bytes=VMEM_LIMIT))(gathered)


def _pack_small(w_out, w_pa, w_pp, w_pool):
    return jnp.concatenate([w_out, w_pa.reshape(-1, D), w_pp.reshape(-1, D), w_pool.reshape(-1, D)], axis=0)


def _unpack_small(a):
    o = R_OUT
    return (a[:R_PA - o], a[R_PA - o:R_PP - o].reshape(AW, 256), a[R_PP - o:R_PL - o].reshape(AW, 256),
            a[R_PL - o:].reshape(4, 32, PG))


def _pack_vec(b_gate, gamma, beta, pscale, extra):
    z = jnp.zeros((D,), F32)
    return jnp.stack([b_gate[:D], b_gate[D:], gamma, beta, jnp.concatenate([pscale, z[:D - AW]]),
                      jnp.broadcast_to(extra, (D,)), z, z])


def kernel(x, w_in, b_gate, w_pool, pool_scale, w_proj_attn, w_proj_pool, w_out, ln_gamma, ln_beta, loss_target, m_w_in, m_b_gate, m_w_pool, m_pool_scale, m_w_proj_attn, m_w_proj_pool, m_w_out, m_ln_gamma, m_ln_beta, v_w_in, v_b_gate, v_w_pool, v_pool_scale, v_w_proj_attn, v_w_proj_pool, v_w_out, v_ln_gamma, v_ln_beta):
    coords = jnp.stack([lax.axis_index("x"), lax.axis_index("y"), lax.axis_index("c")]).astype(jnp.int32)
    x2, tgt = _permute_tokens(x[0]), _permute_tokens(loss_target[0])
    xb, xt = _prep_x(x2)

    pack = jnp.concatenate([w_in[0].astype(BF16),
                            _pack_small(w_out[0], w_proj_attn[0], w_proj_pool[0], w_pool[0]).astype(BF16)], axis=0)
    h, gw = _ag_proj(_arrival_order(*coords), xb, pack)
    wout = gw[:, R_OUT:R_PA].reshape(D, D)
    wpa = gw[:, R_PA:R_PP].reshape(N_DEV, AW, 256).transpose(1, 0, 2).reshape(AW, D)
    wpp = gw[:, R_PP:R_PL].reshape(N_DEV, AW, 256).transpose(1, 0, 2).reshape(AW, D)
    wpool = gw[:, R_PL:].reshape(N_DEV, 4, 32, PG).transpose(1, 0, 2, 3).reshape(4, PG, PG)

    o, lse = zip(*[_attn_fwd(g, h) for g in range(len(DILATIONS))])
    merged, a, b, p, yat, ypt, mgt = _mix_fwd(h, o, lse, wpa, wpp, wpool, pool_scale, b_gate)
    dr, drb, dm, loss_part, dgamma, dbeta = _out_ln(merged, x2, tgt, wout, ln_gamma, ln_beta)

    dh, da, db, dbgate = _gate_bwd(dm, a, b, h, b_gate)
    dh, do0, do1, do2, dl0, dl1, dl2, dwpool, dpscale = _mix_bwd(
        dh, da, db, h, o, lse, p, wpa, wpp, wpool, pool_scale)
    for g, (do_g, dl_g) in enumerate(zip((do0, do1, do2), (dl0, dl1, dl2))):
        dh = _attn_bwd(g, dh, h, do_g, lse[g], dl_g)

    q = _grad_w_in_rs(_rs_columns(*coords), xt, dh)
    flat = lambda n, k: (0, n)
    d_wout = _grad_w("grad_w_out", mgt, drb, jax.ShapeDtypeStruct((D, D), BF16),
                     pl.BlockSpec((D, 1024), flat), 1024, 1024)
    d_wpa = _grad_w("grad_w_pa", yat, da, jax.ShapeDtypeStruct((AW, D), BF16),
                    pl.BlockSpec((AW, 1024), flat), 1024, 1024)
    d_wpp = _grad_w("grad_w_pp", ypt, db, jax.ShapeDtypeStruct((AW, D), BF16),
                    pl.BlockSpec((AW, 1024), flat), 1024, 1024)
    small = jnp.concatenate([
        d_wout.reshape(N_DEV, 256, D),
        d_wpa.reshape(AW, N_DEV, 256).transpose(1, 0, 2).reshape(N_DEV, -1, D),
        d_wpp.reshape(AW, N_DEV, 256).transpose(1, 0, 2).reshape(N_DEV, -1, D),
        dwpool.astype(BF16).reshape(4, N_DEV, 32, PG).transpose(1, 0, 2, 3).reshape(N_DEV, -1, D)], axis=1)
    q = _pair_sum_small(coords, small, _rs_sibling(small), q)

    grad_x, l2 = _grad_x_rs(dh, gw, dr, q)
    g_in, d_in, m_in, v_in = _adam_shard("adam_w_in", q, l2, w_in[0], m_w_in[0], v_w_in[0], 256)
    g_small = [t.reshape(w.shape) for t, w in zip(_unpack_small(_sum_small(q, l2)),
                                                  (w_out, w_proj_attn, w_proj_pool, w_pool))]
    small = (g_small,) + _adam_whole("adam_small", g_small, (w_out, w_proj_attn, w_proj_pool, w_pool),
                                     (m_w_out, m_w_proj_attn, m_w_proj_pool, m_w_pool),
                                     (v_w_out, v_w_proj_attn, v_w_proj_pool, v_w_pool))

    vec = _pack_vec(dbgate[0], dgamma[0], dbeta[0], dpscale[0], loss_part[0, 0])
    *g_vec, loss = _sum_replicated(_all_gather_direct("ag_vec", vec))
    vecs = (g_vec,) + _adam_whole("adam_replicated", g_vec, (b_gate, pool_scale, ln_gamma, ln_beta),
                                  (m_b_gate, m_pool_scale, m_ln_gamma, m_ln_beta),
                                  (v_b_gate, v_pool_scale, v_ln_gamma, v_ln_beta))
    loss = loss[0, 0]

    def leaves(kind, big):
        out, pa, pp, pool = small[kind]
        bg, ps, gm, bt = vecs[kind]
        return [big[None], bg, pool, ps, pa, pp, out, gm, bt]

    return (loss, _permute_tokens(grad_x)[None], *leaves(0, g_in), *leaves(1, d_in), *leaves(2, m_in), *leaves(3, v_in))
```

```python
import functools

import jax
import jax.numpy as jnp
from jax import lax
from jax.experimental import pallas as pl
from jax.experimental.pallas import tpu as pltpu

F32 = jnp.float32
BF16 = jnp.bfloat16

S = 4096
D = 2048
NW = 16384
AW = 1024
HD = 128
NH = 8
QB = 128
NBLK = S // QB
DILATIONS = (1, 4, 16)
POOL_WINDOWS = (2, 4, 8, 16)
PG = 256
N_DEV = 8
COL_Q, COL_K, COL_V = 0, 3 * AW, 6 * AW
COL_ZA, COL_U, COL_ZP, COL_G = 9 * AW, 10 * AW, 11 * AW, 12 * AW
DH_Z, DH_G = COL_ZA, COL_G
ALPHA = 2.0 ** 0.25
LN_EPS = 1e-5
NEG_INF = -1e30
LR, B1, B2, EPS, WD, STEP = 0.001, 0.9, 0.999, 1e-08, 0.01, 10
R_IN, R_OUT, R_PA, R_PP, R_PL = 0, 2048, 2304, 2432, 2560
R_ALL = 2576
R_SMALL = R_ALL - R_OUT
VMEM_LIMIT = 56 * 1024 * 1024
MESH = pl.DeviceIdType.MESH
ANY = pl.BlockSpec(memory_space=pl.ANY)


def _cparams(n_axes):
    return pltpu.CompilerParams(dimension_semantics=("arbitrary",) * n_axes, vmem_limit_bytes=VMEM_LIMIT)


def _sigmoid(z):
    return 0.5 * jnp.tanh(0.5 * z) + 0.5


def _nt(a, b):
    return lax.dot_general(a, b, (((1,), (1,)), ((), ())), preferred_element_type=F32)


def _tn(a, b):
    return lax.dot_general(a, b, (((0,), (0,)), ((), ())), preferred_element_type=F32)


def _nn(a, b):
    return jnp.dot(a, b, preferred_element_type=F32)


def _lin(x, y, c):
    return 4 * x + 2 * y + c


def _flip(v, f):
    return 1 - v if f else v


def _exchange(name, src, plan, *, dst_shape=None, local_dst=None):
    n = len(plan)
    in_place = dst_shape is None
    out_sds = jax.ShapeDtypeStruct(src.shape, src.dtype) if in_place else dst_shape

    def body(src_ref, dst_ref, send_sems, recv_sems, local_sem):
        x, y, c = lax.axis_index("x"), lax.axis_index("y"), lax.axis_index("c")

        def copy(k, sender):
            flip, src_index, dst_index = plan[k]
            sx, sy, sc = sender
            to = (_flip(sx, flip[0]), _flip(sy, flip[1]), _flip(sc, flip[2]))
            s = src_ref if src_index is None else src_ref.at[src_index(sx, sy, sc)]
            return pltpu.make_async_remote_copy(
                src_ref=s, dst_ref=dst_ref.at[dst_index(sx, sy, sc)],
                send_sem=send_sems.at[k], recv_sem=recv_sems.at[k],
                device_id=to, device_id_type=MESH)

        me = (x, y, c)
        if local_dst is not None:
            mine = pltpu.make_async_copy(src_ref, dst_ref.at[local_dst(x, y, c)], local_sem)
            mine.start()
        sends = [copy(k, me) for k in range(n)]
        for cp in sends:
            cp.start()
        for k in range(n):
            flip = plan[k][0]
            copy(k, (_flip(x, flip[0]), _flip(y, flip[1]), _flip(c, flip[2]))).wait_recv()
        for cp in sends:
            cp.wait_send()
        if local_dst is not None:
            mine.wait()

    return pl.pallas_call(
        body, name=name, out_shape=out_sds, in_specs=[ANY], out_specs=ANY,
        input_output_aliases={0: 0} if in_place else {},
        scratch_shapes=[pltpu.SemaphoreType.DMA((n,)), pltpu.SemaphoreType.DMA((n,)),
                        pltpu.SemaphoreType.DMA(())],
    )(src)


FLIP_C, FLIP_X, FLIP_Y, FLIP_XY = (0, 0, 1), (1, 0, 0), (0, 1, 0), (1, 1, 0)
CHIP_FLIPS = ((0, 0), (1, 0), (0, 1), (1, 1))


AG_PIECES = ((pl.ds(R_IN, D), pl.ds(0, 1024)), (pl.ds(R_IN, D), pl.ds(1024, 1024)),
             (pl.ds(R_OUT, R_PA - R_OUT), pl.ds(0, D)), (pl.ds(R_PA, R_ALL - R_PA), pl.ds(0, D)))
N_PIECES = len(AG_PIECES)
SIB, TO_X, TO_Y, ON, PASS_X, PASS_Y, PASS_D = range(7)
AG_TILES = ((0, 0), (0, 1), (1, 0), (1, 1), (2, 0), (4, 0), (3, 0), (5, 0),
            (2, 1), (4, 1), (3, 1), (5, 1), (6, 0), (6, 1), (7, 0), (7, 1))
W, G = "wait", "go"
AG_STEPS = {
    2: [(W, SIB, 0)], 3: [(W, SIB, 1)],
    4: [(W, TO_X, 0), (G, ON, 0), (G, PASS_X, 0)], 5: [(W, TO_Y, 0), (G, PASS_Y, 0)],
    6: [(W, PASS_X, 0)], 7: [(W, PASS_Y, 0)],
    8: [(W, TO_X, 1), (G, PASS_X, 1)],
    9: [(W, TO_Y, 1), (G, ON, 1), (G, PASS_Y, 1), (G, TO_X, 2), (G, TO_X, 3), (G, TO_Y, 2), (G, TO_Y, 3)],
    10: [(W, PASS_X, 1)], 11: [(W, PASS_Y, 1)],
    12: [(W, ON, 0), (G, PASS_D, 0)], 13: [(W, ON, 1), (G, PASS_D, 1)],
    14: [(W, PASS_D, 0), (W, TO_X, 2), (G, ON, 2), (G, PASS_X, 2), (W, TO_X, 3), (G, PASS_X, 3),
         (W, TO_Y, 2), (G, PASS_Y, 2), (W, TO_Y, 3), (G, ON, 3), (G, PASS_Y, 3)],
    15: [(W, PASS_D, 1)],
}
AG_LAST = [(W, SIB, 2), (W, SIB, 3), (W, ON, 2), (G, PASS_D, 2), (W, ON, 3), (G, PASS_D, 3),
           (W, PASS_X, 2), (W, PASS_X, 3), (W, PASS_Y, 2), (W, PASS_Y, 3), (W, PASS_D, 2), (W, PASS_D, 3)]


def _arrival_order(x, y, c):
    chips = [(x, y), (1 - x, y), (x, 1 - y), (1 - x, 1 - y)]
    return jnp.stack([_lin(px, py, pc) for px, py in chips for pc in (c, 1 - c)]).astype(jnp.int32)


def _ag_proj(order, xb, pack):
    tm, tn = 1024, 1024
    nrow, ntile = S // tm, len(AG_TILES)
    slabs = jnp.stack([order[pos] for pos, _ in AG_TILES])
    cols = jnp.stack([2 * order[pos] + half for pos, half in AG_TILES])

    def body(cols_ref, slabs_ref, x_ref, pack_ref, h_ref, gw_ref, wbuf, wsem, send_sems, recv_sems, local_sem):
        t, i = pl.program_id(0), pl.program_id(1)
        x, y, c = lax.axis_index("x"), lax.axis_index("y"), lax.axis_index("c")
        me = _lin(x, y, c)
        dev = {"sib": (x, y, 1 - c), "x": (1 - x, y, c), "y": (x, 1 - y, c), "d": (1 - x, 1 - y, c)}

        def slab_of(name, other_core=False):
            px, py, pc = dev[name]
            return _lin(px, py, 1 - pc if other_core else pc)

        def rdma(slab, kind, piece, to, from_pack=False):
            k = kind * N_PIECES + piece
            there = gw_ref.at[(slab, *AG_PIECES[piece])]
            return pltpu.make_async_remote_copy(
                src_ref=pack_ref.at[AG_PIECES[piece]] if from_pack else there, dst_ref=there,
                send_sem=send_sems.at[k], recv_sem=recv_sems.at[k], device_id=dev[to], device_id_type=MESH)

        def mine(kind, piece):
            if kind in (SIB, TO_X, TO_Y):
                return rdma(me, kind, piece, ("sib", "x", "y")[kind], from_pack=True)
            if kind == ON:
                frm, to = ("x", "y") if piece % 2 == 0 else ("y", "x")
                return rdma(slab_of(frm), kind, piece, to)
            return rdma(slab_of({PASS_X: "x", PASS_Y: "y", PASS_D: "d"}[kind]), kind, piece, "sib")

        def landing(kind, piece):
            slab = {SIB: slab_of("sib"), TO_X: slab_of("x"), TO_Y: slab_of("y"), ON: slab_of("d"),
                    PASS_X: slab_of("x", True), PASS_Y: slab_of("y", True), PASS_D: slab_of("d", True)}[kind]
            return rdma(slab, kind, piece, "sib")

        def run(steps):
            for what, kind, piece in steps:
                if what == W:
                    landing(kind, piece).wait_recv()
                else:
                    mine(kind, piece).start()

        local = pltpu.make_async_copy(pack_ref, gw_ref.at[me], local_sem)

        def fetch(slab, half, slot):
            src = pack_ref.at[AG_PIECES[half]] if slab is None else gw_ref.at[(slab, *AG_PIECES[half])]
            return pltpu.make_async_copy(src, wbuf.at[slot], wsem.at[slot])

        @pl.when((t == 0) & (i == 0))
        def _():
            local.start()
            run([(G, kind, piece) for piece in (0, 1) for kind in (TO_X, TO_Y, SIB)] + [(G, SIB, 2), (G, SIB, 3)])
            first = fetch(None, 0, 0)
            first.start()
            first.wait()

        for nxt in range(1, ntile):
            @pl.when((t == nxt - 1) & (i == nrow - 1))
            def _(nxt=nxt):
                run(AG_STEPS.get(nxt, []))
                fetch(None if AG_TILES[nxt][0] == 0 else slabs_ref[nxt], AG_TILES[nxt][1], nxt % 2).start()

        for slot in (0, 1):
            @pl.when(t % 2 == slot)
            def _(slot=slot):
                @pl.when((i == 0) & (t > 0))
                def _():
                    fetch(None, 0, slot).wait()
                h_ref[...] = _nn(x_ref[...], wbuf[slot]).astype(h_ref.dtype)

        @pl.when((t == ntile - 1) & (i == nrow - 1))
        def _():
            run(AG_LAST)
            for kind in range(7):
                for piece in range(N_PIECES):
                    mine(kind, piece).wait_send()
            local.wait()

    n_sem = 7 * N_PIECES
    grid_spec = pltpu.PrefetchScalarGridSpec(
        num_scalar_prefetch=2, grid=(ntile, nrow),
        in_specs=[pl.BlockSpec((tm, D), lambda t, i, cols, slabs: (i, 0)), ANY],
        out_specs=[pl.BlockSpec((tm, tn), lambda t, i, cols, slabs: (i, cols[t])), ANY],
        scratch_shapes=[pltpu.VMEM((2, D, tn), BF16), pltpu.SemaphoreType.DMA((2,)),
                        pltpu.SemaphoreType.DMA((n_sem,)), pltpu.SemaphoreType.DMA((n_sem,)),
                        pltpu.SemaphoreType.DMA(())])
    return pl.pallas_call(
        body, name="ag_proj", grid_spec=grid_spec,
        out_shape=[jax.ShapeDtypeStruct((S, NW), BF16), jax.ShapeDtypeStruct((N_DEV, R_ALL, D), BF16)],
        compiler_params=_cparams(2))(cols, slabs, xb, pack)


def _all_gather_direct(name, vec):
    own = lambda x, y, c: _lin(x, y, c)
    flips = [(fx, fy, fc) for fx in (0, 1) for fy in (0, 1) for fc in (0, 1) if (fx, fy, fc) != (0, 0, 0)]
    return _exchange(name, vec, [(f, None, own) for f in flips],
                     dst_shape=jax.ShapeDtypeStruct((N_DEV,) + vec.shape, vec.dtype), local_dst=own)


def _rs_sibling(p):
    plan = [(FLIP_C, (lambda x, y, c, f=f: _lin(_flip(x, f[0]), _flip(y, f[1]), 1 - c)),
             (lambda x, y, c, k=k: k)) for k, f in enumerate(CHIP_FLIPS)]
    return _exchange("rs_sibling", p, plan, dst_shape=jax.ShapeDtypeStruct((4,) + p.shape[1:], p.dtype))


def _pair_sum_small(coords, p, l1, q):
    def body(crd, p_ref, l_ref, _, q_ref, buf, sem):
        k = pl.program_id(0)
        buf[...] = (p_ref[...].astype(F32) + l_ref[...].astype(F32)).astype(buf.dtype)
        out = pltpu.make_async_copy(buf, q_ref.at[k, pl.ds(R_OUT, R_SMALL)], sem)
        out.start()
        out.wait()

    def p_map(k, crd):
        fx, fy = k % 2, k // 2
        px = crd[0] + fx - 2 * fx * crd[0]
        py = crd[1] + fy - 2 * fy * crd[1]
        return (_lin(px, py, crd[2]), 0, 0)

    grid_spec = pltpu.PrefetchScalarGridSpec(
        num_scalar_prefetch=1, grid=(4,),
        in_specs=[pl.BlockSpec((None, R_SMALL, D), p_map),
                  pl.BlockSpec((None, R_SMALL, D), lambda k, crd: (k, 0, 0)), ANY],
        out_specs=ANY,
        scratch_shapes=[pltpu.VMEM((R_SMALL, D), BF16), pltpu.SemaphoreType.DMA(())])
    return pl.pallas_call(body, name="pair_sum_small", grid_spec=grid_spec,
                          out_shape=jax.ShapeDtypeStruct(q.shape, q.dtype), input_output_aliases={3: 0},
                          compiler_params=_cparams(1))(coords, p, l1, q)


def _h_block(k):
    return jnp.where(k < 9, (k % 3) * 3 + k // 3, k)


RS_PIECES = (pl.ds(0, 1280), pl.ds(1280, R_ALL - 1280))
RS_ROWS = (1280, R_ALL - 1280)
RS_CHUNKS = ((320,) * 4, (432,) * 3)
RS_MERGE_STEP = 3


def _grad_x_rs(dh, g, dr, q):
    tm, tk = 512, 1024
    ni, nk = S // tm, NW // tk
    rmax = max(RS_ROWS)
    cmax = max(max(c) for c in RS_CHUNKS)

    def body(dh_ref, w_ref, dr_ref, q_ref, o_ref, l2_ref, ld_ref, mg_ref, va, vb, send_sems, recv_sems, sems):
        i, k = pl.program_id(0), pl.program_id(1)
        x, y, c = lax.axis_index("x"), lax.axis_index("y"), lax.axis_index("c")
        nbr = ((1 - x, y, c), (x, 1 - y, c))

        def rows(ref, piece):
            return ref.at[piece, pl.ds(0, RS_ROWS[piece])]

        copies = (
            (q_ref.at[3, RS_PIECES[0]], rows(ld_ref, 0), 0),
            (q_ref.at[3, RS_PIECES[1]], rows(ld_ref, 1), 1),
            (q_ref.at[1, RS_PIECES[0]], l2_ref.at[0, RS_PIECES[0]], 0),
            (q_ref.at[2, RS_PIECES[1]], l2_ref.at[1, RS_PIECES[1]], 1),
            (rows(mg_ref, 0), l2_ref.at[1, RS_PIECES[0]], 1),
            (rows(mg_ref, 1), l2_ref.at[0, RS_PIECES[1]], 0),
        )

        def copy(n):
            src, dst, axis = copies[n]
            return pltpu.make_async_remote_copy(src_ref=src, dst_ref=dst, send_sem=send_sems.at[n],
                                                recv_sem=recv_sems.at[n], device_id=nbr[axis], device_id_type=MESH)

        def merge(piece, mine):
            start = 0
            for n_rows in RS_CHUNKS[piece]:
                own = pltpu.make_async_copy(q_ref.at[mine, pl.ds(RS_PIECES[piece].start + start, n_rows)],
                                            va.at[pl.ds(0, n_rows)], sems.at[0])
                got = pltpu.make_async_copy(ld_ref.at[piece, pl.ds(start, n_rows)], vb.at[pl.ds(0, n_rows)], sems.at[1])
                own.start()
                got.start()
                own.wait()
                got.wait()
                va[pl.ds(0, n_rows)] = (va[pl.ds(0, n_rows)].astype(F32)
                                        + vb[pl.ds(0, n_rows)].astype(F32)).astype(va.dtype)
                out = pltpu.make_async_copy(va.at[pl.ds(0, n_rows)], mg_ref.at[piece, pl.ds(start, n_rows)], sems.at[2])
                out.start()
                out.wait()
                start += n_rows

        @pl.when((i == 0) & (k == 0))
        def _():
            for n in range(4):
                copy(n).start()

        @pl.when((i == RS_MERGE_STEP) & (k == 0))
        def _():
            copy(0).wait_recv()
            merge(0, 2)
            copy(4).start()
            copy(1).wait_recv()
            merge(1, 1)
            copy(5).start()

        @pl.when(k == 0)
        def _():
            o_ref[...] = ALPHA * dr_ref[...]

        o_ref[...] += _nt(dh_ref[...], w_ref[...])

        @pl.when((i == ni - 1) & (k == nk - 1))
        def _():
            for n in range(2, 6):
                copy(n).wait_recv()
            for n in range(6):
                copy(n).wait_send()

    slab = q.shape[1:]
    out = pl.pallas_call(
        body, name="grad_x_rs", grid=(ni, nk),
        in_specs=[pl.BlockSpec((tm, tk), lambda i, k: (i, k)),
                  pl.BlockSpec((None, D, tk), lambda i, k: (_h_block(k) // 2, 0, _h_block(k) % 2)),
                  pl.BlockSpec((tm, D), lambda i, k: (i, 0)), ANY],
        out_specs=[pl.BlockSpec((tm, D), lambda i, k: (i, 0)), ANY, ANY, ANY],
        out_shape=[jax.ShapeDtypeStruct((S, D), F32), jax.ShapeDtypeStruct((2,) + slab, q.dtype),
                   jax.ShapeDtypeStruct((2, rmax, slab[1]), q.dtype), jax.ShapeDtypeStruct((2, rmax, slab[1]), q.dtype)],
        scratch_shapes=[pltpu.VMEM((cmax, slab[1]), q.dtype), pltpu.VMEM((cmax, slab[1]), q.dtype),
                        pltpu.SemaphoreType.DMA((6,)), pltpu.SemaphoreType.DMA((6,)), pltpu.SemaphoreType.DMA((3,))],
        compiler_params=_cparams(2))(dh, g, dr, q)
    return out[0], out[1]


def _rs_columns(x, y, c):
    out = []
    for core in (1 - c, c):
        for fx, fy in CHIP_FLIPS:
            for half in (0, 1):
                out.append(_h_block(2 * _lin(_flip(x, fx), _flip(y, fy), core) + half))
    return jnp.stack(out).astype(jnp.int32)


def _grad_w_in_rs(cols, xt, dh):
    tn, tk = 1024, 1024
    nk = S // tk
    n_half = 8

    def body(cols_ref, a_ref, b_ref, q_ref, l1_ref, acc_ref, stage, landed, send_sems, recv_sems, sem):
        t, k = pl.program_id(0), pl.program_id(1)
        sib = (lax.axis_index("x"), lax.axis_index("y"), 1 - lax.axis_index("c"))

        @pl.when(k == 0)
        def _():
            acc_ref[...] = jnp.zeros_like(acc_ref)

        acc_ref[...] += _nn(a_ref[...], b_ref[...])

        def there(n):
            return l1_ref.at[n // 2, :, pl.ds((n % 2) * tn, tn)]

        def send(n):
            return pltpu.make_async_remote_copy(src_ref=stage.at[n % 2], dst_ref=there(n), send_sem=send_sems.at[n],
                                                recv_sem=recv_sems.at[n], device_id=sib, device_id_type=MESH)

        for n in range(n_half):
            @pl.when((t == n) & (k == nk - 1))
            def _(n=n):
                if n >= 2:
                    send(n - 2).wait_send()
                stage[n % 2] = acc_ref[...].astype(stage.dtype)
                send(n).start()

        def fetch(n):
            return pltpu.make_async_copy(there(n), landed, sem)

        for n in range(n_half):
            @pl.when((t == n_half + n) & (k == nk - 2))
            def _(n=n):
                if n == 0:
                    send(n_half - 2).wait_send()
                    send(n_half - 1).wait_send()
                send(n).wait_recv()
                fetch(n).start()

            @pl.when((t == n_half + n) & (k == nk - 1))
            def _(n=n):
                fetch(n).wait()
                q_ref[...] = (acc_ref[...] + landed[...].astype(F32)).astype(q_ref.dtype)

    mine = lambda t: jnp.maximum(t - n_half, 0)
    grid_spec = pltpu.PrefetchScalarGridSpec(
        num_scalar_prefetch=1, grid=(2 * n_half, nk),
        in_specs=[pl.BlockSpec((D, tk), lambda t, k, cols: (0, k)),
                  pl.BlockSpec((tk, tn), lambda t, k, cols: (k, cols[t]))],
        out_specs=[pl.BlockSpec((None, D, tn), lambda t, k, cols: (mine(t) // 2, 0, mine(t) % 2)), ANY],
        scratch_shapes=[pltpu.VMEM((D, tn), F32), pltpu.VMEM((2, D, tn), BF16), pltpu.VMEM((D, tn), BF16),
                        pltpu.SemaphoreType.DMA((n_half,)), pltpu.SemaphoreType.DMA((n_half,)),
                        pltpu.SemaphoreType.DMA(())])
    q, _ = pl.pallas_call(
        body, name="grad_w_in_rs", grid_spec=grid_spec,
        out_shape=[jax.ShapeDtypeStruct((4, R_ALL, D), BF16), jax.ShapeDtypeStruct((4, D, D), BF16)],
        compiler_params=_cparams(2))(cols, xt, dh)
    return q


def _grad_w(name, at, b, out_shape, out_spec, tn, tk):
    m, k_all = at.shape
    n_all = b.shape[1]
    nk = k_all // tk

    def body(a_ref, b_ref, o_ref, acc_ref):
        k = pl.program_id(1)

        @pl.when(k == 0)
        def _():
            acc_ref[...] = jnp.zeros_like(acc_ref)

        acc_ref[...] += _nn(a_ref[...], b_ref[...])

        @pl.when(k == nk - 1)
        def _():
            o_ref[...] = acc_ref[...].astype(o_ref.dtype)

    return pl.pallas_call(
        body, name=name, grid=(n_all // tn, nk),
        in_specs=[pl.BlockSpec((m, tk), lambda n, k: (0, k)),
                  pl.BlockSpec((tk, tn), lambda n, k: (k, n))],
        out_specs=out_spec, out_shape=out_shape,
        scratch_shapes=[pltpu.VMEM((m, tn), F32)], compiler_params=_cparams(2))(at, b)


NR = 16
TI = 16
TM = NR * TI
NT = S // TM
ATT_QB = (256, 128, 256)
ATT_NB = (16, 8, 1)
ATT_BLOCKS = (16, 32, 16)


def _permute_tokens(a):
    return a.reshape(NT, TI, NR, a.shape[-1]).transpose(0, 2, 1, 3).reshape(a.shape)


def _attn_shape(g, c):
    if g == 0:
        return (S, c)
    if g == 1:
        return (NT, 4, 4, TI, c)
    return (NT, NR, TI, c)


def _attn_view(g, a):
    return a.reshape(_attn_shape(g, a.shape[-1]))


def _attn_spec(g, width, col, blk):
    if g == 0:
        return pl.BlockSpec((TM, width), lambda b: (blk(b), col))
    if g == 1:
        return pl.BlockSpec((2, 4, None, TI, width), lambda b: (blk(b) % 8, 0, blk(b) // 8, 0, col))
    return pl.BlockSpec((NT, None, TI, width), lambda b: (0, blk(b), 0, col))


def _pieces(g):
    if g == 1:
        return [(t, m) for t in range(2) for m in range(4)]
    return [(t,) for t in range(NT)]


def _get(g, ref, sl):
    if g == 0:
        return ref[:, sl]
    return jnp.concatenate([ref[(*p, slice(None), sl)] for p in _pieces(g)], axis=0)


def _put(g, ref, sl, val):
    if g == 0:
        ref[:, sl] = val
    else:
        for n, p in enumerate(_pieces(g)):
            ref[(*p, slice(None), sl)] = val[TI * n:TI * (n + 1)]


def _block_pos(g, a):
    if g == 0:
        return 16 * (a % 16) + a // 16
    if g == 1:
        return 64 * (a // 64) + 4 * (a % 16) + (a // 16) % 4
    return a


def _attn_mask(g, n):
    qb = ATT_QB[g]
    if ATT_NB[g] == 1:
        qa = lax.broadcasted_iota(jnp.int32, (qb, qb), 0)
        kc = lax.broadcasted_iota(jnp.int32, (qb, qb), 1)
        dist = _block_pos(g, qa) - _block_pos(g, kc)
        return (dist >= 0) & (dist <= QB)
    qa = lax.broadcasted_iota(jnp.int32, (qb, 2 * qb), 0)
    kc = lax.broadcasted_iota(jnp.int32, (qb, 2 * qb), 1)
    cur = kc >= qb
    dist = _block_pos(g, qa) - _block_pos(g, kc % qb) + jnp.where(cur, 0, qb)
    return (dist >= 0) & (dist <= QB) & (cur | (n > 0))


def _keys(g, prev_ref, cur_ref, sl):
    if ATT_NB[g] == 1:
        return _get(g, cur_ref, sl)
    return jnp.concatenate([_get(g, prev_ref, sl), _get(g, cur_ref, sl)], axis=0)


def _qkv_specs(g, clamp):
    cur = lambda col: _attn_spec(g, AW, col, clamp)
    prev = lambda col: _attn_spec(g, AW, col, lambda b: jnp.maximum(clamp(b) - 1, 0))
    qc, kc, vc = (c // AW + g for c in (COL_Q, COL_K, COL_V))
    return [cur(qc), cur(kc), prev(kc), cur(vc), prev(vc)]


def _attn_fwd(g, h):
    scale = HD ** -0.5
    hv = _attn_view(g, h)

    def body(q_ref, kc_ref, kp_ref, vc_ref, vp_ref, o_ref, l_ref):
        valid = _attn_mask(g, pl.program_id(0) % ATT_NB[g])
        for hh in range(NH):
            sl = slice(hh * HD, (hh + 1) * HD)
            kh, vh = _keys(g, kp_ref, kc_ref, sl), _keys(g, vp_ref, vc_ref, sl)
            s = jnp.where(valid, _nt(_get(g, q_ref, sl), kh) * scale, NEG_INF)
            m = jnp.max(s, axis=-1, keepdims=True)
            e = jnp.exp(s - m)
            den = jnp.sum(e, axis=-1, keepdims=True)
            _put(g, o_ref, sl, (_nn(e.astype(BF16), vh) * (1.0 / den)).astype(o_ref.dtype))
            _put(g, l_ref, slice(hh, hh + 1), m + jnp.log(den))

    same = lambda b: b
    o, lse = pl.pallas_call(
        body, name=f"attn_fwd_{g}", grid=(ATT_BLOCKS[g],),
        in_specs=_qkv_specs(g, same),
        out_specs=[_attn_spec(g, AW, 0, same), _attn_spec(g, NH, 0, same)],
        out_shape=[jax.ShapeDtypeStruct(_attn_shape(g, AW), BF16), jax.ShapeDtypeStruct(_attn_shape(g, NH), F32)],
        compiler_params=_cparams(1))(hv, hv, hv, hv, hv)
    return o.reshape(S, AW), lse.reshape(S, NH)


def _attn_bwd(g, dh, h, do, lse, delta):
    scale = HD ** -0.5
    qb = ATT_QB[g]
    carried = ATT_NB[g] > 1
    last = ATT_BLOCKS[g] - 1
    clamp = lambda b: jnp.minimum(b, last)
    behind = lambda b: jnp.maximum(b - 1, 0)
    hv = _attn_view(g, h)

    def body(q_ref, kc_ref, kp_ref, vc_ref, vp_ref, do_ref, l_ref, dl_ref, _, dh_ref, *carry):
        b = pl.program_id(0)

        def write(col, val):
            _put(g, dh_ref, slice(col, col + HD), val.astype(dh_ref.dtype))

        def block():
            valid = _attn_mask(g, b % ATT_NB[g])
            for hh in range(NH):
                sl = slice(hh * HD, (hh + 1) * HD)
                one = slice(hh, hh + 1)
                qh, doh = _get(g, q_ref, sl), _get(g, do_ref, sl)
                kh, vh = _keys(g, kp_ref, kc_ref, sl), _keys(g, vp_ref, vc_ref, sl)
                s = _nt(qh, kh) * scale
                p = jnp.where(valid, jnp.exp(s - _get(g, l_ref, one)), 0.0)
                ds = p * (_nt(doh, vh) - _get(g, dl_ref, one))
                dsb = (ds * scale).astype(BF16)
                dq = _nn(dsb, kh)
                dk2 = _tn(dsb, qh)
                dv2 = _tn(p.astype(BF16), doh)
                if carried:
                    cq_ref, ck_ref, cv_ref = carry
                    write(hh * HD, cq_ref[:, sl])
                    write(AW + hh * HD, ck_ref[:, sl] + dk2[:qb])
                    write(2 * AW + hh * HD, cv_ref[:, sl] + dv2[:qb])
                    cq_ref[:, sl] = dq
                    ck_ref[:, sl] = dk2[qb:]
                    cv_ref[:, sl] = dv2[qb:]
                else:
                    write(hh * HD, dq)
                    write(AW + hh * HD, dk2)
                    write(2 * AW + hh * HD, dv2)

        if not carried:
            block()
            return

        @pl.when(b == 0)
        def _():
            for ref in carry:
                ref[...] = jnp.zeros_like(ref)

        pl.when(b <= last)(block)

        @pl.when(b > last)
        def _():
            for hh in range(NH):
                for n, ref in enumerate(carry):
                    write(n * AW + hh * HD, ref[:, hh * HD:(hh + 1) * HD])

    out = pl.pallas_call(
        body, name=f"attn_bwd_{g}", grid=(ATT_BLOCKS[g] + carried,),
        in_specs=_qkv_specs(g, clamp) + [_attn_spec(g, AW, 0, clamp), _attn_spec(g, NH, 0, clamp),
                                         _attn_spec(g, NH, 0, clamp), ANY],
        out_specs=_attn_spec(g, 3 * AW, g, behind if carried else clamp),
        out_shape=jax.ShapeDtypeStruct(_attn_shape(g, NW), BF16),
        input_output_aliases={8: 0},
        scratch_shapes=[pltpu.VMEM((qb, AW), F32)] * (3 if carried else 0),
        compiler_params=_cparams(1))(hv, hv, hv, hv, hv, _attn_view(g, do), _attn_view(g, lse), _attn_view(g, delta),
                                     _attn_view(g, dh))
    return out.reshape(S, NW)


def _group_weights(l0, l1, l2):
    m = jnp.maximum(jnp.maximum(l0, l1), l2)
    e0, e1, e2 = jnp.exp(l0 - m), jnp.exp(l1 - m), jnp.exp(l2 - m)
    inv = 1.0 / (e0 + e1 + e2)
    return e0 * inv, e1 * inv, e2 * inv


def _residue(ref, r, sl):
    return ref[r * TI:(r + 1) * TI, sl].astype(F32)


def _total(parts):
    return functools.reduce(lambda x, y: x + y, parts)


def _pool_tokens(up_ref, uc_ref, p_ref, tile):
    j0 = lax.broadcasted_iota(jnp.int32, (TI, 1), 0) == 0
    first = (tile == 0) & j0
    for r in range(NR):
        out = []
        for g, w in enumerate(POOL_WINDOWS):
            sl = slice(g * PG, (g + 1) * PG)
            own = _residue(uc_ref, r, sl)
            acc = _total([own] + [_residue(uc_ref, r - k, sl) for k in range(1, min(r, w - 1) + 1)])
            wrapped = [NR + r - k for k in range(r + 1, w)]
            if wrapped:
                wc = _total([_residue(uc_ref, q, sl) for q in wrapped])
                wp = jnp.where(tile > 0, _total([_residue(up_ref, q, sl) for q in wrapped]), 0.0)
                acc = acc + jnp.where(j0, pltpu.roll(wp, 1, 0), pltpu.roll(wc, 1, 0))
            out.append(acc * jnp.where(first, 1.0 / min(r + 1, w), 1.0 / w) - own)
        p_ref[r * TI:(r + 1) * TI, :] = jnp.concatenate(out, axis=1).astype(p_ref.dtype)


def _pool_tokens_bwd(dp, nxt_ref, du_ref, tile):
    ji = lax.broadcasted_iota(jnp.int32, (TI, 1), 0)
    first = (tile == 0) & (ji == 0)
    piece = lambda g, r: dp[g][r * TI:(r + 1) * TI]
    dpc = [[piece(g, r) * jnp.where(first, 1.0 / min(r + 1, w), 1.0 / w) for r in range(NR)]
           for g, w in enumerate(POOL_WINDOWS)]
    for r in range(NR):
        out = []
        for g, w in enumerate(POOL_WINDOWS):
            sl = slice(g * PG, (g + 1) * PG)
            acc = _total([dpc[g][r + k] for k in range(w) if r + k < NR])
            wrapped = [r + k - NR for k in range(1, w) if r + k >= NR]
            if wrapped:
                wc = _total([dpc[g][q] for q in wrapped])
                wn = _total([nxt_ref[q * TI:(q + 1) * TI, sl] for q in wrapped])
                acc = acc + jnp.where(ji == TI - 1, pltpu.roll(wn, TI - 1, 0), pltpu.roll(wc, TI - 1, 0))
            out.append(acc - piece(g, r))
        du_ref[r * TI:(r + 1) * TI, :] = jnp.concatenate(out, axis=1).astype(du_ref.dtype)
    for r in range(NR):
        nxt_ref[r * TI:(r + 1) * TI, :] = jnp.concatenate([dpc[g][r] for g in range(len(POOL_WINDOWS))], axis=1)


def _pool_linear(pb, wpool_ref):
    return jnp.concatenate([_nn(pb[:, g * PG:(g + 1) * PG], wpool_ref[g]) for g in range(len(POOL_WINDOWS))], axis=1)


def _tok(width, col=0, rev=False):
    if rev:
        return pl.BlockSpec((TM, width), lambda i: (NT - 1 - i, col))
    return pl.BlockSpec((TM, width), lambda i: (i, col))


def _whole(shape):
    return pl.BlockSpec(shape, lambda i: (0,) * len(shape))


def _prep_x(x2):
    rows = 1024

    def body(x_ref, xb_ref, xt_ref):
        xb_ref[...] = x_ref[...].astype(BF16)
        xt_ref[...] = x_ref[...].T.astype(BF16)

    return pl.pallas_call(
        body, name="prep_x", grid=(S // rows,), in_specs=[pl.BlockSpec((rows, D), lambda i: (i, 0))],
        out_specs=[pl.BlockSpec((rows, D), lambda i: (i, 0)), pl.BlockSpec((D, rows), lambda i: (0, i))],
        out_shape=[jax.ShapeDtypeStruct((S, D), BF16), jax.ShapeDtypeStruct((D, S), BF16)],
        compiler_params=_cparams(1))(x2)


def _mix_fwd(h, o, lse, wpa, wpp, wpool, pscale, bgate):
    def body(o0_ref, o1_ref, o2_ref, l0_ref, l1_ref, l2_ref, za_ref, uc_ref, up_ref, zp_ref, gp_ref,
             wpa_ref, wpp_ref, wpool_ref, ps_ref, bg_ref,
             mg_ref, a_ref, b_ref, p_ref, yat_ref, ypt_ref, mgt_ref, ya_ref, yp_ref):
        i = pl.program_id(0)
        w0, w1, w2 = _group_weights(l0_ref[...], l1_ref[...], l2_ref[...])
        za = za_ref[...].astype(F32)
        silu_a = za * _sigmoid(za)
        for hh in range(NH):
            sl = slice(hh * HD, (hh + 1) * HD)
            c = slice(hh, hh + 1)
            oh = (w0[:, c] * o0_ref[:, sl].astype(F32) + w1[:, c] * o1_ref[:, sl].astype(F32)
                  + w2[:, c] * o2_ref[:, sl].astype(F32))
            ya = oh * silu_a[:, sl]
            ya_ref[:, sl] = ya.astype(BF16)
            yat_ref[sl, :] = ya.T.astype(BF16)
        _pool_tokens(up_ref, uc_ref, p_ref, i)
        zp = zp_ref[...].astype(F32)
        yp = _pool_linear(p_ref[...], wpool_ref) * ps_ref[...] * (zp * _sigmoid(zp))
        yp_ref[...] = yp.astype(BF16)
        ypt_ref[...] = yp.T.astype(BF16)
        a = _nn(ya_ref[...], wpa_ref[...])
        b = _nn(yp_ref[...], wpp_ref[...])
        a_ref[...] = a.astype(BF16)
        b_ref[...] = b.astype(BF16)
        gates = _sigmoid(gp_ref[...].astype(F32) + bg_ref[...])
        mg = gates[:, :D] * a + gates[:, D:] * b
        mg_ref[...] = mg.astype(BF16)
        mgt_ref[...] = mg.T.astype(BF16)

    u_prev = pl.BlockSpec((TM, AW), lambda i: (jnp.maximum(i - 1, 0), COL_U // AW))
    across = lambda width: pl.BlockSpec((width, TM), lambda i: (0, i))
    return pl.pallas_call(
        body, name="mix_fwd", grid=(NT,),
        in_specs=[_tok(AW)] * 3 + [_tok(NH)] * 3
        + [_tok(AW, COL_ZA // AW), _tok(AW, COL_U // AW), u_prev, _tok(AW, COL_ZP // AW), _tok(2 * D, COL_G // (2 * D))]
        + [_whole((AW, D)), _whole((AW, D)), _whole((4, PG, PG)), _whole((1, AW)), _whole((1, 2 * D))],
        out_specs=[_tok(D), _tok(D), _tok(D), _tok(AW), across(AW), across(AW), across(D)],
        out_shape=[jax.ShapeDtypeStruct((S, D), BF16)] * 3 + [jax.ShapeDtypeStruct((S, AW), BF16)]
        + [jax.ShapeDtypeStruct((AW, S), BF16)] * 2 + [jax.ShapeDtypeStruct((D, S), BF16)],
        scratch_shapes=[pltpu.VMEM((TM, AW), BF16), pltpu.VMEM((TM, AW), BF16)],
        compiler_params=_cparams(1))(*o, *lse, h, h, h, h, h, wpa, wpp, wpool, pscale, bgate)


def _out_ln(merged, x, target, wout, gamma, beta):
    def body(mg_ref, x_ref, t_ref, w_ref, g_ref, b_ref, dr_ref, drb_ref, dm_ref, loss_ref, dg_ref, db_ref):
        i = pl.program_id(0)

        @pl.when(i == 0)
        def _():
            loss_ref[...] = jnp.zeros_like(loss_ref)
            dg_ref[...] = jnp.zeros_like(dg_ref)
            db_ref[...] = jnp.zeros_like(db_ref)

        r = ALPHA * x_ref[...] + _nn(mg_ref[...], w_ref[...])
        mu = jnp.mean(r, axis=-1, keepdims=True)
        rc = r - mu
        rstd = lax.rsqrt(jnp.mean(rc * rc, axis=-1, keepdims=True) + LN_EPS)
        xhat = rc * rstd
        err = xhat * g_ref[...] + b_ref[...] - t_ref[...]
        loss_ref[...] += 0.5 * jnp.sum(jnp.mean(err * err, axis=-1, keepdims=True), axis=0, keepdims=True)
        dy = err * (1.0 / D)
        dg_ref[...] += jnp.sum(dy * xhat, axis=0, keepdims=True)
        db_ref[...] += jnp.sum(dy, axis=0, keepdims=True)
        dxh = dy * g_ref[...]
        dr = rstd * (dxh - jnp.mean(dxh, axis=-1, keepdims=True)
                     - xhat * jnp.mean(dxh * xhat, axis=-1, keepdims=True))
        dr_ref[...] = dr
        drb_ref[...] = dr.astype(BF16)
        dm_ref[...] = _nt(drb_ref[...], w_ref[...]).astype(BF16)

    return pl.pallas_call(
        body, name="out_ln", grid=(NT,),
        in_specs=[_tok(D), _tok(D), _tok(D), _whole((D, D)), _whole((1, D)), _whole((1, D))],
        out_specs=[_tok(D), _tok(D), _tok(D), _whole((8, 128)), _whole((1, D)), _whole((1, D))],
        out_shape=[jax.ShapeDtypeStruct((S, D), F32), jax.ShapeDtypeStruct((S, D), BF16),
                   jax.ShapeDtypeStruct((S, D), BF16), jax.ShapeDtypeStruct((8, 128), F32),
                   jax.ShapeDtypeStruct((1, D), F32), jax.ShapeDtypeStruct((1, D), F32)],
        compiler_params=_cparams(1))(merged, x, target, wout, gamma, beta)


def _gate_bwd(dm, a, b, h, bgate):
    def body(dm_ref, a_ref, b_ref, gp_ref, bg_ref, dgp_ref, da_ref, db_ref, dbg_ref):
        @pl.when(pl.program_id(0) == 0)
        def _():
            dbg_ref[...] = jnp.zeros_like(dbg_ref)

        dm_ = dm_ref[...].astype(F32)
        gates = _sigmoid(gp_ref[...].astype(F32) + bg_ref[...])
        ga, gb = gates[:, :D], gates[:, D:]
        da_ref[...] = (dm_ * ga).astype(BF16)
        db_ref[...] = (dm_ * gb).astype(BF16)
        dgp = jnp.concatenate([dm_ * a_ref[...].astype(F32) * ga * (1.0 - ga),
                               dm_ * b_ref[...].astype(F32) * gb * (1.0 - gb)], axis=1)
        dgp_ref[...] = dgp.astype(BF16)
        dbg_ref[...] += jnp.sum(dgp, axis=0, keepdims=True)

    return pl.pallas_call(
        body, name="gate_bwd", grid=(NT,),
        in_specs=[_tok(D), _tok(D), _tok(D), _tok(2 * D, COL_G // (2 * D)), _whole((1, 2 * D))],
        out_specs=[_tok(2 * D, DH_G // (2 * D)), _tok(D), _tok(D), _whole((1, 2 * D))],
        out_shape=[jax.ShapeDtypeStruct((S, NW), BF16), jax.ShapeDtypeStruct((S, D), BF16),
                   jax.ShapeDtypeStruct((S, D), BF16), jax.ShapeDtypeStruct((1, 2 * D), F32)],
        compiler_params=_cparams(1))(dm, a, b, h, bgate)


def _mix_bwd(dh, da, db, h, o, lse, p, wpa, wpp, wpool, pscale):
    def body(_, da_ref, db_ref, o0_ref, o1_ref, o2_ref, l0_ref, l1_ref, l2_ref, za_ref, zp_ref, p_ref,
             wpa_ref, wpp_ref, wpool_ref, ps_ref,
             dh_ref, do0_ref, do1_ref, do2_ref, dl0_ref, dl1_ref, dl2_ref, dwp_ref, dps_ref,
             nxt_ref):
        i = pl.program_id(0)
        tile = NT - 1 - i
        dza_ref, du_ref, dzp_ref = (dh_ref.at[:, pl.ds(n * AW, AW)] for n in range(3))

        @pl.when(i == 0)
        def _():
            nxt_ref[...] = jnp.zeros_like(nxt_ref)
            dwp_ref[...] = jnp.zeros_like(dwp_ref)
            dps_ref[...] = jnp.zeros_like(dps_ref)

        dya = _nt(da_ref[...], wpa_ref[...])
        w0, w1, w2 = _group_weights(l0_ref[...], l1_ref[...], l2_ref[...])
        za = za_ref[...].astype(F32)
        sig = _sigmoid(za)
        silu_a = za * sig
        dsilu_a = sig * (1.0 + za * (1.0 - sig))
        for hh in range(NH):
            sl = slice(hh * HD, (hh + 1) * HD)
            c = slice(hh, hh + 1)
            oh = (w0[:, c] * o0_ref[:, sl].astype(F32) + w1[:, c] * o1_ref[:, sl].astype(F32)
                  + w2[:, c] * o2_ref[:, sl].astype(F32))
            doh = dya[:, sl] * silu_a[:, sl]
            dza_ref[:, sl] = (dya[:, sl] * oh * dsilu_a[:, sl]).astype(BF16)
            dot_ = jnp.sum(doh * oh, axis=-1, keepdims=True)
            do0_ref[:, sl] = (w0[:, c] * doh).astype(BF16)
            do1_ref[:, sl] = (w1[:, c] * doh).astype(BF16)
            do2_ref[:, sl] = (w2[:, c] * doh).astype(BF16)
            dl0_ref[:, c] = w0[:, c] * dot_
            dl1_ref[:, c] = w1[:, c] * dot_
            dl2_ref[:, c] = w2[:, c] * dot_
        dyp = _nt(db_ref[...], wpp_ref[...])
        pb = p_ref[...]
        pw = _pool_linear(pb, wpool_ref)
        zp = zp_ref[...].astype(F32)
        sigp = _sigmoid(zp)
        dypre = dyp * (zp * sigp)
        dzp_ref[...] = (dyp * (pw * ps_ref[...]) * (sigp * (1.0 + zp * (1.0 - sigp)))).astype(BF16)
        dps_ref[...] += jnp.sum(dypre * pw, axis=0, keepdims=True)
        dpw = (dypre * ps_ref[...]).astype(BF16)
        dp = []
        for g in range(len(POOL_WINDOWS)):
            sl = slice(g * PG, (g + 1) * PG)
            dwp_ref[g] += _tn(pb[:, sl], dpw[:, sl])
            dp.append(_nt(dpw[:, sl], wpool_ref[g]))
        _pool_tokens_bwd(dp, nxt_ref, du_ref, tile)

    r = functools.partial(_tok, rev=True)
    return pl.pallas_call(
        body, name="mix_bwd", grid=(NT,),
        in_specs=[ANY, r(D), r(D)] + [r(AW)] * 3 + [r(NH)] * 3 + [r(AW, COL_ZA // AW), r(AW, COL_ZP // AW), r(AW)]
        + [_whole((AW, D)), _whole((AW, D)), _whole((4, PG, PG)), _whole((1, AW))],
        out_specs=[r(3 * AW, DH_Z // (3 * AW))] + [r(AW)] * 3 + [r(NH)] * 3 + [_whole((4, PG, PG)), _whole((1, AW))],
        out_shape=[jax.ShapeDtypeStruct((S, NW), BF16)] + [jax.ShapeDtypeStruct((S, AW), BF16)] * 3
        + [jax.ShapeDtypeStruct((S, NH), F32)] * 3
        + [jax.ShapeDtypeStruct((4, PG, PG), F32), jax.ShapeDtypeStruct((1, AW), F32)],
        input_output_aliases={0: 0},
        scratch_shapes=[pltpu.VMEM((TM, AW), F32)],
        compiler_params=_cparams(1))(dh, da, db, *o, *lse, h, h, p, wpa, wpp, wpool, pscale)


def _adamw(w, g, m, v):
    m = B1 * m + (1.0 - B1) * g
    v = B2 * v + (1.0 - B2) * jnp.square(g)
    m_hat = m / (1.0 - B1 ** STEP)
    v_hat = v / (1.0 - B2 ** STEP)
    return -LR * (m_hat / (jnp.sqrt(v_hat) + EPS) + WD * w), m, v


def _adam_shard(name, q, l2, w, m, v, tr):
    rows = w.shape[0]

    def body(q_ref, l_ref, w_ref, m_ref, v_ref, g_out, d_out, m_out, v_out):
        g = q_ref[...].astype(F32)
        for k in range(2):
            g = g + l_ref[k].astype(F32)
        g_out[...] = g
        d_out[...], m_out[...], v_out[...] = _adamw(w_ref[...], g, m_ref[...], v_ref[...])

    blk = pl.BlockSpec((tr, D), lambda i: (i, 0))
    return pl.pallas_call(
        body, name=name, grid=(rows // tr,),
        in_specs=[pl.BlockSpec((None, tr, D), lambda i: (0, i, 0)), pl.BlockSpec((2, tr, D), lambda i: (0, i, 0)),
                  blk, blk, blk],
        out_specs=[blk] * 4, out_shape=[jax.ShapeDtypeStruct((rows, D), F32)] * 4,
        compiler_params=_cparams(1))(q, l2, w, m, v)


def _sum_small(q, l2):
    def body(q_ref, l_ref, g_out, buf, sems):
        rows = pl.ds(R_OUT, R_SMALL)
        copies = [pltpu.make_async_copy(src, buf.at[n], sems.at[n])
                  for n, src in enumerate((q_ref.at[0, rows], l_ref.at[0, rows], l_ref.at[1, rows]))]
        for cp in copies:
            cp.start()
        for cp in copies:
            cp.wait()
        g_out[...] = buf[0].astype(F32) + buf[1].astype(F32) + buf[2].astype(F32)

    return pl.pallas_call(
        body, name="sum_small", in_specs=[ANY, ANY], out_shape=jax.ShapeDtypeStruct((R_SMALL, D), F32),
        scratch_shapes=[pltpu.VMEM((3, R_SMALL, D), q.dtype), pltpu.SemaphoreType.DMA((3,))],
        compiler_params=pltpu.CompilerParams(vmem_limit_bytes=VMEM_LIMIT))(q, l2)


def _adam_whole(name, grads, weights, ms, vs):
    n = len(grads)

    def body(*refs):
        ins, outs = refs[:4 * n], refs[4 * n:]
        for t in range(n):
            g, w, m, v = (ins[k * n + t][...] for k in range(4))
            outs[t][...], outs[n + t][...], outs[2 * n + t][...] = _adamw(w, g, m, v)

    out = pl.pallas_call(
        body, name=name, out_shape=[jax.ShapeDtypeStruct(w.shape, F32) for w in weights] * 3,
        compiler_params=pltpu.CompilerParams(vmem_limit_bytes=VMEM_LIMIT))(*grads, *weights, *ms, *vs)
    return out[:n], out[n:2 * n], out[2 * n:]


def _sum_replicated(gathered):
    def body(g_ref, bg_out, ps_out, gm_out, bt_out, loss_out):
        g = g_ref[0]
        for k in range(1, N_DEV):
            g = g + g_ref[k]
        bg_out[...] = jnp.concatenate([g[0:1], g[1:2]], axis=1)
        gm_out[...] = g[2:3]
        bt_out[...] = g[3:4]
        ps_out[...] = g[4:5, :AW]
        loss_out[...] = jnp.broadcast_to(g[5:6, :128], loss_out.shape)

    return pl.pallas_call(
        body, name="sum_replicated",
        out_shape=[jax.ShapeDtypeStruct(shape, F32) for shape in ((1, 2 * D), (1, AW), (1, D), (1, D), (8, 128))],
        compiler_params=pltpu.CompilerParams(vmem_limit_bytes=VMEM_LIMIT))(gathered)


def _pack_small(w_out, w_pa, w_pp, w_pool):
    return jnp.concatenate([w_out, w_pa.reshape(-1, D), w_pp.reshape(-1, D), w_pool.reshape(-1, D)], axis=0)


def _unpack_small(a):
    o = R_OUT
    return (a[:R_PA - o], a[R_PA - o:R_PP - o].reshape(AW, 256), a[R_PP - o:R_PL - o].reshape(AW, 256),
            a[R_PL - o:].reshape(4, 32, PG))


def _pack_vec(b_gate, gamma, beta, pscale, extra):
    z = jnp.zeros((D,), F32)
    return jnp.stack([b_gate[:D], b_gate[D:], gamma, beta, jnp.concatenate([pscale, z[:D - AW]]),
                      jnp.broadcast_to(extra, (D,)), z, z])


def kernel(x, w_in, b_gate, w_pool, pool_scale, w_proj_attn, w_proj_pool, w_out, ln_gamma, ln_beta, loss_target, m_w_in, m_b_gate, m_w_pool, m_pool_scale, m_w_proj_attn, m_w_proj_pool, m_w_out, m_ln_gamma, m_ln_beta, v_w_in, v_b_gate, v_w_pool, v_pool_scale, v_w_proj_attn, v_w_proj_pool, v_w_out, v_ln_gamma, v_ln_beta):
    coords = jnp.stack([lax.axis_index("x"), lax.axis_index("y"), lax.axis_index("c")]).astype(jnp.int32)
    x2, tgt = _permute_tokens(x[0]), _permute_tokens(loss_target[0])
    xb, xt = _prep_x(x2)

    pack = jnp.concatenate([w_in[0].astype(BF16),
                            _pack_small(w_out[0], w_proj_attn[0], w_proj_pool[0], w_pool[0]).astype(BF16)], axis=0)
    h, gw = _ag_proj(_arrival_order(*coords), xb, pack)
    wout = gw[:, R_OUT:R_PA].reshape(D, D)
    wpa = gw[:, R_PA:R_PP].reshape(N_DEV, AW, 256).transpose(1, 0, 2).reshape(AW, D)
    wpp = gw[:, R_PP:R_PL].reshape(N_DEV, AW, 256).transpose(1, 0, 2).reshape(AW, D)
    wpool = gw[:, R_PL:].reshape(N_DEV, 4, 32, PG).transpose(1, 0, 2, 3).reshape(4, PG, PG)

    o, lse = zip(*[_attn_fwd(g, h) for g in range(len(DILATIONS))])
    merged, a, b, p, yat, ypt, mgt = _mix_fwd(h, o, lse, wpa, wpp, wpool, pool_scale, b_gate)
    dr, drb, dm, loss_part, dgamma, dbeta = _out_ln(merged, x2, tgt, wout, ln_gamma, ln_beta)

    dh, da, db, dbgate = _gate_bwd(dm, a, b, h, b_gate)
    dh, do0, do1, do2, dl0, dl1, dl2, dwpool, dpscale = _mix_bwd(
        dh, da, db, h, o, lse, p, wpa, wpp, wpool, pool_scale)
    for g, (do_g, dl_g) in enumerate(zip((do0, do1, do2), (dl0, dl1, dl2))):
        dh = _attn_bwd(g, dh, h, do_g, lse[g], dl_g)

    q = _grad_w_in_rs(_rs_columns(*coords), xt, dh)
    flat = lambda n, k: (0, n)
    d_wout = _grad_w("grad_w_out", mgt, drb, jax.ShapeDtypeStruct((D, D), BF16),
                     pl.BlockSpec((D, 1024), flat), 1024, 1024)
    d_wpa = _grad_w("grad_w_pa", yat, da, jax.ShapeDtypeStruct((AW, D), BF16),
                    pl.BlockSpec((AW, 1024), flat), 1024, 1024)
    d_wpp = _grad_w("grad_w_pp", ypt, db, jax.ShapeDtypeStruct((AW, D), BF16),
                    pl.BlockSpec((AW, 1024), flat), 1024, 1024)
    small = jnp.concatenate([
        d_wout.reshape(N_DEV, 256, D),
        d_wpa.reshape(AW, N_DEV, 256).transpose(1, 0, 2).reshape(N_DEV, -1, D),
        d_wpp.reshape(AW, N_DEV, 256).transpose(1, 0, 2).reshape(N_DEV, -1, D),
        dwpool.astype(BF16).reshape(4, N_DEV, 32, PG).transpose(1, 0, 2, 3).reshape(N_DEV, -1, D)], axis=1)
    q = _pair_sum_small(coords, small, _rs_sibling(small), q)

    grad_x, l2 = _grad_x_rs(dh, gw, dr, q)
    g_in, d_in, m_in, v_in = _adam_shard("adam_w_in", q, l2, w_in[0], m_w_in[0], v_w_in[0], 256)
    g_small = [t.reshape(w.shape) for t, w in zip(_unpack_small(_sum_small(q, l2)),
                                                  (w_out, w_proj_attn, w_proj_pool, w_pool))]
    small = (g_small,) + _adam_whole("adam_small", g_small, (w_out, w_proj_attn, w_proj_pool, w_pool),
                                     (m_w_out, m_w_proj_attn, m_w_proj_pool, m_w_pool),
                                     (v_w_out, v_w_proj_attn, v_w_proj_pool, v_w_pool))

    vec = _pack_vec(dbgate[0], dgamma[0], dbeta[0], dpscale[0], loss_part[0, 0])
    *g_vec, loss = _sum_replicated(_all_gather_direct("ag_vec", vec))
    vecs = (g_vec,) + _adam_whole("adam_replicated", g_vec, (b_gate, pool_scale, ln_gamma, ln_beta),
                                  (m_b_gate, m_pool_scale, m_ln_gamma, m_ln_beta),
                                  (v_b_gate, v_pool_scale, v_ln_gamma, v_ln_beta))
    loss = loss[0, 0]

    def leaves(kind, big):
        out, pa, pp, pool = small[kind]
        bg, ps, gm, bt = vecs[kind]
        return [big[None], bg, pool, ps, pa, pp, out, gm, bt]

    return (loss, _permute_tokens(grad_x)[None], *leaves(0, g_in), *leaves(1, d_in), *leaves(2, m_in), *leaves(3, v_in))
```

```python
import functools

import jax
import jax.numpy as jnp
from jax import lax
from jax.experimental import pallas as pl
from jax.experimental.pallas import tpu as pltpu

F32 = jnp.float32
BF16 = jnp.bfloat16

S = 4096
D = 2048
NW = 16384
AW = 1024
HD = 128
NH = 8
QB = 128
NBLK = S // QB
DILATIONS = (1, 4, 16)
POOL_WINDOWS = (2, 4, 8, 16)
PG = 256
N_DEV = 8
COL_Q, COL_K, COL_V = 0, 3 * AW, 6 * AW
COL_ZA, COL_U, COL_ZP, COL_G = 9 * AW, 10 * AW, 11 * AW, 12 * AW
DH_Z, DH_G = COL_ZA, COL_G
ALPHA = 2.0 ** 0.25
LN_EPS = 1e-5
NEG_INF = -1e30
LR, B1, B2, EPS, WD, STEP = 0.001, 0.9, 0.999, 1e-08, 0.01, 10
R_IN, R_OUT, R_PA, R_PP, R_PL = 0, 2048, 2304, 2432, 2560
R_ALL = 2576
R_SMALL = R_ALL - R_OUT
VMEM_LIMIT = 56 * 1024 * 1024
MESH = pl.DeviceIdType.MESH
ANY = pl.BlockSpec(memory_space=pl.ANY)


def _cparams(n_axes):
    return pltpu.CompilerParams(dimension_semantics=("arbitrary",) * n_axes, vmem_limit_bytes=VMEM_LIMIT)


def _sigmoid(z):
    return 0.5 * jnp.tanh(0.5 * z) + 0.5


def _nt(a, b):
    return lax.dot_general(a, b, (((1,), (1,)), ((), ())), preferred_element_type=F32)


def _tn(a, b):
    return lax.dot_general(a, b, (((0,), (0,)), ((), ())), preferred_element_type=F32)


def _nn(a, b):
    return jnp.dot(a, b, preferred_element_type=F32)


def _lin(x, y, c):
    return 4 * x + 2 * y + c


def _flip(v, f):
    return 1 - v if f else v


def _exchange(name, src, plan, *, dst_shape=None, local_dst=None):
    n = len(plan)
    in_place = dst_shape is None
    out_sds = jax.ShapeDtypeStruct(src.shape, src.dtype) if in_place else dst_shape

    def body(src_ref, dst_ref, send_sems, recv_sems, local_sem):
        x, y, c = lax.axis_index("x"), lax.axis_index("y"), lax.axis_index("c")

        def copy(k, sender):
            flip, src_index, dst_index = plan[k]
            sx, sy, sc = sender
            to = (_flip(sx, flip[0]), _flip(sy, flip[1]), _flip(sc, flip[2]))
            s = src_ref if src_index is None else src_ref.at[src_index(sx, sy, sc)]
            return pltpu.make_async_remote_copy(
                src_ref=s, dst_ref=dst_ref.at[dst_index(sx, sy, sc)],
                send_sem=send_sems.at[k], recv_sem=recv_sems.at[k],
                device_id=to, device_id_type=MESH)

        me = (x, y, c)
        if local_dst is not None:
            mine = pltpu.make_async_copy(src_ref, dst_ref.at[local_dst(x, y, c)], local_sem)
            mine.start()
        sends = [copy(k, me) for k in range(n)]
        for cp in sends:
            cp.start()
        for k in range(n):
            flip = plan[k][0]
            copy(k, (_flip(x, flip[0]), _flip(y, flip[1]), _flip(c, flip[2]))).wait_recv()
        for cp in sends:
            cp.wait_send()
        if local_dst is not None:
            mine.wait()

    return pl.pallas_call(
        body, name=name, out_shape=out_sds, in_specs=[ANY], out_specs=ANY,
        input_output_aliases={0: 0} if in_place else {},
        scratch_shapes=[pltpu.SemaphoreType.DMA((n,)), pltpu.SemaphoreType.DMA((n,)),
                        pltpu.SemaphoreType.DMA(())],
    )(src)


FLIP_C, FLIP_X, FLIP_Y, FLIP_XY = (0, 0, 1), (1, 0, 0), (0, 1, 0), (1, 1, 0)
CHIP_FLIPS = ((0, 0), (1, 0), (0, 1), (1, 1))


AG_PIECES = ((pl.ds(R_IN, D), pl.ds(0, 1024)), (pl.ds(R_IN, D), pl.ds(1024, 1024)),
             (pl.ds(R_OUT, R_PA - R_OUT), pl.ds(0, D)), (pl.ds(R_PA, R_ALL - R_PA), pl.ds(0, D)))
N_PIECES = len(AG_PIECES)
SIB, TO_X, TO_Y, ON, PASS_X, PASS_Y, PASS_D = range(7)
AG_TILES = ((0, 0), (0, 1), (1, 0), (1, 1), (2, 0), (4, 0), (3, 0), (5, 0),
            (2, 1), (4, 1), (3, 1), (5, 1), (6, 0), (6, 1), (7, 0), (7, 1))
W, G = "wait", "go"
AG_STEPS = {
    2: [(W, SIB, 0)], 3: [(W, SIB, 1)],
    4: [(W, TO_X, 0), (G, ON, 0), (G, PASS_X, 0)], 5: [(W, TO_Y, 0), (G, PASS_Y, 0)],
    6: [(W, PASS_X, 0)], 7: [(W, PASS_Y, 0)],
    8: [(W, TO_X, 1), (G, PASS_X, 1)],
    9: [(W, TO_Y, 1), (G, ON, 1), (G, PASS_Y, 1), (G, TO_X, 2), (G, TO_X, 3), (G, TO_Y, 2), (G, TO_Y, 3)],
    10: [(W, PASS_X, 1)], 11: [(W, PASS_Y, 1)],
    12: [(W, ON, 0), (G, PASS_D, 0)], 13: [(W, ON, 1), (G, PASS_D, 1)],
    14: [(W, PASS_D, 0), (W, TO_X, 2), (G, ON, 2), (G, PASS_X, 2), (W, TO_X, 3), (G, PASS_X, 3),
         (W, TO_Y, 2), (G, PASS_Y, 2), (W, TO_Y, 3), (G, ON, 3), (G, PASS_Y, 3)],
    15: [(W, PASS_D, 1)],
}
AG_LAST = [(W, SIB, 2), (W, SIB, 3), (W, ON, 2), (G, PASS_D, 2), (W, ON, 3), (G, PASS_D, 3),
           (W, PASS_X, 2), (W, PASS_X, 3), (W, PASS_Y, 2), (W, PASS_Y, 3), (W, PASS_D, 2), (W, PASS_D, 3)]


def _arrival_order(x, y, c):
    chips = [(x, y), (1 - x, y), (x, 1 - y), (1 - x, 1 - y)]
    return jnp.stack([_lin(px, py, pc) for px, py in chips for pc in (c, 1 - c)]).astype(jnp.int32)


def _ag_proj(order, xb, pack):
    tm, tn = 2048, 1024
    nrow, ntile = S // tm, len(AG_TILES)
    slabs = jnp.stack([order[pos] for pos, _ in AG_TILES])
    cols = jnp.stack([2 * order[pos] + half for pos, half in AG_TILES])

    def body(cols_ref, slabs_ref, x_ref, pack_ref, h_ref, gw_ref, wbuf, wsem, send_sems, recv_sems, local_sem):
        t, i = pl.program_id(0), pl.program_id(1)
        x, y, c = lax.axis_index("x"), lax.axis_index("y"), lax.axis_index("c")
        me = _lin(x, y, c)
        dev = {"sib": (x, y, 1 - c), "x": (1 - x, y, c), "y": (x, 1 - y, c), "d": (1 - x, 1 - y, c)}

        def slab_of(name, other_core=False):
            px, py, pc = dev[name]
            return _lin(px, py, 1 - pc if other_core else pc)

        def rdma(slab, kind, piece, to, from_pack=False):
            k = kind * N_PIECES + piece
            there = gw_ref.at[(slab, *AG_PIECES[piece])]
            return pltpu.make_async_remote_copy(
                src_ref=pack_ref.at[AG_PIECES[piece]] if from_pack else there, dst_ref=there,
                send_sem=send_sems.at[k], recv_sem=recv_sems.at[k], device_id=dev[to], device_id_type=MESH)

        def mine(kind, piece):
            if kind in (SIB, TO_X, TO_Y):
                return rdma(me, kind, piece, ("sib", "x", "y")[kind], from_pack=True)
            if kind == ON:
                frm, to = ("x", "y") if piece % 2 == 0 else ("y", "x")
                return rdma(slab_of(frm), kind, piece, to)
            return rdma(slab_of({PASS_X: "x", PASS_Y: "y", PASS_D: "d"}[kind]), kind, piece, "sib")

        def landing(kind, piece):
            slab = {SIB: slab_of("sib"), TO_X: slab_of("x"), TO_Y: slab_of("y"), ON: slab_of("d"),
                    PASS_X: slab_of("x", True), PASS_Y: slab_of("y", True), PASS_D: slab_of("d", True)}[kind]
            return rdma(slab, kind, piece, "sib")

        def run(steps):
            for what, kind, piece in steps:
                if what == W:
                    landing(kind, piece).wait_recv()
                else:
                    mine(kind, piece).start()

        local = pltpu.make_async_copy(pack_ref, gw_ref.at[me], local_sem)

        def fetch(slab, half, slot):
            src = pack_ref.at[AG_PIECES[half]] if slab is None else gw_ref.at[(slab, *AG_PIECES[half])]
            return pltpu.make_async_copy(src, wbuf.at[slot], wsem.at[slot])

        @pl.when((t == 0) & (i == 0))
        def _():
            local.start()
            run([(G, kind, piece) for piece in (0, 1) for kind in (TO_X, TO_Y, SIB)] + [(G, SIB, 2), (G, SIB, 3)])
            first = fetch(None, 0, 0)
            first.start()
            first.wait()

        for nxt in range(1, ntile):
            @pl.when((t == nxt - 1) & (i == nrow - 1))
            def _(nxt=nxt):
                run(AG_STEPS.get(nxt, []))
                fetch(None if AG_TILES[nxt][0] == 0 else slabs_ref[nxt], AG_TILES[nxt][1], nxt % 2).start()

        for slot in (0, 1):
            @pl.when(t % 2 == slot)
            def _(slot=slot):
                @pl.when((i == 0) & (t > 0))
                def _():
                    fetch(None, 0, slot).wait()
                h_ref[...] = _nn(x_ref[...], wbuf[slot]).astype(h_ref.dtype)

        @pl.when((t == ntile - 1) & (i == nrow - 1))
        def _():
            run(AG_LAST)
            for kind in range(7):
                for piece in range(N_PIECES):
                    mine(kind, piece).wait_send()
            local.wait()

    n_sem = 7 * N_PIECES
    grid_spec = pltpu.PrefetchScalarGridSpec(
        num_scalar_prefetch=2, grid=(ntile, nrow),
        in_specs=[pl.BlockSpec((tm, D), lambda t, i, cols, slabs: (i, 0)), ANY],
        out_specs=[pl.BlockSpec((tm, tn), lambda t, i, cols, slabs: (i, cols[t])), ANY],
        scratch_shapes=[pltpu.VMEM((2, D, tn), BF16), pltpu.SemaphoreType.DMA((2,)),
                        pltpu.SemaphoreType.DMA((n_sem,)), pltpu.SemaphoreType.DMA((n_sem,)),
                        pltpu.SemaphoreType.DMA(())])
    return pl.pallas_call(
        body, name="ag_proj", grid_spec=grid_spec,
        out_shape=[jax.ShapeDtypeStruct((S, NW), BF16), jax.ShapeDtypeStruct((N_DEV, R_ALL, D), BF16)],
        compiler_params=_cparams(2))(cols, slabs, xb, pack)


def _all_gather_direct(name, vec):
    own = lambda x, y, c: _lin(x, y, c)
    flips = [(fx, fy, fc) for fx in (0, 1) for fy in (0, 1) for fc in (0, 1) if (fx, fy, fc) != (0, 0, 0)]
    return _exchange(name, vec, [(f, None, own) for f in flips],
                     dst_shape=jax.ShapeDtypeStruct((N_DEV,) + vec.shape, vec.dtype), local_dst=own)


def _rs_sibling(p):
    plan = [(FLIP_C, (lambda x, y, c, f=f: _lin(_flip(x, f[0]), _flip(y, f[1]), 1 - c)),
             (lambda x, y, c, k=k: k)) for k, f in enumerate(CHIP_FLIPS)]
    return _exchange("rs_sibling", p, plan, dst_shape=jax.ShapeDtypeStruct((4,) + p.shape[1:], p.dtype))


def _pair_sum_small(coords, p, l1, q):
    def body(crd, p_ref, l_ref, _, q_ref, buf, sem):
        k = pl.program_id(0)
        buf[...] = (p_ref[...].astype(F32) + l_ref[...].astype(F32)).astype(buf.dtype)
        out = pltpu.make_async_copy(buf, q_ref.at[k, pl.ds(R_OUT, R_SMALL)], sem)
        out.start()
        out.wait()

    def p_map(k, crd):
        fx, fy = k % 2, k // 2
        px = crd[0] + fx - 2 * fx * crd[0]
        py = crd[1] + fy - 2 * fy * crd[1]
        return (_lin(px, py, crd[2]), 0, 0)

    grid_spec = pltpu.PrefetchScalarGridSpec(
        num_scalar_prefetch=1, grid=(4,),
        in_specs=[pl.BlockSpec((None, R_SMALL, D), p_map),
                  pl.BlockSpec((None, R_SMALL, D), lambda k, crd: (k, 0, 0)), ANY],
        out_specs=ANY,
        scratch_shapes=[pltpu.VMEM((R_SMALL, D), BF16), pltpu.SemaphoreType.DMA(())])
    return pl.pallas_call(body, name="pair_sum_small", grid_spec=grid_spec,
                          out_shape=jax.ShapeDtypeStruct(q.shape, q.dtype), input_output_aliases={3: 0},
                          compiler_params=_cparams(1))(coords, p, l1, q)


def _h_block(k):
    return jnp.where(k < 9, (k % 3) * 3 + k // 3, k)


RS_PIECES = (pl.ds(0, 1280), pl.ds(1280, R_ALL - 1280))
RS_ROWS = (1280, R_ALL - 1280)
RS_CHUNKS = ((320,) * 4, (432,) * 3)
RS_MERGE_STEP = 2


def _grad_x_rs(dh, g, dr, q):
    tm, tk = 1024, 1024
    ni, nk = S // tm, NW // tk
    rmax = max(RS_ROWS)
    cmax = max(max(c) for c in RS_CHUNKS)

    def body(dh_ref, w_ref, dr_ref, q_ref, o_ref, l2_ref, ld_ref, mg_ref, va, vb, send_sems, recv_sems, sems):
        i, k = pl.program_id(0), pl.program_id(1)
        x, y, c = lax.axis_index("x"), lax.axis_index("y"), lax.axis_index("c")
        nbr = ((1 - x, y, c), (x, 1 - y, c))

        def rows(ref, piece):
            return ref.at[piece, pl.ds(0, RS_ROWS[piece])]

        copies = (
            (q_ref.at[3, RS_PIECES[0]], rows(ld_ref, 0), 0),
            (q_ref.at[3, RS_PIECES[1]], rows(ld_ref, 1), 1),
            (q_ref.at[1, RS_PIECES[0]], l2_ref.at[0, RS_PIECES[0]], 0),
            (q_ref.at[2, RS_PIECES[1]], l2_ref.at[1, RS_PIECES[1]], 1),
            (rows(mg_ref, 0), l2_ref.at[1, RS_PIECES[0]], 1),
            (rows(mg_ref, 1), l2_ref.at[0, RS_PIECES[1]], 0),
        )

        def copy(n):
            src, dst, axis = copies[n]
            return pltpu.make_async_remote_copy(src_ref=src, dst_ref=dst, send_sem=send_sems.at[n],
                                                recv_sem=recv_sems.at[n], device_id=nbr[axis], device_id_type=MESH)

        def merge(piece, mine):
            start = 0
            for n_rows in RS_CHUNKS[piece]:
                own = pltpu.make_async_copy(q_ref.at[mine, pl.ds(RS_PIECES[piece].start + start, n_rows)],
                                            va.at[pl.ds(0, n_rows)], sems.at[0])
                got = pltpu.make_async_copy(ld_ref.at[piece, pl.ds(start, n_rows)], vb.at[pl.ds(0, n_rows)], sems.at[1])
                own.start()
                got.start()
                own.wait()
                got.wait()
                va[pl.ds(0, n_rows)] = (va[pl.ds(0, n_rows)].astype(F32)
                                        + vb[pl.ds(0, n_rows)].astype(F32)).astype(va.dtype)
                out = pltpu.make_async_copy(va.at[pl.ds(0, n_rows)], mg_ref.at[piece, pl.ds(start, n_rows)], sems.at[2])
                out.start()
                out.wait()
                start += n_rows

        @pl.when((i == 0) & (k == 0))
        def _():
            for n in range(4):
                copy(n).start()

        @pl.when((i == RS_MERGE_STEP) & (k == 0))
        def _():
            copy(0).wait_recv()
            merge(0, 2)
            copy(4).start()
            copy(1).wait_recv()
            merge(1, 1)
            copy(5).start()

        @pl.when(k == 0)
        def _():
            o_ref[...] = ALPHA * dr_ref[...]

        o_ref[...] += _nt(dh_ref[...], w_ref[...])

        @pl.when((i == ni - 1) & (k == nk - 1))
        def _():
            for n in range(2, 6):
                copy(n).wait_recv()
            for n in range(6):
                copy(n).wait_send()

    slab = q.shape[1:]
    out = pl.pallas_call(
        body, name="grad_x_rs", grid=(ni, nk),
        in_specs=[pl.BlockSpec((tm, tk), lambda i, k: (i, k)),
                  pl.BlockSpec((None, D, tk), lambda i, k: (_h_block(k) // 2, 0, _h_block(k) % 2)),
                  pl.BlockSpec((tm, D), lambda i, k: (i, 0)), ANY],
        out_specs=[pl.BlockSpec((tm, D), lambda i, k: (i, 0)), ANY, ANY, ANY],
        out_shape=[jax.ShapeDtypeStruct((S, D), F32), jax.ShapeDtypeStruct((2,) + slab, q.dtype),
                   jax.ShapeDtypeStruct((2, rmax, slab[1]), q.dtype), jax.ShapeDtypeStruct((2, rmax, slab[1]), q.dtype)],
        scratch_shapes=[pltpu.VMEM((cmax, slab[1]), q.dtype), pltpu.VMEM((cmax, slab[1]), q.dtype),
                        pltpu.SemaphoreType.DMA((6,)), pltpu.SemaphoreType.DMA((6,)), pltpu.SemaphoreType.DMA((3,))],
        compiler_params=_cparams(2))(dh, g, dr, q)
    return out[0], out[1]


def _rs_columns(x, y, c):
    out = []
    for core in (1 - c, c):
        for fx, fy in CHIP_FLIPS:
            for half in (0, 1):
                out.append(_h_block(2 * _lin(_flip(x, fx), _flip(y, fy), core) + half))
    return jnp.stack(out).astype(jnp.int32)


def _grad_w_in_rs(cols, xt, dh):
    tn, tk = 1024, 1024
    nk = S // tk
    n_half = 8

    def body(cols_ref, a_ref, b_ref, q_ref, l1_ref, acc_ref, stage, landed, send_sems, recv_sems, sem):
        t, k = pl.program_id(0), pl.program_id(1)
        sib = (lax.axis_index("x"), lax.axis_index("y"), 1 - lax.axis_index("c"))

        @pl.when(k == 0)
        def _():
            acc_ref[...] = jnp.zeros_like(acc_ref)

        acc_ref[...] += _nn(a_ref[...], b_ref[...])

        def there(n):
            return l1_ref.at[n // 2, :, pl.ds((n % 2) * tn, tn)]

        def send(n):
            return pltpu.make_async_remote_copy(src_ref=stage.at[n % 2], dst_ref=there(n), send_sem=send_sems.at[n],
                                                recv_sem=recv_sems.at[n], device_id=sib, device_id_type=MESH)

        for n in range(n_half):
            @pl.when((t == n) & (k == nk - 1))
            def _(n=n):
                if n >= 2:
                    send(n - 2).wait_send()
                stage[n % 2] = acc_ref[...].astype(stage.dtype)
                send(n).start()

        def fetch(n):
            return pltpu.make_async_copy(there(n), landed, sem)

        for n in range(n_half):
            @pl.when((t == n_half + n) & (k == nk - 2))
            def _(n=n):
                if n == 0:
                    send(n_half - 2).wait_send()
                    send(n_half - 1).wait_send()
                send(n).wait_recv()
                fetch(n).start()

            @pl.when((t == n_half + n) & (k == nk - 1))
            def _(n=n):
                fetch(n).wait()
                q_ref[...] = (acc_ref[...] + landed[...].astype(F32)).astype(q_ref.dtype)

    mine = lambda t: jnp.maximum(t - n_half, 0)
    grid_spec = pltpu.PrefetchScalarGridSpec(
        num_scalar_prefetch=1, grid=(2 * n_half, nk),
        in_specs=[pl.BlockSpec((D, tk), lambda t, k, cols: (0, k)),
                  pl.BlockSpec((tk, tn), lambda t, k, cols: (k, cols[t]))],
        out_specs=[pl.BlockSpec((None, D, tn), lambda t, k, cols: (mine(t) // 2, 0, mine(t) % 2)), ANY],
        scratch_shapes=[pltpu.VMEM((D, tn), F32), pltpu.VMEM((2, D, tn), BF16), pltpu.VMEM((D, tn), BF16),
                        pltpu.SemaphoreType.DMA((n_half,)), pltpu.SemaphoreType.DMA((n_half,)),
                        pltpu.SemaphoreType.DMA(())])
    q, _ = pl.pallas_call(
        body, name="grad_w_in_rs", grid_spec=grid_spec,
        out_shape=[jax.ShapeDtypeStruct((4, R_ALL, D), BF16), jax.ShapeDtypeStruct((4, D, D), BF16)],
        compiler_params=_cparams(2))(cols, xt, dh)
    return q


def _grad_w(name, at, b, out_shape, out_spec, tn, tk):
    m, k_all = at.shape
    n_all = b.shape[1]
    nk = k_all // tk

    def body(a_ref, b_ref, o_ref, acc_ref):
        k = pl.program_id(1)

        @pl.when(k == 0)
        def _():
            acc_ref[...] = jnp.zeros_like(acc_ref)

        acc_ref[...] += _nn(a_ref[...], b_ref[...])

        @pl.when(k == nk - 1)
        def _():
            o_ref[...] = acc_ref[...].astype(o_ref.dtype)

    return pl.pallas_call(
        body, name=name, grid=(n_all // tn, nk),
        in_specs=[pl.BlockSpec((m, tk), lambda n, k: (0, k)),
                  pl.BlockSpec((tk, tn), lambda n, k: (k, n))],
        out_specs=out_spec, out_shape=out_shape,
        scratch_shapes=[pltpu.VMEM((m, tn), F32)], compiler_params=_cparams(2))(at, b)


NR = 16
TI = 16
TM = NR * TI
NT = S // TM
ATT_QB = (256, 128, 256)
ATT_NB = (16, 8, 1)
ATT_BLOCKS = (16, 32, 16)


def _permute_tokens(a):
    return a.reshape(NT, TI, NR, a.shape[-1]).transpose(0, 2, 1, 3).reshape(a.shape)


def _attn_shape(g, c):
    if g == 0:
        return (S, c)
    if g == 1:
        return (NT, 4, 4, TI, c)
    return (NT, NR, TI, c)


def _attn_view(g, a):
    return a.reshape(_attn_shape(g, a.shape[-1]))


def _attn_spec(g, width, col, blk):
    if g == 0:
        return pl.BlockSpec((TM, width), lambda b: (blk(b), col))
    if g == 1:
        return pl.BlockSpec((2, 4, None, TI, width), lambda b: (blk(b) % 8, 0, blk(b) // 8, 0, col))
    return pl.BlockSpec((NT, None, TI, width), lambda b: (0, blk(b), 0, col))


def _pieces(g):
    if g == 1:
        return [(t, m) for t in range(2) for m in range(4)]
    return [(t,) for t in range(NT)]


def _get(g, ref, sl):
    if g == 0:
        return ref[:, sl]
    return jnp.concatenate([ref[(*p, slice(None), sl)] for p in _pieces(g)], axis=0)


def _put(g, ref, sl, val):
    if g == 0:
        ref[:, sl] = val
    else:
        for n, p in enumerate(_pieces(g)):
            ref[(*p, slice(None), sl)] = val[TI * n:TI * (n + 1)]


def _block_pos(g, a):
    if g == 0:
        return 16 * (a % 16) + a // 16
    if g == 1:
        return 64 * (a // 64) + 4 * (a % 16) + (a // 16) % 4
    return a


def _attn_mask(g, n):
    qb = ATT_QB[g]
    if ATT_NB[g] == 1:
        qa = lax.broadcasted_iota(jnp.int32, (qb, qb), 0)
        kc = lax.broadcasted_iota(jnp.int32, (qb, qb), 1)
        dist = _block_pos(g, qa) - _block_pos(g, kc)
        return (dist >= 0) & (dist <= QB)
    qa = lax.broadcasted_iota(jnp.int32, (qb, 2 * qb), 0)
    kc = lax.broadcasted_iota(jnp.int32, (qb, 2 * qb), 1)
    cur = kc >= qb
    dist = _block_pos(g, qa) - _block_pos(g, kc % qb) + jnp.where(cur, 0, qb)
    return (dist >= 0) & (dist <= QB) & (cur | (n > 0))


def _keys(g, prev_ref, cur_ref, sl):
    if ATT_NB[g] == 1:
        return _get(g, cur_ref, sl)
    return jnp.concatenate([_get(g, prev_ref, sl), _get(g, cur_ref, sl)], axis=0)


def _qkv_specs(g, clamp):
    cur = lambda col: _attn_spec(g, AW, col, clamp)
    prev = lambda col: _attn_spec(g, AW, col, lambda b: jnp.maximum(clamp(b) - 1, 0))
    qc, kc, vc = (c // AW + g for c in (COL_Q, COL_K, COL_V))
    return [cur(qc), cur(kc), prev(kc), cur(vc), prev(vc)]


def _attn_fwd(g, h):
    scale = HD ** -0.5
    hv = _attn_view(g, h)

    def body(q_ref, kc_ref, kp_ref, vc_ref, vp_ref, o_ref, l_ref):
        valid = _attn_mask(g, pl.program_id(0) % ATT_NB[g])
        for hh in range(NH):
            sl = slice(hh * HD, (hh + 1) * HD)
            kh, vh = _keys(g, kp_ref, kc_ref, sl), _keys(g, vp_ref, vc_ref, sl)
            s = jnp.where(valid, _nt(_get(g, q_ref, sl), kh) * scale, NEG_INF)
            m = jnp.max(s, axis=-1, keepdims=True)
            e = jnp.exp(s - m)
            den = jnp.sum(e, axis=-1, keepdims=True)
            _put(g, o_ref, sl, (_nn(e.astype(BF16), vh) * (1.0 / den)).astype(o_ref.dtype))
            _put(g, l_ref, slice(hh, hh + 1), m + jnp.log(den))

    same = lambda b: b
    o, lse = pl.pallas_call(
        body, name=f"attn_fwd_{g}", grid=(ATT_BLOCKS[g],),
        in_specs=_qkv_specs(g, same),
        out_specs=[_attn_spec(g, AW, 0, same), _attn_spec(g, NH, 0, same)],
        out_shape=[jax.ShapeDtypeStruct(_attn_shape(g, AW), BF16), jax.ShapeDtypeStruct(_attn_shape(g, NH), F32)],
        compiler_params=_cparams(1))(hv, hv, hv, hv, hv)
    return o.reshape(S, AW), lse.reshape(S, NH)


def _attn_bwd(g, dh, h, do, lse, delta):
    scale = HD ** -0.5
    qb = ATT_QB[g]
    carried = ATT_NB[g] > 1
    last = ATT_BLOCKS[g] - 1
    clamp = lambda b: jnp.minimum(b, last)
    behind = lambda b: jnp.maximum(b - 1, 0)
    hv = _attn_view(g, h)

    def body(q_ref, kc_ref, kp_ref, vc_ref, vp_ref, do_ref, l_ref, dl_ref, _, dh_ref, *carry):
        b = pl.program_id(0)

        def write(col, val):
            _put(g, dh_ref, slice(col, col + HD), val.astype(dh_ref.dtype))

        def block():
            valid = _attn_mask(g, b % ATT_NB[g])
            for hh in range(NH):
                sl = slice(hh * HD, (hh + 1) * HD)
                one = slice(hh, hh + 1)
                qh, doh = _get(g, q_ref, sl), _get(g, do_ref, sl)
                kh, vh = _keys(g, kp_ref, kc_ref, sl), _keys(g, vp_ref, vc_ref, sl)
                s = _nt(qh, kh) * scale
                p = jnp.where(valid, jnp.exp(s - _get(g, l_ref, one)), 0.0)
                ds = p * (_nt(doh, vh) - _get(g, dl_ref, one))
                dsb = (ds * scale).astype(BF16)
                dq = _nn(dsb, kh)
                dk2 = _tn(dsb, qh)
                dv2 = _tn(p.astype(BF16), doh)
                if carried:
                    cq_ref, ck_ref, cv_ref = carry
                    write(hh * HD, cq_ref[:, sl])
                    write(AW + hh * HD, ck_ref[:, sl] + dk2[:qb])
                    write(2 * AW + hh * HD, cv_ref[:, sl] + dv2[:qb])
                    cq_ref[:, sl] = dq
                    ck_ref[:, sl] = dk2[qb:]
                    cv_ref[:, sl] = dv2[qb:]
                else:
                    write(hh * HD, dq)
                    write(AW + hh * HD, dk2)
                    write(2 * AW + hh * HD, dv2)

        if not carried:
            block()
            return

        @pl.when(b == 0)
        def _():
            for ref in carry:
                ref[...] = jnp.zeros_like(ref)

        pl.when(b <= last)(block)

        @pl.when(b > last)
        def _():
            for hh in range(NH):
                for n, ref in enumerate(carry):
                    write(n * AW + hh * HD, ref[:, hh * HD:(hh + 1) * HD])

    out = pl.pallas_call(
        body, name=f"attn_bwd_{g}", grid=(ATT_BLOCKS[g] + carried,),
        in_specs=_qkv_specs(g, clamp) + [_attn_spec(g, AW, 0, clamp), _attn_spec(g, NH, 0, clamp),
                                         _attn_spec(g, NH, 0, clamp), ANY],
        out_specs=_attn_spec(g, 3 * AW, g, behind if carried else clamp),
        out_shape=jax.ShapeDtypeStruct(_attn_shape(g, NW), BF16),
        input_output_aliases={8: 0},
        scratch_shapes=[pltpu.VMEM((qb, AW), F32)] * (3 if carried else 0),
        compiler_params=_cparams(1))(hv, hv, hv, hv, hv, _attn_view(g, do), _attn_view(g, lse), _attn_view(g, delta),
                                     _attn_view(g, dh))
    return out.reshape(S, NW)


def _group_weights(l0, l1, l2):
    m = jnp.maximum(jnp.maximum(l0, l1), l2)
    e0, e1, e2 = jnp.exp(l0 - m), jnp.exp(l1 - m), jnp.exp(l2 - m)
    inv = 1.0 / (e0 + e1 + e2)
    return e0 * inv, e1 * inv, e2 * inv


def _residue(ref, r, sl):
    return ref[r * TI:(r + 1) * TI, sl].astype(F32)


def _total(parts):
    return functools.reduce(lambda x, y: x + y, parts)


def _pool_tokens(up_ref, uc_ref, p_ref, tile):
    j0 = lax.broadcasted_iota(jnp.int32, (TI, 1), 0) == 0
    first = (tile == 0) & j0
    for r in range(NR):
        out = []
        for g, w in enumerate(POOL_WINDOWS):
            sl = slice(g * PG, (g + 1) * PG)
            own = _residue(uc_ref, r, sl)
            acc = _total([own] + [_residue(uc_ref, r - k, sl) for k in range(1, min(r, w - 1) + 1)])
            wrapped = [NR + r - k for k in range(r + 1, w)]
            if wrapped:
                wc = _total([_residue(uc_ref, q, sl) for q in wrapped])
                wp = jnp.where(tile > 0, _total([_residue(up_ref, q, sl) for q in wrapped]), 0.0)
                acc = acc + jnp.where(j0, pltpu.roll(wp, 1, 0), pltpu.roll(wc, 1, 0))
            out.append(acc * jnp.where(first, 1.0 / min(r + 1, w), 1.0 / w) - own)
        p_ref[r * TI:(r + 1) * TI, :] = jnp.concatenate(out, axis=1).astype(p_ref.dtype)


def _pool_tokens_bwd(dp, nxt_ref, du_ref, tile):
    ji = lax.broadcasted_iota(jnp.int32, (TI, 1), 0)
    first = (tile == 0) & (ji == 0)
    piece = lambda g, r: dp[g][r * TI:(r + 1) * TI]
    dpc = [[piece(g, r) * jnp.where(first, 1.0 / min(r + 1, w), 1.0 / w) for r in range(NR)]
           for g, w in enumerate(POOL_WINDOWS)]
    for r in range(NR):
        out = []
        for g, w in enumerate(POOL_WINDOWS):
            sl = slice(g * PG, (g + 1) * PG)
            acc = _total([dpc[g][r + k] for k in range(w) if r + k < NR])
            wrapped = [r + k - NR for k in range(1, w) if r + k >= NR]
            if wrapped:
                wc = _total([dpc[g][q] for q in wrapped])
                wn = _total([nxt_ref[q * TI:(q + 1) * TI, sl] for q in wrapped])
                acc = acc + jnp.where(ji == TI - 1, pltpu.roll(wn, TI - 1, 0), pltpu.roll(wc, TI - 1, 0))
            out.append(acc - piece(g, r))
        du_ref[r * TI:(r + 1) * TI, :] = jnp.concatenate(out, axis=1).astype(du_ref.dtype)
    for r in range(NR):
        nxt_ref[r * TI:(r + 1) * TI, :] = jnp.concatenate([dpc[g][r] for g in range(len(POOL_WINDOWS))], axis=1)


def _pool_linear(pb, wpool_ref):
    return jnp.concatenate([_nn(pb[:, g * PG:(g + 1) * PG], wpool_ref[g]) for g in range(len(POOL_WINDOWS))], axis=1)


def _tok(width, col=0, rev=False):
    if rev:
        return pl.BlockSpec((TM, width), lambda i: (NT - 1 - i, col))
    return pl.BlockSpec((TM, width), lambda i: (i, col))


def _whole(shape):
    return pl.BlockSpec(shape, lambda i: (0,) * len(shape))


def _prep_x(x2):
    rows = 1024

    def body(x_ref, xb_ref, xt_ref):
        xb_ref[...] = x_ref[...].astype(BF16)
        xt_ref[...] = x_ref[...].T.astype(BF16)

    return pl.pallas_call(
        body, name="prep_x", grid=(S // rows,), in_specs=[pl.BlockSpec((rows, D), lambda i: (i, 0))],
        out_specs=[pl.BlockSpec((rows, D), lambda i: (i, 0)), pl.BlockSpec((D, rows), lambda i: (0, i))],
        out_shape=[jax.ShapeDtypeStruct((S, D), BF16), jax.ShapeDtypeStruct((D, S), BF16)],
        compiler_params=_cparams(1))(x2)


def _mix_fwd(h, o, lse, wpa, wpp, wpool, pscale, bgate):
    def body(o0_ref, o1_ref, o2_ref, l0_ref, l1_ref, l2_ref, za_ref, uc_ref, up_ref, zp_ref, gp_ref,
             wpa_ref, wpp_ref, wpool_ref, ps_ref, bg_ref,
             mg_ref, a_ref, b_ref, p_ref, yat_ref, ypt_ref, mgt_ref, ya_ref, yp_ref):
        i = pl.program_id(0)
        w0, w1, w2 = _group_weights(l0_ref[...], l1_ref[...], l2_ref[...])
        za = za_ref[...].astype(F32)
        silu_a = za * _sigmoid(za)
        for hh in range(NH):
            sl = slice(hh * HD, (hh + 1) * HD)
            c = slice(hh, hh + 1)
            oh = (w0[:, c] * o0_ref[:, sl].astype(F32) + w1[:, c] * o1_ref[:, sl].astype(F32)
                  + w2[:, c] * o2_ref[:, sl].astype(F32))
            ya = oh * silu_a[:, sl]
            ya_ref[:, sl] = ya.astype(BF16)
            yat_ref[sl, :] = ya.T.astype(BF16)
        _pool_tokens(up_ref, uc_ref, p_ref, i)
        zp = zp_ref[...].astype(F32)
        yp = _pool_linear(p_ref[...], wpool_ref) * ps_ref[...] * (zp * _sigmoid(zp))
        yp_ref[...] = yp.astype(BF16)
        ypt_ref[...] = yp.T.astype(BF16)
        a = _nn(ya_ref[...], wpa_ref[...])
        b = _nn(yp_ref[...], wpp_ref[...])
        a_ref[...] = a.astype(BF16)
        b_ref[...] = b.astype(BF16)
        gates = _sigmoid(gp_ref[...].astype(F32) + bg_ref[...])
        mg = gates[:, :D] * a + gates[:, D:] * b
        mg_ref[...] = mg.astype(BF16)
        mgt_ref[...] = mg.T.astype(BF16)

    u_prev = pl.BlockSpec((TM, AW), lambda i: (jnp.maximum(i - 1, 0), COL_U // AW))
    across = lambda width: pl.BlockSpec((width, TM), lambda i: (0, i))
    return pl.pallas_call(
        body, name="mix_fwd", grid=(NT,),
        in_specs=[_tok(AW)] * 3 + [_tok(NH)] * 3
        + [_tok(AW, COL_ZA // AW), _tok(AW, COL_U // AW), u_prev, _tok(AW, COL_ZP // AW), _tok(2 * D, COL_G // (2 * D))]
        + [_whole((AW, D)), _whole((AW, D)), _whole((4, PG, PG)), _whole((1, AW)), _whole((1, 2 * D))],
        out_specs=[_tok(D), _tok(D), _tok(D), _tok(AW), across(AW), across(AW), across(D)],
        out_shape=[jax.ShapeDtypeStruct((S, D), BF16)] * 3 + [jax.ShapeDtypeStruct((S, AW), BF16)]
        + [jax.ShapeDtypeStruct((AW, S), BF16)] * 2 + [jax.ShapeDtypeStruct((D, S), BF16)],
        scratch_shapes=[pltpu.VMEM((TM, AW), BF16), pltpu.VMEM((TM, AW), BF16)],
        compiler_params=_cparams(1))(*o, *lse, h, h, h, h, h, wpa, wpp, wpool, pscale, bgate)


def _out_ln(merged, x, target, wout, gamma, beta):
    def body(mg_ref, x_ref, t_ref, w_ref, g_ref, b_ref, dr_ref, drb_ref, dm_ref, loss_ref, dg_ref, db_ref):
        i = pl.program_id(0)

        @pl.when(i == 0)
        def _():
            loss_ref[...] = jnp.zeros_like(loss_ref)
            dg_ref[...] = jnp.zeros_like(dg_ref)
            db_ref[...] = jnp.zeros_like(db_ref)

        r = ALPHA * x_ref[...] + _nn(mg_ref[...], w_ref[...])
        mu = jnp.mean(r, axis=-1, keepdims=True)
        rc = r - mu
        rstd = lax.rsqrt(jnp.mean(rc * rc, axis=-1, keepdims=True) + LN_EPS)
        xhat = rc * rstd
        err = xhat * g_ref[...] + b_ref[...] - t_ref[...]
        loss_ref[...] += 0.5 * jnp.sum(jnp.mean(err * err, axis=-1, keepdims=True), axis=0, keepdims=True)
        dy = err * (1.0 / D)
        dg_ref[...] += jnp.sum(dy * xhat, axis=0, keepdims=True)
        db_ref[...] += jnp.sum(dy, axis=0, keepdims=True)
        dxh = dy * g_ref[...]
        dr = rstd * (dxh - jnp.mean(dxh, axis=-1, keepdims=True)
                     - xhat * jnp.mean(dxh * xhat, axis=-1, keepdims=True))
        dr_ref[...] = dr
        drb_ref[...] = dr.astype(BF16)
        dm_ref[...] = _nt(drb_ref[...], w_ref[...]).astype(BF16)

    return pl.pallas_call(
        body, name="out_ln", grid=(NT,),
        in_specs=[_tok(D), _tok(D), _tok(D), _whole((D, D)), _whole((1, D)), _whole((1, D))],
        out_specs=[_tok(D), _tok(D), _tok(D), _whole((8, 128)), _whole((1, D)), _whole((1, D))],
        out_shape=[jax.ShapeDtypeStruct((S, D), F32), jax.ShapeDtypeStruct((S, D), BF16),
                   jax.ShapeDtypeStruct((S, D), BF16), jax.ShapeDtypeStruct((8, 128), F32),
                   jax.ShapeDtypeStruct((1, D), F32), jax.ShapeDtypeStruct((1, D), F32)],
        compiler_params=_cparams(1))(merged, x, target, wout, gamma, beta)


def _gate_bwd(dm, a, b, h, bgate):
    def body(dm_ref, a_ref, b_ref, gp_ref, bg_ref, dgp_ref, da_ref, db_ref, dbg_ref):
        @pl.when(pl.program_id(0) == 0)
        def _():
            dbg_ref[...] = jnp.zeros_like(dbg_ref)

        dm_ = dm_ref[...].astype(F32)
        gates = _sigmoid(gp_ref[...].astype(F32) + bg_ref[...])
        ga, gb = gates[:, :D], gates[:, D:]
        da_ref[...] = (dm_ * ga).astype(BF16)
        db_ref[...] = (dm_ * gb).astype(BF16)
        dgp = jnp.concatenate([dm_ * a_ref[...].astype(F32) * ga * (1.0 - ga),
                               dm_ * b_ref[...].astype(F32) * gb * (1.0 - gb)], axis=1)
        dgp_ref[...] = dgp.astype(BF16)
        dbg_ref[...] += jnp.sum(dgp, axis=0, keepdims=True)

    return pl.pallas_call(
        body, name="gate_bwd", grid=(NT,),
        in_specs=[_tok(D), _tok(D), _tok(D), _tok(2 * D, COL_G // (2 * D)), _whole((1, 2 * D))],
        out_specs=[_tok(2 * D, DH_G // (2 * D)), _tok(D), _tok(D), _whole((1, 2 * D))],
        out_shape=[jax.ShapeDtypeStruct((S, NW), BF16), jax.ShapeDtypeStruct((S, D), BF16),
                   jax.ShapeDtypeStruct((S, D), BF16), jax.ShapeDtypeStruct((1, 2 * D), F32)],
        compiler_params=_cparams(1))(dm, a, b, h, bgate)


def _mix_bwd(dh, da, db, h, o, lse, p, wpa, wpp, wpool, pscale):
    def body(_, da_ref, db_ref, o0_ref, o1_ref, o2_ref, l0_ref, l1_ref, l2_ref, za_ref, zp_ref, p_ref,
             wpa_ref, wpp_ref, wpool_ref, ps_ref,
             dh_ref, do0_ref, do1_ref, do2_ref, dl0_ref, dl1_ref, dl2_ref, dwp_ref, dps_ref,
             nxt_ref):
        i = pl.program_id(0)
        tile = NT - 1 - i
        dza_ref, du_ref, dzp_ref = (dh_ref.at[:, pl.ds(n * AW, AW)] for n in range(3))

        @pl.when(i == 0)
        def _():
            nxt_ref[...] = jnp.zeros_like(nxt_ref)
            dwp_ref[...] = jnp.zeros_like(dwp_ref)
            dps_ref[...] = jnp.zeros_like(dps_ref)

        dya = _nt(da_ref[...], wpa_ref[...])
        w0, w1, w2 = _group_weights(l0_ref[...], l1_ref[...], l2_ref[...])
        za = za_ref[...].astype(F32)
        sig = _sigmoid(za)
        silu_a = za * sig
        dsilu_a = sig * (1.0 + za * (1.0 - sig))
        for hh in range(NH):
            sl = slice(hh * HD, (hh + 1) * HD)
            c = slice(hh, hh + 1)
            oh = (w0[:, c] * o0_ref[:, sl].astype(F32) + w1[:, c] * o1_ref[:, sl].astype(F32)
                  + w2[:, c] * o2_ref[:, sl].astype(F32))
            doh = dya[:, sl] * silu_a[:, sl]
            dza_ref[:, sl] = (dya[:, sl] * oh * dsilu_a[:, sl]).astype(BF16)
            dot_ = jnp.sum(doh * oh, axis=-1, keepdims=True)
            do0_ref[:, sl] = (w0[:, c] * doh).astype(BF16)
            do1_ref[:, sl] = (w1[:, c] * doh).astype(BF16)
            do2_ref[:, sl] = (w2[:, c] * doh).astype(BF16)
            dl0_ref[:, c] = w0[:, c] * dot_
            dl1_ref[:, c] = w1[:, c] * dot_
            dl2_ref[:, c] = w2[:, c] * dot_
        dyp = _nt(db_ref[...], wpp_ref[...])
        pb = p_ref[...]
        pw = _pool_linear(pb, wpool_ref)
        zp = zp_ref[...].astype(F32)
        sigp = _sigmoid(zp)
        dypre = dyp * (zp * sigp)
        dzp_ref[...] = (dyp * (pw * ps_ref[...]) * (sigp * (1.0 + zp * (1.0 - sigp)))).astype(BF16)
        dps_ref[...] += jnp.sum(dypre * pw, axis=0, keepdims=True)
        dpw = (dypre * ps_ref[...]).astype(BF16)
        dp = []
        for g in range(len(POOL_WINDOWS)):
            sl = slice(g * PG, (g + 1) * PG)
            dwp_ref[g] += _tn(pb[:, sl], dpw[:, sl])
            dp.append(_nt(dpw[:, sl], wpool_ref[g]))
        _pool_tokens_bwd(dp, nxt_ref, du_ref, tile)

    r = functools.partial(_tok, rev=True)
    return pl.pallas_call(
        body, name="mix_bwd", grid=(NT,),
        in_specs=[ANY, r(D), r(D)] + [r(AW)] * 3 + [r(NH)] * 3 + [r(AW, COL_ZA // AW), r(AW, COL_ZP // AW), r(AW)]
        + [_whole((AW, D)), _whole((AW, D)), _whole((4, PG, PG)), _whole((1, AW))],
        out_specs=[r(3 * AW, DH_Z // (3 * AW))] + [r(AW)] * 3 + [r(NH)] * 3 + [_whole((4, PG, PG)), _whole((1, AW))],
        out_shape=[jax.ShapeDtypeStruct((S, NW), BF16)] + [jax.ShapeDtypeStruct((S, AW), BF16)] * 3
        + [jax.ShapeDtypeStruct((S, NH), F32)] * 3
        + [jax.ShapeDtypeStruct((4, PG, PG), F32), jax.ShapeDtypeStruct((1, AW), F32)],
        input_output_aliases={0: 0},
        scratch_shapes=[pltpu.VMEM((TM, AW), F32)],
        compiler_params=_cparams(1))(dh, da, db, *o, *lse, h, h, p, wpa, wpp, wpool, pscale)


def _adamw(w, g, m, v):
    m = B1 * m + (1.0 - B1) * g
    v = B2 * v + (1.0 - B2) * jnp.square(g)
    m_hat = m / (1.0 - B1 ** STEP)
    v_hat = v / (1.0 - B2 ** STEP)
    return -LR * (m_hat / (jnp.sqrt(v_hat) + EPS) + WD * w), m, v


def _adam_shard(name, q, l2, w, m, v, tr):
    rows = w.shape[0]

    def body(q_ref, l_ref, w_ref, m_ref, v_ref, g_out, d_out, m_out, v_out):
        g = q_ref[...].astype(F32)
        for k in range(2):
            g = g + l_ref[k].astype(F32)
        g_out[...] = g
        d_out[...], m_out[...], v_out[...] = _adamw(w_ref[...], g, m_ref[...], v_ref[...])

    blk = pl.BlockSpec((tr, D), lambda i: (i, 0))
    return pl.pallas_call(
        body, name=name, grid=(rows // tr,),
        in_specs=[pl.BlockSpec((None, tr, D), lambda i: (0, i, 0)), pl.BlockSpec((2, tr, D), lambda i: (0, i, 0)),
                  blk, blk, blk],
        out_specs=[blk] * 4, out_shape=[jax.ShapeDtypeStruct((rows, D), F32)] * 4,
        compiler_params=_cparams(1))(q, l2, w, m, v)


def _sum_small(q, l2):
    def body(q_ref, l_ref, g_out, buf, sems):
        rows = pl.ds(R_OUT, R_SMALL)
        copies = [pltpu.make_async_copy(src, buf.at[n], sems.at[n])
                  for n, src in enumerate((q_ref.at[0, rows], l_ref.at[0, rows], l_ref.at[1, rows]))]
        for cp in copies:
            cp.start()
        for cp in copies:
            cp.wait()
        g_out[...] = buf[0].astype(F32) + buf[1].astype(F32) + buf[2].astype(F32)

    return pl.pallas_call(
        body, name="sum_small", in_specs=[ANY, ANY], out_shape=jax.ShapeDtypeStruct((R_SMALL, D), F32),
        scratch_shapes=[pltpu.VMEM((3, R_SMALL, D), q.dtype), pltpu.SemaphoreType.DMA((3,))],
        compiler_params=pltpu.CompilerParams(vmem_limit_bytes=VMEM_LIMIT))(q, l2)


def _adam_whole(name, grads, weights, ms, vs):
    n = len(grads)

    def body(*refs):
        ins, outs = refs[:4 * n], refs[4 * n:]
        for t in range(n):
            g, w, m, v = (ins[k * n + t][...] for k in range(4))
            outs[t][...], outs[n + t][...], outs[2 * n + t][...] = _adamw(w, g, m, v)

    out = pl.pallas_call(
        body, name=name, out_shape=[jax.ShapeDtypeStruct(w.shape, F32) for w in weights] * 3,
        compiler_params=pltpu.CompilerParams(vmem_limit_bytes=VMEM_LIMIT))(*grads, *weights, *ms, *vs)
    return out[:n], out[n:2 * n], out[2 * n:]


def _sum_replicated(gathered):
    def body(g_ref, bg_out, ps_out, gm_out, bt_out, loss_out):
        g = g_ref[0]
        for k in range(1, N_DEV):
            g = g + g_ref[k]
        bg_out[...] = jnp.concatenate([g[0:1], g[1:2]], axis=1)
        gm_out[...] = g[2:3]
        bt_out[...] = g[3:4]
        ps_out[...] = g[4:5, :AW]
        loss_out[...] = jnp.broadcast_to(g[5:6, :128], loss_out.shape)

    return pl.pallas_call(
        body, name="sum_replicated",
        out_shape=[jax.ShapeDtypeStruct(shape, F32) for shape in ((1, 2 * D), (1, AW), (1, D), (1, D), (8, 128))],
        compiler_params=pltpu.CompilerParams(vmem_limit_bytes=VMEM_LIMIT))(gathered)


def _pack_small(w_out, w_pa, w_pp, w_pool):
    return jnp.concatenate([w_out, w_pa.reshape(-1, D), w_pp.reshape(-1, D), w_pool.reshape(-1, D)], axis=0)


def _unpack_small(a):
    o = R_OUT
    return (a[:R_PA - o], a[R_PA - o:R_PP - o].reshape(AW, 256), a[R_PP - o:R_PL - o].reshape(AW, 256),
            a[R_PL - o:].reshape(4, 32, PG))


def _pack_vec(b_gate, gamma, beta, pscale, extra):
    z = jnp.zeros((D,), F32)
    return jnp.stack([b_gate[:D], b_gate[D:], gamma, beta, jnp.concatenate([pscale, z[:D - AW]]),
                      jnp.broadcast_to(extra, (D,)), z, z])


def kernel(x, w_in, b_gate, w_pool, pool_scale, w_proj_attn, w_proj_pool, w_out, ln_gamma, ln_beta, loss_target, m_w_in, m_b_gate, m_w_pool, m_pool_scale, m_w_proj_attn, m_w_proj_pool, m_w_out, m_ln_gamma, m_ln_beta, v_w_in, v_b_gate, v_w_pool, v_pool_scale, v_w_proj_attn, v_w_proj_pool, v_w_out, v_ln_gamma, v_ln_beta):
    coords = jnp.stack([lax.axis_index("x"), lax.axis_index("y"), lax.axis_index("c")]).astype(jnp.int32)
    x2, tgt = _permute_tokens(x[0]), _permute_tokens(loss_target[0])
    xb, xt = _prep_x(x2)

    pack = jnp.concatenate([w_in[0].astype(BF16),
                            _pack_small(w_out[0], w_proj_attn[0], w_proj_pool[0], w_pool[0]).astype(BF16)], axis=0)
    h, gw = _ag_proj(_arrival_order(*coords), xb, pack)
    wout = gw[:, R_OUT:R_PA].reshape(D, D)
    wpa = gw[:, R_PA:R_PP].reshape(N_DEV, AW, 256).transpose(1, 0, 2).reshape(AW, D)
    wpp = gw[:, R_PP:R_PL].reshape(N_DEV, AW, 256).transpose(1, 0, 2).reshape(AW, D)
    wpool = gw[:, R_PL:].reshape(N_DEV, 4, 32, PG).transpose(1, 0, 2, 3).reshape(4, PG, PG)

    o, lse = zip(*[_attn_fwd(g, h) for g in range(len(DILATIONS))])
    merged, a, b, p, yat, ypt, mgt = _mix_fwd(h, o, lse, wpa, wpp, wpool, pool_scale, b_gate)
    dr, drb, dm, loss_part, dgamma, dbeta = _out_ln(merged, x2, tgt, wout, ln_gamma, ln_beta)

    dh, da, db, dbgate = _gate_bwd(dm, a, b, h, b_gate)
    dh, do0, do1, do2, dl0, dl1, dl2, dwpool, dpscale = _mix_bwd(
        dh, da, db, h, o, lse, p, wpa, wpp, wpool, pool_scale)
    for g, (do_g, dl_g) in enumerate(zip((do0, do1, do2), (dl0, dl1, dl2))):
        dh = _attn_bwd(g, dh, h, do_g, lse[g], dl_g)

    q = _grad_w_in_rs(_rs_columns(*coords), xt, dh)
    flat = lambda n, k: (0, n)
    d_wout = _grad_w("grad_w_out", mgt, drb, jax.ShapeDtypeStruct((D, D), BF16),
                     pl.BlockSpec((D, 1024), flat), 1024, 1024)
    d_wpa = _grad_w("grad_w_pa", yat, da, jax.ShapeDtypeStruct((AW, D), BF16),
                    pl.BlockSpec((AW, 1024), flat), 1024, 1024)
    d_wpp = _grad_w("grad_w_pp", ypt, db, jax.ShapeDtypeStruct((AW, D), BF16),
                    pl.BlockSpec((AW, 1024), flat), 1024, 1024)
    small = jnp.concatenate([
        d_wout.reshape(N_DEV, 256, D),
        d_wpa.reshape(AW, N_DEV, 256).transpose(1, 0, 2).reshape(N_DEV, -1, D),
        d_wpp.reshape(AW, N_DEV, 256).transpose(1, 0, 2).reshape(N_DEV, -1, D),
        dwpool.astype(BF16).reshape(4, N_DEV, 32, PG).transpose(1, 0, 2, 3).reshape(N_DEV, -1, D)], axis=1)
    q = _pair_sum_small(coords, small, _rs_sibling(small), q)

    grad_x, l2 = _grad_x_rs(dh, gw, dr, q)
    g_in, d_in, m_in, v_in = _adam_shard("adam_w_in", q, l2, w_in[0], m_w_in[0], v_w_in[0], 256)
    g_small = [t.reshape(w.shape) for t, w in zip(_unpack_small(_sum_small(q, l2)),
                                                  (w_out, w_proj_attn, w_proj_pool, w_pool))]
    small = (g_small,) + _adam_whole("adam_small", g_small, (w_out, w_proj_attn, w_proj_pool, w_pool),
                                     (m_w_out, m_w_proj_attn, m_w_proj_pool, m_w_pool),
                                     (v_w_out, v_w_proj_attn, v_w_proj_pool, v_w_pool))

    vec = _pack_vec(dbgate[0], dgamma[0], dbeta[0], dpscale[0], loss_part[0, 0])
    *g_vec, loss = _sum_replicated(_all_gather_direct("ag_vec", vec))
    vecs = (g_vec,) + _adam_whole("adam_replicated", g_vec, (b_gate, pool_scale, ln_gamma, ln_beta),
                                  (m_b_gate, m_pool_scale, m_ln_gamma, m_ln_beta),
                                  (v_b_gate, v_pool_scale, v_ln_gamma, v_ln_beta))
    loss = loss[0, 0]

    def leaves(kind, big):
        out, pa, pp, pool = small[kind]
        bg, ps, gm, bt = vecs[kind]
        return [big[None], bg, pool, ps, pa, pp, out, gm, bt]

    return (loss, _permute_tokens(grad_x)[None], *leaves(0, g_in), *leaves(1, d_in), *leaves(2, m_in), *leaves(3, v_in))
```

```python
import functools

import jax
import jax.numpy as jnp
from jax import lax
from jax.experimental import pallas as pl
from jax.experimental.pallas import tpu as pltpu

F32 = jnp.float32
BF16 = jnp.bfloat16

S = 4096
D = 2048
NW = 16384
AW = 1024
HD = 128
NH = 8
QB = 128
NBLK = S // QB
DILATIONS = (1, 4, 16)
POOL_WINDOWS = (2, 4, 8, 16)
PG = 256
N_DEV = 8
COL_Q, COL_K, COL_V = 0, 3 * AW, 6 * AW
COL_ZA, COL_U, COL_ZP, COL_G = 9 * AW, 10 * AW, 11 * AW, 12 * AW
DH_Z, DH_G = COL_ZA, COL_G
ALPHA = 2.0 ** 0.25
LN_EPS = 1e-5
NEG_INF = -1e30
LR, B1, B2, EPS, WD, STEP = 0.001, 0.9, 0.999, 1e-08, 0.01, 10
R_IN, R_OUT, R_PA, R_PP, R_PL = 0, 2048, 2304, 2432, 2560
R_ALL = 2576
R_SMALL = R_ALL - R_OUT
VMEM_LIMIT = 56 * 1024 * 1024
MESH = pl.DeviceIdType.MESH
ANY = pl.BlockSpec(memory_space=pl.ANY)


def _cparams(n_axes):
    return pltpu.CompilerParams(dimension_semantics=("arbitrary",) * n_axes, vmem_limit_bytes=VMEM_LIMIT)


def _sigmoid(z):
    return 0.5 * jnp.tanh(0.5 * z) + 0.5


def _nt(a, b):
    return lax.dot_general(a, b, (((1,), (1,)), ((), ())), preferred_element_type=F32)


def _tn(a, b):
    return lax.dot_general(a, b, (((0,), (0,)), ((), ())), preferred_element_type=F32)


def _nn(a, b):
    return jnp.dot(a, b, preferred_element_type=F32)


def _lin(x, y, c):
    return 4 * x + 2 * y + c


def _flip(v, f):
    return 1 - v if f else v


def _exchange(name, src, plan, *, dst_shape=None, local_dst=None):
    n = len(plan)
    in_place = dst_shape is None
    out_sds = jax.ShapeDtypeStruct(src.shape, src.dtype) if in_place else dst_shape

    def body(src_ref, dst_ref, send_sems, recv_sems, local_sem):
        x, y, c = lax.axis_index("x"), lax.axis_index("y"), lax.axis_index("c")

        def copy(k, sender):
            flip, src_index, dst_index = plan[k]
            sx, sy, sc = sender
            to = (_flip(sx, flip[0]), _flip(sy, flip[1]), _flip(sc, flip[2]))
            s = src_ref if src_index is None else src_ref.at[src_index(sx, sy, sc)]
            return pltpu.make_async_remote_copy(
                src_ref=s, dst_ref=dst_ref.at[dst_index(sx, sy, sc)],
                send_sem=send_sems.at[k], recv_sem=recv_sems.at[k],
                device_id=to, device_id_type=MESH)

        me = (x, y, c)
        if local_dst is not None:
            mine = pltpu.make_async_copy(src_ref, dst_ref.at[local_dst(x, y, c)], local_sem)
            mine.start()
        sends = [copy(k, me) for k in range(n)]
        for cp in sends:
            cp.start()
        for k in range(n):
            flip = plan[k][0]
            copy(k, (_flip(x, flip[0]), _flip(y, flip[1]), _flip(c, flip[2]))).wait_recv()
        for cp in sends:
            cp.wait_send()
        if local_dst is not None:
            mine.wait()

    return pl.pallas_call(
        body, name=name, out_shape=out_sds, in_specs=[ANY], out_specs=ANY,
        input_output_aliases={0: 0} if in_place else {},
        scratch_shapes=[pltpu.SemaphoreType.DMA((n,)), pltpu.SemaphoreType.DMA((n,)),
                        pltpu.SemaphoreType.DMA(())],
    )(src)


FLIP_C, FLIP_X, FLIP_Y, FLIP_XY = (0, 0, 1), (1, 0, 0), (0, 1, 0), (1, 1, 0)
CHIP_FLIPS = ((0, 0), (1, 0), (0, 1), (1, 1))


AG_PIECES = ((pl.ds(R_IN, D), pl.ds(0, 1024)), (pl.ds(R_IN, D), pl.ds(1024, 1024)),
             (pl.ds(R_OUT, R_PA - R_OUT), pl.ds(0, D)), (pl.ds(R_PA, R_ALL - R_PA), pl.ds(0, D)))
N_PIECES = len(AG_PIECES)
SIB, TO_X, TO_Y, ON, PASS_X, PASS_Y, PASS_D = range(7)
AG_TILES = ((0, 0), (0, 1), (1, 0), (1, 1), (2, 0), (4, 0), (3, 0), (5, 0),
            (2, 1), (4, 1), (3, 1), (5, 1), (6, 0), (6, 1), (7, 0), (7, 1))
W, G = "wait", "go"
AG_STEPS = {
    2: [(W, SIB, 0)], 3: [(W, SIB, 1)],
    4: [(W, TO_X, 0), (G, ON, 0), (G, PASS_X, 0)], 5: [(W, TO_Y, 0), (G, PASS_Y, 0)],
    6: [(W, PASS_X, 0)], 7: [(W, PASS_Y, 0)],
    8: [(W, TO_X, 1), (G, PASS_X, 1), (W, TO_Y, 1), (G, ON, 1), (G, PASS_Y, 1),
        (G, TO_X, 2), (G, TO_X, 3), (G, TO_Y, 2), (G, TO_Y, 3)],
    10: [(W, PASS_X, 1)], 11: [(W, PASS_Y, 1)],
    12: [(W, ON, 0), (G, PASS_D, 0)], 13: [(W, ON, 1), (G, PASS_D, 1)],
    14: [(W, PASS_D, 0), (W, TO_X, 2), (G, ON, 2), (G, PASS_X, 2), (W, TO_X, 3), (G, PASS_X, 3),
         (W, TO_Y, 2), (G, PASS_Y, 2), (W, TO_Y, 3), (G, ON, 3), (G, PASS_Y, 3)],
    15: [(W, PASS_D, 1)],
}
AG_LAST = [(W, SIB, 2), (W, SIB, 3), (W, ON, 2), (G, PASS_D, 2), (W, ON, 3), (G, PASS_D, 3),
           (W, PASS_X, 2), (W, PASS_X, 3), (W, PASS_Y, 2), (W, PASS_Y, 3), (W, PASS_D, 2), (W, PASS_D, 3)]


def _arrival_order(x, y, c):
    chips = [(x, y), (1 - x, y), (x, 1 - y), (1 - x, 1 - y)]
    return jnp.stack([_lin(px, py, pc) for px, py in chips for pc in (c, 1 - c)]).astype(jnp.int32)


def _ag_proj(order, xb, pack):
    tm, tn = 1024, 1024
    nrow, ntile = S // tm, len(AG_TILES)
    slabs = jnp.stack([order[pos] for pos, _ in AG_TILES])
    cols = jnp.stack([2 * order[pos] + half for pos, half in AG_TILES])

    def body(cols_ref, slabs_ref, x_ref, pack_ref, h_ref, gw_ref, wbuf, wsem, send_sems, recv_sems, local_sem):
        t, i = pl.program_id(0), pl.program_id(1)
        x, y, c = lax.axis_index("x"), lax.axis_index("y"), lax.axis_index("c")
        me = _lin(x, y, c)
        dev = {"sib": (x, y, 1 - c), "x": (1 - x, y, c), "y": (x, 1 - y, c), "d": (1 - x, 1 - y, c)}

        def slab_of(name, other_core=False):
            px, py, pc = dev[name]
            return _lin(px, py, 1 - pc if other_core else pc)

        def rdma(slab, kind, piece, to, from_pack=False):
            k = kind * N_PIECES + piece
            there = gw_ref.at[(slab, *AG_PIECES[piece])]
            return pltpu.make_async_remote_copy(
                src_ref=pack_ref.at[AG_PIECES[piece]] if from_pack else there, dst_ref=there,
                send_sem=send_sems.at[k], recv_sem=recv_sems.at[k], device_id=dev[to], device_id_type=MESH)

        def mine(kind, piece):
            if kind in (SIB, TO_X, TO_Y):
                return rdma(me, kind, piece, ("sib", "x", "y")[kind], from_pack=True)
            if kind == ON:
                frm, to = ("x", "y") if piece % 2 == 0 else ("y", "x")
                return rdma(slab_of(frm), kind, piece, to)
            return rdma(slab_of({PASS_X: "x", PASS_Y: "y", PASS_D: "d"}[kind]), kind, piece, "sib")

        def landing(kind, piece):
            slab = {SIB: slab_of("sib"), TO_X: slab_of("x"), TO_Y: slab_of("y"), ON: slab_of("d"),
                    PASS_X: slab_of("x", True), PASS_Y: slab_of("y", True), PASS_D: slab_of("d", True)}[kind]
            return rdma(slab, kind, piece, "sib")

        def run(steps):
            for what, kind, piece in steps:
                if what == W:
                    landing(kind, piece).wait_recv()
                else:
                    mine(kind, piece).start()

        local = pltpu.make_async_copy(pack_ref, gw_ref.at[me], local_sem)

        def fetch(slab, half, slot):
            src = pack_ref.at[AG_PIECES[half]] if slab is None else gw_ref.at[(slab, *AG_PIECES[half])]
            return pltpu.make_async_copy(src, wbuf.at[slot], wsem.at[slot])

        @pl.when((t == 0) & (i == 0))
        def _():
            local.start()
            run([(G, kind, piece) for piece in (0, 1) for kind in (TO_X, TO_Y, SIB)] + [(G, SIB, 2), (G, SIB, 3)])
            first = fetch(None, 0, 0)
            first.start()
            first.wait()

        for nxt in range(1, ntile):
            @pl.when((t == nxt - 1) & (i == nrow - 1))
            def _(nxt=nxt):
                run(AG_STEPS.get(nxt, []))
                fetch(None if AG_TILES[nxt][0] == 0 else slabs_ref[nxt], AG_TILES[nxt][1], nxt % 2).start()

        for slot in (0, 1):
            @pl.when(t % 2 == slot)
            def _(slot=slot):
                @pl.when((i == 0) & (t > 0))
                def _():
                    fetch(None, 0, slot).wait()
                h_ref[...] = _nn(x_ref[...], wbuf[slot]).astype(h_ref.dtype)

        @pl.when((t == ntile - 1) & (i == nrow - 1))
        def _():
            run(AG_LAST)
            for kind in range(7):
                for piece in range(N_PIECES):
                    mine(kind, piece).wait_send()
            local.wait()

    n_sem = 7 * N_PIECES
    grid_spec = pltpu.PrefetchScalarGridSpec(
        num_scalar_prefetch=2, grid=(ntile, nrow),
        in_specs=[pl.BlockSpec((tm, D), lambda t, i, cols, slabs: (i, 0)), ANY],
        out_specs=[pl.BlockSpec((tm, tn), lambda t, i, cols, slabs: (i, cols[t])), ANY],
        scratch_shapes=[pltpu.VMEM((2, D, tn), BF16), pltpu.SemaphoreType.DMA((2,)),
                        pltpu.SemaphoreType.DMA((n_sem,)), pltpu.SemaphoreType.DMA((n_sem,)),
                        pltpu.SemaphoreType.DMA(())])
    return pl.pallas_call(
        body, name="ag_proj", grid_spec=grid_spec,
        out_shape=[jax.ShapeDtypeStruct((S, NW), BF16), jax.ShapeDtypeStruct((N_DEV, R_ALL, D), BF16)],
        compiler_params=_cparams(2))(cols, slabs, xb, pack)


def _rs_sibling(p):
    plan = [(FLIP_C, (lambda x, y, c, f=f: _lin(_flip(x, f[0]), _flip(y, f[1]), 1 - c)),
             (lambda x, y, c, k=k: k)) for k, f in enumerate(CHIP_FLIPS)]
    return _exchange("rs_sibling", p, plan, dst_shape=jax.ShapeDtypeStruct((4,) + p.shape[1:], p.dtype))


def _pair_sum_small(coords, p, l1, q):
    def body(crd, p_ref, l_ref, _, q_ref, buf, sem):
        k = pl.program_id(0)
        buf[...] = (p_ref[...].astype(F32) + l_ref[...].astype(F32)).astype(buf.dtype)
        out = pltpu.make_async_copy(buf, q_ref.at[k, pl.ds(R_OUT, R_SMALL)], sem)
        out.start()
        out.wait()

    def p_map(k, crd):
        fx, fy = k % 2, k // 2
        px = crd[0] + fx - 2 * fx * crd[0]
        py = crd[1] + fy - 2 * fy * crd[1]
        return (_lin(px, py, crd[2]), 0, 0)

    grid_spec = pltpu.PrefetchScalarGridSpec(
        num_scalar_prefetch=1, grid=(4,),
        in_specs=[pl.BlockSpec((None, R_SMALL, D), p_map),
                  pl.BlockSpec((None, R_SMALL, D), lambda k, crd: (k, 0, 0)), ANY],
        out_specs=ANY,
        scratch_shapes=[pltpu.VMEM((R_SMALL, D), BF16), pltpu.SemaphoreType.DMA(())])
    return pl.pallas_call(body, name="pair_sum_small", grid_spec=grid_spec,
                          out_shape=jax.ShapeDtypeStruct(q.shape, q.dtype), input_output_aliases={3: 0},
                          compiler_params=_cparams(1))(coords, p, l1, q)


def _h_block(k):
    return jnp.where(k < 9, (k % 3) * 3 + k // 3, k)


RS_PIECES = (pl.ds(0, 1280), pl.ds(1280, R_ALL - 1280))
RS_ROWS = (1280, R_ALL - 1280)
RS_CHUNKS = ((320,) * 4, (432,) * 3)
RS_MERGE_STEP = 2


def _grad_x_rs(dh, g, dr, q, vec):
    others = [(fx, fy, fc) for fx in (0, 1) for fy in (0, 1) for fc in (0, 1) if (fx, fy, fc) != (0, 0, 0)]
    tm, tk = 1024, 1024
    ni, nk = S // tm, NW // tk
    rmax = max(RS_ROWS)
    cmax = max(max(c) for c in RS_CHUNKS)

    def body(dh_ref, w_ref, dr_ref, q_ref, vec_ref, o_ref, l2_ref, ld_ref, mg_ref, all_ref, va, vb,
             send_sems, recv_sems, sems):
        i, k = pl.program_id(0), pl.program_id(1)
        x, y, c = lax.axis_index("x"), lax.axis_index("y"), lax.axis_index("c")
        nbr = ((1 - x, y, c), (x, 1 - y, c))

        def vec_copy(n, sender):
            sx, sy, sc = sender
            fx, fy, fc = others[n]
            return pltpu.make_async_remote_copy(
                src_ref=vec_ref, dst_ref=all_ref.at[_lin(sx, sy, sc)], send_sem=send_sems.at[6 + n],
                recv_sem=recv_sems.at[6 + n], device_id=(_flip(sx, fx), _flip(sy, fy), _flip(sc, fc)),
                device_id_type=MESH)

        vec_own = pltpu.make_async_copy(vec_ref, all_ref.at[_lin(x, y, c)], sems.at[3])

        def rows(ref, piece):
            return ref.at[piece, pl.ds(0, RS_ROWS[piece])]

        copies = (
            (q_ref.at[3, RS_PIECES[0]], rows(ld_ref, 0), 0),
            (q_ref.at[3, RS_PIECES[1]], rows(ld_ref, 1), 1),
            (q_ref.at[1, RS_PIECES[0]], l2_ref.at[0, RS_PIECES[0]], 0),
            (q_ref.at[2, RS_PIECES[1]], l2_ref.at[1, RS_PIECES[1]], 1),
            (rows(mg_ref, 0), l2_ref.at[1, RS_PIECES[0]], 1),
            (rows(mg_ref, 1), l2_ref.at[0, RS_PIECES[1]], 0),
        )

        def copy(n):
            src, dst, axis = copies[n]
            return pltpu.make_async_remote_copy(src_ref=src, dst_ref=dst, send_sem=send_sems.at[n],
                                                recv_sem=recv_sems.at[n], device_id=nbr[axis], device_id_type=MESH)

        def merge(piece, mine):
            start = 0
            for n_rows in RS_CHUNKS[piece]:
                own = pltpu.make_async_copy(q_ref.at[mine, pl.ds(RS_PIECES[piece].start + start, n_rows)],
                                            va.at[pl.ds(0, n_rows)], sems.at[0])
                got = pltpu.make_async_copy(ld_ref.at[piece, pl.ds(start, n_rows)], vb.at[pl.ds(0, n_rows)], sems.at[1])
                own.start()
                got.start()
                own.wait()
                got.wait()
                va[pl.ds(0, n_rows)] = (va[pl.ds(0, n_rows)].astype(F32)
                                        + vb[pl.ds(0, n_rows)].astype(F32)).astype(va.dtype)
                out = pltpu.make_async_copy(va.at[pl.ds(0, n_rows)], mg_ref.at[piece, pl.ds(start, n_rows)], sems.at[2])
                out.start()
                out.wait()
                start += n_rows

        @pl.when((i == 0) & (k == 0))
        def _():
            for n in range(4):
                copy(n).start()
            vec_own.start()
            for n in range(len(others)):
                vec_copy(n, (x, y, c)).start()

        @pl.when((i == RS_MERGE_STEP) & (k == 0))
        def _():
            copy(0).wait_recv()
            merge(0, 2)
            copy(4).start()
            copy(1).wait_recv()
            merge(1, 1)
            copy(5).start()

        @pl.when(k == 0)
        def _():
            o_ref[...] = ALPHA * dr_ref[...]

        o_ref[...] += _nt(dh_ref[...], w_ref[...])

        @pl.when((i == ni - 1) & (k == nk - 1))
        def _():
            for n in range(2, 6):
                copy(n).wait_recv()
            for n in range(6):
                copy(n).wait_send()
            for n, (fx, fy, fc) in enumerate(others):
                vec_copy(n, (_flip(x, fx), _flip(y, fy), _flip(c, fc))).wait_recv()
                vec_copy(n, (x, y, c)).wait_send()
            vec_own.wait()

    slab = q.shape[1:]
    out = pl.pallas_call(
        body, name="grad_x_rs", grid=(ni, nk),
        in_specs=[pl.BlockSpec((tm, tk), lambda i, k: (i, k)),
                  pl.BlockSpec((None, D, tk), lambda i, k: (_h_block(k) // 2, 0, _h_block(k) % 2)),
                  pl.BlockSpec((tm, D), lambda i, k: (i, 0)), ANY, ANY],
        out_specs=[pl.BlockSpec((tm, D), lambda i, k: (i, 0)), ANY, ANY, ANY, ANY],
        out_shape=[jax.ShapeDtypeStruct((S, D), F32), jax.ShapeDtypeStruct((2,) + slab, q.dtype),
                   jax.ShapeDtypeStruct((2, rmax, slab[1]), q.dtype), jax.ShapeDtypeStruct((2, rmax, slab[1]), q.dtype),
                   jax.ShapeDtypeStruct((N_DEV,) + vec.shape, vec.dtype)],
        scratch_shapes=[pltpu.VMEM((cmax, slab[1]), q.dtype), pltpu.VMEM((cmax, slab[1]), q.dtype),
                        pltpu.SemaphoreType.DMA((6 + len(others),)), pltpu.SemaphoreType.DMA((6 + len(others),)),
                        pltpu.SemaphoreType.DMA((4,))],
        compiler_params=_cparams(2))(dh, g, dr, q, vec)
    return out[0], out[1], out[4]


def _rs_columns(x, y, c):
    out = []
    for core in (1 - c, c):
        for fx, fy in CHIP_FLIPS:
            for half in (0, 1):
                out.append(_h_block(2 * _lin(_flip(x, fx), _flip(y, fy), core) + half))
    return jnp.stack(out).astype(jnp.int32)


def _grad_w_in_rs(cols, xt, dh):
    tn, tk = 1024, 1024
    nk = S // tk
    n_half = 8

    def body(cols_ref, a_ref, b_ref, q_ref, l1_ref, acc_ref, stage, landed, send_sems, recv_sems, sem):
        t, k = pl.program_id(0), pl.program_id(1)
        sib = (lax.axis_index("x"), lax.axis_index("y"), 1 - lax.axis_index("c"))

        @pl.when(k == 0)
        def _():
            acc_ref[...] = jnp.zeros_like(acc_ref)

        acc_ref[...] += _nn(a_ref[...], b_ref[...])

        def there(n):
            return l1_ref.at[n // 2, :, pl.ds((n % 2) * tn, tn)]

        def send(n):
            return pltpu.make_async_remote_copy(src_ref=stage.at[n % 2], dst_ref=there(n), send_sem=send_sems.at[n],
                                                recv_sem=recv_sems.at[n], device_id=sib, device_id_type=MESH)

        for n in range(n_half):
            @pl.when((t == n) & (k == nk - 1))
            def _(n=n):
                if n >= 2:
                    send(n - 2).wait_send()
                stage[n % 2] = acc_ref[...].astype(stage.dtype)
                send(n).start()

        def fetch(n):
            return pltpu.make_async_copy(there(n), landed, sem)

        for n in range(n_half):
            @pl.when((t == n_half + n) & (k == nk - 2))
            def _(n=n):
                if n == 0:
                    send(n_half - 2).wait_send()
                    send(n_half - 1).wait_send()
                send(n).wait_recv()
                fetch(n).start()

            @pl.when((t == n_half + n) & (k == nk - 1))
            def _(n=n):
                fetch(n).wait()
                q_ref[...] = (acc_ref[...] + landed[...].astype(F32)).astype(q_ref.dtype)

    mine = lambda t: jnp.maximum(t - n_half, 0)
    grid_spec = pltpu.PrefetchScalarGridSpec(
        num_scalar_prefetch=1, grid=(2 * n_half, nk),
        in_specs=[pl.BlockSpec((D, tk), lambda t, k, cols: (0, k)),
                  pl.BlockSpec((tk, tn), lambda t, k, cols: (k, cols[t]))],
        out_specs=[pl.BlockSpec((None, D, tn), lambda t, k, cols: (mine(t) // 2, 0, mine(t) % 2)), ANY],
        scratch_shapes=[pltpu.VMEM((D, tn), F32), pltpu.VMEM((2, D, tn), BF16), pltpu.VMEM((D, tn), BF16),
                        pltpu.SemaphoreType.DMA((n_half,)), pltpu.SemaphoreType.DMA((n_half,)),
                        pltpu.SemaphoreType.DMA(())])
    q, _ = pl.pallas_call(
        body, name="grad_w_in_rs", grid_spec=grid_spec,
        out_shape=[jax.ShapeDtypeStruct((4, R_ALL, D), BF16), jax.ShapeDtypeStruct((4, D, D), BF16)],
        compiler_params=_cparams(2))(cols, xt, dh)
    return q


def _grad_w(name, at, b, out_shape, out_spec, tn, tk):
    m, k_all = at.shape
    n_all = b.shape[1]
    nk = k_all // tk

    def body(a_ref, b_ref, o_ref, acc_ref):
        k = pl.program_id(1)

        @pl.when(k == 0)
        def _():
            acc_ref[...] = jnp.zeros_like(acc_ref)

        acc_ref[...] += _nn(a_ref[...], b_ref[...])

        @pl.when(k == nk - 1)
        def _():
            o_ref[...] = acc_ref[...].astype(o_ref.dtype)

    return pl.pallas_call(
        body, name=name, grid=(n_all // tn, nk),
        in_specs=[pl.BlockSpec((m, tk), lambda n, k: (0, k)),
                  pl.BlockSpec((tk, tn), lambda n, k: (k, n))],
        out_specs=out_spec, out_shape=out_shape,
        scratch_shapes=[pltpu.VMEM((m, tn), F32)], compiler_params=_cparams(2))(at, b)


NR = 16
TI = 16
TM = NR * TI
NT = S // TM
ATT_QB = (256, 128, 256)
ATT_NB = (16, 8, 1)
ATT_BLOCKS = (16, 32, 16)


def _permute_tokens(a):
    return a.reshape(NT, TI, NR, a.shape[-1]).transpose(0, 2, 1, 3).reshape(a.shape)


def _attn_shape(g, c):
    if g == 0:
        return (S, c)
    if g == 1:
        return (NT, 4, 4, TI, c)
    return (NT, NR, TI, c)


def _attn_view(g, a):
    return a.reshape(_attn_shape(g, a.shape[-1]))


def _attn_spec(g, width, col, blk):
    if g == 0:
        return pl.BlockSpec((TM, width), lambda b: (blk(b), col))
    if g == 1:
        return pl.BlockSpec((2, 4, None, TI, width), lambda b: (blk(b) % 8, 0, blk(b) // 8, 0, col))
    return pl.BlockSpec((NT, None, TI, width), lambda b: (0, blk(b), 0, col))


def _pieces(g):
    if g == 1:
        return [(t, m) for t in range(2) for m in range(4)]
    return [(t,) for t in range(NT)]


def _get(g, ref, sl):
    if g == 0:
        return ref[:, sl]
    return jnp.concatenate([ref[(*p, slice(None), sl)] for p in _pieces(g)], axis=0)


def _put(g, ref, sl, val):
    if g == 0:
        ref[:, sl] = val
    else:
        for n, p in enumerate(_pieces(g)):
            ref[(*p, slice(None), sl)] = val[TI * n:TI * (n + 1)]


def _block_pos(g, a):
    if g == 0:
        return 16 * (a % 16) + a // 16
    if g == 1:
        return 64 * (a // 64) + 4 * (a % 16) + (a // 16) % 4
    return a


def _attn_mask(g, n):
    qb = ATT_QB[g]
    if ATT_NB[g] == 1:
        qa = lax.broadcasted_iota(jnp.int32, (qb, qb), 0)
        kc = lax.broadcasted_iota(jnp.int32, (qb, qb), 1)
        dist = _block_pos(g, qa) - _block_pos(g, kc)
        return (dist >= 0) & (dist <= QB)
    qa = lax.broadcasted_iota(jnp.int32, (qb, 2 * qb), 0)
    kc = lax.broadcasted_iota(jnp.int32, (qb, 2 * qb), 1)
    cur = kc >= qb
    dist = _block_pos(g, qa) - _block_pos(g, kc % qb) + jnp.where(cur, 0, qb)
    return (dist >= 0) & (dist <= QB) & (cur | (n > 0))


def _keys(g, prev_ref, cur_ref, sl):
    if ATT_NB[g] == 1:
        return _get(g, cur_ref, sl)
    return jnp.concatenate([_get(g, prev_ref, sl), _get(g, cur_ref, sl)], axis=0)


def _qkv_specs(g, clamp):
    cur = lambda col: _attn_spec(g, AW, col, clamp)
    prev = lambda col: _attn_spec(g, AW, col, lambda b: jnp.maximum(clamp(b) - 1, 0))
    qc, kc, vc = (c // AW + g for c in (COL_Q, COL_K, COL_V))
    return [cur(qc), cur(kc), prev(kc), cur(vc), prev(vc)]


def _attn_fwd(g, h):
    scale = HD ** -0.5
    hv = _attn_view(g, h)

    def body(q_ref, kc_ref, kp_ref, vc_ref, vp_ref, o_ref, l_ref):
        valid = _attn_mask(g, pl.program_id(0) % ATT_NB[g])
        for hh in range(NH):
            sl = slice(hh * HD, (hh + 1) * HD)
            kh, vh = _keys(g, kp_ref, kc_ref, sl), _keys(g, vp_ref, vc_ref, sl)
            s = jnp.where(valid, _nt(_get(g, q_ref, sl), kh) * scale, NEG_INF)
            m = jnp.max(s, axis=-1, keepdims=True)
            e = jnp.exp(s - m)
            den = jnp.sum(e, axis=-1, keepdims=True)
            _put(g, o_ref, sl, (_nn(e.astype(BF16), vh) * (1.0 / den)).astype(o_ref.dtype))
            _put(g, l_ref, slice(hh, hh + 1), m + jnp.log(den))

    same = lambda b: b
    o, lse = pl.pallas_call(
        body, name=f"attn_fwd_{g}", grid=(ATT_BLOCKS[g],),
        in_specs=_qkv_specs(g, same),
        out_specs=[_attn_spec(g, AW, 0, same), _attn_spec(g, NH, 0, same)],
        out_shape=[jax.ShapeDtypeStruct(_attn_shape(g, AW), BF16), jax.ShapeDtypeStruct(_attn_shape(g, NH), F32)],
        compiler_params=_cparams(1))(hv, hv, hv, hv, hv)
    return o.reshape(S, AW), lse.reshape(S, NH)


def _attn_bwd(g, dh, h, do, lse, delta):
    scale = HD ** -0.5
    qb = ATT_QB[g]
    carried = ATT_NB[g] > 1
    last = ATT_BLOCKS[g] - 1
    clamp = lambda b: jnp.minimum(b, last)
    behind = lambda b: jnp.maximum(b - 1, 0)
    hv = _attn_view(g, h)

    def body(q_ref, kc_ref, kp_ref, vc_ref, vp_ref, do_ref, l_ref, dl_ref, _, dh_ref, *carry):
        b = pl.program_id(0)

        def write(col, val):
            _put(g, dh_ref, slice(col, col + HD), val.astype(dh_ref.dtype))

        def block():
            valid = _attn_mask(g, b % ATT_NB[g])
            for hh in range(NH):
                sl = slice(hh * HD, (hh + 1) * HD)
                one = slice(hh, hh + 1)
                qh, doh = _get(g, q_ref, sl), _get(g, do_ref, sl)
                kh, vh = _keys(g, kp_ref, kc_ref, sl), _keys(g, vp_ref, vc_ref, sl)
                s = _nt(qh, kh) * scale
                p = jnp.where(valid, jnp.exp(s - _get(g, l_ref, one)), 0.0)
                ds = p * (_nt(doh, vh) - _get(g, dl_ref, one))
                dsb = (ds * scale).astype(BF16)
                dq = _nn(dsb, kh)
                dk2 = _tn(dsb, qh)
                dv2 = _tn(p.astype(BF16), doh)
                if carried:
                    cq_ref, ck_ref, cv_ref = carry
                    write(hh * HD, cq_ref[:, sl])
                    write(AW + hh * HD, ck_ref[:, sl] + dk2[:qb])
                    write(2 * AW + hh * HD, cv_ref[:, sl] + dv2[:qb])
                    cq_ref[:, sl] = dq
                    ck_ref[:, sl] = dk2[qb:]
                    cv_ref[:, sl] = dv2[qb:]
                else:
                    write(hh * HD, dq)
                    write(AW + hh * HD, dk2)
                    write(2 * AW + hh * HD, dv2)

        if not carried:
            block()
            return

        @pl.when(b == 0)
        def _():
            for ref in carry:
                ref[...] = jnp.zeros_like(ref)

        pl.when(b <= last)(block)

        @pl.when(b > last)
        def _():
            for hh in range(NH):
                for n, ref in enumerate(carry):
                    write(n * AW + hh * HD, ref[:, hh * HD:(hh + 1) * HD])

    out = pl.pallas_call(
        body, name=f"attn_bwd_{g}", grid=(ATT_BLOCKS[g] + carried,),
        in_specs=_qkv_specs(g, clamp) + [_attn_spec(g, AW, 0, clamp), _attn_spec(g, NH, 0, clamp),
                                         _attn_spec(g, NH, 0, clamp), ANY],
        out_specs=_attn_spec(g, 3 * AW, g, behind if carried else clamp),
        out_shape=jax.ShapeDtypeStruct(_attn_shape(g, NW), BF16),
        input_output_aliases={8: 0},
        scratch_shapes=[pltpu.VMEM((qb, AW), F32)] * (3 if carried else 0),
        compiler_params=_cparams(1))(hv, hv, hv, hv, hv, _attn_view(g, do), _attn_view(g, lse), _attn_view(g, delta),
                                     _attn_view(g, dh))
    return out.reshape(S, NW)


def _group_weights(l0, l1, l2):
    m = jnp.maximum(jnp.maximum(l0, l1), l2)
    e0, e1, e2 = jnp.exp(l0 - m), jnp.exp(l1 - m), jnp.exp(l2 - m)
    inv = 1.0 / (e0 + e1 + e2)
    return e0 * inv, e1 * inv, e2 * inv


def _residue(ref, r, sl):
    return ref[r * TI:(r + 1) * TI, sl].astype(F32)


def _total(parts):
    return functools.reduce(lambda x, y: x + y, parts)


def _pool_tokens(up_ref, uc_ref, p_ref, tile):
    j0 = lax.broadcasted_iota(jnp.int32, (TI, 1), 0) == 0
    first = (tile == 0) & j0
    for r in range(NR):
        out = []
        for g, w in enumerate(POOL_WINDOWS):
            sl = slice(g * PG, (g + 1) * PG)
            own = _residue(uc_ref, r, sl)
            acc = _total([own] + [_residue(uc_ref, r - k, sl) for k in range(1, min(r, w - 1) + 1)])
            wrapped = [NR + r - k for k in range(r + 1, w)]
            if wrapped:
                wc = _total([_residue(uc_ref, q, sl) for q in wrapped])
                wp = jnp.where(tile > 0, _total([_residue(up_ref, q, sl) for q in wrapped]), 0.0)
                acc = acc + jnp.where(j0, pltpu.roll(wp, 1, 0), pltpu.roll(wc, 1, 0))
            out.append(acc * jnp.where(first, 1.0 / min(r + 1, w), 1.0 / w) - own)
        p_ref[r * TI:(r + 1) * TI, :] = jnp.concatenate(out, axis=1).astype(p_ref.dtype)


def _pool_tokens_bwd(dp, nxt_ref, du_ref, tile):
    ji = lax.broadcasted_iota(jnp.int32, (TI, 1), 0)
    first = (tile == 0) & (ji == 0)
    piece = lambda g, r: dp[g][r * TI:(r + 1) * TI]
    dpc = [[piece(g, r) * jnp.where(first, 1.0 / min(r + 1, w), 1.0 / w) for r in range(NR)]
           for g, w in enumerate(POOL_WINDOWS)]
    for r in range(NR):
        out = []
        for g, w in enumerate(POOL_WINDOWS):
            sl = slice(g * PG, (g + 1) * PG)
            acc = _total([dpc[g][r + k] for k in range(w) if r + k < NR])
            wrapped = [r + k - NR for k in range(1, w) if r + k >= NR]
            if wrapped:
                wc = _total([dpc[g][q] for q in wrapped])
                wn = _total([nxt_ref[q * TI:(q + 1) * TI, sl] for q in wrapped])
                acc = acc + jnp.where(ji == TI - 1, pltpu.roll(wn, TI - 1, 0), pltpu.roll(wc, TI - 1, 0))
            out.append(acc - piece(g, r))
        du_ref[r * TI:(r + 1) * TI, :] = jnp.concatenate(out, axis=1).astype(du_ref.dtype)
    for r in range(NR):
        nxt_ref[r * TI:(r + 1) * TI, :] = jnp.concatenate([dpc[g][r] for g in range(len(POOL_WINDOWS))], axis=1)


def _pool_linear(pb, wpool_ref):
    return jnp.concatenate([_nn(pb[:, g * PG:(g + 1) * PG], wpool_ref[g]) for g in range(len(POOL_WINDOWS))], axis=1)


def _tok(width, col=0, rev=False):
    if rev:
        return pl.BlockSpec((TM, width), lambda i: (NT - 1 - i, col))
    return pl.BlockSpec((TM, width), lambda i: (i, col))


def _whole(shape):
    return pl.BlockSpec(shape, lambda i: (0,) * len(shape))


def _prep_x(x2):
    rows = 1024

    def body(x_ref, xb_ref, xt_ref):
        xb_ref[...] = x_ref[...].astype(BF16)
        xt_ref[...] = x_ref[...].T.astype(BF16)

    return pl.pallas_call(
        body, name="prep_x", grid=(S // rows,), in_specs=[pl.BlockSpec((rows, D), lambda i: (i, 0))],
        out_specs=[pl.BlockSpec((rows, D), lambda i: (i, 0)), pl.BlockSpec((D, rows), lambda i: (0, i))],
        out_shape=[jax.ShapeDtypeStruct((S, D), BF16), jax.ShapeDtypeStruct((D, S), BF16)],
        compiler_params=_cparams(1))(x2)


def _mix_fwd(h, o, lse, wpa, wpp, wpool, pscale, bgate):
    def body(o0_ref, o1_ref, o2_ref, l0_ref, l1_ref, l2_ref, za_ref, uc_ref, up_ref, zp_ref, gp_ref,
             wpa_ref, wpp_ref, wpool_ref, ps_ref, bg_ref,
             mg_ref, a_ref, b_ref, p_ref, yat_ref, ypt_ref, mgt_ref, ya_ref, yp_ref):
        i = pl.program_id(0)
        w0, w1, w2 = _group_weights(l0_ref[...], l1_ref[...], l2_ref[...])
        za = za_ref[...].astype(F32)
        silu_a = za * _sigmoid(za)
        for hh in range(NH):
            sl = slice(hh * HD, (hh + 1) * HD)
            c = slice(hh, hh + 1)
            oh = (w0[:, c] * o0_ref[:, sl].astype(F32) + w1[:, c] * o1_ref[:, sl].astype(F32)
                  + w2[:, c] * o2_ref[:, sl].astype(F32))
            ya = oh * silu_a[:, sl]
            ya_ref[:, sl] = ya.astype(BF16)
            yat_ref[sl, :] = ya.T.astype(BF16)
        _pool_tokens(up_ref, uc_ref, p_ref, i)
        zp = zp_ref[...].astype(F32)
        yp = _pool_linear(p_ref[...], wpool_ref) * ps_ref[...] * (zp * _sigmoid(zp))
        yp_ref[...] = yp.astype(BF16)
        ypt_ref[...] = yp.T.astype(BF16)
        a = _nn(ya_ref[...], wpa_ref[...])
        b = _nn(yp_ref[...], wpp_ref[...])
        a_ref[...] = a.astype(BF16)
        b_ref[...] = b.astype(BF16)
        gates = _sigmoid(gp_ref[...].astype(F32) + bg_ref[...])
        mg = gates[:, :D] * a + gates[:, D:] * b
        mg_ref[...] = mg.astype(BF16)
        mgt_ref[...] = mg.T.astype(BF16)

    u_prev = pl.BlockSpec((TM, AW), lambda i: (jnp.maximum(i - 1, 0), COL_U // AW))
    across = lambda width: pl.BlockSpec((width, TM), lambda i: (0, i))
    return pl.pallas_call(
        body, name="mix_fwd", grid=(NT,),
        in_specs=[_tok(AW)] * 3 + [_tok(NH)] * 3
        + [_tok(AW, COL_ZA // AW), _tok(AW, COL_U // AW), u_prev, _tok(AW, COL_ZP // AW), _tok(2 * D, COL_G // (2 * D))]
        + [_whole((AW, D)), _whole((AW, D)), _whole((4, PG, PG)), _whole((1, AW)), _whole((1, 2 * D))],
        out_specs=[_tok(D), _tok(D), _tok(D), _tok(AW), across(AW), across(AW), across(D)],
        out_shape=[jax.ShapeDtypeStruct((S, D), BF16)] * 3 + [jax.ShapeDtypeStruct((S, AW), BF16)]
        + [jax.ShapeDtypeStruct((AW, S), BF16)] * 2 + [jax.ShapeDtypeStruct((D, S), BF16)],
        scratch_shapes=[pltpu.VMEM((TM, AW), BF16), pltpu.VMEM((TM, AW), BF16)],
        compiler_params=_cparams(1))(*o, *lse, h, h, h, h, h, wpa, wpp, wpool, pscale, bgate)


def _out_ln(merged, x, target, wout, gamma, beta):
    def body(mg_ref, x_ref, t_ref, w_ref, g_ref, b_ref, dr_ref, drb_ref, dm_ref, loss_ref, dg_ref, db_ref):
        i = pl.program_id(0)

        @pl.when(i == 0)
        def _():
            loss_ref[...] = jnp.zeros_like(loss_ref)
            dg_ref[...] = jnp.zeros_like(dg_ref)
            db_ref[...] = jnp.zeros_like(db_ref)

        r = ALPHA * x_ref[...] + _nn(mg_ref[...], w_ref[...])
        mu = jnp.mean(r, axis=-1, keepdims=True)
        rc = r - mu
        rstd = lax.rsqrt(jnp.mean(rc * rc, axis=-1, keepdims=True) + LN_EPS)
        xhat = rc * rstd
        err = xhat * g_ref[...] + b_ref[...] - t_ref[...]
        loss_ref[...] += 0.5 * jnp.sum(jnp.mean(err * err, axis=-1, keepdims=True), axis=0, keepdims=True)
        dy = err * (1.0 / D)
        dg_ref[...] += jnp.sum(dy * xhat, axis=0, keepdims=True)
        db_ref[...] += jnp.sum(dy, axis=0, keepdims=True)
        dxh = dy * g_ref[...]
        dr = rstd * (dxh - jnp.mean(dxh, axis=-1, keepdims=True)
                     - xhat * jnp.mean(dxh * xhat, axis=-1, keepdims=True))
        dr_ref[...] = dr
        drb_ref[...] = dr.astype(BF16)
        dm_ref[...] = _nt(drb_ref[...], w_ref[...]).astype(BF16)

    return pl.pallas_call(
        body, name="out_ln", grid=(NT,),
        in_specs=[_tok(D), _tok(D), _tok(D), _whole((D, D)), _whole((1, D)), _whole((1, D))],
        out_specs=[_tok(D), _tok(D), _tok(D), _whole((8, 128)), _whole((1, D)), _whole((1, D))],
        out_shape=[jax.ShapeDtypeStruct((S, D), F32), jax.ShapeDtypeStruct((S, D), BF16),
                   jax.ShapeDtypeStruct((S, D), BF16), jax.ShapeDtypeStruct((8, 128), F32),
                   jax.ShapeDtypeStruct((1, D), F32), jax.ShapeDtypeStruct((1, D), F32)],
        compiler_params=_cparams(1))(merged, x, target, wout, gamma, beta)


def _gate_bwd(dm, a, b, h, bgate):
    def body(dm_ref, a_ref, b_ref, gp_ref, bg_ref, dgp_ref, da_ref, db_ref, dbg_ref):
        @pl.when(pl.program_id(0) == 0)
        def _():
            dbg_ref[...] = jnp.zeros_like(dbg_ref)

        dm_ = dm_ref[...].astype(F32)
        gates = _sigmoid(gp_ref[...].astype(F32) + bg_ref[...])
        ga, gb = gates[:, :D], gates[:, D:]
        da_ref[...] = (dm_ * ga).astype(BF16)
        db_ref[...] = (dm_ * gb).astype(BF16)
        dgp = jnp.concatenate([dm_ * a_ref[...].astype(F32) * ga * (1.0 - ga),
                               dm_ * b_ref[...].astype(F32) * gb * (1.0 - gb)], axis=1)
        dgp_ref[...] = dgp.astype(BF16)
        dbg_ref[...] += jnp.sum(dgp, axis=0, keepdims=True)

    return pl.pallas_call(
        body, name="gate_bwd", grid=(NT,),
        in_specs=[_tok(D), _tok(D), _tok(D), _tok(2 * D, COL_G // (2 * D)), _whole((1, 2 * D))],
        out_specs=[_tok(2 * D, DH_G // (2 * D)), _tok(D), _tok(D), _whole((1, 2 * D))],
        out_shape=[jax.ShapeDtypeStruct((S, NW), BF16), jax.ShapeDtypeStruct((S, D), BF16),
                   jax.ShapeDtypeStruct((S, D), BF16), jax.ShapeDtypeStruct((1, 2 * D), F32)],
        compiler_params=_cparams(1))(dm, a, b, h, bgate)


def _mix_bwd(dh, da, db, h, o, lse, p, wpa, wpp, wpool, pscale):
    def body(_, da_ref, db_ref, o0_ref, o1_ref, o2_ref, l0_ref, l1_ref, l2_ref, za_ref, zp_ref, p_ref,
             wpa_ref, wpp_ref, wpool_ref, ps_ref,
             dh_ref, do0_ref, do1_ref, do2_ref, dl0_ref, dl1_ref, dl2_ref, dwp_ref, dps_ref,
             nxt_ref):
        i = pl.program_id(0)
        tile = NT - 1 - i
        dza_ref, du_ref, dzp_ref = (dh_ref.at[:, pl.ds(n * AW, AW)] for n in range(3))

        @pl.when(i == 0)
        def _():
            nxt_ref[...] = jnp.zeros_like(nxt_ref)
            dwp_ref[...] = jnp.zeros_like(dwp_ref)
            dps_ref[...] = jnp.zeros_like(dps_ref)

        dya = _nt(da_ref[...], wpa_ref[...])
        w0, w1, w2 = _group_weights(l0_ref[...], l1_ref[...], l2_ref[...])
        za = za_ref[...].astype(F32)
        sig = _sigmoid(za)
        silu_a = za * sig
        dsilu_a = sig * (1.0 + za * (1.0 - sig))
        for hh in range(NH):
            sl = slice(hh * HD, (hh + 1) * HD)
            c = slice(hh, hh + 1)
            oh = (w0[:, c] * o0_ref[:, sl].astype(F32) + w1[:, c] * o1_ref[:, sl].astype(F32)
                  + w2[:, c] * o2_ref[:, sl].astype(F32))
            doh = dya[:, sl] * silu_a[:, sl]
            dza_ref[:, sl] = (dya[:, sl] * oh * dsilu_a[:, sl]).astype(BF16)
            dot_ = jnp.sum(doh * oh, axis=-1, keepdims=True)
            do0_ref[:, sl] = (w0[:, c] * doh).astype(BF16)
            do1_ref[:, sl] = (w1[:, c] * doh).astype(BF16)
            do2_ref[:, sl] = (w2[:, c] * doh).astype(BF16)
            dl0_ref[:, c] = w0[:, c] * dot_
            dl1_ref[:, c] = w1[:, c] * dot_
            dl2_ref[:, c] = w2[:, c] * dot_
        dyp = _nt(db_ref[...], wpp_ref[...])
        pb = p_ref[...]
        pw = _pool_linear(pb, wpool_ref)
        zp = zp_ref[...].astype(F32)
        sigp = _sigmoid(zp)
        dypre = dyp * (zp * sigp)
        dzp_ref[...] = (dyp * (pw * ps_ref[...]) * (sigp * (1.0 + zp * (1.0 - sigp)))).astype(BF16)
        dps_ref[...] += jnp.sum(dypre * pw, axis=0, keepdims=True)
        dpw = (dypre * ps_ref[...]).astype(BF16)
        dp = []
        for g in range(len(POOL_WINDOWS)):
            sl = slice(g * PG, (g + 1) * PG)
            dwp_ref[g] += _tn(pb[:, sl], dpw[:, sl])
            dp.append(_nt(dpw[:, sl], wpool_ref[g]))
        _pool_tokens_bwd(dp, nxt_ref, du_ref, tile)

    r = functools.partial(_tok, rev=True)
    return pl.pallas_call(
        body, name="mix_bwd", grid=(NT,),
        in_specs=[ANY, r(D), r(D)] + [r(AW)] * 3 + [r(NH)] * 3 + [r(AW, COL_ZA // AW), r(AW, COL_ZP // AW), r(AW)]
        + [_whole((AW, D)), _whole((AW, D)), _whole((4, PG, PG)), _whole((1, AW))],
        out_specs=[r(3 * AW, DH_Z // (3 * AW))] + [r(AW)] * 3 + [r(NH)] * 3 + [_whole((4, PG, PG)), _whole((1, AW))],
        out_shape=[jax.ShapeDtypeStruct((S, NW), BF16)] + [jax.ShapeDtypeStruct((S, AW), BF16)] * 3
        + [jax.ShapeDtypeStruct((S, NH), F32)] * 3
        + [jax.ShapeDtypeStruct((4, PG, PG), F32), jax.ShapeDtypeStruct((1, AW), F32)],
        input_output_aliases={0: 0},
        scratch_shapes=[pltpu.VMEM((TM, AW), F32)],
        compiler_params=_cparams(1))(dh, da, db, *o, *lse, h, h, p, wpa, wpp, wpool, pscale)


def _adamw(w, g, m, v):
    m = B1 * m + (1.0 - B1) * g
    v = B2 * v + (1.0 - B2) * jnp.square(g)
    m_hat = m / (1.0 - B1 ** STEP)
    v_hat = v / (1.0 - B2 ** STEP)
    return -LR * (m_hat / (jnp.sqrt(v_hat) + EPS) + WD * w), m, v


def _adam_shard(name, q, l2, w, m, v, tr):
    rows = w.shape[0]

    def body(q_ref, l_ref, w_ref, m_ref, v_ref, g_out, d_out, m_out, v_out):
        g = q_ref[...].astype(F32)
        for k in range(2):
            g = g + l_ref[k].astype(F32)
        g_out[...] = g
        d_out[...], m_out[...], v_out[...] = _adamw(w_ref[...], g, m_ref[...], v_ref[...])

    blk = pl.BlockSpec((tr, D), lambda i: (i, 0))
    return pl.pallas_call(
        body, name=name, grid=(rows // tr,),
        in_specs=[pl.BlockSpec((None, tr, D), lambda i: (0, i, 0)), pl.BlockSpec((2, tr, D), lambda i: (0, i, 0)),
                  blk, blk, blk],
        out_specs=[blk] * 4, out_shape=[jax.ShapeDtypeStruct((rows, D), F32)] * 4,
        compiler_params=_cparams(1))(q, l2, w, m, v)


def _sum_small(q, l2):
    def body(q_ref, l_ref, g_out, buf, sems):
        rows = pl.ds(R_OUT, R_SMALL)
        copies = [pltpu.make_async_copy(src, buf.at[n], sems.at[n])
                  for n, src in enumerate((q_ref.at[0, rows], l_ref.at[0, rows], l_ref.at[1, rows]))]
        for cp in copies:
            cp.start()
        for cp in copies:
            cp.wait()
        g_out[...] = buf[0].astype(F32) + buf[1].astype(F32) + buf[2].astype(F32)

    return pl.pallas_call(
        body, name="sum_small", in_specs=[ANY, ANY], out_shape=jax.ShapeDtypeStruct((R_SMALL, D), F32),
        scratch_shapes=[pltpu.VMEM((3, R_SMALL, D), q.dtype), pltpu.SemaphoreType.DMA((3,))],
        compiler_params=pltpu.CompilerParams(vmem_limit_bytes=VMEM_LIMIT))(q, l2)


def _adam_whole(name, grads, weights, ms, vs):
    n = len(grads)

    def body(*refs):
        ins, outs = refs[:4 * n], refs[4 * n:]
        for t in range(n):
            g, w, m, v = (ins[k * n + t][...] for k in range(4))
            outs[t][...], outs[n + t][...], outs[2 * n + t][...] = _adamw(w, g, m, v)

    out = pl.pallas_call(
        body, name=name, out_shape=[jax.ShapeDtypeStruct(w.shape, F32) for w in weights] * 3,
        compiler_params=pltpu.CompilerParams(vmem_limit_bytes=VMEM_LIMIT))(*grads, *weights, *ms, *vs)
    return out[:n], out[n:2 * n], out[2 * n:]


def _sum_replicated(gathered):
    def body(g_ref, bg_out, ps_out, gm_out, bt_out, loss_out):
        g = g_ref[0]
        for k in range(1, N_DEV):
            g = g + g_ref[k]
        bg_out[...] = jnp.concatenate([g[0:1], g[1:2]], axis=1)
        gm_out[...] = g[2:3]
        bt_out[...] = g[3:4]
        ps_out[...] = g[4:5, :AW]
        loss_out[...] = jnp.broadcast_to(g[5:6, :128], loss_out.shape)

    return pl.pallas_call(
        body, name="sum_replicated",
        out_shape=[jax.ShapeDtypeStruct(shape, F32) for shape in ((1, 2 * D), (1, AW), (1, D), (1, D), (8, 128))],
        compiler_params=pltpu.CompilerParams(vmem_limit_bytes=VMEM_LIMIT))(gathered)


def _pack_small(w_out, w_pa, w_pp, w_pool):
    return jnp.concatenate([w_out, w_pa.reshape(-1, D), w_pp.reshape(-1, D), w_pool.reshape(-1, D)], axis=0)


def _unpack_small(a):
    o = R_OUT
    return (a[:R_PA - o], a[R_PA - o:R_PP - o].reshape(AW, 256), a[R_PP - o:R_PL - o].reshape(AW, 256),
            a[R_PL - o:].reshape(4, 32, PG))


def _pack_vec(b_gate, gamma, beta, pscale, extra):
    z = jnp.zeros((D,), F32)
    return jnp.stack([b_gate[:D], b_gate[D:], gamma, beta, jnp.concatenate([pscale, z[:D - AW]]),
                      jnp.broadcast_to(extra, (D,)), z, z])


def kernel(x, w_in, b_gate, w_pool, pool_scale, w_proj_attn, w_proj_pool, w_out, ln_gamma, ln_beta, loss_target, m_w_in, m_b_gate, m_w_pool, m_pool_scale, m_w_proj_attn, m_w_proj_pool, m_w_out, m_ln_gamma, m_ln_beta, v_w_in, v_b_gate, v_w_pool, v_pool_scale, v_w_proj_attn, v_w_proj_pool, v_w_out, v_ln_gamma, v_ln_beta):
    coords = jnp.stack([lax.axis_index("x"), lax.axis_index("y"), lax.axis_index("c")]).astype(jnp.int32)
    x2, tgt = _permute_tokens(x[0]), _permute_tokens(loss_target[0])
    xb, xt = _prep_x(x2)

    pack = jnp.concatenate([w_in[0].astype(BF16),
                            _pack_small(w_out[0], w_proj_attn[0], w_proj_pool[0], w_pool[0]).astype(BF16)], axis=0)
    h, gw = _ag_proj(_arrival_order(*coords), xb, pack)
    wout = gw[:, R_OUT:R_PA].reshape(D, D)
    wpa = gw[:, R_PA:R_PP].reshape(N_DEV, AW, 256).transpose(1, 0, 2).reshape(AW, D)
    wpp = gw[:, R_PP:R_PL].reshape(N_DEV, AW, 256).transpose(1, 0, 2).reshape(AW, D)
    wpool = gw[:, R_PL:].reshape(N_DEV, 4, 32, PG).transpose(1, 0, 2, 3).reshape(4, PG, PG)

    o, lse = zip(*[_attn_fwd(g, h) for g in range(len(DILATIONS))])
    merged, a, b, p, yat, ypt, mgt = _mix_fwd(h, o, lse, wpa, wpp, wpool, pool_scale, b_gate)
    dr, drb, dm, loss_part, dgamma, dbeta = _out_ln(merged, x2, tgt, wout, ln_gamma, ln_beta)

    dh, da, db, dbgate = _gate_bwd(dm, a, b, h, b_gate)
    dh, do0, do1, do2, dl0, dl1, dl2, dwpool, dpscale = _mix_bwd(
        dh, da, db, h, o, lse, p, wpa, wpp, wpool, pool_scale)
    for g, (do_g, dl_g) in enumerate(zip((do0, do1, do2), (dl0, dl1, dl2))):
        dh = _attn_bwd(g, dh, h, do_g, lse[g], dl_g)

    q = _grad_w_in_rs(_rs_columns(*coords), xt, dh)
    flat = lambda n, k: (0, n)
    d_wout = _grad_w("grad_w_out", mgt, drb, jax.ShapeDtypeStruct((D, D), BF16),
                     pl.BlockSpec((D, 1024), flat), 1024, 1024)
    d_wpa = _grad_w("grad_w_pa", yat, da, jax.ShapeDtypeStruct((AW, D), BF16),
                    pl.BlockSpec((AW, 1024), flat), 1024, 1024)
    d_wpp = _grad_w("grad_w_pp", ypt, db, jax.ShapeDtypeStruct((AW, D), BF16),
                    pl.BlockSpec((AW, 1024), flat), 1024, 1024)
    small = jnp.concatenate([
        d_wout.reshape(N_DEV, 256, D),
        d_wpa.reshape(AW, N_DEV, 256).transpose(1, 0, 2).reshape(N_DEV, -1, D),
        d_wpp.reshape(AW, N_DEV, 256).transpose(1, 0, 2).reshape(N_DEV, -1, D),
        dwpool.astype(BF16).reshape(4, N_DEV, 32, PG).transpose(1, 0, 2, 3).reshape(N_DEV, -1, D)], axis=1)
    q = _pair_sum_small(coords, small, _rs_sibling(small), q)

    vec = _pack_vec(dbgate[0], dgamma[0], dbeta[0], dpscale[0], loss_part[0, 0])
    grad_x, l2, vecs_all = _grad_x_rs(dh, gw, dr, q, vec)
    g_in, d_in, m_in, v_in = _adam_shard("adam_w_in", q, l2, w_in[0], m_w_in[0], v_w_in[0], 256)
    g_small = [t.reshape(w.shape) for t, w in zip(_unpack_small(_sum_small(q, l2)),
                                                  (w_out, w_proj_attn, w_proj_pool, w_pool))]
    small = (g_small,) + _adam_whole("adam_small", g_small, (w_out, w_proj_attn, w_proj_pool, w_pool),
                                     (m_w_out, m_w_proj_attn, m_w_proj_pool, m_w_pool),
                                     (v_w_out, v_w_proj_attn, v_w_proj_pool, v_w_pool))

    *g_vec, loss = _sum_replicated(vecs_all)
    vecs = (g_vec,) + _adam_whole("adam_replicated", g_vec, (b_gate, pool_scale, ln_gamma, ln_beta),
                                  (m_b_gate, m_pool_scale, m_ln_gamma, m_ln_beta),
                                  (v_b_gate, v_pool_scale, v_ln_gamma, v_ln_beta))
    loss = loss[0, 0]

    def leaves(kind, big):
        out, pa, pp, pool = small[kind]
        bg, ps, gm, bt = vecs[kind]
        return [big[None], bg, pool, ps, pa, pp, out, gm, bt]

    return (loss, _permute_tokens(grad_x)[None], *leaves(0, g_in), *leaves(1, d_in), *leaves(2, m_in), *leaves(3, v_in))
```

```python
import functools

import jax
import jax.numpy as jnp
from jax import lax
from jax.experimental import pallas as pl
from jax.experimental.pallas import tpu as pltpu

F32 = jnp.float32
BF16 = jnp.bfloat16

S = 4096
D = 2048
NW = 16384
AW = 1024
HD = 128
NH = 8
QB = 128
NBLK = S // QB
DILATIONS = (1, 4, 16)
POOL_WINDOWS = (2, 4, 8, 16)
PG = 256
N_DEV = 8
COL_Q, COL_K, COL_V = 0, 3 * AW, 6 * AW
COL_ZA, COL_U, COL_ZP, COL_G = 9 * AW, 10 * AW, 11 * AW, 12 * AW
DH_Z, DH_G = COL_ZA, COL_G
ALPHA = 2.0 ** 0.25
LN_EPS = 1e-5
NEG_INF = -1e30
LR, B1, B2, EPS, WD, STEP = 0.001, 0.9, 0.999, 1e-08, 0.01, 10
R_IN, R_OUT, R_PA, R_PP, R_PL = 0, 2048, 2304, 2432, 2560
R_ALL = 2576
R_SMALL = R_ALL - R_OUT
VMEM_LIMIT = 56 * 1024 * 1024
MESH = pl.DeviceIdType.MESH
ANY = pl.BlockSpec(memory_space=pl.ANY)


def _cparams(n_axes):
    return pltpu.CompilerParams(dimension_semantics=("arbitrary",) * n_axes, vmem_limit_bytes=VMEM_LIMIT)


def _sigmoid(z):
    return 0.5 * jnp.tanh(0.5 * z) + 0.5


def _nt(a, b):
    return lax.dot_general(a, b, (((1,), (1,)), ((), ())), preferred_element_type=F32)


def _tn(a, b):
    return lax.dot_general(a, b, (((0,), (0,)), ((), ())), preferred_element_type=F32)


def _nn(a, b):
    return jnp.dot(a, b, preferred_element_type=F32)


def _lin(x, y, c):
    return 4 * x + 2 * y + c


def _flip(v, f):
    return 1 - v if f else v


def _exchange(name, src, plan, *, dst_shape=None, local_dst=None):
    n = len(plan)
    in_place = dst_shape is None
    out_sds = jax.ShapeDtypeStruct(src.shape, src.dtype) if in_place else dst_shape

    def body(src_ref, dst_ref, send_sems, recv_sems, local_sem):
        x, y, c = lax.axis_index("x"), lax.axis_index("y"), lax.axis_index("c")

        def copy(k, sender):
            flip, src_index, dst_index = plan[k]
            sx, sy, sc = sender
            to = (_flip(sx, flip[0]), _flip(sy, flip[1]), _flip(sc, flip[2]))
            s = src_ref if src_index is None else src_ref.at[src_index(sx, sy, sc)]
            return pltpu.make_async_remote_copy(
                src_ref=s, dst_ref=dst_ref.at[dst_index(sx, sy, sc)],
                send_sem=send_sems.at[k], recv_sem=recv_sems.at[k],
                device_id=to, device_id_type=MESH)

        me = (x, y, c)
        if local_dst is not None:
            mine = pltpu.make_async_copy(src_ref, dst_ref.at[local_dst(x, y, c)], local_sem)
            mine.start()
        sends = [copy(k, me) for k in range(n)]
        for cp in sends:
            cp.start()
        for k in range(n):
            flip = plan[k][0]
            copy(k, (_flip(x, flip[0]), _flip(y, flip[1]), _flip(c, flip[2]))).wait_recv()
        for cp in sends:
            cp.wait_send()
        if local_dst is not None:
            mine.wait()

    return pl.pallas_call(
        body, name=name, out_shape=out_sds, in_specs=[ANY], out_specs=ANY,
        input_output_aliases={0: 0} if in_place else {},
        scratch_shapes=[pltpu.SemaphoreType.DMA((n,)), pltpu.SemaphoreType.DMA((n,)),
                        pltpu.SemaphoreType.DMA(())],
    )(src)


FLIP_C, FLIP_X, FLIP_Y, FLIP_XY = (0, 0, 1), (1, 0, 0), (0, 1, 0), (1, 1, 0)
CHIP_FLIPS = ((0, 0), (1, 0), (0, 1), (1, 1))


AG_PIECES = ((pl.ds(R_IN, D), pl.ds(0, 1024)), (pl.ds(R_IN, D), pl.ds(1024, 1024)),
             (pl.ds(R_OUT, R_PA - R_OUT), pl.ds(0, D)), (pl.ds(R_PA, R_ALL - R_PA), pl.ds(0, D)))
N_PIECES = len(AG_PIECES)
SIB, TO_X, TO_Y, ON, PASS_X, PASS_Y, PASS_D = range(7)
AG_TILES = ((0, 0), (0, 1), (1, 0), (1, 1), (2, 0), (4, 0), (3, 0), (5, 0),
            (2, 1), (4, 1), (3, 1), (5, 1), (6, 0), (6, 1), (7, 0), (7, 1))
W, G = "wait", "go"
AG_STEPS = {
    2: [(W, SIB, 0)], 3: [(W, SIB, 1)],
    4: [(W, TO_X, 0), (G, ON, 0), (G, PASS_X, 0)], 5: [(W, TO_Y, 0), (G, PASS_Y, 0)],
    6: [(W, PASS_X, 0)], 7: [(W, PASS_Y, 0)],
    8: [(W, TO_X, 1), (G, PASS_X, 1), (W, TO_Y, 1), (G, ON, 1), (G, PASS_Y, 1),
        (G, TO_X, 2), (G, TO_X, 3), (G, TO_Y, 2), (G, TO_Y, 3)],
    10: [(W, PASS_X, 1)], 11: [(W, PASS_Y, 1)],
    12: [(W, ON, 0), (G, PASS_D, 0)], 13: [(W, ON, 1), (G, PASS_D, 1)],
    14: [(W, PASS_D, 0), (W, TO_X, 2), (G, ON, 2), (G, PASS_X, 2), (W, TO_X, 3), (G, PASS_X, 3),
         (W, TO_Y, 2), (G, PASS_Y, 2), (W, TO_Y, 3), (G, ON, 3), (G, PASS_Y, 3)],
    15: [(W, PASS_D, 1)],
}
AG_LAST = [(W, SIB, 2), (W, SIB, 3), (W, ON, 2), (G, PASS_D, 2), (W, ON, 3), (G, PASS_D, 3),
           (W, PASS_X, 2), (W, PASS_X, 3), (W, PASS_Y, 2), (W, PASS_Y, 3), (W, PASS_D, 2), (W, PASS_D, 3)]


def _arrival_order(x, y, c):
    chips = [(x, y), (1 - x, y), (x, 1 - y), (1 - x, 1 - y)]
    return jnp.stack([_lin(px, py, pc) for px, py in chips for pc in (c, 1 - c)]).astype(jnp.int32)


def _ag_proj(order, xb, pack):
    tm, tn = 1024, 1024
    nrow, ntile = S // tm, len(AG_TILES)
    slabs = jnp.stack([order[pos] for pos, _ in AG_TILES])
    cols = jnp.stack([2 * order[pos] + half for pos, half in AG_TILES])

    def body(cols_ref, slabs_ref, x_ref, pack_ref, h_ref, gw_ref, wbuf, wsem, send_sems, recv_sems, local_sem):
        t, i = pl.program_id(0), pl.program_id(1)
        x, y, c = lax.axis_index("x"), lax.axis_index("y"), lax.axis_index("c")
        me = _lin(x, y, c)
        dev = {"sib": (x, y, 1 - c), "x": (1 - x, y, c), "y": (x, 1 - y, c), "d": (1 - x, 1 - y, c)}

        def slab_of(name, other_core=False):
            px, py, pc = dev[name]
            return _lin(px, py, 1 - pc if other_core else pc)

        def rdma(slab, kind, piece, to, from_pack=False):
            k = kind * N_PIECES + piece
            there = gw_ref.at[(slab, *AG_PIECES[piece])]
            return pltpu.make_async_remote_copy(
                src_ref=pack_ref.at[AG_PIECES[piece]] if from_pack else there, dst_ref=there,
                send_sem=send_sems.at[k], recv_sem=recv_sems.at[k], device_id=dev[to], device_id_type=MESH)

        def mine(kind, piece):
            if kind in (SIB, TO_X, TO_Y):
                return rdma(me, kind, piece, ("sib", "x", "y")[kind], from_pack=True)
            if kind == ON:
                frm, to = ("x", "y") if piece % 2 == 0 else ("y", "x")
                return rdma(slab_of(frm), kind, piece, to)
            return rdma(slab_of({PASS_X: "x", PASS_Y: "y", PASS_D: "d"}[kind]), kind, piece, "sib")

        def landing(kind, piece):
            slab = {SIB: slab_of("sib"), TO_X: slab_of("x"), TO_Y: slab_of("y"), ON: slab_of("d"),
                    PASS_X: slab_of("x", True), PASS_Y: slab_of("y", True), PASS_D: slab_of("d", True)}[kind]
            return rdma(slab, kind, piece, "sib")

        def run(steps):
            for what, kind, piece in steps:
                if what == W:
                    landing(kind, piece).wait_recv()
                else:
                    mine(kind, piece).start()

        local = pltpu.make_async_copy(pack_ref, gw_ref.at[me], local_sem)

        def fetch(slab, half, slot):
            src = pack_ref.at[AG_PIECES[half]] if slab is None else gw_ref.at[(slab, *AG_PIECES[half])]
            return pltpu.make_async_copy(src, wbuf.at[slot], wsem.at[slot])

        @pl.when((t == 0) & (i == 0))
        def _():
            local.start()
            run([(G, kind, piece) for piece in (0, 1) for kind in (TO_X, TO_Y, SIB)] + [(G, SIB, 2), (G, SIB, 3)])
            first = fetch(None, 0, 0)
            first.start()
            first.wait()

        for nxt in range(1, ntile):
            @pl.when((t == nxt - 1) & (i == nrow - 1))
            def _(nxt=nxt):
                run(AG_STEPS.get(nxt, []))
                fetch(None if AG_TILES[nxt][0] == 0 else slabs_ref[nxt], AG_TILES[nxt][1], nxt % 2).start()

        for slot in (0, 1):
            @pl.when(t % 2 == slot)
            def _(slot=slot):
                @pl.when((i == 0) & (t > 0))
                def _():
                    fetch(None, 0, slot).wait()
                h_ref[...] = _nn(x_ref[...], wbuf[slot]).astype(h_ref.dtype)

        @pl.when((t == ntile - 1) & (i == nrow - 1))
        def _():
            run(AG_LAST)
            for kind in range(7):
                for piece in range(N_PIECES):
                    mine(kind, piece).wait_send()
            local.wait()

    n_sem = 7 * N_PIECES
    grid_spec = pltpu.PrefetchScalarGridSpec(
        num_scalar_prefetch=2, grid=(ntile, nrow),
        in_specs=[pl.BlockSpec((tm, D), lambda t, i, cols, slabs: (i, 0)), ANY],
        out_specs=[pl.BlockSpec((tm, tn), lambda t, i, cols, slabs: (i, cols[t])), ANY],
        scratch_shapes=[pltpu.VMEM((2, D, tn), BF16), pltpu.SemaphoreType.DMA((2,)),
                        pltpu.SemaphoreType.DMA((n_sem,)), pltpu.SemaphoreType.DMA((n_sem,)),
                        pltpu.SemaphoreType.DMA(())])
    return pl.pallas_call(
        body, name="ag_proj", grid_spec=grid_spec,
        out_shape=[jax.ShapeDtypeStruct((S, NW), BF16), jax.ShapeDtypeStruct((N_DEV, R_ALL, D), BF16)],
        compiler_params=_cparams(2))(cols, slabs, xb, pack)


def _rs_sibling(p):
    plan = [(FLIP_C, (lambda x, y, c, f=f: _lin(_flip(x, f[0]), _flip(y, f[1]), 1 - c)),
             (lambda x, y, c, k=k: k)) for k, f in enumerate(CHIP_FLIPS)]
    return _exchange("rs_sibling", p, plan, dst_shape=jax.ShapeDtypeStruct((4,) + p.shape[1:], p.dtype))


def _pair_sum_small(coords, p, l1, q):
    def body(crd, p_ref, l_ref, _, q_ref, buf, sem):
        k = pl.program_id(0)
        buf[...] = (p_ref[...].astype(F32) + l_ref[...].astype(F32)).astype(buf.dtype)
        out = pltpu.make_async_copy(buf, q_ref.at[k, pl.ds(R_OUT, R_SMALL)], sem)
        out.start()
        out.wait()

    def p_map(k, crd):
        fx, fy = k % 2, k // 2
        px = crd[0] + fx - 2 * fx * crd[0]
        py = crd[1] + fy - 2 * fy * crd[1]
        return (_lin(px, py, crd[2]), 0, 0)

    grid_spec = pltpu.PrefetchScalarGridSpec(
        num_scalar_prefetch=1, grid=(4,),
        in_specs=[pl.BlockSpec((None, R_SMALL, D), p_map),
                  pl.BlockSpec((None, R_SMALL, D), lambda k, crd: (k, 0, 0)), ANY],
        out_specs=ANY,
        scratch_shapes=[pltpu.VMEM((R_SMALL, D), BF16), pltpu.SemaphoreType.DMA(())])
    return pl.pallas_call(body, name="pair_sum_small", grid_spec=grid_spec,
                          out_shape=jax.ShapeDtypeStruct(q.shape, q.dtype), input_output_aliases={3: 0},
                          compiler_params=_cparams(1))(coords, p, l1, q)


def _h_block(k):
    return jnp.where(k < 9, (k % 3) * 3 + k // 3, k)


RS_PIECES = (pl.ds(0, 1280), pl.ds(1280, R_ALL - 1280))
RS_ROWS = (1280, R_ALL - 1280)
RS_CHUNKS = ((320,) * 4, (432,) * 3)
RS_MERGE_STEP = 2


def _grad_x_rs(dh, g, dr, q, vec):
    others = [(fx, fy, fc) for fx in (0, 1) for fy in (0, 1) for fc in (0, 1) if (fx, fy, fc) != (0, 0, 0)]
    tm, tk = 1024, 1024
    ni, nk = S // tm, NW // tk
    rmax = max(RS_ROWS)
    cmax = max(max(c) for c in RS_CHUNKS)

    def body(dh_ref, w_ref, dr_ref, q_ref, vec_ref, o_ref, l2_ref, ld_ref, mg_ref, all_ref, va, vb,
             send_sems, recv_sems, sems):
        i, k = pl.program_id(0), pl.program_id(1)
        x, y, c = lax.axis_index("x"), lax.axis_index("y"), lax.axis_index("c")
        nbr = ((1 - x, y, c), (x, 1 - y, c))

        def vec_copy(n, sender):
            sx, sy, sc = sender
            fx, fy, fc = others[n]
            return pltpu.make_async_remote_copy(
                src_ref=vec_ref, dst_ref=all_ref.at[_lin(sx, sy, sc)], send_sem=send_sems.at[6 + n],
                recv_sem=recv_sems.at[6 + n], device_id=(_flip(sx, fx), _flip(sy, fy), _flip(sc, fc)),
                device_id_type=MESH)

        vec_own = pltpu.make_async_copy(vec_ref, all_ref.at[_lin(x, y, c)], sems.at[3])

        def rows(ref, piece):
            return ref.at[piece, pl.ds(0, RS_ROWS[piece])]

        copies = (
            (q_ref.at[3, RS_PIECES[0]], rows(ld_ref, 0), 0),
            (q_ref.at[3, RS_PIECES[1]], rows(ld_ref, 1), 1),
            (q_ref.at[1, RS_PIECES[0]], l2_ref.at[0, RS_PIECES[0]], 0),
            (q_ref.at[2, RS_PIECES[1]], l2_ref.at[1, RS_PIECES[1]], 1),
            (rows(mg_ref, 0), l2_ref.at[1, RS_PIECES[0]], 1),
            (rows(mg_ref, 1), l2_ref.at[0, RS_PIECES[1]], 0),
        )

        def copy(n):
            src, dst, axis = copies[n]
            return pltpu.make_async_remote_copy(src_ref=src, dst_ref=dst, send_sem=send_sems.at[n],
                                                recv_sem=recv_sems.at[n], device_id=nbr[axis], device_id_type=MESH)

        def merge(piece, mine):
            start = 0
            for n_rows in RS_CHUNKS[piece]:
                own = pltpu.make_async_copy(q_ref.at[mine, pl.ds(RS_PIECES[piece].start + start, n_rows)],
                                            va.at[pl.ds(0, n_rows)], sems.at[0])
                got = pltpu.make_async_copy(ld_ref.at[piece, pl.ds(start, n_rows)], vb.at[pl.ds(0, n_rows)], sems.at[1])
                own.start()
                got.start()
                own.wait()
                got.wait()
                va[pl.ds(0, n_rows)] = (va[pl.ds(0, n_rows)].astype(F32)
                                        + vb[pl.ds(0, n_rows)].astype(F32)).astype(va.dtype)
                out = pltpu.make_async_copy(va.at[pl.ds(0, n_rows)], mg_ref.at[piece, pl.ds(start, n_rows)], sems.at[2])
                out.start()
                out.wait()
                start += n_rows

        @pl.when((i == 0) & (k == 0))
        def _():
            for n in range(4):
                copy(n).start()
            vec_own.start()
            for n in range(len(others)):
                vec_copy(n, (x, y, c)).start()

        @pl.when((i == RS_MERGE_STEP) & (k == 0))
        def _():
            copy(0).wait_recv()
            merge(0, 2)
            copy(4).start()
            copy(1).wait_recv()
            merge(1, 1)
            copy(5).start()

        @pl.when(k == 0)
        def _():
            o_ref[...] = ALPHA * dr_ref[...]

        o_ref[...] += _nt(dh_ref[...], w_ref[...])

        @pl.when((i == ni - 1) & (k == nk - 1))
        def _():
            for n in range(2, 6):
                copy(n).wait_recv()
            for n in range(6):
                copy(n).wait_send()
            for n, (fx, fy, fc) in enumerate(others):
                vec_copy(n, (_flip(x, fx), _flip(y, fy), _flip(c, fc))).wait_recv()
                vec_copy(n, (x, y, c)).wait_send()
            vec_own.wait()

    slab = q.shape[1:]
    out = pl.pallas_call(
        body, name="grad_x_rs", grid=(ni, nk),
        in_specs=[pl.BlockSpec((tm, tk), lambda i, k: (i, k)),
                  pl.BlockSpec((None, D, tk), lambda i, k: (_h_block(k) // 2, 0, _h_block(k) % 2)),
                  pl.BlockSpec((tm, D), lambda i, k: (i, 0)), ANY, ANY],
        out_specs=[pl.BlockSpec((tm, D), lambda i, k: (i, 0)), ANY, ANY, ANY, ANY],
        out_shape=[jax.ShapeDtypeStruct((S, D), F32), jax.ShapeDtypeStruct((2,) + slab, q.dtype),
                   jax.ShapeDtypeStruct((2, rmax, slab[1]), q.dtype), jax.ShapeDtypeStruct((2, rmax, slab[1]), q.dtype),
                   jax.ShapeDtypeStruct((N_DEV,) + vec.shape, vec.dtype)],
        scratch_shapes=[pltpu.VMEM((cmax, slab[1]), q.dtype), pltpu.VMEM((cmax, slab[1]), q.dtype),
                        pltpu.SemaphoreType.DMA((6 + len(others),)), pltpu.SemaphoreType.DMA((6 + len(others),)),
                        pltpu.SemaphoreType.DMA((4,))],
        compiler_params=_cparams(2))(dh, g, dr, q, vec)
    return out[0], out[1], out[4]


GW_TN = 512
GW_PARTS = D // GW_TN


def _rs_columns(x, y, c):
    per_h = AW // GW_TN
    out = []
    for core in (1 - c, c):
        for fx, fy in CHIP_FLIPS:
            for part in range(GW_PARTS):
                h_block = 2 * _lin(_flip(x, fx), _flip(y, fy), core) + part // per_h
                out.append(_h_block(h_block) * per_h + part % per_h)
    return jnp.stack(out).astype(jnp.int32)


def _grad_w_in_rs(cols, xt, dh):
    n_tile = 4 * GW_PARTS

    def body(cols_ref, a_ref, b_ref, q_ref, l1_ref, stage, landed, send_sems, recv_sems, sem):
        t = pl.program_id(0)
        sib = (lax.axis_index("x"), lax.axis_index("y"), 1 - lax.axis_index("c"))

        def there(n):
            return l1_ref.at[n // GW_PARTS, :, pl.ds((n % GW_PARTS) * GW_TN, GW_TN)]

        def send(n):
            return pltpu.make_async_remote_copy(src_ref=stage.at[n % 2], dst_ref=there(n), send_sem=send_sems.at[n],
                                                recv_sem=recv_sems.at[n], device_id=sib, device_id_type=MESH)

        def fetch(n):
            return pltpu.make_async_copy(there(n), landed, sem)

        for n in range(n_tile):
            @pl.when(t == n_tile + n)
            def _(n=n):
                if n == 0:
                    send(n_tile - 2).wait_send()
                    send(n_tile - 1).wait_send()
                send(n).wait_recv()
                fetch(n).start()

        part = _nn(a_ref[...], b_ref[...])

        for n in range(n_tile):
            @pl.when(t == n)
            def _(n=n):
                if n >= 2:
                    send(n - 2).wait_send()
                stage[n % 2] = part.astype(stage.dtype)
                send(n).start()

            @pl.when(t == n_tile + n)
            def _(n=n):
                fetch(n).wait()
                q_ref[...] = (part + landed[...].astype(F32)).astype(q_ref.dtype)

    mine = lambda t: jnp.maximum(t - n_tile, 0)
    grid_spec = pltpu.PrefetchScalarGridSpec(
        num_scalar_prefetch=1, grid=(2 * n_tile,),
        in_specs=[pl.BlockSpec((D, S), lambda t, cols: (0, 0), pipeline_mode=pl.Buffered(1)),
                  pl.BlockSpec((S, GW_TN), lambda t, cols: (0, cols[t]))],
        out_specs=[pl.BlockSpec((None, D, GW_TN), lambda t, cols: (mine(t) // GW_PARTS, 0, mine(t) % GW_PARTS)), ANY],
        scratch_shapes=[pltpu.VMEM((2, D, GW_TN), BF16), pltpu.VMEM((D, GW_TN), BF16),
                        pltpu.SemaphoreType.DMA((n_tile,)), pltpu.SemaphoreType.DMA((n_tile,)),
                        pltpu.SemaphoreType.DMA(())])
    q, _ = pl.pallas_call(
        body, name="grad_w_in_rs", grid_spec=grid_spec,
        out_shape=[jax.ShapeDtypeStruct((4, R_ALL, D), BF16), jax.ShapeDtypeStruct((4, D, D), BF16)],
        compiler_params=_cparams(1))(cols, xt, dh)
    return q


def _grad_w(name, at, b):
    m, n_all = at.shape[0], b.shape[1]

    def body(a_ref, b_ref, o_ref):
        o_ref[...] = _nn(a_ref[...], b_ref[...]).astype(o_ref.dtype)

    return pl.pallas_call(
        body, name=name, grid=(n_all // GW_TN,),
        in_specs=[pl.BlockSpec((m, S), lambda n: (0, 0), pipeline_mode=pl.Buffered(1)),
                  pl.BlockSpec((S, GW_TN), lambda n: (0, n))],
        out_specs=pl.BlockSpec((m, GW_TN), lambda n: (0, n)), out_shape=jax.ShapeDtypeStruct((m, n_all), BF16),
        compiler_params=_cparams(1))(at, b)


NR = 16
TI = 16
TM = NR * TI
NT = S // TM
ATT_QB = (256, 128, 256)
ATT_NB = (16, 8, 1)
ATT_BLOCKS = (16, 32, 16)


def _permute_tokens(a):
    return a.reshape(NT, TI, NR, a.shape[-1]).transpose(0, 2, 1, 3).reshape(a.shape)


def _attn_shape(g, c):
    if g == 0:
        return (S, c)
    if g == 1:
        return (NT, 4, 4, TI, c)
    return (NT, NR, TI, c)


def _attn_view(g, a):
    return a.reshape(_attn_shape(g, a.shape[-1]))


def _attn_spec(g, width, col, blk):
    if g == 0:
        return pl.BlockSpec((TM, width), lambda b: (blk(b), col))
    if g == 1:
        return pl.BlockSpec((2, 4, None, TI, width), lambda b: (blk(b) % 8, 0, blk(b) // 8, 0, col))
    return pl.BlockSpec((NT, None, TI, width), lambda b: (0, blk(b), 0, col))


def _pieces(g):
    if g == 1:
        return [(t, m) for t in range(2) for m in range(4)]
    return [(t,) for t in range(NT)]


def _get(g, ref, sl):
    if g == 0:
        return ref[:, sl]
    return jnp.concatenate([ref[(*p, slice(None), sl)] for p in _pieces(g)], axis=0)


def _put(g, ref, sl, val):
    if g == 0:
        ref[:, sl] = val
    else:
        for n, p in enumerate(_pieces(g)):
            ref[(*p, slice(None), sl)] = val[TI * n:TI * (n + 1)]


def _block_pos(g, a):
    if g == 0:
        return 16 * (a % 16) + a // 16
    if g == 1:
        return 64 * (a // 64) + 4 * (a % 16) + (a // 16) % 4
    return a


def _attn_mask(g, n):
    qb = ATT_QB[g]
    if ATT_NB[g] == 1:
        qa = lax.broadcasted_iota(jnp.int32, (qb, qb), 0)
        kc = lax.broadcasted_iota(jnp.int32, (qb, qb), 1)
        dist = _block_pos(g, qa) - _block_pos(g, kc)
        return (dist >= 0) & (dist <= QB)
    qa = lax.broadcasted_iota(jnp.int32, (qb, 2 * qb), 0)
    kc = lax.broadcasted_iota(jnp.int32, (qb, 2 * qb), 1)
    cur = kc >= qb
    dist = _block_pos(g, qa) - _block_pos(g, kc % qb) + jnp.where(cur, 0, qb)
    return (dist >= 0) & (dist <= QB) & (cur | (n > 0))


def _keys(g, prev_ref, cur_ref, sl):
    if ATT_NB[g] == 1:
        return _get(g, cur_ref, sl)
    return jnp.concatenate([_get(g, prev_ref, sl), _get(g, cur_ref, sl)], axis=0)


def _qkv_specs(g, clamp):
    cur = lambda col: _attn_spec(g, AW, col, clamp)
    prev = lambda col: _attn_spec(g, AW, col, lambda b: jnp.maximum(clamp(b) - 1, 0))
    qc, kc, vc = (c // AW + g for c in (COL_Q, COL_K, COL_V))
    return [cur(qc), cur(kc), prev(kc), cur(vc), prev(vc)]


def _attn_fwd(g, h):
    scale = HD ** -0.5
    hv = _attn_view(g, h)

    def body(q_ref, kc_ref, kp_ref, vc_ref, vp_ref, o_ref, l_ref):
        valid = _attn_mask(g, pl.program_id(0) % ATT_NB[g])
        for hh in range(NH):
            sl = slice(hh * HD, (hh + 1) * HD)
            kh, vh = _keys(g, kp_ref, kc_ref, sl), _keys(g, vp_ref, vc_ref, sl)
            s = jnp.where(valid, _nt(_get(g, q_ref, sl), kh) * scale, NEG_INF)
            m = jnp.max(s, axis=-1, keepdims=True)
            e = jnp.exp(s - m)
            den = jnp.sum(e, axis=-1, keepdims=True)
            _put(g, o_ref, sl, (_nn(e.astype(BF16), vh) * (1.0 / den)).astype(o_ref.dtype))
            _put(g, l_ref, slice(hh, hh + 1), m + jnp.log(den))

    same = lambda b: b
    o, lse = pl.pallas_call(
        body, name=f"attn_fwd_{g}", grid=(ATT_BLOCKS[g],),
        in_specs=_qkv_specs(g, same),
        out_specs=[_attn_spec(g, AW, 0, same), _attn_spec(g, NH, 0, same)],
        out_shape=[jax.ShapeDtypeStruct(_attn_shape(g, AW), BF16), jax.ShapeDtypeStruct(_attn_shape(g, NH), F32)],
        compiler_params=_cparams(1))(hv, hv, hv, hv, hv)
    return o.reshape(S, AW), lse.reshape(S, NH)


def _attn_bwd(g, dh, h, do, lse, delta):
    scale = HD ** -0.5
    qb = ATT_QB[g]
    carried = ATT_NB[g] > 1
    last = ATT_BLOCKS[g] - 1
    clamp = lambda b: jnp.minimum(b, last)
    behind = lambda b: jnp.maximum(b - 1, 0)
    hv = _attn_view(g, h)

    def body(q_ref, kc_ref, kp_ref, vc_ref, vp_ref, do_ref, l_ref, dl_ref, _, dh_ref, *carry):
        b = pl.program_id(0)

        def write(col, val):
            _put(g, dh_ref, slice(col, col + HD), val.astype(dh_ref.dtype))

        def block():
            valid = _attn_mask(g, b % ATT_NB[g])
            for hh in range(NH):
                sl = slice(hh * HD, (hh + 1) * HD)
                one = slice(hh, hh + 1)
                qh, doh = _get(g, q_ref, sl), _get(g, do_ref, sl)
                kh, vh = _keys(g, kp_ref, kc_ref, sl), _keys(g, vp_ref, vc_ref, sl)
                s = _nt(qh, kh) * scale
                p = jnp.where(valid, jnp.exp(s - _get(g, l_ref, one)), 0.0)
                ds = p * (_nt(doh, vh) - _get(g, dl_ref, one))
                dsb = (ds * scale).astype(BF16)
                dq = _nn(dsb, kh)
                dk2 = _tn(dsb, qh)
                dv2 = _tn(p.astype(BF16), doh)
                if carried:
                    cq_ref, ck_ref, cv_ref = carry
                    write(hh * HD, cq_ref[:, sl])
                    write(AW + hh * HD, ck_ref[:, sl] + dk2[:qb])
                    write(2 * AW + hh * HD, cv_ref[:, sl] + dv2[:qb])
                    cq_ref[:, sl] = dq
                    ck_ref[:, sl] = dk2[qb:]
                    cv_ref[:, sl] = dv2[qb:]
                else:
                    write(hh * HD, dq)
                    write(AW + hh * HD, dk2)
                    write(2 * AW + hh * HD, dv2)

        if not carried:
            block()
            return

        @pl.when(b == 0)
        def _():
            for ref in carry:
                ref[...] = jnp.zeros_like(ref)

        pl.when(b <= last)(block)

        @pl.when(b > last)
        def _():
            for hh in range(NH):
                for n, ref in enumerate(carry):
                    write(n * AW + hh * HD, ref[:, hh * HD:(hh + 1) * HD])

    out = pl.pallas_call(
        body, name=f"attn_bwd_{g}", grid=(ATT_BLOCKS[g] + carried,),
        in_specs=_qkv_specs(g, clamp) + [_attn_spec(g, AW, 0, clamp), _attn_spec(g, NH, 0, clamp),
                                         _attn_spec(g, NH, 0, clamp), ANY],
        out_specs=_attn_spec(g, 3 * AW, g, behind if carried else clamp),
        out_shape=jax.ShapeDtypeStruct(_attn_shape(g, NW), BF16),
        input_output_aliases={8: 0},
        scratch_shapes=[pltpu.VMEM((qb, AW), F32)] * (3 if carried else 0),
        compiler_params=_cparams(1))(hv, hv, hv, hv, hv, _attn_view(g, do), _attn_view(g, lse), _attn_view(g, delta),
                                     _attn_view(g, dh))
    return out.reshape(S, NW)


def _group_weights(l0, l1, l2):
    m = jnp.maximum(jnp.maximum(l0, l1), l2)
    e0, e1, e2 = jnp.exp(l0 - m), jnp.exp(l1 - m), jnp.exp(l2 - m)
    inv = 1.0 / (e0 + e1 + e2)
    return e0 * inv, e1 * inv, e2 * inv


def _residue(ref, r, sl):
    return ref[r * TI:(r + 1) * TI, sl].astype(F32)


def _total(parts):
    return functools.reduce(lambda x, y: x + y, parts)


def _pool_tokens(up_ref, uc_ref, p_ref, tile):
    j0 = lax.broadcasted_iota(jnp.int32, (TI, 1), 0) == 0
    first = (tile == 0) & j0
    for r in range(NR):
        out = []
        for g, w in enumerate(POOL_WINDOWS):
            sl = slice(g * PG, (g + 1) * PG)
            own = _residue(uc_ref, r, sl)
            acc = _total([own] + [_residue(uc_ref, r - k, sl) for k in range(1, min(r, w - 1) + 1)])
            wrapped = [NR + r - k for k in range(r + 1, w)]
            if wrapped:
                wc = _total([_residue(uc_ref, q, sl) for q in wrapped])
                wp = jnp.where(tile > 0, _total([_residue(up_ref, q, sl) for q in wrapped]), 0.0)
                acc = acc + jnp.where(j0, pltpu.roll(wp, 1, 0), pltpu.roll(wc, 1, 0))
            out.append(acc * jnp.where(first, 1.0 / min(r + 1, w), 1.0 / w) - own)
        p_ref[r * TI:(r + 1) * TI, :] = jnp.concatenate(out, axis=1).astype(p_ref.dtype)


def _pool_tokens_bwd(dp, nxt_ref, du_ref, tile):
    ji = lax.broadcasted_iota(jnp.int32, (TI, 1), 0)
    first = (tile == 0) & (ji == 0)
    piece = lambda g, r: dp[g][r * TI:(r + 1) * TI]
    dpc = [[piece(g, r) * jnp.where(first, 1.0 / min(r + 1, w), 1.0 / w) for r in range(NR)]
           for g, w in enumerate(POOL_WINDOWS)]
    for r in range(NR):
        out = []
        for g, w in enumerate(POOL_WINDOWS):
            sl = slice(g * PG, (g + 1) * PG)
            acc = _total([dpc[g][r + k] for k in range(w) if r + k < NR])
            wrapped = [r + k - NR for k in range(1, w) if r + k >= NR]
            if wrapped:
                wc = _total([dpc[g][q] for q in wrapped])
                wn = _total([nxt_ref[q * TI:(q + 1) * TI, sl] for q in wrapped])
                acc = acc + jnp.where(ji == TI - 1, pltpu.roll(wn, TI - 1, 0), pltpu.roll(wc, TI - 1, 0))
            out.append(acc - piece(g, r))
        du_ref[r * TI:(r + 1) * TI, :] = jnp.concatenate(out, axis=1).astype(du_ref.dtype)
    for r in range(NR):
        nxt_ref[r * TI:(r + 1) * TI, :] = jnp.concatenate([dpc[g][r] for g in range(len(POOL_WINDOWS))], axis=1)


def _pool_linear(pb, wpool_ref):
    return jnp.concatenate([_nn(pb[:, g * PG:(g + 1) * PG], wpool_ref[g]) for g in range(len(POOL_WINDOWS))], axis=1)


def _tok(width, col=0, rev=False):
    if rev:
        return pl.BlockSpec((TM, width), lambda i: (NT - 1 - i, col))
    return pl.BlockSpec((TM, width), lambda i: (i, col))


def _whole(shape):
    return pl.BlockSpec(shape, lambda i: (0,) * len(shape))


def _prep_x(x2):
    rows = 1024

    def body(x_ref, xb_ref, xt_ref):
        xb_ref[...] = x_ref[...].astype(BF16)
        xt_ref[...] = x_ref[...].T.astype(BF16)

    return pl.pallas_call(
        body, name="prep_x", grid=(S // rows,), in_specs=[pl.BlockSpec((rows, D), lambda i: (i, 0))],
        out_specs=[pl.BlockSpec((rows, D), lambda i: (i, 0)), pl.BlockSpec((D, rows), lambda i: (0, i))],
        out_shape=[jax.ShapeDtypeStruct((S, D), BF16), jax.ShapeDtypeStruct((D, S), BF16)],
        compiler_params=_cparams(1))(x2)


def _mix_fwd(h, o, lse, wpa, wpp, wpool, pscale, bgate):
    def body(o0_ref, o1_ref, o2_ref, l0_ref, l1_ref, l2_ref, za_ref, uc_ref, up_ref, zp_ref, gp_ref,
             wpa_ref, wpp_ref, wpool_ref, ps_ref, bg_ref,
             mg_ref, a_ref, b_ref, p_ref, yat_ref, ypt_ref, mgt_ref, ya_ref, yp_ref):
        i = pl.program_id(0)
        w0, w1, w2 = _group_weights(l0_ref[...], l1_ref[...], l2_ref[...])
        za = za_ref[...].astype(F32)
        silu_a = za * _sigmoid(za)
        for hh in range(NH):
            sl = slice(hh * HD, (hh + 1) * HD)
            c = slice(hh, hh + 1)
            oh = (w0[:, c] * o0_ref[:, sl].astype(F32) + w1[:, c] * o1_ref[:, sl].astype(F32)
                  + w2[:, c] * o2_ref[:, sl].astype(F32))
            ya = oh * silu_a[:, sl]
            ya_ref[:, sl] = ya.astype(BF16)
            yat_ref[sl, :] = ya.T.astype(BF16)
        _pool_tokens(up_ref, uc_ref, p_ref, i)
        zp = zp_ref[...].astype(F32)
        yp = _pool_linear(p_ref[...], wpool_ref) * ps_ref[...] * (zp * _sigmoid(zp))
        yp_ref[...] = yp.astype(BF16)
        ypt_ref[...] = yp.T.astype(BF16)
        a = _nn(ya_ref[...], wpa_ref[...])
        b = _nn(yp_ref[...], wpp_ref[...])
        a_ref[...] = a.astype(BF16)
        b_ref[...] = b.astype(BF16)
        gates = _sigmoid(gp_ref[...].astype(F32) + bg_ref[...])
        mg = gates[:, :D] * a + gates[:, D:] * b
        mg_ref[...] = mg.astype(BF16)
        mgt_ref[...] = mg.T.astype(BF16)

    u_prev = pl.BlockSpec((TM, AW), lambda i: (jnp.maximum(i - 1, 0), COL_U // AW))
    across = lambda width: pl.BlockSpec((width, TM), lambda i: (0, i))
    return pl.pallas_call(
        body, name="mix_fwd", grid=(NT,),
        in_specs=[_tok(AW)] * 3 + [_tok(NH)] * 3
        + [_tok(AW, COL_ZA // AW), _tok(AW, COL_U // AW), u_prev, _tok(AW, COL_ZP // AW), _tok(2 * D, COL_G // (2 * D))]
        + [_whole((AW, D)), _whole((AW, D)), _whole((4, PG, PG)), _whole((1, AW)), _whole((1, 2 * D))],
        out_specs=[_tok(D), _tok(D), _tok(D), _tok(AW), across(AW), across(AW), across(D)],
        out_shape=[jax.ShapeDtypeStruct((S, D), BF16)] * 3 + [jax.ShapeDtypeStruct((S, AW), BF16)]
        + [jax.ShapeDtypeStruct((AW, S), BF16)] * 2 + [jax.ShapeDtypeStruct((D, S), BF16)],
        scratch_shapes=[pltpu.VMEM((TM, AW), BF16), pltpu.VMEM((TM, AW), BF16)],
        compiler_params=_cparams(1))(*o, *lse, h, h, h, h, h, wpa, wpp, wpool, pscale, bgate)


def _out_ln(merged, x, target, wout, gamma, beta):
    def body(mg_ref, x_ref, t_ref, w_ref, g_ref, b_ref, dr_ref, drb_ref, dm_ref, loss_ref, dg_ref, db_ref):
        i = pl.program_id(0)

        @pl.when(i == 0)
        def _():
            loss_ref[...] = jnp.zeros_like(loss_ref)
            dg_ref[...] = jnp.zeros_like(dg_ref)
            db_ref[...] = jnp.zeros_like(db_ref)

        r = ALPHA * x_ref[...] + _nn(mg_ref[...], w_ref[...])
        mu = jnp.mean(r, axis=-1, keepdims=True)
        rc = r - mu
        rstd = lax.rsqrt(jnp.mean(rc * rc, axis=-1, keepdims=True) + LN_EPS)
        xhat = rc * rstd
        err = xhat * g_ref[...] + b_ref[...] - t_ref[...]
        loss_ref[...] += 0.5 * jnp.sum(jnp.mean(err * err, axis=-1, keepdims=True), axis=0, keepdims=True)
        dy = err * (1.0 / D)
        dg_ref[...] += jnp.sum(dy * xhat, axis=0, keepdims=True)
        db_ref[...] += jnp.sum(dy, axis=0, keepdims=True)
        dxh = dy * g_ref[...]
        dr = rstd * (dxh - jnp.mean(dxh, axis=-1, keepdims=True)
                     - xhat * jnp.mean(dxh * xhat, axis=-1, keepdims=True))
        dr_ref[...] = dr
        drb_ref[...] = dr.astype(BF16)
        dm_ref[...] = _nt(drb_ref[...], w_ref[...]).astype(BF16)

    return pl.pallas_call(
        body, name="out_ln", grid=(NT,),
        in_specs=[_tok(D), _tok(D), _tok(D), _whole((D, D)), _whole((1, D)), _whole((1, D))],
        out_specs=[_tok(D), _tok(D), _tok(D), _whole((8, 128)), _whole((1, D)), _whole((1, D))],
        out_shape=[jax.ShapeDtypeStruct((S, D), F32), jax.ShapeDtypeStruct((S, D), BF16),
                   jax.ShapeDtypeStruct((S, D), BF16), jax.ShapeDtypeStruct((8, 128), F32),
                   jax.ShapeDtypeStruct((1, D), F32), jax.ShapeDtypeStruct((1, D), F32)],
        compiler_params=_cparams(1))(merged, x, target, wout, gamma, beta)


def _gate_bwd(dm, a, b, h, bgate):
    def body(dm_ref, a_ref, b_ref, gp_ref, bg_ref, dgp_ref, da_ref, db_ref, dbg_ref):
        @pl.when(pl.program_id(0) == 0)
        def _():
            dbg_ref[...] = jnp.zeros_like(dbg_ref)

        dm_ = dm_ref[...].astype(F32)
        gates = _sigmoid(gp_ref[...].astype(F32) + bg_ref[...])
        ga, gb = gates[:, :D], gates[:, D:]
        da_ref[...] = (dm_ * ga).astype(BF16)
        db_ref[...] = (dm_ * gb).astype(BF16)
        dgp = jnp.concatenate([dm_ * a_ref[...].astype(F32) * ga * (1.0 - ga),
                               dm_ * b_ref[...].astype(F32) * gb * (1.0 - gb)], axis=1)
        dgp_ref[...] = dgp.astype(BF16)
        dbg_ref[...] += jnp.sum(dgp, axis=0, keepdims=True)

    return pl.pallas_call(
        body, name="gate_bwd", grid=(NT,),
        in_specs=[_tok(D), _tok(D), _tok(D), _tok(2 * D, COL_G // (2 * D)), _whole((1, 2 * D))],
        out_specs=[_tok(2 * D, DH_G // (2 * D)), _tok(D), _tok(D), _whole((1, 2 * D))],
        out_shape=[jax.ShapeDtypeStruct((S, NW), BF16), jax.ShapeDtypeStruct((S, D), BF16),
                   jax.ShapeDtypeStruct((S, D), BF16), jax.ShapeDtypeStruct((1, 2 * D), F32)],
        compiler_params=_cparams(1))(dm, a, b, h, bgate)


def _mix_bwd(dh, da, db, h, o, lse, p, wpa, wpp, wpool, pscale):
    def body(_, da_ref, db_ref, o0_ref, o1_ref, o2_ref, l0_ref, l1_ref, l2_ref, za_ref, zp_ref, p_ref,
             wpa_ref, wpp_ref, wpool_ref, ps_ref,
             dh_ref, do0_ref, do1_ref, do2_ref, dl0_ref, dl1_ref, dl2_ref, dwp_ref, dps_ref,
             nxt_ref):
        i = pl.program_id(0)
        tile = NT - 1 - i
        dza_ref, du_ref, dzp_ref = (dh_ref.at[:, pl.ds(n * AW, AW)] for n in range(3))

        @pl.when(i == 0)
        def _():
            nxt_ref[...] = jnp.zeros_like(nxt_ref)
            dwp_ref[...] = jnp.zeros_like(dwp_ref)
            dps_ref[...] = jnp.zeros_like(dps_ref)

        dya = _nt(da_ref[...], wpa_ref[...])
        w0, w1, w2 = _group_weights(l0_ref[...], l1_ref[...], l2_ref[...])
        za = za_ref[...].astype(F32)
        sig = _sigmoid(za)
        silu_a = za * sig
        dsilu_a = sig * (1.0 + za * (1.0 - sig))
        for hh in range(NH):
            sl = slice(hh * HD, (hh + 1) * HD)
            c = slice(hh, hh + 1)
            oh = (w0[:, c] * o0_ref[:, sl].astype(F32) + w1[:, c] * o1_ref[:, sl].astype(F32)
                  + w2[:, c] * o2_ref[:, sl].astype(F32))
            doh = dya[:, sl] * silu_a[:, sl]
            dza_ref[:, sl] = (dya[:, sl] * oh * dsilu_a[:, sl]).astype(BF16)
            dot_ = jnp.sum(doh * oh, axis=-1, keepdims=True)
            do0_ref[:, sl] = (w0[:, c] * doh).astype(BF16)
            do1_ref[:, sl] = (w1[:, c] * doh).astype(BF16)
            do2_ref[:, sl] = (w2[:, c] * doh).astype(BF16)
            dl0_ref[:, c] = w0[:, c] * dot_
            dl1_ref[:, c] = w1[:, c] * dot_
            dl2_ref[:, c] = w2[:, c] * dot_
        dyp = _nt(db_ref[...], wpp_ref[...])
        pb = p_ref[...]
        pw = _pool_linear(pb, wpool_ref)
        zp = zp_ref[...].astype(F32)
        sigp = _sigmoid(zp)
        dypre = dyp * (zp * sigp)
        dzp_ref[...] = (dyp * (pw * ps_ref[...]) * (sigp * (1.0 + zp * (1.0 - sigp)))).astype(BF16)
        dps_ref[...] += jnp.sum(dypre * pw, axis=0, keepdims=True)
        dpw = (dypre * ps_ref[...]).astype(BF16)
        dp = []
        for g in range(len(POOL_WINDOWS)):
            sl = slice(g * PG, (g + 1) * PG)
            dwp_ref[g] += _tn(pb[:, sl], dpw[:, sl])
            dp.append(_nt(dpw[:, sl], wpool_ref[g]))
        _pool_tokens_bwd(dp, nxt_ref, du_ref, tile)

    r = functools.partial(_tok, rev=True)
    return pl.pallas_call(
        body, name="mix_bwd", grid=(NT,),
        in_specs=[ANY, r(D), r(D)] + [r(AW)] * 3 + [r(NH)] * 3 + [r(AW, COL_ZA // AW), r(AW, COL_ZP // AW), r(AW)]
        + [_whole((AW, D)), _whole((AW, D)), _whole((4, PG, PG)), _whole((1, AW))],
        out_specs=[r(3 * AW, DH_Z // (3 * AW))] + [r(AW)] * 3 + [r(NH)] * 3 + [_whole((4, PG, PG)), _whole((1, AW))],
        out_shape=[jax.ShapeDtypeStruct((S, NW), BF16)] + [jax.ShapeDtypeStruct((S, AW), BF16)] * 3
        + [jax.ShapeDtypeStruct((S, NH), F32)] * 3
        + [jax.ShapeDtypeStruct((4, PG, PG), F32), jax.ShapeDtypeStruct((1, AW), F32)],
        input_output_aliases={0: 0},
        scratch_shapes=[pltpu.VMEM((TM, AW), F32)],
        compiler_params=_cparams(1))(dh, da, db, *o, *lse, h, h, p, wpa, wpp, wpool, pscale)


def _adamw(w, g, m, v):
    m = B1 * m + (1.0 - B1) * g
    v = B2 * v + (1.0 - B2) * jnp.square(g)
    m_hat = m / (1.0 - B1 ** STEP)
    v_hat = v / (1.0 - B2 ** STEP)
    return -LR * (m_hat / (jnp.sqrt(v_hat) + EPS) + WD * w), m, v


def _adam_shard(name, q, l2, w, m, v, tr):
    rows = w.shape[0]

    def body(q_ref, l_ref, w_ref, m_ref, v_ref, g_out, d_out, m_out, v_out):
        g = q_ref[...].astype(F32)
        for k in range(2):
            g = g + l_ref[k].astype(F32)
        g_out[...] = g
        d_out[...], m_out[...], v_out[...] = _adamw(w_ref[...], g, m_ref[...], v_ref[...])

    blk = pl.BlockSpec((tr, D), lambda i: (i, 0))
    return pl.pallas_call(
        body, name=name, grid=(rows // tr,),
        in_specs=[pl.BlockSpec((None, tr, D), lambda i: (0, i, 0)), pl.BlockSpec((2, tr, D), lambda i: (0, i, 0)),
                  blk, blk, blk],
        out_specs=[blk] * 4, out_shape=[jax.ShapeDtypeStruct((rows, D), F32)] * 4,
        compiler_params=_cparams(1))(q, l2, w, m, v)


def _sum_small(q, l2):
    def body(q_ref, l_ref, g_out, buf, sems):
        rows = pl.ds(R_OUT, R_SMALL)
        copies = [pltpu.make_async_copy(src, buf.at[n], sems.at[n])
                  for n, src in enumerate((q_ref.at[0, rows], l_ref.at[0, rows], l_ref.at[1, rows]))]
        for cp in copies:
            cp.start()
        for cp in copies:
            cp.wait()
        g_out[...] = buf[0].astype(F32) + buf[1].astype(F32) + buf[2].astype(F32)

    return pl.pallas_call(
        body, name="sum_small", in_specs=[ANY, ANY], out_shape=jax.ShapeDtypeStruct((R_SMALL, D), F32),
        scratch_shapes=[pltpu.VMEM((3, R_SMALL, D), q.dtype), pltpu.SemaphoreType.DMA((3,))],
        compiler_params=pltpu.CompilerParams(vmem_limit_bytes=VMEM_LIMIT))(q, l2)


def _adam_whole(name, grads, weights, ms, vs):
    n = len(grads)

    def body(*refs):
        ins, outs = refs[:4 * n], refs[4 * n:]
        for t in range(n):
            g, w, m, v = (ins[k * n + t][...] for k in range(4))
            outs[t][...], outs[n + t][...], outs[2 * n + t][...] = _adamw(w, g, m, v)

    out = pl.pallas_call(
        body, name=name, out_shape=[jax.ShapeDtypeStruct(w.shape, F32) for w in weights] * 3,
        compiler_params=pltpu.CompilerParams(vmem_limit_bytes=VMEM_LIMIT))(*grads, *weights, *ms, *vs)
    return out[:n], out[n:2 * n], out[2 * n:]


def _sum_replicated(gathered):
    def body(g_ref, bg_out, ps_out, gm_out, bt_out, loss_out):
        g = g_ref[0]
        for k in range(1, N_DEV):
            g = g + g_ref[k]
        bg_out[...] = jnp.concatenate([g[0:1], g[1:2]], axis=1)
        gm_out[...] = g[2:3]
        bt_out[...] = g[3:4]
        ps_out[...] = g[4:5, :AW]
        loss_out[...] = jnp.broadcast_to(g[5:6, :128], loss_out.shape)

    return pl.pallas_call(
        body, name="sum_replicated",
        out_shape=[jax.ShapeDtypeStruct(shape, F32) for shape in ((1, 2 * D), (1, AW), (1, D), (1, D), (8, 128))],
        compiler_params=pltpu.CompilerParams(vmem_limit_bytes=VMEM_LIMIT))(gathered)


def _pack_small(w_out, w_pa, w_pp, w_pool):
    return jnp.concatenate([w_out, w_pa.reshape(-1, D), w_pp.reshape(-1, D), w_pool.reshape(-1, D)], axis=0)


def _unpack_small(a):
    o = R_OUT
    return (a[:R_PA - o], a[R_PA - o:R_PP - o].reshape(AW, 256), a[R_PP - o:R_PL - o].reshape(AW, 256),
            a[R_PL - o:].reshape(4, 32, PG))


def _pack_vec(b_gate, gamma, beta, pscale, extra):
    z = jnp.zeros((D,), F32)
    return jnp.stack([b_gate[:D], b_gate[D:], gamma, beta, jnp.concatenate([pscale, z[:D - AW]]),
                      jnp.broadcast_to(extra, (D,)), z, z])


def kernel(x, w_in, b_gate, w_pool, pool_scale, w_proj_attn, w_proj_pool, w_out, ln_gamma, ln_beta, loss_target, m_w_in, m_b_gate, m_w_pool, m_pool_scale, m_w_proj_attn, m_w_proj_pool, m_w_out, m_ln_gamma, m_ln_beta, v_w_in, v_b_gate, v_w_pool, v_pool_scale, v_w_proj_attn, v_w_proj_pool, v_w_out, v_ln_gamma, v_ln_beta):
    coords = jnp.stack([lax.axis_index("x"), lax.axis_index("y"), lax.axis_index("c")]).astype(jnp.int32)
    x2, tgt = _permute_tokens(x[0]), _permute_tokens(loss_target[0])
    xb, xt = _prep_x(x2)

    pack = jnp.concatenate([w_in[0].astype(BF16),
                            _pack_small(w_out[0], w_proj_attn[0], w_proj_pool[0], w_pool[0]).astype(BF16)], axis=0)
    h, gw = _ag_proj(_arrival_order(*coords), xb, pack)
    wout = gw[:, R_OUT:R_PA].reshape(D, D)
    wpa = gw[:, R_PA:R_PP].reshape(N_DEV, AW, 256).transpose(1, 0, 2).reshape(AW, D)
    wpp = gw[:, R_PP:R_PL].reshape(N_DEV, AW, 256).transpose(1, 0, 2).reshape(AW, D)
    wpool = gw[:, R_PL:].reshape(N_DEV, 4, 32, PG).transpose(1, 0, 2, 3).reshape(4, PG, PG)

    o, lse = zip(*[_attn_fwd(g, h) for g in range(len(DILATIONS))])
    merged, a, b, p, yat, ypt, mgt = _mix_fwd(h, o, lse, wpa, wpp, wpool, pool_scale, b_gate)
    dr, drb, dm, loss_part, dgamma, dbeta = _out_ln(merged, x2, tgt, wout, ln_gamma, ln_beta)

    dh, da, db, dbgate = _gate_bwd(dm, a, b, h, b_gate)
    dh, do0, do1, do2, dl0, dl1, dl2, dwpool, dpscale = _mix_bwd(
        dh, da, db, h, o, lse, p, wpa, wpp, wpool, pool_scale)
    for g, (do_g, dl_g) in enumerate(zip((do0, do1, do2), (dl0, dl1, dl2))):
        dh = _attn_bwd(g, dh, h, do_g, lse[g], dl_g)

    q = _grad_w_in_rs(_rs_columns(*coords), xt, dh)
    d_wout = _grad_w("grad_w_out", mgt, drb)
    d_wpa = _grad_w("grad_w_pa", yat, da)
    d_wpp = _grad_w("grad_w_pp", ypt, db)
    small = jnp.concatenate([
        d_wout.reshape(N_DEV, 256, D),
        d_wpa.reshape(AW, N_DEV, 256).transpose(1, 0, 2).reshape(N_DEV, -1, D),
        d_wpp.reshape(AW, N_DEV, 256).transpose(1, 0, 2).reshape(N_DEV, -1, D),
        dwpool.astype(BF16).reshape(4, N_DEV, 32, PG).transpose(1, 0, 2, 3).reshape(N_DEV, -1, D)], axis=1)
    q = _pair_sum_small(coords, small, _rs_sibling(small), q)

    vec = _pack_vec(dbgate[0], dgamma[0], dbeta[0], dpscale[0], loss_part[0, 0])
    grad_x, l2, vecs_all = _grad_x_rs(dh, gw, dr, q, vec)
    g_in, d_in, m_in, v_in = _adam_shard("adam_w_in", q, l2, w_in[0], m_w_in[0], v_w_in[0], 256)
    g_small = [t.reshape(w.shape) for t, w in zip(_unpack_small(_sum_small(q, l2)),
                                                  (w_out, w_proj_attn, w_proj_pool, w_pool))]
    small = (g_small,) + _adam_whole("adam_small", g_small, (w_out, w_proj_attn, w_proj_pool, w_pool),
                                     (m_w_out, m_w_proj_attn, m_w_proj_pool, m_w_pool),
                                     (v_w_out, v_w_proj_attn, v_w_proj_pool, v_w_pool))

    *g_vec, loss = _sum_replicated(vecs_all)
    vecs = (g_vec,) + _adam_whole("adam_replicated", g_vec, (b_gate, pool_scale, ln_gamma, ln_beta),
                                  (m_b_gate, m_pool_scale, m_ln_gamma, m_ln_beta),
                                  (v_b_gate, v_pool_scale, v_ln_gamma, v_ln_beta))
    loss = loss[0, 0]

    def leaves(kind, big):
        out, pa, pp, pool = small[kind]
        bg, ps, gm, bt = vecs[kind]
        return [big[None], bg, pool, ps, pa, pp, out, gm, bt]

    return (loss, _permute_tokens(grad_x)[None], *leaves(0, g_in), *leaves(1, d_in), *leaves(2, m_in), *leaves(3, v_in))
```

```python
import functools

import jax
import jax.numpy as jnp
from jax import lax
from jax.experimental import pallas as pl
from jax.experimental.pallas import tpu as pltpu

F32 = jnp.float32
BF16 = jnp.bfloat16

S = 4096
D = 2048
NW = 16384
AW = 1024
HD = 128
NH = 8
QB = 128
DILATIONS = (1, 4, 16)
POOL_WINDOWS = (2, 4, 8, 16)
PG = 256
N_DEV = 8
COL_Q, COL_K, COL_V = 0, 3 * AW, 6 * AW
COL_ZA, COL_U, COL_ZP, COL_G = 9 * AW, 10 * AW, 11 * AW, 12 * AW
DH_Z, DH_G = COL_ZA, COL_G
ALPHA = 2.0 ** 0.25
LN_EPS = 1e-5
NEG_INF = -1e30
LR, B1, B2, EPS, WD, STEP = 0.001, 0.9, 0.999, 1e-08, 0.01, 10
R_IN, R_OUT, R_PA, R_PP, R_PL = 0, 2048, 2304, 2432, 2560
R_ALL = 2576
R_SMALL = R_ALL - R_OUT
VMEM_LIMIT = 56 * 1024 * 1024
MESH = pl.DeviceIdType.MESH
ANY = pl.BlockSpec(memory_space=pl.ANY)


def _cparams(n_axes):
    return pltpu.CompilerParams(dimension_semantics=("arbitrary",) * n_axes, vmem_limit_bytes=VMEM_LIMIT)


def _sigmoid(z):
    return 0.5 * jnp.tanh(0.5 * z) + 0.5


def _nt(a, b):
    return lax.dot_general(a, b, (((1,), (1,)), ((), ())), preferred_element_type=F32)


def _tn(a, b):
    return lax.dot_general(a, b, (((0,), (0,)), ((), ())), preferred_element_type=F32)


def _nn(a, b):
    return jnp.dot(a, b, preferred_element_type=F32)


def _lin(x, y, c):
    return 4 * x + 2 * y + c


def _flip(v, f):
    return 1 - v if f else v


CHIP_FLIPS = ((0, 0), (1, 0), (0, 1), (1, 1))


AG_PIECES = ((pl.ds(R_IN, D), pl.ds(0, 1024)), (pl.ds(R_IN, D), pl.ds(1024, 1024)),
             (pl.ds(R_OUT, R_PA - R_OUT), pl.ds(0, D)), (pl.ds(R_PA, R_ALL - R_PA), pl.ds(0, D)))
N_PIECES = len(AG_PIECES)
SIB, TO_X, TO_Y, ON, PASS_X, PASS_Y, PASS_D = range(7)
AG_TILES = ((0, 0), (0, 1), (1, 0), (1, 1), (2, 0), (4, 0), (3, 0), (5, 0),
            (2, 1), (4, 1), (3, 1), (5, 1), (6, 0), (6, 1), (7, 0), (7, 1))
W, G = "wait", "go"
AG_STEPS = {
    2: [(W, SIB, 0)], 3: [(W, SIB, 1)],
    4: [(W, TO_X, 0), (G, ON, 0), (G, PASS_X, 0)], 5: [(W, TO_Y, 0), (G, PASS_Y, 0)],
    6: [(W, PASS_X, 0)], 7: [(W, PASS_Y, 0)],
    8: [(W, TO_X, 1), (G, PASS_X, 1), (W, TO_Y, 1), (G, ON, 1), (G, PASS_Y, 1),
        (G, TO_X, 2), (G, TO_X, 3), (G, TO_Y, 2), (G, TO_Y, 3)],
    10: [(W, PASS_X, 1)], 11: [(W, PASS_Y, 1)],
    12: [(W, ON, 0), (G, PASS_D, 0)], 13: [(W, ON, 1), (G, PASS_D, 1)],
    14: [(W, PASS_D, 0), (W, TO_X, 2), (G, ON, 2), (G, PASS_X, 2), (W, TO_X, 3), (G, PASS_X, 3),
         (W, TO_Y, 2), (G, PASS_Y, 2), (W, TO_Y, 3), (G, ON, 3), (G, PASS_Y, 3)],
    15: [(W, PASS_D, 1)],
}
AG_LAST = [(W, SIB, 2), (W, SIB, 3), (W, ON, 2), (G, PASS_D, 2), (W, ON, 3), (G, PASS_D, 3),
           (W, PASS_X, 2), (W, PASS_X, 3), (W, PASS_Y, 2), (W, PASS_Y, 3), (W, PASS_D, 2), (W, PASS_D, 3)]


def _arrival_order(x, y, c):
    chips = [(x, y), (1 - x, y), (x, 1 - y), (1 - x, 1 - y)]
    return jnp.stack([_lin(px, py, pc) for px, py in chips for pc in (c, 1 - c)]).astype(jnp.int32)


def _ag_proj(order, xb, wbf, spack):
    tm, tn = 1024, 1024
    nrow, ntile = S // tm, len(AG_TILES)
    slabs = jnp.stack([order[pos] for pos, _ in AG_TILES])
    cols = jnp.stack([2 * order[pos] + half for pos, half in AG_TILES])

    def body(cols_ref, slabs_ref, x_ref, w_ref, s_ref, h_ref, gw_ref, wbuf, wsem, send_sems, recv_sems, local_sems):
        t, i = pl.program_id(0), pl.program_id(1)
        x, y, c = lax.axis_index("x"), lax.axis_index("y"), lax.axis_index("c")
        me = _lin(x, y, c)
        dev = {"sib": (x, y, 1 - c), "x": (1 - x, y, c), "y": (x, 1 - y, c), "d": (1 - x, 1 - y, c)}

        def slab_of(name, other_core=False):
            px, py, pc = dev[name]
            return _lin(px, py, 1 - pc if other_core else pc)

        def own(piece):
            rows, colz = AG_PIECES[piece]
            return w_ref.at[:, colz] if piece < 2 else s_ref.at[pl.ds(rows.start - R_OUT, rows.size)]

        def rdma(slab, kind, piece, to, from_own=False):
            k = kind * N_PIECES + piece
            there = gw_ref.at[(slab, *AG_PIECES[piece])]
            return pltpu.make_async_remote_copy(
                src_ref=own(piece) if from_own else there, dst_ref=there,
                send_sem=send_sems.at[k], recv_sem=recv_sems.at[k], device_id=dev[to], device_id_type=MESH)

        def mine(kind, piece):
            if kind in (SIB, TO_X, TO_Y):
                return rdma(me, kind, piece, ("sib", "x", "y")[kind], from_own=True)
            if kind == ON:
                frm, to = ("x", "y") if piece % 2 == 0 else ("y", "x")
                return rdma(slab_of(frm), kind, piece, to)
            return rdma(slab_of({PASS_X: "x", PASS_Y: "y", PASS_D: "d"}[kind]), kind, piece, "sib")

        def landing(kind, piece):
            slab = {SIB: slab_of("sib"), TO_X: slab_of("x"), TO_Y: slab_of("y"), ON: slab_of("d"),
                    PASS_X: slab_of("x", True), PASS_Y: slab_of("y", True), PASS_D: slab_of("d", True)}[kind]
            return rdma(slab, kind, piece, "sib")

        def run(steps):
            for what, kind, piece in steps:
                if what == W:
                    landing(kind, piece).wait_recv()
                else:
                    mine(kind, piece).start()

        local = [pltpu.make_async_copy(w_ref, gw_ref.at[me, pl.ds(R_IN, D)], local_sems.at[0]),
                 pltpu.make_async_copy(s_ref, gw_ref.at[me, pl.ds(R_OUT, R_SMALL)], local_sems.at[1])]

        def fetch(slab, half, slot):
            src = own(half) if slab is None else gw_ref.at[(slab, *AG_PIECES[half])]
            return pltpu.make_async_copy(src, wbuf.at[slot], wsem.at[slot])

        @pl.when((t == 0) & (i == 0))
        def _():
            for cp in local:
                cp.start()
            run([(G, kind, piece) for piece in (0, 1) for kind in (TO_X, TO_Y, SIB)] + [(G, SIB, 2), (G, SIB, 3)])
            first = fetch(None, 0, 0)
            first.start()
            first.wait()

        for nxt in range(1, ntile):
            @pl.when((t == nxt - 1) & (i == nrow - 1))
            def _(nxt=nxt):
                run(AG_STEPS.get(nxt, []))
                fetch(None if AG_TILES[nxt][0] == 0 else slabs_ref[nxt], AG_TILES[nxt][1], nxt % 2).start()

        for slot in (0, 1):
            @pl.when(t % 2 == slot)
            def _(slot=slot):
                @pl.when((i == 0) & (t > 0))
                def _():
                    fetch(None, 0, slot).wait()
                h_ref[...] = _nn(x_ref[...], wbuf[slot]).astype(h_ref.dtype)

        @pl.when((t == ntile - 1) & (i == nrow - 1))
        def _():
            run(AG_LAST)
            for kind in range(7):
                for piece in range(N_PIECES):
                    mine(kind, piece).wait_send()
            for cp in local:
                cp.wait()

    n_sem = 7 * N_PIECES
    grid_spec = pltpu.PrefetchScalarGridSpec(
        num_scalar_prefetch=2, grid=(ntile, nrow),
        in_specs=[pl.BlockSpec((tm, D), lambda t, i, cols, slabs: (i, 0)), ANY, ANY],
        out_specs=[pl.BlockSpec((tm, tn), lambda t, i, cols, slabs: (i, cols[t])), ANY],
        scratch_shapes=[pltpu.VMEM((2, D, tn), BF16), pltpu.SemaphoreType.DMA((2,)),
                        pltpu.SemaphoreType.DMA((n_sem,)), pltpu.SemaphoreType.DMA((n_sem,)),
                        pltpu.SemaphoreType.DMA((2,))])
    return pl.pallas_call(
        body, name="ag_proj", grid_spec=grid_spec,
        out_shape=[jax.ShapeDtypeStruct((S, NW), BF16), jax.ShapeDtypeStruct((N_DEV, R_ALL, D), BF16)],
        compiler_params=_cparams(2))(cols, slabs, xb, wbf, spack)


def _rs_sibling(p):
    n = len(CHIP_FLIPS)

    def body(p_ref, l_ref, send_sems, recv_sems):
        x, y, c = lax.axis_index("x"), lax.axis_index("y"), lax.axis_index("c")
        copies = [pltpu.make_async_remote_copy(
            src_ref=p_ref.at[_lin(_flip(x, fx), _flip(y, fy), 1 - c)], dst_ref=l_ref.at[k],
            send_sem=send_sems.at[k], recv_sem=recv_sems.at[k], device_id=(x, y, 1 - c), device_id_type=MESH)
            for k, (fx, fy) in enumerate(CHIP_FLIPS)]
        for cp in copies:
            cp.start()
        for cp in copies:
            cp.wait_recv()
        for cp in copies:
            cp.wait_send()

    return pl.pallas_call(
        body, name="rs_sibling", out_shape=jax.ShapeDtypeStruct((n,) + p.shape[1:], p.dtype),
        in_specs=[ANY], out_specs=ANY,
        scratch_shapes=[pltpu.SemaphoreType.DMA((n,)), pltpu.SemaphoreType.DMA((n,))])(p)


def _pair_sum_small(coords, p, l1, q):
    def body(crd, p_ref, l_ref, _, q_ref, buf, sem):
        k = pl.program_id(0)
        buf[...] = (p_ref[...].astype(F32) + l_ref[...].astype(F32)).astype(buf.dtype)
        out = pltpu.make_async_copy(buf, q_ref.at[k, pl.ds(R_OUT, R_SMALL)], sem)
        out.start()
        out.wait()

    def p_map(k, crd):
        fx, fy = k % 2, k // 2
        px = crd[0] + fx - 2 * fx * crd[0]
        py = crd[1] + fy - 2 * fy * crd[1]
        return (_lin(px, py, crd[2]), 0, 0)

    grid_spec = pltpu.PrefetchScalarGridSpec(
        num_scalar_prefetch=1, grid=(4,),
        in_specs=[pl.BlockSpec((None, R_SMALL, D), p_map),
                  pl.BlockSpec((None, R_SMALL, D), lambda k, crd: (k, 0, 0)), ANY],
        out_specs=ANY,
        scratch_shapes=[pltpu.VMEM((R_SMALL, D), BF16), pltpu.SemaphoreType.DMA(())])
    return pl.pallas_call(body, name="pair_sum_small", grid_spec=grid_spec,
                          out_shape=jax.ShapeDtypeStruct(q.shape, q.dtype), input_output_aliases={3: 0},
                          compiler_params=_cparams(1))(coords, p, l1, q)


def _h_block(k):
    return jnp.where(k < 9, (k % 3) * 3 + k // 3, k)


RS_PIECES = (pl.ds(0, 1280), pl.ds(1280, R_ALL - 1280))
RS_ROWS = (1280, R_ALL - 1280)
RS_CHUNKS = ((320,) * 4, (432,) * 3)
RS_MERGE_STEP = 2


def _grad_x_rs(dh, g, dr, q, vec):
    others = [(fx, fy, fc) for fx in (0, 1) for fy in (0, 1) for fc in (0, 1) if (fx, fy, fc) != (0, 0, 0)]
    tm, tk = 1024, 1024
    ni, nk = S // tm, NW // tk
    rmax = max(RS_ROWS)
    cmax = max(max(c) for c in RS_CHUNKS)

    def body(dh_ref, w_ref, dr_ref, q_ref, vec_ref, o_ref, l2_ref, ld_ref, mg_ref, all_ref, va, vb,
             send_sems, recv_sems, sems):
        i, k = pl.program_id(0), pl.program_id(1)
        x, y, c = lax.axis_index("x"), lax.axis_index("y"), lax.axis_index("c")
        nbr = ((1 - x, y, c), (x, 1 - y, c))

        def vec_copy(n, sender):
            sx, sy, sc = sender
            fx, fy, fc = others[n]
            return pltpu.make_async_remote_copy(
                src_ref=vec_ref, dst_ref=all_ref.at[_lin(sx, sy, sc)], send_sem=send_sems.at[6 + n],
                recv_sem=recv_sems.at[6 + n], device_id=(_flip(sx, fx), _flip(sy, fy), _flip(sc, fc)),
                device_id_type=MESH)

        vec_own = pltpu.make_async_copy(vec_ref, all_ref.at[_lin(x, y, c)], sems.at[3])

        def rows(ref, piece):
            return ref.at[piece, pl.ds(0, RS_ROWS[piece])]

        copies = (
            (q_ref.at[3, RS_PIECES[0]], rows(ld_ref, 0), 0),
            (q_ref.at[3, RS_PIECES[1]], rows(ld_ref, 1), 1),
            (q_ref.at[1, RS_PIECES[0]], l2_ref.at[0, RS_PIECES[0]], 0),
            (q_ref.at[2, RS_PIECES[1]], l2_ref.at[1, RS_PIECES[1]], 1),
            (rows(mg_ref, 0), l2_ref.at[1, RS_PIECES[0]], 1),
            (rows(mg_ref, 1), l2_ref.at[0, RS_PIECES[1]], 0),
        )

        def copy(n):
            src, dst, axis = copies[n]
            return pltpu.make_async_remote_copy(src_ref=src, dst_ref=dst, send_sem=send_sems.at[n],
                                                recv_sem=recv_sems.at[n], device_id=nbr[axis], device_id_type=MESH)

        def merge(piece, mine):
            start = 0
            for n_rows in RS_CHUNKS[piece]:
                own = pltpu.make_async_copy(q_ref.at[mine, pl.ds(RS_PIECES[piece].start + start, n_rows)],
                                            va.at[pl.ds(0, n_rows)], sems.at[0])
                got = pltpu.make_async_copy(ld_ref.at[piece, pl.ds(start, n_rows)], vb.at[pl.ds(0, n_rows)], sems.at[1])
                own.start()
                got.start()
                own.wait()
                got.wait()
                va[pl.ds(0, n_rows)] = (va[pl.ds(0, n_rows)].astype(F32)
                                        + vb[pl.ds(0, n_rows)].astype(F32)).astype(va.dtype)
                out = pltpu.make_async_copy(va.at[pl.ds(0, n_rows)], mg_ref.at[piece, pl.ds(start, n_rows)], sems.at[2])
                out.start()
                out.wait()
                start += n_rows

        @pl.when((i == 0) & (k == 0))
        def _():
            for n in range(4):
                copy(n).start()
            vec_own.start()
            for n in range(len(others)):
                vec_copy(n, (x, y, c)).start()

        @pl.when((i == RS_MERGE_STEP) & (k == 0))
        def _():
            copy(0).wait_recv()
            merge(0, 2)
            copy(4).start()
            copy(1).wait_recv()
            merge(1, 1)
            copy(5).start()

        @pl.when(k == 0)
        def _():
            o_ref[...] = ALPHA * dr_ref[...]

        o_ref[...] += _nt(dh_ref[...], w_ref[...])

        @pl.when((i == ni - 1) & (k == nk - 1))
        def _():
            for n in range(2, 6):
                copy(n).wait_recv()
            for n in range(6):
                copy(n).wait_send()
            for n, (fx, fy, fc) in enumerate(others):
                vec_copy(n, (_flip(x, fx), _flip(y, fy), _flip(c, fc))).wait_recv()
                vec_copy(n, (x, y, c)).wait_send()
            vec_own.wait()

    slab = q.shape[1:]
    out = pl.pallas_call(
        body, name="grad_x_rs", grid=(ni, nk),
        in_specs=[pl.BlockSpec((tm, tk), lambda i, k: (i, k)),
                  pl.BlockSpec((None, D, tk), lambda i, k: (_h_block(k) // 2, 0, _h_block(k) % 2)),
                  pl.BlockSpec((tm, D), lambda i, k: (i, 0)), ANY, ANY],
        out_specs=[pl.BlockSpec((tm, D), lambda i, k: (i, 0)), ANY, ANY, ANY, ANY],
        out_shape=[jax.ShapeDtypeStruct((S, D), F32), jax.ShapeDtypeStruct((2,) + slab, q.dtype),
                   jax.ShapeDtypeStruct((2, rmax, slab[1]), q.dtype), jax.ShapeDtypeStruct((2, rmax, slab[1]), q.dtype),
                   jax.ShapeDtypeStruct((N_DEV,) + vec.shape, vec.dtype)],
        scratch_shapes=[pltpu.VMEM((cmax, slab[1]), q.dtype), pltpu.VMEM((cmax, slab[1]), q.dtype),
                        pltpu.SemaphoreType.DMA((6 + len(others),)), pltpu.SemaphoreType.DMA((6 + len(others),)),
                        pltpu.SemaphoreType.DMA((4,))],
        compiler_params=_cparams(2))(dh, g, dr, q, vec)
    return out[0], out[1], out[4]


GW_TN = 512
GW_PARTS = D // GW_TN


def _rs_columns(x, y, c):
    per_h = AW // GW_TN
    out = []
    for core in (1 - c, c):
        for fx, fy in CHIP_FLIPS:
            for part in range(GW_PARTS):
                h_block = 2 * _lin(_flip(x, fx), _flip(y, fy), core) + part // per_h
                out.append(_h_block(h_block) * per_h + part % per_h)
    return jnp.stack(out).astype(jnp.int32)


def _grad_w_in_rs(cols, xt, dh):
    n_tile = 4 * GW_PARTS

    def body(cols_ref, a_ref, b_ref, q_ref, l1_ref, stage, landed, send_sems, recv_sems, sem):
        t = pl.program_id(0)
        sib = (lax.axis_index("x"), lax.axis_index("y"), 1 - lax.axis_index("c"))

        def there(n):
            return l1_ref.at[n // GW_PARTS, :, pl.ds((n % GW_PARTS) * GW_TN, GW_TN)]

        def send(n):
            return pltpu.make_async_remote_copy(src_ref=stage.at[n % 2], dst_ref=there(n), send_sem=send_sems.at[n],
                                                recv_sem=recv_sems.at[n], device_id=sib, device_id_type=MESH)

        def fetch(n):
            return pltpu.make_async_copy(there(n), landed, sem)

        for n in range(n_tile):
            @pl.when(t == n_tile + n)
            def _(n=n):
                if n == 0:
                    send(n_tile - 2).wait_send()
                    send(n_tile - 1).wait_send()
                send(n).wait_recv()
                fetch(n).start()

        part = _nn(a_ref[...], b_ref[...])

        for n in range(n_tile):
            @pl.when(t == n)
            def _(n=n):
                if n >= 2:
                    send(n - 2).wait_send()
                stage[n % 2] = part.astype(stage.dtype)
                send(n).start()

            @pl.when(t == n_tile + n)
            def _(n=n):
                fetch(n).wait()
                q_ref[...] = (part + landed[...].astype(F32)).astype(q_ref.dtype)

    mine = lambda t: jnp.maximum(t - n_tile, 0)
    grid_spec = pltpu.PrefetchScalarGridSpec(
        num_scalar_prefetch=1, grid=(2 * n_tile,),
        in_specs=[pl.BlockSpec((D, S), lambda t, cols: (0, 0), pipeline_mode=pl.Buffered(1)),
                  pl.BlockSpec((S, GW_TN), lambda t, cols: (0, cols[t]))],
        out_specs=[pl.BlockSpec((None, D, GW_TN), lambda t, cols: (mine(t) // GW_PARTS, 0, mine(t) % GW_PARTS)), ANY],
        scratch_shapes=[pltpu.VMEM((2, D, GW_TN), BF16), pltpu.VMEM((D, GW_TN), BF16),
                        pltpu.SemaphoreType.DMA((n_tile,)), pltpu.SemaphoreType.DMA((n_tile,)),
                        pltpu.SemaphoreType.DMA(())])
    q, _ = pl.pallas_call(
        body, name="grad_w_in_rs", grid_spec=grid_spec,
        out_shape=[jax.ShapeDtypeStruct((4, R_ALL, D), BF16), jax.ShapeDtypeStruct((4, D, D), BF16)],
        compiler_params=_cparams(1))(cols, xt, dh)
    return q


def _grad_w(name, at, b):
    m, n_all = at.shape[0], b.shape[1]

    def body(a_ref, b_ref, o_ref):
        o_ref[...] = _nn(a_ref[...], b_ref[...]).astype(o_ref.dtype)

    return pl.pallas_call(
        body, name=name, grid=(n_all // GW_TN,),
        in_specs=[pl.BlockSpec((m, S), lambda n: (0, 0), pipeline_mode=pl.Buffered(1)),
                  pl.BlockSpec((S, GW_TN), lambda n: (0, n))],
        out_specs=pl.BlockSpec((m, GW_TN), lambda n: (0, n)), out_shape=jax.ShapeDtypeStruct((m, n_all), BF16),
        compiler_params=_cparams(1))(at, b)


NR = 16
TI = 16
TM = NR * TI
NT = S // TM
ATT_QB = (256, 128, 256)
ATT_NB = (16, 8, 1)
ATT_BLOCKS = (16, 32, 16)


def _permute_tokens(a):
    return a.reshape(NT, TI, NR, a.shape[-1]).transpose(0, 2, 1, 3).reshape(a.shape)


def _attn_shape(g, c):
    if g == 0:
        return (S, c)
    if g == 1:
        return (NT, 4, 4, TI, c)
    return (NT, NR, TI, c)


def _attn_view(g, a):
    return a.reshape(_attn_shape(g, a.shape[-1]))


def _attn_spec(g, width, col, blk):
    if g == 0:
        return pl.BlockSpec((TM, width), lambda b: (blk(b), col))
    if g == 1:
        return pl.BlockSpec((2, 4, None, TI, width), lambda b: (blk(b) % 8, 0, blk(b) // 8, 0, col))
    return pl.BlockSpec((NT, None, TI, width), lambda b: (0, blk(b), 0, col))


def _pieces(g):
    if g == 1:
        return [(t, m) for t in range(2) for m in range(4)]
    return [(t,) for t in range(NT)]


def _get(g, ref, sl):
    if g == 0:
        return ref[:, sl]
    return jnp.concatenate([ref[(*p, slice(None), sl)] for p in _pieces(g)], axis=0)


def _put(g, ref, sl, val):
    if g == 0:
        ref[:, sl] = val
    else:
        for n, p in enumerate(_pieces(g)):
            ref[(*p, slice(None), sl)] = val[TI * n:TI * (n + 1)]


def _block_pos(g, a):
    if g == 0:
        return 16 * (a % 16) + a // 16
    if g == 1:
        return 64 * (a // 64) + 4 * (a % 16) + (a // 16) % 4
    return a


def _attn_mask(g, n):
    qb = ATT_QB[g]
    if ATT_NB[g] == 1:
        qa = lax.broadcasted_iota(jnp.int32, (qb, qb), 0)
        kc = lax.broadcasted_iota(jnp.int32, (qb, qb), 1)
        dist = _block_pos(g, qa) - _block_pos(g, kc)
        return (dist >= 0) & (dist <= QB)
    qa = lax.broadcasted_iota(jnp.int32, (qb, 2 * qb), 0)
    kc = lax.broadcasted_iota(jnp.int32, (qb, 2 * qb), 1)
    cur = kc >= qb
    dist = _block_pos(g, qa) - _block_pos(g, kc % qb) + jnp.where(cur, 0, qb)
    return (dist >= 0) & (dist <= QB) & (cur | (n > 0))


def _keys(g, prev_ref, cur_ref, sl):
    if ATT_NB[g] == 1:
        return _get(g, cur_ref, sl)
    return jnp.concatenate([_get(g, prev_ref, sl), _get(g, cur_ref, sl)], axis=0)


def _qkv_specs(g, clamp):
    cur = lambda col: _attn_spec(g, AW, col, clamp)
    prev = lambda col: _attn_spec(g, AW, col, lambda b: jnp.maximum(clamp(b) - 1, 0))
    qc, kc, vc = (c // AW + g for c in (COL_Q, COL_K, COL_V))
    return [cur(qc), cur(kc), prev(kc), cur(vc), prev(vc)]


def _attn_fwd(g, h):
    scale = HD ** -0.5
    hv = _attn_view(g, h)

    def body(q_ref, kc_ref, kp_ref, vc_ref, vp_ref, o_ref, l_ref):
        valid = _attn_mask(g, pl.program_id(0) % ATT_NB[g])
        for hh in range(NH):
            sl = slice(hh * HD, (hh + 1) * HD)
            kh, vh = _keys(g, kp_ref, kc_ref, sl), _keys(g, vp_ref, vc_ref, sl)
            s = jnp.where(valid, _nt(_get(g, q_ref, sl), kh) * scale, NEG_INF)
            m = jnp.max(s, axis=-1, keepdims=True)
            e = jnp.exp(s - m)
            den = jnp.sum(e, axis=-1, keepdims=True)
            _put(g, o_ref, sl, (_nn(e.astype(BF16), vh) * (1.0 / den)).astype(o_ref.dtype))
            _put(g, l_ref, slice(hh, hh + 1), m + jnp.log(den))

    same = lambda b: b
    o, lse = pl.pallas_call(
        body, name=f"attn_fwd_{g}", grid=(ATT_BLOCKS[g],),
        in_specs=_qkv_specs(g, same),
        out_specs=[_attn_spec(g, AW, 0, same), _attn_spec(g, NH, 0, same)],
        out_shape=[jax.ShapeDtypeStruct(_attn_shape(g, AW), BF16), jax.ShapeDtypeStruct(_attn_shape(g, NH), F32)],
        compiler_params=_cparams(1))(hv, hv, hv, hv, hv)
    return o.reshape(S, AW), lse.reshape(S, NH)


def _attn_bwd(g, dh, h, do, lse, delta):
    scale = HD ** -0.5
    qb = ATT_QB[g]
    carried = ATT_NB[g] > 1
    last = ATT_BLOCKS[g] - 1
    clamp = lambda b: jnp.minimum(b, last)
    behind = lambda b: jnp.maximum(b - 1, 0)
    hv = _attn_view(g, h)

    def body(q_ref, kc_ref, kp_ref, vc_ref, vp_ref, do_ref, l_ref, dl_ref, _, dh_ref, *carry):
        b = pl.program_id(0)

        def write(col, val):
            _put(g, dh_ref, slice(col, col + HD), val.astype(dh_ref.dtype))

        def block():
            valid = _attn_mask(g, b % ATT_NB[g])
            for hh in range(NH):
                sl = slice(hh * HD, (hh + 1) * HD)
                one = slice(hh, hh + 1)
                qh, doh = _get(g, q_ref, sl), _get(g, do_ref, sl)
                kh, vh = _keys(g, kp_ref, kc_ref, sl), _keys(g, vp_ref, vc_ref, sl)
                s = _nt(qh, kh) * scale
                p = jnp.where(valid, jnp.exp(s - _get(g, l_ref, one)), 0.0)
                ds = p * (_nt(doh, vh) - _get(g, dl_ref, one))
                dsb = (ds * scale).astype(BF16)
                dq = _nn(dsb, kh)
                dk2 = _tn(dsb, qh)
                dv2 = _tn(p.astype(BF16), doh)
                if carried:
                    cq_ref, ck_ref, cv_ref = carry
                    write(hh * HD, cq_ref[:, sl])
                    write(AW + hh * HD, ck_ref[:, sl] + dk2[:qb])
                    write(2 * AW + hh * HD, cv_ref[:, sl] + dv2[:qb])
                    cq_ref[:, sl] = dq
                    ck_ref[:, sl] = dk2[qb:]
                    cv_ref[:, sl] = dv2[qb:]
                else:
                    write(hh * HD, dq)
                    write(AW + hh * HD, dk2)
                    write(2 * AW + hh * HD, dv2)

        if not carried:
            block()
            return

        @pl.when(b == 0)
        def _():
            for ref in carry:
                ref[...] = jnp.zeros_like(ref)

        pl.when(b <= last)(block)

        @pl.when(b > last)
        def _():
            for hh in range(NH):
                for n, ref in enumerate(carry):
                    write(n * AW + hh * HD, ref[:, hh * HD:(hh + 1) * HD])

    out = pl.pallas_call(
        body, name=f"attn_bwd_{g}", grid=(ATT_BLOCKS[g] + carried,),
        in_specs=_qkv_specs(g, clamp) + [_attn_spec(g, AW, 0, clamp), _attn_spec(g, NH, 0, clamp),
                                         _attn_spec(g, NH, 0, clamp), ANY],
        out_specs=_attn_spec(g, 3 * AW, g, behind if carried else clamp),
        out_shape=jax.ShapeDtypeStruct(_attn_shape(g, NW), BF16),
        input_output_aliases={8: 0},
        scratch_shapes=[pltpu.VMEM((qb, AW), F32)] * (3 if carried else 0),
        compiler_params=_cparams(1))(hv, hv, hv, hv, hv, _attn_view(g, do), _attn_view(g, lse), _attn_view(g, delta),
                                     _attn_view(g, dh))
    return out.reshape(S, NW)


def _group_weights(l0, l1, l2):
    m = jnp.maximum(jnp.maximum(l0, l1), l2)
    e0, e1, e2 = jnp.exp(l0 - m), jnp.exp(l1 - m), jnp.exp(l2 - m)
    inv = 1.0 / (e0 + e1 + e2)
    return e0 * inv, e1 * inv, e2 * inv


def _residue(ref, r, sl):
    return ref[r * TI:(r + 1) * TI, sl].astype(F32)


def _total(parts):
    return functools.reduce(lambda x, y: x + y, parts)


def _pool_tokens(up_ref, uc_ref, p_ref, tile):
    j0 = lax.broadcasted_iota(jnp.int32, (TI, 1), 0) == 0
    first = (tile == 0) & j0
    for r in range(NR):
        out = []
        for g, w in enumerate(POOL_WINDOWS):
            sl = slice(g * PG, (g + 1) * PG)
            own = _residue(uc_ref, r, sl)
            acc = _total([own] + [_residue(uc_ref, r - k, sl) for k in range(1, min(r, w - 1) + 1)])
            wrapped = [NR + r - k for k in range(r + 1, w)]
            if wrapped:
                wc = _total([_residue(uc_ref, q, sl) for q in wrapped])
                wp = jnp.where(tile > 0, _total([_residue(up_ref, q, sl) for q in wrapped]), 0.0)
                acc = acc + jnp.where(j0, pltpu.roll(wp, 1, 0), pltpu.roll(wc, 1, 0))
            out.append(acc * jnp.where(first, 1.0 / min(r + 1, w), 1.0 / w) - own)
        p_ref[r * TI:(r + 1) * TI, :] = jnp.concatenate(out, axis=1).astype(p_ref.dtype)


def _pool_tokens_bwd(dp, nxt_ref, du_ref, tile):
    ji = lax.broadcasted_iota(jnp.int32, (TI, 1), 0)
    first = (tile == 0) & (ji == 0)
    piece = lambda g, r: dp[g][r * TI:(r + 1) * TI]
    dpc = [[piece(g, r) * jnp.where(first, 1.0 / min(r + 1, w), 1.0 / w) for r in range(NR)]
           for g, w in enumerate(POOL_WINDOWS)]
    for r in range(NR):
        out = []
        for g, w in enumerate(POOL_WINDOWS):
            sl = slice(g * PG, (g + 1) * PG)
            acc = _total([dpc[g][r + k] for k in range(w) if r + k < NR])
            wrapped = [r + k - NR for k in range(1, w) if r + k >= NR]
            if wrapped:
                wc = _total([dpc[g][q] for q in wrapped])
                wn = _total([nxt_ref[q * TI:(q + 1) * TI, sl] for q in wrapped])
                acc = acc + jnp.where(ji == TI - 1, pltpu.roll(wn, TI - 1, 0), pltpu.roll(wc, TI - 1, 0))
            out.append(acc - piece(g, r))
        du_ref[r * TI:(r + 1) * TI, :] = jnp.concatenate(out, axis=1).astype(du_ref.dtype)
    for r in range(NR):
        nxt_ref[r * TI:(r + 1) * TI, :] = jnp.concatenate([dpc[g][r] for g in range(len(POOL_WINDOWS))], axis=1)


def _pool_linear(pb, wpool_ref):
    return jnp.concatenate([_nn(pb[:, g * PG:(g + 1) * PG], wpool_ref[g]) for g in range(len(POOL_WINDOWS))], axis=1)


def _tok(width, col=0, rev=False):
    if rev:
        return pl.BlockSpec((TM, width), lambda i: (NT - 1 - i, col))
    return pl.BlockSpec((TM, width), lambda i: (i, col))


def _whole(shape):
    return pl.BlockSpec(shape, lambda i: (0,) * len(shape))


def _prep_x(x2):
    rows = 1024

    def body(x_ref, xb_ref, xt_ref):
        xb_ref[...] = x_ref[...].astype(BF16)
        xt_ref[...] = x_ref[...].T.astype(BF16)

    return pl.pallas_call(
        body, name="prep_x", grid=(S // rows,), in_specs=[pl.BlockSpec((rows, D), lambda i: (i, 0))],
        out_specs=[pl.BlockSpec((rows, D), lambda i: (i, 0)), pl.BlockSpec((D, rows), lambda i: (0, i))],
        out_shape=[jax.ShapeDtypeStruct((S, D), BF16), jax.ShapeDtypeStruct((D, S), BF16)],
        compiler_params=_cparams(1))(x2)


def _mix_fwd(h, o, lse, wpa, wpp, wpool, pscale, bgate):
    def body(o0_ref, o1_ref, o2_ref, l0_ref, l1_ref, l2_ref, za_ref, uc_ref, up_ref, zp_ref, gp_ref,
             wpa_ref, wpp_ref, wpool_ref, ps_ref, bg_ref,
             mg_ref, a_ref, b_ref, p_ref, yat_ref, ypt_ref, mgt_ref, ya_ref, yp_ref):
        i = pl.program_id(0)
        w0, w1, w2 = _group_weights(l0_ref[...], l1_ref[...], l2_ref[...])
        za = za_ref[...].astype(F32)
        silu_a = za * _sigmoid(za)
        for hh in range(NH):
            sl = slice(hh * HD, (hh + 1) * HD)
            c = slice(hh, hh + 1)
            oh = (w0[:, c] * o0_ref[:, sl].astype(F32) + w1[:, c] * o1_ref[:, sl].astype(F32)
                  + w2[:, c] * o2_ref[:, sl].astype(F32))
            ya = oh * silu_a[:, sl]
            ya_ref[:, sl] = ya.astype(BF16)
            yat_ref[sl, :] = ya.T.astype(BF16)
        _pool_tokens(up_ref, uc_ref, p_ref, i)
        zp = zp_ref[...].astype(F32)
        yp = _pool_linear(p_ref[...], wpool_ref) * ps_ref[...] * (zp * _sigmoid(zp))
        yp_ref[...] = yp.astype(BF16)
        ypt_ref[...] = yp.T.astype(BF16)
        a = _nn(ya_ref[...], wpa_ref[...])
        b = _nn(yp_ref[...], wpp_ref[...])
        a_ref[...] = a.astype(BF16)
        b_ref[...] = b.astype(BF16)
        gates = _sigmoid(gp_ref[...].astype(F32) + bg_ref[...])
        mg = gates[:, :D] * a + gates[:, D:] * b
        mg_ref[...] = mg.astype(BF16)
        mgt_ref[...] = mg.T.astype(BF16)

    u_prev = pl.BlockSpec((TM, AW), lambda i: (jnp.maximum(i - 1, 0), COL_U // AW))
    across = lambda width: pl.BlockSpec((width, TM), lambda i: (0, i))
    return pl.pallas_call(
        body, name="mix_fwd", grid=(NT,),
        in_specs=[_tok(AW)] * 3 + [_tok(NH)] * 3
        + [_tok(AW, COL_ZA // AW), _tok(AW, COL_U // AW), u_prev, _tok(AW, COL_ZP // AW), _tok(2 * D, COL_G // (2 * D))]
        + [_whole((AW, D)), _whole((AW, D)), _whole((4, PG, PG)), _whole((1, AW)), _whole((1, 2 * D))],
        out_specs=[_tok(D), _tok(D), _tok(D), _tok(AW), across(AW), across(AW), across(D)],
        out_shape=[jax.ShapeDtypeStruct((S, D), BF16)] * 3 + [jax.ShapeDtypeStruct((S, AW), BF16)]
        + [jax.ShapeDtypeStruct((AW, S), BF16)] * 2 + [jax.ShapeDtypeStruct((D, S), BF16)],
        scratch_shapes=[pltpu.VMEM((TM, AW), BF16), pltpu.VMEM((TM, AW), BF16)],
        compiler_params=_cparams(1))(*o, *lse, h, h, h, h, h, wpa, wpp, wpool, pscale, bgate)


def _out_ln(merged, x, target, wout, gamma, beta):
    def body(mg_ref, x_ref, t_ref, w_ref, g_ref, b_ref, dr_ref, drb_ref, dm_ref, loss_ref, dg_ref, db_ref):
        i = pl.program_id(0)

        @pl.when(i == 0)
        def _():
            loss_ref[...] = jnp.zeros_like(loss_ref)
            dg_ref[...] = jnp.zeros_like(dg_ref)
            db_ref[...] = jnp.zeros_like(db_ref)

        r = ALPHA * x_ref[...] + _nn(mg_ref[...], w_ref[...])
        mu = jnp.mean(r, axis=-1, keepdims=True)
        rc = r - mu
        rstd = lax.rsqrt(jnp.mean(rc * rc, axis=-1, keepdims=True) + LN_EPS)
        xhat = rc * rstd
        err = xhat * g_ref[...] + b_ref[...] - t_ref[...]
        loss_ref[...] += 0.5 * jnp.sum(jnp.mean(err * err, axis=-1, keepdims=True), axis=0, keepdims=True)
        dy = err * (1.0 / D)
        dg_ref[...] += jnp.sum(dy * xhat, axis=0, keepdims=True)
        db_ref[...] += jnp.sum(dy, axis=0, keepdims=True)
        dxh = dy * g_ref[...]
        dr = rstd * (dxh - jnp.mean(dxh, axis=-1, keepdims=True)
                     - xhat * jnp.mean(dxh * xhat, axis=-1, keepdims=True))
        dr_ref[...] = dr
        drb_ref[...] = dr.astype(BF16)
        dm_ref[...] = _nt(drb_ref[...], w_ref[...]).astype(BF16)

    return pl.pallas_call(
        body, name="out_ln", grid=(NT,),
        in_specs=[_tok(D), _tok(D), _tok(D), _whole((D, D)), _whole((1, D)), _whole((1, D))],
        out_specs=[_tok(D), _tok(D), _tok(D), _whole((8, 128)), _whole((1, D)), _whole((1, D))],
        out_shape=[jax.ShapeDtypeStruct((S, D), F32), jax.ShapeDtypeStruct((S, D), BF16),
                   jax.ShapeDtypeStruct((S, D), BF16), jax.ShapeDtypeStruct((8, 128), F32),
                   jax.ShapeDtypeStruct((1, D), F32), jax.ShapeDtypeStruct((1, D), F32)],
        compiler_params=_cparams(1))(merged, x, target, wout, gamma, beta)


def _gate_bwd(dm, a, b, h, bgate):
    def body(dm_ref, a_ref, b_ref, gp_ref, bg_ref, dgp_ref, da_ref, db_ref, dbg_ref):
        @pl.when(pl.program_id(0) == 0)
        def _():
            dbg_ref[...] = jnp.zeros_like(dbg_ref)

        dm_ = dm_ref[...].astype(F32)
        gates = _sigmoid(gp_ref[...].astype(F32) + bg_ref[...])
        ga, gb = gates[:, :D], gates[:, D:]
        da_ref[...] = (dm_ * ga).astype(BF16)
        db_ref[...] = (dm_ * gb).astype(BF16)
        dgp = jnp.concatenate([dm_ * a_ref[...].astype(F32) * ga * (1.0 - ga),
                               dm_ * b_ref[...].astype(F32) * gb * (1.0 - gb)], axis=1)
        dgp_ref[...] = dgp.astype(BF16)
        dbg_ref[...] += jnp.sum(dgp, axis=0, keepdims=True)

    return pl.pallas_call(
        body, name="gate_bwd", grid=(NT,),
        in_specs=[_tok(D), _tok(D), _tok(D), _tok(2 * D, COL_G // (2 * D)), _whole((1, 2 * D))],
        out_specs=[_tok(2 * D, DH_G // (2 * D)), _tok(D), _tok(D), _whole((1, 2 * D))],
        out_shape=[jax.ShapeDtypeStruct((S, NW), BF16), jax.ShapeDtypeStruct((S, D), BF16),
                   jax.ShapeDtypeStruct((S, D), BF16), jax.ShapeDtypeStruct((1, 2 * D), F32)],
        compiler_params=_cparams(1))(dm, a, b, h, bgate)


def _mix_bwd(dh, da, db, h, o, lse, p, wpa, wpp, wpool, pscale):
    def body(_, da_ref, db_ref, o0_ref, o1_ref, o2_ref, l0_ref, l1_ref, l2_ref, za_ref, zp_ref, p_ref,
             wpa_ref, wpp_ref, wpool_ref, ps_ref,
             dh_ref, do0_ref, do1_ref, do2_ref, dl0_ref, dl1_ref, dl2_ref, dwp_ref, dps_ref,
             nxt_ref):
        i = pl.program_id(0)
        tile = NT - 1 - i
        dza_ref, du_ref, dzp_ref = (dh_ref.at[:, pl.ds(n * AW, AW)] for n in range(3))

        @pl.when(i == 0)
        def _():
            nxt_ref[...] = jnp.zeros_like(nxt_ref)
            dwp_ref[...] = jnp.zeros_like(dwp_ref)
            dps_ref[...] = jnp.zeros_like(dps_ref)

        dya = _nt(da_ref[...], wpa_ref[...])
        w0, w1, w2 = _group_weights(l0_ref[...], l1_ref[...], l2_ref[...])
        za = za_ref[...].astype(F32)
        sig = _sigmoid(za)
        silu_a = za * sig
        dsilu_a = sig * (1.0 + za * (1.0 - sig))
        for hh in range(NH):
            sl = slice(hh * HD, (hh + 1) * HD)
            c = slice(hh, hh + 1)
            oh = (w0[:, c] * o0_ref[:, sl].astype(F32) + w1[:, c] * o1_ref[:, sl].astype(F32)
                  + w2[:, c] * o2_ref[:, sl].astype(F32))
            doh = dya[:, sl] * silu_a[:, sl]
            dza_ref[:, sl] = (dya[:, sl] * oh * dsilu_a[:, sl]).astype(BF16)
            dot_ = jnp.sum(doh * oh, axis=-1, keepdims=True)
            do0_ref[:, sl] = (w0[:, c] * doh).astype(BF16)
            do1_ref[:, sl] = (w1[:, c] * doh).astype(BF16)
            do2_ref[:, sl] = (w2[:, c] * doh).astype(BF16)
            dl0_ref[:, c] = w0[:, c] * dot_
            dl1_ref[:, c] = w1[:, c] * dot_
            dl2_ref[:, c] = w2[:, c] * dot_
        dyp = _nt(db_ref[...], wpp_ref[...])
        pb = p_ref[...]
        pw = _pool_linear(pb, wpool_ref)
        zp = zp_ref[...].astype(F32)
        sigp = _sigmoid(zp)
        dypre = dyp * (zp * sigp)
        dzp_ref[...] = (dyp * (pw * ps_ref[...]) * (sigp * (1.0 + zp * (1.0 - sigp)))).astype(BF16)
        dps_ref[...] += jnp.sum(dypre * pw, axis=0, keepdims=True)
        dpw = (dypre * ps_ref[...]).astype(BF16)
        dp = []
        for g in range(len(POOL_WINDOWS)):
            sl = slice(g * PG, (g + 1) * PG)
            dwp_ref[g] += _tn(pb[:, sl], dpw[:, sl])
            dp.append(_nt(dpw[:, sl], wpool_ref[g]))
        _pool_tokens_bwd(dp, nxt_ref, du_ref, tile)

    r = functools.partial(_tok, rev=True)
    return pl.pallas_call(
        body, name="mix_bwd", grid=(NT,),
        in_specs=[ANY, r(D), r(D)] + [r(AW)] * 3 + [r(NH)] * 3 + [r(AW, COL_ZA // AW), r(AW, COL_ZP // AW), r(AW)]
        + [_whole((AW, D)), _whole((AW, D)), _whole((4, PG, PG)), _whole((1, AW))],
        out_specs=[r(3 * AW, DH_Z // (3 * AW))] + [r(AW)] * 3 + [r(NH)] * 3 + [_whole((4, PG, PG)), _whole((1, AW))],
        out_shape=[jax.ShapeDtypeStruct((S, NW), BF16)] + [jax.ShapeDtypeStruct((S, AW), BF16)] * 3
        + [jax.ShapeDtypeStruct((S, NH), F32)] * 3
        + [jax.ShapeDtypeStruct((4, PG, PG), F32), jax.ShapeDtypeStruct((1, AW), F32)],
        input_output_aliases={0: 0},
        scratch_shapes=[pltpu.VMEM((TM, AW), F32)],
        compiler_params=_cparams(1))(dh, da, db, *o, *lse, h, h, p, wpa, wpp, wpool, pscale)


def _adamw(w, g, m, v):
    m = B1 * m + (1.0 - B1) * g
    v = B2 * v + (1.0 - B2) * jnp.square(g)
    m_hat = m / (1.0 - B1 ** STEP)
    v_hat = v / (1.0 - B2 ** STEP)
    return -LR * (m_hat / (jnp.sqrt(v_hat) + EPS) + WD * w), m, v


def _adam_shard(name, q, l2, w, m, v, tr):
    rows = w.shape[0]

    def body(q_ref, l_ref, w_ref, m_ref, v_ref, g_out, d_out, m_out, v_out):
        g = q_ref[...].astype(F32)
        for k in range(2):
            g = g + l_ref[k].astype(F32)
        g_out[...] = g
        d_out[...], m_out[...], v_out[...] = _adamw(w_ref[...], g, m_ref[...], v_ref[...])

    blk = pl.BlockSpec((tr, D), lambda i: (i, 0))
    return pl.pallas_call(
        body, name=name, grid=(rows // tr,),
        in_specs=[pl.BlockSpec((None, tr, D), lambda i: (0, i, 0)), pl.BlockSpec((2, tr, D), lambda i: (0, i, 0)),
                  blk, blk, blk],
        out_specs=[blk] * 4, out_shape=[jax.ShapeDtypeStruct((rows, D), F32)] * 4,
        compiler_params=_cparams(1))(q, l2, w, m, v)


def _sum_small(q, l2):
    def body(q_ref, l_ref, g_out, buf, sems):
        rows = pl.ds(R_OUT, R_SMALL)
        copies = [pltpu.make_async_copy(src, buf.at[n], sems.at[n])
                  for n, src in enumerate((q_ref.at[0, rows], l_ref.at[0, rows], l_ref.at[1, rows]))]
        for cp in copies:
            cp.start()
        for cp in copies:
            cp.wait()
        g_out[...] = buf[0].astype(F32) + buf[1].astype(F32) + buf[2].astype(F32)

    return pl.pallas_call(
        body, name="sum_small", in_specs=[ANY, ANY], out_shape=jax.ShapeDtypeStruct((R_SMALL, D), F32),
        scratch_shapes=[pltpu.VMEM((3, R_SMALL, D), q.dtype), pltpu.SemaphoreType.DMA((3,))],
        compiler_params=pltpu.CompilerParams(vmem_limit_bytes=VMEM_LIMIT))(q, l2)


def _adam_whole(name, grads, weights, ms, vs):
    n = len(grads)

    def body(*refs):
        ins, outs = refs[:4 * n], refs[4 * n:]
        for t in range(n):
            g, w, m, v = (ins[k * n + t][...] for k in range(4))
            outs[t][...], outs[n + t][...], outs[2 * n + t][...] = _adamw(w, g, m, v)

    out = pl.pallas_call(
        body, name=name, out_shape=[jax.ShapeDtypeStruct(w.shape, F32) for w in weights] * 3,
        compiler_params=pltpu.CompilerParams(vmem_limit_bytes=VMEM_LIMIT))(*grads, *weights, *ms, *vs)
    return out[:n], out[n:2 * n], out[2 * n:]


def _sum_replicated(gathered):
    def body(g_ref, bg_out, ps_out, gm_out, bt_out, loss_out):
        g = g_ref[0]
        for k in range(1, N_DEV):
            g = g + g_ref[k]
        bg_out[...] = jnp.concatenate([g[0:1], g[1:2]], axis=1)
        gm_out[...] = g[2:3]
        bt_out[...] = g[3:4]
        ps_out[...] = g[4:5, :AW]
        loss_out[...] = jnp.broadcast_to(g[5:6, :128], loss_out.shape)

    return pl.pallas_call(
        body, name="sum_replicated",
        out_shape=[jax.ShapeDtypeStruct(shape, F32) for shape in ((1, 2 * D), (1, AW), (1, D), (1, D), (8, 128))],
        compiler_params=pltpu.CompilerParams(vmem_limit_bytes=VMEM_LIMIT))(gathered)


def _pack_small(w_out, w_pa, w_pp, w_pool):
    return jnp.concatenate([w_out, w_pa.reshape(-1, D), w_pp.reshape(-1, D), w_pool.reshape(-1, D)], axis=0)


def _unpack_small(a):
    o = R_OUT
    return (a[:R_PA - o], a[R_PA - o:R_PP - o].reshape(AW, 256), a[R_PP - o:R_PL - o].reshape(AW, 256),
            a[R_PL - o:].reshape(4, 32, PG))


def _pack_vec(b_gate, gamma, beta, pscale, extra):
    z = jnp.zeros((D,), F32)
    return jnp.stack([b_gate[:D], b_gate[D:], gamma, beta, jnp.concatenate([pscale, z[:D - AW]]),
                      jnp.broadcast_to(extra, (D,)), z, z])


def kernel(x, w_in, b_gate, w_pool, pool_scale, w_proj_attn, w_proj_pool, w_out, ln_gamma, ln_beta, loss_target, m_w_in, m_b_gate, m_w_pool, m_pool_scale, m_w_proj_attn, m_w_proj_pool, m_w_out, m_ln_gamma, m_ln_beta, v_w_in, v_b_gate, v_w_pool, v_pool_scale, v_w_proj_attn, v_w_proj_pool, v_w_out, v_ln_gamma, v_ln_beta):
    coords = jnp.stack([lax.axis_index("x"), lax.axis_index("y"), lax.axis_index("c")]).astype(jnp.int32)
    x2, tgt = _permute_tokens(x[0]), _permute_tokens(loss_target[0])
    xb, xt = _prep_x(x2)

    spack = _pack_small(w_out[0], w_proj_attn[0], w_proj_pool[0], w_pool[0]).astype(BF16)
    h, gw = _ag_proj(_arrival_order(*coords), xb, w_in[0].astype(BF16), spack)
    wout = gw[:, R_OUT:R_PA].reshape(D, D)
    wpa = gw[:, R_PA:R_PP].reshape(N_DEV, AW, 256).transpose(1, 0, 2).reshape(AW, D)
    wpp = gw[:, R_PP:R_PL].reshape(N_DEV, AW, 256).transpose(1, 0, 2).reshape(AW, D)
    wpool = gw[:, R_PL:].reshape(N_DEV, 4, 32, PG).transpose(1, 0, 2, 3).reshape(4, PG, PG)

    o, lse = zip(*[_attn_fwd(g, h) for g in range(len(DILATIONS))])
    merged, a, b, p, yat, ypt, mgt = _mix_fwd(h, o, lse, wpa, wpp, wpool, pool_scale, b_gate)
    dr, drb, dm, loss_part, dgamma, dbeta = _out_ln(merged, x2, tgt, wout, ln_gamma, ln_beta)

    dh, da, db, dbgate = _gate_bwd(dm, a, b, h, b_gate)
    dh, do0, do1, do2, dl0, dl1, dl2, dwpool, dpscale = _mix_bwd(
        dh, da, db, h, o, lse, p, wpa, wpp, wpool, pool_scale)
    for g, (do_g, dl_g) in enumerate(zip((do0, do1, do2), (dl0, dl1, dl2))):
        dh = _attn_bwd(g, dh, h, do_g, lse[g], dl_g)

    q = _grad_w_in_rs(_rs_columns(*coords), xt, dh)
    d_wout = _grad_w("grad_w_out", mgt, drb)
    d_wpa = _grad_w("grad_w_pa", yat, da)
    d_wpp = _grad_w("grad_w_pp", ypt, db)
    small = jnp.concatenate([
        d_wout.reshape(N_DEV, 256, D),
        d_wpa.reshape(AW, N_DEV, 256).transpose(1, 0, 2).reshape(N_DEV, -1, D),
        d_wpp.reshape(AW, N_DEV, 256).transpose(1, 0, 2).reshape(N_DEV, -1, D),
        dwpool.astype(BF16).reshape(4, N_DEV, 32, PG).transpose(1, 0, 2, 3).reshape(N_DEV, -1, D)], axis=1)
    q = _pair_sum_small(coords, small, _rs_sibling(small), q)

    vec = _pack_vec(dbgate[0], dgamma[0], dbeta[0], dpscale[0], loss_part[0, 0])
    grad_x, l2, vecs_all = _grad_x_rs(dh, gw, dr, q, vec)
    g_in, d_in, m_in, v_in = _adam_shard("adam_w_in", q, l2, w_in[0], m_w_in[0], v_w_in[0], 256)
    g_small = [t.reshape(w.shape) for t, w in zip(_unpack_small(_sum_small(q, l2)),
                                                  (w_out, w_proj_attn, w_proj_pool, w_pool))]
    small = (g_small,) + _adam_whole("adam_small", g_small, (w_out, w_proj_attn, w_proj_pool, w_pool),
                                     (m_w_out, m_w_proj_attn, m_w_proj_pool, m_w_pool),
                                     (v_w_out, v_w_proj_attn, v_w_proj_pool, v_w_pool))

    *g_vec, loss = _sum_replicated(vecs_all)
    vecs = (g_vec,) + _adam_whole("adam_replicated", g_vec, (b_gate, pool_scale, ln_gamma, ln_beta),
                                  (m_b_gate, m_pool_scale, m_ln_gamma, m_ln_beta),
                                  (v_b_gate, v_pool_scale, v_ln_gamma, v_ln_beta))
    loss = loss[0, 0]

    def leaves(kind, big):
        out, pa, pp, pool = small[kind]
        bg, ps, gm, bt = vecs[kind]
        return [big[None], bg, pool, ps, pa, pp, out, gm, bt]

    return (loss, _permute_tokens(grad_x)[None], *leaves(0, g_in), *leaves(1, d_in), *leaves(2, m_in), *leaves(3, v_in))
```

```python
import functools

import jax
import jax.numpy as jnp
from jax import lax
from jax.experimental import pallas as pl
from jax.experimental.pallas import tpu as pltpu

F32 = jnp.float32
BF16 = jnp.bfloat16

S = 4096
D = 2048
NW = 16384
AW = 1024
HD = 128
NH = 8
QB = 128
DILATIONS = (1, 4, 16)
POOL_WINDOWS = (2, 4, 8, 16)
PG = 256
N_DEV = 8
COL_Q, COL_K, COL_V = 0, 3 * AW, 6 * AW
COL_ZA, COL_U, COL_ZP, COL_G = 9 * AW, 10 * AW, 11 * AW, 12 * AW
DH_Z, DH_G = COL_ZA, COL_G
ALPHA = 2.0 ** 0.25
LN_EPS = 1e-5
NEG_INF = -1e30
LR, B1, B2, EPS, WD, STEP = 0.001, 0.9, 0.999, 1e-08, 0.01, 10
R_IN, R_OUT, R_PA, R_PP, R_PL = 0, 2048, 2304, 2432, 2560
R_ALL = 2576
R_SMALL = R_ALL - R_OUT
VMEM_LIMIT = 56 * 1024 * 1024
MESH = pl.DeviceIdType.MESH
ANY = pl.BlockSpec(memory_space=pl.ANY)


def _cparams(n_axes):
    return pltpu.CompilerParams(dimension_semantics=("arbitrary",) * n_axes, vmem_limit_bytes=VMEM_LIMIT)


def _sigmoid(z):
    return 0.5 * jnp.tanh(0.5 * z) + 0.5


def _nt(a, b):
    return lax.dot_general(a, b, (((1,), (1,)), ((), ())), preferred_element_type=F32)


def _tn(a, b):
    return lax.dot_general(a, b, (((0,), (0,)), ((), ())), preferred_element_type=F32)


def _nn(a, b):
    return jnp.dot(a, b, preferred_element_type=F32)


def _lin(x, y, c):
    return 4 * x + 2 * y + c


def _flip(v, f):
    return 1 - v if f else v


CHIP_FLIPS = ((0, 0), (1, 0), (0, 1), (1, 1))


AG_PIECES = ((pl.ds(R_IN, D), pl.ds(0, 1024)), (pl.ds(R_IN, D), pl.ds(1024, 1024)),
             (pl.ds(R_OUT, R_PA - R_OUT), pl.ds(0, D)), (pl.ds(R_PA, R_ALL - R_PA), pl.ds(0, D)))
N_PIECES = len(AG_PIECES)
SIB, TO_X, TO_Y, ON, PASS_X, PASS_Y, PASS_D = range(7)
AG_TILES = ((0, 0), (0, 1), (1, 0), (1, 1), (2, 0), (4, 0), (3, 0), (5, 0),
            (2, 1), (4, 1), (3, 1), (5, 1), (6, 0), (6, 1), (7, 0), (7, 1))
W, G = "wait", "go"
AG_STEPS = {
    2: [(W, SIB, 0)], 3: [(W, SIB, 1)],
    4: [(W, TO_X, 0), (G, ON, 0), (G, PASS_X, 0)], 5: [(W, TO_Y, 0), (G, PASS_Y, 0)],
    6: [(W, PASS_X, 0)], 7: [(W, PASS_Y, 0)],
    8: [(W, TO_X, 1), (G, PASS_X, 1), (W, TO_Y, 1), (G, ON, 1), (G, PASS_Y, 1),
        (G, TO_X, 2), (G, TO_X, 3), (G, TO_Y, 2), (G, TO_Y, 3)],
    10: [(W, PASS_X, 1)], 11: [(W, PASS_Y, 1)],
    12: [(W, ON, 0), (G, PASS_D, 0)], 13: [(W, ON, 1), (G, PASS_D, 1)],
    14: [(W, PASS_D, 0), (W, TO_X, 2), (G, ON, 2), (G, PASS_X, 2), (W, TO_X, 3), (G, PASS_X, 3),
         (W, TO_Y, 2), (G, PASS_Y, 2), (W, TO_Y, 3), (G, ON, 3), (G, PASS_Y, 3)],
    15: [(W, PASS_D, 1)],
}
AG_LAST = [(W, SIB, 2), (W, SIB, 3), (W, ON, 2), (G, PASS_D, 2), (W, ON, 3), (G, PASS_D, 3),
           (W, PASS_X, 2), (W, PASS_X, 3), (W, PASS_Y, 2), (W, PASS_Y, 3), (W, PASS_D, 2), (W, PASS_D, 3)]


def _arrival_order(x, y, c):
    chips = [(x, y), (1 - x, y), (x, 1 - y), (1 - x, 1 - y)]
    return jnp.stack([_lin(px, py, pc) for px, py in chips for pc in (c, 1 - c)]).astype(jnp.int32)


def _ag_proj(order, x2, wbf, spack):
    tm, tn = 1024, 1024
    nrow, ntile = S // tm, len(AG_TILES)
    slabs = jnp.stack([order[pos] for pos, _ in AG_TILES])
    cols = jnp.stack([2 * order[pos] + half for pos, half in AG_TILES])

    def body(cols_ref, slabs_ref, x_ref, w_ref, s_ref, h_ref, gw_ref, xt_ref, wbuf, wsem, send_sems, recv_sems,
             local_sems):
        t, i = pl.program_id(0), pl.program_id(1)
        x, y, c = lax.axis_index("x"), lax.axis_index("y"), lax.axis_index("c")
        me = _lin(x, y, c)
        dev = {"sib": (x, y, 1 - c), "x": (1 - x, y, c), "y": (x, 1 - y, c), "d": (1 - x, 1 - y, c)}

        def slab_of(name, other_core=False):
            px, py, pc = dev[name]
            return _lin(px, py, 1 - pc if other_core else pc)

        def own(piece):
            rows, colz = AG_PIECES[piece]
            return w_ref.at[:, colz] if piece < 2 else s_ref.at[pl.ds(rows.start - R_OUT, rows.size)]

        def rdma(slab, kind, piece, to, from_own=False):
            k = kind * N_PIECES + piece
            there = gw_ref.at[(slab, *AG_PIECES[piece])]
            return pltpu.make_async_remote_copy(
                src_ref=own(piece) if from_own else there, dst_ref=there,
                send_sem=send_sems.at[k], recv_sem=recv_sems.at[k], device_id=dev[to], device_id_type=MESH)

        def mine(kind, piece):
            if kind in (SIB, TO_X, TO_Y):
                return rdma(me, kind, piece, ("sib", "x", "y")[kind], from_own=True)
            if kind == ON:
                frm, to = ("x", "y") if piece % 2 == 0 else ("y", "x")
                return rdma(slab_of(frm), kind, piece, to)
            return rdma(slab_of({PASS_X: "x", PASS_Y: "y", PASS_D: "d"}[kind]), kind, piece, "sib")

        def landing(kind, piece):
            slab = {SIB: slab_of("sib"), TO_X: slab_of("x"), TO_Y: slab_of("y"), ON: slab_of("d"),
                    PASS_X: slab_of("x", True), PASS_Y: slab_of("y", True), PASS_D: slab_of("d", True)}[kind]
            return rdma(slab, kind, piece, "sib")

        def run(steps):
            for what, kind, piece in steps:
                if what == W:
                    landing(kind, piece).wait_recv()
                else:
                    mine(kind, piece).start()

        local = [pltpu.make_async_copy(w_ref, gw_ref.at[me, pl.ds(R_IN, D)], local_sems.at[0]),
                 pltpu.make_async_copy(s_ref, gw_ref.at[me, pl.ds(R_OUT, R_SMALL)], local_sems.at[1])]

        def fetch(slab, half, slot):
            src = own(half) if slab is None else gw_ref.at[(slab, *AG_PIECES[half])]
            return pltpu.make_async_copy(src, wbuf.at[slot], wsem.at[slot])

        @pl.when((t == 0) & (i == 0))
        def _():
            for cp in local:
                cp.start()
            run([(G, kind, piece) for piece in (0, 1) for kind in (TO_X, TO_Y, SIB)] + [(G, SIB, 2), (G, SIB, 3)])
            first = fetch(None, 0, 0)
            first.start()
            first.wait()

        for nxt in range(1, ntile):
            @pl.when((t == nxt - 1) & (i == nrow - 1))
            def _(nxt=nxt):
                run(AG_STEPS.get(nxt, []))
                fetch(None if AG_TILES[nxt][0] == 0 else slabs_ref[nxt], AG_TILES[nxt][1], nxt % 2).start()

        for slot in (0, 1):
            @pl.when(t % 2 == slot)
            def _(slot=slot):
                @pl.when((i == 0) & (t > 0))
                def _():
                    fetch(None, 0, slot).wait()
                h_ref[...] = _nn(x_ref[...].astype(BF16), wbuf[slot]).astype(h_ref.dtype)

        @pl.when(t == 0)
        def _():
            xt_ref[...] = x_ref[...].T.astype(BF16)

        @pl.when((t == ntile - 1) & (i == nrow - 1))
        def _():
            run(AG_LAST)
            for kind in range(7):
                for piece in range(N_PIECES):
                    mine(kind, piece).wait_send()
            for cp in local:
                cp.wait()

    n_sem = 7 * N_PIECES
    grid_spec = pltpu.PrefetchScalarGridSpec(
        num_scalar_prefetch=2, grid=(ntile, nrow),
        in_specs=[pl.BlockSpec((tm, D), lambda t, i, cols, slabs: (i, 0)), ANY, ANY],
        out_specs=[pl.BlockSpec((tm, tn), lambda t, i, cols, slabs: (i, cols[t])), ANY,
                   pl.BlockSpec((D, tm), lambda t, i, cols, slabs: (0, jnp.where(t == 0, i, nrow - 1)))],
        scratch_shapes=[pltpu.VMEM((2, D, tn), BF16), pltpu.SemaphoreType.DMA((2,)),
                        pltpu.SemaphoreType.DMA((n_sem,)), pltpu.SemaphoreType.DMA((n_sem,)),
                        pltpu.SemaphoreType.DMA((2,))])
    return pl.pallas_call(
        body, name="ag_proj", grid_spec=grid_spec,
        out_shape=[jax.ShapeDtypeStruct((S, NW), BF16), jax.ShapeDtypeStruct((N_DEV, R_ALL, D), BF16),
                   jax.ShapeDtypeStruct((D, S), BF16)],
        compiler_params=_cparams(2))(cols, slabs, x2, wbf, spack)


def _rs_sibling(p):
    n = len(CHIP_FLIPS)

    def body(p_ref, l_ref, send_sems, recv_sems):
        x, y, c = lax.axis_index("x"), lax.axis_index("y"), lax.axis_index("c")
        copies = [pltpu.make_async_remote_copy(
            src_ref=p_ref.at[_lin(_flip(x, fx), _flip(y, fy), 1 - c)], dst_ref=l_ref.at[k],
            send_sem=send_sems.at[k], recv_sem=recv_sems.at[k], device_id=(x, y, 1 - c), device_id_type=MESH)
            for k, (fx, fy) in enumerate(CHIP_FLIPS)]
        for cp in copies:
            cp.start()
        for cp in copies:
            cp.wait_recv()
        for cp in copies:
            cp.wait_send()

    return pl.pallas_call(
        body, name="rs_sibling", out_shape=jax.ShapeDtypeStruct((n,) + p.shape[1:], p.dtype),
        in_specs=[ANY], out_specs=ANY,
        scratch_shapes=[pltpu.SemaphoreType.DMA((n,)), pltpu.SemaphoreType.DMA((n,))])(p)


def _pair_sum_small(coords, p, l1, q):
    def body(crd, p_ref, l_ref, _, q_ref, buf, sem):
        k = pl.program_id(0)
        buf[...] = (p_ref[...].astype(F32) + l_ref[...].astype(F32)).astype(buf.dtype)
        out = pltpu.make_async_copy(buf, q_ref.at[k, pl.ds(R_OUT, R_SMALL)], sem)
        out.start()
        out.wait()

    def p_map(k, crd):
        fx, fy = k % 2, k // 2
        px = crd[0] + fx - 2 * fx * crd[0]
        py = crd[1] + fy - 2 * fy * crd[1]
        return (_lin(px, py, crd[2]), 0, 0)

    grid_spec = pltpu.PrefetchScalarGridSpec(
        num_scalar_prefetch=1, grid=(4,),
        in_specs=[pl.BlockSpec((None, R_SMALL, D), p_map),
                  pl.BlockSpec((None, R_SMALL, D), lambda k, crd: (k, 0, 0)), ANY],
        out_specs=ANY,
        scratch_shapes=[pltpu.VMEM((R_SMALL, D), BF16), pltpu.SemaphoreType.DMA(())])
    return pl.pallas_call(body, name="pair_sum_small", grid_spec=grid_spec,
                          out_shape=jax.ShapeDtypeStruct(q.shape, q.dtype), input_output_aliases={3: 0},
                          compiler_params=_cparams(1))(coords, p, l1, q)


def _h_block(k):
    return jnp.where(k < 9, (k % 3) * 3 + k // 3, k)


RS_PIECES = (pl.ds(0, 1280), pl.ds(1280, R_ALL - 1280))
RS_ROWS = (1280, R_ALL - 1280)
RS_CHUNKS = ((320,) * 4, (432,) * 3)
RS_MERGE_STEP = 2


def _grad_x_rs(dh, g, dr, q, vec):
    others = [(fx, fy, fc) for fx in (0, 1) for fy in (0, 1) for fc in (0, 1) if (fx, fy, fc) != (0, 0, 0)]
    tm, tk = 1024, 1024
    ni, nk = S // tm, NW // tk
    rmax = max(RS_ROWS)
    cmax = max(max(c) for c in RS_CHUNKS)

    def body(dh_ref, w_ref, dr_ref, q_ref, vec_ref, o_ref, l2_ref, ld_ref, mg_ref, all_ref, va, vb,
             send_sems, recv_sems, sems):
        i, k = pl.program_id(0), pl.program_id(1)
        x, y, c = lax.axis_index("x"), lax.axis_index("y"), lax.axis_index("c")
        nbr = ((1 - x, y, c), (x, 1 - y, c))

        def vec_copy(n, sender):
            sx, sy, sc = sender
            fx, fy, fc = others[n]
            return pltpu.make_async_remote_copy(
                src_ref=vec_ref, dst_ref=all_ref.at[_lin(sx, sy, sc)], send_sem=send_sems.at[6 + n],
                recv_sem=recv_sems.at[6 + n], device_id=(_flip(sx, fx), _flip(sy, fy), _flip(sc, fc)),
                device_id_type=MESH)

        vec_own = pltpu.make_async_copy(vec_ref, all_ref.at[_lin(x, y, c)], sems.at[3])

        def rows(ref, piece):
            return ref.at[piece, pl.ds(0, RS_ROWS[piece])]

        copies = (
            (q_ref.at[3, RS_PIECES[0]], rows(ld_ref, 0), 0),
            (q_ref.at[3, RS_PIECES[1]], rows(ld_ref, 1), 1),
            (q_ref.at[1, RS_PIECES[0]], l2_ref.at[0, RS_PIECES[0]], 0),
            (q_ref.at[2, RS_PIECES[1]], l2_ref.at[1, RS_PIECES[1]], 1),
            (rows(mg_ref, 0), l2_ref.at[1, RS_PIECES[0]], 1),
            (rows(mg_ref, 1), l2_ref.at[0, RS_PIECES[1]], 0),
        )

        def copy(n):
            src, dst, axis = copies[n]
            return pltpu.make_async_remote_copy(src_ref=src, dst_ref=dst, send_sem=send_sems.at[n],
                                                recv_sem=recv_sems.at[n], device_id=nbr[axis], device_id_type=MESH)

        def merge(piece, mine):
            start = 0
            for n_rows in RS_CHUNKS[piece]:
                own = pltpu.make_async_copy(q_ref.at[mine, pl.ds(RS_PIECES[piece].start + start, n_rows)],
                                            va.at[pl.ds(0, n_rows)], sems.at[0])
                got = pltpu.make_async_copy(ld_ref.at[piece, pl.ds(start, n_rows)], vb.at[pl.ds(0, n_rows)], sems.at[1])
                own.start()
                got.start()
                own.wait()
                got.wait()
                va[pl.ds(0, n_rows)] = (va[pl.ds(0, n_rows)].astype(F32)
                                        + vb[pl.ds(0, n_rows)].astype(F32)).astype(va.dtype)
                out = pltpu.make_async_copy(va.at[pl.ds(0, n_rows)], mg_ref.at[piece, pl.ds(start, n_rows)], sems.at[2])
                out.start()
                out.wait()
                start += n_rows

        @pl.when((i == 0) & (k == 0))
        def _():
            for n in range(4):
                copy(n).start()
            vec_own.start()
            for n in range(len(others)):
                vec_copy(n, (x, y, c)).start()

        @pl.when((i == RS_MERGE_STEP) & (k == 0))
        def _():
            copy(0).wait_recv()
            merge(0, 2)
            copy(4).start()
            copy(1).wait_recv()
            merge(1, 1)
            copy(5).start()

        @pl.when(k == 0)
        def _():
            o_ref[...] = ALPHA * dr_ref[...]

        o_ref[...] += _nt(dh_ref[...], w_ref[...])

        @pl.when((i == ni - 1) & (k == nk - 1))
        def _():
            for n in range(2, 6):
                copy(n).wait_recv()
            for n in range(6):
                copy(n).wait_send()
            for n, (fx, fy, fc) in enumerate(others):
                vec_copy(n, (_flip(x, fx), _flip(y, fy), _flip(c, fc))).wait_recv()
                vec_copy(n, (x, y, c)).wait_send()
            vec_own.wait()

    slab = q.shape[1:]
    out = pl.pallas_call(
        body, name="grad_x_rs", grid=(ni, nk),
        in_specs=[pl.BlockSpec((tm, tk), lambda i, k: (i, k)),
                  pl.BlockSpec((None, D, tk), lambda i, k: (_h_block(k) // 2, 0, _h_block(k) % 2)),
                  pl.BlockSpec((tm, D), lambda i, k: (i, 0)), ANY, ANY],
        out_specs=[pl.BlockSpec((tm, D), lambda i, k: (i, 0)), ANY, ANY, ANY, ANY],
        out_shape=[jax.ShapeDtypeStruct((S, D), F32), jax.ShapeDtypeStruct((2,) + slab, q.dtype),
                   jax.ShapeDtypeStruct((2, rmax, slab[1]), q.dtype), jax.ShapeDtypeStruct((2, rmax, slab[1]), q.dtype),
                   jax.ShapeDtypeStruct((N_DEV,) + vec.shape, vec.dtype)],
        scratch_shapes=[pltpu.VMEM((cmax, slab[1]), q.dtype), pltpu.VMEM((cmax, slab[1]), q.dtype),
                        pltpu.SemaphoreType.DMA((6 + len(others),)), pltpu.SemaphoreType.DMA((6 + len(others),)),
                        pltpu.SemaphoreType.DMA((4,))],
        compiler_params=_cparams(2))(dh, g, dr, q, vec)
    return out[0], out[1], out[4]


GW_TN = 512
GW_PARTS = D // GW_TN


def _rs_columns(x, y, c):
    per_h = AW // GW_TN
    out = []
    for core in (1 - c, c):
        for fx, fy in CHIP_FLIPS:
            for part in range(GW_PARTS):
                h_block = 2 * _lin(_flip(x, fx), _flip(y, fy), core) + part // per_h
                out.append(_h_block(h_block) * per_h + part % per_h)
    return jnp.stack(out).astype(jnp.int32)


def _grad_w_in_rs(cols, xt, dh):
    n_tile = 4 * GW_PARTS

    def body(cols_ref, a_ref, b_ref, q_ref, l1_ref, stage, landed, send_sems, recv_sems, sem):
        t = pl.program_id(0)
        sib = (lax.axis_index("x"), lax.axis_index("y"), 1 - lax.axis_index("c"))

        def there(n):
            return l1_ref.at[n // GW_PARTS, :, pl.ds((n % GW_PARTS) * GW_TN, GW_TN)]

        def send(n):
            return pltpu.make_async_remote_copy(src_ref=stage.at[n % 2], dst_ref=there(n), send_sem=send_sems.at[n],
                                                recv_sem=recv_sems.at[n], device_id=sib, device_id_type=MESH)

        def fetch(n):
            return pltpu.make_async_copy(there(n), landed, sem)

        for n in range(n_tile):
            @pl.when(t == n_tile + n)
            def _(n=n):
                if n == 0:
                    send(n_tile - 2).wait_send()
                    send(n_tile - 1).wait_send()
                send(n).wait_recv()
                fetch(n).start()

        part = _nn(a_ref[...], b_ref[...])

        for n in range(n_tile):
            @pl.when(t == n)
            def _(n=n):
                if n >= 2:
                    send(n - 2).wait_send()
                stage[n % 2] = part.astype(stage.dtype)
                send(n).start()

            @pl.when(t == n_tile + n)
            def _(n=n):
                fetch(n).wait()
                q_ref[...] = (part + landed[...].astype(F32)).astype(q_ref.dtype)

    mine = lambda t: jnp.maximum(t - n_tile, 0)
    grid_spec = pltpu.PrefetchScalarGridSpec(
        num_scalar_prefetch=1, grid=(2 * n_tile,),
        in_specs=[pl.BlockSpec((D, S), lambda t, cols: (0, 0), pipeline_mode=pl.Buffered(1)),
                  pl.BlockSpec((S, GW_TN), lambda t, cols: (0, cols[t]))],
        out_specs=[pl.BlockSpec((None, D, GW_TN), lambda t, cols: (mine(t) // GW_PARTS, 0, mine(t) % GW_PARTS)), ANY],
        scratch_shapes=[pltpu.VMEM((2, D, GW_TN), BF16), pltpu.VMEM((D, GW_TN), BF16),
                        pltpu.SemaphoreType.DMA((n_tile,)), pltpu.SemaphoreType.DMA((n_tile,)),
                        pltpu.SemaphoreType.DMA(())])
    q, _ = pl.pallas_call(
        body, name="grad_w_in_rs", grid_spec=grid_spec,
        out_shape=[jax.ShapeDtypeStruct((4, R_ALL, D), BF16), jax.ShapeDtypeStruct((4, D, D), BF16)],
        compiler_params=_cparams(1))(cols, xt, dh)
    return q


def _grad_w(name, at, b):
    m, n_all = at.shape[0], b.shape[1]

    def body(a_ref, b_ref, o_ref):
        o_ref[...] = _nn(a_ref[...], b_ref[...]).astype(o_ref.dtype)

    return pl.pallas_call(
        body, name=name, grid=(n_all // GW_TN,),
        in_specs=[pl.BlockSpec((m, S), lambda n: (0, 0), pipeline_mode=pl.Buffered(1)),
                  pl.BlockSpec((S, GW_TN), lambda n: (0, n))],
        out_specs=pl.BlockSpec((m, GW_TN), lambda n: (0, n)), out_shape=jax.ShapeDtypeStruct((m, n_all), BF16),
        compiler_params=_cparams(1))(at, b)


NR = 16
TI = 16
TM = NR * TI
NT = S // TM
ATT_QB = (256, 128, 256)
ATT_NB = (16, 8, 1)
ATT_BLOCKS = (16, 32, 16)


def _permute_tokens(a):
    return a.reshape(NT, TI, NR, a.shape[-1]).transpose(0, 2, 1, 3).reshape(a.shape)


def _attn_shape(g, c):
    if g == 0:
        return (S, c)
    if g == 1:
        return (NT, 4, 4, TI, c)
    return (NT, NR, TI, c)


def _attn_view(g, a):
    return a.reshape(_attn_shape(g, a.shape[-1]))


def _attn_spec(g, width, col, blk):
    if g == 0:
        return pl.BlockSpec((TM, width), lambda b: (blk(b), col))
    if g == 1:
        return pl.BlockSpec((2, 4, None, TI, width), lambda b: (blk(b) % 8, 0, blk(b) // 8, 0, col))
    return pl.BlockSpec((NT, None, TI, width), lambda b: (0, blk(b), 0, col))


def _pieces(g):
    if g == 1:
        return [(t, m) for t in range(2) for m in range(4)]
    return [(t,) for t in range(NT)]


def _get(g, ref, sl):
    if g == 0:
        return ref[:, sl]
    return jnp.concatenate([ref[(*p, slice(None), sl)] for p in _pieces(g)], axis=0)


def _put(g, ref, sl, val):
    if g == 0:
        ref[:, sl] = val
    else:
        for n, p in enumerate(_pieces(g)):
            ref[(*p, slice(None), sl)] = val[TI * n:TI * (n + 1)]


def _block_pos(g, a):
    if g == 0:
        return 16 * (a % 16) + a // 16
    if g == 1:
        return 64 * (a // 64) + 4 * (a % 16) + (a // 16) % 4
    return a


def _attn_mask(g, n):
    qb = ATT_QB[g]
    if ATT_NB[g] == 1:
        qa = lax.broadcasted_iota(jnp.int32, (qb, qb), 0)
        kc = lax.broadcasted_iota(jnp.int32, (qb, qb), 1)
        dist = _block_pos(g, qa) - _block_pos(g, kc)
        return (dist >= 0) & (dist <= QB)
    qa = lax.broadcasted_iota(jnp.int32, (qb, 2 * qb), 0)
    kc = lax.broadcasted_iota(jnp.int32, (qb, 2 * qb), 1)
    cur = kc >= qb
    dist = _block_pos(g, qa) - _block_pos(g, kc % qb) + jnp.where(cur, 0, qb)
    return (dist >= 0) & (dist <= QB) & (cur | (n > 0))


def _keys(g, prev_ref, cur_ref, sl):
    if ATT_NB[g] == 1:
        return _get(g, cur_ref, sl)
    return jnp.concatenate([_get(g, prev_ref, sl), _get(g, cur_ref, sl)], axis=0)


def _qkv_specs(g, clamp):
    cur = lambda col: _attn_spec(g, AW, col, clamp)
    prev = lambda col: _attn_spec(g, AW, col, lambda b: jnp.maximum(clamp(b) - 1, 0))
    qc, kc, vc = (c // AW + g for c in (COL_Q, COL_K, COL_V))
    return [cur(qc), cur(kc), prev(kc), cur(vc), prev(vc)]


def _attn_fwd(g, h):
    scale = HD ** -0.5
    hv = _attn_view(g, h)

    def body(q_ref, kc_ref, kp_ref, vc_ref, vp_ref, o_ref, l_ref):
        valid = _attn_mask(g, pl.program_id(0) % ATT_NB[g])
        for hh in range(NH):
            sl = slice(hh * HD, (hh + 1) * HD)
            kh, vh = _keys(g, kp_ref, kc_ref, sl), _keys(g, vp_ref, vc_ref, sl)
            s = jnp.where(valid, _nt(_get(g, q_ref, sl), kh) * scale, NEG_INF)
            m = jnp.max(s, axis=-1, keepdims=True)
            e = jnp.exp(s - m)
            den = jnp.sum(e, axis=-1, keepdims=True)
            _put(g, o_ref, sl, (_nn(e.astype(BF16), vh) * (1.0 / den)).astype(o_ref.dtype))
            _put(g, l_ref, slice(hh, hh + 1), m + jnp.log(den))

    same = lambda b: b
    o, lse = pl.pallas_call(
        body, name=f"attn_fwd_{g}", grid=(ATT_BLOCKS[g],),
        in_specs=_qkv_specs(g, same),
        out_specs=[_attn_spec(g, AW, 0, same), _attn_spec(g, NH, 0, same)],
        out_shape=[jax.ShapeDtypeStruct(_attn_shape(g, AW), BF16), jax.ShapeDtypeStruct(_attn_shape(g, NH), F32)],
        compiler_params=_cparams(1))(hv, hv, hv, hv, hv)
    return o.reshape(S, AW), lse.reshape(S, NH)


def _attn_bwd(g, dh, h, do, lse, delta):
    scale = HD ** -0.5
    qb = ATT_QB[g]
    carried = ATT_NB[g] > 1
    last = ATT_BLOCKS[g] - 1
    clamp = lambda b: jnp.minimum(b, last)
    behind = lambda b: jnp.maximum(b - 1, 0)
    hv = _attn_view(g, h)

    def body(q_ref, kc_ref, kp_ref, vc_ref, vp_ref, do_ref, l_ref, dl_ref, _, dh_ref, *carry):
        b = pl.program_id(0)

        def write(col, val):
            _put(g, dh_ref, slice(col, col + HD), val.astype(dh_ref.dtype))

        def block():
            valid = _attn_mask(g, b % ATT_NB[g])
            for hh in range(NH):
                sl = slice(hh * HD, (hh + 1) * HD)
                one = slice(hh, hh + 1)
                qh, doh = _get(g, q_ref, sl), _get(g, do_ref, sl)
                kh, vh = _keys(g, kp_ref, kc_ref, sl), _keys(g, vp_ref, vc_ref, sl)
                s = _nt(qh, kh) * scale
                p = jnp.where(valid, jnp.exp(s - _get(g, l_ref, one)), 0.0)
                ds = p * (_nt(doh, vh) - _get(g, dl_ref, one))
                dsb = (ds * scale).astype(BF16)
                dq = _nn(dsb, kh)
                dk2 = _tn(dsb, qh)
                dv2 = _tn(p.astype(BF16), doh)
                if carried:
                    cq_ref, ck_ref, cv_ref = carry
                    write(hh * HD, cq_ref[:, sl])
                    write(AW + hh * HD, ck_ref[:, sl] + dk2[:qb])
                    write(2 * AW + hh * HD, cv_ref[:, sl] + dv2[:qb])
                    cq_ref[:, sl] = dq
                    ck_ref[:, sl] = dk2[qb:]
                    cv_ref[:, sl] = dv2[qb:]
                else:
                    write(hh * HD, dq)
                    write(AW + hh * HD, dk2)
                    write(2 * AW + hh * HD, dv2)

        if not carried:
            block()
            return

        @pl.when(b == 0)
        def _():
            for ref in carry:
                ref[...] = jnp.zeros_like(ref)

        pl.when(b <= last)(block)

        @pl.when(b > last)
        def _():
            for hh in range(NH):
                for n, ref in enumerate(carry):
                    write(n * AW + hh * HD, ref[:, hh * HD:(hh + 1) * HD])

    out = pl.pallas_call(
        body, name=f"attn_bwd_{g}", grid=(ATT_BLOCKS[g] + carried,),
        in_specs=_qkv_specs(g, clamp) + [_attn_spec(g, AW, 0, clamp), _attn_spec(g, NH, 0, clamp),
                                         _attn_spec(g, NH, 0, clamp), ANY],
        out_specs=_attn_spec(g, 3 * AW, g, behind if carried else clamp),
        out_shape=jax.ShapeDtypeStruct(_attn_shape(g, NW), BF16),
        input_output_aliases={8: 0},
        scratch_shapes=[pltpu.VMEM((qb, AW), F32)] * (3 if carried else 0),
        compiler_params=_cparams(1))(hv, hv, hv, hv, hv, _attn_view(g, do), _attn_view(g, lse), _attn_view(g, delta),
                                     _attn_view(g, dh))
    return out.reshape(S, NW)


def _group_weights(l0, l1, l2):
    m = jnp.maximum(jnp.maximum(l0, l1), l2)
    e0, e1, e2 = jnp.exp(l0 - m), jnp.exp(l1 - m), jnp.exp(l2 - m)
    inv = 1.0 / (e0 + e1 + e2)
    return e0 * inv, e1 * inv, e2 * inv


def _residue(ref, r, sl):
    return ref[r * TI:(r + 1) * TI, sl].astype(F32)


def _total(parts):
    return functools.reduce(lambda x, y: x + y, parts)


def _pool_tokens(up_ref, uc_ref, p_ref, tile):
    j0 = lax.broadcasted_iota(jnp.int32, (TI, 1), 0) == 0
    first = (tile == 0) & j0
    for r in range(NR):
        out = []
        for g, w in enumerate(POOL_WINDOWS):
            sl = slice(g * PG, (g + 1) * PG)
            own = _residue(uc_ref, r, sl)
            acc = _total([own] + [_residue(uc_ref, r - k, sl) for k in range(1, min(r, w - 1) + 1)])
            wrapped = [NR + r - k for k in range(r + 1, w)]
            if wrapped:
                wc = _total([_residue(uc_ref, q, sl) for q in wrapped])
                wp = jnp.where(tile > 0, _total([_residue(up_ref, q, sl) for q in wrapped]), 0.0)
                acc = acc + jnp.where(j0, pltpu.roll(wp, 1, 0), pltpu.roll(wc, 1, 0))
            out.append(acc * jnp.where(first, 1.0 / min(r + 1, w), 1.0 / w) - own)
        p_ref[r * TI:(r + 1) * TI, :] = jnp.concatenate(out, axis=1).astype(p_ref.dtype)


def _pool_tokens_bwd(dp, nxt_ref, du_ref, tile):
    ji = lax.broadcasted_iota(jnp.int32, (TI, 1), 0)
    first = (tile == 0) & (ji == 0)
    piece = lambda g, r: dp[g][r * TI:(r + 1) * TI]
    dpc = [[piece(g, r) * jnp.where(first, 1.0 / min(r + 1, w), 1.0 / w) for r in range(NR)]
           for g, w in enumerate(POOL_WINDOWS)]
    for r in range(NR):
        out = []
        for g, w in enumerate(POOL_WINDOWS):
            sl = slice(g * PG, (g + 1) * PG)
            acc = _total([dpc[g][r + k] for k in range(w) if r + k < NR])
            wrapped = [r + k - NR for k in range(1, w) if r + k >= NR]
            if wrapped:
                wc = _total([dpc[g][q] for q in wrapped])
                wn = _total([nxt_ref[q * TI:(q + 1) * TI, sl] for q in wrapped])
                acc = acc + jnp.where(ji == TI - 1, pltpu.roll(wn, TI - 1, 0), pltpu.roll(wc, TI - 1, 0))
            out.append(acc - piece(g, r))
        du_ref[r * TI:(r + 1) * TI, :] = jnp.concatenate(out, axis=1).astype(du_ref.dtype)
    for r in range(NR):
        nxt_ref[r * TI:(r + 1) * TI, :] = jnp.concatenate([dpc[g][r] for g in range(len(POOL_WINDOWS))], axis=1)


def _pool_linear(pb, wpool_ref):
    return jnp.concatenate([_nn(pb[:, g * PG:(g + 1) * PG], wpool_ref[g]) for g in range(len(POOL_WINDOWS))], axis=1)


def _tok(width, col=0, rev=False):
    if rev:
        return pl.BlockSpec((TM, width), lambda i: (NT - 1 - i, col))
    return pl.BlockSpec((TM, width), lambda i: (i, col))


def _whole(shape):
    return pl.BlockSpec(shape, lambda i: (0,) * len(shape))


def _mix_fwd(h, o, lse, wpa, wpp, wpool, pscale, bgate):
    def body(o0_ref, o1_ref, o2_ref, l0_ref, l1_ref, l2_ref, za_ref, uc_ref, up_ref, zp_ref, gp_ref,
             wpa_ref, wpp_ref, wpool_ref, ps_ref, bg_ref,
             mg_ref, a_ref, b_ref, p_ref, yat_ref, ypt_ref, mgt_ref, ya_ref, yp_ref):
        i = pl.program_id(0)
        w0, w1, w2 = _group_weights(l0_ref[...], l1_ref[...], l2_ref[...])
        za = za_ref[...].astype(F32)
        silu_a = za * _sigmoid(za)
        for hh in range(NH):
            sl = slice(hh * HD, (hh + 1) * HD)
            c = slice(hh, hh + 1)
            oh = (w0[:, c] * o0_ref[:, sl].astype(F32) + w1[:, c] * o1_ref[:, sl].astype(F32)
                  + w2[:, c] * o2_ref[:, sl].astype(F32))
            ya = oh * silu_a[:, sl]
            ya_ref[:, sl] = ya.astype(BF16)
            yat_ref[sl, :] = ya.T.astype(BF16)
        _pool_tokens(up_ref, uc_ref, p_ref, i)
        zp = zp_ref[...].astype(F32)
        yp = _pool_linear(p_ref[...], wpool_ref) * ps_ref[...] * (zp * _sigmoid(zp))
        yp_ref[...] = yp.astype(BF16)
        ypt_ref[...] = yp.T.astype(BF16)
        a = _nn(ya_ref[...], wpa_ref[...])
        b = _nn(yp_ref[...], wpp_ref[...])
        a_ref[...] = a.astype(BF16)
        b_ref[...] = b.astype(BF16)
        gates = _sigmoid(gp_ref[...].astype(F32) + bg_ref[...])
        mg = gates[:, :D] * a + gates[:, D:] * b
        mg_ref[...] = mg.astype(BF16)
        mgt_ref[...] = mg.T.astype(BF16)

    u_prev = pl.BlockSpec((TM, AW), lambda i: (jnp.maximum(i - 1, 0), COL_U // AW))
    across = lambda width: pl.BlockSpec((width, TM), lambda i: (0, i))
    return pl.pallas_call(
        body, name="mix_fwd", grid=(NT,),
        in_specs=[_tok(AW)] * 3 + [_tok(NH)] * 3
        + [_tok(AW, COL_ZA // AW), _tok(AW, COL_U // AW), u_prev, _tok(AW, COL_ZP // AW), _tok(2 * D, COL_G // (2 * D))]
        + [_whole((AW, D)), _whole((AW, D)), _whole((4, PG, PG)), _whole((1, AW)), _whole((1, 2 * D))],
        out_specs=[_tok(D), _tok(D), _tok(D), _tok(AW), across(AW), across(AW), across(D)],
        out_shape=[jax.ShapeDtypeStruct((S, D), BF16)] * 3 + [jax.ShapeDtypeStruct((S, AW), BF16)]
        + [jax.ShapeDtypeStruct((AW, S), BF16)] * 2 + [jax.ShapeDtypeStruct((D, S), BF16)],
        scratch_shapes=[pltpu.VMEM((TM, AW), BF16), pltpu.VMEM((TM, AW), BF16)],
        compiler_params=_cparams(1))(*o, *lse, h, h, h, h, h, wpa, wpp, wpool, pscale, bgate)


def _out_ln(merged, x, target, wout, gamma, beta):
    def body(mg_ref, x_ref, t_ref, w_ref, g_ref, b_ref, dr_ref, drb_ref, dm_ref, loss_ref, dg_ref, db_ref):
        i = pl.program_id(0)

        @pl.when(i == 0)
        def _():
            loss_ref[...] = jnp.zeros_like(loss_ref)
            dg_ref[...] = jnp.zeros_like(dg_ref)
            db_ref[...] = jnp.zeros_like(db_ref)

        r = ALPHA * x_ref[...] + _nn(mg_ref[...], w_ref[...])
        mu = jnp.mean(r, axis=-1, keepdims=True)
        rc = r - mu
        rstd = lax.rsqrt(jnp.mean(rc * rc, axis=-1, keepdims=True) + LN_EPS)
        xhat = rc * rstd
        err = xhat * g_ref[...] + b_ref[...] - t_ref[...]
        loss_ref[...] += 0.5 * jnp.sum(jnp.mean(err * err, axis=-1, keepdims=True), axis=0, keepdims=True)
        dy = err * (1.0 / D)
        dg_ref[...] += jnp.sum(dy * xhat, axis=0, keepdims=True)
        db_ref[...] += jnp.sum(dy, axis=0, keepdims=True)
        dxh = dy * g_ref[...]
        dr = rstd * (dxh - jnp.mean(dxh, axis=-1, keepdims=True)
                     - xhat * jnp.mean(dxh * xhat, axis=-1, keepdims=True))
        dr_ref[...] = dr
        drb_ref[...] = dr.astype(BF16)
        dm_ref[...] = _nt(drb_ref[...], w_ref[...]).astype(BF16)

    return pl.pallas_call(
        body, name="out_ln", grid=(NT,),
        in_specs=[_tok(D), _tok(D), _tok(D), _whole((D, D)), _whole((1, D)), _whole((1, D))],
        out_specs=[_tok(D), _tok(D), _tok(D), _whole((8, 128)), _whole((1, D)), _whole((1, D))],
        out_shape=[jax.ShapeDtypeStruct((S, D), F32), jax.ShapeDtypeStruct((S, D), BF16),
                   jax.ShapeDtypeStruct((S, D), BF16), jax.ShapeDtypeStruct((8, 128), F32),
                   jax.ShapeDtypeStruct((1, D), F32), jax.ShapeDtypeStruct((1, D), F32)],
        compiler_params=_cparams(1))(merged, x, target, wout, gamma, beta)


def _gate_bwd(dm, a, b, h, bgate):
    def body(dm_ref, a_ref, b_ref, gp_ref, bg_ref, dgp_ref, da_ref, db_ref, dbg_ref):
        @pl.when(pl.program_id(0) == 0)
        def _():
            dbg_ref[...] = jnp.zeros_like(dbg_ref)

        dm_ = dm_ref[...].astype(F32)
        gates = _sigmoid(gp_ref[...].astype(F32) + bg_ref[...])
        ga, gb = gates[:, :D], gates[:, D:]
        da_ref[...] = (dm_ * ga).astype(BF16)
        db_ref[...] = (dm_ * gb).astype(BF16)
        dgp = jnp.concatenate([dm_ * a_ref[...].astype(F32) * ga * (1.0 - ga),
                               dm_ * b_ref[...].astype(F32) * gb * (1.0 - gb)], axis=1)
        dgp_ref[...] = dgp.astype(BF16)
        dbg_ref[...] += jnp.sum(dgp, axis=0, keepdims=True)

    return pl.pallas_call(
        body, name="gate_bwd", grid=(NT,),
        in_specs=[_tok(D), _tok(D), _tok(D), _tok(2 * D, COL_G // (2 * D)), _whole((1, 2 * D))],
        out_specs=[_tok(2 * D, DH_G // (2 * D)), _tok(D), _tok(D), _whole((1, 2 * D))],
        out_shape=[jax.ShapeDtypeStruct((S, NW), BF16), jax.ShapeDtypeStruct((S, D), BF16),
                   jax.ShapeDtypeStruct((S, D), BF16), jax.ShapeDtypeStruct((1, 2 * D), F32)],
        compiler_params=_cparams(1))(dm, a, b, h, bgate)


def _mix_bwd(dh, da, db, h, o, lse, p, wpa, wpp, wpool, pscale):
    def body(_, da_ref, db_ref, o0_ref, o1_ref, o2_ref, l0_ref, l1_ref, l2_ref, za_ref, zp_ref, p_ref,
             wpa_ref, wpp_ref, wpool_ref, ps_ref,
             dh_ref, do0_ref, do1_ref, do2_ref, dl0_ref, dl1_ref, dl2_ref, dwp_ref, dps_ref,
             nxt_ref):
        i = pl.program_id(0)
        tile = NT - 1 - i
        dza_ref, du_ref, dzp_ref = (dh_ref.at[:, pl.ds(n * AW, AW)] for n in range(3))

        @pl.when(i == 0)
        def _():
            nxt_ref[...] = jnp.zeros_like(nxt_ref)
            dwp_ref[...] = jnp.zeros_like(dwp_ref)
            dps_ref[...] = jnp.zeros_like(dps_ref)

        dya = _nt(da_ref[...], wpa_ref[...])
        w0, w1, w2 = _group_weights(l0_ref[...], l1_ref[...], l2_ref[...])
        za = za_ref[...].astype(F32)
        sig = _sigmoid(za)
        silu_a = za * sig
        dsilu_a = sig * (1.0 + za * (1.0 - sig))
        for hh in range(NH):
            sl = slice(hh * HD, (hh + 1) * HD)
            c = slice(hh, hh + 1)
            oh = (w0[:, c] * o0_ref[:, sl].astype(F32) + w1[:, c] * o1_ref[:, sl].astype(F32)
                  + w2[:, c] * o2_ref[:, sl].astype(F32))
            doh = dya[:, sl] * silu_a[:, sl]
            dza_ref[:, sl] = (dya[:, sl] * oh * dsilu_a[:, sl]).astype(BF16)
            dot_ = jnp.sum(doh * oh, axis=-1, keepdims=True)
            do0_ref[:, sl] = (w0[:, c] * doh).astype(BF16)
            do1_ref[:, sl] = (w1[:, c] * doh).astype(BF16)
            do2_ref[:, sl] = (w2[:, c] * doh).astype(BF16)
            dl0_ref[:, c] = w0[:, c] * dot_
            dl1_ref[:, c] = w1[:, c] * dot_
            dl2_ref[:, c] = w2[:, c] * dot_
        dyp = _nt(db_ref[...], wpp_ref[...])
        pb = p_ref[...]
        pw = _pool_linear(pb, wpool_ref)
        zp = zp_ref[...].astype(F32)
        sigp = _sigmoid(zp)
        dypre = dyp * (zp * sigp)
        dzp_ref[...] = (dyp * (pw * ps_ref[...]) * (sigp * (1.0 + zp * (1.0 - sigp)))).astype(BF16)
        dps_ref[...] += jnp.sum(dypre * pw, axis=0, keepdims=True)
        dpw = (dypre * ps_ref[...]).astype(BF16)
        dp = []
        for g in range(len(POOL_WINDOWS)):
            sl = slice(g * PG, (g + 1) * PG)
            dwp_ref[g] += _tn(pb[:, sl], dpw[:, sl])
            dp.append(_nt(dpw[:, sl], wpool_ref[g]))
        _pool_tokens_bwd(dp, nxt_ref, du_ref, tile)

    r = functools.partial(_tok, rev=True)
    return pl.pallas_call(
        body, name="mix_bwd", grid=(NT,),
        in_specs=[ANY, r(D), r(D)] + [r(AW)] * 3 + [r(NH)] * 3 + [r(AW, COL_ZA // AW), r(AW, COL_ZP // AW), r(AW)]
        + [_whole((AW, D)), _whole((AW, D)), _whole((4, PG, PG)), _whole((1, AW))],
        out_specs=[r(3 * AW, DH_Z // (3 * AW))] + [r(AW)] * 3 + [r(NH)] * 3 + [_whole((4, PG, PG)), _whole((1, AW))],
        out_shape=[jax.ShapeDtypeStruct((S, NW), BF16)] + [jax.ShapeDtypeStruct((S, AW), BF16)] * 3
        + [jax.ShapeDtypeStruct((S, NH), F32)] * 3
        + [jax.ShapeDtypeStruct((4, PG, PG), F32), jax.ShapeDtypeStruct((1, AW), F32)],
        input_output_aliases={0: 0},
        scratch_shapes=[pltpu.VMEM((TM, AW), F32)],
        compiler_params=_cparams(1))(dh, da, db, *o, *lse, h, h, p, wpa, wpp, wpool, pscale)


def _adamw(w, g, m, v):
    m = B1 * m + (1.0 - B1) * g
    v = B2 * v + (1.0 - B2) * jnp.square(g)
    m_hat = m / (1.0 - B1 ** STEP)
    v_hat = v / (1.0 - B2 ** STEP)
    return -LR * (m_hat / (jnp.sqrt(v_hat) + EPS) + WD * w), m, v


def _adam_shard(name, q, l2, w, m, v, tr):
    rows = w.shape[0]

    def body(q_ref, l_ref, w_ref, m_ref, v_ref, g_out, d_out, m_out, v_out):
        g = q_ref[...].astype(F32)
        for k in range(2):
            g = g + l_ref[k].astype(F32)
        g_out[...] = g
        d_out[...], m_out[...], v_out[...] = _adamw(w_ref[...], g, m_ref[...], v_ref[...])

    blk = pl.BlockSpec((tr, D), lambda i: (i, 0))
    return pl.pallas_call(
        body, name=name, grid=(rows // tr,),
        in_specs=[pl.BlockSpec((None, tr, D), lambda i: (0, i, 0)), pl.BlockSpec((2, tr, D), lambda i: (0, i, 0)),
                  blk, blk, blk],
        out_specs=[blk] * 4, out_shape=[jax.ShapeDtypeStruct((rows, D), F32)] * 4,
        compiler_params=_cparams(1))(q, l2, w, m, v)


def _sum_small(q, l2):
    def body(q_ref, l_ref, g_out, buf, sems):
        rows = pl.ds(R_OUT, R_SMALL)
        copies = [pltpu.make_async_copy(src, buf.at[n], sems.at[n])
                  for n, src in enumerate((q_ref.at[0, rows], l_ref.at[0, rows], l_ref.at[1, rows]))]
        for cp in copies:
            cp.start()
        for cp in copies:
            cp.wait()
        g_out[...] = buf[0].astype(F32) + buf[1].astype(F32) + buf[2].astype(F32)

    return pl.pallas_call(
        body, name="sum_small", in_specs=[ANY, ANY], out_shape=jax.ShapeDtypeStruct((R_SMALL, D), F32),
        scratch_shapes=[pltpu.VMEM((3, R_SMALL, D), q.dtype), pltpu.SemaphoreType.DMA((3,))],
        compiler_params=pltpu.CompilerParams(vmem_limit_bytes=VMEM_LIMIT))(q, l2)


def _adam_whole(name, grads, weights, ms, vs):
    n = len(grads)

    def body(*refs):
        ins, outs = refs[:4 * n], refs[4 * n:]
        for t in range(n):
            g, w, m, v = (ins[k * n + t][...] for k in range(4))
            outs[t][...], outs[n + t][...], outs[2 * n + t][...] = _adamw(w, g, m, v)

    out = pl.pallas_call(
        body, name=name, out_shape=[jax.ShapeDtypeStruct(w.shape, F32) for w in weights] * 3,
        compiler_params=pltpu.CompilerParams(vmem_limit_bytes=VMEM_LIMIT))(*grads, *weights, *ms, *vs)
    return out[:n], out[n:2 * n], out[2 * n:]


def _sum_replicated(gathered):
    def body(g_ref, bg_out, ps_out, gm_out, bt_out, loss_out):
        g = g_ref[0]
        for k in range(1, N_DEV):
            g = g + g_ref[k]
        bg_out[...] = jnp.concatenate([g[0:1], g[1:2]], axis=1)
        gm_out[...] = g[2:3]
        bt_out[...] = g[3:4]
        ps_out[...] = g[4:5, :AW]
        loss_out[...] = jnp.broadcast_to(g[5:6, :128], loss_out.shape)

    return pl.pallas_call(
        body, name="sum_replicated",
        out_shape=[jax.ShapeDtypeStruct(shape, F32) for shape in ((1, 2 * D), (1, AW), (1, D), (1, D), (8, 128))],
        compiler_params=pltpu.CompilerParams(vmem_limit_bytes=VMEM_LIMIT))(gathered)


def _pack_small(w_out, w_pa, w_pp, w_pool):
    return jnp.concatenate([w_out, w_pa.reshape(-1, D), w_pp.reshape(-1, D), w_pool.reshape(-1, D)], axis=0)


def _unpack_small(a):
    o = R_OUT
    return (a[:R_PA - o], a[R_PA - o:R_PP - o].reshape(AW, 256), a[R_PP - o:R_PL - o].reshape(AW, 256),
            a[R_PL - o:].reshape(4, 32, PG))


def _pack_vec(b_gate, gamma, beta, pscale, extra):
    z = jnp.zeros((D,), F32)
    return jnp.stack([b_gate[:D], b_gate[D:], gamma, beta, jnp.concatenate([pscale, z[:D - AW]]),
                      jnp.broadcast_to(extra, (D,)), z, z])


def kernel(x, w_in, b_gate, w_pool, pool_scale, w_proj_attn, w_proj_pool, w_out, ln_gamma, ln_beta, loss_target, m_w_in, m_b_gate, m_w_pool, m_pool_scale, m_w_proj_attn, m_w_proj_pool, m_w_out, m_ln_gamma, m_ln_beta, v_w_in, v_b_gate, v_w_pool, v_pool_scale, v_w_proj_attn, v_w_proj_pool, v_w_out, v_ln_gamma, v_ln_beta):
    coords = jnp.stack([lax.axis_index("x"), lax.axis_index("y"), lax.axis_index("c")]).astype(jnp.int32)
    x2, tgt = _permute_tokens(x[0]), _permute_tokens(loss_target[0])

    spack = _pack_small(w_out[0], w_proj_attn[0], w_proj_pool[0], w_pool[0]).astype(BF16)
    h, gw, xt = _ag_proj(_arrival_order(*coords), x2, w_in[0].astype(BF16), spack)
    wout = gw[:, R_OUT:R_PA].reshape(D, D)
    wpa = gw[:, R_PA:R_PP].reshape(N_DEV, AW, 256).transpose(1, 0, 2).reshape(AW, D)
    wpp = gw[:, R_PP:R_PL].reshape(N_DEV, AW, 256).transpose(1, 0, 2).reshape(AW, D)
    wpool = gw[:, R_PL:].reshape(N_DEV, 4, 32, PG).transpose(1, 0, 2, 3).reshape(4, PG, PG)

    o, lse = zip(*[_attn_fwd(g, h) for g in range(len(DILATIONS))])
    merged, a, b, p, yat, ypt, mgt = _mix_fwd(h, o, lse, wpa, wpp, wpool, pool_scale, b_gate)
    dr, drb, dm, loss_part, dgamma, dbeta = _out_ln(merged, x2, tgt, wout, ln_gamma, ln_beta)

    dh, da, db, dbgate = _gate_bwd(dm, a, b, h, b_gate)
    dh, do0, do1, do2, dl0, dl1, dl2, dwpool, dpscale = _mix_bwd(
        dh, da, db, h, o, lse, p, wpa, wpp, wpool, pool_scale)
    for g, (do_g, dl_g) in enumerate(zip((do0, do1, do2), (dl0, dl1, dl2))):
        dh = _attn_bwd(g, dh, h, do_g, lse[g], dl_g)

    q = _grad_w_in_rs(_rs_columns(*coords), xt, dh)
    d_wout = _grad_w("grad_w_out", mgt, drb)
    d_wpa = _grad_w("grad_w_pa", yat, da)
    d_wpp = _grad_w("grad_w_pp", ypt, db)
    small = jnp.concatenate([
        d_wout.reshape(N_DEV, 256, D),
        d_wpa.reshape(AW, N_DEV, 256).transpose(1, 0, 2).reshape(N_DEV, -1, D),
        d_wpp.reshape(AW, N_DEV, 256).transpose(1, 0, 2).reshape(N_DEV, -1, D),
        dwpool.astype(BF16).reshape(4, N_DEV, 32, PG).transpose(1, 0, 2, 3).reshape(N_DEV, -1, D)], axis=1)
    q = _pair_sum_small(coords, small, _rs_sibling(small), q)

    vec = _pack_vec(dbgate[0], dgamma[0], dbeta[0], dpscale[0], loss_part[0, 0])
    grad_x, l2, vecs_all = _grad_x_rs(dh, gw, dr, q, vec)
    g_in, d_in, m_in, v_in = _adam_shard("adam_w_in", q, l2, w_in[0], m_w_in[0], v_w_in[0], 256)
    g_small = [t.reshape(w.shape) for t, w in zip(_unpack_small(_sum_small(q, l2)),
                                                  (w_out, w_proj_attn, w_proj_pool, w_pool))]
    small = (g_small,) + _adam_whole("adam_small", g_small, (w_out, w_proj_attn, w_proj_pool, w_pool),
                                     (m_w_out, m_w_proj_attn, m_w_proj_pool, m_w_pool),
                                     (v_w_out, v_w_proj_attn, v_w_proj_pool, v_w_pool))

    *g_vec, loss = _sum_replicated(vecs_all)
    vecs = (g_vec,) + _adam_whole("adam_replicated", g_vec, (b_gate, pool_scale, ln_gamma, ln_beta),
                                  (m_b_gate, m_pool_scale, m_ln_gamma, m_ln_beta),
                                  (v_b_gate, v_pool_scale, v_ln_gamma, v_ln_beta))
    loss = loss[0, 0]

    def leaves(kind, big):
        out, pa, pp, pool = small[kind]
        bg, ps, gm, bt = vecs[kind]
        return [big[None], bg, pool, ps, pa, pp, out, gm, bt]

    return (loss, _permute_tokens(grad_x)[None], *leaves(0, g_in), *leaves(1, d_in), *leaves(2, m_in), *leaves(3, v_in))
```

```python
import functools

import jax
import jax.numpy as jnp
from jax import lax
from jax.experimental import pallas as pl
from jax.experimental.pallas import tpu as pltpu

F32 = jnp.float32
BF16 = jnp.bfloat16

S = 4096
D = 2048
NW = 16384
AW = 1024
HD = 128
NH = 8
QB = 128
DILATIONS = (1, 4, 16)
POOL_WINDOWS = (2, 4, 8, 16)
PG = 256
N_DEV = 8
COL_Q, COL_K, COL_V = 0, 3 * AW, 6 * AW
COL_ZA, COL_U, COL_ZP, COL_G = 9 * AW, 10 * AW, 11 * AW, 12 * AW
DH_Z, DH_G = COL_ZA, COL_G
ALPHA = 2.0 ** 0.25
LN_EPS = 1e-5
NEG_INF = -1e30
LR, B1, B2, EPS, WD, STEP = 0.001, 0.9, 0.999, 1e-08, 0.01, 10
R_IN, R_OUT, R_PA, R_PP, R_PL = 0, 2048, 2304, 2432, 2560
R_ALL = 2576
R_SMALL = R_ALL - R_OUT
VMEM_LIMIT = 56 * 1024 * 1024
MESH = pl.DeviceIdType.MESH
ANY = pl.BlockSpec(memory_space=pl.ANY)


def _cparams(n_axes):
    return pltpu.CompilerParams(dimension_semantics=("arbitrary",) * n_axes, vmem_limit_bytes=VMEM_LIMIT)


def _sigmoid(z):
    return 0.5 * jnp.tanh(0.5 * z) + 0.5


def _nt(a, b):
    return lax.dot_general(a, b, (((1,), (1,)), ((), ())), preferred_element_type=F32)


def _tn(a, b):
    return lax.dot_general(a, b, (((0,), (0,)), ((), ())), preferred_element_type=F32)


def _nn(a, b):
    return jnp.dot(a, b, preferred_element_type=F32)


def _lin(x, y, c):
    return 4 * x + 2 * y + c


def _flip(v, f):
    return 1 - v if f else v


CHIP_FLIPS = ((0, 0), (1, 0), (0, 1), (1, 1))


AG_PIECES = ((pl.ds(R_IN, D), pl.ds(0, 1024)), (pl.ds(R_IN, D), pl.ds(1024, 1024)),
             (pl.ds(R_OUT, R_PA - R_OUT), pl.ds(0, D)), (pl.ds(R_PA, R_ALL - R_PA), pl.ds(0, D)))
N_PIECES = len(AG_PIECES)
SIB, TO_X, TO_Y, ON, PASS_X, PASS_Y, PASS_D = range(7)
AG_TILES = ((0, 0), (0, 1), (1, 0), (1, 1), (2, 0), (4, 0), (3, 0), (5, 0),
            (2, 1), (4, 1), (3, 1), (5, 1), (6, 0), (6, 1), (7, 0), (7, 1))
XT_TILE = 7
W, G = "wait", "go"
AG_STEPS = {
    2: [(W, SIB, 0)], 3: [(W, SIB, 1)],
    4: [(W, TO_X, 0), (G, ON, 0), (G, PASS_X, 0)], 5: [(W, TO_Y, 0), (G, PASS_Y, 0)],
    6: [(W, PASS_X, 0)], 7: [(W, PASS_Y, 0)],
    8: [(W, TO_X, 1), (G, PASS_X, 1), (W, TO_Y, 1), (G, ON, 1), (G, PASS_Y, 1),
        (G, TO_X, 2), (G, TO_X, 3), (G, TO_Y, 2), (G, TO_Y, 3)],
    10: [(W, PASS_X, 1)], 11: [(W, PASS_Y, 1)],
    12: [(W, ON, 0), (G, PASS_D, 0)], 13: [(W, ON, 1), (G, PASS_D, 1)],
    14: [(W, PASS_D, 0), (W, TO_X, 2), (G, ON, 2), (G, PASS_X, 2), (W, TO_X, 3), (G, PASS_X, 3),
         (W, TO_Y, 2), (G, PASS_Y, 2), (W, TO_Y, 3), (G, ON, 3), (G, PASS_Y, 3)],
    15: [(W, PASS_D, 1)],
}
AG_LAST = [(W, SIB, 2), (W, SIB, 3), (W, ON, 2), (G, PASS_D, 2), (W, ON, 3), (G, PASS_D, 3),
           (W, PASS_X, 2), (W, PASS_X, 3), (W, PASS_Y, 2), (W, PASS_Y, 3), (W, PASS_D, 2), (W, PASS_D, 3)]


def _arrival_order(x, y, c):
    chips = [(x, y), (1 - x, y), (x, 1 - y), (1 - x, 1 - y)]
    return jnp.stack([_lin(px, py, pc) for px, py in chips for pc in (c, 1 - c)]).astype(jnp.int32)


def _ag_proj(order, x2, wbf, spack):
    tm, tn = 1024, 1024
    nrow, ntile = S // tm, len(AG_TILES)
    slabs = jnp.stack([order[pos] for pos, _ in AG_TILES])
    cols = jnp.stack([2 * order[pos] + half for pos, half in AG_TILES])

    def body(cols_ref, slabs_ref, x_ref, w_ref, s_ref, h_ref, gw_ref, xt_ref, wbuf, wsem, send_sems, recv_sems,
             local_sems):
        t, i = pl.program_id(0), pl.program_id(1)
        x, y, c = lax.axis_index("x"), lax.axis_index("y"), lax.axis_index("c")
        me = _lin(x, y, c)
        dev = {"sib": (x, y, 1 - c), "x": (1 - x, y, c), "y": (x, 1 - y, c), "d": (1 - x, 1 - y, c)}

        def slab_of(name, other_core=False):
            px, py, pc = dev[name]
            return _lin(px, py, 1 - pc if other_core else pc)

        def own(piece):
            rows, colz = AG_PIECES[piece]
            return w_ref.at[:, colz] if piece < 2 else s_ref.at[pl.ds(rows.start - R_OUT, rows.size)]

        def rdma(slab, kind, piece, to, from_own=False):
            k = kind * N_PIECES + piece
            there = gw_ref.at[(slab, *AG_PIECES[piece])]
            return pltpu.make_async_remote_copy(
                src_ref=own(piece) if from_own else there, dst_ref=there,
                send_sem=send_sems.at[k], recv_sem=recv_sems.at[k], device_id=dev[to], device_id_type=MESH)

        def mine(kind, piece):
            if kind in (SIB, TO_X, TO_Y):
                return rdma(me, kind, piece, ("sib", "x", "y")[kind], from_own=True)
            if kind == ON:
                frm, to = ("x", "y") if piece % 2 == 0 else ("y", "x")
                return rdma(slab_of(frm), kind, piece, to)
            return rdma(slab_of({PASS_X: "x", PASS_Y: "y", PASS_D: "d"}[kind]), kind, piece, "sib")

        def landing(kind, piece):
            slab = {SIB: slab_of("sib"), TO_X: slab_of("x"), TO_Y: slab_of("y"), ON: slab_of("d"),
                    PASS_X: slab_of("x", True), PASS_Y: slab_of("y", True), PASS_D: slab_of("d", True)}[kind]
            return rdma(slab, kind, piece, "sib")

        def run(steps):
            for what, kind, piece in steps:
                if what == W:
                    landing(kind, piece).wait_recv()
                else:
                    mine(kind, piece).start()

        local = [pltpu.make_async_copy(w_ref, gw_ref.at[me, pl.ds(R_IN, D)], local_sems.at[0]),
                 pltpu.make_async_copy(s_ref, gw_ref.at[me, pl.ds(R_OUT, R_SMALL)], local_sems.at[1])]

        def fetch(slab, half, slot):
            src = own(half) if slab is None else gw_ref.at[(slab, *AG_PIECES[half])]
            return pltpu.make_async_copy(src, wbuf.at[slot], wsem.at[slot])

        @pl.when((t == 0) & (i == 0))
        def _():
            for cp in local:
                cp.start()
            run([(G, kind, piece) for piece in (0, 1) for kind in (TO_X, TO_Y, SIB)] + [(G, SIB, 2), (G, SIB, 3)])
            first = fetch(None, 0, 0)
            first.start()
            first.wait()

        for nxt in range(1, ntile):
            @pl.when((t == nxt - 1) & (i == nrow - 1))
            def _(nxt=nxt):
                run(AG_STEPS.get(nxt, []))
                fetch(None if AG_TILES[nxt][0] == 0 else slabs_ref[nxt], AG_TILES[nxt][1], nxt % 2).start()

        for slot in (0, 1):
            @pl.when(t % 2 == slot)
            def _(slot=slot):
                @pl.when((i == 0) & (t > 0))
                def _():
                    fetch(None, 0, slot).wait()
                h_ref[...] = _nn(x_ref[...].astype(BF16), wbuf[slot]).astype(h_ref.dtype)

        @pl.when(t == XT_TILE)
        def _():
            xt_ref[...] = x_ref[...].T.astype(BF16)

        @pl.when((t == ntile - 1) & (i == nrow - 1))
        def _():
            run(AG_LAST)
            for kind in range(7):
                for piece in range(N_PIECES):
                    mine(kind, piece).wait_send()
            for cp in local:
                cp.wait()

    n_sem = 7 * N_PIECES
    grid_spec = pltpu.PrefetchScalarGridSpec(
        num_scalar_prefetch=2, grid=(ntile, nrow),
        in_specs=[pl.BlockSpec((tm, D), lambda t, i, cols, slabs: (i, 0)), ANY, ANY],
        out_specs=[pl.BlockSpec((tm, tn), lambda t, i, cols, slabs: (i, cols[t])), ANY,
                   pl.BlockSpec((D, tm), lambda t, i, cols, slabs: (
                       0, jnp.where(t < XT_TILE, 0, jnp.where(t == XT_TILE, i, nrow - 1))))],
        scratch_shapes=[pltpu.VMEM((2, D, tn), BF16), pltpu.SemaphoreType.DMA((2,)),
                        pltpu.SemaphoreType.DMA((n_sem,)), pltpu.SemaphoreType.DMA((n_sem,)),
                        pltpu.SemaphoreType.DMA((2,))])
    return pl.pallas_call(
        body, name="ag_proj", grid_spec=grid_spec,
        out_shape=[jax.ShapeDtypeStruct((S, NW), BF16), jax.ShapeDtypeStruct((N_DEV, R_ALL, D), BF16),
                   jax.ShapeDtypeStruct((D, S), BF16)],
        compiler_params=_cparams(2))(cols, slabs, x2, wbf, spack)


def _rs_sibling(p):
    n = len(CHIP_FLIPS)

    def body(p_ref, l_ref, send_sems, recv_sems):
        x, y, c = lax.axis_index("x"), lax.axis_index("y"), lax.axis_index("c")
        copies = [pltpu.make_async_remote_copy(
            src_ref=p_ref.at[_lin(_flip(x, fx), _flip(y, fy), 1 - c)], dst_ref=l_ref.at[k],
            send_sem=send_sems.at[k], recv_sem=recv_sems.at[k], device_id=(x, y, 1 - c), device_id_type=MESH)
            for k, (fx, fy) in enumerate(CHIP_FLIPS)]
        for cp in copies:
            cp.start()
        for cp in copies:
            cp.wait_recv()
        for cp in copies:
            cp.wait_send()

    return pl.pallas_call(
        body, name="rs_sibling", out_shape=jax.ShapeDtypeStruct((n,) + p.shape[1:], p.dtype),
        in_specs=[ANY], out_specs=ANY,
        scratch_shapes=[pltpu.SemaphoreType.DMA((n,)), pltpu.SemaphoreType.DMA((n,))])(p)


def _pair_sum_small(coords, p, l1, q):
    def body(crd, p_ref, l_ref, _, q_ref, buf, sem):
        k = pl.program_id(0)
        buf[...] = (p_ref[...].astype(F32) + l_ref[...].astype(F32)).astype(buf.dtype)
        out = pltpu.make_async_copy(buf, q_ref.at[k, pl.ds(R_OUT, R_SMALL)], sem)
        out.start()
        out.wait()

    def p_map(k, crd):
        fx, fy = k % 2, k // 2
        px = crd[0] + fx - 2 * fx * crd[0]
        py = crd[1] + fy - 2 * fy * crd[1]
        return (_lin(px, py, crd[2]), 0, 0)

    grid_spec = pltpu.PrefetchScalarGridSpec(
        num_scalar_prefetch=1, grid=(4,),
        in_specs=[pl.BlockSpec((None, R_SMALL, D), p_map),
                  pl.BlockSpec((None, R_SMALL, D), lambda k, crd: (k, 0, 0)), ANY],
        out_specs=ANY,
        scratch_shapes=[pltpu.VMEM((R_SMALL, D), BF16), pltpu.SemaphoreType.DMA(())])
    return pl.pallas_call(body, name="pair_sum_small", grid_spec=grid_spec,
                          out_shape=jax.ShapeDtypeStruct(q.shape, q.dtype), input_output_aliases={3: 0},
                          compiler_params=_cparams(1))(coords, p, l1, q)


def _h_block(k):
    return jnp.where(k < 9, (k % 3) * 3 + k // 3, k)


RS_PIECES = (pl.ds(0, 1280), pl.ds(1280, R_ALL - 1280))
RS_ROWS = (1280, R_ALL - 1280)
RS_CHUNKS = ((320,) * 4, (432,) * 3)
RS_MERGE_STEP = 2


def _grad_x_rs(dh, g, dr, q, vec):
    others = [(fx, fy, fc) for fx in (0, 1) for fy in (0, 1) for fc in (0, 1) if (fx, fy, fc) != (0, 0, 0)]
    tm, tk = 1024, 1024
    ni, nk = S // tm, NW // tk
    rmax = max(RS_ROWS)
    cmax = max(max(c) for c in RS_CHUNKS)

    def body(dh_ref, w_ref, dr_ref, q_ref, vec_ref, o_ref, l2_ref, ld_ref, mg_ref, all_ref, va, vb,
             send_sems, recv_sems, sems):
        i, k = pl.program_id(0), pl.program_id(1)
        x, y, c = lax.axis_index("x"), lax.axis_index("y"), lax.axis_index("c")
        nbr = ((1 - x, y, c), (x, 1 - y, c))

        def vec_copy(n, sender):
            sx, sy, sc = sender
            fx, fy, fc = others[n]
            return pltpu.make_async_remote_copy(
                src_ref=vec_ref, dst_ref=all_ref.at[_lin(sx, sy, sc)], send_sem=send_sems.at[6 + n],
                recv_sem=recv_sems.at[6 + n], device_id=(_flip(sx, fx), _flip(sy, fy), _flip(sc, fc)),
                device_id_type=MESH)

        vec_own = pltpu.make_async_copy(vec_ref, all_ref.at[_lin(x, y, c)], sems.at[3])

        def rows(ref, piece):
            return ref.at[piece, pl.ds(0, RS_ROWS[piece])]

        copies = (
            (q_ref.at[3, RS_PIECES[0]], rows(ld_ref, 0), 0),
            (q_ref.at[3, RS_PIECES[1]], rows(ld_ref, 1), 1),
            (q_ref.at[1, RS_PIECES[0]], l2_ref.at[0, RS_PIECES[0]], 0),
            (q_ref.at[2, RS_PIECES[1]], l2_ref.at[1, RS_PIECES[1]], 1),
            (rows(mg_ref, 0), l2_ref.at[1, RS_PIECES[0]], 1),
            (rows(mg_ref, 1), l2_ref.at[0, RS_PIECES[1]], 0),
        )

        def copy(n):
            src, dst, axis = copies[n]
            return pltpu.make_async_remote_copy(src_ref=src, dst_ref=dst, send_sem=send_sems.at[n],
                                                recv_sem=recv_sems.at[n], device_id=nbr[axis], device_id_type=MESH)

        def merge(piece, mine):
            start = 0
            for n_rows in RS_CHUNKS[piece]:
                own = pltpu.make_async_copy(q_ref.at[mine, pl.ds(RS_PIECES[piece].start + start, n_rows)],
                                            va.at[pl.ds(0, n_rows)], sems.at[0])
                got = pltpu.make_async_copy(ld_ref.at[piece, pl.ds(start, n_rows)], vb.at[pl.ds(0, n_rows)], sems.at[1])
                own.start()
                got.start()
                own.wait()
                got.wait()
                va[pl.ds(0, n_rows)] = (va[pl.ds(0, n_rows)].astype(F32)
                                        + vb[pl.ds(0, n_rows)].astype(F32)).astype(va.dtype)
                out = pltpu.make_async_copy(va.at[pl.ds(0, n_rows)], mg_ref.at[piece, pl.ds(start, n_rows)], sems.at[2])
                out.start()
                out.wait()
                start += n_rows

        @pl.when((i == 0) & (k == 0))
        def _():
            for n in range(4):
                copy(n).start()
            vec_own.start()
            for n in range(len(others)):
                vec_copy(n, (x, y, c)).start()

        @pl.when((i == RS_MERGE_STEP) & (k == 0))
        def _():
            copy(0).wait_recv()
            merge(0, 2)
            copy(4).start()
            copy(1).wait_recv()
            merge(1, 1)
            copy(5).start()

        @pl.when(k == 0)
        def _():
            o_ref[...] = ALPHA * dr_ref[...]

        o_ref[...] += _nt(dh_ref[...], w_ref[...])

        @pl.when((i == ni - 1) & (k == nk - 1))
        def _():
            for n in range(2, 6):
                copy(n).wait_recv()
            for n in range(6):
                copy(n).wait_send()
            for n, (fx, fy, fc) in enumerate(others):
                vec_copy(n, (_flip(x, fx), _flip(y, fy), _flip(c, fc))).wait_recv()
                vec_copy(n, (x, y, c)).wait_send()
            vec_own.wait()

    slab = q.shape[1:]
    out = pl.pallas_call(
        body, name="grad_x_rs", grid=(ni, nk),
        in_specs=[pl.BlockSpec((tm, tk), lambda i, k: (i, k)),
                  pl.BlockSpec((None, D, tk), lambda i, k: (_h_block(k) // 2, 0, _h_block(k) % 2)),
                  pl.BlockSpec((tm, D), lambda i, k: (i, 0)), ANY, ANY],
        out_specs=[pl.BlockSpec((tm, D), lambda i, k: (i, 0)), ANY, ANY, ANY, ANY],
        out_shape=[jax.ShapeDtypeStruct((S, D), F32), jax.ShapeDtypeStruct((2,) + slab, q.dtype),
                   jax.ShapeDtypeStruct((2, rmax, slab[1]), q.dtype), jax.ShapeDtypeStruct((2, rmax, slab[1]), q.dtype),
                   jax.ShapeDtypeStruct((N_DEV,) + vec.shape, vec.dtype)],
        scratch_shapes=[pltpu.VMEM((cmax, slab[1]), q.dtype), pltpu.VMEM((cmax, slab[1]), q.dtype),
                        pltpu.SemaphoreType.DMA((6 + len(others),)), pltpu.SemaphoreType.DMA((6 + len(others),)),
                        pltpu.SemaphoreType.DMA((4,))],
        compiler_params=_cparams(2))(dh, g, dr, q, vec)
    return out[0], out[1], out[4]


GW_TN = 512
GW_PARTS = D // GW_TN


def _rs_columns(x, y, c):
    per_h = AW // GW_TN
    out = []
    for core in (1 - c, c):
        for fx, fy in CHIP_FLIPS:
            for part in range(GW_PARTS):
                h_block = 2 * _lin(_flip(x, fx), _flip(y, fy), core) + part // per_h
                out.append(_h_block(h_block) * per_h + part % per_h)
    return jnp.stack(out).astype(jnp.int32)


def _grad_w_in_rs(cols, xt, dh):
    n_tile = 4 * GW_PARTS

    def body(cols_ref, a_ref, b_ref, q_ref, l1_ref, stage, landed, send_sems, recv_sems, sem):
        t = pl.program_id(0)
        sib = (lax.axis_index("x"), lax.axis_index("y"), 1 - lax.axis_index("c"))

        def there(n):
            return l1_ref.at[n // GW_PARTS, :, pl.ds((n % GW_PARTS) * GW_TN, GW_TN)]

        def send(n):
            return pltpu.make_async_remote_copy(src_ref=stage.at[n % 2], dst_ref=there(n), send_sem=send_sems.at[n],
                                                recv_sem=recv_sems.at[n], device_id=sib, device_id_type=MESH)

        def fetch(n):
            return pltpu.make_async_copy(there(n), landed, sem)

        for n in range(n_tile):
            @pl.when(t == n_tile + n)
            def _(n=n):
                if n == 0:
                    send(n_tile - 2).wait_send()
                    send(n_tile - 1).wait_send()
                send(n).wait_recv()
                fetch(n).start()

        part = _nn(a_ref[...], b_ref[...])

        for n in range(n_tile):
            @pl.when(t == n)
            def _(n=n):
                if n >= 2:
                    send(n - 2).wait_send()
                stage[n % 2] = part.astype(stage.dtype)
                send(n).start()

            @pl.when(t == n_tile + n)
            def _(n=n):
                fetch(n).wait()
                q_ref[...] = (part + landed[...].astype(F32)).astype(q_ref.dtype)

    mine = lambda t: jnp.maximum(t - n_tile, 0)
    grid_spec = pltpu.PrefetchScalarGridSpec(
        num_scalar_prefetch=1, grid=(2 * n_tile,),
        in_specs=[pl.BlockSpec((D, S), lambda t, cols: (0, 0), pipeline_mode=pl.Buffered(1)),
                  pl.BlockSpec((S, GW_TN), lambda t, cols: (0, cols[t]))],
        out_specs=[pl.BlockSpec((None, D, GW_TN), lambda t, cols: (mine(t) // GW_PARTS, 0, mine(t) % GW_PARTS)), ANY],
        scratch_shapes=[pltpu.VMEM((2, D, GW_TN), BF16), pltpu.VMEM((D, GW_TN), BF16),
                        pltpu.SemaphoreType.DMA((n_tile,)), pltpu.SemaphoreType.DMA((n_tile,)),
                        pltpu.SemaphoreType.DMA(())])
    q, _ = pl.pallas_call(
        body, name="grad_w_in_rs", grid_spec=grid_spec,
        out_shape=[jax.ShapeDtypeStruct((4, R_ALL, D), BF16), jax.ShapeDtypeStruct((4, D, D), BF16)],
        compiler_params=_cparams(1))(cols, xt, dh)
    return q


def _grad_w(name, at, b):
    m, n_all = at.shape[0], b.shape[1]

    def body(a_ref, b_ref, o_ref):
        o_ref[...] = _nn(a_ref[...], b_ref[...]).astype(o_ref.dtype)

    return pl.pallas_call(
        body, name=name, grid=(n_all // GW_TN,),
        in_specs=[pl.BlockSpec((m, S), lambda n: (0, 0), pipeline_mode=pl.Buffered(1)),
                  pl.BlockSpec((S, GW_TN), lambda n: (0, n))],
        out_specs=pl.BlockSpec((m, GW_TN), lambda n: (0, n)), out_shape=jax.ShapeDtypeStruct((m, n_all), BF16),
        compiler_params=_cparams(1))(at, b)


NR = 16
TI = 16
TM = NR * TI
NT = S // TM
ATT_QB = (256, 128, 256)
ATT_NB = (16, 8, 1)
ATT_BLOCKS = (16, 32, 16)


def _permute_tokens(a):
    return a.reshape(NT, TI, NR, a.shape[-1]).transpose(0, 2, 1, 3).reshape(a.shape)


def _attn_shape(g, c):
    if g == 0:
        return (S, c)
    if g == 1:
        return (NT, 4, 4, TI, c)
    return (NT, NR, TI, c)


def _attn_view(g, a):
    return a.reshape(_attn_shape(g, a.shape[-1]))


def _attn_spec(g, width, col, blk):
    if g == 0:
        return pl.BlockSpec((TM, width), lambda b: (blk(b), col))
    if g == 1:
        return pl.BlockSpec((2, 4, None, TI, width), lambda b: (blk(b) % 8, 0, blk(b) // 8, 0, col))
    return pl.BlockSpec((NT, None, TI, width), lambda b: (0, blk(b), 0, col))


def _pieces(g):
    if g == 1:
        return [(t, m) for t in range(2) for m in range(4)]
    return [(t,) for t in range(NT)]


def _get(g, ref, sl):
    if g == 0:
        return ref[:, sl]
    return jnp.concatenate([ref[(*p, slice(None), sl)] for p in _pieces(g)], axis=0)


def _put(g, ref, sl, val):
    if g == 0:
        ref[:, sl] = val
    else:
        for n, p in enumerate(_pieces(g)):
            ref[(*p, slice(None), sl)] = val[TI * n:TI * (n + 1)]


def _block_pos(g, a):
    if g == 0:
        return 16 * (a % 16) + a // 16
    if g == 1:
        return 64 * (a // 64) + 4 * (a % 16) + (a // 16) % 4
    return a


def _attn_mask(g, n):
    qb = ATT_QB[g]
    if ATT_NB[g] == 1:
        qa = lax.broadcasted_iota(jnp.int32, (qb, qb), 0)
        kc = lax.broadcasted_iota(jnp.int32, (qb, qb), 1)
        dist = _block_pos(g, qa) - _block_pos(g, kc)
        return (dist >= 0) & (dist <= QB)
    qa = lax.broadcasted_iota(jnp.int32, (qb, 2 * qb), 0)
    kc = lax.broadcasted_iota(jnp.int32, (qb, 2 * qb), 1)
    cur = kc >= qb
    dist = _block_pos(g, qa) - _block_pos(g, kc % qb) + jnp.where(cur, 0, qb)
    return (dist >= 0) & (dist <= QB) & (cur | (n > 0))


def _keys(g, prev_ref, cur_ref, sl):
    if ATT_NB[g] == 1:
        return _get(g, cur_ref, sl)
    return jnp.concatenate([_get(g, prev_ref, sl), _get(g, cur_ref, sl)], axis=0)


def _qkv_specs(g, clamp):
    cur = lambda col: _attn_spec(g, AW, col, clamp)
    prev = lambda col: _attn_spec(g, AW, col, lambda b: jnp.maximum(clamp(b) - 1, 0))
    qc, kc, vc = (c // AW + g for c in (COL_Q, COL_K, COL_V))
    return [cur(qc), cur(kc), prev(kc), cur(vc), prev(vc)]


def _attn_fwd(g, h):
    scale = HD ** -0.5
    hv = _attn_view(g, h)

    def body(q_ref, kc_ref, kp_ref, vc_ref, vp_ref, o_ref, l_ref):
        valid = _attn_mask(g, pl.program_id(0) % ATT_NB[g])
        for hh in range(NH):
            sl = slice(hh * HD, (hh + 1) * HD)
            kh, vh = _keys(g, kp_ref, kc_ref, sl), _keys(g, vp_ref, vc_ref, sl)
            s = jnp.where(valid, _nt(_get(g, q_ref, sl), kh) * scale, NEG_INF)
            m = jnp.max(s, axis=-1, keepdims=True)
            e = jnp.exp(s - m)
            den = jnp.sum(e, axis=-1, keepdims=True)
            _put(g, o_ref, sl, (_nn(e.astype(BF16), vh) * (1.0 / den)).astype(o_ref.dtype))
            _put(g, l_ref, slice(hh, hh + 1), m + jnp.log(den))

    same = lambda b: b
    o, lse = pl.pallas_call(
        body, name=f"attn_fwd_{g}", grid=(ATT_BLOCKS[g],),
        in_specs=_qkv_specs(g, same),
        out_specs=[_attn_spec(g, AW, 0, same), _attn_spec(g, NH, 0, same)],
        out_shape=[jax.ShapeDtypeStruct(_attn_shape(g, AW), BF16), jax.ShapeDtypeStruct(_attn_shape(g, NH), F32)],
        compiler_params=_cparams(1))(hv, hv, hv, hv, hv)
    return o.reshape(S, AW), lse.reshape(S, NH)


def _attn_bwd(g, dh, h, do, lse, delta):
    scale = HD ** -0.5
    qb = ATT_QB[g]
    carried = ATT_NB[g] > 1
    last = ATT_BLOCKS[g] - 1
    clamp = lambda b: jnp.minimum(b, last)
    behind = lambda b: jnp.maximum(b - 1, 0)
    hv = _attn_view(g, h)

    def body(q_ref, kc_ref, kp_ref, vc_ref, vp_ref, do_ref, l_ref, dl_ref, _, dh_ref, *carry):
        b = pl.program_id(0)

        def write(col, val):
            _put(g, dh_ref, slice(col, col + HD), val.astype(dh_ref.dtype))

        def block():
            valid = _attn_mask(g, b % ATT_NB[g])
            for hh in range(NH):
                sl = slice(hh * HD, (hh + 1) * HD)
                one = slice(hh, hh + 1)
                qh, doh = _get(g, q_ref, sl), _get(g, do_ref, sl)
                kh, vh = _keys(g, kp_ref, kc_ref, sl), _keys(g, vp_ref, vc_ref, sl)
                s = _nt(qh, kh) * scale
                p = jnp.where(valid, jnp.exp(s - _get(g, l_ref, one)), 0.0)
                ds = p * (_nt(doh, vh) - _get(g, dl_ref, one))
                dsb = (ds * scale).astype(BF16)
                dq = _nn(dsb, kh)
                dk2 = _tn(dsb, qh)
                dv2 = _tn(p.astype(BF16), doh)
                if carried:
                    cq_ref, ck_ref, cv_ref = carry
                    write(hh * HD, cq_ref[:, sl])
                    write(AW + hh * HD, ck_ref[:, sl] + dk2[:qb])
                    write(2 * AW + hh * HD, cv_ref[:, sl] + dv2[:qb])
                    cq_ref[:, sl] = dq
                    ck_ref[:, sl] = dk2[qb:]
                    cv_ref[:, sl] = dv2[qb:]
                else:
                    write(hh * HD, dq)
                    write(AW + hh * HD, dk2)
                    write(2 * AW + hh * HD, dv2)

        if not carried:
            block()
            return

        @pl.when(b == 0)
        def _():
            for ref in carry:
                ref[...] = jnp.zeros_like(ref)

        pl.when(b <= last)(block)

        @pl.when(b > last)
        def _():
            for hh in range(NH):
                for n, ref in enumerate(carry):
                    write(n * AW + hh * HD, ref[:, hh * HD:(hh + 1) * HD])

    out = pl.pallas_call(
        body, name=f"attn_bwd_{g}", grid=(ATT_BLOCKS[g] + carried,),
        in_specs=_qkv_specs(g, clamp) + [_attn_spec(g, AW, 0, clamp), _attn_spec(g, NH, 0, clamp),
                                         _attn_spec(g, NH, 0, clamp), ANY],
        out_specs=_attn_spec(g, 3 * AW, g, behind if carried else clamp),
        out_shape=jax.ShapeDtypeStruct(_attn_shape(g, NW), BF16),
        input_output_aliases={8: 0},
        scratch_shapes=[pltpu.VMEM((qb, AW), F32)] * (3 if carried else 0),
        compiler_params=_cparams(1))(hv, hv, hv, hv, hv, _attn_view(g, do), _attn_view(g, lse), _attn_view(g, delta),
                                     _attn_view(g, dh))
    return out.reshape(S, NW)


def _group_weights(l0, l1, l2):
    m = jnp.maximum(jnp.maximum(l0, l1), l2)
    e0, e1, e2 = jnp.exp(l0 - m), jnp.exp(l1 - m), jnp.exp(l2 - m)
    inv = 1.0 / (e0 + e1 + e2)
    return e0 * inv, e1 * inv, e2 * inv


def _residue(ref, r, sl):
    return ref[r * TI:(r + 1) * TI, sl].astype(F32)


def _total(parts):
    return functools.reduce(lambda x, y: x + y, parts)


def _pool_tokens(up_ref, uc_ref, p_ref, tile):
    j0 = lax.broadcasted_iota(jnp.int32, (TI, 1), 0) == 0
    first = (tile == 0) & j0
    for r in range(NR):
        out = []
        for g, w in enumerate(POOL_WINDOWS):
            sl = slice(g * PG, (g + 1) * PG)
            own = _residue(uc_ref, r, sl)
            acc = _total([own] + [_residue(uc_ref, r - k, sl) for k in range(1, min(r, w - 1) + 1)])
            wrapped = [NR + r - k for k in range(r + 1, w)]
            if wrapped:
                wc = _total([_residue(uc_ref, q, sl) for q in wrapped])
                wp = jnp.where(tile > 0, _total([_residue(up_ref, q, sl) for q in wrapped]), 0.0)
                acc = acc + jnp.where(j0, pltpu.roll(wp, 1, 0), pltpu.roll(wc, 1, 0))
            out.append(acc * jnp.where(first, 1.0 / min(r + 1, w), 1.0 / w) - own)
        p_ref[r * TI:(r + 1) * TI, :] = jnp.concatenate(out, axis=1).astype(p_ref.dtype)


def _pool_tokens_bwd(dp, nxt_ref, du_ref, tile):
    ji = lax.broadcasted_iota(jnp.int32, (TI, 1), 0)
    first = (tile == 0) & (ji == 0)
    piece = lambda g, r: dp[g][r * TI:(r + 1) * TI]
    dpc = [[piece(g, r) * jnp.where(first, 1.0 / min(r + 1, w), 1.0 / w) for r in range(NR)]
           for g, w in enumerate(POOL_WINDOWS)]
    for r in range(NR):
        out = []
        for g, w in enumerate(POOL_WINDOWS):
            sl = slice(g * PG, (g + 1) * PG)
            acc = _total([dpc[g][r + k] for k in range(w) if r + k < NR])
            wrapped = [r + k - NR for k in range(1, w) if r + k >= NR]
            if wrapped:
                wc = _total([dpc[g][q] for q in wrapped])
                wn = _total([nxt_ref[q * TI:(q + 1) * TI, sl] for q in wrapped])
                acc = acc + jnp.where(ji == TI - 1, pltpu.roll(wn, TI - 1, 0), pltpu.roll(wc, TI - 1, 0))
            out.append(acc - piece(g, r))
        du_ref[r * TI:(r + 1) * TI, :] = jnp.concatenate(out, axis=1).astype(du_ref.dtype)
    for r in range(NR):
        nxt_ref[r * TI:(r + 1) * TI, :] = jnp.concatenate([dpc[g][r] for g in range(len(POOL_WINDOWS))], axis=1)


def _pool_linear(pb, wpool_ref):
    return jnp.concatenate([_nn(pb[:, g * PG:(g + 1) * PG], wpool_ref[g]) for g in range(len(POOL_WINDOWS))], axis=1)


def _tok(width, col=0, rev=False):
    if rev:
        return pl.BlockSpec((TM, width), lambda i: (NT - 1 - i, col))
    return pl.BlockSpec((TM, width), lambda i: (i, col))


def _whole(shape):
    return pl.BlockSpec(shape, lambda i: (0,) * len(shape))


def _mix_fwd(h, o, lse, wpa, wpp, wpool, pscale, bgate):
    def body(o0_ref, o1_ref, o2_ref, l0_ref, l1_ref, l2_ref, za_ref, uc_ref, up_ref, zp_ref, gp_ref,
             wpa_ref, wpp_ref, wpool_ref, ps_ref, bg_ref,
             mg_ref, a_ref, b_ref, p_ref, yat_ref, ypt_ref, mgt_ref, ya_ref, yp_ref):
        i = pl.program_id(0)
        w0, w1, w2 = _group_weights(l0_ref[...], l1_ref[...], l2_ref[...])
        za = za_ref[...].astype(F32)
        silu_a = za * _sigmoid(za)
        for hh in range(NH):
            sl = slice(hh * HD, (hh + 1) * HD)
            c = slice(hh, hh + 1)
            oh = (w0[:, c] * o0_ref[:, sl].astype(F32) + w1[:, c] * o1_ref[:, sl].astype(F32)
                  + w2[:, c] * o2_ref[:, sl].astype(F32))
            ya = oh * silu_a[:, sl]
            ya_ref[:, sl] = ya.astype(BF16)
            yat_ref[sl, :] = ya.T.astype(BF16)
        _pool_tokens(up_ref, uc_ref, p_ref, i)
        zp = zp_ref[...].astype(F32)
        yp = _pool_linear(p_ref[...], wpool_ref) * ps_ref[...] * (zp * _sigmoid(zp))
        yp_ref[...] = yp.astype(BF16)
        ypt_ref[...] = yp.T.astype(BF16)
        a = _nn(ya_ref[...], wpa_ref[...])
        b = _nn(yp_ref[...], wpp_ref[...])
        a_ref[...] = a.astype(BF16)
        b_ref[...] = b.astype(BF16)
        gates = _sigmoid(gp_ref[...].astype(F32) + bg_ref[...])
        mg = gates[:, :D] * a + gates[:, D:] * b
        mg_ref[...] = mg.astype(BF16)
        mgt_ref[...] = mg.T.astype(BF16)

    u_prev = pl.BlockSpec((TM, AW), lambda i: (jnp.maximum(i - 1, 0), COL_U // AW))
    across = lambda width: pl.BlockSpec((width, TM), lambda i: (0, i))
    return pl.pallas_call(
        body, name="mix_fwd", grid=(NT,),
        in_specs=[_tok(AW)] * 3 + [_tok(NH)] * 3
        + [_tok(AW, COL_ZA // AW), _tok(AW, COL_U // AW), u_prev, _tok(AW, COL_ZP // AW), _tok(2 * D, COL_G // (2 * D))]
        + [_whole((AW, D)), _whole((AW, D)), _whole((4, PG, PG)), _whole((1, AW)), _whole((1, 2 * D))],
        out_specs=[_tok(D), _tok(D), _tok(D), _tok(AW), across(AW), across(AW), across(D)],
        out_shape=[jax.ShapeDtypeStruct((S, D), BF16)] * 3 + [jax.ShapeDtypeStruct((S, AW), BF16)]
        + [jax.ShapeDtypeStruct((AW, S), BF16)] * 2 + [jax.ShapeDtypeStruct((D, S), BF16)],
        scratch_shapes=[pltpu.VMEM((TM, AW), BF16), pltpu.VMEM((TM, AW), BF16)],
        compiler_params=_cparams(1))(*o, *lse, h, h, h, h, h, wpa, wpp, wpool, pscale, bgate)


def _out_ln(merged, x, target, wout, gamma, beta):
    def body(mg_ref, x_ref, t_ref, w_ref, g_ref, b_ref, dr_ref, drb_ref, dm_ref, loss_ref, dg_ref, db_ref):
        i = pl.program_id(0)

        @pl.when(i == 0)
        def _():
            loss_ref[...] = jnp.zeros_like(loss_ref)
            dg_ref[...] = jnp.zeros_like(dg_ref)
            db_ref[...] = jnp.zeros_like(db_ref)

        r = ALPHA * x_ref[...] + _nn(mg_ref[...], w_ref[...])
        mu = jnp.mean(r, axis=-1, keepdims=True)
        rc = r - mu
        rstd = lax.rsqrt(jnp.mean(rc * rc, axis=-1, keepdims=True) + LN_EPS)
        xhat = rc * rstd
        err = xhat * g_ref[...] + b_ref[...] - t_ref[...]
        loss_ref[...] += 0.5 * jnp.sum(jnp.mean(err * err, axis=-1, keepdims=True), axis=0, keepdims=True)
        dy = err * (1.0 / D)
        dg_ref[...] += jnp.sum(dy * xhat, axis=0, keepdims=True)
        db_ref[...] += jnp.sum(dy, axis=0, keepdims=True)
        dxh = dy * g_ref[...]
        dr = rstd * (dxh - jnp.mean(dxh, axis=-1, keepdims=True)
                     - xhat * jnp.mean(dxh * xhat, axis=-1, keepdims=True))
        dr_ref[...] = dr
        drb_ref[...] = dr.astype(BF16)
        dm_ref[...] = _nt(drb_ref[...], w_ref[...]).astype(BF16)

    return pl.pallas_call(
        body, name="out_ln", grid=(NT,),
        in_specs=[_tok(D), _tok(D), _tok(D), _whole((D, D)), _whole((1, D)), _whole((1, D))],
        out_specs=[_tok(D), _tok(D), _tok(D), _whole((8, 128)), _whole((1, D)), _whole((1, D))],
        out_shape=[jax.ShapeDtypeStruct((S, D), F32), jax.ShapeDtypeStruct((S, D), BF16),
                   jax.ShapeDtypeStruct((S, D), BF16), jax.ShapeDtypeStruct((8, 128), F32),
                   jax.ShapeDtypeStruct((1, D), F32), jax.ShapeDtypeStruct((1, D), F32)],
        compiler_params=_cparams(1))(merged, x, target, wout, gamma, beta)


def _gate_bwd(dm, a, b, h, bgate):
    def body(dm_ref, a_ref, b_ref, gp_ref, bg_ref, dgp_ref, da_ref, db_ref, dbg_ref):
        @pl.when(pl.program_id(0) == 0)
        def _():
            dbg_ref[...] = jnp.zeros_like(dbg_ref)

        dm_ = dm_ref[...].astype(F32)
        gates = _sigmoid(gp_ref[...].astype(F32) + bg_ref[...])
        ga, gb = gates[:, :D], gates[:, D:]
        da_ref[...] = (dm_ * ga).astype(BF16)
        db_ref[...] = (dm_ * gb).astype(BF16)
        dgp = jnp.concatenate([dm_ * a_ref[...].astype(F32) * ga * (1.0 - ga),
                               dm_ * b_ref[...].astype(F32) * gb * (1.0 - gb)], axis=1)
        dgp_ref[...] = dgp.astype(BF16)
        dbg_ref[...] += jnp.sum(dgp, axis=0, keepdims=True)

    return pl.pallas_call(
        body, name="gate_bwd", grid=(NT,),
        in_specs=[_tok(D), _tok(D), _tok(D), _tok(2 * D, COL_G // (2 * D)), _whole((1, 2 * D))],
        out_specs=[_tok(2 * D, DH_G // (2 * D)), _tok(D), _tok(D), _whole((1, 2 * D))],
        out_shape=[jax.ShapeDtypeStruct((S, NW), BF16), jax.ShapeDtypeStruct((S, D), BF16),
                   jax.ShapeDtypeStruct((S, D), BF16), jax.ShapeDtypeStruct((1, 2 * D), F32)],
        compiler_params=_cparams(1))(dm, a, b, h, bgate)


def _mix_bwd(dh, da, db, h, o, lse, p, wpa, wpp, wpool, pscale):
    def body(_, da_ref, db_ref, o0_ref, o1_ref, o2_ref, l0_ref, l1_ref, l2_ref, za_ref, zp_ref, p_ref,
             wpa_ref, wpp_ref, wpool_ref, ps_ref,
             dh_ref, do0_ref, do1_ref, do2_ref, dl0_ref, dl1_ref, dl2_ref, dwp_ref, dps_ref,
             nxt_ref):
        i = pl.program_id(0)
        tile = NT - 1 - i
        dza_ref, du_ref, dzp_ref = (dh_ref.at[:, pl.ds(n * AW, AW)] for n in range(3))

        @pl.when(i == 0)
        def _():
            nxt_ref[...] = jnp.zeros_like(nxt_ref)
            dwp_ref[...] = jnp.zeros_like(dwp_ref)
            dps_ref[...] = jnp.zeros_like(dps_ref)

        dya = _nt(da_ref[...], wpa_ref[...])
        w0, w1, w2 = _group_weights(l0_ref[...], l1_ref[...], l2_ref[...])
        za = za_ref[...].astype(F32)
        sig = _sigmoid(za)
        silu_a = za * sig
        dsilu_a = sig * (1.0 + za * (1.0 - sig))
        for hh in range(NH):
            sl = slice(hh * HD, (hh + 1) * HD)
            c = slice(hh, hh + 1)
            oh = (w0[:, c] * o0_ref[:, sl].astype(F32) + w1[:, c] * o1_ref[:, sl].astype(F32)
                  + w2[:, c] * o2_ref[:, sl].astype(F32))
            doh = dya[:, sl] * silu_a[:, sl]
            dza_ref[:, sl] = (dya[:, sl] * oh * dsilu_a[:, sl]).astype(BF16)
            dot_ = jnp.sum(doh * oh, axis=-1, keepdims=True)
            do0_ref[:, sl] = (w0[:, c] * doh).astype(BF16)
            do1_ref[:, sl] = (w1[:, c] * doh).astype(BF16)
            do2_ref[:, sl] = (w2[:, c] * doh).astype(BF16)
            dl0_ref[:, c] = w0[:, c] * dot_
            dl1_ref[:, c] = w1[:, c] * dot_
            dl2_ref[:, c] = w2[:, c] * dot_
        dyp = _nt(db_ref[...], wpp_ref[...])
        pb = p_ref[...]
        pw = _pool_linear(pb, wpool_ref)
        zp = zp_ref[...].astype(F32)
        sigp = _sigmoid(zp)
        dypre = dyp * (zp * sigp)
        dzp_ref[...] = (dyp * (pw * ps_ref[...]) * (sigp * (1.0 + zp * (1.0 - sigp)))).astype(BF16)
        dps_ref[...] += jnp.sum(dypre * pw, axis=0, keepdims=True)
        dpw = (dypre * ps_ref[...]).astype(BF16)
        dp = []
        for g in range(len(POOL_WINDOWS)):
            sl = slice(g * PG, (g + 1) * PG)
            dwp_ref[g] += _tn(pb[:, sl], dpw[:, sl])
            dp.append(_nt(dpw[:, sl], wpool_ref[g]))
        _pool_tokens_bwd(dp, nxt_ref, du_ref, tile)

    r = functools.partial(_tok, rev=True)
    return pl.pallas_call(
        body, name="mix_bwd", grid=(NT,),
        in_specs=[ANY, r(D), r(D)] + [r(AW)] * 3 + [r(NH)] * 3 + [r(AW, COL_ZA // AW), r(AW, COL_ZP // AW), r(AW)]
        + [_whole((AW, D)), _whole((AW, D)), _whole((4, PG, PG)), _whole((1, AW))],
        out_specs=[r(3 * AW, DH_Z // (3 * AW))] + [r(AW)] * 3 + [r(NH)] * 3 + [_whole((4, PG, PG)), _whole((1, AW))],
        out_shape=[jax.ShapeDtypeStruct((S, NW), BF16)] + [jax.ShapeDtypeStruct((S, AW), BF16)] * 3
        + [jax.ShapeDtypeStruct((S, NH), F32)] * 3
        + [jax.ShapeDtypeStruct((4, PG, PG), F32), jax.ShapeDtypeStruct((1, AW), F32)],
        input_output_aliases={0: 0},
        scratch_shapes=[pltpu.VMEM((TM, AW), F32)],
        compiler_params=_cparams(1))(dh, da, db, *o, *lse, h, h, p, wpa, wpp, wpool, pscale)


def _adamw(w, g, m, v):
    m = B1 * m + (1.0 - B1) * g
    v = B2 * v + (1.0 - B2) * jnp.square(g)
    m_hat = m / (1.0 - B1 ** STEP)
    v_hat = v / (1.0 - B2 ** STEP)
    return -LR * (m_hat / (jnp.sqrt(v_hat) + EPS) + WD * w), m, v


def _adam_shard(name, q, l2, w, m, v, tr):
    rows = w.shape[0]

    def body(q_ref, l_ref, w_ref, m_ref, v_ref, g_out, d_out, m_out, v_out):
        g = q_ref[...].astype(F32)
        for k in range(2):
            g = g + l_ref[k].astype(F32)
        g_out[...] = g
        d_out[...], m_out[...], v_out[...] = _adamw(w_ref[...], g, m_ref[...], v_ref[...])

    blk = pl.BlockSpec((tr, D), lambda i: (i, 0))
    return pl.pallas_call(
        body, name=name, grid=(rows // tr,),
        in_specs=[pl.BlockSpec((None, tr, D), lambda i: (0, i, 0)), pl.BlockSpec((2, tr, D), lambda i: (0, i, 0)),
                  blk, blk, blk],
        out_specs=[blk] * 4, out_shape=[jax.ShapeDtypeStruct((rows, D), F32)] * 4,
        compiler_params=_cparams(1))(q, l2, w, m, v)


def _sum_small(q, l2):
    def body(q_ref, l_ref, g_out, buf, sems):
        rows = pl.ds(R_OUT, R_SMALL)
        copies = [pltpu.make_async_copy(src, buf.at[n], sems.at[n])
                  for n, src in enumerate((q_ref.at[0, rows], l_ref.at[0, rows], l_ref.at[1, rows]))]
        for cp in copies:
            cp.start()
        for cp in copies:
            cp.wait()
        g_out[...] = buf[0].astype(F32) + buf[1].astype(F32) + buf[2].astype(F32)

    return pl.pallas_call(
        body, name="sum_small", in_specs=[ANY, ANY], out_shape=jax.ShapeDtypeStruct((R_SMALL, D), F32),
        scratch_shapes=[pltpu.VMEM((3, R_SMALL, D), q.dtype), pltpu.SemaphoreType.DMA((3,))],
        compiler_params=pltpu.CompilerParams(vmem_limit_bytes=VMEM_LIMIT))(q, l2)


def _adam_whole(name, grads, weights, ms, vs):
    n = len(grads)

    def body(*refs):
        ins, outs = refs[:4 * n], refs[4 * n:]
        for t in range(n):
            g, w, m, v = (ins[k * n + t][...] for k in range(4))
            outs[t][...], outs[n + t][...], outs[2 * n + t][...] = _adamw(w, g, m, v)

    out = pl.pallas_call(
        body, name=name, out_shape=[jax.ShapeDtypeStruct(w.shape, F32) for w in weights] * 3,
        compiler_params=pltpu.CompilerParams(vmem_limit_bytes=VMEM_LIMIT))(*grads, *weights, *ms, *vs)
    return out[:n], out[n:2 * n], out[2 * n:]


def _sum_replicated(gathered):
    def body(g_ref, bg_out, ps_out, gm_out, bt_out, loss_out):
        g = g_ref[0]
        for k in range(1, N_DEV):
            g = g + g_ref[k]
        bg_out[...] = jnp.concatenate([g[0:1], g[1:2]], axis=1)
        gm_out[...] = g[2:3]
        bt_out[...] = g[3:4]
        ps_out[...] = g[4:5, :AW]
        loss_out[...] = jnp.broadcast_to(g[5:6, :128], loss_out.shape)

    return pl.pallas_call(
        body, name="sum_replicated",
        out_shape=[jax.ShapeDtypeStruct(shape, F32) for shape in ((1, 2 * D), (1, AW), (1, D), (1, D), (8, 128))],
        compiler_params=pltpu.CompilerParams(vmem_limit_bytes=VMEM_LIMIT))(gathered)


def _pack_small(w_out, w_pa, w_pp, w_pool):
    return jnp.concatenate([w_out, w_pa.reshape(-1, D), w_pp.reshape(-1, D), w_pool.reshape(-1, D)], axis=0)


def _unpack_small(a):
    o = R_OUT
    return (a[:R_PA - o], a[R_PA - o:R_PP - o].reshape(AW, 256), a[R_PP - o:R_PL - o].reshape(AW, 256),
            a[R_PL - o:].reshape(4, 32, PG))


def _pack_vec(b_gate, gamma, beta, pscale, extra):
    z = jnp.zeros((D,), F32)
    return jnp.stack([b_gate[:D], b_gate[D:], gamma, beta, jnp.concatenate([pscale, z[:D - AW]]),
                      jnp.broadcast_to(extra, (D,)), z, z])


def kernel(x, w_in, b_gate, w_pool, pool_scale, w_proj_attn, w_proj_pool, w_out, ln_gamma, ln_beta, loss_target, m_w_in, m_b_gate, m_w_pool, m_pool_scale, m_w_proj_attn, m_w_proj_pool, m_w_out, m_ln_gamma, m_ln_beta, v_w_in, v_b_gate, v_w_pool, v_pool_scale, v_w_proj_attn, v_w_proj_pool, v_w_out, v_ln_gamma, v_ln_beta):
    coords = jnp.stack([lax.axis_index("x"), lax.axis_index("y"), lax.axis_index("c")]).astype(jnp.int32)
    x2, tgt = _permute_tokens(x[0]), _permute_tokens(loss_target[0])

    spack = _pack_small(w_out[0], w_proj_attn[0], w_proj_pool[0], w_pool[0]).astype(BF16)
    h, gw, xt = _ag_proj(_arrival_order(*coords), x2, w_in[0].astype(BF16), spack)
    wout = gw[:, R_OUT:R_PA].reshape(D, D)
    wpa = gw[:, R_PA:R_PP].reshape(N_DEV, AW, 256).transpose(1, 0, 2).reshape(AW, D)
    wpp = gw[:, R_PP:R_PL].reshape(N_DEV, AW, 256).transpose(1, 0, 2).reshape(AW, D)
    wpool = gw[:, R_PL:].reshape(N_DEV, 4, 32, PG).transpose(1, 0, 2, 3).reshape(4, PG, PG)

    o, lse = zip(*[_attn_fwd(g, h) for g in range(len(DILATIONS))])
    merged, a, b, p, yat, ypt, mgt = _mix_fwd(h, o, lse, wpa, wpp, wpool, pool_scale, b_gate)
    dr, drb, dm, loss_part, dgamma, dbeta = _out_ln(merged, x2, tgt, wout, ln_gamma, ln_beta)

    dh, da, db, dbgate = _gate_bwd(dm, a, b, h, b_gate)
    dh, do0, do1, do2, dl0, dl1, dl2, dwpool, dpscale = _mix_bwd(
        dh, da, db, h, o, lse, p, wpa, wpp, wpool, pool_scale)
    for g, (do_g, dl_g) in enumerate(zip((do0, do1, do2), (dl0, dl1, dl2))):
        dh = _attn_bwd(g, dh, h, do_g, lse[g], dl_g)

    q = _grad_w_in_rs(_rs_columns(*coords), xt, dh)
    d_wout = _grad_w("grad_w_out", mgt, drb)
    d_wpa = _grad_w("grad_w_pa", yat, da)
    d_wpp = _grad_w("grad_w_pp", ypt, db)
    small = jnp.concatenate([
        d_wout.reshape(N_DEV, 256, D),
        d_wpa.reshape(AW, N_DEV, 256).transpose(1, 0, 2).reshape(N_DEV, -1, D),
        d_wpp.reshape(AW, N_DEV, 256).transpose(1, 0, 2).reshape(N_DEV, -1, D),
        dwpool.astype(BF16).reshape(4, N_DEV, 32, PG).transpose(1, 0, 2, 3).reshape(N_DEV, -1, D)], axis=1)
    q = _pair_sum_small(coords, small, _rs_sibling(small), q)

    vec = _pack_vec(dbgate[0], dgamma[0], dbeta[0], dpscale[0], loss_part[0, 0])
    grad_x, l2, vecs_all = _grad_x_rs(dh, gw, dr, q, vec)
    g_in, d_in, m_in, v_in = _adam_shard("adam_w_in", q, l2, w_in[0], m_w_in[0], v_w_in[0], 256)
    g_small = [t.reshape(w.shape) for t, w in zip(_unpack_small(_sum_small(q, l2)),
                                                  (w_out, w_proj_attn, w_proj_pool, w_pool))]
    small = (g_small,) + _adam_whole("adam_small", g_small, (w_out, w_proj_attn, w_proj_pool, w_pool),
                                     (m_w_out, m_w_proj_attn, m_w_proj_pool, m_w_pool),
                                     (v_w_out, v_w_proj_attn, v_w_proj_pool, v_w_pool))

    *g_vec, loss = _sum_replicated(vecs_all)
    vecs = (g_vec,) + _adam_whole("adam_replicated", g_vec, (b_gate, pool_scale, ln_gamma, ln_beta),
                                  (m_b_gate, m_pool_scale, m_ln_gamma, m_ln_beta),
                                  (v_b_gate, v_pool_scale, v_ln_gamma, v_ln_beta))
    loss = loss[0, 0]

    def leaves(kind, big):
        out, pa, pp, pool = small[kind]
        bg, ps, gm, bt = vecs[kind]
        return [big[None], bg, pool, ps, pa, pp, out, gm, bt]

    return (loss, _permute_tokens(grad_x)[None], *leaves(0, g_in), *leaves(1, d_in), *leaves(2, m_in), *leaves(3, v_in))
```

```python
import functools

import jax
import jax.numpy as jnp
from jax import lax
from jax.experimental import pallas as pl
from jax.experimental.pallas import tpu as pltpu

F32 = jnp.float32
BF16 = jnp.bfloat16

S = 4096
D = 2048
NW = 16384
AW = 1024
HD = 128
NH = 8
QB = 128
DILATIONS = (1, 4, 16)
POOL_WINDOWS = (2, 4, 8, 16)
PG = 256
N_DEV = 8
COL_Q, COL_K, COL_V = 0, 3 * AW, 6 * AW
COL_ZA, COL_U, COL_ZP, COL_G = 9 * AW, 10 * AW, 11 * AW, 12 * AW
DH_Z, DH_G = COL_ZA, COL_G
ALPHA = 2.0 ** 0.25
LN_EPS = 1e-5
NEG_INF = -1e30
LR, B1, B2, EPS, WD, STEP = 0.001, 0.9, 0.999, 1e-08, 0.01, 10
R_IN, R_OUT, R_PA, R_PP, R_PL = 0, 2048, 2304, 2432, 2560
R_ALL = 2576
R_SMALL = R_ALL - R_OUT
VMEM_LIMIT = 56 * 1024 * 1024
MESH = pl.DeviceIdType.MESH
ANY = pl.BlockSpec(memory_space=pl.ANY)


def _cparams(n_axes):
    return pltpu.CompilerParams(dimension_semantics=("arbitrary",) * n_axes, vmem_limit_bytes=VMEM_LIMIT)


def _sigmoid(z):
    return 0.5 * jnp.tanh(0.5 * z) + 0.5


def _nt(a, b):
    return lax.dot_general(a, b, (((1,), (1,)), ((), ())), preferred_element_type=F32)


def _tn(a, b):
    return lax.dot_general(a, b, (((0,), (0,)), ((), ())), preferred_element_type=F32)


def _nn(a, b):
    return jnp.dot(a, b, preferred_element_type=F32)


def _lin(x, y, c):
    return 4 * x + 2 * y + c


def _flip(v, f):
    return 1 - v if f else v


CHIP_FLIPS = ((0, 0), (1, 0), (0, 1), (1, 1))


AG_PIECES = ((pl.ds(R_IN, D), pl.ds(0, 1024)), (pl.ds(R_IN, D), pl.ds(1024, 1024)),
             (pl.ds(R_OUT, R_PA - R_OUT), pl.ds(0, D)), (pl.ds(R_PA, R_ALL - R_PA), pl.ds(0, D)))
N_PIECES = len(AG_PIECES)
SIB, TO_X, TO_Y, ON, PASS_X, PASS_Y, PASS_D = range(7)
AG_TILES = ((0, 0), (0, 1), (1, 0), (1, 1), (2, 0), (4, 0), (3, 0), (5, 0),
            (2, 1), (4, 1), (3, 1), (5, 1), (6, 0), (6, 1), (7, 0), (7, 1))
XT_TILE = 7
W, G = "wait", "go"
AG_STEPS = {
    2: [(W, SIB, 0)], 3: [(W, SIB, 1)],
    4: [(W, TO_X, 0), (G, ON, 0), (G, PASS_X, 0)], 5: [(W, TO_Y, 0), (G, PASS_Y, 0)],
    6: [(W, PASS_X, 0)], 7: [(W, PASS_Y, 0)],
    8: [(W, TO_X, 1), (G, PASS_X, 1), (W, TO_Y, 1), (G, ON, 1), (G, PASS_Y, 1),
        (G, TO_X, 2), (G, TO_X, 3), (G, TO_Y, 2), (G, TO_Y, 3)],
    10: [(W, PASS_X, 1)], 11: [(W, PASS_Y, 1)],
    12: [(W, ON, 0), (G, PASS_D, 0)], 13: [(W, ON, 1), (G, PASS_D, 1)],
    14: [(W, PASS_D, 0), (W, TO_X, 2), (G, ON, 2), (G, PASS_X, 2), (W, TO_X, 3), (G, PASS_X, 3),
         (W, TO_Y, 2), (G, PASS_Y, 2), (W, TO_Y, 3), (G, ON, 3), (G, PASS_Y, 3)],
    15: [(W, PASS_D, 1)],
}
AG_LAST = [(W, SIB, 2), (W, SIB, 3), (W, ON, 2), (G, PASS_D, 2), (W, ON, 3), (G, PASS_D, 3),
           (W, PASS_X, 2), (W, PASS_X, 3), (W, PASS_Y, 2), (W, PASS_Y, 3), (W, PASS_D, 2), (W, PASS_D, 3)]


def _arrival_order(x, y, c):
    chips = [(x, y), (1 - x, y), (x, 1 - y), (1 - x, 1 - y)]
    return jnp.stack([_lin(px, py, pc) for px, py in chips for pc in (c, 1 - c)]).astype(jnp.int32)


def _ag_proj(order, xb, wbf, spack):
    tm, tn = 1024, 1024
    nrow, ntile = S // tm, len(AG_TILES)
    slabs = jnp.stack([order[pos] for pos, _ in AG_TILES])
    cols = jnp.stack([2 * order[pos] + half for pos, half in AG_TILES])

    def body(cols_ref, slabs_ref, x_ref, w_ref, s_ref, h_ref, gw_ref, xt_ref, wbuf, wsem, send_sems, recv_sems,
             local_sems):
        t, i = pl.program_id(0), pl.program_id(1)
        x, y, c = lax.axis_index("x"), lax.axis_index("y"), lax.axis_index("c")
        me = _lin(x, y, c)
        dev = {"sib": (x, y, 1 - c), "x": (1 - x, y, c), "y": (x, 1 - y, c), "d": (1 - x, 1 - y, c)}

        def slab_of(name, other_core=False):
            px, py, pc = dev[name]
            return _lin(px, py, 1 - pc if other_core else pc)

        def own(piece):
            rows, colz = AG_PIECES[piece]
            return w_ref.at[:, colz] if piece < 2 else s_ref.at[pl.ds(rows.start - R_OUT, rows.size)]

        def rdma(slab, kind, piece, to, from_own=False):
            k = kind * N_PIECES + piece
            there = gw_ref.at[(slab, *AG_PIECES[piece])]
            return pltpu.make_async_remote_copy(
                src_ref=own(piece) if from_own else there, dst_ref=there,
                send_sem=send_sems.at[k], recv_sem=recv_sems.at[k], device_id=dev[to], device_id_type=MESH)

        def mine(kind, piece):
            if kind in (SIB, TO_X, TO_Y):
                return rdma(me, kind, piece, ("sib", "x", "y")[kind], from_own=True)
            if kind == ON:
                frm, to = ("x", "y") if piece % 2 == 0 else ("y", "x")
                return rdma(slab_of(frm), kind, piece, to)
            return rdma(slab_of({PASS_X: "x", PASS_Y: "y", PASS_D: "d"}[kind]), kind, piece, "sib")

        def landing(kind, piece):
            slab = {SIB: slab_of("sib"), TO_X: slab_of("x"), TO_Y: slab_of("y"), ON: slab_of("d"),
                    PASS_X: slab_of("x", True), PASS_Y: slab_of("y", True), PASS_D: slab_of("d", True)}[kind]
            return rdma(slab, kind, piece, "sib")

        def run(steps):
            for what, kind, piece in steps:
                if what == W:
                    landing(kind, piece).wait_recv()
                else:
                    mine(kind, piece).start()

        local = [pltpu.make_async_copy(w_ref, gw_ref.at[me, pl.ds(R_IN, D)], local_sems.at[0]),
                 pltpu.make_async_copy(s_ref, gw_ref.at[me, pl.ds(R_OUT, R_SMALL)], local_sems.at[1])]

        def fetch(slab, half, slot):
            src = own(half) if slab is None else gw_ref.at[(slab, *AG_PIECES[half])]
            return pltpu.make_async_copy(src, wbuf.at[slot], wsem.at[slot])

        @pl.when((t == 0) & (i == 0))
        def _():
            for cp in local:
                cp.start()
            run([(G, kind, piece) for piece in (0, 1) for kind in (TO_X, TO_Y, SIB)] + [(G, SIB, 2), (G, SIB, 3)])
            first = fetch(None, 0, 0)
            first.start()
            first.wait()

        for nxt in range(1, ntile):
            @pl.when((t == nxt - 1) & (i == nrow - 1))
            def _(nxt=nxt):
                run(AG_STEPS.get(nxt, []))
                fetch(None if AG_TILES[nxt][0] == 0 else slabs_ref[nxt], AG_TILES[nxt][1], nxt % 2).start()

        for slot in (0, 1):
            @pl.when(t % 2 == slot)
            def _(slot=slot):
                @pl.when((i == 0) & (t > 0))
                def _():
                    fetch(None, 0, slot).wait()
                h_ref[...] = _nn(x_ref[...], wbuf[slot]).astype(h_ref.dtype)

        @pl.when(t == XT_TILE)
        def _():
            xt_ref[...] = x_ref[...].T

        @pl.when((t == ntile - 1) & (i == nrow - 1))
        def _():
            run(AG_LAST)
            for kind in range(7):
                for piece in range(N_PIECES):
                    mine(kind, piece).wait_send()
            for cp in local:
                cp.wait()

    n_sem = 7 * N_PIECES
    grid_spec = pltpu.PrefetchScalarGridSpec(
        num_scalar_prefetch=2, grid=(ntile, nrow),
        in_specs=[pl.BlockSpec((tm, D), lambda t, i, cols, slabs: (i, 0)), ANY, ANY],
        out_specs=[pl.BlockSpec((tm, tn), lambda t, i, cols, slabs: (i, cols[t])), ANY,
                   pl.BlockSpec((D, tm), lambda t, i, cols, slabs: (
                       0, jnp.where(t < XT_TILE, 0, jnp.where(t == XT_TILE, i, nrow - 1))))],
        scratch_shapes=[pltpu.VMEM((2, D, tn), BF16), pltpu.SemaphoreType.DMA((2,)),
                        pltpu.SemaphoreType.DMA((n_sem,)), pltpu.SemaphoreType.DMA((n_sem,)),
                        pltpu.SemaphoreType.DMA((2,))])
    return pl.pallas_call(
        body, name="ag_proj", grid_spec=grid_spec,
        out_shape=[jax.ShapeDtypeStruct((S, NW), BF16), jax.ShapeDtypeStruct((N_DEV, R_ALL, D), BF16),
                   jax.ShapeDtypeStruct((D, S), BF16)],
        compiler_params=_cparams(2))(cols, slabs, xb, wbf, spack)


def _rs_sibling(p):
    n = len(CHIP_FLIPS)

    def body(p_ref, l_ref, send_sems, recv_sems):
        x, y, c = lax.axis_index("x"), lax.axis_index("y"), lax.axis_index("c")
        copies = [pltpu.make_async_remote_copy(
            src_ref=p_ref.at[_lin(_flip(x, fx), _flip(y, fy), 1 - c)], dst_ref=l_ref.at[k],
            send_sem=send_sems.at[k], recv_sem=recv_sems.at[k], device_id=(x, y, 1 - c), device_id_type=MESH)
            for k, (fx, fy) in enumerate(CHIP_FLIPS)]
        for cp in copies:
            cp.start()
        for cp in copies:
            cp.wait_recv()
        for cp in copies:
            cp.wait_send()

    return pl.pallas_call(
        body, name="rs_sibling", out_shape=jax.ShapeDtypeStruct((n,) + p.shape[1:], p.dtype),
        in_specs=[ANY], out_specs=ANY,
        scratch_shapes=[pltpu.SemaphoreType.DMA((n,)), pltpu.SemaphoreType.DMA((n,))])(p)


def _pair_sum_small(coords, p, l1, q):
    def body(crd, p_ref, l_ref, _, q_ref, buf, sem):
        k = pl.program_id(0)
        buf[...] = (p_ref[...].astype(F32) + l_ref[...].astype(F32)).astype(buf.dtype)
        out = pltpu.make_async_copy(buf, q_ref.at[k, pl.ds(R_OUT, R_SMALL)], sem)
        out.start()
        out.wait()

    def p_map(k, crd):
        fx, fy = k % 2, k // 2
        px = crd[0] + fx - 2 * fx * crd[0]
        py = crd[1] + fy - 2 * fy * crd[1]
        return (_lin(px, py, crd[2]), 0, 0)

    grid_spec = pltpu.PrefetchScalarGridSpec(
        num_scalar_prefetch=1, grid=(4,),
        in_specs=[pl.BlockSpec((None, R_SMALL, D), p_map),
                  pl.BlockSpec((None, R_SMALL, D), lambda k, crd: (k, 0, 0)), ANY],
        out_specs=ANY,
        scratch_shapes=[pltpu.VMEM((R_SMALL, D), BF16), pltpu.SemaphoreType.DMA(())])
    return pl.pallas_call(body, name="pair_sum_small", grid_spec=grid_spec,
                          out_shape=jax.ShapeDtypeStruct(q.shape, q.dtype), input_output_aliases={3: 0},
                          compiler_params=_cparams(1))(coords, p, l1, q)


def _h_block(k):
    return jnp.where(k < 9, (k % 3) * 3 + k // 3, k)


RS_PIECES = (pl.ds(0, 1280), pl.ds(1280, R_ALL - 1280))
RS_ROWS = (1280, R_ALL - 1280)
RS_CHUNKS = ((320,) * 4, (432,) * 3)
RS_MERGE_STEP = 2


def _grad_x_rs(dh, g, dr, q, vec):
    others = [(fx, fy, fc) for fx in (0, 1) for fy in (0, 1) for fc in (0, 1) if (fx, fy, fc) != (0, 0, 0)]
    tm, tk = 1024, 1024
    ni, nk = S // tm, NW // tk
    rmax = max(RS_ROWS)
    cmax = max(max(c) for c in RS_CHUNKS)

    def body(dh_ref, w_ref, dr_ref, q_ref, vec_ref, o_ref, l2_ref, ld_ref, mg_ref, all_ref, va, vb,
             send_sems, recv_sems, sems):
        i, k = pl.program_id(0), pl.program_id(1)
        x, y, c = lax.axis_index("x"), lax.axis_index("y"), lax.axis_index("c")
        nbr = ((1 - x, y, c), (x, 1 - y, c))

        def vec_copy(n, sender):
            sx, sy, sc = sender
            fx, fy, fc = others[n]
            return pltpu.make_async_remote_copy(
                src_ref=vec_ref, dst_ref=all_ref.at[_lin(sx, sy, sc)], send_sem=send_sems.at[6 + n],
                recv_sem=recv_sems.at[6 + n], device_id=(_flip(sx, fx), _flip(sy, fy), _flip(sc, fc)),
                device_id_type=MESH)

        vec_own = pltpu.make_async_copy(vec_ref, all_ref.at[_lin(x, y, c)], sems.at[3])

        def rows(ref, piece):
            return ref.at[piece, pl.ds(0, RS_ROWS[piece])]

        copies = (
            (q_ref.at[3, RS_PIECES[0]], rows(ld_ref, 0), 0),
            (q_ref.at[3, RS_PIECES[1]], rows(ld_ref, 1), 1),
            (q_ref.at[1, RS_PIECES[0]], l2_ref.at[0, RS_PIECES[0]], 0),
            (q_ref.at[2, RS_PIECES[1]], l2_ref.at[1, RS_PIECES[1]], 1),
            (rows(mg_ref, 0), l2_ref.at[1, RS_PIECES[0]], 1),
            (rows(mg_ref, 1), l2_ref.at[0, RS_PIECES[1]], 0),
        )

        def copy(n):
            src, dst, axis = copies[n]
            return pltpu.make_async_remote_copy(src_ref=src, dst_ref=dst, send_sem=send_sems.at[n],
                                                recv_sem=recv_sems.at[n], device_id=nbr[axis], device_id_type=MESH)

        def merge(piece, mine):
            start = 0
            for n_rows in RS_CHUNKS[piece]:
                own = pltpu.make_async_copy(q_ref.at[mine, pl.ds(RS_PIECES[piece].start + start, n_rows)],
                                            va.at[pl.ds(0, n_rows)], sems.at[0])
                got = pltpu.make_async_copy(ld_ref.at[piece, pl.ds(start, n_rows)], vb.at[pl.ds(0, n_rows)], sems.at[1])
                own.start()
                got.start()
                own.wait()
                got.wait()
                va[pl.ds(0, n_rows)] = (va[pl.ds(0, n_rows)].astype(F32)
                                        + vb[pl.ds(0, n_rows)].astype(F32)).astype(va.dtype)
                out = pltpu.make_async_copy(va.at[pl.ds(0, n_rows)], mg_ref.at[piece, pl.ds(start, n_rows)], sems.at[2])
                out.start()
                out.wait()
                start += n_rows

        @pl.when((i == 0) & (k == 0))
        def _():
            for n in range(4):
                copy(n).start()
            vec_own.start()
            for n in range(len(others)):
                vec_copy(n, (x, y, c)).start()

        @pl.when((i == RS_MERGE_STEP) & (k == 0))
        def _():
            copy(0).wait_recv()
            merge(0, 2)
            copy(4).start()
            copy(1).wait_recv()
            merge(1, 1)
            copy(5).start()

        @pl.when(k == 0)
        def _():
            o_ref[...] = ALPHA * dr_ref[...]

        o_ref[...] += _nt(dh_ref[...], w_ref[...])

        @pl.when((i == ni - 1) & (k == nk - 1))
        def _():
            for n in range(2, 6):
                copy(n).wait_recv()
            for n in range(6):
                copy(n).wait_send()
            for n, (fx, fy, fc) in enumerate(others):
                vec_copy(n, (_flip(x, fx), _flip(y, fy), _flip(c, fc))).wait_recv()
                vec_copy(n, (x, y, c)).wait_send()
            vec_own.wait()

    slab = q.shape[1:]
    out = pl.pallas_call(
        body, name="grad_x_rs", grid=(ni, nk),
        in_specs=[pl.BlockSpec((tm, tk), lambda i, k: (i, k)),
                  pl.BlockSpec((None, D, tk), lambda i, k: (_h_block(k) // 2, 0, _h_block(k) % 2)),
                  pl.BlockSpec((tm, D), lambda i, k: (i, 0)), ANY, ANY],
        out_specs=[pl.BlockSpec((tm, D), lambda i, k: (i, 0)), ANY, ANY, ANY, ANY],
        out_shape=[jax.ShapeDtypeStruct((S, D), F32), jax.ShapeDtypeStruct((2,) + slab, q.dtype),
                   jax.ShapeDtypeStruct((2, rmax, slab[1]), q.dtype), jax.ShapeDtypeStruct((2, rmax, slab[1]), q.dtype),
                   jax.ShapeDtypeStruct((N_DEV,) + vec.shape, vec.dtype)],
        scratch_shapes=[pltpu.VMEM((cmax, slab[1]), q.dtype), pltpu.VMEM((cmax, slab[1]), q.dtype),
                        pltpu.SemaphoreType.DMA((6 + len(others),)), pltpu.SemaphoreType.DMA((6 + len(others),)),
                        pltpu.SemaphoreType.DMA((4,))],
        compiler_params=_cparams(2))(dh, g, dr, q, vec)
    return out[0], out[1], out[4]


GW_TN = 512
GW_PARTS = D // GW_TN


def _rs_columns(x, y, c):
    per_h = AW // GW_TN
    out = []
    for core in (1 - c, c):
        for fx, fy in CHIP_FLIPS:
            for part in range(GW_PARTS):
                h_block = 2 * _lin(_flip(x, fx), _flip(y, fy), core) + part // per_h
                out.append(_h_block(h_block) * per_h + part % per_h)
    return jnp.stack(out).astype(jnp.int32)


def _grad_w_in_rs(cols, xt, dh):
    n_tile = 4 * GW_PARTS

    def body(cols_ref, a_ref, b_ref, q_ref, l1_ref, stage, landed, send_sems, recv_sems, sem):
        t = pl.program_id(0)
        sib = (lax.axis_index("x"), lax.axis_index("y"), 1 - lax.axis_index("c"))

        def there(n):
            return l1_ref.at[n // GW_PARTS, :, pl.ds((n % GW_PARTS) * GW_TN, GW_TN)]

        def send(n):
            return pltpu.make_async_remote_copy(src_ref=stage.at[n % 2], dst_ref=there(n), send_sem=send_sems.at[n],
                                                recv_sem=recv_sems.at[n], device_id=sib, device_id_type=MESH)

        def fetch(n):
            return pltpu.make_async_copy(there(n), landed, sem)

        for n in range(n_tile):
            @pl.when(t == n_tile + n)
            def _(n=n):
                if n == 0:
                    send(n_tile - 2).wait_send()
                    send(n_tile - 1).wait_send()
                send(n).wait_recv()
                fetch(n).start()

        part = _nn(a_ref[...], b_ref[...])

        for n in range(n_tile):
            @pl.when(t == n)
            def _(n=n):
                if n >= 2:
                    send(n - 2).wait_send()
                stage[n % 2] = part.astype(stage.dtype)
                send(n).start()

            @pl.when(t == n_tile + n)
            def _(n=n):
                fetch(n).wait()
                q_ref[...] = (part + landed[...].astype(F32)).astype(q_ref.dtype)

    mine = lambda t: jnp.maximum(t - n_tile, 0)
    grid_spec = pltpu.PrefetchScalarGridSpec(
        num_scalar_prefetch=1, grid=(2 * n_tile,),
        in_specs=[pl.BlockSpec((D, S), lambda t, cols: (0, 0), pipeline_mode=pl.Buffered(1)),
                  pl.BlockSpec((S, GW_TN), lambda t, cols: (0, cols[t]))],
        out_specs=[pl.BlockSpec((None, D, GW_TN), lambda t, cols: (mine(t) // GW_PARTS, 0, mine(t) % GW_PARTS)), ANY],
        scratch_shapes=[pltpu.VMEM((2, D, GW_TN), BF16), pltpu.VMEM((D, GW_TN), BF16),
                        pltpu.SemaphoreType.DMA((n_tile,)), pltpu.SemaphoreType.DMA((n_tile,)),
                        pltpu.SemaphoreType.DMA(())])
    q, _ = pl.pallas_call(
        body, name="grad_w_in_rs", grid_spec=grid_spec,
        out_shape=[jax.ShapeDtypeStruct((4, R_ALL, D), BF16), jax.ShapeDtypeStruct((4, D, D), BF16)],
        compiler_params=_cparams(1))(cols, xt, dh)
    return q


def _grad_w(name, at, b):
    m, n_all = at.shape[0], b.shape[1]

    def body(a_ref, b_ref, o_ref):
        o_ref[...] = _nn(a_ref[...], b_ref[...]).astype(o_ref.dtype)

    return pl.pallas_call(
        body, name=name, grid=(n_all // GW_TN,),
        in_specs=[pl.BlockSpec((m, S), lambda n: (0, 0), pipeline_mode=pl.Buffered(1)),
                  pl.BlockSpec((S, GW_TN), lambda n: (0, n))],
        out_specs=pl.BlockSpec((m, GW_TN), lambda n: (0, n)), out_shape=jax.ShapeDtypeStruct((m, n_all), BF16),
        compiler_params=_cparams(1))(at, b)


NR = 16
TI = 16
TM = NR * TI
NT = S // TM
ATT_QB = (256, 128, 256)
ATT_NB = (16, 8, 1)
ATT_BLOCKS = (16, 32, 16)


def _permute_tokens(a):
    return a.reshape(NT, TI, NR, a.shape[-1]).transpose(0, 2, 1, 3).reshape(a.shape)


def _attn_shape(g, c):
    if g == 0:
        return (S, c)
    if g == 1:
        return (NT, 4, 4, TI, c)
    return (NT, NR, TI, c)


def _attn_view(g, a):
    return a.reshape(_attn_shape(g, a.shape[-1]))


def _attn_spec(g, width, col, blk):
    if g == 0:
        return pl.BlockSpec((TM, width), lambda b: (blk(b), col))
    if g == 1:
        return pl.BlockSpec((2, 4, None, TI, width), lambda b: (blk(b) % 8, 0, blk(b) // 8, 0, col))
    return pl.BlockSpec((NT, None, TI, width), lambda b: (0, blk(b), 0, col))


def _pieces(g):
    if g == 1:
        return [(t, m) for t in range(2) for m in range(4)]
    return [(t,) for t in range(NT)]


def _get(g, ref, sl):
    if g == 0:
        return ref[:, sl]
    return jnp.concatenate([ref[(*p, slice(None), sl)] for p in _pieces(g)], axis=0)


def _put(g, ref, sl, val):
    if g == 0:
        ref[:, sl] = val
    else:
        for n, p in enumerate(_pieces(g)):
            ref[(*p, slice(None), sl)] = val[TI * n:TI * (n + 1)]


def _block_pos(g, a):
    if g == 0:
        return 16 * (a % 16) + a // 16
    if g == 1:
        return 64 * (a // 64) + 4 * (a % 16) + (a // 16) % 4
    return a


def _attn_mask(g, n):
    qb = ATT_QB[g]
    if ATT_NB[g] == 1:
        qa = lax.broadcasted_iota(jnp.int32, (qb, qb), 0)
        kc = lax.broadcasted_iota(jnp.int32, (qb, qb), 1)
        dist = _block_pos(g, qa) - _block_pos(g, kc)
        return (dist >= 0) & (dist <= QB)
    qa = lax.broadcasted_iota(jnp.int32, (qb, 2 * qb), 0)
    kc = lax.broadcasted_iota(jnp.int32, (qb, 2 * qb), 1)
    cur = kc >= qb
    dist = _block_pos(g, qa) - _block_pos(g, kc % qb) + jnp.where(cur, 0, qb)
    return (dist >= 0) & (dist <= QB) & (cur | (n > 0))


def _keys(g, prev_ref, cur_ref, sl):
    if ATT_NB[g] == 1:
        return _get(g, cur_ref, sl)
    return jnp.concatenate([_get(g, prev_ref, sl), _get(g, cur_ref, sl)], axis=0)


def _qkv_specs(g, clamp):
    cur = lambda col: _attn_spec(g, AW, col, clamp)
    prev = lambda col: _attn_spec(g, AW, col, lambda b: jnp.maximum(clamp(b) - 1, 0))
    qc, kc, vc = (c // AW + g for c in (COL_Q, COL_K, COL_V))
    return [cur(qc), cur(kc), prev(kc), cur(vc), prev(vc)]


def _attn_fwd(g, h):
    scale = HD ** -0.5
    hv = _attn_view(g, h)

    def body(q_ref, kc_ref, kp_ref, vc_ref, vp_ref, o_ref, l_ref):
        valid = _attn_mask(g, pl.program_id(0) % ATT_NB[g])
        for hh in range(NH):
            sl = slice(hh * HD, (hh + 1) * HD)
            kh, vh = _keys(g, kp_ref, kc_ref, sl), _keys(g, vp_ref, vc_ref, sl)
            s = jnp.where(valid, _nt(_get(g, q_ref, sl), kh) * scale, NEG_INF)
            m = jnp.max(s, axis=-1, keepdims=True)
            e = jnp.exp(s - m)
            den = jnp.sum(e, axis=-1, keepdims=True)
            _put(g, o_ref, sl, (_nn(e.astype(BF16), vh) * (1.0 / den)).astype(o_ref.dtype))
            _put(g, l_ref, slice(hh, hh + 1), m + jnp.log(den))

    same = lambda b: b
    o, lse = pl.pallas_call(
        body, name=f"attn_fwd_{g}", grid=(ATT_BLOCKS[g],),
        in_specs=_qkv_specs(g, same),
        out_specs=[_attn_spec(g, AW, 0, same), _attn_spec(g, NH, 0, same)],
        out_shape=[jax.ShapeDtypeStruct(_attn_shape(g, AW), BF16), jax.ShapeDtypeStruct(_attn_shape(g, NH), F32)],
        compiler_params=_cparams(1))(hv, hv, hv, hv, hv)
    return o.reshape(S, AW), lse.reshape(S, NH)


def _attn_bwd(g, dh, h, do, lse, delta):
    scale = HD ** -0.5
    qb = ATT_QB[g]
    carried = ATT_NB[g] > 1
    last = ATT_BLOCKS[g] - 1
    clamp = lambda b: jnp.minimum(b, last)
    behind = lambda b: jnp.maximum(b - 1, 0)
    hv = _attn_view(g, h)

    def body(q_ref, kc_ref, kp_ref, vc_ref, vp_ref, do_ref, l_ref, dl_ref, _, dh_ref, *carry):
        b = pl.program_id(0)

        def write(col, val):
            _put(g, dh_ref, slice(col, col + HD), val.astype(dh_ref.dtype))

        def block():
            valid = _attn_mask(g, b % ATT_NB[g])
            for hh in range(NH):
                sl = slice(hh * HD, (hh + 1) * HD)
                one = slice(hh, hh + 1)
                qh, doh = _get(g, q_ref, sl), _get(g, do_ref, sl)
                kh, vh = _keys(g, kp_ref, kc_ref, sl), _keys(g, vp_ref, vc_ref, sl)
                s = _nt(qh, kh) * scale
                p = jnp.where(valid, jnp.exp(s - _get(g, l_ref, one)), 0.0)
                ds = p * (_nt(doh, vh) - _get(g, dl_ref, one))
                dsb = (ds * scale).astype(BF16)
                dq = _nn(dsb, kh)
                dk2 = _tn(dsb, qh)
                dv2 = _tn(p.astype(BF16), doh)
                if carried:
                    cq_ref, ck_ref, cv_ref = carry
                    write(hh * HD, cq_ref[:, sl])
                    write(AW + hh * HD, ck_ref[:, sl] + dk2[:qb])
                    write(2 * AW + hh * HD, cv_ref[:, sl] + dv2[:qb])
                    cq_ref[:, sl] = dq
                    ck_ref[:, sl] = dk2[qb:]
                    cv_ref[:, sl] = dv2[qb:]
                else:
                    write(hh * HD, dq)
                    write(AW + hh * HD, dk2)
                    write(2 * AW + hh * HD, dv2)

        if not carried:
            block()
            return

        @pl.when(b == 0)
        def _():
            for ref in carry:
                ref[...] = jnp.zeros_like(ref)

        pl.when(b <= last)(block)

        @pl.when(b > last)
        def _():
            for hh in range(NH):
                for n, ref in enumerate(carry):
                    write(n * AW + hh * HD, ref[:, hh * HD:(hh + 1) * HD])

    out = pl.pallas_call(
        body, name=f"attn_bwd_{g}", grid=(ATT_BLOCKS[g] + carried,),
        in_specs=_qkv_specs(g, clamp) + [_attn_spec(g, AW, 0, clamp), _attn_spec(g, NH, 0, clamp),
                                         _attn_spec(g, NH, 0, clamp), ANY],
        out_specs=_attn_spec(g, 3 * AW, g, behind if carried else clamp),
        out_shape=jax.ShapeDtypeStruct(_attn_shape(g, NW), BF16),
        input_output_aliases={8: 0},
        scratch_shapes=[pltpu.VMEM((qb, AW), F32)] * (3 if carried else 0),
        compiler_params=_cparams(1))(hv, hv, hv, hv, hv, _attn_view(g, do), _attn_view(g, lse), _attn_view(g, delta),
                                     _attn_view(g, dh))
    return out.reshape(S, NW)


def _group_weights(l0, l1, l2):
    m = jnp.maximum(jnp.maximum(l0, l1), l2)
    e0, e1, e2 = jnp.exp(l0 - m), jnp.exp(l1 - m), jnp.exp(l2 - m)
    inv = 1.0 / (e0 + e1 + e2)
    return e0 * inv, e1 * inv, e2 * inv


def _residue(ref, r, sl):
    return ref[r * TI:(r + 1) * TI, sl].astype(F32)


def _total(parts):
    return functools.reduce(lambda x, y: x + y, parts)


def _pool_tokens(up_ref, uc_ref, p_ref, tile):
    j0 = lax.broadcasted_iota(jnp.int32, (TI, 1), 0) == 0
    first = (tile == 0) & j0
    for r in range(NR):
        out = []
        for g, w in enumerate(POOL_WINDOWS):
            sl = slice(g * PG, (g + 1) * PG)
            own = _residue(uc_ref, r, sl)
            acc = _total([own] + [_residue(uc_ref, r - k, sl) for k in range(1, min(r, w - 1) + 1)])
            wrapped = [NR + r - k for k in range(r + 1, w)]
            if wrapped:
                wc = _total([_residue(uc_ref, q, sl) for q in wrapped])
                wp = jnp.where(tile > 0, _total([_residue(up_ref, q, sl) for q in wrapped]), 0.0)
                acc = acc + jnp.where(j0, pltpu.roll(wp, 1, 0), pltpu.roll(wc, 1, 0))
            out.append(acc * jnp.where(first, 1.0 / min(r + 1, w), 1.0 / w) - own)
        p_ref[r * TI:(r + 1) * TI, :] = jnp.concatenate(out, axis=1).astype(p_ref.dtype)


def _pool_tokens_bwd(dp, nxt_ref, du_ref, tile):
    ji = lax.broadcasted_iota(jnp.int32, (TI, 1), 0)
    first = (tile == 0) & (ji == 0)
    piece = lambda g, r: dp[g][r * TI:(r + 1) * TI]
    dpc = [[piece(g, r) * jnp.where(first, 1.0 / min(r + 1, w), 1.0 / w) for r in range(NR)]
           for g, w in enumerate(POOL_WINDOWS)]
    for r in range(NR):
        out = []
        for g, w in enumerate(POOL_WINDOWS):
            sl = slice(g * PG, (g + 1) * PG)
            acc = _total([dpc[g][r + k] for k in range(w) if r + k < NR])
            wrapped = [r + k - NR for k in range(1, w) if r + k >= NR]
            if wrapped:
                wc = _total([dpc[g][q] for q in wrapped])
                wn = _total([nxt_ref[q * TI:(q + 1) * TI, sl] for q in wrapped])
                acc = acc + jnp.where(ji == TI - 1, pltpu.roll(wn, TI - 1, 0), pltpu.roll(wc, TI - 1, 0))
            out.append(acc - piece(g, r))
        du_ref[r * TI:(r + 1) * TI, :] = jnp.concatenate(out, axis=1).astype(du_ref.dtype)
    for r in range(NR):
        nxt_ref[r * TI:(r + 1) * TI, :] = jnp.concatenate([dpc[g][r] for g in range(len(POOL_WINDOWS))], axis=1)


def _pool_linear(pb, wpool_ref):
    return jnp.concatenate([_nn(pb[:, g * PG:(g + 1) * PG], wpool_ref[g]) for g in range(len(POOL_WINDOWS))], axis=1)


def _tok(width, col=0, rev=False):
    if rev:
        return pl.BlockSpec((TM, width), lambda i: (NT - 1 - i, col))
    return pl.BlockSpec((TM, width), lambda i: (i, col))


def _whole(shape):
    return pl.BlockSpec(shape, lambda i: (0,) * len(shape))


def _mix_fwd(h, o, lse, wpa, wpp, wpool, pscale, bgate):
    def body(o0_ref, o1_ref, o2_ref, l0_ref, l1_ref, l2_ref, za_ref, uc_ref, up_ref, zp_ref, gp_ref,
             wpa_ref, wpp_ref, wpool_ref, ps_ref, bg_ref,
             mg_ref, a_ref, b_ref, p_ref, yat_ref, ypt_ref, mgt_ref, ya_ref, yp_ref):
        i = pl.program_id(0)
        w0, w1, w2 = _group_weights(l0_ref[...], l1_ref[...], l2_ref[...])
        za = za_ref[...].astype(F32)
        silu_a = za * _sigmoid(za)
        for hh in range(NH):
            sl = slice(hh * HD, (hh + 1) * HD)
            c = slice(hh, hh + 1)
            oh = (w0[:, c] * o0_ref[:, sl].astype(F32) + w1[:, c] * o1_ref[:, sl].astype(F32)
                  + w2[:, c] * o2_ref[:, sl].astype(F32))
            ya = oh * silu_a[:, sl]
            ya_ref[:, sl] = ya.astype(BF16)
            yat_ref[sl, :] = ya.T.astype(BF16)
        _pool_tokens(up_ref, uc_ref, p_ref, i)
        zp = zp_ref[...].astype(F32)
        yp = _pool_linear(p_ref[...], wpool_ref) * ps_ref[...] * (zp * _sigmoid(zp))
        yp_ref[...] = yp.astype(BF16)
        ypt_ref[...] = yp.T.astype(BF16)
        a = _nn(ya_ref[...], wpa_ref[...])
        b = _nn(yp_ref[...], wpp_ref[...])
        a_ref[...] = a.astype(BF16)
        b_ref[...] = b.astype(BF16)
        gates = _sigmoid(gp_ref[...].astype(F32) + bg_ref[...])
        mg = gates[:, :D] * a + gates[:, D:] * b
        mg_ref[...] = mg.astype(BF16)
        mgt_ref[...] = mg.T.astype(BF16)

    u_prev = pl.BlockSpec((TM, AW), lambda i: (jnp.maximum(i - 1, 0), COL_U // AW))
    across = lambda width: pl.BlockSpec((width, TM), lambda i: (0, i))
    return pl.pallas_call(
        body, name="mix_fwd", grid=(NT,),
        in_specs=[_tok(AW)] * 3 + [_tok(NH)] * 3
        + [_tok(AW, COL_ZA // AW), _tok(AW, COL_U // AW), u_prev, _tok(AW, COL_ZP // AW), _tok(2 * D, COL_G // (2 * D))]
        + [_whole((AW, D)), _whole((AW, D)), _whole((4, PG, PG)), _whole((1, AW)), _whole((1, 2 * D))],
        out_specs=[_tok(D), _tok(D), _tok(D), _tok(AW), across(AW), across(AW), across(D)],
        out_shape=[jax.ShapeDtypeStruct((S, D), BF16)] * 3 + [jax.ShapeDtypeStruct((S, AW), BF16)]
        + [jax.ShapeDtypeStruct((AW, S), BF16)] * 2 + [jax.ShapeDtypeStruct((D, S), BF16)],
        scratch_shapes=[pltpu.VMEM((TM, AW), BF16), pltpu.VMEM((TM, AW), BF16)],
        compiler_params=_cparams(1))(*o, *lse, h, h, h, h, h, wpa, wpp, wpool, pscale, bgate)


def _out_ln(merged, x, target, wout, gamma, beta):
    def body(mg_ref, x_ref, t_ref, w_ref, g_ref, b_ref, dr_ref, drb_ref, dm_ref, loss_ref, dg_ref, db_ref):
        i = pl.program_id(0)

        @pl.when(i == 0)
        def _():
            loss_ref[...] = jnp.zeros_like(loss_ref)
            dg_ref[...] = jnp.zeros_like(dg_ref)
            db_ref[...] = jnp.zeros_like(db_ref)

        r = ALPHA * x_ref[...] + _nn(mg_ref[...], w_ref[...])
        mu = jnp.mean(r, axis=-1, keepdims=True)
        rc = r - mu
        rstd = lax.rsqrt(jnp.mean(rc * rc, axis=-1, keepdims=True) + LN_EPS)
        xhat = rc * rstd
        err = xhat * g_ref[...] + b_ref[...] - t_ref[...]
        loss_ref[...] += 0.5 * jnp.sum(jnp.mean(err * err, axis=-1, keepdims=True), axis=0, keepdims=True)
        dy = err * (1.0 / D)
        dg_ref[...] += jnp.sum(dy * xhat, axis=0, keepdims=True)
        db_ref[...] += jnp.sum(dy, axis=0, keepdims=True)
        dxh = dy * g_ref[...]
        dr = rstd * (dxh - jnp.mean(dxh, axis=-1, keepdims=True)
                     - xhat * jnp.mean(dxh * xhat, axis=-1, keepdims=True))
        dr_ref[...] = dr
        drb_ref[...] = dr.astype(BF16)
        dm_ref[...] = _nt(drb_ref[...], w_ref[...]).astype(BF16)

    return pl.pallas_call(
        body, name="out_ln", grid=(NT,),
        in_specs=[_tok(D), _tok(D), _tok(D), _whole((D, D)), _whole((1, D)), _whole((1, D))],
        out_specs=[_tok(D), _tok(D), _tok(D), _whole((8, 128)), _whole((1, D)), _whole((1, D))],
        out_shape=[jax.ShapeDtypeStruct((S, D), F32), jax.ShapeDtypeStruct((S, D), BF16),
                   jax.ShapeDtypeStruct((S, D), BF16), jax.ShapeDtypeStruct((8, 128), F32),
                   jax.ShapeDtypeStruct((1, D), F32), jax.ShapeDtypeStruct((1, D), F32)],
        compiler_params=_cparams(1))(merged, x, target, wout, gamma, beta)


def _gate_bwd(dm, a, b, h, bgate):
    def body(dm_ref, a_ref, b_ref, gp_ref, bg_ref, dgp_ref, da_ref, db_ref, dbg_ref):
        @pl.when(pl.program_id(0) == 0)
        def _():
            dbg_ref[...] = jnp.zeros_like(dbg_ref)

        dm_ = dm_ref[...].astype(F32)
        gates = _sigmoid(gp_ref[...].astype(F32) + bg_ref[...])
        ga, gb = gates[:, :D], gates[:, D:]
        da_ref[...] = (dm_ * ga).astype(BF16)
        db_ref[...] = (dm_ * gb).astype(BF16)
        dgp = jnp.concatenate([dm_ * a_ref[...].astype(F32) * ga * (1.0 - ga),
                               dm_ * b_ref[...].astype(F32) * gb * (1.0 - gb)], axis=1)
        dgp_ref[...] = dgp.astype(BF16)
        dbg_ref[...] += jnp.sum(dgp, axis=0, keepdims=True)

    return pl.pallas_call(
        body, name="gate_bwd", grid=(NT,),
        in_specs=[_tok(D), _tok(D), _tok(D), _tok(2 * D, COL_G // (2 * D)), _whole((1, 2 * D))],
        out_specs=[_tok(2 * D, DH_G // (2 * D)), _tok(D), _tok(D), _whole((1, 2 * D))],
        out_shape=[jax.ShapeDtypeStruct((S, NW), BF16), jax.ShapeDtypeStruct((S, D), BF16),
                   jax.ShapeDtypeStruct((S, D), BF16), jax.ShapeDtypeStruct((1, 2 * D), F32)],
        compiler_params=_cparams(1))(dm, a, b, h, bgate)


def _mix_bwd(dh, da, db, h, o, lse, p, wpa, wpp, wpool, pscale):
    def body(_, da_ref, db_ref, o0_ref, o1_ref, o2_ref, l0_ref, l1_ref, l2_ref, za_ref, zp_ref, p_ref,
             wpa_ref, wpp_ref, wpool_ref, ps_ref,
             dh_ref, do0_ref, do1_ref, do2_ref, dl0_ref, dl1_ref, dl2_ref, dwp_ref, dps_ref,
             nxt_ref):
        i = pl.program_id(0)
        tile = NT - 1 - i
        dza_ref, du_ref, dzp_ref = (dh_ref.at[:, pl.ds(n * AW, AW)] for n in range(3))

        @pl.when(i == 0)
        def _():
            nxt_ref[...] = jnp.zeros_like(nxt_ref)
            dwp_ref[...] = jnp.zeros_like(dwp_ref)
            dps_ref[...] = jnp.zeros_like(dps_ref)

        dya = _nt(da_ref[...], wpa_ref[...])
        w0, w1, w2 = _group_weights(l0_ref[...], l1_ref[...], l2_ref[...])
        za = za_ref[...].astype(F32)
        sig = _sigmoid(za)
        silu_a = za * sig
        dsilu_a = sig * (1.0 + za * (1.0 - sig))
        for hh in range(NH):
            sl = slice(hh * HD, (hh + 1) * HD)
            c = slice(hh, hh + 1)
            oh = (w0[:, c] * o0_ref[:, sl].astype(F32) + w1[:, c] * o1_ref[:, sl].astype(F32)
                  + w2[:, c] * o2_ref[:, sl].astype(F32))
            doh = dya[:, sl] * silu_a[:, sl]
            dza_ref[:, sl] = (dya[:, sl] * oh * dsilu_a[:, sl]).astype(BF16)
            dot_ = jnp.sum(doh * oh, axis=-1, keepdims=True)
            do0_ref[:, sl] = (w0[:, c] * doh).astype(BF16)
            do1_ref[:, sl] = (w1[:, c] * doh).astype(BF16)
            do2_ref[:, sl] = (w2[:, c] * doh).astype(BF16)
            dl0_ref[:, c] = w0[:, c] * dot_
            dl1_ref[:, c] = w1[:, c] * dot_
            dl2_ref[:, c] = w2[:, c] * dot_
        dyp = _nt(db_ref[...], wpp_ref[...])
        pb = p_ref[...]
        pw = _pool_linear(pb, wpool_ref)
        zp = zp_ref[...].astype(F32)
        sigp = _sigmoid(zp)
        dypre = dyp * (zp * sigp)
        dzp_ref[...] = (dyp * (pw * ps_ref[...]) * (sigp * (1.0 + zp * (1.0 - sigp)))).astype(BF16)
        dps_ref[...] += jnp.sum(dypre * pw, axis=0, keepdims=True)
        dpw = (dypre * ps_ref[...]).astype(BF16)
        dp = []
        for g in range(len(POOL_WINDOWS)):
            sl = slice(g * PG, (g + 1) * PG)
            dwp_ref[g] += _tn(pb[:, sl], dpw[:, sl])
            dp.append(_nt(dpw[:, sl], wpool_ref[g]))
        _pool_tokens_bwd(dp, nxt_ref, du_ref, tile)

    r = functools.partial(_tok, rev=True)
    return pl.pallas_call(
        body, name="mix_bwd", grid=(NT,),
        in_specs=[ANY, r(D), r(D)] + [r(AW)] * 3 + [r(NH)] * 3 + [r(AW, COL_ZA // AW), r(AW, COL_ZP // AW), r(AW)]
        + [_whole((AW, D)), _whole((AW, D)), _whole((4, PG, PG)), _whole((1, AW))],
        out_specs=[r(3 * AW, DH_Z // (3 * AW))] + [r(AW)] * 3 + [r(NH)] * 3 + [_whole((4, PG, PG)), _whole((1, AW))],
        out_shape=[jax.ShapeDtypeStruct((S, NW), BF16)] + [jax.ShapeDtypeStruct((S, AW), BF16)] * 3
        + [jax.ShapeDtypeStruct((S, NH), F32)] * 3
        + [jax.ShapeDtypeStruct((4, PG, PG), F32), jax.ShapeDtypeStruct((1, AW), F32)],
        input_output_aliases={0: 0},
        scratch_shapes=[pltpu.VMEM((TM, AW), F32)],
        compiler_params=_cparams(1))(dh, da, db, *o, *lse, h, h, p, wpa, wpp, wpool, pscale)


def _adamw(w, g, m, v):
    m = B1 * m + (1.0 - B1) * g
    v = B2 * v + (1.0 - B2) * jnp.square(g)
    m_hat = m / (1.0 - B1 ** STEP)
    v_hat = v / (1.0 - B2 ** STEP)
    return -LR * (m_hat / (jnp.sqrt(v_hat) + EPS) + WD * w), m, v


def _adam_shard(name, q, l2, w, m, v, tr):
    rows = w.shape[0]

    def body(q_ref, l_ref, w_ref, m_ref, v_ref, g_out, d_out, m_out, v_out):
        g = q_ref[...].astype(F32)
        for k in range(2):
            g = g + l_ref[k].astype(F32)
        g_out[...] = g
        d_out[...], m_out[...], v_out[...] = _adamw(w_ref[...], g, m_ref[...], v_ref[...])

    blk = pl.BlockSpec((tr, D), lambda i: (i, 0))
    return pl.pallas_call(
        body, name=name, grid=(rows // tr,),
        in_specs=[pl.BlockSpec((None, tr, D), lambda i: (0, i, 0)), pl.BlockSpec((2, tr, D), lambda i: (0, i, 0)),
                  blk, blk, blk],
        out_specs=[blk] * 4, out_shape=[jax.ShapeDtypeStruct((rows, D), F32)] * 4,
        compiler_params=_cparams(1))(q, l2, w, m, v)


def _sum_small(q, l2):
    def body(q_ref, l_ref, g_out, buf, sems):
        rows = pl.ds(R_OUT, R_SMALL)
        copies = [pltpu.make_async_copy(src, buf.at[n], sems.at[n])
                  for n, src in enumerate((q_ref.at[0, rows], l_ref.at[0, rows], l_ref.at[1, rows]))]
        for cp in copies:
            cp.start()
        for cp in copies:
            cp.wait()
        g_out[...] = buf[0].astype(F32) + buf[1].astype(F32) + buf[2].astype(F32)

    return pl.pallas_call(
        body, name="sum_small", in_specs=[ANY, ANY], out_shape=jax.ShapeDtypeStruct((R_SMALL, D), F32),
        scratch_shapes=[pltpu.VMEM((3, R_SMALL, D), q.dtype), pltpu.SemaphoreType.DMA((3,))],
        compiler_params=pltpu.CompilerParams(vmem_limit_bytes=VMEM_LIMIT))(q, l2)


def _adam_whole(name, grads, weights, ms, vs):
    n = len(grads)

    def body(*refs):
        ins, outs = refs[:4 * n], refs[4 * n:]
        for t in range(n):
            g, w, m, v = (ins[k * n + t][...] for k in range(4))
            outs[t][...], outs[n + t][...], outs[2 * n + t][...] = _adamw(w, g, m, v)

    out = pl.pallas_call(
        body, name=name, out_shape=[jax.ShapeDtypeStruct(w.shape, F32) for w in weights] * 3,
        compiler_params=pltpu.CompilerParams(vmem_limit_bytes=VMEM_LIMIT))(*grads, *weights, *ms, *vs)
    return out[:n], out[n:2 * n], out[2 * n:]


def _sum_replicated(gathered):
    def body(g_ref, bg_out, ps_out, gm_out, bt_out, loss_out):
        g = g_ref[0]
        for k in range(1, N_DEV):
            g = g + g_ref[k]
        bg_out[...] = jnp.concatenate([g[0:1], g[1:2]], axis=1)
        gm_out[...] = g[2:3]
        bt_out[...] = g[3:4]
        ps_out[...] = g[4:5, :AW]
        loss_out[...] = jnp.broadcast_to(g[5:6, :128], loss_out.shape)

    return pl.pallas_call(
        body, name="sum_replicated",
        out_shape=[jax.ShapeDtypeStruct(shape, F32) for shape in ((1, 2 * D), (1, AW), (1, D), (1, D), (8, 128))],
        compiler_params=pltpu.CompilerParams(vmem_limit_bytes=VMEM_LIMIT))(gathered)


def _pack_small(w_out, w_pa, w_pp, w_pool):
    return jnp.concatenate([w_out, w_pa.reshape(-1, D), w_pp.reshape(-1, D), w_pool.reshape(-1, D)], axis=0)


def _unpack_small(a):
    o = R_OUT
    return (a[:R_PA - o], a[R_PA - o:R_PP - o].reshape(AW, 256), a[R_PP - o:R_PL - o].reshape(AW, 256),
            a[R_PL - o:].reshape(4, 32, PG))


def _pack_vec(b_gate, gamma, beta, pscale, extra):
    z = jnp.zeros((D,), F32)
    return jnp.stack([b_gate[:D], b_gate[D:], gamma, beta, jnp.concatenate([pscale, z[:D - AW]]),
                      jnp.broadcast_to(extra, (D,)), z, z])


def kernel(x, w_in, b_gate, w_pool, pool_scale, w_proj_attn, w_proj_pool, w_out, ln_gamma, ln_beta, loss_target, m_w_in, m_b_gate, m_w_pool, m_pool_scale, m_w_proj_attn, m_w_proj_pool, m_w_out, m_ln_gamma, m_ln_beta, v_w_in, v_b_gate, v_w_pool, v_pool_scale, v_w_proj_attn, v_w_proj_pool, v_w_out, v_ln_gamma, v_ln_beta):
    coords = jnp.stack([lax.axis_index("x"), lax.axis_index("y"), lax.axis_index("c")]).astype(jnp.int32)
    x2, tgt = _permute_tokens(x[0]), _permute_tokens(loss_target[0])

    spack = _pack_small(w_out[0], w_proj_attn[0], w_proj_pool[0], w_pool[0]).astype(BF16)
    h, gw, xt = _ag_proj(_arrival_order(*coords), x2.astype(BF16), w_in[0].astype(BF16), spack)
    wout = gw[:, R_OUT:R_PA].reshape(D, D)
    wpa = gw[:, R_PA:R_PP].reshape(N_DEV, AW, 256).transpose(1, 0, 2).reshape(AW, D)
    wpp = gw[:, R_PP:R_PL].reshape(N_DEV, AW, 256).transpose(1, 0, 2).reshape(AW, D)
    wpool = gw[:, R_PL:].reshape(N_DEV, 4, 32, PG).transpose(1, 0, 2, 3).reshape(4, PG, PG)

    o, lse = zip(*[_attn_fwd(g, h) for g in range(len(DILATIONS))])
    merged, a, b, p, yat, ypt, mgt = _mix_fwd(h, o, lse, wpa, wpp, wpool, pool_scale, b_gate)
    dr, drb, dm, loss_part, dgamma, dbeta = _out_ln(merged, x2, tgt, wout, ln_gamma, ln_beta)

    dh, da, db, dbgate = _gate_bwd(dm, a, b, h, b_gate)
    dh, do0, do1, do2, dl0, dl1, dl2, dwpool, dpscale = _mix_bwd(
        dh, da, db, h, o, lse, p, wpa, wpp, wpool, pool_scale)
    for g, (do_g, dl_g) in enumerate(zip((do0, do1, do2), (dl0, dl1, dl2))):
        dh = _attn_bwd(g, dh, h, do_g, lse[g], dl_g)

    q = _grad_w_in_rs(_rs_columns(*coords), xt, dh)
    d_wout = _grad_w("grad_w_out", mgt, drb)
    d_wpa = _grad_w("grad_w_pa", yat, da)
    d_wpp = _grad_w("grad_w_pp", ypt, db)
    small = jnp.concatenate([
        d_wout.reshape(N_DEV, 256, D),
        d_wpa.reshape(AW, N_DEV, 256).transpose(1, 0, 2).reshape(N_DEV, -1, D),
        d_wpp.reshape(AW, N_DEV, 256).transpose(1, 0, 2).reshape(N_DEV, -1, D),
        dwpool.astype(BF16).reshape(4, N_DEV, 32, PG).transpose(1, 0, 2, 3).reshape(N_DEV, -1, D)], axis=1)
    q = _pair_sum_small(coords, small, _rs_sibling(small), q)

    vec = _pack_vec(dbgate[0], dgamma[0], dbeta[0], dpscale[0], loss_part[0, 0])
    grad_x, l2, vecs_all = _grad_x_rs(dh, gw, dr, q, vec)
    g_in, d_in, m_in, v_in = _adam_shard("adam_w_in", q, l2, w_in[0], m_w_in[0], v_w_in[0], 256)
    g_small = [t.reshape(w.shape) for t, w in zip(_unpack_small(_sum_small(q, l2)),
                                                  (w_out, w_proj_attn, w_proj_pool, w_pool))]
    small = (g_small,) + _adam_whole("adam_small", g_small, (w_out, w_proj_attn, w_proj_pool, w_pool),
                                     (m_w_out, m_w_proj_attn, m_w_proj_pool, m_w_pool),
                                     (v_w_out, v_w_proj_attn, v_w_proj_pool, v_w_pool))

    *g_vec, loss = _sum_replicated(vecs_all)
    vecs = (g_vec,) + _adam_whole("adam_replicated", g_vec, (b_gate, pool_scale, ln_gamma, ln_beta),
                                  (m_b_gate, m_pool_scale, m_ln_gamma, m_ln_beta),
                                  (v_b_gate, v_pool_scale, v_ln_gamma, v_ln_beta))
    loss = loss[0, 0]

    def leaves(kind, big):
        out, pa, pp, pool = small[kind]
        bg, ps, gm, bt = vecs[kind]
        return [big[None], bg, pool, ps, pa, pp, out, gm, bt]

    return (loss, _permute_tokens(grad_x)[None], *leaves(0, g_in), *leaves(1, d_in), *leaves(2, m_in), *leaves(3, v_in))
```

```python
import functools

import jax
import jax.numpy as jnp
from jax import lax
from jax.experimental import pallas as pl
from jax.experimental.pallas import tpu as pltpu

F32 = jnp.float32
BF16 = jnp.bfloat16

S = 4096
D = 2048
NW = 16384
AW = 1024
HD = 128
NH = 8
QB = 128
DILATIONS = (1, 4, 16)
POOL_WINDOWS = (2, 4, 8, 16)
PG = 256
N_DEV = 8
COL_Q, COL_K, COL_V = 0, 3 * AW, 6 * AW
COL_ZA, COL_U, COL_ZP, COL_G = 9 * AW, 10 * AW, 11 * AW, 12 * AW
DH_Z, DH_G = COL_ZA, COL_G
ALPHA = 2.0 ** 0.25
LN_EPS = 1e-5
NEG_INF = -1e30
LR, B1, B2, EPS, WD, STEP = 0.001, 0.9, 0.999, 1e-08, 0.01, 10
R_IN, R_OUT, R_PA, R_PP, R_PL = 0, 2048, 2304, 2432, 2560
R_ALL = 2576
R_SMALL = R_ALL - R_OUT
VMEM_LIMIT = 56 * 1024 * 1024
MESH = pl.DeviceIdType.MESH
ANY = pl.BlockSpec(memory_space=pl.ANY)


def _cparams(n_axes):
    return pltpu.CompilerParams(dimension_semantics=("arbitrary",) * n_axes, vmem_limit_bytes=VMEM_LIMIT)


def _sigmoid(z):
    return 0.5 * jnp.tanh(0.5 * z) + 0.5


def _nt(a, b):
    return lax.dot_general(a, b, (((1,), (1,)), ((), ())), preferred_element_type=F32)


def _tn(a, b):
    return lax.dot_general(a, b, (((0,), (0,)), ((), ())), preferred_element_type=F32)


def _nn(a, b):
    return jnp.dot(a, b, preferred_element_type=F32)


def _lin(x, y, c):
    return 4 * x + 2 * y + c


def _flip(v, f):
    return 1 - v if f else v


CHIP_FLIPS = ((0, 0), (1, 0), (0, 1), (1, 1))


AG_PIECES = ((pl.ds(R_IN, D), pl.ds(0, 1024)), (pl.ds(R_IN, D), pl.ds(1024, 1024)),
             (pl.ds(R_OUT, R_PA - R_OUT), pl.ds(0, D)), (pl.ds(R_PA, R_ALL - R_PA), pl.ds(0, D)))
N_PIECES = len(AG_PIECES)
SIB, TO_X, TO_Y, ON, PASS_X, PASS_Y, PASS_D = range(7)
AG_TILES = ((0, 0), (0, 1), (1, 0), (1, 1), (2, 0), (4, 0), (3, 0), (5, 0),
            (2, 1), (4, 1), (3, 1), (5, 1), (6, 0), (6, 1), (7, 0), (7, 1))
XT_TILE = 7
W, G = "wait", "go"
AG_STEPS = {
    2: [(W, SIB, 0)], 3: [(W, SIB, 1)],
    4: [(W, TO_X, 0), (G, ON, 0), (G, PASS_X, 0)], 5: [(W, TO_Y, 0), (G, PASS_Y, 0)],
    6: [(W, PASS_X, 0)], 7: [(W, PASS_Y, 0)],
    8: [(W, TO_X, 1), (G, PASS_X, 1), (W, TO_Y, 1), (G, ON, 1), (G, PASS_Y, 1),
        (G, TO_X, 2), (G, TO_X, 3), (G, TO_Y, 2), (G, TO_Y, 3)],
    10: [(W, PASS_X, 1)], 11: [(W, PASS_Y, 1)],
    12: [(W, ON, 0), (G, PASS_D, 0)], 13: [(W, ON, 1), (G, PASS_D, 1)],
    14: [(W, PASS_D, 0), (W, TO_X, 2), (G, ON, 2), (G, PASS_X, 2), (W, TO_X, 3), (G, PASS_X, 3),
         (W, TO_Y, 2), (G, PASS_Y, 2), (W, TO_Y, 3), (G, ON, 3), (G, PASS_Y, 3)],
    15: [(W, PASS_D, 1)],
}
AG_LAST = [(W, SIB, 2), (W, SIB, 3), (W, ON, 2), (G, PASS_D, 2), (W, ON, 3), (G, PASS_D, 3),
           (W, PASS_X, 2), (W, PASS_X, 3), (W, PASS_Y, 2), (W, PASS_Y, 3), (W, PASS_D, 2), (W, PASS_D, 3)]


def _arrival_order(x, y, c):
    chips = [(x, y), (1 - x, y), (x, 1 - y), (1 - x, 1 - y)]
    return jnp.stack([_lin(px, py, pc) for px, py in chips for pc in (c, 1 - c)]).astype(jnp.int32)


def _ag_proj(order, xb, wbf, spack):
    tm, tn = 1024, 1024
    nrow, ntile = S // tm, len(AG_TILES)
    slabs = jnp.stack([order[pos] for pos, _ in AG_TILES])
    cols = jnp.stack([2 * order[pos] + half for pos, half in AG_TILES])

    def body(cols_ref, slabs_ref, x_ref, w_ref, s_ref, h_ref, gw_ref, xt_ref, wbuf, wsem, send_sems, recv_sems,
             local_sems):
        t, i = pl.program_id(0), pl.program_id(1)
        x, y, c = lax.axis_index("x"), lax.axis_index("y"), lax.axis_index("c")
        me = _lin(x, y, c)
        dev = {"sib": (x, y, 1 - c), "x": (1 - x, y, c), "y": (x, 1 - y, c), "d": (1 - x, 1 - y, c)}

        def slab_of(name, other_core=False):
            px, py, pc = dev[name]
            return _lin(px, py, 1 - pc if other_core else pc)

        def own(piece):
            rows, colz = AG_PIECES[piece]
            return w_ref.at[:, colz] if piece < 2 else s_ref.at[pl.ds(rows.start - R_OUT, rows.size)]

        def rdma(slab, kind, piece, to, from_own=False):
            k = kind * N_PIECES + piece
            there = gw_ref.at[(slab, *AG_PIECES[piece])]
            return pltpu.make_async_remote_copy(
                src_ref=own(piece) if from_own else there, dst_ref=there,
                send_sem=send_sems.at[k], recv_sem=recv_sems.at[k], device_id=dev[to], device_id_type=MESH)

        def mine(kind, piece):
            if kind in (SIB, TO_X, TO_Y):
                return rdma(me, kind, piece, ("sib", "x", "y")[kind], from_own=True)
            if kind == ON:
                frm, to = ("x", "y") if piece % 2 == 0 else ("y", "x")
                return rdma(slab_of(frm), kind, piece, to)
            return rdma(slab_of({PASS_X: "x", PASS_Y: "y", PASS_D: "d"}[kind]), kind, piece, "sib")

        def landing(kind, piece):
            slab = {SIB: slab_of("sib"), TO_X: slab_of("x"), TO_Y: slab_of("y"), ON: slab_of("d"),
                    PASS_X: slab_of("x", True), PASS_Y: slab_of("y", True), PASS_D: slab_of("d", True)}[kind]
            return rdma(slab, kind, piece, "sib")

        def run(steps):
            for what, kind, piece in steps:
                if what == W:
                    landing(kind, piece).wait_recv()
                else:
                    mine(kind, piece).start()

        local = [pltpu.make_async_copy(w_ref, gw_ref.at[me, pl.ds(R_IN, D)], local_sems.at[0]),
                 pltpu.make_async_copy(s_ref, gw_ref.at[me, pl.ds(R_OUT, R_SMALL)], local_sems.at[1])]

        def fetch(slab, half, slot):
            src = own(half) if slab is None else gw_ref.at[(slab, *AG_PIECES[half])]
            return pltpu.make_async_copy(src, wbuf.at[slot], wsem.at[slot])

        @pl.when((t == 0) & (i == 0))
        def _():
            for cp in local:
                cp.start()
            run([(G, kind, piece) for piece in (0, 1) for kind in (TO_X, TO_Y, SIB)] + [(G, SIB, 2), (G, SIB, 3)])
            first = fetch(None, 0, 0)
            first.start()
            first.wait()

        for nxt in range(1, ntile):
            @pl.when((t == nxt - 1) & (i == nrow - 1))
            def _(nxt=nxt):
                run(AG_STEPS.get(nxt, []))
                fetch(None if AG_TILES[nxt][0] == 0 else slabs_ref[nxt], AG_TILES[nxt][1], nxt % 2).start()

        for slot in (0, 1):
            @pl.when(t % 2 == slot)
            def _(slot=slot):
                @pl.when((i == 0) & (t > 0))
                def _():
                    fetch(None, 0, slot).wait()
                h_ref[...] = _nn(x_ref[...], wbuf[slot]).astype(h_ref.dtype)

        @pl.when(t == XT_TILE)
        def _():
            xt_ref[...] = x_ref[...].T

        @pl.when((t == ntile - 1) & (i == nrow - 1))
        def _():
            run(AG_LAST)
            for kind in range(7):
                for piece in range(N_PIECES):
                    mine(kind, piece).wait_send()
            for cp in local:
                cp.wait()

    n_sem = 7 * N_PIECES
    grid_spec = pltpu.PrefetchScalarGridSpec(
        num_scalar_prefetch=2, grid=(ntile, nrow),
        in_specs=[pl.BlockSpec((tm, D), lambda t, i, cols, slabs: (i, 0)), ANY, ANY],
        out_specs=[pl.BlockSpec((tm, tn), lambda t, i, cols, slabs: (i, cols[t])), ANY,
                   pl.BlockSpec((D, tm), lambda t, i, cols, slabs: (
                       0, jnp.where(t < XT_TILE, 0, jnp.where(t == XT_TILE, i, nrow - 1))))],
        scratch_shapes=[pltpu.VMEM((2, D, tn), BF16), pltpu.SemaphoreType.DMA((2,)),
                        pltpu.SemaphoreType.DMA((n_sem,)), pltpu.SemaphoreType.DMA((n_sem,)),
                        pltpu.SemaphoreType.DMA((2,))])
    return pl.pallas_call(
        body, name="ag_proj", grid_spec=grid_spec,
        out_shape=[jax.ShapeDtypeStruct((S, NW), BF16), jax.ShapeDtypeStruct((N_DEV, R_ALL, D), BF16),
                   jax.ShapeDtypeStruct((D, S), BF16)],
        compiler_params=_cparams(2))(cols, slabs, xb, wbf, spack)


def _rs_sibling(p):
    n = len(CHIP_FLIPS)

    def body(p_ref, l_ref, send_sems, recv_sems):
        x, y, c = lax.axis_index("x"), lax.axis_index("y"), lax.axis_index("c")
        copies = [pltpu.make_async_remote_copy(
            src_ref=p_ref.at[_lin(_flip(x, fx), _flip(y, fy), 1 - c)], dst_ref=l_ref.at[k],
            send_sem=send_sems.at[k], recv_sem=recv_sems.at[k], device_id=(x, y, 1 - c), device_id_type=MESH)
            for k, (fx, fy) in enumerate(CHIP_FLIPS)]
        for cp in copies:
            cp.start()
        for cp in copies:
            cp.wait_recv()
        for cp in copies:
            cp.wait_send()

    return pl.pallas_call(
        body, name="rs_sibling", out_shape=jax.ShapeDtypeStruct((n,) + p.shape[1:], p.dtype),
        in_specs=[ANY], out_specs=ANY,
        scratch_shapes=[pltpu.SemaphoreType.DMA((n,)), pltpu.SemaphoreType.DMA((n,))])(p)


def _pair_sum_small(coords, p, l1, q):
    def body(crd, p_ref, l_ref, _, q_ref, buf, sem):
        k = pl.program_id(0)
        buf[...] = (p_ref[...].astype(F32) + l_ref[...].astype(F32)).astype(buf.dtype)
        out = pltpu.make_async_copy(buf, q_ref.at[k, pl.ds(R_OUT, R_SMALL)], sem)
        out.start()
        out.wait()

    def p_map(k, crd):
        fx, fy = k % 2, k // 2
        px = crd[0] + fx - 2 * fx * crd[0]
        py = crd[1] + fy - 2 * fy * crd[1]
        return (_lin(px, py, crd[2]), 0, 0)

    grid_spec = pltpu.PrefetchScalarGridSpec(
        num_scalar_prefetch=1, grid=(4,),
        in_specs=[pl.BlockSpec((None, R_SMALL, D), p_map),
                  pl.BlockSpec((None, R_SMALL, D), lambda k, crd: (k, 0, 0)), ANY],
        out_specs=ANY,
        scratch_shapes=[pltpu.VMEM((R_SMALL, D), BF16), pltpu.SemaphoreType.DMA(())])
    return pl.pallas_call(body, name="pair_sum_small", grid_spec=grid_spec,
                          out_shape=jax.ShapeDtypeStruct(q.shape, q.dtype), input_output_aliases={3: 0},
                          compiler_params=_cparams(1))(coords, p, l1, q)


def _h_block(k):
    return jnp.where(k < 9, (k % 3) * 3 + k // 3, k)


RS_PIECES = (pl.ds(0, 1280), pl.ds(1280, R_ALL - 1280))
RS_ROWS = (1280, R_ALL - 1280)
RS_CHUNKS = ((320,) * 4, (432,) * 3)
RS_MERGE_STEP = 2


def _grad_x_rs(dh, g, dr, q, vec):
    others = [(fx, fy, fc) for fx in (0, 1) for fy in (0, 1) for fc in (0, 1) if (fx, fy, fc) != (0, 0, 0)]
    tm, tk = 1024, 1024
    ni, nk = S // tm, NW // tk
    rmax = max(RS_ROWS)
    cmax = max(max(c) for c in RS_CHUNKS)

    def body(dh_ref, w_ref, dr_ref, q_ref, vec_ref, o_ref, l2_ref, ld_ref, mg_ref, all_ref, va, vb,
             send_sems, recv_sems, sems):
        i, k = pl.program_id(0), pl.program_id(1)
        x, y, c = lax.axis_index("x"), lax.axis_index("y"), lax.axis_index("c")
        nbr = ((1 - x, y, c), (x, 1 - y, c))

        def vec_copy(n, sender):
            sx, sy, sc = sender
            fx, fy, fc = others[n]
            return pltpu.make_async_remote_copy(
                src_ref=vec_ref, dst_ref=all_ref.at[_lin(sx, sy, sc)], send_sem=send_sems.at[6 + n],
                recv_sem=recv_sems.at[6 + n], device_id=(_flip(sx, fx), _flip(sy, fy), _flip(sc, fc)),
                device_id_type=MESH)

        vec_own = pltpu.make_async_copy(vec_ref, all_ref.at[_lin(x, y, c)], sems.at[3])

        def rows(ref, piece):
            return ref.at[piece, pl.ds(0, RS_ROWS[piece])]

        copies = (
            (q_ref.at[3, RS_PIECES[0]], rows(ld_ref, 0), 0),
            (q_ref.at[3, RS_PIECES[1]], rows(ld_ref, 1), 1),
            (q_ref.at[1, RS_PIECES[0]], l2_ref.at[0, RS_PIECES[0]], 0),
            (q_ref.at[2, RS_PIECES[1]], l2_ref.at[1, RS_PIECES[1]], 1),
            (rows(mg_ref, 0), l2_ref.at[1, RS_PIECES[0]], 1),
            (rows(mg_ref, 1), l2_ref.at[0, RS_PIECES[1]], 0),
        )

        def copy(n):
            src, dst, axis = copies[n]
            return pltpu.make_async_remote_copy(src_ref=src, dst_ref=dst, send_sem=send_sems.at[n],
                                                recv_sem=recv_sems.at[n], device_id=nbr[axis], device_id_type=MESH)

        def merge(piece, mine):
            start = 0
            for n_rows in RS_CHUNKS[piece]:
                own = pltpu.make_async_copy(q_ref.at[mine, pl.ds(RS_PIECES[piece].start + start, n_rows)],
                                            va.at[pl.ds(0, n_rows)], sems.at[0])
                got = pltpu.make_async_copy(ld_ref.at[piece, pl.ds(start, n_rows)], vb.at[pl.ds(0, n_rows)], sems.at[1])
                own.start()
                got.start()
                own.wait()
                got.wait()
                va[pl.ds(0, n_rows)] = (va[pl.ds(0, n_rows)].astype(F32)
                                        + vb[pl.ds(0, n_rows)].astype(F32)).astype(va.dtype)
                out = pltpu.make_async_copy(va.at[pl.ds(0, n_rows)], mg_ref.at[piece, pl.ds(start, n_rows)], sems.at[2])
                out.start()
                out.wait()
                start += n_rows

        @pl.when((i == 0) & (k == 0))
        def _():
            for n in range(4):
                copy(n).start()
            vec_own.start()
            for n in range(len(others)):
                vec_copy(n, (x, y, c)).start()

        @pl.when((i == RS_MERGE_STEP) & (k == 0))
        def _():
            copy(0).wait_recv()
            merge(0, 2)
            copy(4).start()
            copy(1).wait_recv()
            merge(1, 1)
            copy(5).start()

        @pl.when(k == 0)
        def _():
            o_ref[...] = ALPHA * dr_ref[...]

        o_ref[...] += _nt(dh_ref[...], w_ref[...])

        @pl.when((i == ni - 1) & (k == nk - 1))
        def _():
            for n in range(2, 6):
                copy(n).wait_recv()
            for n in range(6):
                copy(n).wait_send()
            for n, (fx, fy, fc) in enumerate(others):
                vec_copy(n, (_flip(x, fx), _flip(y, fy), _flip(c, fc))).wait_recv()
                vec_copy(n, (x, y, c)).wait_send()
            vec_own.wait()

    slab = q.shape[1:]
    out = pl.pallas_call(
        body, name="grad_x_rs", grid=(ni, nk),
        in_specs=[pl.BlockSpec((tm, tk), lambda i, k: (i, k)),
                  pl.BlockSpec((None, D, tk), lambda i, k: (_h_block(k) // 2, 0, _h_block(k) % 2)),
                  pl.BlockSpec((tm, D), lambda i, k: (i, 0)), ANY, ANY],
        out_specs=[pl.BlockSpec((tm, D), lambda i, k: (i, 0)), ANY, ANY, ANY, ANY],
        out_shape=[jax.ShapeDtypeStruct((S, D), F32), jax.ShapeDtypeStruct((2,) + slab, q.dtype),
                   jax.ShapeDtypeStruct((2, rmax, slab[1]), q.dtype), jax.ShapeDtypeStruct((2, rmax, slab[1]), q.dtype),
                   jax.ShapeDtypeStruct((N_DEV,) + vec.shape, vec.dtype)],
        scratch_shapes=[pltpu.VMEM((cmax, slab[1]), q.dtype), pltpu.VMEM((cmax, slab[1]), q.dtype),
                        pltpu.SemaphoreType.DMA((6 + len(others),)), pltpu.SemaphoreType.DMA((6 + len(others),)),
                        pltpu.SemaphoreType.DMA((4,))],
        compiler_params=_cparams(2))(dh, g, dr, q, vec)
    return out[0], out[1], out[4]


GW_TN = 512
GW_PARTS = D // GW_TN


def _rs_columns(x, y, c):
    per_h = AW // GW_TN
    out = []
    for core in (1 - c, c):
        for fx, fy in CHIP_FLIPS:
            for part in range(GW_PARTS):
                h_block = 2 * _lin(_flip(x, fx), _flip(y, fy), core) + part // per_h
                out.append(_h_block(h_block) * per_h + part % per_h)
    return jnp.stack(out).astype(jnp.int32)


def _grad_w_in_rs(cols, xt, dh):
    n_tile = 4 * GW_PARTS

    def body(cols_ref, a_ref, b_ref, q_ref, l1_ref, stage, landed, send_sems, recv_sems, sem):
        t = pl.program_id(0)
        sib = (lax.axis_index("x"), lax.axis_index("y"), 1 - lax.axis_index("c"))

        def there(n):
            return l1_ref.at[n // GW_PARTS, :, pl.ds((n % GW_PARTS) * GW_TN, GW_TN)]

        def send(n):
            return pltpu.make_async_remote_copy(src_ref=stage.at[n % 2], dst_ref=there(n), send_sem=send_sems.at[n],
                                                recv_sem=recv_sems.at[n], device_id=sib, device_id_type=MESH)

        def fetch(n):
            return pltpu.make_async_copy(there(n), landed, sem)

        for n in range(n_tile):
            @pl.when(t == n_tile + n)
            def _(n=n):
                if n == 0:
                    send(n_tile - 2).wait_send()
                    send(n_tile - 1).wait_send()
                send(n).wait_recv()
                fetch(n).start()

        part = _nn(a_ref[...], b_ref[...])

        for n in range(n_tile):
            @pl.when(t == n)
            def _(n=n):
                if n >= 2:
                    send(n - 2).wait_send()
                stage[n % 2] = part.astype(stage.dtype)
                send(n).start()

            @pl.when(t == n_tile + n)
            def _(n=n):
                fetch(n).wait()
                q_ref[...] = (part + landed[...].astype(F32)).astype(q_ref.dtype)

    mine = lambda t: jnp.maximum(t - n_tile, 0)
    grid_spec = pltpu.PrefetchScalarGridSpec(
        num_scalar_prefetch=1, grid=(2 * n_tile,),
        in_specs=[pl.BlockSpec((D, S), lambda t, cols: (0, 0), pipeline_mode=pl.Buffered(1)),
                  pl.BlockSpec((S, GW_TN), lambda t, cols: (0, cols[t]))],
        out_specs=[pl.BlockSpec((None, D, GW_TN), lambda t, cols: (mine(t) // GW_PARTS, 0, mine(t) % GW_PARTS)), ANY],
        scratch_shapes=[pltpu.VMEM((2, D, GW_TN), BF16), pltpu.VMEM((D, GW_TN), BF16),
                        pltpu.SemaphoreType.DMA((n_tile,)), pltpu.SemaphoreType.DMA((n_tile,)),
                        pltpu.SemaphoreType.DMA(())])
    q, _ = pl.pallas_call(
        body, name="grad_w_in_rs", grid_spec=grid_spec,
        out_shape=[jax.ShapeDtypeStruct((4, R_ALL, D), BF16), jax.ShapeDtypeStruct((4, D, D), BF16)],
        compiler_params=_cparams(1))(cols, xt, dh)
    return q


def _grad_w(name, at, b):
    m, n_all = at.shape[0], b.shape[1]

    def body(a_ref, b_ref, o_ref):
        o_ref[...] = _nn(a_ref[...], b_ref[...]).astype(o_ref.dtype)

    return pl.pallas_call(
        body, name=name, grid=(n_all // GW_TN,),
        in_specs=[pl.BlockSpec((m, S), lambda n: (0, 0), pipeline_mode=pl.Buffered(1)),
                  pl.BlockSpec((S, GW_TN), lambda n: (0, n))],
        out_specs=pl.BlockSpec((m, GW_TN), lambda n: (0, n)), out_shape=jax.ShapeDtypeStruct((m, n_all), BF16),
        compiler_params=_cparams(1))(at, b)


NR = 16
TI = 16
TM = NR * TI
NT = S // TM
ATT_QB = (256, 128, 256)
ATT_NB = (16, 8, 1)
ATT_BLOCKS = (16, 32, 16)


def _permute_tokens(a):
    return a.reshape(NT, TI, NR, a.shape[-1]).transpose(0, 2, 1, 3).reshape(a.shape)


def _attn_shape(g, c):
    if g == 0:
        return (S, c)
    if g == 1:
        return (NT, 4, 4, TI, c)
    return (NT, NR, TI, c)


def _attn_view(g, a):
    return a.reshape(_attn_shape(g, a.shape[-1]))


def _attn_spec(g, width, col, blk):
    if g == 0:
        return pl.BlockSpec((TM, width), lambda b: (blk(b), col))
    if g == 1:
        return pl.BlockSpec((2, 4, None, TI, width), lambda b: (blk(b) % 8, 0, blk(b) // 8, 0, col))
    return pl.BlockSpec((NT, None, TI, width), lambda b: (0, blk(b), 0, col))


def _pieces(g):
    if g == 1:
        return [(t, m) for t in range(2) for m in range(4)]
    return [(t,) for t in range(NT)]


def _get(g, ref, sl):
    if g == 0:
        return ref[:, sl]
    return jnp.concatenate([ref[(*p, slice(None), sl)] for p in _pieces(g)], axis=0)


def _put(g, ref, sl, val):
    if g == 0:
        ref[:, sl] = val
    else:
        for n, p in enumerate(_pieces(g)):
            ref[(*p, slice(None), sl)] = val[TI * n:TI * (n + 1)]


def _block_pos(g, a):
    if g == 0:
        return 16 * (a % 16) + a // 16
    if g == 1:
        return 64 * (a // 64) + 4 * (a % 16) + (a // 16) % 4
    return a


def _attn_mask(g, n):
    qb = ATT_QB[g]
    if ATT_NB[g] == 1:
        qa = lax.broadcasted_iota(jnp.int32, (qb, qb), 0)
        kc = lax.broadcasted_iota(jnp.int32, (qb, qb), 1)
        dist = _block_pos(g, qa) - _block_pos(g, kc)
        return (dist >= 0) & (dist <= QB)
    qa = lax.broadcasted_iota(jnp.int32, (qb, 2 * qb), 0)
    kc = lax.broadcasted_iota(jnp.int32, (qb, 2 * qb), 1)
    cur = kc >= qb
    dist = _block_pos(g, qa) - _block_pos(g, kc % qb) + jnp.where(cur, 0, qb)
    return (dist >= 0) & (dist <= QB) & (cur | (n > 0))


def _keys(g, prev_ref, cur_ref, sl):
    if ATT_NB[g] == 1:
        return _get(g, cur_ref, sl)
    return jnp.concatenate([_get(g, prev_ref, sl), _get(g, cur_ref, sl)], axis=0)


def _qkv_specs(g, clamp):
    cur = lambda col: _attn_spec(g, AW, col, clamp)
    prev = lambda col: _attn_spec(g, AW, col, lambda b: jnp.maximum(clamp(b) - 1, 0))
    qc, kc, vc = (c // AW + g for c in (COL_Q, COL_K, COL_V))
    return [cur(qc), cur(kc), prev(kc), cur(vc), prev(vc)]


def _attn_fwd(g, h):
    scale = HD ** -0.5
    hv = _attn_view(g, h)

    def body(q_ref, kc_ref, kp_ref, vc_ref, vp_ref, o_ref, l_ref):
        valid = _attn_mask(g, pl.program_id(0) % ATT_NB[g])
        for hh in range(NH):
            sl = slice(hh * HD, (hh + 1) * HD)
            kh, vh = _keys(g, kp_ref, kc_ref, sl), _keys(g, vp_ref, vc_ref, sl)
            s = jnp.where(valid, _nt(_get(g, q_ref, sl), kh) * scale, NEG_INF)
            m = jnp.max(s, axis=-1, keepdims=True)
            e = jnp.exp(s - m)
            den = jnp.sum(e, axis=-1, keepdims=True)
            _put(g, o_ref, sl, (_nn(e.astype(BF16), vh) * (1.0 / den)).astype(o_ref.dtype))
            _put(g, l_ref, slice(hh, hh + 1), m + jnp.log(den))

    same = lambda b: b
    o, lse = pl.pallas_call(
        body, name=f"attn_fwd_{g}", grid=(ATT_BLOCKS[g],),
        in_specs=_qkv_specs(g, same),
        out_specs=[_attn_spec(g, AW, 0, same), _attn_spec(g, NH, 0, same)],
        out_shape=[jax.ShapeDtypeStruct(_attn_shape(g, AW), BF16), jax.ShapeDtypeStruct(_attn_shape(g, NH), F32)],
        compiler_params=_cparams(1))(hv, hv, hv, hv, hv)
    return o.reshape(S, AW), lse.reshape(S, NH)


def _attn_bwd(g, dh, h, do, lse, delta):
    scale = HD ** -0.5
    qb = ATT_QB[g]
    carried = ATT_NB[g] > 1
    last = ATT_BLOCKS[g] - 1
    clamp = lambda b: jnp.minimum(b, last)
    behind = lambda b: jnp.maximum(b - 1, 0)
    hv = _attn_view(g, h)

    def body(q_ref, kc_ref, kp_ref, vc_ref, vp_ref, do_ref, l_ref, dl_ref, _, dh_ref, *carry):
        b = pl.program_id(0)

        def write(col, val):
            _put(g, dh_ref, slice(col, col + HD), val.astype(dh_ref.dtype))

        def block():
            valid = _attn_mask(g, b % ATT_NB[g])
            for hh in range(NH):
                sl = slice(hh * HD, (hh + 1) * HD)
                one = slice(hh, hh + 1)
                qh, doh = _get(g, q_ref, sl), _get(g, do_ref, sl)
                kh, vh = _keys(g, kp_ref, kc_ref, sl), _keys(g, vp_ref, vc_ref, sl)
                s = _nt(qh, kh) * scale
                p = jnp.where(valid, jnp.exp(s - _get(g, l_ref, one)), 0.0)
                ds = p * (_nt(doh, vh) - _get(g, dl_ref, one))
                dsb = (ds * scale).astype(BF16)
                dq = _nn(dsb, kh)
                dk2 = _tn(dsb, qh)
                dv2 = _tn(p.astype(BF16), doh)
                if carried:
                    cq_ref, ck_ref, cv_ref = carry
                    write(hh * HD, cq_ref[:, sl])
                    write(AW + hh * HD, ck_ref[:, sl] + dk2[:qb])
                    write(2 * AW + hh * HD, cv_ref[:, sl] + dv2[:qb])
                    cq_ref[:, sl] = dq
                    ck_ref[:, sl] = dk2[qb:]
                    cv_ref[:, sl] = dv2[qb:]
                else:
                    write(hh * HD, dq)
                    write(AW + hh * HD, dk2)
                    write(2 * AW + hh * HD, dv2)

        if not carried:
            block()
            return

        @pl.when(b == 0)
        def _():
            for ref in carry:
                ref[...] = jnp.zeros_like(ref)

        pl.when(b <= last)(block)

        @pl.when(b > last)
        def _():
            for hh in range(NH):
                for n, ref in enumerate(carry):
                    write(n * AW + hh * HD, ref[:, hh * HD:(hh + 1) * HD])

    out = pl.pallas_call(
        body, name=f"attn_bwd_{g}", grid=(ATT_BLOCKS[g] + carried,),
        in_specs=_qkv_specs(g, clamp) + [_attn_spec(g, AW, 0, clamp), _attn_spec(g, NH, 0, clamp),
                                         _attn_spec(g, NH, 0, clamp), ANY],
        out_specs=_attn_spec(g, 3 * AW, g, behind if carried else clamp),
        out_shape=jax.ShapeDtypeStruct(_attn_shape(g, NW), BF16),
        input_output_aliases={8: 0},
        scratch_shapes=[pltpu.VMEM((qb, AW), F32)] * (3 if carried else 0),
        compiler_params=_cparams(1))(hv, hv, hv, hv, hv, _attn_view(g, do), _attn_view(g, lse), _attn_view(g, delta),
                                     _attn_view(g, dh))
    return out.reshape(S, NW)


def _group_weights(l0, l1, l2):
    m = jnp.maximum(jnp.maximum(l0, l1), l2)
    e0, e1, e2 = jnp.exp(l0 - m), jnp.exp(l1 - m), jnp.exp(l2 - m)
    inv = 1.0 / (e0 + e1 + e2)
    return e0 * inv, e1 * inv, e2 * inv


def _residue(ref, r, sl):
    return ref[r * TI:(r + 1) * TI, sl].astype(F32)


def _total(parts):
    return functools.reduce(lambda x, y: x + y, parts)


def _pool_tokens(up_ref, uc_ref, p_ref, tile):
    j0 = lax.broadcasted_iota(jnp.int32, (TI, 1), 0) == 0
    first = (tile == 0) & j0
    for r in range(NR):
        out = []
        for g, w in enumerate(POOL_WINDOWS):
            sl = slice(g * PG, (g + 1) * PG)
            own = _residue(uc_ref, r, sl)
            acc = _total([own] + [_residue(uc_ref, r - k, sl) for k in range(1, min(r, w - 1) + 1)])
            wrapped = [NR + r - k for k in range(r + 1, w)]
            if wrapped:
                wc = _total([_residue(uc_ref, q, sl) for q in wrapped])
                wp = jnp.where(tile > 0, _total([_residue(up_ref, q, sl) for q in wrapped]), 0.0)
                acc = acc + jnp.where(j0, pltpu.roll(wp, 1, 0), pltpu.roll(wc, 1, 0))
            out.append(acc * jnp.where(first, 1.0 / min(r + 1, w), 1.0 / w) - own)
        p_ref[r * TI:(r + 1) * TI, :] = jnp.concatenate(out, axis=1).astype(p_ref.dtype)


def _pool_tokens_bwd(dp, nxt_ref, du_ref, tile):
    ji = lax.broadcasted_iota(jnp.int32, (TI, 1), 0)
    first = (tile == 0) & (ji == 0)
    piece = lambda g, r: dp[g][r * TI:(r + 1) * TI]
    dpc = [[piece(g, r) * jnp.where(first, 1.0 / min(r + 1, w), 1.0 / w) for r in range(NR)]
           for g, w in enumerate(POOL_WINDOWS)]
    for r in range(NR):
        out = []
        for g, w in enumerate(POOL_WINDOWS):
            sl = slice(g * PG, (g + 1) * PG)
            acc = _total([dpc[g][r + k] for k in range(w) if r + k < NR])
            wrapped = [r + k - NR for k in range(1, w) if r + k >= NR]
            if wrapped:
                wc = _total([dpc[g][q] for q in wrapped])
                wn = _total([nxt_ref[q * TI:(q + 1) * TI, sl] for q in wrapped])
                acc = acc + jnp.where(ji == TI - 1, pltpu.roll(wn, TI - 1, 0), pltpu.roll(wc, TI - 1, 0))
            out.append(acc - piece(g, r))
        du_ref[r * TI:(r + 1) * TI, :] = jnp.concatenate(out, axis=1).astype(du_ref.dtype)
    for r in range(NR):
        nxt_ref[r * TI:(r + 1) * TI, :] = jnp.concatenate([dpc[g][r] for g in range(len(POOL_WINDOWS))], axis=1)


def _pool_linear(pb, wpool_ref):
    return jnp.concatenate([_nn(pb[:, g * PG:(g + 1) * PG], wpool_ref[g]) for g in range(len(POOL_WINDOWS))], axis=1)


def _tok(width, col=0, rev=False):
    if rev:
        return pl.BlockSpec((TM, width), lambda i: (NT - 1 - i, col))
    return pl.BlockSpec((TM, width), lambda i: (i, col))


def _whole(shape):
    return pl.BlockSpec(shape, lambda i: (0,) * len(shape))


def _mix_fwd(h, o, lse, wpa, wpp, wpool, pscale, bgate):
    def body(o0_ref, o1_ref, o2_ref, l0_ref, l1_ref, l2_ref, za_ref, uc_ref, up_ref, zp_ref, gp_ref,
             wpa_ref, wpp_ref, wpool_ref, ps_ref, bg_ref,
             mg_ref, a_ref, b_ref, p_ref, yat_ref, ypt_ref, mgt_ref, ya_ref, yp_ref):
        i = pl.program_id(0)
        w0, w1, w2 = _group_weights(l0_ref[...], l1_ref[...], l2_ref[...])
        za = za_ref[...].astype(F32)
        silu_a = za * _sigmoid(za)
        for hh in range(NH):
            sl = slice(hh * HD, (hh + 1) * HD)
            c = slice(hh, hh + 1)
            oh = (w0[:, c] * o0_ref[:, sl].astype(F32) + w1[:, c] * o1_ref[:, sl].astype(F32)
                  + w2[:, c] * o2_ref[:, sl].astype(F32))
            ya = oh * silu_a[:, sl]
            ya_ref[:, sl] = ya.astype(BF16)
            yat_ref[sl, :] = ya.T.astype(BF16)
        _pool_tokens(up_ref, uc_ref, p_ref, i)
        zp = zp_ref[...].astype(F32)
        yp = _pool_linear(p_ref[...], wpool_ref) * ps_ref[...] * (zp * _sigmoid(zp))
        yp_ref[...] = yp.astype(BF16)
        ypt_ref[...] = yp.T.astype(BF16)
        a = _nn(ya_ref[...], wpa_ref[...])
        b = _nn(yp_ref[...], wpp_ref[...])
        a_ref[...] = a.astype(BF16)
        b_ref[...] = b.astype(BF16)
        gates = _sigmoid(gp_ref[...].astype(F32) + bg_ref[...])
        mg = gates[:, :D] * a + gates[:, D:] * b
        mg_ref[...] = mg.astype(BF16)
        mgt_ref[...] = mg.T.astype(BF16)

    u_prev = pl.BlockSpec((TM, AW), lambda i: (jnp.maximum(i - 1, 0), COL_U // AW))
    across = lambda width: pl.BlockSpec((width, TM), lambda i: (0, i))
    return pl.pallas_call(
        body, name="mix_fwd", grid=(NT,),
        in_specs=[_tok(AW)] * 3 + [_tok(NH)] * 3
        + [_tok(AW, COL_ZA // AW), _tok(AW, COL_U // AW), u_prev, _tok(AW, COL_ZP // AW), _tok(2 * D, COL_G // (2 * D))]
        + [_whole((AW, D)), _whole((AW, D)), _whole((4, PG, PG)), _whole((1, AW)), _whole((1, 2 * D))],
        out_specs=[_tok(D), _tok(D), _tok(D), _tok(AW), across(AW), across(AW), across(D)],
        out_shape=[jax.ShapeDtypeStruct((S, D), BF16)] * 3 + [jax.ShapeDtypeStruct((S, AW), BF16)]
        + [jax.ShapeDtypeStruct((AW, S), BF16)] * 2 + [jax.ShapeDtypeStruct((D, S), BF16)],
        scratch_shapes=[pltpu.VMEM((TM, AW), BF16), pltpu.VMEM((TM, AW), BF16)],
        compiler_params=_cparams(1))(*o, *lse, h, h, h, h, h, wpa, wpp, wpool, pscale, bgate)


def _out_ln(merged, x, target, wout, gamma, beta):
    def body(mg_ref, x_ref, t_ref, w_ref, g_ref, b_ref, dr_ref, drb_ref, dm_ref, loss_ref, dg_ref, db_ref):
        i = pl.program_id(0)

        @pl.when(i == 0)
        def _():
            loss_ref[...] = jnp.zeros_like(loss_ref)
            dg_ref[...] = jnp.zeros_like(dg_ref)
            db_ref[...] = jnp.zeros_like(db_ref)

        r = ALPHA * x_ref[...] + _nn(mg_ref[...], w_ref[...])
        mu = jnp.mean(r, axis=-1, keepdims=True)
        rc = r - mu
        rstd = lax.rsqrt(jnp.mean(rc * rc, axis=-1, keepdims=True) + LN_EPS)
        xhat = rc * rstd
        err = xhat * g_ref[...] + b_ref[...] - t_ref[...]
        loss_ref[...] += 0.5 * jnp.sum(jnp.mean(err * err, axis=-1, keepdims=True), axis=0, keepdims=True)
        dy = err * (1.0 / D)
        dg_ref[...] += jnp.sum(dy * xhat, axis=0, keepdims=True)
        db_ref[...] += jnp.sum(dy, axis=0, keepdims=True)
        dxh = dy * g_ref[...]
        dr = rstd * (dxh - jnp.mean(dxh, axis=-1, keepdims=True)
                     - xhat * jnp.mean(dxh * xhat, axis=-1, keepdims=True))
        dr_ref[...] = dr
        drb_ref[...] = dr.astype(BF16)
        dm_ref[...] = _nt(drb_ref[...], w_ref[...]).astype(BF16)

    return pl.pallas_call(
        body, name="out_ln", grid=(NT,),
        in_specs=[_tok(D), _tok(D), _tok(D), _whole((D, D)), _whole((1, D)), _whole((1, D))],
        out_specs=[_tok(D), _tok(D), _tok(D), _whole((8, 128)), _whole((1, D)), _whole((1, D))],
        out_shape=[jax.ShapeDtypeStruct((S, D), F32), jax.ShapeDtypeStruct((S, D), BF16),
                   jax.ShapeDtypeStruct((S, D), BF16), jax.ShapeDtypeStruct((8, 128), F32),
                   jax.ShapeDtypeStruct((1, D), F32), jax.ShapeDtypeStruct((1, D), F32)],
        compiler_params=_cparams(1))(merged, x, target, wout, gamma, beta)


def _gate_bwd(dm, a, b, h, bgate):
    def body(dm_ref, a_ref, b_ref, gp_ref, bg_ref, dgp_ref, da_ref, db_ref, dbg_ref):
        @pl.when(pl.program_id(0) == 0)
        def _():
            dbg_ref[...] = jnp.zeros_like(dbg_ref)

        dm_ = dm_ref[...].astype(F32)
        gates = _sigmoid(gp_ref[...].astype(F32) + bg_ref[...])
        ga, gb = gates[:, :D], gates[:, D:]
        da_ref[...] = (dm_ * ga).astype(BF16)
        db_ref[...] = (dm_ * gb).astype(BF16)
        dgp = jnp.concatenate([dm_ * a_ref[...].astype(F32) * ga * (1.0 - ga),
                               dm_ * b_ref[...].astype(F32) * gb * (1.0 - gb)], axis=1)
        dgp_ref[...] = dgp.astype(BF16)
        dbg_ref[...] += jnp.sum(dgp, axis=0, keepdims=True)

    return pl.pallas_call(
        body, name="gate_bwd", grid=(NT,),
        in_specs=[_tok(D), _tok(D), _tok(D), _tok(2 * D, COL_G // (2 * D)), _whole((1, 2 * D))],
        out_specs=[_tok(2 * D, DH_G // (2 * D)), _tok(D), _tok(D), _whole((1, 2 * D))],
        out_shape=[jax.ShapeDtypeStruct((S, NW), BF16), jax.ShapeDtypeStruct((S, D), BF16),
                   jax.ShapeDtypeStruct((S, D), BF16), jax.ShapeDtypeStruct((1, 2 * D), F32)],
        compiler_params=_cparams(1))(dm, a, b, h, bgate)


def _mix_bwd(dh, da, db, h, o, lse, p, wpa, wpp, wpool, pscale):
    def body(_, da_ref, db_ref, o0_ref, o1_ref, o2_ref, l0_ref, l1_ref, l2_ref, za_ref, zp_ref, p_ref,
             wpa_ref, wpp_ref, wpool_ref, ps_ref,
             dh_ref, do0_ref, do1_ref, do2_ref, dl0_ref, dl1_ref, dl2_ref, dwp_ref, dps_ref,
             nxt_ref):
        i = pl.program_id(0)
        tile = NT - 1 - i
        dza_ref, du_ref, dzp_ref = (dh_ref.at[:, pl.ds(n * AW, AW)] for n in range(3))

        @pl.when(i == 0)
        def _():
            nxt_ref[...] = jnp.zeros_like(nxt_ref)
            dwp_ref[...] = jnp.zeros_like(dwp_ref)
            dps_ref[...] = jnp.zeros_like(dps_ref)

        dya = _nt(da_ref[...], wpa_ref[...])
        w0, w1, w2 = _group_weights(l0_ref[...], l1_ref[...], l2_ref[...])
        za = za_ref[...].astype(F32)
        sig = _sigmoid(za)
        silu_a = za * sig
        dsilu_a = sig * (1.0 + za * (1.0 - sig))
        for hh in range(NH):
            sl = slice(hh * HD, (hh + 1) * HD)
            c = slice(hh, hh + 1)
            oh = (w0[:, c] * o0_ref[:, sl].astype(F32) + w1[:, c] * o1_ref[:, sl].astype(F32)
                  + w2[:, c] * o2_ref[:, sl].astype(F32))
            doh = dya[:, sl] * silu_a[:, sl]
            dza_ref[:, sl] = (dya[:, sl] * oh * dsilu_a[:, sl]).astype(BF16)
            dot_ = jnp.sum(doh * oh, axis=-1, keepdims=True)
            do0_ref[:, sl] = (w0[:, c] * doh).astype(BF16)
            do1_ref[:, sl] = (w1[:, c] * doh).astype(BF16)
            do2_ref[:, sl] = (w2[:, c] * doh).astype(BF16)
            dl0_ref[:, c] = w0[:, c] * dot_
            dl1_ref[:, c] = w1[:, c] * dot_
            dl2_ref[:, c] = w2[:, c] * dot_
        dyp = _nt(db_ref[...], wpp_ref[...])
        pb = p_ref[...]
        pw = _pool_linear(pb, wpool_ref)
        zp = zp_ref[...].astype(F32)
        sigp = _sigmoid(zp)
        dypre = dyp * (zp * sigp)
        dzp_ref[...] = (dyp * (pw * ps_ref[...]) * (sigp * (1.0 + zp * (1.0 - sigp)))).astype(BF16)
        dps_ref[...] += jnp.sum(dypre * pw, axis=0, keepdims=True)
        dpw = (dypre * ps_ref[...]).astype(BF16)
        dp = []
        for g in range(len(POOL_WINDOWS)):
            sl = slice(g * PG, (g + 1) * PG)
            dwp_ref[g] += _tn(pb[:, sl], dpw[:, sl])
            dp.append(_nt(dpw[:, sl], wpool_ref[g]))
        _pool_tokens_bwd(dp, nxt_ref, du_ref, tile)

    r = functools.partial(_tok, rev=True)
    return pl.pallas_call(
        body, name="mix_bwd", grid=(NT,),
        in_specs=[ANY, r(D), r(D)] + [r(AW)] * 3 + [r(NH)] * 3 + [r(AW, COL_ZA // AW), r(AW, COL_ZP // AW), r(AW)]
        + [_whole((AW, D)), _whole((AW, D)), _whole((4, PG, PG)), _whole((1, AW))],
        out_specs=[r(3 * AW, DH_Z // (3 * AW))] + [r(AW)] * 3 + [r(NH)] * 3 + [_whole((4, PG, PG)), _whole((1, AW))],
        out_shape=[jax.ShapeDtypeStruct((S, NW), BF16)] + [jax.ShapeDtypeStruct((S, AW), BF16)] * 3
        + [jax.ShapeDtypeStruct((S, NH), F32)] * 3
        + [jax.ShapeDtypeStruct((4, PG, PG), F32), jax.ShapeDtypeStruct((1, AW), F32)],
        input_output_aliases={0: 0},
        scratch_shapes=[pltpu.VMEM((TM, AW), F32)],
        compiler_params=_cparams(1))(dh, da, db, *o, *lse, h, h, p, wpa, wpp, wpool, pscale)


def _adamw(w, g, m, v):
    m = B1 * m + (1.0 - B1) * g
    v = B2 * v + (1.0 - B2) * jnp.square(g)
    m_hat = m / (1.0 - B1 ** STEP)
    v_hat = v / (1.0 - B2 ** STEP)
    return -LR * (m_hat / (jnp.sqrt(v_hat) + EPS) + WD * w), m, v


def _adam_shard(name, q, l2, w, m, v, tr):
    rows = w.shape[0]

    def body(q_ref, l_ref, w_ref, m_ref, v_ref, g_out, d_out, m_out, v_out):
        g = q_ref[...].astype(F32)
        for k in range(2):
            g = g + l_ref[k].astype(F32)
        g_out[...] = g
        d_out[...], m_out[...], v_out[...] = _adamw(w_ref[...], g, m_ref[...], v_ref[...])

    blk = pl.BlockSpec((tr, D), lambda i: (i, 0))
    return pl.pallas_call(
        body, name=name, grid=(rows // tr,),
        in_specs=[pl.BlockSpec((None, tr, D), lambda i: (0, i, 0)), pl.BlockSpec((2, tr, D), lambda i: (0, i, 0)),
                  blk, blk, blk],
        out_specs=[blk] * 4, out_shape=[jax.ShapeDtypeStruct((rows, D), F32)] * 4,
        compiler_params=_cparams(1))(q, l2, w, m, v)


def _sum_small(q, l2):
    def body(q_ref, l_ref, g_out, buf, sems):
        rows = pl.ds(R_OUT, R_SMALL)
        copies = [pltpu.make_async_copy(src, buf.at[n], sems.at[n])
                  for n, src in enumerate((q_ref.at[0, rows], l_ref.at[0, rows], l_ref.at[1, rows]))]
        for cp in copies:
            cp.start()
        for cp in copies:
            cp.wait()
        g_out[...] = buf[0].astype(F32) + buf[1].astype(F32) + buf[2].astype(F32)

    return pl.pallas_call(
        body, name="sum_small", in_specs=[ANY, ANY], out_shape=jax.ShapeDtypeStruct((R_SMALL, D), F32),
        scratch_shapes=[pltpu.VMEM((3, R_SMALL, D), q.dtype), pltpu.SemaphoreType.DMA((3,))],
        compiler_params=pltpu.CompilerParams(vmem_limit_bytes=VMEM_LIMIT))(q, l2)


def _adam_whole(name, grads, weights, ms, vs):
    n = len(grads)

    def body(*refs):
        ins, outs = refs[:4 * n], refs[4 * n:]
        for t in range(n):
            g, w, m, v = (ins[k * n + t][...] for k in range(4))
            outs[t][...], outs[n + t][...], outs[2 * n + t][...] = _adamw(w, g, m, v)

    out = pl.pallas_call(
        body, name=name, out_shape=[jax.ShapeDtypeStruct(w.shape, F32) for w in weights] * 3,
        compiler_params=pltpu.CompilerParams(vmem_limit_bytes=VMEM_LIMIT))(*grads, *weights, *ms, *vs)
    return out[:n], out[n:2 * n], out[2 * n:]


def _sum_replicated(gathered):
    def body(g_ref, bg_out, ps_out, gm_out, bt_out, loss_out):
        g = g_ref[0]
        for k in range(1, N_DEV):
            g = g + g_ref[k]
        bg_out[...] = jnp.concatenate([g[0:1], g[1:2]], axis=1)
        gm_out[...] = g[2:3]
        bt_out[...] = g[3:4]
        ps_out[...] = g[4:5, :AW]
        loss_out[...] = jnp.broadcast_to(g[5:6, :128], loss_out.shape)

    return pl.pallas_call(
        body, name="sum_replicated",
        out_shape=[jax.ShapeDtypeStruct(shape, F32) for shape in ((1, 2 * D), (1, AW), (1, D), (1, D), (8, 128))],
        compiler_params=pltpu.CompilerParams(vmem_limit_bytes=VMEM_LIMIT))(gathered)


def _pack_small(w_out, w_pa, w_pp, w_pool):
    return jnp.concatenate([w_out, w_pa.reshape(-1, D), w_pp.reshape(-1, D), w_pool.reshape(-1, D)], axis=0)


def _unpack_small(a):
    o = R_OUT
    return (a[:R_PA - o], a[R_PA - o:R_PP - o].reshape(AW, 256), a[R_PP - o:R_PL - o].reshape(AW, 256),
            a[R_PL - o:].reshape(4, 32, PG))


def _pack_vec(b_gate, gamma, beta, pscale, extra):
    z = jnp.zeros((D,), F32)
    return jnp.stack([b_gate[:D], b_gate[D:], gamma, beta, jnp.concatenate([pscale, z[:D - AW]]),
                      jnp.broadcast_to(extra, (D,)), z, z])


def kernel(x, w_in, b_gate, w_pool, pool_scale, w_proj_attn, w_proj_pool, w_out, ln_gamma, ln_beta, loss_target, m_w_in, m_b_gate, m_w_pool, m_pool_scale, m_w_proj_attn, m_w_proj_pool, m_w_out, m_ln_gamma, m_ln_beta, v_w_in, v_b_gate, v_w_pool, v_pool_scale, v_w_proj_attn, v_w_proj_pool, v_w_out, v_ln_gamma, v_ln_beta):
    coords = jnp.stack([lax.axis_index("x"), lax.axis_index("y"), lax.axis_index("c")]).astype(jnp.int32)
    x2, tgt = _permute_tokens(x[0]), _permute_tokens(loss_target[0])

    spack = _pack_small(w_out[0], w_proj_attn[0], w_proj_pool[0], w_pool[0]).astype(BF16)
    xb = _permute_tokens(x[0].astype(BF16))
    h, gw, xt = _ag_proj(_arrival_order(*coords), xb, w_in[0].astype(BF16), spack)
    wout = gw[:, R_OUT:R_PA].reshape(D, D)
    wpa = gw[:, R_PA:R_PP].reshape(N_DEV, AW, 256).transpose(1, 0, 2).reshape(AW, D)
    wpp = gw[:, R_PP:R_PL].reshape(N_DEV, AW, 256).transpose(1, 0, 2).reshape(AW, D)
    wpool = gw[:, R_PL:].reshape(N_DEV, 4, 32, PG).transpose(1, 0, 2, 3).reshape(4, PG, PG)

    o, lse = zip(*[_attn_fwd(g, h) for g in range(len(DILATIONS))])
    merged, a, b, p, yat, ypt, mgt = _mix_fwd(h, o, lse, wpa, wpp, wpool, pool_scale, b_gate)
    dr, drb, dm, loss_part, dgamma, dbeta = _out_ln(merged, x2, tgt, wout, ln_gamma, ln_beta)

    dh, da, db, dbgate = _gate_bwd(dm, a, b, h, b_gate)
    dh, do0, do1, do2, dl0, dl1, dl2, dwpool, dpscale = _mix_bwd(
        dh, da, db, h, o, lse, p, wpa, wpp, wpool, pool_scale)
    for g, (do_g, dl_g) in enumerate(zip((do0, do1, do2), (dl0, dl1, dl2))):
        dh = _attn_bwd(g, dh, h, do_g, lse[g], dl_g)

    q = _grad_w_in_rs(_rs_columns(*coords), xt, dh)
    d_wout = _grad_w("grad_w_out", mgt, drb)
    d_wpa = _grad_w("grad_w_pa", yat, da)
    d_wpp = _grad_w("grad_w_pp", ypt, db)
    small = jnp.concatenate([
        d_wout.reshape(N_DEV, 256, D),
        d_wpa.reshape(AW, N_DEV, 256).transpose(1, 0, 2).reshape(N_DEV, -1, D),
        d_wpp.reshape(AW, N_DEV, 256).transpose(1, 0, 2).reshape(N_DEV, -1, D),
        dwpool.astype(BF16).reshape(4, N_DEV, 32, PG).transpose(1, 0, 2, 3).reshape(N_DEV, -1, D)], axis=1)
    q = _pair_sum_small(coords, small, _rs_sibling(small), q)

    vec = _pack_vec(dbgate[0], dgamma[0], dbeta[0], dpscale[0], loss_part[0, 0])
    grad_x, l2, vecs_all = _grad_x_rs(dh, gw, dr, q, vec)
    g_in, d_in, m_in, v_in = _adam_shard("adam_w_in", q, l2, w_in[0], m_w_in[0], v_w_in[0], 256)
    g_small = [t.reshape(w.shape) for t, w in zip(_unpack_small(_sum_small(q, l2)),
                                                  (w_out, w_proj_attn, w_proj_pool, w_pool))]
    small = (g_small,) + _adam_whole("adam_small", g_small, (w_out, w_proj_attn, w_proj_pool, w_pool),
                                     (m_w_out, m_w_proj_attn, m_w_proj_pool, m_w_pool),
                                     (v_w_out, v_w_proj_attn, v_w_proj_pool, v_w_pool))

    *g_vec, loss = _sum_replicated(vecs_all)
    vecs = (g_vec,) + _adam_whole("adam_replicated", g_vec, (b_gate, pool_scale, ln_gamma, ln_beta),
                                  (m_b_gate, m_pool_scale, m_ln_gamma, m_ln_beta),
                                  (v_b_gate, v_pool_scale, v_ln_gamma, v_ln_beta))
    loss = loss[0, 0]

    def leaves(kind, big):
        out, pa, pp, pool = small[kind]
        bg, ps, gm, bt = vecs[kind]
        return [big[None], bg, pool, ps, pa, pp, out, gm, bt]

    return (loss, _permute_tokens(grad_x)[None], *leaves(0, g_in), *leaves(1, d_in), *leaves(2, m_in), *leaves(3, v_in))
```

```python
import functools

import jax
import jax.numpy as jnp
from jax import lax
from jax.experimental import pallas as pl
from jax.experimental.pallas import tpu as pltpu

F32 = jnp.float32
BF16 = jnp.bfloat16

S = 4096
D = 2048
NW = 16384
AW = 1024
HD = 128
NH = 8
QB = 128
DILATIONS = (1, 4, 16)
POOL_WINDOWS = (2, 4, 8, 16)
PG = 256
N_DEV = 8
COL_Q, COL_K, COL_V = 0, 3 * AW, 6 * AW
COL_ZA, COL_U, COL_ZP, COL_G = 9 * AW, 10 * AW, 11 * AW, 12 * AW
DH_Z, DH_G = COL_ZA, COL_G
ALPHA = 2.0 ** 0.25
LN_EPS = 1e-5
NEG_INF = -1e30
LR, B1, B2, EPS, WD, STEP = 0.001, 0.9, 0.999, 1e-08, 0.01, 10
R_IN, R_OUT, R_PA, R_PP, R_PL = 0, 2048, 2304, 2432, 2560
R_ALL = 2576
R_SMALL = R_ALL - R_OUT
VMEM_LIMIT = 56 * 1024 * 1024
MESH = pl.DeviceIdType.MESH
ANY = pl.BlockSpec(memory_space=pl.ANY)


def _cparams(n_axes):
    return pltpu.CompilerParams(dimension_semantics=("arbitrary",) * n_axes, vmem_limit_bytes=VMEM_LIMIT)


def _sigmoid(z):
    return 0.5 * jnp.tanh(0.5 * z) + 0.5


def _nt(a, b):
    return lax.dot_general(a, b, (((1,), (1,)), ((), ())), preferred_element_type=F32)


def _tn(a, b):
    return lax.dot_general(a, b, (((0,), (0,)), ((), ())), preferred_element_type=F32)


def _nn(a, b):
    return jnp.dot(a, b, preferred_element_type=F32)


def _lin(x, y, c):
    return 4 * x + 2 * y + c


def _flip(v, f):
    return 1 - v if f else v


CHIP_FLIPS = ((0, 0), (1, 0), (0, 1), (1, 1))


AG_PIECES = ((pl.ds(R_IN, D), pl.ds(0, 1024)), (pl.ds(R_IN, D), pl.ds(1024, 1024)),
             (pl.ds(R_OUT, R_PA - R_OUT), pl.ds(0, D)), (pl.ds(R_PA, R_ALL - R_PA), pl.ds(0, D)))
N_PIECES = len(AG_PIECES)
SIB, TO_X, TO_Y, ON, PASS_X, PASS_Y, PASS_D = range(7)
AG_TILES = ((0, 0), (0, 1), (1, 0), (1, 1), (2, 0), (4, 0), (3, 0), (5, 0),
            (2, 1), (4, 1), (3, 1), (5, 1), (6, 0), (6, 1), (7, 0), (7, 1))
W, G = "wait", "go"
AG_STEPS = {
    2: [(W, SIB, 0)], 3: [(W, SIB, 1)],
    4: [(W, TO_X, 0), (G, ON, 0), (G, PASS_X, 0)], 5: [(W, TO_Y, 0), (G, PASS_Y, 0)],
    6: [(W, PASS_X, 0)], 7: [(W, PASS_Y, 0)],
    8: [(W, TO_X, 1), (G, PASS_X, 1), (W, TO_Y, 1), (G, ON, 1), (G, PASS_Y, 1),
        (G, TO_X, 2), (G, TO_X, 3), (G, TO_Y, 2), (G, TO_Y, 3)],
    10: [(W, PASS_X, 1)], 11: [(W, PASS_Y, 1)],
    12: [(W, ON, 0), (G, PASS_D, 0)], 13: [(W, ON, 1), (G, PASS_D, 1)],
    14: [(W, PASS_D, 0), (W, TO_X, 2), (G, ON, 2), (G, PASS_X, 2), (W, TO_X, 3), (G, PASS_X, 3),
         (W, TO_Y, 2), (G, PASS_Y, 2), (W, TO_Y, 3), (G, ON, 3), (G, PASS_Y, 3)],
    15: [(W, PASS_D, 1)],
}
AG_LAST = [(W, SIB, 2), (W, SIB, 3), (W, ON, 2), (G, PASS_D, 2), (W, ON, 3), (G, PASS_D, 3),
           (W, PASS_X, 2), (W, PASS_X, 3), (W, PASS_Y, 2), (W, PASS_Y, 3), (W, PASS_D, 2), (W, PASS_D, 3)]


def _arrival_order(x, y, c):
    chips = [(x, y), (1 - x, y), (x, 1 - y), (1 - x, 1 - y)]
    return jnp.stack([_lin(px, py, pc) for px, py in chips for pc in (c, 1 - c)]).astype(jnp.int32)


def _ag_proj(order, x2, wbf, spack):
    tm, tn = 1024, 1024
    nrow, ntile = S // tm, len(AG_TILES)
    slabs = jnp.stack([order[pos] for pos, _ in AG_TILES])
    cols = jnp.stack([2 * order[pos] + half for pos, half in AG_TILES])

    def body(cols_ref, slabs_ref, x_ref, w_ref, s_ref, h_ref, gw_ref, xt_ref, xbv, wbuf, wsem, send_sems, recv_sems,
             local_sems):
        t, i = pl.program_id(0), pl.program_id(1)
        x, y, c = lax.axis_index("x"), lax.axis_index("y"), lax.axis_index("c")
        me = _lin(x, y, c)
        dev = {"sib": (x, y, 1 - c), "x": (1 - x, y, c), "y": (x, 1 - y, c), "d": (1 - x, 1 - y, c)}

        def slab_of(name, other_core=False):
            px, py, pc = dev[name]
            return _lin(px, py, 1 - pc if other_core else pc)

        def own(piece):
            rows, colz = AG_PIECES[piece]
            return w_ref.at[:, colz] if piece < 2 else s_ref.at[pl.ds(rows.start - R_OUT, rows.size)]

        def rdma(slab, kind, piece, to, from_own=False):
            k = kind * N_PIECES + piece
            there = gw_ref.at[(slab, *AG_PIECES[piece])]
            return pltpu.make_async_remote_copy(
                src_ref=own(piece) if from_own else there, dst_ref=there,
                send_sem=send_sems.at[k], recv_sem=recv_sems.at[k], device_id=dev[to], device_id_type=MESH)

        def mine(kind, piece):
            if kind in (SIB, TO_X, TO_Y):
                return rdma(me, kind, piece, ("sib", "x", "y")[kind], from_own=True)
            if kind == ON:
                frm, to = ("x", "y") if piece % 2 == 0 else ("y", "x")
                return rdma(slab_of(frm), kind, piece, to)
            return rdma(slab_of({PASS_X: "x", PASS_Y: "y", PASS_D: "d"}[kind]), kind, piece, "sib")

        def landing(kind, piece):
            slab = {SIB: slab_of("sib"), TO_X: slab_of("x"), TO_Y: slab_of("y"), ON: slab_of("d"),
                    PASS_X: slab_of("x", True), PASS_Y: slab_of("y", True), PASS_D: slab_of("d", True)}[kind]
            return rdma(slab, kind, piece, "sib")

        def run(steps):
            for what, kind, piece in steps:
                if what == W:
                    landing(kind, piece).wait_recv()
                else:
                    mine(kind, piece).start()

        local = [pltpu.make_async_copy(w_ref, gw_ref.at[me, pl.ds(R_IN, D)], local_sems.at[0]),
                 pltpu.make_async_copy(s_ref, gw_ref.at[me, pl.ds(R_OUT, R_SMALL)], local_sems.at[1])]

        def fetch(slab, half, slot):
            src = own(half) if slab is None else gw_ref.at[(slab, *AG_PIECES[half])]
            return pltpu.make_async_copy(src, wbuf.at[slot], wsem.at[slot])

        @pl.when((t == 0) & (i == 0))
        def _():
            for cp in local:
                cp.start()
            run([(G, kind, piece) for piece in (0, 1) for kind in (TO_X, TO_Y, SIB)] + [(G, SIB, 2), (G, SIB, 3)])
            first = fetch(None, 0, 0)
            first.start()
            first.wait()

        for nxt in range(1, ntile):
            @pl.when((t == nxt - 1) & (i == nrow - 1))
            def _(nxt=nxt):
                run(AG_STEPS.get(nxt, []))
                fetch(None if AG_TILES[nxt][0] == 0 else slabs_ref[nxt], AG_TILES[nxt][1], nxt % 2).start()

        @pl.when(t == 0)
        def _():
            xbv[i] = x_ref[...].astype(BF16)
            xt_ref[...] = x_ref[...].T.astype(BF16)

        for slot in (0, 1):
            @pl.when(t % 2 == slot)
            def _(slot=slot):
                @pl.when((i == 0) & (t > 0))
                def _():
                    fetch(None, 0, slot).wait()
                h_ref[...] = _nn(xbv[i], wbuf[slot]).astype(h_ref.dtype)

        @pl.when((t == ntile - 1) & (i == nrow - 1))
        def _():
            run(AG_LAST)
            for kind in range(7):
                for piece in range(N_PIECES):
                    mine(kind, piece).wait_send()
            for cp in local:
                cp.wait()

    n_sem = 7 * N_PIECES
    grid_spec = pltpu.PrefetchScalarGridSpec(
        num_scalar_prefetch=2, grid=(ntile, nrow),
        in_specs=[pl.BlockSpec((tm, D), lambda t, i, cols, slabs: (jnp.where(t == 0, i, nrow - 1), 0)), ANY, ANY],
        out_specs=[pl.BlockSpec((tm, tn), lambda t, i, cols, slabs: (i, cols[t])), ANY,
                   pl.BlockSpec((D, tm), lambda t, i, cols, slabs: (0, jnp.where(t == 0, i, nrow - 1)))],
        scratch_shapes=[pltpu.VMEM((nrow, tm, D), BF16), pltpu.VMEM((2, D, tn), BF16), pltpu.SemaphoreType.DMA((2,)),
                        pltpu.SemaphoreType.DMA((n_sem,)), pltpu.SemaphoreType.DMA((n_sem,)),
                        pltpu.SemaphoreType.DMA((2,))])
    return pl.pallas_call(
        body, name="ag_proj", grid_spec=grid_spec,
        out_shape=[jax.ShapeDtypeStruct((S, NW), BF16), jax.ShapeDtypeStruct((N_DEV, R_ALL, D), BF16),
                   jax.ShapeDtypeStruct((D, S), BF16)],
        compiler_params=_cparams(2))(cols, slabs, x2, wbf, spack)


def _rs_sibling(p):
    n = len(CHIP_FLIPS)

    def body(p_ref, l_ref, send_sems, recv_sems):
        x, y, c = lax.axis_index("x"), lax.axis_index("y"), lax.axis_index("c")
        copies = [pltpu.make_async_remote_copy(
            src_ref=p_ref.at[_lin(_flip(x, fx), _flip(y, fy), 1 - c)], dst_ref=l_ref.at[k],
            send_sem=send_sems.at[k], recv_sem=recv_sems.at[k], device_id=(x, y, 1 - c), device_id_type=MESH)
            for k, (fx, fy) in enumerate(CHIP_FLIPS)]
        for cp in copies:
            cp.start()
        for cp in copies:
            cp.wait_recv()
        for cp in copies:
            cp.wait_send()

    return pl.pallas_call(
        body, name="rs_sibling", out_shape=jax.ShapeDtypeStruct((n,) + p.shape[1:], p.dtype),
        in_specs=[ANY], out_specs=ANY,
        scratch_shapes=[pltpu.SemaphoreType.DMA((n,)), pltpu.SemaphoreType.DMA((n,))])(p)


def _pair_sum_small(coords, p, l1, q):
    def body(crd, p_ref, l_ref, _, q_ref, buf, sem):
        k = pl.program_id(0)
        buf[...] = (p_ref[...].astype(F32) + l_ref[...].astype(F32)).astype(buf.dtype)
        out = pltpu.make_async_copy(buf, q_ref.at[k, pl.ds(R_OUT, R_SMALL)], sem)
        out.start()
        out.wait()

    def p_map(k, crd):
        fx, fy = k % 2, k // 2
        px = crd[0] + fx - 2 * fx * crd[0]
        py = crd[1] + fy - 2 * fy * crd[1]
        return (_lin(px, py, crd[2]), 0, 0)

    grid_spec = pltpu.PrefetchScalarGridSpec(
        num_scalar_prefetch=1, grid=(4,),
        in_specs=[pl.BlockSpec((None, R_SMALL, D), p_map),
                  pl.BlockSpec((None, R_SMALL, D), lambda k, crd: (k, 0, 0)), ANY],
        out_specs=ANY,
        scratch_shapes=[pltpu.VMEM((R_SMALL, D), BF16), pltpu.SemaphoreType.DMA(())])
    return pl.pallas_call(body, name="pair_sum_small", grid_spec=grid_spec,
                          out_shape=jax.ShapeDtypeStruct(q.shape, q.dtype), input_output_aliases={3: 0},
                          compiler_params=_cparams(1))(coords, p, l1, q)


def _h_block(k):
    return jnp.where(k < 9, (k % 3) * 3 + k // 3, k)


RS_PIECES = (pl.ds(0, 1280), pl.ds(1280, R_ALL - 1280))
RS_ROWS = (1280, R_ALL - 1280)
RS_CHUNKS = ((320,) * 4, (432,) * 3)
RS_MERGE_STEP = 2


def _grad_x_rs(dh, g, dr, q, vec):
    others = [(fx, fy, fc) for fx in (0, 1) for fy in (0, 1) for fc in (0, 1) if (fx, fy, fc) != (0, 0, 0)]
    tm, tk = 1024, 1024
    ni, nk = S // tm, NW // tk
    rmax = max(RS_ROWS)
    cmax = max(max(c) for c in RS_CHUNKS)

    def body(dh_ref, w_ref, dr_ref, q_ref, vec_ref, o_ref, l2_ref, ld_ref, mg_ref, all_ref, va, vb,
             send_sems, recv_sems, sems):
        i, k = pl.program_id(0), pl.program_id(1)
        x, y, c = lax.axis_index("x"), lax.axis_index("y"), lax.axis_index("c")
        nbr = ((1 - x, y, c), (x, 1 - y, c))

        def vec_copy(n, sender):
            sx, sy, sc = sender
            fx, fy, fc = others[n]
            return pltpu.make_async_remote_copy(
                src_ref=vec_ref, dst_ref=all_ref.at[_lin(sx, sy, sc)], send_sem=send_sems.at[6 + n],
                recv_sem=recv_sems.at[6 + n], device_id=(_flip(sx, fx), _flip(sy, fy), _flip(sc, fc)),
                device_id_type=MESH)

        vec_own = pltpu.make_async_copy(vec_ref, all_ref.at[_lin(x, y, c)], sems.at[3])

        def rows(ref, piece):
            return ref.at[piece, pl.ds(0, RS_ROWS[piece])]

        copies = (
            (q_ref.at[3, RS_PIECES[0]], rows(ld_ref, 0), 0),
            (q_ref.at[3, RS_PIECES[1]], rows(ld_ref, 1), 1),
            (q_ref.at[1, RS_PIECES[0]], l2_ref.at[0, RS_PIECES[0]], 0),
            (q_ref.at[2, RS_PIECES[1]], l2_ref.at[1, RS_PIECES[1]], 1),
            (rows(mg_ref, 0), l2_ref.at[1, RS_PIECES[0]], 1),
            (rows(mg_ref, 1), l2_ref.at[0, RS_PIECES[1]], 0),
        )

        def copy(n):
            src, dst, axis = copies[n]
            return pltpu.make_async_remote_copy(src_ref=src, dst_ref=dst, send_sem=send_sems.at[n],
                                                recv_sem=recv_sems.at[n], device_id=nbr[axis], device_id_type=MESH)

        def merge(piece, mine):
            start = 0
            for n_rows in RS_CHUNKS[piece]:
                own = pltpu.make_async_copy(q_ref.at[mine, pl.ds(RS_PIECES[piece].start + start, n_rows)],
                                            va.at[pl.ds(0, n_rows)], sems.at[0])
                got = pltpu.make_async_copy(ld_ref.at[piece, pl.ds(start, n_rows)], vb.at[pl.ds(0, n_rows)], sems.at[1])
                own.start()
                got.start()
                own.wait()
                got.wait()
                va[pl.ds(0, n_rows)] = (va[pl.ds(0, n_rows)].astype(F32)
                                        + vb[pl.ds(0, n_rows)].astype(F32)).astype(va.dtype)
                out = pltpu.make_async_copy(va.at[pl.ds(0, n_rows)], mg_ref.at[piece, pl.ds(start, n_rows)], sems.at[2])
                out.start()
                out.wait()
                start += n_rows

        @pl.when((i == 0) & (k == 0))
        def _():
            for n in range(4):
                copy(n).start()
            vec_own.start()
            for n in range(len(others)):
                vec_copy(n, (x, y, c)).start()

        @pl.when((i == RS_MERGE_STEP) & (k == 0))
        def _():
            copy(0).wait_recv()
            merge(0, 2)
            copy(4).start()
            copy(1).wait_recv()
            merge(1, 1)
            copy(5).start()

        @pl.when(k == 0)
        def _():
            o_ref[...] = ALPHA * dr_ref[...]

        o_ref[...] += _nt(dh_ref[...], w_ref[...])

        @pl.when((i == ni - 1) & (k == nk - 1))
        def _():
            for n in range(2, 6):
                copy(n).wait_recv()
            for n in range(6):
                copy(n).wait_send()
            for n, (fx, fy, fc) in enumerate(others):
                vec_copy(n, (_flip(x, fx), _flip(y, fy), _flip(c, fc))).wait_recv()
                vec_copy(n, (x, y, c)).wait_send()
            vec_own.wait()

    slab = q.shape[1:]
    out = pl.pallas_call(
        body, name="grad_x_rs", grid=(ni, nk),
        in_specs=[pl.BlockSpec((tm, tk), lambda i, k: (i, k)),
                  pl.BlockSpec((None, D, tk), lambda i, k: (_h_block(k) // 2, 0, _h_block(k) % 2)),
                  pl.BlockSpec((tm, D), lambda i, k: (i, 0)), ANY, ANY],
        out_specs=[pl.BlockSpec((tm, D), lambda i, k: (i, 0)), ANY, ANY, ANY, ANY],
        out_shape=[jax.ShapeDtypeStruct((S, D), F32), jax.ShapeDtypeStruct((2,) + slab, q.dtype),
                   jax.ShapeDtypeStruct((2, rmax, slab[1]), q.dtype), jax.ShapeDtypeStruct((2, rmax, slab[1]), q.dtype),
                   jax.ShapeDtypeStruct((N_DEV,) + vec.shape, vec.dtype)],
        scratch_shapes=[pltpu.VMEM((cmax, slab[1]), q.dtype), pltpu.VMEM((cmax, slab[1]), q.dtype),
                        pltpu.SemaphoreType.DMA((6 + len(others),)), pltpu.SemaphoreType.DMA((6 + len(others),)),
                        pltpu.SemaphoreType.DMA((4,))],
        compiler_params=_cparams(2))(dh, g, dr, q, vec)
    return out[0], out[1], out[4]


GW_TN = 512
GW_PARTS = D // GW_TN


def _rs_columns(x, y, c):
    per_h = AW // GW_TN
    out = []
    for core in (1 - c, c):
        for fx, fy in CHIP_FLIPS:
            for part in range(GW_PARTS):
                h_block = 2 * _lin(_flip(x, fx), _flip(y, fy), core) + part // per_h
                out.append(_h_block(h_block) * per_h + part % per_h)
    return jnp.stack(out).astype(jnp.int32)


def _grad_w_in_rs(cols, xt, dh):
    n_tile = 4 * GW_PARTS

    def body(cols_ref, a_ref, b_ref, q_ref, l1_ref, stage, landed, send_sems, recv_sems, sem):
        t = pl.program_id(0)
        sib = (lax.axis_index("x"), lax.axis_index("y"), 1 - lax.axis_index("c"))

        def there(n):
            return l1_ref.at[n // GW_PARTS, :, pl.ds((n % GW_PARTS) * GW_TN, GW_TN)]

        def send(n):
            return pltpu.make_async_remote_copy(src_ref=stage.at[n % 2], dst_ref=there(n), send_sem=send_sems.at[n],
                                                recv_sem=recv_sems.at[n], device_id=sib, device_id_type=MESH)

        def fetch(n):
            return pltpu.make_async_copy(there(n), landed, sem)

        for n in range(n_tile):
            @pl.when(t == n_tile + n)
            def _(n=n):
                if n == 0:
                    send(n_tile - 2).wait_send()
                    send(n_tile - 1).wait_send()
                send(n).wait_recv()
                fetch(n).start()

        part = _nn(a_ref[...], b_ref[...])

        for n in range(n_tile):
            @pl.when(t == n)
            def _(n=n):
                if n >= 2:
                    send(n - 2).wait_send()
                stage[n % 2] = part.astype(stage.dtype)
                send(n).start()

            @pl.when(t == n_tile + n)
            def _(n=n):
                fetch(n).wait()
                q_ref[...] = (part + landed[...].astype(F32)).astype(q_ref.dtype)

    mine = lambda t: jnp.maximum(t - n_tile, 0)
    grid_spec = pltpu.PrefetchScalarGridSpec(
        num_scalar_prefetch=1, grid=(2 * n_tile,),
        in_specs=[pl.BlockSpec((D, S), lambda t, cols: (0, 0), pipeline_mode=pl.Buffered(1)),
                  pl.BlockSpec((S, GW_TN), lambda t, cols: (0, cols[t]))],
        out_specs=[pl.BlockSpec((None, D, GW_TN), lambda t, cols: (mine(t) // GW_PARTS, 0, mine(t) % GW_PARTS)), ANY],
        scratch_shapes=[pltpu.VMEM((2, D, GW_TN), BF16), pltpu.VMEM((D, GW_TN), BF16),
                        pltpu.SemaphoreType.DMA((n_tile,)), pltpu.SemaphoreType.DMA((n_tile,)),
                        pltpu.SemaphoreType.DMA(())])
    q, _ = pl.pallas_call(
        body, name="grad_w_in_rs", grid_spec=grid_spec,
        out_shape=[jax.ShapeDtypeStruct((4, R_ALL, D), BF16), jax.ShapeDtypeStruct((4, D, D), BF16)],
        compiler_params=_cparams(1))(cols, xt, dh)
    return q


def _grad_w(name, at, b):
    m, n_all = at.shape[0], b.shape[1]

    def body(a_ref, b_ref, o_ref):
        o_ref[...] = _nn(a_ref[...], b_ref[...]).astype(o_ref.dtype)

    return pl.pallas_call(
        body, name=name, grid=(n_all // GW_TN,),
        in_specs=[pl.BlockSpec((m, S), lambda n: (0, 0), pipeline_mode=pl.Buffered(1)),
                  pl.BlockSpec((S, GW_TN), lambda n: (0, n))],
        out_specs=pl.BlockSpec((m, GW_TN), lambda n: (0, n)), out_shape=jax.ShapeDtypeStruct((m, n_all), BF16),
        compiler_params=_cparams(1))(at, b)


NR = 16
TI = 16
TM = NR * TI
NT = S // TM
ATT_QB = (256, 128, 256)
ATT_NB = (16, 8, 1)
ATT_BLOCKS = (16, 32, 16)


def _permute_tokens(a):
    return a.reshape(NT, TI, NR, a.shape[-1]).transpose(0, 2, 1, 3).reshape(a.shape)


def _attn_shape(g, c):
    if g == 0:
        return (S, c)
    if g == 1:
        return (NT, 4, 4, TI, c)
    return (NT, NR, TI, c)


def _attn_view(g, a):
    return a.reshape(_attn_shape(g, a.shape[-1]))


def _attn_spec(g, width, col, blk):
    if g == 0:
        return pl.BlockSpec((TM, width), lambda b: (blk(b), col))
    if g == 1:
        return pl.BlockSpec((2, 4, None, TI, width), lambda b: (blk(b) % 8, 0, blk(b) // 8, 0, col))
    return pl.BlockSpec((NT, None, TI, width), lambda b: (0, blk(b), 0, col))


def _pieces(g):
    if g == 1:
        return [(t, m) for t in range(2) for m in range(4)]
    return [(t,) for t in range(NT)]


def _get(g, ref, sl):
    if g == 0:
        return ref[:, sl]
    return jnp.concatenate([ref[(*p, slice(None), sl)] for p in _pieces(g)], axis=0)


def _put(g, ref, sl, val):
    if g == 0:
        ref[:, sl] = val
    else:
        for n, p in enumerate(_pieces(g)):
            ref[(*p, slice(None), sl)] = val[TI * n:TI * (n + 1)]


def _block_pos(g, a):
    if g == 0:
        return 16 * (a % 16) + a // 16
    if g == 1:
        return 64 * (a // 64) + 4 * (a % 16) + (a // 16) % 4
    return a


def _attn_mask(g, n):
    qb = ATT_QB[g]
    if ATT_NB[g] == 1:
        qa = lax.broadcasted_iota(jnp.int32, (qb, qb), 0)
        kc = lax.broadcasted_iota(jnp.int32, (qb, qb), 1)
        dist = _block_pos(g, qa) - _block_pos(g, kc)
        return (dist >= 0) & (dist <= QB)
    qa = lax.broadcasted_iota(jnp.int32, (qb, 2 * qb), 0)
    kc = lax.broadcasted_iota(jnp.int32, (qb, 2 * qb), 1)
    cur = kc >= qb
    dist = _block_pos(g, qa) - _block_pos(g, kc % qb) + jnp.where(cur, 0, qb)
    return (dist >= 0) & (dist <= QB) & (cur | (n > 0))


def _keys(g, prev_ref, cur_ref, sl):
    if ATT_NB[g] == 1:
        return _get(g, cur_ref, sl)
    return jnp.concatenate([_get(g, prev_ref, sl), _get(g, cur_ref, sl)], axis=0)


def _qkv_specs(g, clamp):
    cur = lambda col: _attn_spec(g, AW, col, clamp)
    prev = lambda col: _attn_spec(g, AW, col, lambda b: jnp.maximum(clamp(b) - 1, 0))
    qc, kc, vc = (c // AW + g for c in (COL_Q, COL_K, COL_V))
    return [cur(qc), cur(kc), prev(kc), cur(vc), prev(vc)]


def _attn_fwd(g, h):
    scale = HD ** -0.5
    hv = _attn_view(g, h)

    def body(q_ref, kc_ref, kp_ref, vc_ref, vp_ref, o_ref, l_ref):
        valid = _attn_mask(g, pl.program_id(0) % ATT_NB[g])
        for hh in range(NH):
            sl = slice(hh * HD, (hh + 1) * HD)
            kh, vh = _keys(g, kp_ref, kc_ref, sl), _keys(g, vp_ref, vc_ref, sl)
            s = jnp.where(valid, _nt(_get(g, q_ref, sl), kh) * scale, NEG_INF)
            m = jnp.max(s, axis=-1, keepdims=True)
            e = jnp.exp(s - m)
            den = jnp.sum(e, axis=-1, keepdims=True)
            _put(g, o_ref, sl, (_nn(e.astype(BF16), vh) * (1.0 / den)).astype(o_ref.dtype))
            _put(g, l_ref, slice(hh, hh + 1), m + jnp.log(den))

    same = lambda b: b
    o, lse = pl.pallas_call(
        body, name=f"attn_fwd_{g}", grid=(ATT_BLOCKS[g],),
        in_specs=_qkv_specs(g, same),
        out_specs=[_attn_spec(g, AW, 0, same), _attn_spec(g, NH, 0, same)],
        out_shape=[jax.ShapeDtypeStruct(_attn_shape(g, AW), BF16), jax.ShapeDtypeStruct(_attn_shape(g, NH), F32)],
        compiler_params=_cparams(1))(hv, hv, hv, hv, hv)
    return o.reshape(S, AW), lse.reshape(S, NH)


def _attn_bwd(g, dh, h, do, lse, delta):
    scale = HD ** -0.5
    qb = ATT_QB[g]
    carried = ATT_NB[g] > 1
    last = ATT_BLOCKS[g] - 1
    clamp = lambda b: jnp.minimum(b, last)
    behind = lambda b: jnp.maximum(b - 1, 0)
    hv = _attn_view(g, h)

    def body(q_ref, kc_ref, kp_ref, vc_ref, vp_ref, do_ref, l_ref, dl_ref, _, dh_ref, *carry):
        b = pl.program_id(0)

        def write(col, val):
            _put(g, dh_ref, slice(col, col + HD), val.astype(dh_ref.dtype))

        def block():
            valid = _attn_mask(g, b % ATT_NB[g])
            for hh in range(NH):
                sl = slice(hh * HD, (hh + 1) * HD)
                one = slice(hh, hh + 1)
                qh, doh = _get(g, q_ref, sl), _get(g, do_ref, sl)
                kh, vh = _keys(g, kp_ref, kc_ref, sl), _keys(g, vp_ref, vc_ref, sl)
                s = _nt(qh, kh) * scale
                p = jnp.where(valid, jnp.exp(s - _get(g, l_ref, one)), 0.0)
                ds = p * (_nt(doh, vh) - _get(g, dl_ref, one))
                dsb = (ds * scale).astype(BF16)
                dq = _nn(dsb, kh)
                dk2 = _tn(dsb, qh)
                dv2 = _tn(p.astype(BF16), doh)
                if carried:
                    cq_ref, ck_ref, cv_ref = carry
                    write(hh * HD, cq_ref[:, sl])
                    write(AW + hh * HD, ck_ref[:, sl] + dk2[:qb])
                    write(2 * AW + hh * HD, cv_ref[:, sl] + dv2[:qb])
                    cq_ref[:, sl] = dq
                    ck_ref[:, sl] = dk2[qb:]
                    cv_ref[:, sl] = dv2[qb:]
                else:
                    write(hh * HD, dq)
                    write(AW + hh * HD, dk2)
                    write(2 * AW + hh * HD, dv2)

        if not carried:
            block()
            return

        @pl.when(b == 0)
        def _():
            for ref in carry:
                ref[...] = jnp.zeros_like(ref)

        pl.when(b <= last)(block)

        @pl.when(b > last)
        def _():
            for hh in range(NH):
                for n, ref in enumerate(carry):
                    write(n * AW + hh * HD, ref[:, hh * HD:(hh + 1) * HD])

    out = pl.pallas_call(
        body, name=f"attn_bwd_{g}", grid=(ATT_BLOCKS[g] + carried,),
        in_specs=_qkv_specs(g, clamp) + [_attn_spec(g, AW, 0, clamp), _attn_spec(g, NH, 0, clamp),
                                         _attn_spec(g, NH, 0, clamp), ANY],
        out_specs=_attn_spec(g, 3 * AW, g, behind if carried else clamp),
        out_shape=jax.ShapeDtypeStruct(_attn_shape(g, NW), BF16),
        input_output_aliases={8: 0},
        scratch_shapes=[pltpu.VMEM((qb, AW), F32)] * (3 if carried else 0),
        compiler_params=_cparams(1))(hv, hv, hv, hv, hv, _attn_view(g, do), _attn_view(g, lse), _attn_view(g, delta),
                                     _attn_view(g, dh))
    return out.reshape(S, NW)


def _group_weights(l0, l1, l2):
    m = jnp.maximum(jnp.maximum(l0, l1), l2)
    e0, e1, e2 = jnp.exp(l0 - m), jnp.exp(l1 - m), jnp.exp(l2 - m)
    inv = 1.0 / (e0 + e1 + e2)
    return e0 * inv, e1 * inv, e2 * inv


def _residue(ref, r, sl):
    return ref[r * TI:(r + 1) * TI, sl].astype(F32)


def _total(parts):
    return functools.reduce(lambda x, y: x + y, parts)


def _pool_tokens(up_ref, uc_ref, p_ref, tile):
    j0 = lax.broadcasted_iota(jnp.int32, (TI, 1), 0) == 0
    first = (tile == 0) & j0
    for r in range(NR):
        out = []
        for g, w in enumerate(POOL_WINDOWS):
            sl = slice(g * PG, (g + 1) * PG)
            own = _residue(uc_ref, r, sl)
            acc = _total([own] + [_residue(uc_ref, r - k, sl) for k in range(1, min(r, w - 1) + 1)])
            wrapped = [NR + r - k for k in range(r + 1, w)]
            if wrapped:
                wc = _total([_residue(uc_ref, q, sl) for q in wrapped])
                wp = jnp.where(tile > 0, _total([_residue(up_ref, q, sl) for q in wrapped]), 0.0)
                acc = acc + jnp.where(j0, pltpu.roll(wp, 1, 0), pltpu.roll(wc, 1, 0))
            out.append(acc * jnp.where(first, 1.0 / min(r + 1, w), 1.0 / w) - own)
        p_ref[r * TI:(r + 1) * TI, :] = jnp.concatenate(out, axis=1).astype(p_ref.dtype)


def _pool_tokens_bwd(dp, nxt_ref, du_ref, tile):
    ji = lax.broadcasted_iota(jnp.int32, (TI, 1), 0)
    first = (tile == 0) & (ji == 0)
    piece = lambda g, r: dp[g][r * TI:(r + 1) * TI]
    dpc = [[piece(g, r) * jnp.where(first, 1.0 / min(r + 1, w), 1.0 / w) for r in range(NR)]
           for g, w in enumerate(POOL_WINDOWS)]
    for r in range(NR):
        out = []
        for g, w in enumerate(POOL_WINDOWS):
            sl = slice(g * PG, (g + 1) * PG)
            acc = _total([dpc[g][r + k] for k in range(w) if r + k < NR])
            wrapped = [r + k - NR for k in range(1, w) if r + k >= NR]
            if wrapped:
                wc = _total([dpc[g][q] for q in wrapped])
                wn = _total([nxt_ref[q * TI:(q + 1) * TI, sl] for q in wrapped])
                acc = acc + jnp.where(ji == TI - 1, pltpu.roll(wn, TI - 1, 0), pltpu.roll(wc, TI - 1, 0))
            out.append(acc - piece(g, r))
        du_ref[r * TI:(r + 1) * TI, :] = jnp.concatenate(out, axis=1).astype(du_ref.dtype)
    for r in range(NR):
        nxt_ref[r * TI:(r + 1) * TI, :] = jnp.concatenate([dpc[g][r] for g in range(len(POOL_WINDOWS))], axis=1)


def _pool_linear(pb, wpool_ref):
    return jnp.concatenate([_nn(pb[:, g * PG:(g + 1) * PG], wpool_ref[g]) for g in range(len(POOL_WINDOWS))], axis=1)


def _tok(width, col=0, rev=False):
    if rev:
        return pl.BlockSpec((TM, width), lambda i: (NT - 1 - i, col))
    return pl.BlockSpec((TM, width), lambda i: (i, col))


def _whole(shape):
    return pl.BlockSpec(shape, lambda i: (0,) * len(shape))


def _mix_fwd(h, o, lse, wpa, wpp, wpool, pscale, bgate):
    def body(o0_ref, o1_ref, o2_ref, l0_ref, l1_ref, l2_ref, za_ref, uc_ref, up_ref, zp_ref, gp_ref,
             wpa_ref, wpp_ref, wpool_ref, ps_ref, bg_ref,
             mg_ref, a_ref, b_ref, p_ref, yat_ref, ypt_ref, mgt_ref, ya_ref, yp_ref):
        i = pl.program_id(0)
        w0, w1, w2 = _group_weights(l0_ref[...], l1_ref[...], l2_ref[...])
        za = za_ref[...].astype(F32)
        silu_a = za * _sigmoid(za)
        for hh in range(NH):
            sl = slice(hh * HD, (hh + 1) * HD)
            c = slice(hh, hh + 1)
            oh = (w0[:, c] * o0_ref[:, sl].astype(F32) + w1[:, c] * o1_ref[:, sl].astype(F32)
                  + w2[:, c] * o2_ref[:, sl].astype(F32))
            ya = oh * silu_a[:, sl]
            ya_ref[:, sl] = ya.astype(BF16)
            yat_ref[sl, :] = ya.T.astype(BF16)
        _pool_tokens(up_ref, uc_ref, p_ref, i)
        zp = zp_ref[...].astype(F32)
        yp = _pool_linear(p_ref[...], wpool_ref) * ps_ref[...] * (zp * _sigmoid(zp))
        yp_ref[...] = yp.astype(BF16)
        ypt_ref[...] = yp.T.astype(BF16)
        a = _nn(ya_ref[...], wpa_ref[...])
        b = _nn(yp_ref[...], wpp_ref[...])
        a_ref[...] = a.astype(BF16)
        b_ref[...] = b.astype(BF16)
        gates = _sigmoid(gp_ref[...].astype(F32) + bg_ref[...])
        mg = gates[:, :D] * a + gates[:, D:] * b
        mg_ref[...] = mg.astype(BF16)
        mgt_ref[...] = mg.T.astype(BF16)

    u_prev = pl.BlockSpec((TM, AW), lambda i: (jnp.maximum(i - 1, 0), COL_U // AW))
    across = lambda width: pl.BlockSpec((width, TM), lambda i: (0, i))
    return pl.pallas_call(
        body, name="mix_fwd", grid=(NT,),
        in_specs=[_tok(AW)] * 3 + [_tok(NH)] * 3
        + [_tok(AW, COL_ZA // AW), _tok(AW, COL_U // AW), u_prev, _tok(AW, COL_ZP // AW), _tok(2 * D, COL_G // (2 * D))]
        + [_whole((AW, D)), _whole((AW, D)), _whole((4, PG, PG)), _whole((1, AW)), _whole((1, 2 * D))],
        out_specs=[_tok(D), _tok(D), _tok(D), _tok(AW), across(AW), across(AW), across(D)],
        out_shape=[jax.ShapeDtypeStruct((S, D), BF16)] * 3 + [jax.ShapeDtypeStruct((S, AW), BF16)]
        + [jax.ShapeDtypeStruct((AW, S), BF16)] * 2 + [jax.ShapeDtypeStruct((D, S), BF16)],
        scratch_shapes=[pltpu.VMEM((TM, AW), BF16), pltpu.VMEM((TM, AW), BF16)],
        compiler_params=_cparams(1))(*o, *lse, h, h, h, h, h, wpa, wpp, wpool, pscale, bgate)


def _out_ln(merged, x, target, wout, gamma, beta):
    def body(mg_ref, x_ref, t_ref, w_ref, g_ref, b_ref, dr_ref, drb_ref, dm_ref, loss_ref, dg_ref, db_ref):
        i = pl.program_id(0)

        @pl.when(i == 0)
        def _():
            loss_ref[...] = jnp.zeros_like(loss_ref)
            dg_ref[...] = jnp.zeros_like(dg_ref)
            db_ref[...] = jnp.zeros_like(db_ref)

        r = ALPHA * x_ref[...] + _nn(mg_ref[...], w_ref[...])
        mu = jnp.mean(r, axis=-1, keepdims=True)
        rc = r - mu
        rstd = lax.rsqrt(jnp.mean(rc * rc, axis=-1, keepdims=True) + LN_EPS)
        xhat = rc * rstd
        err = xhat * g_ref[...] + b_ref[...] - t_ref[...]
        loss_ref[...] += 0.5 * jnp.sum(jnp.mean(err * err, axis=-1, keepdims=True), axis=0, keepdims=True)
        dy = err * (1.0 / D)
        dg_ref[...] += jnp.sum(dy * xhat, axis=0, keepdims=True)
        db_ref[...] += jnp.sum(dy, axis=0, keepdims=True)
        dxh = dy * g_ref[...]
        dr = rstd * (dxh - jnp.mean(dxh, axis=-1, keepdims=True)
                     - xhat * jnp.mean(dxh * xhat, axis=-1, keepdims=True))
        dr_ref[...] = dr
        drb_ref[...] = dr.astype(BF16)
        dm_ref[...] = _nt(drb_ref[...], w_ref[...]).astype(BF16)

    return pl.pallas_call(
        body, name="out_ln", grid=(NT,),
        in_specs=[_tok(D), _tok(D), _tok(D), _whole((D, D)), _whole((1, D)), _whole((1, D))],
        out_specs=[_tok(D), _tok(D), _tok(D), _whole((8, 128)), _whole((1, D)), _whole((1, D))],
        out_shape=[jax.ShapeDtypeStruct((S, D), F32), jax.ShapeDtypeStruct((S, D), BF16),
                   jax.ShapeDtypeStruct((S, D), BF16), jax.ShapeDtypeStruct((8, 128), F32),
                   jax.ShapeDtypeStruct((1, D), F32), jax.ShapeDtypeStruct((1, D), F32)],
        compiler_params=_cparams(1))(merged, x, target, wout, gamma, beta)


def _gate_bwd(dm, a, b, h, bgate):
    def body(dm_ref, a_ref, b_ref, gp_ref, bg_ref, dgp_ref, da_ref, db_ref, dbg_ref):
        @pl.when(pl.program_id(0) == 0)
        def _():
            dbg_ref[...] = jnp.zeros_like(dbg_ref)

        dm_ = dm_ref[...].astype(F32)
        gates = _sigmoid(gp_ref[...].astype(F32) + bg_ref[...])
        ga, gb = gates[:, :D], gates[:, D:]
        da_ref[...] = (dm_ * ga).astype(BF16)
        db_ref[...] = (dm_ * gb).astype(BF16)
        dgp = jnp.concatenate([dm_ * a_ref[...].astype(F32) * ga * (1.0 - ga),
                               dm_ * b_ref[...].astype(F32) * gb * (1.0 - gb)], axis=1)
        dgp_ref[...] = dgp.astype(BF16)
        dbg_ref[...] += jnp.sum(dgp, axis=0, keepdims=True)

    return pl.pallas_call(
        body, name="gate_bwd", grid=(NT,),
        in_specs=[_tok(D), _tok(D), _tok(D), _tok(2 * D, COL_G // (2 * D)), _whole((1, 2 * D))],
        out_specs=[_tok(2 * D, DH_G // (2 * D)), _tok(D), _tok(D), _whole((1, 2 * D))],
        out_shape=[jax.ShapeDtypeStruct((S, NW), BF16), jax.ShapeDtypeStruct((S, D), BF16),
                   jax.ShapeDtypeStruct((S, D), BF16), jax.ShapeDtypeStruct((1, 2 * D), F32)],
        compiler_params=_cparams(1))(dm, a, b, h, bgate)


def _mix_bwd(dh, da, db, h, o, lse, p, wpa, wpp, wpool, pscale):
    def body(_, da_ref, db_ref, o0_ref, o1_ref, o2_ref, l0_ref, l1_ref, l2_ref, za_ref, zp_ref, p_ref,
             wpa_ref, wpp_ref, wpool_ref, ps_ref,
             dh_ref, do0_ref, do1_ref, do2_ref, dl0_ref, dl1_ref, dl2_ref, dwp_ref, dps_ref,
             nxt_ref):
        i = pl.program_id(0)
        tile = NT - 1 - i
        dza_ref, du_ref, dzp_ref = (dh_ref.at[:, pl.ds(n * AW, AW)] for n in range(3))

        @pl.when(i == 0)
        def _():
            nxt_ref[...] = jnp.zeros_like(nxt_ref)
            dwp_ref[...] = jnp.zeros_like(dwp_ref)
            dps_ref[...] = jnp.zeros_like(dps_ref)

        dya = _nt(da_ref[...], wpa_ref[...])
        w0, w1, w2 = _group_weights(l0_ref[...], l1_ref[...], l2_ref[...])
        za = za_ref[...].astype(F32)
        sig = _sigmoid(za)
        silu_a = za * sig
        dsilu_a = sig * (1.0 + za * (1.0 - sig))
        for hh in range(NH):
            sl = slice(hh * HD, (hh + 1) * HD)
            c = slice(hh, hh + 1)
            oh = (w0[:, c] * o0_ref[:, sl].astype(F32) + w1[:, c] * o1_ref[:, sl].astype(F32)
                  + w2[:, c] * o2_ref[:, sl].astype(F32))
            doh = dya[:, sl] * silu_a[:, sl]
            dza_ref[:, sl] = (dya[:, sl] * oh * dsilu_a[:, sl]).astype(BF16)
            dot_ = jnp.sum(doh * oh, axis=-1, keepdims=True)
            do0_ref[:, sl] = (w0[:, c] * doh).astype(BF16)
            do1_ref[:, sl] = (w1[:, c] * doh).astype(BF16)
            do2_ref[:, sl] = (w2[:, c] * doh).astype(BF16)
            dl0_ref[:, c] = w0[:, c] * dot_
            dl1_ref[:, c] = w1[:, c] * dot_
            dl2_ref[:, c] = w2[:, c] * dot_
        dyp = _nt(db_ref[...], wpp_ref[...])
        pb = p_ref[...]
        pw = _pool_linear(pb, wpool_ref)
        zp = zp_ref[...].astype(F32)
        sigp = _sigmoid(zp)
        dypre = dyp * (zp * sigp)
        dzp_ref[...] = (dyp * (pw * ps_ref[...]) * (sigp * (1.0 + zp * (1.0 - sigp)))).astype(BF16)
        dps_ref[...] += jnp.sum(dypre * pw, axis=0, keepdims=True)
        dpw = (dypre * ps_ref[...]).astype(BF16)
        dp = []
        for g in range(len(POOL_WINDOWS)):
            sl = slice(g * PG, (g + 1) * PG)
            dwp_ref[g] += _tn(pb[:, sl], dpw[:, sl])
            dp.append(_nt(dpw[:, sl], wpool_ref[g]))
        _pool_tokens_bwd(dp, nxt_ref, du_ref, tile)

    r = functools.partial(_tok, rev=True)
    return pl.pallas_call(
        body, name="mix_bwd", grid=(NT,),
        in_specs=[ANY, r(D), r(D)] + [r(AW)] * 3 + [r(NH)] * 3 + [r(AW, COL_ZA // AW), r(AW, COL_ZP // AW), r(AW)]
        + [_whole((AW, D)), _whole((AW, D)), _whole((4, PG, PG)), _whole((1, AW))],
        out_specs=[r(3 * AW, DH_Z // (3 * AW))] + [r(AW)] * 3 + [r(NH)] * 3 + [_whole((4, PG, PG)), _whole((1, AW))],
        out_shape=[jax.ShapeDtypeStruct((S, NW), BF16)] + [jax.ShapeDtypeStruct((S, AW), BF16)] * 3
        + [jax.ShapeDtypeStruct((S, NH), F32)] * 3
        + [jax.ShapeDtypeStruct((4, PG, PG), F32), jax.ShapeDtypeStruct((1, AW), F32)],
        input_output_aliases={0: 0},
        scratch_shapes=[pltpu.VMEM((TM, AW), F32)],
        compiler_params=_cparams(1))(dh, da, db, *o, *lse, h, h, p, wpa, wpp, wpool, pscale)


def _adamw(w, g, m, v):
    m = B1 * m + (1.0 - B1) * g
    v = B2 * v + (1.0 - B2) * jnp.square(g)
    m_hat = m / (1.0 - B1 ** STEP)
    v_hat = v / (1.0 - B2 ** STEP)
    return -LR * (m_hat / (jnp.sqrt(v_hat) + EPS) + WD * w), m, v


def _adam_shard(name, q, l2, w, m, v, tr):
    rows = w.shape[0]

    def body(q_ref, l_ref, w_ref, m_ref, v_ref, g_out, d_out, m_out, v_out):
        g = q_ref[...].astype(F32)
        for k in range(2):
            g = g + l_ref[k].astype(F32)
        g_out[...] = g
        d_out[...], m_out[...], v_out[...] = _adamw(w_ref[...], g, m_ref[...], v_ref[...])

    blk = pl.BlockSpec((tr, D), lambda i: (i, 0))
    return pl.pallas_call(
        body, name=name, grid=(rows // tr,),
        in_specs=[pl.BlockSpec((None, tr, D), lambda i: (0, i, 0)), pl.BlockSpec((2, tr, D), lambda i: (0, i, 0)),
                  blk, blk, blk],
        out_specs=[blk] * 4, out_shape=[jax.ShapeDtypeStruct((rows, D), F32)] * 4,
        compiler_params=_cparams(1))(q, l2, w, m, v)


def _sum_small(q, l2):
    def body(q_ref, l_ref, g_out, buf, sems):
        rows = pl.ds(R_OUT, R_SMALL)
        copies = [pltpu.make_async_copy(src, buf.at[n], sems.at[n])
                  for n, src in enumerate((q_ref.at[0, rows], l_ref.at[0, rows], l_ref.at[1, rows]))]
        for cp in copies:
            cp.start()
        for cp in copies:
            cp.wait()
        g_out[...] = buf[0].astype(F32) + buf[1].astype(F32) + buf[2].astype(F32)

    return pl.pallas_call(
        body, name="sum_small", in_specs=[ANY, ANY], out_shape=jax.ShapeDtypeStruct((R_SMALL, D), F32),
        scratch_shapes=[pltpu.VMEM((3, R_SMALL, D), q.dtype), pltpu.SemaphoreType.DMA((3,))],
        compiler_params=pltpu.CompilerParams(vmem_limit_bytes=VMEM_LIMIT))(q, l2)


def _adam_whole(name, grads, weights, ms, vs):
    n = len(grads)

    def body(*refs):
        ins, outs = refs[:4 * n], refs[4 * n:]
        for t in range(n):
            g, w, m, v = (ins[k * n + t][...] for k in range(4))
            outs[t][...], outs[n + t][...], outs[2 * n + t][...] = _adamw(w, g, m, v)

    out = pl.pallas_call(
        body, name=name, out_shape=[jax.ShapeDtypeStruct(w.shape, F32) for w in weights] * 3,
        compiler_params=pltpu.CompilerParams(vmem_limit_bytes=VMEM_LIMIT))(*grads, *weights, *ms, *vs)
    return out[:n], out[n:2 * n], out[2 * n:]


def _sum_replicated(gathered):
    def body(g_ref, bg_out, ps_out, gm_out, bt_out, loss_out):
        g = g_ref[0]
        for k in range(1, N_DEV):
            g = g + g_ref[k]
        bg_out[...] = jnp.concatenate([g[0:1], g[1:2]], axis=1)
        gm_out[...] = g[2:3]
        bt_out[...] = g[3:4]
        ps_out[...] = g[4:5, :AW]
        loss_out[...] = jnp.broadcast_to(g[5:6, :128], loss_out.shape)

    return pl.pallas_call(
        body, name="sum_replicated",
        out_shape=[jax.ShapeDtypeStruct(shape, F32) for shape in ((1, 2 * D), (1, AW), (1, D), (1, D), (8, 128))],
        compiler_params=pltpu.CompilerParams(vmem_limit_bytes=VMEM_LIMIT))(gathered)


def _pack_small(w_out, w_pa, w_pp, w_pool):
    return jnp.concatenate([w_out, w_pa.reshape(-1, D), w_pp.reshape(-1, D), w_pool.reshape(-1, D)], axis=0)


def _unpack_small(a):
    o = R_OUT
    return (a[:R_PA - o], a[R_PA - o:R_PP - o].reshape(AW, 256), a[R_PP - o:R_PL - o].reshape(AW, 256),
            a[R_PL - o:].reshape(4, 32, PG))


def _pack_vec(b_gate, gamma, beta, pscale, extra):
    z = jnp.zeros((D,), F32)
    return jnp.stack([b_gate[:D], b_gate[D:], gamma, beta, jnp.concatenate([pscale, z[:D - AW]]),
                      jnp.broadcast_to(extra, (D,)), z, z])


def kernel(x, w_in, b_gate, w_pool, pool_scale, w_proj_attn, w_proj_pool, w_out, ln_gamma, ln_beta, loss_target, m_w_in, m_b_gate, m_w_pool, m_pool_scale, m_w_proj_attn, m_w_proj_pool, m_w_out, m_ln_gamma, m_ln_beta, v_w_in, v_b_gate, v_w_pool, v_pool_scale, v_w_proj_attn, v_w_proj_pool, v_w_out, v_ln_gamma, v_ln_beta):
    coords = jnp.stack([lax.axis_index("x"), lax.axis_index("y"), lax.axis_index("c")]).astype(jnp.int32)
    x2, tgt = _permute_tokens(x[0]), _permute_tokens(loss_target[0])

    spack = _pack_small(w_out[0], w_proj_attn[0], w_proj_pool[0], w_pool[0]).astype(BF16)
    h, gw, xt = _ag_proj(_arrival_order(*coords), x2, w_in[0].astype(BF16), spack)
    wout = gw[:, R_OUT:R_PA].reshape(D, D)
    wpa = gw[:, R_PA:R_PP].reshape(N_DEV, AW, 256).transpose(1, 0, 2).reshape(AW, D)
    wpp = gw[:, R_PP:R_PL].reshape(N_DEV, AW, 256).transpose(1, 0, 2).reshape(AW, D)
    wpool = gw[:, R_PL:].reshape(N_DEV, 4, 32, PG).transpose(1, 0, 2, 3).reshape(4, PG, PG)

    o, lse = zip(*[_attn_fwd(g, h) for g in range(len(DILATIONS))])
    merged, a, b, p, yat, ypt, mgt = _mix_fwd(h, o, lse, wpa, wpp, wpool, pool_scale, b_gate)
    dr, drb, dm, loss_part, dgamma, dbeta = _out_ln(merged, x2, tgt, wout, ln_gamma, ln_beta)

    dh, da, db, dbgate = _gate_bwd(dm, a, b, h, b_gate)
    dh, do0, do1, do2, dl0, dl1, dl2, dwpool, dpscale = _mix_bwd(
        dh, da, db, h, o, lse, p, wpa, wpp, wpool, pool_scale)
    for g, (do_g, dl_g) in enumerate(zip((do0, do1, do2), (dl0, dl1, dl2))):
        dh = _attn_bwd(g, dh, h, do_g, lse[g], dl_g)

    q = _grad_w_in_rs(_rs_columns(*coords), xt, dh)
    d_wout = _grad_w("grad_w_out", mgt, drb)
    d_wpa = _grad_w("grad_w_pa", yat, da)
    d_wpp = _grad_w("grad_w_pp", ypt, db)
    small = jnp.concatenate([
        d_wout.reshape(N_DEV, 256, D),
        d_wpa.reshape(AW, N_DEV, 256).transpose(1, 0, 2).reshape(N_DEV, -1, D),
        d_wpp.reshape(AW, N_DEV, 256).transpose(1, 0, 2).reshape(N_DEV, -1, D),
        dwpool.astype(BF16).reshape(4, N_DEV, 32, PG).transpose(1, 0, 2, 3).reshape(N_DEV, -1, D)], axis=1)
    q = _pair_sum_small(coords, small, _rs_sibling(small), q)

    vec = _pack_vec(dbgate[0], dgamma[0], dbeta[0], dpscale[0], loss_part[0, 0])
    grad_x, l2, vecs_all = _grad_x_rs(dh, gw, dr, q, vec)
    g_in, d_in, m_in, v_in = _adam_shard("adam_w_in", q, l2, w_in[0], m_w_in[0], v_w_in[0], 256)
    g_small = [t.reshape(w.shape) for t, w in zip(_unpack_small(_sum_small(q, l2)),
                                                  (w_out, w_proj_attn, w_proj_pool, w_pool))]
    small = (g_small,) + _adam_whole("adam_small", g_small, (w_out, w_proj_attn, w_proj_pool, w_pool),
                                     (m_w_out, m_w_proj_attn, m_w_proj_pool, m_w_pool),
                                     (v_w_out, v_w_proj_attn, v_w_proj_pool, v_w_pool))

    *g_vec, loss = _sum_replicated(vecs_all)
    vecs = (g_vec,) + _adam_whole("adam_replicated", g_vec, (b_gate, pool_scale, ln_gamma, ln_beta),
                                  (m_b_gate, m_pool_scale, m_ln_gamma, m_ln_beta),
                                  (v_b_gate, v_pool_scale, v_ln_gamma, v_ln_beta))
    loss = loss[0, 0]

    def leaves(kind, big):
        out, pa, pp, pool = small[kind]
        bg, ps, gm, bt = vecs[kind]
        return [big[None], bg, pool, ps, pa, pp, out, gm, bt]

    return (loss, _permute_tokens(grad_x)[None], *leaves(0, g_in), *leaves(1, d_in), *leaves(2, m_in), *leaves(3, v_in))
```

```python
import functools

import jax
import jax.numpy as jnp
from jax import lax
from jax.experimental import pallas as pl
from jax.experimental.pallas import tpu as pltpu

F32 = jnp.float32
BF16 = jnp.bfloat16

S = 4096
D = 2048
NW = 16384
AW = 1024
HD = 128
NH = 8
QB = 128
DILATIONS = (1, 4, 16)
POOL_WINDOWS = (2, 4, 8, 16)
PG = 256
N_DEV = 8
COL_Q, COL_K, COL_V = 0, 3 * AW, 6 * AW
COL_ZA, COL_U, COL_ZP, COL_G = 9 * AW, 10 * AW, 11 * AW, 12 * AW
DH_Z, DH_G = COL_ZA, COL_G
ALPHA = 2.0 ** 0.25
LN_EPS = 1e-5
NEG_INF = -1e30
LR, B1, B2, EPS, WD, STEP = 0.001, 0.9, 0.999, 1e-08, 0.01, 10
R_IN, R_OUT, R_PA, R_PP, R_PL = 0, 2048, 2304, 2432, 2560
R_ALL = 2576
R_SMALL = R_ALL - R_OUT
VMEM_LIMIT = 56 * 1024 * 1024
MESH = pl.DeviceIdType.MESH
ANY = pl.BlockSpec(memory_space=pl.ANY)


def _cparams(n_axes):
    return pltpu.CompilerParams(dimension_semantics=("arbitrary",) * n_axes, vmem_limit_bytes=VMEM_LIMIT)


def _sigmoid(z):
    return 0.5 * jnp.tanh(0.5 * z) + 0.5


def _nt(a, b):
    return lax.dot_general(a, b, (((1,), (1,)), ((), ())), preferred_element_type=F32)


def _tn(a, b):
    return lax.dot_general(a, b, (((0,), (0,)), ((), ())), preferred_element_type=F32)


def _nn(a, b):
    return jnp.dot(a, b, preferred_element_type=F32)


def _lin(x, y, c):
    return 4 * x + 2 * y + c


def _flip(v, f):
    return 1 - v if f else v


CHIP_FLIPS = ((0, 0), (1, 0), (0, 1), (1, 1))


AG_PIECES = ((pl.ds(R_IN, D), pl.ds(0, 1024)), (pl.ds(R_IN, D), pl.ds(1024, 1024)),
             (pl.ds(R_OUT, R_PA - R_OUT), pl.ds(0, D)), (pl.ds(R_PA, R_ALL - R_PA), pl.ds(0, D)))
N_PIECES = len(AG_PIECES)
SIB, TO_X, TO_Y, ON, PASS_X, PASS_Y, PASS_D = range(7)
AG_TILES = ((0, 0), (0, 1), (1, 0), (1, 1), (2, 0), (4, 0), (3, 0), (5, 0),
            (2, 1), (4, 1), (3, 1), (5, 1), (6, 0), (6, 1), (7, 0), (7, 1))
W, G = "wait", "go"
AG_STEPS = {
    2: [(W, SIB, 0)], 3: [(W, SIB, 1)],
    4: [(W, TO_X, 0), (G, ON, 0), (G, PASS_X, 0)], 5: [(W, TO_Y, 0), (G, PASS_Y, 0)],
    6: [(W, PASS_X, 0)], 7: [(W, PASS_Y, 0)],
    8: [(W, TO_X, 1), (G, PASS_X, 1), (W, TO_Y, 1), (G, ON, 1), (G, PASS_Y, 1),
        (G, TO_X, 2), (G, TO_X, 3), (G, TO_Y, 2), (G, TO_Y, 3)],
    10: [(W, PASS_X, 1)], 11: [(W, PASS_Y, 1)],
    12: [(W, ON, 0), (G, PASS_D, 0)], 13: [(W, ON, 1), (G, PASS_D, 1)],
    14: [(W, PASS_D, 0), (W, TO_X, 2), (G, ON, 2), (G, PASS_X, 2), (W, TO_X, 3), (G, PASS_X, 3),
         (W, TO_Y, 2), (G, PASS_Y, 2), (W, TO_Y, 3), (G, ON, 3), (G, PASS_Y, 3)],
    15: [(W, PASS_D, 1)],
}
AG_LAST = [(W, SIB, 2), (W, SIB, 3), (W, ON, 2), (G, PASS_D, 2), (W, ON, 3), (G, PASS_D, 3),
           (W, PASS_X, 2), (W, PASS_X, 3), (W, PASS_Y, 2), (W, PASS_Y, 3), (W, PASS_D, 2), (W, PASS_D, 3)]


def _arrival_order(x, y, c):
    chips = [(x, y), (1 - x, y), (x, 1 - y), (1 - x, 1 - y)]
    return jnp.stack([_lin(px, py, pc) for px, py in chips for pc in (c, 1 - c)]).astype(jnp.int32)


def _ag_proj(order, x2, wbf, spack):
    tm, tn = 1024, 1024
    nrow, ntile = S // tm, len(AG_TILES)
    slabs = jnp.stack([order[pos] for pos, _ in AG_TILES])
    cols = jnp.stack([2 * order[pos] + half for pos, half in AG_TILES])

    def body(cols_ref, slabs_ref, x_ref, w_ref, s_ref, h_ref, gw_ref, xt_ref, xbv, wbuf, wsem, send_sems, recv_sems,
             local_sems):
        t, i = pl.program_id(0), pl.program_id(1)
        x, y, c = lax.axis_index("x"), lax.axis_index("y"), lax.axis_index("c")
        me = _lin(x, y, c)
        dev = {"sib": (x, y, 1 - c), "x": (1 - x, y, c), "y": (x, 1 - y, c), "d": (1 - x, 1 - y, c)}

        def slab_of(name, other_core=False):
            px, py, pc = dev[name]
            return _lin(px, py, 1 - pc if other_core else pc)

        def own(piece):
            rows, colz = AG_PIECES[piece]
            return w_ref.at[:, colz] if piece < 2 else s_ref.at[pl.ds(rows.start - R_OUT, rows.size)]

        def rdma(slab, kind, piece, to, from_own=False):
            k = kind * N_PIECES + piece
            there = gw_ref.at[(slab, *AG_PIECES[piece])]
            return pltpu.make_async_remote_copy(
                src_ref=own(piece) if from_own else there, dst_ref=there,
                send_sem=send_sems.at[k], recv_sem=recv_sems.at[k], device_id=dev[to], device_id_type=MESH)

        def mine(kind, piece):
            if kind in (SIB, TO_X, TO_Y):
                return rdma(me, kind, piece, ("sib", "x", "y")[kind], from_own=True)
            if kind == ON:
                frm, to = ("x", "y") if piece % 2 == 0 else ("y", "x")
                return rdma(slab_of(frm), kind, piece, to)
            return rdma(slab_of({PASS_X: "x", PASS_Y: "y", PASS_D: "d"}[kind]), kind, piece, "sib")

        def landing(kind, piece):
            slab = {SIB: slab_of("sib"), TO_X: slab_of("x"), TO_Y: slab_of("y"), ON: slab_of("d"),
                    PASS_X: slab_of("x", True), PASS_Y: slab_of("y", True), PASS_D: slab_of("d", True)}[kind]
            return rdma(slab, kind, piece, "sib")

        def run(steps):
            for what, kind, piece in steps:
                if what == W:
                    landing(kind, piece).wait_recv()
                else:
                    mine(kind, piece).start()

        local = [pltpu.make_async_copy(w_ref, gw_ref.at[me, pl.ds(R_IN, D)], local_sems.at[0]),
                 pltpu.make_async_copy(s_ref, gw_ref.at[me, pl.ds(R_OUT, R_SMALL)], local_sems.at[1])]

        def fetch(slab, half, slot):
            src = own(half) if slab is None else gw_ref.at[(slab, *AG_PIECES[half])]
            return pltpu.make_async_copy(src, wbuf.at[slot], wsem.at[slot])

        @pl.when((t == 0) & (i == 0))
        def _():
            for cp in local:
                cp.start()
            run([(G, kind, piece) for piece in (0, 1) for kind in (TO_X, TO_Y, SIB)] + [(G, SIB, 2), (G, SIB, 3)])
            first = fetch(None, 0, 0)
            first.start()
            first.wait()

        for nxt in range(1, ntile):
            @pl.when((t == nxt - 1) & (i == nrow - 1))
            def _(nxt=nxt):
                run(AG_STEPS.get(nxt, []))
                fetch(None if AG_TILES[nxt][0] == 0 else slabs_ref[nxt], AG_TILES[nxt][1], nxt % 2).start()

        @pl.when(t == 0)
        def _():
            xbv[i] = x_ref[...].astype(BF16)
            xt_ref[...] = x_ref[...].T.astype(BF16)

        for slot in (0, 1):
            @pl.when(t % 2 == slot)
            def _(slot=slot):
                @pl.when((i == 0) & (t > 0))
                def _():
                    fetch(None, 0, slot).wait()
                h_ref[...] = _nn(xbv[i], wbuf[slot]).astype(h_ref.dtype)

        @pl.when((t == ntile - 1) & (i == nrow - 1))
        def _():
            run(AG_LAST)
            for kind in range(7):
                for piece in range(N_PIECES):
                    mine(kind, piece).wait_send()
            for cp in local:
                cp.wait()

    n_sem = 7 * N_PIECES
    grid_spec = pltpu.PrefetchScalarGridSpec(
        num_scalar_prefetch=2, grid=(ntile, nrow),
        in_specs=[pl.BlockSpec((tm, D), lambda t, i, cols, slabs: (jnp.where(t == 0, i, nrow - 1), 0)), ANY, ANY],
        out_specs=[pl.BlockSpec((tm, tn), lambda t, i, cols, slabs: (i, cols[t])), ANY,
                   pl.BlockSpec((D, tm), lambda t, i, cols, slabs: (0, jnp.where(t == 0, i, nrow - 1)))],
        scratch_shapes=[pltpu.VMEM((nrow, tm, D), BF16), pltpu.VMEM((2, D, tn), BF16), pltpu.SemaphoreType.DMA((2,)),
                        pltpu.SemaphoreType.DMA((n_sem,)), pltpu.SemaphoreType.DMA((n_sem,)),
                        pltpu.SemaphoreType.DMA((2,))])
    return pl.pallas_call(
        body, name="ag_proj", grid_spec=grid_spec,
        out_shape=[jax.ShapeDtypeStruct((S, NW), BF16), jax.ShapeDtypeStruct((N_DEV, R_ALL, D), BF16),
                   jax.ShapeDtypeStruct((D, S), BF16)],
        compiler_params=_cparams(2))(cols, slabs, x2, wbf, spack)


def _rs_sibling(p):
    n = len(CHIP_FLIPS)

    def body(p_ref, l_ref, send_sems, recv_sems):
        x, y, c = lax.axis_index("x"), lax.axis_index("y"), lax.axis_index("c")
        copies = [pltpu.make_async_remote_copy(
            src_ref=p_ref.at[_lin(_flip(x, fx), _flip(y, fy), 1 - c)], dst_ref=l_ref.at[k],
            send_sem=send_sems.at[k], recv_sem=recv_sems.at[k], device_id=(x, y, 1 - c), device_id_type=MESH)
            for k, (fx, fy) in enumerate(CHIP_FLIPS)]
        for cp in copies:
            cp.start()
        for cp in copies:
            cp.wait_recv()
        for cp in copies:
            cp.wait_send()

    return pl.pallas_call(
        body, name="rs_sibling", out_shape=jax.ShapeDtypeStruct((n,) + p.shape[1:], p.dtype),
        in_specs=[ANY], out_specs=ANY,
        scratch_shapes=[pltpu.SemaphoreType.DMA((n,)), pltpu.SemaphoreType.DMA((n,))])(p)


def _pair_sum_small(coords, p, l1, q):
    def body(crd, p_ref, l_ref, _, q_ref, buf, sem):
        k = pl.program_id(0)
        buf[...] = (p_ref[...].astype(F32) + l_ref[...].astype(F32)).astype(buf.dtype)
        out = pltpu.make_async_copy(buf, q_ref.at[k, pl.ds(R_OUT, R_SMALL)], sem)
        out.start()
        out.wait()

    def p_map(k, crd):
        fx, fy = k % 2, k // 2
        px = crd[0] + fx - 2 * fx * crd[0]
        py = crd[1] + fy - 2 * fy * crd[1]
        return (_lin(px, py, crd[2]), 0, 0)

    grid_spec = pltpu.PrefetchScalarGridSpec(
        num_scalar_prefetch=1, grid=(4,),
        in_specs=[pl.BlockSpec((None, R_SMALL, D), p_map),
                  pl.BlockSpec((None, R_SMALL, D), lambda k, crd: (k, 0, 0)), ANY],
        out_specs=ANY,
        scratch_shapes=[pltpu.VMEM((R_SMALL, D), BF16), pltpu.SemaphoreType.DMA(())])
    return pl.pallas_call(body, name="pair_sum_small", grid_spec=grid_spec,
                          out_shape=jax.ShapeDtypeStruct(q.shape, q.dtype), input_output_aliases={3: 0},
                          compiler_params=_cparams(1))(coords, p, l1, q)


def _h_block(k):
    return jnp.where(k < 9, (k % 3) * 3 + k // 3, k)


RS_PIECES = (pl.ds(0, 1280), pl.ds(1280, R_ALL - 1280))
RS_ROWS = (1280, R_ALL - 1280)
RS_CHUNKS = ((320,) * 4, (432,) * 3)
RS_MERGE_STEP = 2


def _grad_x_rs(dh, g, dr, q, vec):
    others = [(fx, fy, fc) for fx in (0, 1) for fy in (0, 1) for fc in (0, 1) if (fx, fy, fc) != (0, 0, 0)]
    tm, tk = 1024, 1024
    ni, nk = S // tm, NW // tk
    rmax = max(RS_ROWS)
    cmax = max(max(c) for c in RS_CHUNKS)

    def body(dh_ref, w_ref, dr_ref, q_ref, vec_ref, o_ref, l2_ref, ld_ref, mg_ref, all_ref, va, vb,
             send_sems, recv_sems, sems):
        i, k = pl.program_id(0), pl.program_id(1)
        x, y, c = lax.axis_index("x"), lax.axis_index("y"), lax.axis_index("c")
        nbr = ((1 - x, y, c), (x, 1 - y, c))

        def vec_copy(n, sender):
            sx, sy, sc = sender
            fx, fy, fc = others[n]
            return pltpu.make_async_remote_copy(
                src_ref=vec_ref, dst_ref=all_ref.at[_lin(sx, sy, sc)], send_sem=send_sems.at[6 + n],
                recv_sem=recv_sems.at[6 + n], device_id=(_flip(sx, fx), _flip(sy, fy), _flip(sc, fc)),
                device_id_type=MESH)

        vec_own = pltpu.make_async_copy(vec_ref, all_ref.at[_lin(x, y, c)], sems.at[3])

        def rows(ref, piece):
            return ref.at[piece, pl.ds(0, RS_ROWS[piece])]

        copies = (
            (q_ref.at[3, RS_PIECES[0]], rows(ld_ref, 0), 0),
            (q_ref.at[3, RS_PIECES[1]], rows(ld_ref, 1), 1),
            (q_ref.at[1, RS_PIECES[0]], l2_ref.at[0, RS_PIECES[0]], 0),
            (q_ref.at[2, RS_PIECES[1]], l2_ref.at[1, RS_PIECES[1]], 1),
            (rows(mg_ref, 0), l2_ref.at[1, RS_PIECES[0]], 1),
            (rows(mg_ref, 1), l2_ref.at[0, RS_PIECES[1]], 0),
        )

        def copy(n):
            src, dst, axis = copies[n]
            return pltpu.make_async_remote_copy(src_ref=src, dst_ref=dst, send_sem=send_sems.at[n],
                                                recv_sem=recv_sems.at[n], device_id=nbr[axis], device_id_type=MESH)

        def merge(piece, mine):
            start = 0
            for n_rows in RS_CHUNKS[piece]:
                own = pltpu.make_async_copy(q_ref.at[mine, pl.ds(RS_PIECES[piece].start + start, n_rows)],
                                            va.at[pl.ds(0, n_rows)], sems.at[0])
                got = pltpu.make_async_copy(ld_ref.at[piece, pl.ds(start, n_rows)], vb.at[pl.ds(0, n_rows)], sems.at[1])
                own.start()
                got.start()
                own.wait()
                got.wait()
                va[pl.ds(0, n_rows)] = (va[pl.ds(0, n_rows)].astype(F32)
                                        + vb[pl.ds(0, n_rows)].astype(F32)).astype(va.dtype)
                out = pltpu.make_async_copy(va.at[pl.ds(0, n_rows)], mg_ref.at[piece, pl.ds(start, n_rows)], sems.at[2])
                out.start()
                out.wait()
                start += n_rows

        @pl.when((i == 0) & (k == 0))
        def _():
            for n in range(4):
                copy(n).start()
            vec_own.start()
            for n in range(len(others)):
                vec_copy(n, (x, y, c)).start()

        @pl.when((i == RS_MERGE_STEP) & (k == 0))
        def _():
            copy(0).wait_recv()
            merge(0, 2)
            copy(4).start()
            copy(1).wait_recv()
            merge(1, 1)
            copy(5).start()

        @pl.when(k == 0)
        def _():
            o_ref[...] = ALPHA * dr_ref[...]

        o_ref[...] += _nt(dh_ref[...], w_ref[...])

        @pl.when((i == ni - 1) & (k == nk - 1))
        def _():
            for n in range(2, 6):
                copy(n).wait_recv()
            for n in range(6):
                copy(n).wait_send()
            for n, (fx, fy, fc) in enumerate(others):
                vec_copy(n, (_flip(x, fx), _flip(y, fy), _flip(c, fc))).wait_recv()
                vec_copy(n, (x, y, c)).wait_send()
            vec_own.wait()

    slab = q.shape[1:]
    out = pl.pallas_call(
        body, name="grad_x_rs", grid=(ni, nk),
        in_specs=[pl.BlockSpec((tm, tk), lambda i, k: (i, k)),
                  pl.BlockSpec((None, D, tk), lambda i, k: (_h_block(k) // 2, 0, _h_block(k) % 2)),
                  pl.BlockSpec((tm, D), lambda i, k: (i, 0)), ANY, ANY],
        out_specs=[pl.BlockSpec((tm, D), lambda i, k: (i, 0)), ANY, ANY, ANY, ANY],
        out_shape=[jax.ShapeDtypeStruct((S, D), F32), jax.ShapeDtypeStruct((2,) + slab, q.dtype),
                   jax.ShapeDtypeStruct((2, rmax, slab[1]), q.dtype), jax.ShapeDtypeStruct((2, rmax, slab[1]), q.dtype),
                   jax.ShapeDtypeStruct((N_DEV,) + vec.shape, vec.dtype)],
        scratch_shapes=[pltpu.VMEM((cmax, slab[1]), q.dtype), pltpu.VMEM((cmax, slab[1]), q.dtype),
                        pltpu.SemaphoreType.DMA((6 + len(others),)), pltpu.SemaphoreType.DMA((6 + len(others),)),
                        pltpu.SemaphoreType.DMA((4,))],
        compiler_params=_cparams(2))(dh, g, dr, q, vec)
    return out[0], out[1], out[4]


GW_TN = 512
GW_PARTS = D // GW_TN


def _rs_columns(x, y, c):
    per_h = AW // GW_TN
    out = []
    for core in (1 - c, c):
        for fx, fy in CHIP_FLIPS:
            for part in range(GW_PARTS):
                h_block = 2 * _lin(_flip(x, fx), _flip(y, fy), core) + part // per_h
                out.append(_h_block(h_block) * per_h + part % per_h)
    return jnp.stack(out).astype(jnp.int32)


def _grad_w_in_rs(cols, xt, dh):
    n_tile = 4 * GW_PARTS

    def body(cols_ref, a_ref, b_ref, q_ref, l1_ref, stage, landed, send_sems, recv_sems, sem):
        t = pl.program_id(0)
        sib = (lax.axis_index("x"), lax.axis_index("y"), 1 - lax.axis_index("c"))

        def there(n):
            return l1_ref.at[n // GW_PARTS, :, pl.ds((n % GW_PARTS) * GW_TN, GW_TN)]

        def send(n):
            return pltpu.make_async_remote_copy(src_ref=stage.at[n % 2], dst_ref=there(n), send_sem=send_sems.at[n],
                                                recv_sem=recv_sems.at[n], device_id=sib, device_id_type=MESH)

        def fetch(n):
            return pltpu.make_async_copy(there(n), landed, sem)

        for n in range(n_tile):
            @pl.when(t == n_tile + n)
            def _(n=n):
                if n == 0:
                    send(n_tile - 2).wait_send()
                    send(n_tile - 1).wait_send()
                send(n).wait_recv()
                fetch(n).start()

        part = _nn(a_ref[...], b_ref[...])

        for n in range(n_tile):
            @pl.when(t == n)
            def _(n=n):
                if n >= 2:
                    send(n - 2).wait_send()
                stage[n % 2] = part.astype(stage.dtype)
                send(n).start()

            @pl.when(t == n_tile + n)
            def _(n=n):
                fetch(n).wait()
                q_ref[...] = (part + landed[...].astype(F32)).astype(q_ref.dtype)

    mine = lambda t: jnp.maximum(t - n_tile, 0)
    grid_spec = pltpu.PrefetchScalarGridSpec(
        num_scalar_prefetch=1, grid=(2 * n_tile,),
        in_specs=[pl.BlockSpec((D, S), lambda t, cols: (0, 0), pipeline_mode=pl.Buffered(1)),
                  pl.BlockSpec((S, GW_TN), lambda t, cols: (0, cols[t]))],
        out_specs=[pl.BlockSpec((None, D, GW_TN), lambda t, cols: (mine(t) // GW_PARTS, 0, mine(t) % GW_PARTS)), ANY],
        scratch_shapes=[pltpu.VMEM((2, D, GW_TN), BF16), pltpu.VMEM((D, GW_TN), BF16),
                        pltpu.SemaphoreType.DMA((n_tile,)), pltpu.SemaphoreType.DMA((n_tile,)),
                        pltpu.SemaphoreType.DMA(())])
    q, _ = pl.pallas_call(
        body, name="grad_w_in_rs", grid_spec=grid_spec,
        out_shape=[jax.ShapeDtypeStruct((4, R_ALL, D), BF16), jax.ShapeDtypeStruct((4, D, D), BF16)],
        compiler_params=_cparams(1))(cols, xt, dh)
    return q


def _grad_w(name, at, b):
    m, n_all = at.shape[0], b.shape[1]

    def body(a_ref, b_ref, o_ref):
        o_ref[...] = _nn(a_ref[...], b_ref[...]).astype(o_ref.dtype)

    return pl.pallas_call(
        body, name=name, grid=(n_all // GW_TN,),
        in_specs=[pl.BlockSpec((m, S), lambda n: (0, 0), pipeline_mode=pl.Buffered(1)),
                  pl.BlockSpec((S, GW_TN), lambda n: (0, n))],
        out_specs=pl.BlockSpec((m, GW_TN), lambda n: (0, n)), out_shape=jax.ShapeDtypeStruct((m, n_all), BF16),
        compiler_params=_cparams(1))(at, b)


NR = 16
TI = 16
TM = NR * TI
NT = S // TM
ATT_QB = (256, 128, 256)
ATT_NB = (16, 8, 1)
ATT_BLOCKS = (16, 32, 16)


def _permute_tokens(a):
    return a.reshape(NT, TI, NR, a.shape[-1]).transpose(0, 2, 1, 3).reshape(a.shape)


def _attn_shape(g, c):
    if g == 0:
        return (S, c)
    if g == 1:
        return (NT, 4, 4, TI, c)
    return (NT, NR, TI, c)


def _attn_view(g, a):
    return a.reshape(_attn_shape(g, a.shape[-1]))


def _attn_spec(g, width, col, blk):
    if g == 0:
        return pl.BlockSpec((TM, width), lambda b: (blk(b), col))
    if g == 1:
        return pl.BlockSpec((2, 4, None, TI, width), lambda b: (blk(b) % 8, 0, blk(b) // 8, 0, col))
    return pl.BlockSpec((NT, None, TI, width), lambda b: (0, blk(b), 0, col))


def _pieces(g):
    if g == 1:
        return [(t, m) for t in range(2) for m in range(4)]
    return [(t,) for t in range(NT)]


def _get(g, ref, sl):
    if g == 0:
        return ref[:, sl]
    return jnp.concatenate([ref[(*p, slice(None), sl)] for p in _pieces(g)], axis=0)


def _put(g, ref, sl, val):
    if g == 0:
        ref[:, sl] = val
    else:
        for n, p in enumerate(_pieces(g)):
            ref[(*p, slice(None), sl)] = val[TI * n:TI * (n + 1)]


def _block_pos(g, a):
    if g == 0:
        return 16 * (a % 16) + a // 16
    if g == 1:
        return 64 * (a // 64) + 4 * (a % 16) + (a // 16) % 4
    return a


def _attn_mask(g, n):
    qb = ATT_QB[g]
    if ATT_NB[g] == 1:
        qa = lax.broadcasted_iota(jnp.int32, (qb, qb), 0)
        kc = lax.broadcasted_iota(jnp.int32, (qb, qb), 1)
        dist = _block_pos(g, qa) - _block_pos(g, kc)
        return (dist >= 0) & (dist <= QB)
    qa = lax.broadcasted_iota(jnp.int32, (qb, 2 * qb), 0)
    kc = lax.broadcasted_iota(jnp.int32, (qb, 2 * qb), 1)
    cur = kc >= qb
    dist = _block_pos(g, qa) - _block_pos(g, kc % qb) + jnp.where(cur, 0, qb)
    return (dist >= 0) & (dist <= QB) & (cur | (n > 0))


def _keys(g, prev_ref, cur_ref, sl):
    if ATT_NB[g] == 1:
        return _get(g, cur_ref, sl)
    return jnp.concatenate([_get(g, prev_ref, sl), _get(g, cur_ref, sl)], axis=0)


def _qkv_specs(g, clamp):
    cur = lambda col: _attn_spec(g, AW, col, clamp)
    prev = lambda col: _attn_spec(g, AW, col, lambda b: jnp.maximum(clamp(b) - 1, 0))
    qc, kc, vc = (c // AW + g for c in (COL_Q, COL_K, COL_V))
    return [cur(qc), cur(kc), prev(kc), cur(vc), prev(vc)]


def _attn_fwd(g, h):
    scale = HD ** -0.5
    hv = _attn_view(g, h)

    def body(q_ref, kc_ref, kp_ref, vc_ref, vp_ref, o_ref, l_ref):
        valid = _attn_mask(g, pl.program_id(0) % ATT_NB[g])
        for hh in range(NH):
            sl = slice(hh * HD, (hh + 1) * HD)
            kh, vh = _keys(g, kp_ref, kc_ref, sl), _keys(g, vp_ref, vc_ref, sl)
            s = jnp.where(valid, _nt(_get(g, q_ref, sl), kh) * scale, NEG_INF)
            m = jnp.max(s, axis=-1, keepdims=True)
            e = jnp.exp(s - m)
            den = jnp.sum(e, axis=-1, keepdims=True)
            _put(g, o_ref, sl, (_nn(e.astype(BF16), vh) * (1.0 / den)).astype(o_ref.dtype))
            _put(g, l_ref, slice(hh, hh + 1), m + jnp.log(den))

    same = lambda b: b
    o, lse = pl.pallas_call(
        body, name=f"attn_fwd_{g}", grid=(ATT_BLOCKS[g],),
        in_specs=_qkv_specs(g, same),
        out_specs=[_attn_spec(g, AW, 0, same), _attn_spec(g, NH, 0, same)],
        out_shape=[jax.ShapeDtypeStruct(_attn_shape(g, AW), BF16), jax.ShapeDtypeStruct(_attn_shape(g, NH), F32)],
        compiler_params=_cparams(1))(hv, hv, hv, hv, hv)
    return o.reshape(S, AW), lse.reshape(S, NH)


def _attn_bwd(g, dh, h, do, lse, delta):
    scale = HD ** -0.5
    qb = ATT_QB[g]
    carried = ATT_NB[g] > 1
    last = ATT_BLOCKS[g] - 1
    clamp = lambda b: jnp.minimum(b, last)
    behind = lambda b: jnp.maximum(b - 1, 0)
    hv = _attn_view(g, h)

    def body(q_ref, kc_ref, kp_ref, vc_ref, vp_ref, do_ref, l_ref, dl_ref, _, dh_ref, *carry):
        b = pl.program_id(0)

        def write(col, val):
            _put(g, dh_ref, slice(col, col + HD), val.astype(dh_ref.dtype))

        def block():
            valid = _attn_mask(g, b % ATT_NB[g])
            for hh in range(NH):
                sl = slice(hh * HD, (hh + 1) * HD)
                one = slice(hh, hh + 1)
                qh, doh = _get(g, q_ref, sl), _get(g, do_ref, sl)
                kh, vh = _keys(g, kp_ref, kc_ref, sl), _keys(g, vp_ref, vc_ref, sl)
                s = _nt(qh, kh) * scale
                p = jnp.where(valid, jnp.exp(s - _get(g, l_ref, one)), 0.0)
                ds = p * (_nt(doh, vh) - _get(g, dl_ref, one))
                dsb = (ds * scale).astype(BF16)
                dq = _nn(dsb, kh)
                dk2 = _tn(dsb, qh)
                dv2 = _tn(p.astype(BF16), doh)
                if carried:
                    cq_ref, ck_ref, cv_ref = carry
                    write(hh * HD, cq_ref[:, sl])
                    write(AW + hh * HD, ck_ref[:, sl] + dk2[:qb])
                    write(2 * AW + hh * HD, cv_ref[:, sl] + dv2[:qb])
                    cq_ref[:, sl] = dq
                    ck_ref[:, sl] = dk2[qb:]
                    cv_ref[:, sl] = dv2[qb:]
                else:
                    write(hh * HD, dq)
                    write(AW + hh * HD, dk2)
                    write(2 * AW + hh * HD, dv2)

        if not carried:
            block()
            return

        @pl.when(b == 0)
        def _():
            for ref in carry:
                ref[...] = jnp.zeros_like(ref)

        pl.when(b <= last)(block)

        @pl.when(b > last)
        def _():
            for hh in range(NH):
                for n, ref in enumerate(carry):
                    write(n * AW + hh * HD, ref[:, hh * HD:(hh + 1) * HD])

    out = pl.pallas_call(
        body, name=f"attn_bwd_{g}", grid=(ATT_BLOCKS[g] + carried,),
        in_specs=_qkv_specs(g, clamp) + [_attn_spec(g, AW, 0, clamp), _attn_spec(g, NH, 0, clamp),
                                         _attn_spec(g, NH, 0, clamp), ANY],
        out_specs=_attn_spec(g, 3 * AW, g, behind if carried else clamp),
        out_shape=jax.ShapeDtypeStruct(_attn_shape(g, NW), BF16),
        input_output_aliases={8: 0},
        scratch_shapes=[pltpu.VMEM((qb, AW), F32)] * (3 if carried else 0),
        compiler_params=_cparams(1))(hv, hv, hv, hv, hv, _attn_view(g, do), _attn_view(g, lse), _attn_view(g, delta),
                                     _attn_view(g, dh))
    return out.reshape(S, NW)


def _group_weights(l0, l1, l2):
    m = jnp.maximum(jnp.maximum(l0, l1), l2)
    e0, e1, e2 = jnp.exp(l0 - m), jnp.exp(l1 - m), jnp.exp(l2 - m)
    inv = 1.0 / (e0 + e1 + e2)
    return e0 * inv, e1 * inv, e2 * inv


def _residue(ref, r, sl):
    return ref[r * TI:(r + 1) * TI, sl].astype(F32)


def _total(parts):
    return functools.reduce(lambda x, y: x + y, parts)


def _pool_tokens(up_ref, uc_ref, p_ref, tile):
    j0 = lax.broadcasted_iota(jnp.int32, (TI, 1), 0) == 0
    first = (tile == 0) & j0
    for r in range(NR):
        out = []
        for g, w in enumerate(POOL_WINDOWS):
            sl = slice(g * PG, (g + 1) * PG)
            own = _residue(uc_ref, r, sl)
            acc = _total([own] + [_residue(uc_ref, r - k, sl) for k in range(1, min(r, w - 1) + 1)])
            wrapped = [NR + r - k for k in range(r + 1, w)]
            if wrapped:
                wc = _total([_residue(uc_ref, q, sl) for q in wrapped])
                wp = jnp.where(tile > 0, _total([_residue(up_ref, q, sl) for q in wrapped]), 0.0)
                acc = acc + jnp.where(j0, pltpu.roll(wp, 1, 0), pltpu.roll(wc, 1, 0))
            out.append(acc * jnp.where(first, 1.0 / min(r + 1, w), 1.0 / w) - own)
        p_ref[r * TI:(r + 1) * TI, :] = jnp.concatenate(out, axis=1).astype(p_ref.dtype)


def _pool_tokens_bwd(dp, nxt_ref, du_ref, tile):
    ji = lax.broadcasted_iota(jnp.int32, (TI, 1), 0)
    first = (tile == 0) & (ji == 0)
    piece = lambda g, r: dp[g][r * TI:(r + 1) * TI]
    dpc = [[piece(g, r) * jnp.where(first, 1.0 / min(r + 1, w), 1.0 / w) for r in range(NR)]
           for g, w in enumerate(POOL_WINDOWS)]
    for r in range(NR):
        out = []
        for g, w in enumerate(POOL_WINDOWS):
            sl = slice(g * PG, (g + 1) * PG)
            acc = _total([dpc[g][r + k] for k in range(w) if r + k < NR])
            wrapped = [r + k - NR for k in range(1, w) if r + k >= NR]
            if wrapped:
                wc = _total([dpc[g][q] for q in wrapped])
                wn = _total([nxt_ref[q * TI:(q + 1) * TI, sl] for q in wrapped])
                acc = acc + jnp.where(ji == TI - 1, pltpu.roll(wn, TI - 1, 0), pltpu.roll(wc, TI - 1, 0))
            out.append(acc - piece(g, r))
        du_ref[r * TI:(r + 1) * TI, :] = jnp.concatenate(out, axis=1).astype(du_ref.dtype)
    for r in range(NR):
        nxt_ref[r * TI:(r + 1) * TI, :] = jnp.concatenate([dpc[g][r] for g in range(len(POOL_WINDOWS))], axis=1)


def _pool_linear(pb, wpool_ref):
    return jnp.concatenate([_nn(pb[:, g * PG:(g + 1) * PG], wpool_ref[g]) for g in range(len(POOL_WINDOWS))], axis=1)


def _tok(width, col=0, rev=False):
    if rev:
        return pl.BlockSpec((TM, width), lambda i: (NT - 1 - i, col))
    return pl.BlockSpec((TM, width), lambda i: (i, col))


def _whole(shape):
    return pl.BlockSpec(shape, lambda i: (0,) * len(shape))


def _mix_fwd(h, o, lse, wpa, wpp, wpool, pscale, bgate):
    def body(o0_ref, o1_ref, o2_ref, l0_ref, l1_ref, l2_ref, za_ref, uc_ref, up_ref, zp_ref, gp_ref,
             wpa_ref, wpp_ref, wpool_ref, ps_ref, bg_ref,
             mg_ref, a_ref, b_ref, p_ref, yat_ref, ypt_ref, mgt_ref, ya_ref, yp_ref):
        i = pl.program_id(0)
        w0, w1, w2 = _group_weights(l0_ref[...], l1_ref[...], l2_ref[...])
        za = za_ref[...].astype(F32)
        silu_a = za * _sigmoid(za)
        for hh in range(NH):
            sl = slice(hh * HD, (hh + 1) * HD)
            c = slice(hh, hh + 1)
            oh = (w0[:, c] * o0_ref[:, sl].astype(F32) + w1[:, c] * o1_ref[:, sl].astype(F32)
                  + w2[:, c] * o2_ref[:, sl].astype(F32))
            ya = oh * silu_a[:, sl]
            ya_ref[:, sl] = ya.astype(BF16)
            yat_ref[sl, :] = ya.T.astype(BF16)
        _pool_tokens(up_ref, uc_ref, p_ref, i)
        zp = zp_ref[...].astype(F32)
        yp = _pool_linear(p_ref[...], wpool_ref) * ps_ref[...] * (zp * _sigmoid(zp))
        yp_ref[...] = yp.astype(BF16)
        ypt_ref[...] = yp.T.astype(BF16)
        a = _nn(ya_ref[...], wpa_ref[...])
        b = _nn(yp_ref[...], wpp_ref[...])
        a_ref[...] = a.astype(BF16)
        b_ref[...] = b.astype(BF16)
        gates = _sigmoid(gp_ref[...].astype(F32) + bg_ref[...])
        mg = gates[:, :D] * a + gates[:, D:] * b
        mg_ref[...] = mg.astype(BF16)
        mgt_ref[...] = mg.T.astype(BF16)

    u_prev = pl.BlockSpec((TM, AW), lambda i: (jnp.maximum(i - 1, 0), COL_U // AW))
    across = lambda width: pl.BlockSpec((width, TM), lambda i: (0, i))
    return pl.pallas_call(
        body, name="mix_fwd", grid=(NT,),
        in_specs=[_tok(AW)] * 3 + [_tok(NH)] * 3
        + [_tok(AW, COL_ZA // AW), _tok(AW, COL_U // AW), u_prev, _tok(AW, COL_ZP // AW), _tok(2 * D, COL_G // (2 * D))]
        + [_whole((AW, D)), _whole((AW, D)), _whole((4, PG, PG)), _whole((1, AW)), _whole((1, 2 * D))],
        out_specs=[_tok(D), _tok(D), _tok(D), _tok(AW), across(AW), across(AW), across(D)],
        out_shape=[jax.ShapeDtypeStruct((S, D), BF16)] * 3 + [jax.ShapeDtypeStruct((S, AW), BF16)]
        + [jax.ShapeDtypeStruct((AW, S), BF16)] * 2 + [jax.ShapeDtypeStruct((D, S), BF16)],
        scratch_shapes=[pltpu.VMEM((TM, AW), BF16), pltpu.VMEM((TM, AW), BF16)],
        compiler_params=_cparams(1))(*o, *lse, h, h, h, h, h, wpa, wpp, wpool, pscale, bgate)


def _out_ln(merged, x, target, wout, gamma, beta, a, b, h, bgate):
    def body(mg_ref, x_ref, t_ref, w_ref, g_ref, b_ref, a_ref, bb_ref, gp_ref, bg_ref,
             dr_ref, drb_ref, dgp_ref, da_ref, db_ref, loss_ref, dg_ref, dbt_ref, dbg_ref):
        i = pl.program_id(0)

        @pl.when(i == 0)
        def _():
            for ref in (loss_ref, dg_ref, dbt_ref, dbg_ref):
                ref[...] = jnp.zeros_like(ref)

        r = ALPHA * x_ref[...] + _nn(mg_ref[...], w_ref[...])
        mu = jnp.mean(r, axis=-1, keepdims=True)
        rc = r - mu
        rstd = lax.rsqrt(jnp.mean(rc * rc, axis=-1, keepdims=True) + LN_EPS)
        xhat = rc * rstd
        err = xhat * g_ref[...] + b_ref[...] - t_ref[...]
        loss_ref[...] += 0.5 * jnp.sum(jnp.mean(err * err, axis=-1, keepdims=True), axis=0, keepdims=True)
        dy = err * (1.0 / D)
        dg_ref[...] += jnp.sum(dy * xhat, axis=0, keepdims=True)
        dbt_ref[...] += jnp.sum(dy, axis=0, keepdims=True)
        dxh = dy * g_ref[...]
        dr = rstd * (dxh - jnp.mean(dxh, axis=-1, keepdims=True)
                     - xhat * jnp.mean(dxh * xhat, axis=-1, keepdims=True))
        dr_ref[...] = dr
        drb_ref[...] = dr.astype(BF16)
        dm_ = _nt(drb_ref[...], w_ref[...]).astype(BF16).astype(F32)
        gates = _sigmoid(gp_ref[...].astype(F32) + bg_ref[...])
        ga, gb = gates[:, :D], gates[:, D:]
        da_ref[...] = (dm_ * ga).astype(BF16)
        db_ref[...] = (dm_ * gb).astype(BF16)
        dgp = jnp.concatenate([dm_ * a_ref[...].astype(F32) * ga * (1.0 - ga),
                               dm_ * bb_ref[...].astype(F32) * gb * (1.0 - gb)], axis=1)
        dgp_ref[...] = dgp.astype(BF16)
        dbg_ref[...] += jnp.sum(dgp, axis=0, keepdims=True)

    w_spec = pl.BlockSpec((D, D), lambda i: (0, 0), pipeline_mode=pl.Buffered(1))
    return pl.pallas_call(
        body, name="out_ln", grid=(NT,),
        in_specs=[_tok(D), _tok(D), _tok(D), w_spec, _whole((1, D)), _whole((1, D)),
                  _tok(D), _tok(D), _tok(2 * D, COL_G // (2 * D)), _whole((1, 2 * D))],
        out_specs=[_tok(D), _tok(D), _tok(2 * D, DH_G // (2 * D)), _tok(D), _tok(D),
                   _whole((8, 128)), _whole((1, D)), _whole((1, D)), _whole((1, 2 * D))],
        out_shape=[jax.ShapeDtypeStruct((S, D), F32), jax.ShapeDtypeStruct((S, D), BF16),
                   jax.ShapeDtypeStruct((S, NW), BF16), jax.ShapeDtypeStruct((S, D), BF16),
                   jax.ShapeDtypeStruct((S, D), BF16), jax.ShapeDtypeStruct((8, 128), F32),
                   jax.ShapeDtypeStruct((1, D), F32), jax.ShapeDtypeStruct((1, D), F32),
                   jax.ShapeDtypeStruct((1, 2 * D), F32)],
        compiler_params=_cparams(1))(merged, x, target, wout, gamma, beta, a, b, h, bgate)


def _mix_bwd(dh, da, db, h, o, lse, p, wpa, wpp, wpool, pscale):
    def body(_, da_ref, db_ref, o0_ref, o1_ref, o2_ref, l0_ref, l1_ref, l2_ref, za_ref, zp_ref, p_ref,
             wpa_ref, wpp_ref, wpool_ref, ps_ref,
             dh_ref, do0_ref, do1_ref, do2_ref, dl0_ref, dl1_ref, dl2_ref, dwp_ref, dps_ref,
             nxt_ref):
        i = pl.program_id(0)
        tile = NT - 1 - i
        dza_ref, du_ref, dzp_ref = (dh_ref.at[:, pl.ds(n * AW, AW)] for n in range(3))

        @pl.when(i == 0)
        def _():
            nxt_ref[...] = jnp.zeros_like(nxt_ref)
            dwp_ref[...] = jnp.zeros_like(dwp_ref)
            dps_ref[...] = jnp.zeros_like(dps_ref)

        dya = _nt(da_ref[...], wpa_ref[...])
        w0, w1, w2 = _group_weights(l0_ref[...], l1_ref[...], l2_ref[...])
        za = za_ref[...].astype(F32)
        sig = _sigmoid(za)
        silu_a = za * sig
        dsilu_a = sig * (1.0 + za * (1.0 - sig))
        for hh in range(NH):
            sl = slice(hh * HD, (hh + 1) * HD)
            c = slice(hh, hh + 1)
            oh = (w0[:, c] * o0_ref[:, sl].astype(F32) + w1[:, c] * o1_ref[:, sl].astype(F32)
                  + w2[:, c] * o2_ref[:, sl].astype(F32))
            doh = dya[:, sl] * silu_a[:, sl]
            dza_ref[:, sl] = (dya[:, sl] * oh * dsilu_a[:, sl]).astype(BF16)
            dot_ = jnp.sum(doh * oh, axis=-1, keepdims=True)
            do0_ref[:, sl] = (w0[:, c] * doh).astype(BF16)
            do1_ref[:, sl] = (w1[:, c] * doh).astype(BF16)
            do2_ref[:, sl] = (w2[:, c] * doh).astype(BF16)
            dl0_ref[:, c] = w0[:, c] * dot_
            dl1_ref[:, c] = w1[:, c] * dot_
            dl2_ref[:, c] = w2[:, c] * dot_
        dyp = _nt(db_ref[...], wpp_ref[...])
        pb = p_ref[...]
        pw = _pool_linear(pb, wpool_ref)
        zp = zp_ref[...].astype(F32)
        sigp = _sigmoid(zp)
        dypre = dyp * (zp * sigp)
        dzp_ref[...] = (dyp * (pw * ps_ref[...]) * (sigp * (1.0 + zp * (1.0 - sigp)))).astype(BF16)
        dps_ref[...] += jnp.sum(dypre * pw, axis=0, keepdims=True)
        dpw = (dypre * ps_ref[...]).astype(BF16)
        dp = []
        for g in range(len(POOL_WINDOWS)):
            sl = slice(g * PG, (g + 1) * PG)
            dwp_ref[g] += _tn(pb[:, sl], dpw[:, sl])
            dp.append(_nt(dpw[:, sl], wpool_ref[g]))
        _pool_tokens_bwd(dp, nxt_ref, du_ref, tile)

    r = functools.partial(_tok, rev=True)
    return pl.pallas_call(
        body, name="mix_bwd", grid=(NT,),
        in_specs=[ANY, r(D), r(D)] + [r(AW)] * 3 + [r(NH)] * 3 + [r(AW, COL_ZA // AW), r(AW, COL_ZP // AW), r(AW)]
        + [_whole((AW, D)), _whole((AW, D)), _whole((4, PG, PG)), _whole((1, AW))],
        out_specs=[r(3 * AW, DH_Z // (3 * AW))] + [r(AW)] * 3 + [r(NH)] * 3 + [_whole((4, PG, PG)), _whole((1, AW))],
        out_shape=[jax.ShapeDtypeStruct((S, NW), BF16)] + [jax.ShapeDtypeStruct((S, AW), BF16)] * 3
        + [jax.ShapeDtypeStruct((S, NH), F32)] * 3
        + [jax.ShapeDtypeStruct((4, PG, PG), F32), jax.ShapeDtypeStruct((1, AW), F32)],
        input_output_aliases={0: 0},
        scratch_shapes=[pltpu.VMEM((TM, AW), F32)],
        compiler_params=_cparams(1))(dh, da, db, *o, *lse, h, h, p, wpa, wpp, wpool, pscale)


def _adamw(w, g, m, v):
    m = B1 * m + (1.0 - B1) * g
    v = B2 * v + (1.0 - B2) * jnp.square(g)
    m_hat = m / (1.0 - B1 ** STEP)
    v_hat = v / (1.0 - B2 ** STEP)
    return -LR * (m_hat / (jnp.sqrt(v_hat) + EPS) + WD * w), m, v


def _adam_shard(name, q, l2, w, m, v, tr):
    rows = w.shape[0]

    def body(q_ref, l_ref, w_ref, m_ref, v_ref, g_out, d_out, m_out, v_out):
        g = q_ref[...].astype(F32)
        for k in range(2):
            g = g + l_ref[k].astype(F32)
        g_out[...] = g
        d_out[...], m_out[...], v_out[...] = _adamw(w_ref[...], g, m_ref[...], v_ref[...])

    blk = pl.BlockSpec((tr, D), lambda i: (i, 0))
    return pl.pallas_call(
        body, name=name, grid=(rows // tr,),
        in_specs=[pl.BlockSpec((None, tr, D), lambda i: (0, i, 0)), pl.BlockSpec((2, tr, D), lambda i: (0, i, 0)),
                  blk, blk, blk],
        out_specs=[blk] * 4, out_shape=[jax.ShapeDtypeStruct((rows, D), F32)] * 4,
        compiler_params=_cparams(1))(q, l2, w, m, v)


def _sum_small(q, l2):
    def body(q_ref, l_ref, g_out, buf, sems):
        rows = pl.ds(R_OUT, R_SMALL)
        copies = [pltpu.make_async_copy(src, buf.at[n], sems.at[n])
                  for n, src in enumerate((q_ref.at[0, rows], l_ref.at[0, rows], l_ref.at[1, rows]))]
        for cp in copies:
            cp.start()
        for cp in copies:
            cp.wait()
        g_out[...] = buf[0].astype(F32) + buf[1].astype(F32) + buf[2].astype(F32)

    return pl.pallas_call(
        body, name="sum_small", in_specs=[ANY, ANY], out_shape=jax.ShapeDtypeStruct((R_SMALL, D), F32),
        scratch_shapes=[pltpu.VMEM((3, R_SMALL, D), q.dtype), pltpu.SemaphoreType.DMA((3,))],
        compiler_params=pltpu.CompilerParams(vmem_limit_bytes=VMEM_LIMIT))(q, l2)


def _adam_whole(name, grads, weights, ms, vs):
    n = len(grads)

    def body(*refs):
        ins, outs = refs[:4 * n], refs[4 * n:]
        for t in range(n):
            g, w, m, v = (ins[k * n + t][...] for k in range(4))
            outs[t][...], outs[n + t][...], outs[2 * n + t][...] = _adamw(w, g, m, v)

    out = pl.pallas_call(
        body, name=name, out_shape=[jax.ShapeDtypeStruct(w.shape, F32) for w in weights] * 3,
        compiler_params=pltpu.CompilerParams(vmem_limit_bytes=VMEM_LIMIT))(*grads, *weights, *ms, *vs)
    return out[:n], out[n:2 * n], out[2 * n:]


def _sum_replicated(gathered):
    def body(g_ref, bg_out, ps_out, gm_out, bt_out, loss_out):
        g = g_ref[0]
        for k in range(1, N_DEV):
            g = g + g_ref[k]
        bg_out[...] = jnp.concatenate([g[0:1], g[1:2]], axis=1)
        gm_out[...] = g[2:3]
        bt_out[...] = g[3:4]
        ps_out[...] = g[4:5, :AW]
        loss_out[...] = jnp.broadcast_to(g[5:6, :128], loss_out.shape)

    return pl.pallas_call(
        body, name="sum_replicated",
        out_shape=[jax.ShapeDtypeStruct(shape, F32) for shape in ((1, 2 * D), (1, AW), (1, D), (1, D), (8, 128))],
        compiler_params=pltpu.CompilerParams(vmem_limit_bytes=VMEM_LIMIT))(gathered)


def _pack_small(w_out, w_pa, w_pp, w_pool):
    return jnp.concatenate([w_out, w_pa.reshape(-1, D), w_pp.reshape(-1, D), w_pool.reshape(-1, D)], axis=0)


def _unpack_small(a):
    o = R_OUT
    return (a[:R_PA - o], a[R_PA - o:R_PP - o].reshape(AW, 256), a[R_PP - o:R_PL - o].reshape(AW, 256),
            a[R_PL - o:].reshape(4, 32, PG))


def _pack_vec(b_gate, gamma, beta, pscale, extra):
    z = jnp.zeros((D,), F32)
    return jnp.stack([b_gate[:D], b_gate[D:], gamma, beta, jnp.concatenate([pscale, z[:D - AW]]),
                      jnp.broadcast_to(extra, (D,)), z, z])


def kernel(x, w_in, b_gate, w_pool, pool_scale, w_proj_attn, w_proj_pool, w_out, ln_gamma, ln_beta, loss_target, m_w_in, m_b_gate, m_w_pool, m_pool_scale, m_w_proj_attn, m_w_proj_pool, m_w_out, m_ln_gamma, m_ln_beta, v_w_in, v_b_gate, v_w_pool, v_pool_scale, v_w_proj_attn, v_w_proj_pool, v_w_out, v_ln_gamma, v_ln_beta):
    coords = jnp.stack([lax.axis_index("x"), lax.axis_index("y"), lax.axis_index("c")]).astype(jnp.int32)
    x2, tgt = _permute_tokens(x[0]), _permute_tokens(loss_target[0])

    spack = _pack_small(w_out[0], w_proj_attn[0], w_proj_pool[0], w_pool[0]).astype(BF16)
    h, gw, xt = _ag_proj(_arrival_order(*coords), x2, w_in[0].astype(BF16), spack)
    wout = gw[:, R_OUT:R_PA].reshape(D, D)
    wpa = gw[:, R_PA:R_PP].reshape(N_DEV, AW, 256).transpose(1, 0, 2).reshape(AW, D)
    wpp = gw[:, R_PP:R_PL].reshape(N_DEV, AW, 256).transpose(1, 0, 2).reshape(AW, D)
    wpool = gw[:, R_PL:].reshape(N_DEV, 4, 32, PG).transpose(1, 0, 2, 3).reshape(4, PG, PG)

    o, lse = zip(*[_attn_fwd(g, h) for g in range(len(DILATIONS))])
    merged, a, b, p, yat, ypt, mgt = _mix_fwd(h, o, lse, wpa, wpp, wpool, pool_scale, b_gate)
    dr, drb, dh, da, db, loss_part, dgamma, dbeta, dbgate = _out_ln(
        merged, x2, tgt, wout, ln_gamma, ln_beta, a, b, h, b_gate)

    dh, do0, do1, do2, dl0, dl1, dl2, dwpool, dpscale = _mix_bwd(
        dh, da, db, h, o, lse, p, wpa, wpp, wpool, pool_scale)
    for g, (do_g, dl_g) in enumerate(zip((do0, do1, do2), (dl0, dl1, dl2))):
        dh = _attn_bwd(g, dh, h, do_g, lse[g], dl_g)

    q = _grad_w_in_rs(_rs_columns(*coords), xt, dh)
    d_wout = _grad_w("grad_w_out", mgt, drb)
    d_wpa = _grad_w("grad_w_pa", yat, da)
    d_wpp = _grad_w("grad_w_pp", ypt, db)
    small = jnp.concatenate([
        d_wout.reshape(N_DEV, 256, D),
        d_wpa.reshape(AW, N_DEV, 256).transpose(1, 0, 2).reshape(N_DEV, -1, D),
        d_wpp.reshape(AW, N_DEV, 256).transpose(1, 0, 2).reshape(N_DEV, -1, D),
        dwpool.astype(BF16).reshape(4, N_DEV, 32, PG).transpose(1, 0, 2, 3).reshape(N_DEV, -1, D)], axis=1)
    q = _pair_sum_small(coords, small, _rs_sibling(small), q)

    vec = _pack_vec(dbgate[0], dgamma[0], dbeta[0], dpscale[0], loss_part[0, 0])
    grad_x, l2, vecs_all = _grad_x_rs(dh, gw, dr, q, vec)
    g_in, d_in, m_in, v_in = _adam_shard("adam_w_in", q, l2, w_in[0], m_w_in[0], v_w_in[0], 256)
    g_small = [t.reshape(w.shape) for t, w in zip(_unpack_small(_sum_small(q, l2)),
                                                  (w_out, w_proj_attn, w_proj_pool, w_pool))]
    small = (g_small,) + _adam_whole("adam_small", g_small, (w_out, w_proj_attn, w_proj_pool, w_pool),
                                     (m_w_out, m_w_proj_attn, m_w_proj_pool, m_w_pool),
                                     (v_w_out, v_w_proj_attn, v_w_proj_pool, v_w_pool))

    *g_vec, loss = _sum_replicated(vecs_all)
    vecs = (g_vec,) + _adam_whole("adam_replicated", g_vec, (b_gate, pool_scale, ln_gamma, ln_beta),
                                  (m_b_gate, m_pool_scale, m_ln_gamma, m_ln_beta),
                                  (v_b_gate, v_pool_scale, v_ln_gamma, v_ln_beta))
    loss = loss[0, 0]

    def leaves(kind, big):
        out, pa, pp, pool = small[kind]
        bg, ps, gm, bt = vecs[kind]
        return [big[None], bg, pool, ps, pa, pp, out, gm, bt]

    return (loss, _permute_tokens(grad_x)[None], *leaves(0, g_in), *leaves(1, d_in), *leaves(2, m_in), *leaves(3, v_in))
```

```python
import functools

import jax
import jax.numpy as jnp
from jax import lax
from jax.experimental import pallas as pl
from jax.experimental.pallas import tpu as pltpu

F32 = jnp.float32
BF16 = jnp.bfloat16

S = 4096
D = 2048
NW = 16384
AW = 1024
HD = 128
NH = 8
QB = 128
DILATIONS = (1, 4, 16)
POOL_WINDOWS = (2, 4, 8, 16)
PG = 256
N_DEV = 8
COL_Q, COL_K, COL_V = 0, 3 * AW, 6 * AW
COL_ZA, COL_U, COL_ZP, COL_G = 9 * AW, 10 * AW, 11 * AW, 12 * AW
DH_Z, DH_G = COL_ZA, COL_G
ALPHA = 2.0 ** 0.25
LN_EPS = 1e-5
NEG_INF = -1e30
LR, B1, B2, EPS, WD, STEP = 0.001, 0.9, 0.999, 1e-08, 0.01, 10
R_IN, R_OUT, R_PA, R_PP, R_PL = 0, 2048, 2304, 2432, 2560
R_ALL = 2576
R_SMALL = R_ALL - R_OUT
VMEM_LIMIT = 56 * 1024 * 1024
MESH = pl.DeviceIdType.MESH
ANY = pl.BlockSpec(memory_space=pl.ANY)


def _cparams(n_axes):
    return pltpu.CompilerParams(dimension_semantics=("arbitrary",) * n_axes, vmem_limit_bytes=VMEM_LIMIT)


def _sigmoid(z):
    return 0.5 * jnp.tanh(0.5 * z) + 0.5


def _nt(a, b):
    return lax.dot_general(a, b, (((1,), (1,)), ((), ())), preferred_element_type=F32)


def _tn(a, b):
    return lax.dot_general(a, b, (((0,), (0,)), ((), ())), preferred_element_type=F32)


def _nn(a, b):
    return jnp.dot(a, b, preferred_element_type=F32)


def _lin(x, y, c):
    return 4 * x + 2 * y + c


def _flip(v, f):
    return 1 - v if f else v


CHIP_FLIPS = ((0, 0), (1, 0), (0, 1), (1, 1))


AG_PIECES = ((pl.ds(R_IN, D), pl.ds(0, 1024)), (pl.ds(R_IN, D), pl.ds(1024, 1024)),
             (pl.ds(R_OUT, R_PA - R_OUT), pl.ds(0, D)), (pl.ds(R_PA, R_ALL - R_PA), pl.ds(0, D)))
N_PIECES = len(AG_PIECES)
SIB, TO_X, TO_Y, ON, PASS_X, PASS_Y, PASS_D = range(7)
AG_TILES = ((0, 0), (0, 1), (1, 0), (1, 1), (2, 0), (4, 0), (3, 0), (5, 0),
            (2, 1), (4, 1), (3, 1), (5, 1), (6, 0), (6, 1), (7, 0), (7, 1))
W, G = "wait", "go"
AG_STEPS = {
    2: [(W, SIB, 0)], 3: [(W, SIB, 1)],
    4: [(W, TO_X, 0), (G, ON, 0), (G, PASS_X, 0)], 5: [(W, TO_Y, 0), (G, PASS_Y, 0)],
    6: [(W, PASS_X, 0)], 7: [(W, PASS_Y, 0)],
    8: [(W, TO_X, 1), (G, PASS_X, 1), (W, TO_Y, 1), (G, ON, 1), (G, PASS_Y, 1),
        (G, TO_X, 2), (G, TO_X, 3), (G, TO_Y, 2), (G, TO_Y, 3)],
    10: [(W, PASS_X, 1)], 11: [(W, PASS_Y, 1)],
    12: [(W, ON, 0), (G, PASS_D, 0)], 13: [(W, ON, 1), (G, PASS_D, 1)],
    14: [(W, PASS_D, 0), (W, TO_X, 2), (G, ON, 2), (G, PASS_X, 2), (W, TO_X, 3), (G, PASS_X, 3),
         (W, TO_Y, 2), (G, PASS_Y, 2), (W, TO_Y, 3), (G, ON, 3), (G, PASS_Y, 3)],
    15: [(W, PASS_D, 1)],
}
AG_LAST = [(W, SIB, 2), (W, SIB, 3), (W, ON, 2), (G, PASS_D, 2), (W, ON, 3), (G, PASS_D, 3),
           (W, PASS_X, 2), (W, PASS_X, 3), (W, PASS_Y, 2), (W, PASS_Y, 3), (W, PASS_D, 2), (W, PASS_D, 3)]


def _arrival_order(x, y, c):
    chips = [(x, y), (1 - x, y), (x, 1 - y), (1 - x, 1 - y)]
    return jnp.stack([_lin(px, py, pc) for px, py in chips for pc in (c, 1 - c)]).astype(jnp.int32)


def _ag_proj(order, x2, wbf, spack):
    tm, tn = 1024, 1024
    nrow, ntile = S // tm, len(AG_TILES)
    slabs = jnp.stack([order[pos] for pos, _ in AG_TILES])
    cols = jnp.stack([2 * order[pos] + half for pos, half in AG_TILES])

    def body(cols_ref, slabs_ref, x_ref, w_ref, s_ref, h_ref, gw_ref, xt_ref, xbv, wbuf, wsem, send_sems, recv_sems,
             local_sems):
        t, i = pl.program_id(0), pl.program_id(1)
        x, y, c = lax.axis_index("x"), lax.axis_index("y"), lax.axis_index("c")
        me = _lin(x, y, c)
        dev = {"sib": (x, y, 1 - c), "x": (1 - x, y, c), "y": (x, 1 - y, c), "d": (1 - x, 1 - y, c)}

        def slab_of(name, other_core=False):
            px, py, pc = dev[name]
            return _lin(px, py, 1 - pc if other_core else pc)

        def own(piece):
            rows, colz = AG_PIECES[piece]
            return w_ref.at[:, colz] if piece < 2 else s_ref.at[pl.ds(rows.start - R_OUT, rows.size)]

        def rdma(slab, kind, piece, to, from_own=False):
            k = kind * N_PIECES + piece
            there = gw_ref.at[(slab, *AG_PIECES[piece])]
            return pltpu.make_async_remote_copy(
                src_ref=own(piece) if from_own else there, dst_ref=there,
                send_sem=send_sems.at[k], recv_sem=recv_sems.at[k], device_id=dev[to], device_id_type=MESH)

        def mine(kind, piece):
            if kind in (SIB, TO_X, TO_Y):
                return rdma(me, kind, piece, ("sib", "x", "y")[kind], from_own=True)
            if kind == ON:
                frm, to = ("x", "y") if piece % 2 == 0 else ("y", "x")
                return rdma(slab_of(frm), kind, piece, to)
            return rdma(slab_of({PASS_X: "x", PASS_Y: "y", PASS_D: "d"}[kind]), kind, piece, "sib")

        def landing(kind, piece):
            slab = {SIB: slab_of("sib"), TO_X: slab_of("x"), TO_Y: slab_of("y"), ON: slab_of("d"),
                    PASS_X: slab_of("x", True), PASS_Y: slab_of("y", True), PASS_D: slab_of("d", True)}[kind]
            return rdma(slab, kind, piece, "sib")

        def run(steps):
            for what, kind, piece in steps:
                if what == W:
                    landing(kind, piece).wait_recv()
                else:
                    mine(kind, piece).start()

        local = [pltpu.make_async_copy(w_ref, gw_ref.at[me, pl.ds(R_IN, D)], local_sems.at[0]),
                 pltpu.make_async_copy(s_ref, gw_ref.at[me, pl.ds(R_OUT, R_SMALL)], local_sems.at[1])]

        def fetch(slab, half, slot):
            src = own(half) if slab is None else gw_ref.at[(slab, *AG_PIECES[half])]
            return pltpu.make_async_copy(src, wbuf.at[slot], wsem.at[slot])

        @pl.when((t == 0) & (i == 0))
        def _():
            for cp in local:
                cp.start()
            run([(G, kind, piece) for piece in (0, 1) for kind in (TO_X, TO_Y, SIB)] + [(G, SIB, 2), (G, SIB, 3)])
            first = fetch(None, 0, 0)
            first.start()
            first.wait()

        for nxt in range(1, ntile):
            @pl.when((t == nxt - 1) & (i == nrow - 1))
            def _(nxt=nxt):
                run(AG_STEPS.get(nxt, []))
                fetch(None if AG_TILES[nxt][0] == 0 else slabs_ref[nxt], AG_TILES[nxt][1], nxt % 2).start()

        @pl.when(t == 0)
        def _():
            xbv[i] = x_ref[...].astype(BF16)
            xt_ref[...] = x_ref[...].T.astype(BF16)

        for slot in (0, 1):
            @pl.when(t % 2 == slot)
            def _(slot=slot):
                @pl.when((i == 0) & (t > 0))
                def _():
                    fetch(None, 0, slot).wait()
                h_ref[...] = _nn(xbv[i], wbuf[slot]).astype(h_ref.dtype)

        @pl.when((t == ntile - 1) & (i == nrow - 1))
        def _():
            run(AG_LAST)
            for kind in range(7):
                for piece in range(N_PIECES):
                    mine(kind, piece).wait_send()
            for cp in local:
                cp.wait()

    n_sem = 7 * N_PIECES
    grid_spec = pltpu.PrefetchScalarGridSpec(
        num_scalar_prefetch=2, grid=(ntile, nrow),
        in_specs=[pl.BlockSpec((tm, D), lambda t, i, cols, slabs: (jnp.where(t == 0, i, nrow - 1), 0)), ANY, ANY],
        out_specs=[pl.BlockSpec((tm, tn), lambda t, i, cols, slabs: (i, cols[t])), ANY,
                   pl.BlockSpec((D, tm), lambda t, i, cols, slabs: (0, jnp.where(t == 0, i, nrow - 1)))],
        scratch_shapes=[pltpu.VMEM((nrow, tm, D), BF16), pltpu.VMEM((2, D, tn), BF16), pltpu.SemaphoreType.DMA((2,)),
                        pltpu.SemaphoreType.DMA((n_sem,)), pltpu.SemaphoreType.DMA((n_sem,)),
                        pltpu.SemaphoreType.DMA((2,))])
    return pl.pallas_call(
        body, name="ag_proj", grid_spec=grid_spec,
        out_shape=[jax.ShapeDtypeStruct((S, NW), BF16), jax.ShapeDtypeStruct((N_DEV, R_ALL, D), BF16),
                   jax.ShapeDtypeStruct((D, S), BF16)],
        compiler_params=_cparams(2))(cols, slabs, x2, wbf, spack)


def _rs_sibling(p):
    n = len(CHIP_FLIPS)

    def body(p_ref, l_ref, send_sems, recv_sems):
        x, y, c = lax.axis_index("x"), lax.axis_index("y"), lax.axis_index("c")
        copies = [pltpu.make_async_remote_copy(
            src_ref=p_ref.at[_lin(_flip(x, fx), _flip(y, fy), 1 - c)], dst_ref=l_ref.at[k],
            send_sem=send_sems.at[k], recv_sem=recv_sems.at[k], device_id=(x, y, 1 - c), device_id_type=MESH)
            for k, (fx, fy) in enumerate(CHIP_FLIPS)]
        for cp in copies:
            cp.start()
        for cp in copies:
            cp.wait_recv()
        for cp in copies:
            cp.wait_send()

    return pl.pallas_call(
        body, name="rs_sibling", out_shape=jax.ShapeDtypeStruct((n,) + p.shape[1:], p.dtype),
        in_specs=[ANY], out_specs=ANY,
        scratch_shapes=[pltpu.SemaphoreType.DMA((n,)), pltpu.SemaphoreType.DMA((n,))])(p)


def _pair_sum_small(coords, p, l1, q):
    def body(crd, p_ref, l_ref, _, q_ref, buf, sem):
        k = pl.program_id(0)
        buf[...] = (p_ref[...].astype(F32) + l_ref[...].astype(F32)).astype(buf.dtype)
        out = pltpu.make_async_copy(buf, q_ref.at[k, pl.ds(R_OUT, R_SMALL)], sem)
        out.start()
        out.wait()

    def p_map(k, crd):
        fx, fy = k % 2, k // 2
        px = crd[0] + fx - 2 * fx * crd[0]
        py = crd[1] + fy - 2 * fy * crd[1]
        return (_lin(px, py, crd[2]), 0, 0)

    grid_spec = pltpu.PrefetchScalarGridSpec(
        num_scalar_prefetch=1, grid=(4,),
        in_specs=[pl.BlockSpec((None, R_SMALL, D), p_map),
                  pl.BlockSpec((None, R_SMALL, D), lambda k, crd: (k, 0, 0)), ANY],
        out_specs=ANY,
        scratch_shapes=[pltpu.VMEM((R_SMALL, D), BF16), pltpu.SemaphoreType.DMA(())])
    return pl.pallas_call(body, name="pair_sum_small", grid_spec=grid_spec,
                          out_shape=jax.ShapeDtypeStruct(q.shape, q.dtype), input_output_aliases={3: 0},
                          compiler_params=_cparams(1))(coords, p, l1, q)


def _h_block(k):
    return jnp.where(k < 9, (k % 3) * 3 + k // 3, k)


RS_PIECES = (pl.ds(0, 1280), pl.ds(1280, R_ALL - 1280))
RS_ROWS = (1280, R_ALL - 1280)
RS_CHUNKS = ((320,) * 4, (432,) * 3)
RS_MERGE_STEP = 2


def _grad_x_rs(dh, g, dr, q, vec):
    others = [(fx, fy, fc) for fx in (0, 1) for fy in (0, 1) for fc in (0, 1) if (fx, fy, fc) != (0, 0, 0)]
    tm, tk = 1024, 1024
    ni, nk = S // tm, NW // tk
    rmax = max(RS_ROWS)
    cmax = max(max(c) for c in RS_CHUNKS)

    def body(dh_ref, w_ref, dr_ref, q_ref, vec_ref, o_ref, l2_ref, ld_ref, mg_ref, all_ref, va, vb,
             send_sems, recv_sems, sems):
        i, k = pl.program_id(0), pl.program_id(1)
        x, y, c = lax.axis_index("x"), lax.axis_index("y"), lax.axis_index("c")
        nbr = ((1 - x, y, c), (x, 1 - y, c))

        def vec_copy(n, sender):
            sx, sy, sc = sender
            fx, fy, fc = others[n]
            return pltpu.make_async_remote_copy(
                src_ref=vec_ref, dst_ref=all_ref.at[_lin(sx, sy, sc)], send_sem=send_sems.at[6 + n],
                recv_sem=recv_sems.at[6 + n], device_id=(_flip(sx, fx), _flip(sy, fy), _flip(sc, fc)),
                device_id_type=MESH)

        vec_own = pltpu.make_async_copy(vec_ref, all_ref.at[_lin(x, y, c)], sems.at[3])

        def rows(ref, piece):
            return ref.at[piece, pl.ds(0, RS_ROWS[piece])]

        copies = (
            (q_ref.at[3, RS_PIECES[0]], rows(ld_ref, 0), 0),
            (q_ref.at[3, RS_PIECES[1]], rows(ld_ref, 1), 1),
            (q_ref.at[1, RS_PIECES[0]], l2_ref.at[0, RS_PIECES[0]], 0),
            (q_ref.at[2, RS_PIECES[1]], l2_ref.at[1, RS_PIECES[1]], 1),
            (rows(mg_ref, 0), l2_ref.at[1, RS_PIECES[0]], 1),
            (rows(mg_ref, 1), l2_ref.at[0, RS_PIECES[1]], 0),
        )

        def copy(n):
            src, dst, axis = copies[n]
            return pltpu.make_async_remote_copy(src_ref=src, dst_ref=dst, send_sem=send_sems.at[n],
                                                recv_sem=recv_sems.at[n], device_id=nbr[axis], device_id_type=MESH)

        def merge(piece, mine):
            start = 0
            for n_rows in RS_CHUNKS[piece]:
                own = pltpu.make_async_copy(q_ref.at[mine, pl.ds(RS_PIECES[piece].start + start, n_rows)],
                                            va.at[pl.ds(0, n_rows)], sems.at[0])
                got = pltpu.make_async_copy(ld_ref.at[piece, pl.ds(start, n_rows)], vb.at[pl.ds(0, n_rows)], sems.at[1])
                own.start()
                got.start()
                own.wait()
                got.wait()
                va[pl.ds(0, n_rows)] = (va[pl.ds(0, n_rows)].astype(F32)
                                        + vb[pl.ds(0, n_rows)].astype(F32)).astype(va.dtype)
                out = pltpu.make_async_copy(va.at[pl.ds(0, n_rows)], mg_ref.at[piece, pl.ds(start, n_rows)], sems.at[2])
                out.start()
                out.wait()
                start += n_rows

        @pl.when((i == 0) & (k == 0))
        def _():
            for n in range(4):
                copy(n).start()
            vec_own.start()
            for n in range(len(others)):
                vec_copy(n, (x, y, c)).start()

        @pl.when((i == RS_MERGE_STEP) & (k == 0))
        def _():
            copy(0).wait_recv()
            merge(0, 2)
            copy(4).start()
            copy(1).wait_recv()
            merge(1, 1)
            copy(5).start()

        @pl.when(k == 0)
        def _():
            o_ref[...] = ALPHA * dr_ref[...]

        o_ref[...] += _nt(dh_ref[...], w_ref[...])

        @pl.when((i == ni - 1) & (k == nk - 1))
        def _():
            for n in range(2, 6):
                copy(n).wait_recv()
            for n in range(6):
                copy(n).wait_send()
            for n, (fx, fy, fc) in enumerate(others):
                vec_copy(n, (_flip(x, fx), _flip(y, fy), _flip(c, fc))).wait_recv()
                vec_copy(n, (x, y, c)).wait_send()
            vec_own.wait()

    slab = q.shape[1:]
    out = pl.pallas_call(
        body, name="grad_x_rs", grid=(ni, nk),
        in_specs=[pl.BlockSpec((tm, tk), lambda i, k: (i, k)),
                  pl.BlockSpec((None, D, tk), lambda i, k: (_h_block(k) // 2, 0, _h_block(k) % 2)),
                  pl.BlockSpec((tm, D), lambda i, k: (i, 0)), ANY, ANY],
        out_specs=[pl.BlockSpec((tm, D), lambda i, k: (i, 0)), ANY, ANY, ANY, ANY],
        out_shape=[jax.ShapeDtypeStruct((S, D), F32), jax.ShapeDtypeStruct((2,) + slab, q.dtype),
                   jax.ShapeDtypeStruct((2, rmax, slab[1]), q.dtype), jax.ShapeDtypeStruct((2, rmax, slab[1]), q.dtype),
                   jax.ShapeDtypeStruct((N_DEV,) + vec.shape, vec.dtype)],
        scratch_shapes=[pltpu.VMEM((cmax, slab[1]), q.dtype), pltpu.VMEM((cmax, slab[1]), q.dtype),
                        pltpu.SemaphoreType.DMA((6 + len(others),)), pltpu.SemaphoreType.DMA((6 + len(others),)),
                        pltpu.SemaphoreType.DMA((4,))],
        compiler_params=_cparams(2))(dh, g, dr, q, vec)
    return out[0], out[1], out[4]


GW_TN = 512
GW_PARTS = D // GW_TN


def _rs_columns(x, y, c):
    per_h = AW // GW_TN
    out = []
    for core in (1 - c, c):
        for fx, fy in CHIP_FLIPS:
            for part in range(GW_PARTS):
                h_block = 2 * _lin(_flip(x, fx), _flip(y, fy), core) + part // per_h
                out.append(_h_block(h_block) * per_h + part % per_h)
    return jnp.stack(out).astype(jnp.int32)


def _grad_w_in_rs(cols, xt, dh):
    n_tile = 4 * GW_PARTS

    def body(cols_ref, a_ref, b_ref, q_ref, l1_ref, stage, landed, send_sems, recv_sems, sem):
        t = pl.program_id(0)
        sib = (lax.axis_index("x"), lax.axis_index("y"), 1 - lax.axis_index("c"))

        def there(n):
            return l1_ref.at[n // GW_PARTS, :, pl.ds((n % GW_PARTS) * GW_TN, GW_TN)]

        def send(n):
            return pltpu.make_async_remote_copy(src_ref=stage.at[n % 2], dst_ref=there(n), send_sem=send_sems.at[n],
                                                recv_sem=recv_sems.at[n], device_id=sib, device_id_type=MESH)

        def fetch(n):
            return pltpu.make_async_copy(there(n), landed, sem)

        for n in range(n_tile):
            @pl.when(t == n_tile + n)
            def _(n=n):
                if n == 0:
                    send(n_tile - 2).wait_send()
                    send(n_tile - 1).wait_send()
                send(n).wait_recv()
                fetch(n).start()

        part = _nn(a_ref[...], b_ref[...])

        for n in range(n_tile):
            @pl.when(t == n)
            def _(n=n):
                if n >= 2:
                    send(n - 2).wait_send()
                stage[n % 2] = part.astype(stage.dtype)
                send(n).start()

            @pl.when(t == n_tile + n)
            def _(n=n):
                fetch(n).wait()
                q_ref[...] = (part + landed[...].astype(F32)).astype(q_ref.dtype)

    mine = lambda t: jnp.maximum(t - n_tile, 0)
    grid_spec = pltpu.PrefetchScalarGridSpec(
        num_scalar_prefetch=1, grid=(2 * n_tile,),
        in_specs=[pl.BlockSpec((D, S), lambda t, cols: (0, 0), pipeline_mode=pl.Buffered(1)),
                  pl.BlockSpec((S, GW_TN), lambda t, cols: (0, cols[t]))],
        out_specs=[pl.BlockSpec((None, D, GW_TN), lambda t, cols: (mine(t) // GW_PARTS, 0, mine(t) % GW_PARTS)), ANY],
        scratch_shapes=[pltpu.VMEM((2, D, GW_TN), BF16), pltpu.VMEM((D, GW_TN), BF16),
                        pltpu.SemaphoreType.DMA((n_tile,)), pltpu.SemaphoreType.DMA((n_tile,)),
                        pltpu.SemaphoreType.DMA(())])
    q, _ = pl.pallas_call(
        body, name="grad_w_in_rs", grid_spec=grid_spec,
        out_shape=[jax.ShapeDtypeStruct((4, R_ALL, D), BF16), jax.ShapeDtypeStruct((4, D, D), BF16)],
        compiler_params=_cparams(1))(cols, xt, dh)
    return q


def _grad_w(name, at, b):
    m, n_all = at.shape[0], b.shape[1]

    def body(a_ref, b_ref, o_ref):
        o_ref[...] = _nn(a_ref[...], b_ref[...]).astype(o_ref.dtype)

    return pl.pallas_call(
        body, name=name, grid=(n_all // GW_TN,),
        in_specs=[pl.BlockSpec((m, S), lambda n: (0, 0), pipeline_mode=pl.Buffered(1)),
                  pl.BlockSpec((S, GW_TN), lambda n: (0, n))],
        out_specs=pl.BlockSpec((m, GW_TN), lambda n: (0, n)), out_shape=jax.ShapeDtypeStruct((m, n_all), BF16),
        compiler_params=_cparams(1))(at, b)


NR = 16
TI = 16
TM = NR * TI
NT = S // TM
ATT_QB = (256, 128, 256)
ATT_NB = (16, 8, 1)
ATT_BLOCKS = (16, 32, 16)


def _permute_tokens(a):
    return a.reshape(NT, TI, NR, a.shape[-1]).transpose(0, 2, 1, 3).reshape(a.shape)


def _attn_shape(g, c):
    if g == 0:
        return (S, c)
    if g == 1:
        return (NT, 4, 4, TI, c)
    return (NT, NR, TI, c)


def _attn_view(g, a):
    return a.reshape(_attn_shape(g, a.shape[-1]))


def _attn_spec(g, width, col, blk):
    if g == 0:
        return pl.BlockSpec((TM, width), lambda b: (blk(b), col))
    if g == 1:
        return pl.BlockSpec((2, 4, None, TI, width), lambda b: (blk(b) % 8, 0, blk(b) // 8, 0, col))
    return pl.BlockSpec((NT, None, TI, width), lambda b: (0, blk(b), 0, col))


def _pieces(g):
    if g == 1:
        return [(t, m) for t in range(2) for m in range(4)]
    return [(t,) for t in range(NT)]


def _get(g, ref, sl):
    if g == 0:
        return ref[:, sl]
    return jnp.concatenate([ref[(*p, slice(None), sl)] for p in _pieces(g)], axis=0)


def _put(g, ref, sl, val):
    if g == 0:
        ref[:, sl] = val
    else:
        for n, p in enumerate(_pieces(g)):
            ref[(*p, slice(None), sl)] = val[TI * n:TI * (n + 1)]


def _block_pos(g, a):
    if g == 0:
        return 16 * (a % 16) + a // 16
    if g == 1:
        return 64 * (a // 64) + 4 * (a % 16) + (a // 16) % 4
    return a


def _attn_mask(g, n):
    qb = ATT_QB[g]
    if ATT_NB[g] == 1:
        qa = lax.broadcasted_iota(jnp.int32, (qb, qb), 0)
        kc = lax.broadcasted_iota(jnp.int32, (qb, qb), 1)
        dist = _block_pos(g, qa) - _block_pos(g, kc)
        return (dist >= 0) & (dist <= QB)
    qa = lax.broadcasted_iota(jnp.int32, (qb, 2 * qb), 0)
    kc = lax.broadcasted_iota(jnp.int32, (qb, 2 * qb), 1)
    cur = kc >= qb
    dist = _block_pos(g, qa) - _block_pos(g, kc % qb) + jnp.where(cur, 0, qb)
    return (dist >= 0) & (dist <= QB) & (cur | (n > 0))


def _keys(g, prev_ref, cur_ref, sl):
    if ATT_NB[g] == 1:
        return _get(g, cur_ref, sl)
    return jnp.concatenate([_get(g, prev_ref, sl), _get(g, cur_ref, sl)], axis=0)


def _qkv_specs(g, clamp):
    cur = lambda col: _attn_spec(g, AW, col, clamp)
    prev = lambda col: _attn_spec(g, AW, col, lambda b: jnp.maximum(clamp(b) - 1, 0))
    qc, kc, vc = (c // AW + g for c in (COL_Q, COL_K, COL_V))
    return [cur(qc), cur(kc), prev(kc), cur(vc), prev(vc)]


def _attn_fwd(g, h):
    scale = HD ** -0.5
    hv = _attn_view(g, h)

    def body(q_ref, kc_ref, kp_ref, vc_ref, vp_ref, o_ref, l_ref):
        valid = _attn_mask(g, pl.program_id(0) % ATT_NB[g])
        for hh in range(NH):
            sl = slice(hh * HD, (hh + 1) * HD)
            kh, vh = _keys(g, kp_ref, kc_ref, sl), _keys(g, vp_ref, vc_ref, sl)
            s = jnp.where(valid, _nt(_get(g, q_ref, sl), kh) * scale, NEG_INF)
            m = jnp.max(s, axis=-1, keepdims=True)
            e = jnp.exp(s - m)
            den = jnp.sum(e, axis=-1, keepdims=True)
            _put(g, o_ref, sl, (_nn(e.astype(BF16), vh) * (1.0 / den)).astype(o_ref.dtype))
            _put(g, l_ref, slice(hh, hh + 1), m + jnp.log(den))

    same = lambda b: b
    o, lse = pl.pallas_call(
        body, name=f"attn_fwd_{g}", grid=(ATT_BLOCKS[g],),
        in_specs=_qkv_specs(g, same),
        out_specs=[_attn_spec(g, AW, 0, same), _attn_spec(g, NH, 0, same)],
        out_shape=[jax.ShapeDtypeStruct(_attn_shape(g, AW), BF16), jax.ShapeDtypeStruct(_attn_shape(g, NH), F32)],
        compiler_params=_cparams(1))(hv, hv, hv, hv, hv)
    return o.reshape(S, AW), lse.reshape(S, NH)


def _attn_bwd(g, dh, h, do, lse, delta):
    scale = HD ** -0.5
    qb = ATT_QB[g]
    carried = ATT_NB[g] > 1
    last = ATT_BLOCKS[g] - 1
    clamp = lambda b: jnp.minimum(b, last)
    behind = lambda b: jnp.maximum(b - 1, 0)
    hv = _attn_view(g, h)

    def body(q_ref, kc_ref, kp_ref, vc_ref, vp_ref, do_ref, l_ref, dl_ref, _, dh_ref, *carry):
        b = pl.program_id(0)

        def write(col, val):
            _put(g, dh_ref, slice(col, col + HD), val.astype(dh_ref.dtype))

        def block():
            valid = _attn_mask(g, b % ATT_NB[g])
            for hh in range(NH):
                sl = slice(hh * HD, (hh + 1) * HD)
                one = slice(hh, hh + 1)
                qh, doh = _get(g, q_ref, sl), _get(g, do_ref, sl)
                kh, vh = _keys(g, kp_ref, kc_ref, sl), _keys(g, vp_ref, vc_ref, sl)
                s = _nt(qh, kh) * scale
                p = jnp.where(valid, jnp.exp(s - _get(g, l_ref, one)), 0.0)
                ds = p * (_nt(doh, vh) - _get(g, dl_ref, one))
                dsb = (ds * scale).astype(BF16)
                dq = _nn(dsb, kh)
                dk2 = _tn(dsb, qh)
                dv2 = _tn(p.astype(BF16), doh)
                if carried:
                    cq_ref, ck_ref, cv_ref = carry
                    write(hh * HD, cq_ref[:, sl])
                    write(AW + hh * HD, ck_ref[:, sl] + dk2[:qb])
                    write(2 * AW + hh * HD, cv_ref[:, sl] + dv2[:qb])
                    cq_ref[:, sl] = dq
                    ck_ref[:, sl] = dk2[qb:]
                    cv_ref[:, sl] = dv2[qb:]
                else:
                    write(hh * HD, dq)
                    write(AW + hh * HD, dk2)
                    write(2 * AW + hh * HD, dv2)

        if not carried:
            block()
            return

        @pl.when(b == 0)
        def _():
            for ref in carry:
                ref[...] = jnp.zeros_like(ref)

        pl.when(b <= last)(block)

        @pl.when(b > last)
        def _():
            for hh in range(NH):
                for n, ref in enumerate(carry):
                    write(n * AW + hh * HD, ref[:, hh * HD:(hh + 1) * HD])

    out = pl.pallas_call(
        body, name=f"attn_bwd_{g}", grid=(ATT_BLOCKS[g] + carried,),
        in_specs=_qkv_specs(g, clamp) + [_attn_spec(g, AW, 0, clamp), _attn_spec(g, NH, 0, clamp),
                                         _attn_spec(g, NH, 0, clamp), ANY],
        out_specs=_attn_spec(g, 3 * AW, g, behind if carried else clamp),
        out_shape=jax.ShapeDtypeStruct(_attn_shape(g, NW), BF16),
        input_output_aliases={8: 0},
        scratch_shapes=[pltpu.VMEM((qb, AW), F32)] * (3 if carried else 0),
        compiler_params=_cparams(1))(hv, hv, hv, hv, hv, _attn_view(g, do), _attn_view(g, lse), _attn_view(g, delta),
                                     _attn_view(g, dh))
    return out.reshape(S, NW)


def _group_weights(l0, l1, l2):
    m = jnp.maximum(jnp.maximum(l0, l1), l2)
    e0, e1, e2 = jnp.exp(l0 - m), jnp.exp(l1 - m), jnp.exp(l2 - m)
    inv = 1.0 / (e0 + e1 + e2)
    return e0 * inv, e1 * inv, e2 * inv


def _residue(ref, r, sl):
    return ref[r * TI:(r + 1) * TI, sl].astype(F32)


def _total(parts):
    return functools.reduce(lambda x, y: x + y, parts)


def _pool_tokens(up_ref, uc_ref, p_ref, tile):
    j0 = lax.broadcasted_iota(jnp.int32, (TI, 1), 0) == 0
    first = (tile == 0) & j0
    for r in range(NR):
        out = []
        for g, w in enumerate(POOL_WINDOWS):
            sl = slice(g * PG, (g + 1) * PG)
            own = _residue(uc_ref, r, sl)
            acc = _total([own] + [_residue(uc_ref, r - k, sl) for k in range(1, min(r, w - 1) + 1)])
            wrapped = [NR + r - k for k in range(r + 1, w)]
            if wrapped:
                wc = _total([_residue(uc_ref, q, sl) for q in wrapped])
                wp = jnp.where(tile > 0, _total([_residue(up_ref, q, sl) for q in wrapped]), 0.0)
                acc = acc + jnp.where(j0, pltpu.roll(wp, 1, 0), pltpu.roll(wc, 1, 0))
            out.append(acc * jnp.where(first, 1.0 / min(r + 1, w), 1.0 / w) - own)
        p_ref[r * TI:(r + 1) * TI, :] = jnp.concatenate(out, axis=1).astype(p_ref.dtype)


def _pool_tokens_bwd(dp, nxt_ref, du_ref, tile):
    ji = lax.broadcasted_iota(jnp.int32, (TI, 1), 0)
    first = (tile == 0) & (ji == 0)
    piece = lambda g, r: dp[g][r * TI:(r + 1) * TI]
    dpc = [[piece(g, r) * jnp.where(first, 1.0 / min(r + 1, w), 1.0 / w) for r in range(NR)]
           for g, w in enumerate(POOL_WINDOWS)]
    for r in range(NR):
        out = []
        for g, w in enumerate(POOL_WINDOWS):
            sl = slice(g * PG, (g + 1) * PG)
            acc = _total([dpc[g][r + k] for k in range(w) if r + k < NR])
            wrapped = [r + k - NR for k in range(1, w) if r + k >= NR]
            if wrapped:
                wc = _total([dpc[g][q] for q in wrapped])
                wn = _total([nxt_ref[q * TI:(q + 1) * TI, sl] for q in wrapped])
                acc = acc + jnp.where(ji == TI - 1, pltpu.roll(wn, TI - 1, 0), pltpu.roll(wc, TI - 1, 0))
            out.append(acc - piece(g, r))
        du_ref[r * TI:(r + 1) * TI, :] = jnp.concatenate(out, axis=1).astype(du_ref.dtype)
    for r in range(NR):
        nxt_ref[r * TI:(r + 1) * TI, :] = jnp.concatenate([dpc[g][r] for g in range(len(POOL_WINDOWS))], axis=1)


def _pool_linear(pb, wpool_ref):
    return jnp.concatenate([_nn(pb[:, g * PG:(g + 1) * PG], wpool_ref[g]) for g in range(len(POOL_WINDOWS))], axis=1)


def _tok(width, col=0, rev=False):
    if rev:
        return pl.BlockSpec((TM, width), lambda i: (NT - 1 - i, col))
    return pl.BlockSpec((TM, width), lambda i: (i, col))


def _whole(shape):
    return pl.BlockSpec(shape, lambda i: (0,) * len(shape))


def _mix_fwd(h, o, lse, wpa, wpp, wpool, pscale, bgate):
    def body(o0_ref, o1_ref, o2_ref, l0_ref, l1_ref, l2_ref, za_ref, uc_ref, up_ref, zp_ref, gp_ref,
             wpa_ref, wpp_ref, wpool_ref, ps_ref, bg_ref,
             mg_ref, a_ref, b_ref, p_ref, yat_ref, ypt_ref, mgt_ref, ya_ref, yp_ref):
        i = pl.program_id(0)
        w0, w1, w2 = _group_weights(l0_ref[...], l1_ref[...], l2_ref[...])
        za = za_ref[...].astype(F32)
        silu_a = za * _sigmoid(za)
        for hh in range(NH):
            sl = slice(hh * HD, (hh + 1) * HD)
            c = slice(hh, hh + 1)
            oh = (w0[:, c] * o0_ref[:, sl].astype(F32) + w1[:, c] * o1_ref[:, sl].astype(F32)
                  + w2[:, c] * o2_ref[:, sl].astype(F32))
            ya = oh * silu_a[:, sl]
            ya_ref[:, sl] = ya.astype(BF16)
            yat_ref[sl, :] = ya.T.astype(BF16)
        _pool_tokens(up_ref, uc_ref, p_ref, i)
        zp = zp_ref[...].astype(F32)
        yp = _pool_linear(p_ref[...], wpool_ref) * ps_ref[...] * (zp * _sigmoid(zp))
        yp_ref[...] = yp.astype(BF16)
        ypt_ref[...] = yp.T.astype(BF16)
        a = _nn(ya_ref[...], wpa_ref[...])
        b = _nn(yp_ref[...], wpp_ref[...])
        a_ref[...] = a.astype(BF16)
        b_ref[...] = b.astype(BF16)
        gates = _sigmoid(gp_ref[...].astype(F32) + bg_ref[...])
        mg = gates[:, :D] * a + gates[:, D:] * b
        mg_ref[...] = mg.astype(BF16)
        mgt_ref[...] = mg.T.astype(BF16)

    u_prev = pl.BlockSpec((TM, AW), lambda i: (jnp.maximum(i - 1, 0), COL_U // AW))
    across = lambda width: pl.BlockSpec((width, TM), lambda i: (0, i))
    return pl.pallas_call(
        body, name="mix_fwd", grid=(NT,),
        in_specs=[_tok(AW)] * 3 + [_tok(NH)] * 3
        + [_tok(AW, COL_ZA // AW), _tok(AW, COL_U // AW), u_prev, _tok(AW, COL_ZP // AW), _tok(2 * D, COL_G // (2 * D))]
        + [_whole((AW, D)), _whole((AW, D)), _whole((4, PG, PG)), _whole((1, AW)), _whole((1, 2 * D))],
        out_specs=[_tok(D), _tok(D), _tok(D), _tok(AW), across(AW), across(AW), across(D)],
        out_shape=[jax.ShapeDtypeStruct((S, D), BF16)] * 3 + [jax.ShapeDtypeStruct((S, AW), BF16)]
        + [jax.ShapeDtypeStruct((AW, S), BF16)] * 2 + [jax.ShapeDtypeStruct((D, S), BF16)],
        scratch_shapes=[pltpu.VMEM((TM, AW), BF16), pltpu.VMEM((TM, AW), BF16)],
        compiler_params=_cparams(1))(*o, *lse, h, h, h, h, h, wpa, wpp, wpool, pscale, bgate)


def _out_ln(merged, x, target, wout, gamma, beta, a, b, h, bgate):
    def body(mg_ref, x_ref, t_ref, w_ref, g_ref, b_ref, a_ref, bb_ref, gp_ref, bg_ref,
             dr_ref, drb_ref, dgp_ref, da_ref, db_ref, loss_ref, dg_ref, dbt_ref, dbg_ref):
        i = pl.program_id(0)

        @pl.when(i == 0)
        def _():
            for ref in (loss_ref, dg_ref, dbt_ref, dbg_ref):
                ref[...] = jnp.zeros_like(ref)

        r = ALPHA * x_ref[...] + _nn(mg_ref[...], w_ref[...])
        mu = jnp.mean(r, axis=-1, keepdims=True)
        rc = r - mu
        rstd = lax.rsqrt(jnp.mean(rc * rc, axis=-1, keepdims=True) + LN_EPS)
        xhat = rc * rstd
        err = xhat * g_ref[...] + b_ref[...] - t_ref[...]
        loss_ref[...] += 0.5 * jnp.sum(jnp.mean(err * err, axis=-1, keepdims=True), axis=0, keepdims=True)
        dy = err * (1.0 / D)
        dg_ref[...] += jnp.sum(dy * xhat, axis=0, keepdims=True)
        dbt_ref[...] += jnp.sum(dy, axis=0, keepdims=True)
        dxh = dy * g_ref[...]
        dr = rstd * (dxh - jnp.mean(dxh, axis=-1, keepdims=True)
                     - xhat * jnp.mean(dxh * xhat, axis=-1, keepdims=True))
        dr_ref[...] = dr
        drb_ref[...] = dr.astype(BF16)
        dm_ = _nt(drb_ref[...], w_ref[...]).astype(BF16).astype(F32)
        gates = _sigmoid(gp_ref[...].astype(F32) + bg_ref[...])
        ga, gb = gates[:, :D], gates[:, D:]
        da_ref[...] = (dm_ * ga).astype(BF16)
        db_ref[...] = (dm_ * gb).astype(BF16)
        dgp = jnp.concatenate([dm_ * a_ref[...].astype(F32) * ga * (1.0 - ga),
                               dm_ * bb_ref[...].astype(F32) * gb * (1.0 - gb)], axis=1)
        dgp_ref[...] = dgp.astype(BF16)
        dbg_ref[...] += jnp.sum(dgp, axis=0, keepdims=True)

    w_spec = pl.BlockSpec((D, D), lambda i: (0, 0), pipeline_mode=pl.Buffered(1))
    return pl.pallas_call(
        body, name="out_ln", grid=(NT,),
        in_specs=[_tok(D), _tok(D), _tok(D), w_spec, _whole((1, D)), _whole((1, D)),
                  _tok(D), _tok(D), _tok(2 * D, COL_G // (2 * D)), _whole((1, 2 * D))],
        out_specs=[_tok(D), _tok(D), _tok(2 * D, DH_G // (2 * D)), _tok(D), _tok(D),
                   _whole((8, 128)), _whole((1, D)), _whole((1, D)), _whole((1, 2 * D))],
        out_shape=[jax.ShapeDtypeStruct((S, D), F32), jax.ShapeDtypeStruct((S, D), BF16),
                   jax.ShapeDtypeStruct((S, NW), BF16), jax.ShapeDtypeStruct((S, D), BF16),
                   jax.ShapeDtypeStruct((S, D), BF16), jax.ShapeDtypeStruct((8, 128), F32),
                   jax.ShapeDtypeStruct((1, D), F32), jax.ShapeDtypeStruct((1, D), F32),
                   jax.ShapeDtypeStruct((1, 2 * D), F32)],
        compiler_params=_cparams(1))(merged, x, target, wout, gamma, beta, a, b, h, bgate)


def _mix_bwd(dh, da, db, h, o, lse, p, wpa, wpp, wpool, pscale):
    def body(_, da_ref, db_ref, o0_ref, o1_ref, o2_ref, l0_ref, l1_ref, l2_ref, za_ref, zp_ref, p_ref,
             wpa_ref, wpp_ref, wpool_ref, ps_ref,
             dh_ref, do0_ref, do1_ref, do2_ref, dl0_ref, dl1_ref, dl2_ref, dwp_ref, dps_ref,
             nxt_ref):
        i = pl.program_id(0)
        tile = NT - 1 - i
        dza_ref, du_ref, dzp_ref = (dh_ref.at[:, pl.ds(n * AW, AW)] for n in range(3))

        @pl.when(i == 0)
        def _():
            nxt_ref[...] = jnp.zeros_like(nxt_ref)
            dwp_ref[...] = jnp.zeros_like(dwp_ref)
            dps_ref[...] = jnp.zeros_like(dps_ref)

        dya = _nt(da_ref[...], wpa_ref[...])
        w0, w1, w2 = _group_weights(l0_ref[...], l1_ref[...], l2_ref[...])
        za = za_ref[...].astype(F32)
        sig = _sigmoid(za)
        silu_a = za * sig
        dsilu_a = sig * (1.0 + za * (1.0 - sig))
        for hh in range(NH):
            sl = slice(hh * HD, (hh + 1) * HD)
            c = slice(hh, hh + 1)
            oh = (w0[:, c] * o0_ref[:, sl].astype(F32) + w1[:, c] * o1_ref[:, sl].astype(F32)
                  + w2[:, c] * o2_ref[:, sl].astype(F32))
            doh = dya[:, sl] * silu_a[:, sl]
            dza_ref[:, sl] = (dya[:, sl] * oh * dsilu_a[:, sl]).astype(BF16)
            dot_ = jnp.sum(doh * oh, axis=-1, keepdims=True)
            do0_ref[:, sl] = (w0[:, c] * doh).astype(BF16)
            do1_ref[:, sl] = (w1[:, c] * doh).astype(BF16)
            do2_ref[:, sl] = (w2[:, c] * doh).astype(BF16)
            dl0_ref[:, c] = w0[:, c] * dot_
            dl1_ref[:, c] = w1[:, c] * dot_
            dl2_ref[:, c] = w2[:, c] * dot_
        dyp = _nt(db_ref[...], wpp_ref[...])
        pb = p_ref[...]
        pw = _pool_linear(pb, wpool_ref)
        zp = zp_ref[...].astype(F32)
        sigp = _sigmoid(zp)
        dypre = dyp * (zp * sigp)
        dzp_ref[...] = (dyp * (pw * ps_ref[...]) * (sigp * (1.0 + zp * (1.0 - sigp)))).astype(BF16)
        dps_ref[...] += jnp.sum(dypre * pw, axis=0, keepdims=True)
        dpw = (dypre * ps_ref[...]).astype(BF16)
        dp = []
        for g in range(len(POOL_WINDOWS)):
            sl = slice(g * PG, (g + 1) * PG)
            dwp_ref[g] += _tn(pb[:, sl], dpw[:, sl])
            dp.append(_nt(dpw[:, sl], wpool_ref[g]))
        _pool_tokens_bwd(dp, nxt_ref, du_ref, tile)

    r = functools.partial(_tok, rev=True)
    return pl.pallas_call(
        body, name="mix_bwd", grid=(NT,),
        in_specs=[ANY, r(D), r(D)] + [r(AW)] * 3 + [r(NH)] * 3 + [r(AW, COL_ZA // AW), r(AW, COL_ZP // AW), r(AW)]
        + [_whole((AW, D)), _whole((AW, D)), _whole((4, PG, PG)), _whole((1, AW))],
        out_specs=[r(3 * AW, DH_Z // (3 * AW))] + [r(AW)] * 3 + [r(NH)] * 3 + [_whole((4, PG, PG)), _whole((1, AW))],
        out_shape=[jax.ShapeDtypeStruct((S, NW), BF16)] + [jax.ShapeDtypeStruct((S, AW), BF16)] * 3
        + [jax.ShapeDtypeStruct((S, NH), F32)] * 3
        + [jax.ShapeDtypeStruct((4, PG, PG), F32), jax.ShapeDtypeStruct((1, AW), F32)],
        input_output_aliases={0: 0},
        scratch_shapes=[pltpu.VMEM((TM, AW), F32)],
        compiler_params=_cparams(1))(dh, da, db, *o, *lse, h, h, p, wpa, wpp, wpool, pscale)


def _adamw(w, g, m, v):
    m = B1 * m + (1.0 - B1) * g
    v = B2 * v + (1.0 - B2) * jnp.square(g)
    m_hat = m / (1.0 - B1 ** STEP)
    v_hat = v / (1.0 - B2 ** STEP)
    return -LR * (m_hat / (jnp.sqrt(v_hat) + EPS) + WD * w), m, v


def _adam_shard(name, q, l2, w, m, v, tr):
    rows = w.shape[0]

    def body(q_ref, l_ref, w_ref, m_ref, v_ref, g_out, d_out, m_out, v_out):
        g = q_ref[...].astype(F32)
        for k in range(2):
            g = g + l_ref[k].astype(F32)
        g_out[...] = g
        d_out[...], m_out[...], v_out[...] = _adamw(w_ref[...], g, m_ref[...], v_ref[...])

    blk = pl.BlockSpec((tr, D), lambda i: (i, 0))
    return pl.pallas_call(
        body, name=name, grid=(rows // tr,),
        in_specs=[pl.BlockSpec((None, tr, D), lambda i: (0, i, 0)), pl.BlockSpec((2, tr, D), lambda i: (0, i, 0)),
                  blk, blk, blk],
        out_specs=[blk] * 4, out_shape=[jax.ShapeDtypeStruct((rows, D), F32)] * 4,
        compiler_params=_cparams(1))(q, l2, w, m, v)


def _sum_small(q, l2):
    def body(q_ref, l_ref, g_out, buf, sems):
        rows = pl.ds(R_OUT, R_SMALL)
        copies = [pltpu.make_async_copy(src, buf.at[n], sems.at[n])
                  for n, src in enumerate((q_ref.at[0, rows], l_ref.at[0, rows], l_ref.at[1, rows]))]
        for cp in copies:
            cp.start()
        for cp in copies:
            cp.wait()
        g_out[...] = buf[0].astype(F32) + buf[1].astype(F32) + buf[2].astype(F32)

    return pl.pallas_call(
        body, name="sum_small", in_specs=[ANY, ANY], out_shape=jax.ShapeDtypeStruct((R_SMALL, D), F32),
        scratch_shapes=[pltpu.VMEM((3, R_SMALL, D), q.dtype), pltpu.SemaphoreType.DMA((3,))],
        compiler_params=pltpu.CompilerParams(vmem_limit_bytes=VMEM_LIMIT))(q, l2)


def _adam_whole(name, grads, weights, ms, vs):
    n = len(grads)

    def body(*refs):
        ins, outs = refs[:4 * n], refs[4 * n:]
        for t in range(n):
            g, w, m, v = (ins[k * n + t][...] for k in range(4))
            outs[t][...], outs[n + t][...], outs[2 * n + t][...] = _adamw(w, g, m, v)

    out = pl.pallas_call(
        body, name=name, out_shape=[jax.ShapeDtypeStruct(w.shape, F32) for w in weights] * 3,
        compiler_params=pltpu.CompilerParams(vmem_limit_bytes=VMEM_LIMIT))(*grads, *weights, *ms, *vs)
    return out[:n], out[n:2 * n], out[2 * n:]


def _adam_replicated(gathered, weights, ms, vs):
    n = len(weights)

    def body(g_ref, *refs):
        ins, outs = refs[:3 * n], refs[3 * n:]
        g = g_ref[0]
        for k in range(1, N_DEV):
            g = g + g_ref[k]
        grads = (jnp.concatenate([g[0:1], g[1:2]], axis=1), g[4:5, :AW], g[2:3], g[3:4])
        for t in range(n):
            w, m, v = (ins[k * n + t][...] for k in range(3))
            outs[t][...] = grads[t]
            outs[n + t][...], outs[2 * n + t][...], outs[3 * n + t][...] = _adamw(w, grads[t], m, v)
        outs[4 * n][...] = jnp.broadcast_to(g[5:6, :128], outs[4 * n].shape)

    out = pl.pallas_call(
        body, name="adam_replicated",
        out_shape=[jax.ShapeDtypeStruct(w.shape, F32) for w in weights] * 4 + [jax.ShapeDtypeStruct((8, 128), F32)],
        compiler_params=pltpu.CompilerParams(vmem_limit_bytes=VMEM_LIMIT))(gathered, *weights, *ms, *vs)
    return [out[k * n:(k + 1) * n] for k in range(4)], out[4 * n]


def _pack_small(w_out, w_pa, w_pp, w_pool):
    return jnp.concatenate([w_out, w_pa.reshape(-1, D), w_pp.reshape(-1, D), w_pool.reshape(-1, D)], axis=0)


def _unpack_small(a):
    o = R_OUT
    return (a[:R_PA - o], a[R_PA - o:R_PP - o].reshape(AW, 256), a[R_PP - o:R_PL - o].reshape(AW, 256),
            a[R_PL - o:].reshape(4, 32, PG))


def _pack_vec(b_gate, gamma, beta, pscale, extra):
    z = jnp.zeros((D,), F32)
    return jnp.stack([b_gate[:D], b_gate[D:], gamma, beta, jnp.concatenate([pscale, z[:D - AW]]),
                      jnp.broadcast_to(extra, (D,)), z, z])


def kernel(x, w_in, b_gate, w_pool, pool_scale, w_proj_attn, w_proj_pool, w_out, ln_gamma, ln_beta, loss_target, m_w_in, m_b_gate, m_w_pool, m_pool_scale, m_w_proj_attn, m_w_proj_pool, m_w_out, m_ln_gamma, m_ln_beta, v_w_in, v_b_gate, v_w_pool, v_pool_scale, v_w_proj_attn, v_w_proj_pool, v_w_out, v_ln_gamma, v_ln_beta):
    coords = jnp.stack([lax.axis_index("x"), lax.axis_index("y"), lax.axis_index("c")]).astype(jnp.int32)
    x2, tgt = _permute_tokens(x[0]), _permute_tokens(loss_target[0])

    spack = _pack_small(w_out[0], w_proj_attn[0], w_proj_pool[0], w_pool[0]).astype(BF16)
    h, gw, xt = _ag_proj(_arrival_order(*coords), x2, w_in[0].astype(BF16), spack)
    wout = gw[:, R_OUT:R_PA].reshape(D, D)
    wpa = gw[:, R_PA:R_PP].reshape(N_DEV, AW, 256).transpose(1, 0, 2).reshape(AW, D)
    wpp = gw[:, R_PP:R_PL].reshape(N_DEV, AW, 256).transpose(1, 0, 2).reshape(AW, D)
    wpool = gw[:, R_PL:].reshape(N_DEV, 4, 32, PG).transpose(1, 0, 2, 3).reshape(4, PG, PG)

    o, lse = zip(*[_attn_fwd(g, h) for g in range(len(DILATIONS))])
    merged, a, b, p, yat, ypt, mgt = _mix_fwd(h, o, lse, wpa, wpp, wpool, pool_scale, b_gate)
    dr, drb, dh, da, db, loss_part, dgamma, dbeta, dbgate = _out_ln(
        merged, x2, tgt, wout, ln_gamma, ln_beta, a, b, h, b_gate)

    dh, do0, do1, do2, dl0, dl1, dl2, dwpool, dpscale = _mix_bwd(
        dh, da, db, h, o, lse, p, wpa, wpp, wpool, pool_scale)
    for g, (do_g, dl_g) in enumerate(zip((do0, do1, do2), (dl0, dl1, dl2))):
        dh = _attn_bwd(g, dh, h, do_g, lse[g], dl_g)

    q = _grad_w_in_rs(_rs_columns(*coords), xt, dh)
    d_wout = _grad_w("grad_w_out", mgt, drb)
    d_wpa = _grad_w("grad_w_pa", yat, da)
    d_wpp = _grad_w("grad_w_pp", ypt, db)
    small = jnp.concatenate([
        d_wout.reshape(N_DEV, 256, D),
        d_wpa.reshape(AW, N_DEV, 256).transpose(1, 0, 2).reshape(N_DEV, -1, D),
        d_wpp.reshape(AW, N_DEV, 256).transpose(1, 0, 2).reshape(N_DEV, -1, D),
        dwpool.astype(BF16).reshape(4, N_DEV, 32, PG).transpose(1, 0, 2, 3).reshape(N_DEV, -1, D)], axis=1)
    q = _pair_sum_small(coords, small, _rs_sibling(small), q)

    vec = _pack_vec(dbgate[0], dgamma[0], dbeta[0], dpscale[0], loss_part[0, 0])
    grad_x, l2, vecs_all = _grad_x_rs(dh, gw, dr, q, vec)
    g_in, d_in, m_in, v_in = _adam_shard("adam_w_in", q, l2, w_in[0], m_w_in[0], v_w_in[0], 256)
    g_small = [t.reshape(w.shape) for t, w in zip(_unpack_small(_sum_small(q, l2)),
                                                  (w_out, w_proj_attn, w_proj_pool, w_pool))]
    small = (g_small,) + _adam_whole("adam_small", g_small, (w_out, w_proj_attn, w_proj_pool, w_pool),
                                     (m_w_out, m_w_proj_attn, m_w_proj_pool, m_w_pool),
                                     (v_w_out, v_w_proj_attn, v_w_proj_pool, v_w_pool))

    vecs, loss = _adam_replicated(vecs_all, (b_gate, pool_scale, ln_gamma, ln_beta),
                                  (m_b_gate, m_pool_scale, m_ln_gamma, m_ln_beta),
                                  (v_b_gate, v_pool_scale, v_ln_gamma, v_ln_beta))
    loss = loss[0, 0]

    def leaves(kind, big):
        out, pa, pp, pool = small[kind]
        bg, ps, gm, bt = vecs[kind]
        return [big[None], bg, pool, ps, pa, pp, out, gm, bt]

    return (loss, _permute_tokens(grad_x)[None], *leaves(0, g_in), *leaves(1, d_in), *leaves(2, m_in), *leaves(3, v_in))
```
